```python
import jax, jax.numpy as jnp
from jax import lax
import numpy as np

D_MODEL = 1024
BATCH = 32
SEQ = 2048
DEPTH = 2

HEAD_DIM = 64
N_Q_HEADS = 16
N_KV_HEADS = 4
GQA_GROUP = N_Q_HEADS // N_KV_HEADS
ATTN_WIDTH = N_Q_HEADS * HEAD_DIM
KV_WIDTH = N_KV_HEADS * HEAD_DIM
WINDOW = 128
BLOCK = WINDOW
CONV_WIDTH = D_MODEL
CONV_K = 3
D_FF = 4 * D_MODEL
RMS_EPS = 1e-6
NEG_INF = -1e30
IN_SPLIT_SIZES = (ATTN_WIDTH, KV_WIDTH, KV_WIDTH, CONV_WIDTH, CONV_WIDTH, CONV_WIDTH, D_MODEL, D_MODEL)
IN_COLS = sum(IN_SPLIT_SIZES)

kernel_name = "hybrid_swa_sink_alibi_shortconv_gated_block"


def rmsnorm(x, g):
    xf = x.astype(jnp.float32)
    y = xf * lax.rsqrt(jnp.mean(xf * xf, axis=-1, keepdims=True) + RMS_EPS)
    return (y * g.astype(jnp.float32)).astype(x.dtype)


def alibi_slopes():
    h = np.arange(1, N_Q_HEADS + 1, dtype=np.float32)
    return jnp.asarray(np.power(np.float32(2.0), -8.0 * h / N_Q_HEADS), dtype=jnp.float32)


def sliding_window_sink_attention(q, k, v, sinks):
    B, S = q.shape[0], q.shape[1]
    nblk = S // BLOCK
    qb = q.reshape(B, nblk, BLOCK, N_KV_HEADS, GQA_GROUP, HEAD_DIM)
    kb = k.reshape(B, nblk, BLOCK, N_KV_HEADS, HEAD_DIM)
    vb = v.reshape(B, nblk, BLOCK, N_KV_HEADS, HEAD_DIM)
    pad = ((0, 0), (1, 0), (0, 0), (0, 0), (0, 0))
    k_band = jnp.concatenate([jnp.pad(kb[:, :-1], pad), kb], axis=2)
    v_band = jnp.concatenate([jnp.pad(vb[:, :-1], pad), vb], axis=2)
    qb = jnp.moveaxis(qb, 1, 0)
    k_band = jnp.moveaxis(k_band, 1, 0)
    v_band = jnp.moveaxis(v_band, 1, 0)

    r = jnp.arange(BLOCK)[:, None]
    j = jnp.arange(2 * BLOCK)[None, :]
    dist = BLOCK + r - j
    in_window = (dist >= 0) & (dist < WINDOW)
    slopes = alibi_slopes().reshape(N_KV_HEADS, GQA_GROUP)
    bias = -slopes[:, :, None, None] * dist.astype(jnp.float32)[None, None]
    sink = sinks.astype(jnp.float32).reshape(1, N_KV_HEADS, GQA_GROUP, 1, 1)
    scale = HEAD_DIM ** -0.5

    def one_block(args):
        qi, ki, vi, i = args
        s = jnp.einsum('bqhgd,bkhd->bhgqk', qi, ki, preferred_element_type=jnp.float32) * scale + bias
        valid = in_window & ((i - 1) * BLOCK + j >= 0)
        s = jnp.where(valid, s, NEG_INF)
        m = jnp.maximum(jnp.max(s, axis=-1, keepdims=True), sink)
        p = jnp.exp(s - m)
        denom = jnp.sum(p, axis=-1, keepdims=True) + jnp.exp(sink - m)
        return jnp.einsum('bhgqk,bkhd->bqhgd', (p / denom).astype(vi.dtype), vi)

    out = lax.map(one_block, (qb, k_band, v_band, jnp.arange(nblk)))
    return jnp.moveaxis(out, 0, 1).reshape(B, S, ATTN_WIDTH)


def gated_short_conv(b_gate, c_gate, u, conv_w, conv_b):
    y = c_gate * u
    z = lax.conv_general_dilated(
        y, conv_w[:, None, :], window_strides=(1,), padding=[(CONV_K - 1, 0)],
        dimension_numbers=('NWC', 'WIO', 'NWC'), feature_group_count=CONV_WIDTH)
    return b_gate * (z + conv_b)


def _fwd_setup_inputs(seed: int = 0) -> dict:
    key = jax.random.key(seed)
    ks = jax.random.split(key, 18)
    f32 = jnp.float32
    nrm = lambda k, shape, s: jax.random.normal(k, shape, f32) * s
    return {
        "x": nrm(ks[0], (BATCH, SEQ, D_MODEL), 1.0),
        "g_mix": 1.0 + nrm(ks[1], (DEPTH, D_MODEL), 0.02),
        "w_in": nrm(ks[2], (DEPTH, D_MODEL, IN_COLS), D_MODEL ** -0.5),
        "b_gates": nrm(ks[3], (DEPTH, 2 * D_MODEL), 0.02),
        "sinks": nrm(ks[4], (DEPTH, N_Q_HEADS), 1.0),
        "w_attn_out": nrm(ks[5], (DEPTH, ATTN_WIDTH, D_MODEL), ATTN_WIDTH ** -0.5),
        "conv_w": nrm(ks[6], (DEPTH, CONV_K, CONV_WIDTH), CONV_K ** -0.5),
        "conv_b": nrm(ks[7], (DEPTH, CONV_WIDTH), 0.02),
        "w_conv_out": nrm(ks[8], (DEPTH, CONV_WIDTH, D_MODEL), CONV_WIDTH ** -0.5),
        "w_o": nrm(ks[9], (DEPTH, D_MODEL, D_MODEL), D_MODEL ** -0.5),
        "g_mlp": 1.0 + nrm(ks[10], (DEPTH, D_MODEL), 0.02),
        "w_up": nrm(ks[11], (DEPTH, D_MODEL, D_FF), D_MODEL ** -0.5),
        "w_down": nrm(ks[12], (DEPTH, D_FF, D_MODEL), D_FF ** -0.5),
        "g_final": 1.0 + nrm(ks[13], (D_MODEL,), 0.02),
    }


def _fwd_reference(x, g_mix, w_in, b_gates, sinks, w_attn_out, conv_w, conv_b, w_conv_out, w_o,
              g_mlp, w_up, w_down, g_final):
    B, S, _ = x.shape
    split_idx = list(np.cumsum(IN_SPLIT_SIZES)[:-1])
    for l in range(DEPTH):
        h = rmsnorm(x, g_mix[l])
        proj = jnp.einsum('bsd,dc->bsc', h, w_in[l])
        q, k, v, cb, cc, cu, ga, gc = jnp.split(proj, split_idx, axis=-1)
        q = q.reshape(B, S, N_Q_HEADS, HEAD_DIM)
        k = k.reshape(B, S, N_KV_HEADS, HEAD_DIM)
        v = v.reshape(B, S, N_KV_HEADS, HEAD_DIM)
        y_attn = jnp.einsum('bse,ed->bsd', sliding_window_sink_attention(q, k, v, sinks[l]), w_attn_out[l])
        y_conv = jnp.einsum('bse,ed->bsd', gated_short_conv(cb, cc, cu, conv_w[l], conv_b[l]), w_conv_out[l])
        gate_a = jax.nn.sigmoid(ga + b_gates[l, :D_MODEL])
        gate_c = jax.nn.sigmoid(gc + b_gates[l, D_MODEL:])
        merged = gate_a * y_attn + gate_c * y_conv
        x = x + jnp.einsum('bsd,de->bse', merged, w_o[l])
        h2 = rmsnorm(x, g_mlp[l])
        u = jnp.square(jax.nn.relu(jnp.einsum('bsd,df->bsf', h2, w_up[l])))
        x = x + jnp.einsum('bsf,fd->bsd', u, w_down[l])
    return rmsnorm(x, g_final)


import jax as _jax
import jax.numpy as _jnp

TWIN_FORMAT = 'train_step'
FWD_PARAMS = ['x', 'g_mix', 'w_in', 'b_gates', 'sinks', 'w_attn_out', 'conv_w', 'conv_b', 'w_conv_out', 'w_o', 'g_mlp', 'w_up', 'w_down', 'g_final']
TWIN_WEIGHTS = ['g_mix', 'w_in', 'b_gates', 'sinks', 'w_attn_out', 'conv_w', 'conv_b', 'w_conv_out', 'w_o', 'g_mlp', 'w_up', 'w_down', 'g_final']
TWIN_DIFF_INPUT = 'x'
TWIN_INPUTS = ['x', 'g_mix', 'w_in', 'b_gates', 'sinks', 'w_attn_out', 'conv_w', 'conv_b', 'w_conv_out', 'w_o', 'g_mlp', 'w_up', 'w_down', 'g_final', 'loss_target', 'm_g_mix', 'm_w_in', 'm_b_gates', 'm_sinks', 'm_w_attn_out', 'm_conv_w', 'm_conv_b', 'm_w_conv_out', 'm_w_o', 'm_g_mlp', 'm_w_up', 'm_w_down', 'm_g_final', 'v_g_mix', 'v_w_in', 'v_b_gates', 'v_sinks', 'v_w_attn_out', 'v_conv_w', 'v_conv_b', 'v_w_conv_out', 'v_w_o', 'v_g_mlp', 'v_w_up', 'v_w_down', 'v_g_final']
TWIN_OUTPUTS = ['loss', 'grad_x', 'grad_g_mix', 'grad_w_in', 'grad_b_gates', 'grad_sinks', 'grad_w_attn_out', 'grad_conv_w', 'grad_conv_b', 'grad_w_conv_out', 'grad_w_o', 'grad_g_mlp', 'grad_w_up', 'grad_w_down', 'grad_g_final', 'delta_g_mix', 'delta_w_in', 'delta_b_gates', 'delta_sinks', 'delta_w_attn_out', 'delta_conv_w', 'delta_conv_b', 'delta_w_conv_out', 'delta_w_o', 'delta_g_mlp', 'delta_w_up', 'delta_w_down', 'delta_g_final', 'new_m_g_mix', 'new_m_w_in', 'new_m_b_gates', 'new_m_sinks', 'new_m_w_attn_out', 'new_m_conv_w', 'new_m_conv_b', 'new_m_w_conv_out', 'new_m_w_o', 'new_m_g_mlp', 'new_m_w_up', 'new_m_w_down', 'new_m_g_final', 'new_v_g_mix', 'new_v_w_in', 'new_v_b_gates', 'new_v_sinks', 'new_v_w_attn_out', 'new_v_conv_w', 'new_v_conv_b', 'new_v_w_conv_out', 'new_v_w_o', 'new_v_g_mlp', 'new_v_w_up', 'new_v_w_down', 'new_v_g_final']
TWIN_LEAF_KINDS = {'loss': 'loss', 'grad_x': 'grad_x', 'grad_g_mix': 'grad_w', 'grad_w_in': 'grad_w', 'grad_b_gates': 'grad_w', 'grad_sinks': 'grad_w', 'grad_w_attn_out': 'grad_w', 'grad_conv_w': 'grad_w', 'grad_conv_b': 'grad_w', 'grad_w_conv_out': 'grad_w', 'grad_w_o': 'grad_w', 'grad_g_mlp': 'grad_w', 'grad_w_up': 'grad_w', 'grad_w_down': 'grad_w', 'grad_g_final': 'grad_w', 'delta_g_mix': 'delta_w', 'delta_w_in': 'delta_w', 'delta_b_gates': 'delta_w', 'delta_sinks': 'delta_w', 'delta_w_attn_out': 'delta_w', 'delta_conv_w': 'delta_w', 'delta_conv_b': 'delta_w', 'delta_w_conv_out': 'delta_w', 'delta_w_o': 'delta_w', 'delta_g_mlp': 'delta_w', 'delta_w_up': 'delta_w', 'delta_w_down': 'delta_w', 'delta_g_final': 'delta_w', 'new_m_g_mix': 'new_m', 'new_m_w_in': 'new_m', 'new_m_b_gates': 'new_m', 'new_m_sinks': 'new_m', 'new_m_w_attn_out': 'new_m', 'new_m_conv_w': 'new_m', 'new_m_conv_b': 'new_m', 'new_m_w_conv_out': 'new_m', 'new_m_w_o': 'new_m', 'new_m_g_mlp': 'new_m', 'new_m_w_up': 'new_m', 'new_m_w_down': 'new_m', 'new_m_g_final': 'new_m', 'new_v_g_mix': 'new_v', 'new_v_w_in': 'new_v', 'new_v_b_gates': 'new_v', 'new_v_sinks': 'new_v', 'new_v_w_attn_out': 'new_v', 'new_v_conv_w': 'new_v', 'new_v_conv_b': 'new_v', 'new_v_w_conv_out': 'new_v', 'new_v_w_o': 'new_v', 'new_v_g_mlp': 'new_v', 'new_v_w_up': 'new_v', 'new_v_w_down': 'new_v', 'new_v_g_final': 'new_v'}


def _forward(args):
    return _fwd_reference(*[args[k] for k in FWD_PARAMS])


def _output_shape():
    out = _jax.eval_shape(lambda: _forward(_fwd_setup_inputs(0)))
    return out.shape, out.dtype

N_MICROBATCH = 1
ADAM_LR = 0.001
ADAM_B1 = 0.9
ADAM_B2 = 0.999
ADAM_EPS = 1e-08
ADAM_WD = 0.01
ADAM_STEP = 10
PER_EXAMPLE_BATCH_AXIS = {'x': 0, 'loss_target': 0}
SHARED_INPUTS = []
_WEIGHT_DTYPES = {'g_mix': _jnp.float32, 'w_in': _jnp.float32, 'b_gates': _jnp.float32, 'sinks': _jnp.float32, 'w_attn_out': _jnp.float32, 'conv_w': _jnp.float32, 'conv_b': _jnp.float32, 'w_conv_out': _jnp.float32, 'w_o': _jnp.float32, 'g_mlp': _jnp.float32, 'w_up': _jnp.float32, 'w_down': _jnp.float32, 'g_final': _jnp.float32}
MOMENT_SCALE = {'g_mix': 2.196924e-01, 'w_in': 8.639616e-02, 'b_gates': 3.566716e-02, 'sinks': 3.912143e-02, 'w_attn_out': 4.121082e-02, 'conv_w': 1.200847e-01, 'conv_b': 1.190823e-01, 'w_conv_out': 1.195129e-01, 'w_o': 1.263726e-01, 'g_mlp': 2.255713e-01, 'w_up': 1.055159e-01, 'w_down': 2.120746e-01, 'g_final': 6.511528e+01}


def _to_microbatches(a, axis):
    t = _jnp.moveaxis(a, axis, 0)
    t = t.reshape((N_MICROBATCH, t.shape[0] // N_MICROBATCH) + t.shape[1:])
    return _jnp.moveaxis(t, 1, axis + 1)


def setup_inputs(seed: int = 0) -> dict:
    inp = _fwd_setup_inputs(seed)
    key = _jax.random.fold_in(_jax.random.key(seed), 7919)
    shape, _ = _output_shape()
    out = dict(inp)
    out["loss_target"] = _jax.random.normal(_jax.random.fold_in(key, 0), shape, _jnp.float32)
    for i, name in enumerate(TWIN_WEIGHTS):
        w = inp[name].astype(_jnp.float32)
        if MOMENT_SCALE is None:
            s = _jnp.sqrt(_jnp.mean(_jnp.square(w)) + 1e-30)
        else:
            s = MOMENT_SCALE[name]
        km, kv = _jax.random.split(_jax.random.fold_in(key, i + 1))
        out[name] = w
        out["m_" + name] = s * _jax.random.normal(km, w.shape, _jnp.float32)
        out["v_" + name] = (s * s) * _jax.random.uniform(kv, w.shape, _jnp.float32, 0.5, 1.5)
    if N_MICROBATCH > 1:
        for name, axis in PER_EXAMPLE_BATCH_AXIS.items():
            out[name] = _to_microbatches(out[name], axis)
    return {'x': out['x'], 'g_mix': out['g_mix'], 'w_in': out['w_in'], 'b_gates': out['b_gates'], 'sinks': out['sinks'], 'w_attn_out': out['w_attn_out'], 'conv_w': out['conv_w'], 'conv_b': out['conv_b'], 'w_conv_out': out['w_conv_out'], 'w_o': out['w_o'], 'g_mlp': out['g_mlp'], 'w_up': out['w_up'], 'w_down': out['w_down'], 'g_final': out['g_final'], 'loss_target': out['loss_target'], 'm_g_mix': out['m_g_mix'], 'm_w_in': out['m_w_in'], 'm_b_gates': out['m_b_gates'], 'm_sinks': out['m_sinks'], 'm_w_attn_out': out['m_w_attn_out'], 'm_conv_w': out['m_conv_w'], 'm_conv_b': out['m_conv_b'], 'm_w_conv_out': out['m_w_conv_out'], 'm_w_o': out['m_w_o'], 'm_g_mlp': out['m_g_mlp'], 'm_w_up': out['m_w_up'], 'm_w_down': out['m_w_down'], 'm_g_final': out['m_g_final'], 'v_g_mix': out['v_g_mix'], 'v_w_in': out['v_w_in'], 'v_b_gates': out['v_b_gates'], 'v_sinks': out['v_sinks'], 'v_w_attn_out': out['v_w_attn_out'], 'v_conv_w': out['v_conv_w'], 'v_conv_b': out['v_conv_b'], 'v_w_conv_out': out['v_w_conv_out'], 'v_w_o': out['v_w_o'], 'v_g_mlp': out['v_g_mlp'], 'v_w_up': out['v_w_up'], 'v_w_down': out['v_w_down'], 'v_g_final': out['v_g_final']}


def _loss(weights, diff, rest, loss_target):
    with _jax.named_scope("forward"):
        args = {**rest, TWIN_DIFF_INPUT: diff, **{k: w.astype(_WEIGHT_DTYPES[k]) for k, w in weights.items()}}
        y = _forward(args)
    with _jax.named_scope("loss_head"):
        err = _jnp.square(y.astype(_jnp.float32) - loss_target)
        return 0.5 * _jnp.sum(_jnp.mean(err, axis=-1)) if err.ndim else 0.5 * err


def _adamw(w, g, m, v):
    m = ADAM_B1 * m + (1.0 - ADAM_B1) * g
    v = ADAM_B2 * v + (1.0 - ADAM_B2) * _jnp.square(g)
    m_hat = m / (1.0 - ADAM_B1 ** ADAM_STEP)
    v_hat = v / (1.0 - ADAM_B2 ** ADAM_STEP)
    delta = -ADAM_LR * (m_hat / (_jnp.sqrt(v_hat) + ADAM_EPS) + ADAM_WD * w)
    return delta, m, v


def reference(x, g_mix, w_in, b_gates, sinks, w_attn_out, conv_w, conv_b, w_conv_out, w_o, g_mlp, w_up, w_down, g_final, loss_target, m_g_mix, m_w_in, m_b_gates, m_sinks, m_w_attn_out, m_conv_w, m_conv_b, m_w_conv_out, m_w_o, m_g_mlp, m_w_up, m_w_down, m_g_final, v_g_mix, v_w_in, v_b_gates, v_sinks, v_w_attn_out, v_conv_w, v_conv_b, v_w_conv_out, v_w_o, v_g_mlp, v_w_up, v_w_down, v_g_final):
    given = dict(x=x, g_mix=g_mix, w_in=w_in, b_gates=b_gates, sinks=sinks, w_attn_out=w_attn_out, conv_w=conv_w, conv_b=conv_b, w_conv_out=w_conv_out, w_o=w_o, g_mlp=g_mlp, w_up=w_up, w_down=w_down, g_final=g_final, loss_target=loss_target, m_g_mix=m_g_mix, m_w_in=m_w_in, m_b_gates=m_b_gates, m_sinks=m_sinks, m_w_attn_out=m_w_attn_out, m_conv_w=m_conv_w, m_conv_b=m_conv_b, m_w_conv_out=m_w_conv_out, m_w_o=m_w_o, m_g_mlp=m_g_mlp, m_w_up=m_w_up, m_w_down=m_w_down, m_g_final=m_g_final, v_g_mix=v_g_mix, v_w_in=v_w_in, v_b_gates=v_b_gates, v_sinks=v_sinks, v_w_attn_out=v_w_attn_out, v_conv_w=v_conv_w, v_conv_b=v_conv_b, v_w_conv_out=v_w_conv_out, v_w_o=v_w_o, v_g_mlp=v_g_mlp, v_w_up=v_w_up, v_w_down=v_w_down, v_g_final=v_g_final)
    weights = {n: given[n] for n in TWIN_WEIGHTS}
    shared = {n: given[n] for n in SHARED_INPUTS}
    per_example = {n: given[n] for n in ['x']}
    grad_fn = _jax.value_and_grad(_loss, argnums=(0, 1))

    def one_microbatch(ex, loss_target):
        ex = dict(ex)
        diff = ex.pop(TWIN_DIFF_INPUT)
        return grad_fn(weights, diff, {**shared, **ex}, loss_target)

    if N_MICROBATCH == 1:
        loss, (grad_w, grad_x) = one_microbatch(per_example, given["loss_target"])
    else:
        def body(carry, xs):
            loss_sum, grad_sum = carry
            l_k, (gw_k, gx_k) = one_microbatch(xs[0], xs[1])
            with _jax.named_scope("update"):
                return (loss_sum + l_k, _jax.tree.map(_jnp.add, grad_sum, gw_k)), gx_k

        init = (_jnp.zeros((), _jnp.float32), _jax.tree.map(_jnp.zeros_like, weights))
        (loss, grad_w), grad_x = _jax.lax.scan(body, init, (per_example, given["loss_target"]))
    with _jax.named_scope("update"):
        delta_w, new_m, new_v = {}, {}, {}
        for n in TWIN_WEIGHTS:
            delta_w[n], new_m[n], new_v[n] = _adamw(weights[n], grad_w[n], given["m_" + n], given["v_" + n])
    return (loss, grad_x, *[grad_w[n] for n in TWIN_WEIGHTS], *[delta_w[n] for n in TWIN_WEIGHTS],
            *[new_m[n] for n in TWIN_WEIGHTS], *[new_v[n] for n in TWIN_WEIGHTS])
```

```python
import functools

import numpy as np
import jax
import jax.numpy as jnp
from jax import lax
from jax.experimental import pallas as pl
from jax.experimental.pallas import tpu as pltpu

F32 = jnp.float32
BF16 = jnp.bfloat16

D_MODEL = 1024
HEAD_DIM = 64
N_Q_HEADS = 16
N_KV_HEADS = 4
GQA_GROUP = 4
WINDOW = 128
D_FF = 4096
DEPTH = 2
RMS_EPS = 1e-6
NEG_INF = -1e30
ATTN_SCALE = HEAD_DIM ** -0.5
QKV_W = 1536
CONV_W = 3072
GATE_W = 2048
IN_COLS = QKV_W + CONV_W + GATE_W
COL_TILE = 512
N_CHIP = 4
ADAM_LR = 0.001
ADAM_B1 = 0.9
ADAM_B2 = 0.999
ADAM_EPS = 1e-08
ADAM_WD = 0.01
ADAM_STEP = 10
V7X_VMEM_BYTES = 64 * 2 ** 20
VMEM_LIMIT = V7X_VMEM_BYTES - 8 * 2 ** 20
MESH = pl.DeviceIdType.MESH
ANY = pl.BlockSpec(memory_space=pl.ANY)
SMALL_ROWS = 24

_SLOPES = [float(v) for v in np.power(np.float32(2.0), -8.0 * np.arange(1, N_Q_HEADS + 1, dtype=np.float32) / N_Q_HEADS)]


def _params(*sem):
    return pltpu.CompilerParams(dimension_semantics=sem, vmem_limit_bytes=VMEM_LIMIT)


def _nt(a, b):
    return lax.dot_general(a, b, (((1,), (1,)), ((), ())), preferred_element_type=F32)


def _tn(a, b):
    return lax.dot_general(a, b, (((0,), (0,)), ((), ())), preferred_element_type=F32)


def _nn(a, b):
    return jnp.dot(a, b, preferred_element_type=F32)


def _rms_stats(xf):
    r = lax.rsqrt(jnp.mean(xf * xf, axis=-1, keepdims=True) + RMS_EPS)
    return r, xf * r


def _rms_bwd(dh, xh, r, g):
    dxh = dh * g
    dx = r * (dxh - xh * jnp.mean(dxh * xh, axis=-1, keepdims=True))
    dg = jnp.sum(dh * xh, axis=0, keepdims=True)
    return dx, dg


def _rms_fwd(x, g):
    T, D = x.shape
    tm = min(512, T)

    def kern(x_ref, g_ref, h_ref):
        _, xh = _rms_stats(x_ref[...])
        h_ref[...] = (xh * g_ref[...]).astype(BF16)

    return pl.pallas_call(
        kern, out_shape=jax.ShapeDtypeStruct((T, D), BF16), grid=(T // tm,),
        in_specs=[pl.BlockSpec((tm, D), lambda i: (i, 0)), pl.BlockSpec((1, D), lambda i: (0, 0))],
        out_specs=pl.BlockSpec((tm, D), lambda i: (i, 0)),
        name="rms_fwd", compiler_params=_params("parallel"))(x, g)


def _proj(h, w, col_off, ncols, tn):
    T, K = h.shape
    tm = min(512, T)
    off = col_off // tn

    def kern(a_ref, w_ref, o_ref):
        o_ref[...] = _nn(a_ref[...], w_ref[...]).astype(BF16)

    return pl.pallas_call(
        kern, out_shape=jax.ShapeDtypeStruct((T, ncols), BF16), grid=(T // tm, ncols // tn),
        in_specs=[pl.BlockSpec((tm, K), lambda i, j: (i, 0)), pl.BlockSpec((K, tn), lambda i, j: (0, off + j))],
        out_specs=pl.BlockSpec((tm, tn), lambda i, j: (i, j)),
        name="proj", compiler_params=_params("parallel", "arbitrary"))(h, w)


def _band_masks(first_i):
    r = lax.broadcasted_iota(jnp.int32, (GQA_GROUP * WINDOW, 2 * WINDOW), 0)
    jj = lax.broadcasted_iota(jnp.int32, (GQA_GROUP * WINDOW, 2 * WINDOW), 1)
    dist = WINDOW + (r & (WINDOW - 1)) - jj
    valid = (dist >= 0) & (dist < WINDOW) & ((jj + (1 - first_i) * WINDOW) >= WINDOW)
    return dist.astype(F32), valid


def _per_head_col(vals):
    rb = lax.broadcasted_iota(jnp.int32, (GQA_GROUP * WINDOW, 1), 0) >> 7
    col = jnp.full((GQA_GROUP * WINDOW, 1), vals[3], F32)
    for g in (2, 1, 0):
        col = jnp.where(rb == g, vals[g], col)
    return col


def _stack_heads(ref, base, hk):
    return jnp.concatenate(
        [ref[:, base + HEAD_DIM * (GQA_GROUP * hk + g): base + HEAD_DIM * (GQA_GROUP * hk + g + 1)] for g in range(GQA_GROUP)],
        axis=0)


def _attn_fwd(pqkv, sinks, seq):
    T = pqkv.shape[0]
    nblk = seq // WINDOW
    K0, V0 = N_Q_HEADS * HEAD_DIM, N_Q_HEADS * HEAD_DIM + N_KV_HEADS * HEAD_DIM

    def kern(sink_ref, cur_ref, prev_ref, o_ref, lse_ref):
        i = pl.program_id(0)
        first_i = ((i % nblk) == 0).astype(jnp.int32)
        distf, valid = _band_masks(first_i)
        outs = []
        for hk in range(N_KV_HEADS):
            k_band = jnp.concatenate([prev_ref[:, HEAD_DIM * hk:HEAD_DIM * (hk + 1)],
                                      cur_ref[:, K0 + HEAD_DIM * hk:K0 + HEAD_DIM * (hk + 1)]], axis=0)
            v_band = jnp.concatenate([prev_ref[:, 256 + HEAD_DIM * hk:256 + HEAD_DIM * (hk + 1)],
                                      cur_ref[:, V0 + HEAD_DIM * hk:V0 + HEAD_DIM * (hk + 1)]], axis=0)
            q_g = _stack_heads(cur_ref, 0, hk)
            slope = _per_head_col([_SLOPES[GQA_GROUP * hk + g] for g in range(GQA_GROUP)])
            sink = _per_head_col([sink_ref[GQA_GROUP * hk + g] for g in range(GQA_GROUP)])
            s = _nt(q_g, k_band) * ATTN_SCALE + (-slope) * distf
            s = jnp.where(valid, s, NEG_INF)
            m = jnp.maximum(jnp.max(s, axis=-1, keepdims=True), sink)
            p = jnp.exp(s - m)
            denom = jnp.sum(p, axis=-1, keepdims=True) + jnp.exp(sink - m)
            o_g = _nn((p / denom).astype(BF16), v_band)
            lse_g = m + jnp.log(denom)
            for g in range(GQA_GROUP):
                h = GQA_GROUP * hk + g
                outs.append(o_g[WINDOW * g:WINDOW * (g + 1)].astype(BF16))
                lse_ref[:, h:h + 1] = lse_g[WINDOW * g:WINDOW * (g + 1)]
        o_ref[...] = jnp.concatenate(outs, axis=1)

    return pl.pallas_call(
        kern,
        out_shape=(jax.ShapeDtypeStruct((T, D_MODEL), BF16), jax.ShapeDtypeStruct((T, N_Q_HEADS), F32)),
        grid=(T // WINDOW,),
        in_specs=[pl.BlockSpec(memory_space=pltpu.SMEM),
                  pl.BlockSpec((WINDOW, QKV_W), lambda i: (i, 0)),
                  pl.BlockSpec((WINDOW, 512), lambda i: (jnp.maximum(i - 1, 0), 2))],
        out_specs=(pl.BlockSpec((WINDOW, D_MODEL), lambda i: (i, 0)), pl.BlockSpec((WINDOW, N_Q_HEADS), lambda i: (i, 0))),
        name="attn_fwd", compiler_params=_params("parallel"))(sinks, pqkv, pqkv)


def _pick_row(a, row):
    rid = lax.broadcasted_iota(jnp.int32, a.shape, 0)
    return jnp.sum(jnp.where(rid == row, a, 0.0), axis=0, keepdims=True)


def _conv_taps(yc, halo_yc, first_i):
    keep = (1 - first_i).astype(F32)
    p1 = _pick_row(halo_yc, 15) * keep
    p2 = _pick_row(halo_yc, 14) * keep
    rowid = lax.broadcasted_iota(jnp.int32, yc.shape, 0)
    s1 = jnp.where(rowid == 0, p1, pltpu.roll(yc, 1, 0))
    s2 = jnp.where(rowid == 0, p2, jnp.where(rowid == 1, p1, pltpu.roll(yc, 2, 0)))
    return s1, s2


def _conv_fwd(pconv, conv_w, conv_b, seq):
    T = pconv.shape[0]
    tm = min(256, seq)
    per_seq = seq // tm
    D = D_MODEL

    def kern(cur_ref, halo_ref, w_ref, b_ref, o_ref):
        i = pl.program_id(0)
        first_i = ((i % per_seq) == 0).astype(jnp.int32)
        cb = cur_ref[:, 0:D].astype(F32)
        yc = cur_ref[:, D:2 * D].astype(F32) * cur_ref[:, 2 * D:3 * D].astype(F32)
        halo_yc = halo_ref[:, D:2 * D].astype(F32) * halo_ref[:, 2 * D:3 * D].astype(F32)
        s1, s2 = _conv_taps(yc, halo_yc, first_i)
        z = w_ref[0:1, :] * s2 + w_ref[1:2, :] * s1 + w_ref[2:3, :] * yc
        o_ref[...] = (cb * (z + b_ref[...])).astype(BF16)

    return pl.pallas_call(
        kern, out_shape=jax.ShapeDtypeStruct((T, D), BF16), grid=(T // tm,),
        in_specs=[pl.BlockSpec((tm, CONV_W), lambda i: (i, 0)),
                  pl.BlockSpec((16, CONV_W), lambda i: (jnp.maximum(i * (tm // 16) - 1, 0), 0)),
                  pl.BlockSpec((3, D), lambda i: (0, 0)), pl.BlockSpec((1, D), lambda i: (0, 0))],
        out_specs=pl.BlockSpec((tm, D), lambda i: (i, 0)),
        name="conv_fwd", compiler_params=_params("parallel"))(pconv, pconv, conv_w, conv_b)


def _mix_fwd(x, att, cv, pgate, b_gates, wao, wco, wo):
    T, D = x.shape
    tm = min(512, T)

    def kern(x_ref, att_ref, cv_ref, pg_ref, bg_ref, wao_ref, wco_ref, wo_ref, x1_ref, ya_ref, yc_ref, mg_ref):
        ya = _nn(att_ref[...], wao_ref[...])
        yc = _nn(cv_ref[...], wco_ref[...])
        sa = jax.nn.sigmoid(pg_ref[:, 0:D].astype(F32) + bg_ref[:, 0:D])
        sc = jax.nn.sigmoid(pg_ref[:, D:2 * D].astype(F32) + bg_ref[:, D:2 * D])
        mg = (sa * ya + sc * yc).astype(BF16)
        ya_ref[...] = ya.astype(BF16)
        yc_ref[...] = yc.astype(BF16)
        mg_ref[...] = mg
        x1_ref[...] = x_ref[...] + _nn(mg, wo_ref[...])

    row = lambda w: pl.BlockSpec((tm, w), lambda i: (i, 0))
    full = lambda a, b: pl.BlockSpec((a, b), lambda i: (0, 0))
    bf = jax.ShapeDtypeStruct((T, D), BF16)
    return pl.pallas_call(
        kern, out_shape=(jax.ShapeDtypeStruct((T, D), F32), bf, bf, bf), grid=(T // tm,),
        in_specs=[row(D), row(D), row(D), row(GATE_W), full(1, GATE_W), full(D, D), full(D, D), full(D, D)],
        out_specs=(row(D), row(D), row(D), row(D)),
        name="mix_fwd", compiler_params=_params("parallel"))(x, att, cv, pgate, b_gates, wao, wco, wo)


def _mlp_fwd(x1, g, wup, wdn):
    T, D = x1.shape
    tm = min(512, T)
    nj = D_FF // D

    def kern(x_ref, g_ref, wup_ref, wdn_ref, x2_ref, a_ref, h_scr, acc_scr):
        j = pl.program_id(1)

        @pl.when(j == 0)
        def _():
            xf = x_ref[...]
            _, xh = _rms_stats(xf)
            h_scr[...] = (xh * g_ref[...]).astype(BF16)
            acc_scr[...] = xf

        a = _nn(h_scr[...], wup_ref[...])
        a_ref[...] = a.astype(BF16)
        u = jnp.square(jnp.maximum(a, 0.0)).astype(BF16)
        acc_scr[...] += _nn(u, wdn_ref[...])

        @pl.when(j == nj - 1)
        def _():
            x2_ref[...] = acc_scr[...]

    return pl.pallas_call(
        kern, out_shape=(jax.ShapeDtypeStruct((T, D), F32), jax.ShapeDtypeStruct((T, D_FF), BF16)), grid=(T // tm, nj),
        in_specs=[pl.BlockSpec((tm, D), lambda i, j: (i, 0)), pl.BlockSpec((1, D), lambda i, j: (0, 0)),
                  pl.BlockSpec((D, D), lambda i, j: (0, j)), pl.BlockSpec((D, D), lambda i, j: (j, 0))],
        out_specs=(pl.BlockSpec((tm, D), lambda i, j: (i, 0)), pl.BlockSpec((tm, D), lambda i, j: (i, j))),
        scratch_shapes=[pltpu.VMEM((tm, D), BF16), pltpu.VMEM((tm, D), F32)],
        name="mlp_fwd", compiler_params=_params("parallel", "arbitrary"))(x1, g, wup, wdn)


def _loss_bwd(x, g, tgt):
    T, D = x.shape
    tm = min(512, T)

    def kern(x_ref, g_ref, t_ref, st_ref, dx_ref):
        i = pl.program_id(0)

        @pl.when(i == 0)
        def _():
            st_ref[...] = jnp.zeros_like(st_ref)

        gg = g_ref[...]
        r, xh = _rms_stats(x_ref[...])
        e = xh * gg - t_ref[...]
        part = 0.5 * jnp.sum(jnp.mean(e * e, axis=-1, keepdims=True), axis=0, keepdims=True)
        dx, dg = _rms_bwd(e * (1.0 / D), xh, r, gg)
        dx_ref[...] = dx
        st_ref[0:1, :] += dg
        st_ref[1:2, 0:1] += part

    return pl.pallas_call(
        kern, out_shape=(jax.ShapeDtypeStruct((8, D), F32), jax.ShapeDtypeStruct((T, D), F32)), grid=(T // tm,),
        in_specs=[pl.BlockSpec((tm, D), lambda i: (i, 0)), pl.BlockSpec((1, D), lambda i: (0, 0)),
                  pl.BlockSpec((tm, D), lambda i: (i, 0))],
        out_specs=(pl.BlockSpec((8, D), lambda i: (0, 0)), pl.BlockSpec((tm, D), lambda i: (i, 0))),
        name="loss_bwd", compiler_params=_params("arbitrary"))(x, g, tgt)


def _mlp_bwd(dx2, x1, a, g, wup, wdn):
    T, D = x1.shape
    tm = min(512, T)
    nj = D_FF // D

    def kern(dx2_ref, x1_ref, a_ref, g_ref, wup_ref, wdn_ref, dx1_ref, da_ref, h2_ref, dg_ref, dyb_scr, acc_scr):
        i, j = pl.program_id(0), pl.program_id(1)

        @pl.when((i == 0) & (j == 0))
        def _():
            dg_ref[...] = jnp.zeros_like(dg_ref)

        @pl.when(j == 0)
        def _():
            dyb_scr[...] = dx2_ref[...].astype(BF16)
            acc_scr[...] = jnp.zeros_like(acc_scr)

        du = _nt(dyb_scr[...], wdn_ref[...])
        da = (du * (2.0 * jnp.maximum(a_ref[...].astype(F32), 0.0))).astype(BF16)
        da_ref[...] = da
        acc_scr[...] += _nt(da, wup_ref[...])

        @pl.when(j == nj - 1)
        def _():
            gg = g_ref[...]
            r, xh = _rms_stats(x1_ref[...])
            h2_ref[...] = (xh * gg).astype(BF16)
            dx, dg = _rms_bwd(acc_scr[...], xh, r, gg)
            dx1_ref[...] = dx2_ref[...] + dx
            dg_ref[...] += dg

    return pl.pallas_call(
        kern,
        out_shape=(jax.ShapeDtypeStruct((T, D), F32), jax.ShapeDtypeStruct((T, D_FF), BF16),
                   jax.ShapeDtypeStruct((T, D), BF16), jax.ShapeDtypeStruct((1, D), F32)),
        grid=(T // tm, nj),
        in_specs=[pl.BlockSpec((tm, D), lambda i, j: (i, 0)), pl.BlockSpec((tm, D), lambda i, j: (i, 0)),
                  pl.BlockSpec((tm, D), lambda i, j: (i, j)), pl.BlockSpec((1, D), lambda i, j: (0, 0)),
                  pl.BlockSpec((D, D), lambda i, j: (0, j)), pl.BlockSpec((D, D), lambda i, j: (j, 0))],
        out_specs=(pl.BlockSpec((tm, D), lambda i, j: (i, 0)), pl.BlockSpec((tm, D), lambda i, j: (i, j)),
                   pl.BlockSpec((tm, D), lambda i, j: (i, 0)), pl.BlockSpec((1, D), lambda i, j: (0, 0))),
        scratch_shapes=[pltpu.VMEM((tm, D), BF16), pltpu.VMEM((tm, D), F32)],
        name="mlp_bwd", compiler_params=_params("arbitrary", "arbitrary"))(dx2, x1, a, g, wup, wdn)


def _mix_bwd(dx1, ya, yc, pgate, b_gates, wao, wco, wo):
    T, D = dx1.shape
    tm = min(256, T)

    def kern(dx_ref, ya_ref, yc_ref, pg_ref, bg_ref, wao_ref, wco_ref, wo_ref,
             datt_ref, dcv_ref, dya_ref, dyc_ref, dgt_ref, dbg_ref):
        @pl.when(pl.program_id(0) == 0)
        def _():
            dbg_ref[...] = jnp.zeros_like(dbg_ref)

        dm = _nt(dx_ref[...].astype(BF16), wo_ref[...])
        sa = jax.nn.sigmoid(pg_ref[:, 0:D].astype(F32) + bg_ref[:, 0:D])
        sc = jax.nn.sigmoid(pg_ref[:, D:2 * D].astype(F32) + bg_ref[:, D:2 * D])
        dya = (dm * sa).astype(BF16)
        dyc = (dm * sc).astype(BF16)
        dga = dm * ya_ref[...].astype(F32) * (sa * (1.0 - sa))
        dgc = dm * yc_ref[...].astype(F32) * (sc * (1.0 - sc))
        dya_ref[...] = dya
        dyc_ref[...] = dyc
        dgt_ref[:, 0:D] = dga.astype(BF16)
        dgt_ref[:, D:2 * D] = dgc.astype(BF16)
        dbg_ref[:, 0:D] += jnp.sum(dga, axis=0, keepdims=True)
        dbg_ref[:, D:2 * D] += jnp.sum(dgc, axis=0, keepdims=True)
        datt_ref[...] = _nt(dya, wao_ref[...]).astype(BF16)
        dcv_ref[...] = _nt(dyc, wco_ref[...]).astype(BF16)

    row = lambda w: pl.BlockSpec((tm, w), lambda i: (i, 0))
    full = lambda a, b: pl.BlockSpec((a, b), lambda i: (0, 0))
    bf = jax.ShapeDtypeStruct((T, D), BF16)
    return pl.pallas_call(
        kern,
        out_shape=(bf, bf, bf, bf, jax.ShapeDtypeStruct((T, GATE_W), BF16), jax.ShapeDtypeStruct((1, GATE_W), F32)),
        grid=(T // tm,),
        in_specs=[row(D), row(D), row(D), row(GATE_W), full(1, GATE_W), full(D, D), full(D, D), full(D, D)],
        out_specs=(row(D), row(D), row(D), row(D), row(GATE_W), full(1, GATE_W)),
        name="mix_bwd", compiler_params=_params("arbitrary"))(dx1, ya, yc, pgate, b_gates, wao, wco, wo)


def _conv_bwd(dcv, pconv, conv_w, conv_b, seq):
    T = pconv.shape[0]
    tm = min(256, seq)
    per_seq = seq // tm
    D = D_MODEL
    nb16 = T // 16

    def kern(dcv_ref, dcvn_ref, cur_ref, prev_ref, next_ref, w_ref, b_ref, o_ref, dwb_ref):
        i = pl.program_id(0)

        @pl.when(i == 0)
        def _():
            dwb_ref[...] = jnp.zeros_like(dwb_ref)

        first_i = ((i % per_seq) == 0).astype(jnp.int32)
        keep_next = 1.0 - (((i + 1) % per_seq) == 0).astype(F32)
        cb = cur_ref[:, 0:D].astype(F32)
        cc = cur_ref[:, D:2 * D].astype(F32)
        cu = cur_ref[:, 2 * D:3 * D].astype(F32)
        yc = cc * cu
        halo_yc = prev_ref[:, D:2 * D].astype(F32) * prev_ref[:, 2 * D:3 * D].astype(F32)
        s1, s2 = _conv_taps(yc, halo_yc, first_i)
        w0, w1, w2 = w_ref[0:1, :], w_ref[1:2, :], w_ref[2:3, :]
        z = w0 * s2 + w1 * s1 + w2 * yc
        dcv = dcv_ref[...].astype(F32)
        dz = dcv * cb
        dzn = dcvn_ref[...].astype(F32) * next_ref[:, 0:D].astype(F32) * keep_next
        n1, n2 = _pick_row(dzn, 0), _pick_row(dzn, 1)
        rowid = lax.broadcasted_iota(jnp.int32, dz.shape, 0)
        u1 = jnp.where(rowid == tm - 1, n1, pltpu.roll(dz, tm - 1, 0))
        u2 = jnp.where(rowid == tm - 1, n2, jnp.where(rowid == tm - 2, n1, pltpu.roll(dz, tm - 2, 0)))
        dyc = w2 * dz + w1 * u1 + w0 * u2
        o_ref[:, 0:D] = (dcv * (z + b_ref[...])).astype(BF16)
        o_ref[:, D:2 * D] = (dyc * cu).astype(BF16)
        o_ref[:, 2 * D:3 * D] = (dyc * cc).astype(BF16)
        dwb_ref[0:1, :] += jnp.sum(dz * s2, axis=0, keepdims=True)
        dwb_ref[1:2, :] += jnp.sum(dz * s1, axis=0, keepdims=True)
        dwb_ref[2:3, :] += jnp.sum(dz * yc, axis=0, keepdims=True)
        dwb_ref[3:4, :] += jnp.sum(dz, axis=0, keepdims=True)

    prev_map = lambda i: (jnp.maximum(i * (tm // 16) - 1, 0), 0)
    next_map = lambda i: (jnp.minimum((i + 1) * (tm // 16), nb16 - 1), 0)
    return pl.pallas_call(
        kern, out_shape=(jax.ShapeDtypeStruct((T, CONV_W), BF16), jax.ShapeDtypeStruct((8, D), F32)), grid=(T // tm,),
        in_specs=[pl.BlockSpec((tm, D), lambda i: (i, 0)), pl.BlockSpec((16, D), next_map),
                  pl.BlockSpec((tm, CONV_W), lambda i: (i, 0)), pl.BlockSpec((16, CONV_W), prev_map),
                  pl.BlockSpec((16, CONV_W), next_map),
                  pl.BlockSpec((3, D), lambda i: (0, 0)), pl.BlockSpec((1, D), lambda i: (0, 0))],
        out_specs=(pl.BlockSpec((tm, CONV_W), lambda i: (i, 0)), pl.BlockSpec((8, D), lambda i: (0, 0))),
        name="conv_bwd", compiler_params=_params("arbitrary"))(dcv, dcv, pconv, pconv, pconv, conv_w, conv_b)


def _attn_bwd(pqkv, att, datt, lse, sinks, seq):
    T = pqkv.shape[0]
    nblk = seq // WINDOW
    nseq = T // seq
    K0, V0 = N_Q_HEADS * HEAD_DIM, N_Q_HEADS * HEAD_DIM + N_KV_HEADS * HEAD_DIM
    KVW = N_KV_HEADS * HEAD_DIM

    def kern(sink_ref, cur_ref, prev_ref, o_ref, do_ref, lse_ref, dq_ref, dkv_ref, ds_ref, kc_scr, vc_scr):
        b, st = pl.program_id(0), pl.program_id(1)

        @pl.when((b == 0) & (st == 0))
        def _():
            ds_ref[...] = jnp.zeros_like(ds_ref)

        @pl.when(st == 0)
        def _():
            kc_scr[...] = jnp.zeros_like(kc_scr)
            vc_scr[...] = jnp.zeros_like(vc_scr)

        @pl.when(st < nblk)
        def _():
            first_i = (st == 0).astype(jnp.int32)
            distf, valid = _band_masks(first_i)
            od = o_ref[...].astype(F32) * do_ref[...].astype(F32)
            dqs = []
            for hk in range(N_KV_HEADS):
                k_band = jnp.concatenate([prev_ref[:, HEAD_DIM * hk:HEAD_DIM * (hk + 1)],
                                          cur_ref[:, K0 + HEAD_DIM * hk:K0 + HEAD_DIM * (hk + 1)]], axis=0)
                v_band = jnp.concatenate([prev_ref[:, KVW + HEAD_DIM * hk:KVW + HEAD_DIM * (hk + 1)],
                                          cur_ref[:, V0 + HEAD_DIM * hk:V0 + HEAD_DIM * (hk + 1)]], axis=0)
                q_g = _stack_heads(cur_ref, 0, hk)
                do_g = _stack_heads(do_ref, 0, hk)
                heads = [GQA_GROUP * hk + g for g in range(GQA_GROUP)]
                slope = _per_head_col([_SLOPES[h] for h in heads])
                sink = _per_head_col([sink_ref[h] for h in heads])
                lse_g = jnp.concatenate([lse_ref[:, h:h + 1] for h in heads], axis=0)
                d_g = jnp.concatenate(
                    [jnp.sum(od[:, HEAD_DIM * h:HEAD_DIM * (h + 1)], axis=-1, keepdims=True) for h in heads], axis=0)
                s = _nt(q_g, k_band) * ATTN_SCALE + (-slope) * distf
                p = jnp.where(valid, jnp.exp(s - lse_g), 0.0)
                dp = _nt(do_g, v_band)
                dsb = (p * (dp - d_g)).astype(BF16)
                dq_g = _nn(dsb, k_band) * ATTN_SCALE
                dk_b = _tn(dsb, q_g) * ATTN_SCALE
                dv_b = _tn(p.astype(BF16), do_g)
                psd = jnp.exp(sink - lse_g) * d_g
                for g in range(GQA_GROUP):
                    h = heads[g]
                    dqs.append(dq_g[WINDOW * g:WINDOW * (g + 1)].astype(BF16))
                    ds_ref[0:1, h:h + 1] -= jnp.sum(psd[WINDOW * g:WINDOW * (g + 1)], axis=0, keepdims=True)
                ksl = slice(HEAD_DIM * hk, HEAD_DIM * (hk + 1))
                vsl = slice(KVW + HEAD_DIM * hk, KVW + HEAD_DIM * (hk + 1))
                dkv_ref[:, ksl] = (kc_scr[:, ksl] + dk_b[0:WINDOW]).astype(BF16)
                dkv_ref[:, vsl] = (vc_scr[:, ksl] + dv_b[0:WINDOW]).astype(BF16)
                kc_scr[:, ksl] = dk_b[WINDOW:2 * WINDOW]
                vc_scr[:, ksl] = dv_b[WINDOW:2 * WINDOW]
            dq_ref[...] = jnp.concatenate(dqs, axis=1)

        @pl.when(st == nblk)
        def _():
            dkv_ref[:, 0:KVW] = kc_scr[...].astype(BF16)
            dkv_ref[:, KVW:2 * KVW] = vc_scr[...].astype(BF16)

    cur_map = lambda b, s: (b * nblk + jnp.minimum(s, nblk - 1), 0)
    prev_row = lambda b, s: b * nblk + jnp.clip(s - 1, 0, nblk - 1)
    return pl.pallas_call(
        kern,
        out_shape=(jax.ShapeDtypeStruct((T, D_MODEL), BF16), jax.ShapeDtypeStruct((T, 2 * KVW), BF16),
                   jax.ShapeDtypeStruct((8, 128), F32)),
        grid=(nseq, nblk + 1),
        in_specs=[pl.BlockSpec(memory_space=pltpu.SMEM),
                  pl.BlockSpec((WINDOW, QKV_W), cur_map),
                  pl.BlockSpec((WINDOW, 2 * KVW), lambda b, s: (prev_row(b, s), 2)),
                  pl.BlockSpec((WINDOW, D_MODEL), cur_map), pl.BlockSpec((WINDOW, D_MODEL), cur_map),
                  pl.BlockSpec((WINDOW, N_Q_HEADS), cur_map)],
        out_specs=(pl.BlockSpec((WINDOW, D_MODEL), cur_map),
                   pl.BlockSpec((WINDOW, 2 * KVW), lambda b, s: (prev_row(b, s), 0)),
                   pl.BlockSpec((8, 128), lambda b, s: (0, 0))),
        scratch_shapes=[pltpu.VMEM((WINDOW, KVW), F32), pltpu.VMEM((WINDOW, KVW), F32)],
        name="attn_bwd", compiler_params=_params("arbitrary", "arbitrary"))(sinks, pqkv, pqkv, att, datt, lse)


def _piece_tiles(pieces):
    out, start = [], 0
    for arr, width in pieces:
        out.append((arr, start, width // COL_TILE))
        start += width // COL_TILE
    return out, start


def _inproj_bwd(pieces, w_in, x, dx_in, g):
    T, D = x.shape
    tm = min(512, T)
    tiles, nk = _piece_tiles(pieces)

    def kern(*refs):
        p_refs = refs[:len(tiles)]
        w_ref, x_ref, dxin_ref, g_ref, dx_ref, dg_ref, acc_scr = refs[len(tiles):]
        i, k = pl.program_id(0), pl.program_id(1)

        @pl.when((i == 0) & (k == 0))
        def _():
            dg_ref[...] = jnp.zeros_like(dg_ref)

        @pl.when(k == 0)
        def _():
            acc_scr[...] = jnp.zeros_like(acc_scr)

        for p_ref, (_, start, n) in zip(p_refs, tiles):
            @pl.when((k >= start) & (k < start + n))
            def _(p_ref=p_ref):
                acc_scr[...] += _nt(p_ref[...], w_ref[...])

        @pl.when(k == nk - 1)
        def _():
            gg = g_ref[...]
            r, xh = _rms_stats(x_ref[...])
            dx, dg = _rms_bwd(acc_scr[...], xh, r, gg)
            dx_ref[...] = dxin_ref[...] + dx
            dg_ref[...] += dg

    p_specs = [pl.BlockSpec((tm, COL_TILE), lambda i, k, s=start, n=n: (i, jnp.clip(k - s, 0, n - 1))) for _, start, n in tiles]
    return pl.pallas_call(
        kern, out_shape=(jax.ShapeDtypeStruct((T, D), F32), jax.ShapeDtypeStruct((1, D), F32)), grid=(T // tm, nk),
        in_specs=p_specs + [pl.BlockSpec((D, COL_TILE), lambda i, k: (0, k)),
                            pl.BlockSpec((tm, D), lambda i, k: (i, 0)), pl.BlockSpec((tm, D), lambda i, k: (i, 0)),
                            pl.BlockSpec((1, D), lambda i, k: (0, 0))],
        out_specs=(pl.BlockSpec((tm, D), lambda i, k: (i, 0)), pl.BlockSpec((1, D), lambda i, k: (0, 0))),
        scratch_shapes=[pltpu.VMEM((tm, D), F32)],
        name="inproj_bwd", compiler_params=_params("arbitrary", "arbitrary"))(*[a for a, _, _ in tiles], w_in, x, dx_in, g)


def _dw_pieces(lhs, pieces):
    T, K = lhs.shape
    tk = min(1024, T)
    nt = T // tk
    tiles, nj = _piece_tiles(pieces)

    def kern(*refs):
        lhs_ref = refs[0]
        p_refs = refs[1:1 + len(tiles)]
        o_ref, ob_ref = refs[1 + len(tiles):]
        j, t = pl.program_id(0), pl.program_id(1)

        @pl.when(t == 0)
        def _():
            o_ref[...] = jnp.zeros_like(o_ref)

        for p_ref, (_, start, n) in zip(p_refs, tiles):
            @pl.when((j >= start) & (j < start + n))
            def _(p_ref=p_ref):
                o_ref[...] += _tn(lhs_ref[...], p_ref[...])

        @pl.when(t == nt - 1)
        def _():
            ob_ref[...] = o_ref[...].astype(BF16)

    def p_map(start, n):
        return lambda j, t: (jnp.where((j >= start) & (j < start + n), t, 0), jnp.clip(j - start, 0, n - 1))

    N = nj * COL_TILE
    return pl.pallas_call(
        kern, out_shape=(jax.ShapeDtypeStruct((K, N), F32), jax.ShapeDtypeStruct((K, N), BF16)), grid=(nj, nt),
        in_specs=[pl.BlockSpec((tk, K), lambda j, t: (t, 0))] + [pl.BlockSpec((tk, COL_TILE), p_map(s, n)) for _, s, n in tiles],
        out_specs=(pl.BlockSpec((K, COL_TILE), lambda j, t: (0, j)), pl.BlockSpec((K, COL_TILE), lambda j, t: (0, j))),
        name="dw_pieces", compiler_params=_params("arbitrary", "arbitrary"))(lhs, *[a for a, _, _ in tiles])


def _dw(lhs, rhs, lhs_by_j=False, relu2_lhs=False):
    T = lhs.shape[0]
    tk = min(1024, T)
    nt = T // tk
    W = D_MODEL
    nj = (lhs.shape[1] if lhs_by_j else rhs.shape[1]) // W

    def kern(lhs_ref, rhs_ref, o_ref, ob_ref):
        t = pl.program_id(1)

        @pl.when(t == 0)
        def _():
            o_ref[...] = jnp.zeros_like(o_ref)

        a = lhs_ref[...]
        if relu2_lhs:
            a = jnp.square(jnp.maximum(a.astype(F32), 0.0)).astype(BF16)
        o_ref[...] += _tn(a, rhs_ref[...].astype(BF16))

        @pl.when(t == nt - 1)
        def _():
            ob_ref[...] = o_ref[...].astype(BF16)

    if lhs_by_j:
        shape, lmap, rmap, omap = (nj * W, W), (lambda j, t: (t, j)), (lambda j, t: (t, 0)), (lambda j, t: (j, 0))
    else:
        shape, lmap, rmap, omap = (W, nj * W), (lambda j, t: (t, 0)), (lambda j, t: (t, j)), (lambda j, t: (0, j))
    return pl.pallas_call(
        kern, out_shape=(jax.ShapeDtypeStruct(shape, F32), jax.ShapeDtypeStruct(shape, BF16)), grid=(nj, nt),
        in_specs=[pl.BlockSpec((tk, W), lmap), pl.BlockSpec((tk, W), rmap)],
        out_specs=(pl.BlockSpec((W, W), omap), pl.BlockSpec((W, W), omap)),
        name="dw", compiler_params=_params("arbitrary", "arbitrary"))(lhs, rhs)


BIG = (("w_in", D_MODEL, IN_COLS, "col"), ("w_attn_out", D_MODEL, D_MODEL, "row"), ("w_conv_out", D_MODEL, D_MODEL, "row"),
       ("w_o", D_MODEL, D_MODEL, "row"), ("w_up", D_MODEL, D_FF, "col"), ("w_down", D_FF, D_MODEL, "row"))


def _shard_dims(rows, cols, kind):
    return (rows, cols // N_CHIP) if kind == "col" else (rows // N_CHIP, cols)


def _window(ref, rows, cols, kind, chip, half):
    sr, sc = _shard_dims(rows, cols, kind)
    hr = sr // 2
    if kind == "col":
        return ref.at[pl.ds(half * hr, hr), pl.ds(chip * sc, sc)]
    return ref.at[pl.ds(chip * sr + half * hr, hr), :]


def _mesh_pos():
    x, y, c = lax.axis_index("x"), lax.axis_index("y"), lax.axis_index("c")
    return x, y, c, 2 * x + y


_REL_BITS = (2, 1, 3)


def _rel_dev(x, y, c, r):
    return ((1 - x, y, c), (x, 1 - y, c), (1 - x, 1 - y, c))[r]


def _for_my_chip(j, fn):
    for js in range(N_CHIP):
        pl.when(j == js)(functools.partial(fn, js))


def _all_gather_weights(shards, cw):
    names = [b[0] for b in BIG]
    n_big = len(BIG) * DEPTH
    n_piece = n_big + 1

    def body(*refs):
        s_refs = refs[:len(BIG)]
        cw_ref = refs[len(BIG)]
        o_refs = refs[len(BIG) + 1:len(BIG) + 1 + n_big]
        cwf_ref = refs[len(BIG) + 1 + n_big]
        send_sem, recv_sem, fsend_sem, frecv_sem, loc_sem = refs[len(BIG) + 2 + n_big:]
        x, y, c, j = _mesh_pos()

        def run(js):
            def src_piece(p, half):
                if p == n_big:
                    return cw_ref.at[half]
                t, l = divmod(p, DEPTH)
                _, rows, cols, kind = BIG[t]
                hr = _shard_dims(rows, cols, kind)[0] // 2
                return s_refs[t].at[l, pl.ds(half * hr, hr), :]

            def dst_piece(p, chip, half):
                if p == n_big:
                    return cwf_ref.at[half, chip]
                t, l = divmod(p, DEPTH)
                _, rows, cols, kind = BIG[t]
                return _window(o_refs[p], rows, cols, kind, chip, half)

            local = []
            for p in range(n_piece):
                for half in range(2):
                    cp = pltpu.make_async_copy(src_piece(p, half), dst_piece(p, js, half), loc_sem.at[2 * p + half])
                    cp.start()
                    local.append(cp)
            sends = []
            for r in range(3):
                for p in range(n_piece):
                    cp = pltpu.make_async_remote_copy(src_piece(p, c), dst_piece(p, js, c), send_sem.at[r * n_piece + p],
                                                      recv_sem.at[r * n_piece + p], _rel_dev(x, y, c, r), MESH)
                    cp.start()
                    sends.append(cp)
            fwds = []
            for r in range(3):
                ks = js ^ _REL_BITS[r]
                for p in range(n_piece):
                    got = dst_piece(p, ks, c)
                    pltpu.make_async_remote_copy(got, got, send_sem.at[r * n_piece + p], recv_sem.at[r * n_piece + p],
                                                 _rel_dev(x, y, c, r), MESH).wait_recv()
                    cp = pltpu.make_async_remote_copy(got, got, fsend_sem.at[r * n_piece + p], frecv_sem.at[r * n_piece + p],
                                                      (x, y, 1 - c), MESH)
                    cp.start()
                    fwds.append(cp)
            for r in range(3):
                ks = js ^ _REL_BITS[r]
                for p in range(n_piece):
                    got = dst_piece(p, ks, 1 - c)
                    pltpu.make_async_remote_copy(got, got, fsend_sem.at[r * n_piece + p], frecv_sem.at[r * n_piece + p],
                                                 (x, y, 1 - c), MESH).wait_recv()
            for cp in sends + fwds:
                cp.wait_send()
            for cp in local:
                cp.wait()

        _for_my_chip(j, run)

    out_shape = [jax.ShapeDtypeStruct((rows, cols), BF16) for _, rows, cols, _ in BIG for _ in range(DEPTH)]
    out_shape.append(jax.ShapeDtypeStruct((DEPTH, N_CHIP, 3, D_MODEL // N_CHIP), F32))
    outs = pl.pallas_call(
        body, out_shape=out_shape, in_specs=[ANY] * (len(BIG) + 1), out_specs=[ANY] * (n_big + 1),
        scratch_shapes=[pltpu.SemaphoreType.DMA((3 * n_piece,))] * 4 + [pltpu.SemaphoreType.DMA((2 * n_piece,))],
        name="all_gather_weights")(*[shards[n] for n in names], cw)
    full = {(names[p // DEPTH], p % DEPTH): outs[p] for p in range(n_big)}
    return full, outs[n_big]


def _sibling_exchange(gb):
    n = len(gb)

    def body(*refs):
        g_refs, o_refs = refs[:n], refs[n:2 * n]
        send_sem, recv_sem = refs[2 * n:]
        x, y, c, _ = _mesh_pos()
        cps = []
        for t in range(n):
            _, rows, cols, kind = gb[t]
            for chip in range(N_CHIP):
                out_w = _window(g_refs[t], rows, cols, kind, chip, 1 - c)
                dst_w = _window(o_refs[t], rows, cols, kind, chip, 1 - c)
                cp = pltpu.make_async_remote_copy(out_w, dst_w, send_sem.at[N_CHIP * t + chip], recv_sem.at[N_CHIP * t + chip],
                                                  (x, y, 1 - c), MESH)
                cp.start()
                cps.append(cp)
        for t in range(n):
            _, rows, cols, kind = gb[t]
            for chip in range(N_CHIP):
                w = _window(o_refs[t], rows, cols, kind, chip, c)
                pltpu.make_async_remote_copy(w, w, send_sem.at[N_CHIP * t + chip], recv_sem.at[N_CHIP * t + chip],
                                             (x, y, 1 - c), MESH).wait_recv()
        for cp in cps:
            cp.wait_send()

    return pl.pallas_call(
        body, out_shape=[jax.ShapeDtypeStruct(a.shape, BF16) for a, _, _, _ in gb], in_specs=[ANY] * n, out_specs=[ANY] * n,
        scratch_shapes=[pltpu.SemaphoreType.DMA((N_CHIP * n,))] * 2, name="grad_sibling_exchange")(*[a for a, _, _, _ in gb])


def _half_add(c_arr, g, sib, rows, cols, kind):
    sr, sc = _shard_dims(rows, cols, kind)
    hr = sr // 2

    def kern(c_ref, g_ref, s_ref, ob_ref, of_ref):
        v = g_ref[...] + s_ref[...].astype(F32)
        of_ref[...] = v
        ob_ref[...] = v.astype(BF16)

    if kind == "col":
        imap = lambda j, c_ref: (c_ref[0], j)
    else:
        imap = lambda j, c_ref: (2 * j + c_ref[0], 0)
    gs = pltpu.PrefetchScalarGridSpec(
        num_scalar_prefetch=1, grid=(N_CHIP,),
        in_specs=[pl.BlockSpec((hr, sc), imap), pl.BlockSpec((hr, sc), imap)],
        out_specs=[pl.BlockSpec((None, hr, sc), lambda j, c_ref: (j, 0, 0))] * 2)
    return pl.pallas_call(
        kern, out_shape=(jax.ShapeDtypeStruct((N_CHIP, hr, sc), BF16), jax.ShapeDtypeStruct((N_CHIP, hr, sc), F32)),
        grid_spec=gs, name="grad_half_add", compiler_params=_params("arbitrary"))(c_arr, g, sib)


def _chip_exchange(sbs):
    n = len(sbs)

    def body(*refs):
        s_refs, o_refs = refs[:n], refs[n:2 * n]
        send_sem, recv_sem = refs[2 * n:]
        x, y, c, j = _mesh_pos()
        cps = []
        for r in range(3):
            k = j ^ _REL_BITS[r]
            for t in range(n):
                cp = pltpu.make_async_remote_copy(s_refs[t].at[k], o_refs[t].at[r], send_sem.at[n * r + t], recv_sem.at[n * r + t],
                                                  _rel_dev(x, y, c, r), MESH)
                cp.start()
                cps.append(cp)
        for cp in cps:
            cp.wait()

    return pl.pallas_call(
        body, out_shape=[jax.ShapeDtypeStruct((3,) + a.shape[1:], BF16) for a in sbs], in_specs=[ANY] * n, out_specs=[ANY] * n,
        scratch_shapes=[pltpu.SemaphoreType.DMA((3 * n,))] * 2, name="grad_chip_exchange")(*sbs)


def _owner_sum(j_arr, sf, rb):
    _, hr, sc = sf.shape

    def kern(j_ref, s_ref, r0_ref, r1_ref, r2_ref, o_ref):
        o_ref[...] = ((s_ref[...] + r0_ref[...].astype(F32)) + r1_ref[...].astype(F32)) + r2_ref[...].astype(F32)

    gs = pltpu.PrefetchScalarGridSpec(
        num_scalar_prefetch=1, grid=(1,),
        in_specs=[pl.BlockSpec((None, hr, sc), lambda i, j_ref: (j_ref[0], 0, 0))]
        + [pl.BlockSpec((None, hr, sc), lambda i, j_ref, r=r: (r, 0, 0)) for r in range(3)],
        out_specs=pl.BlockSpec((hr, sc), lambda i, j_ref: (0, 0)))
    return pl.pallas_call(kern, out_shape=jax.ShapeDtypeStruct((hr, sc), F32), grid_spec=gs, name="grad_owner_sum",
                          compiler_params=_params("arbitrary"))(j_arr, sf, rb, rb, rb)


def _sibling_assemble(halves):
    names = list(halves)
    flat = [halves[n][l] for n in names for l in range(DEPTH)]
    n = len(flat)

    def body(*refs):
        h_refs, o_refs = refs[:n], refs[n:n + len(names)]
        send_sem, recv_sem, loc_sem = refs[n + len(names):]
        x, y, c, _ = _mesh_pos()
        cps, loc = [], []
        for q in range(n):
            hr = flat[q].shape[0]
            o_ref = o_refs[q // DEPTH]
            mine = o_ref.at[q % DEPTH, pl.ds(c * hr, hr), :]
            lc = pltpu.make_async_copy(h_refs[q], mine, loc_sem.at[q])
            lc.start()
            loc.append(lc)
            cp = pltpu.make_async_remote_copy(h_refs[q], mine, send_sem.at[q], recv_sem.at[q], (x, y, 1 - c), MESH)
            cp.start()
            cps.append(cp)
        for q in range(n):
            hr = flat[q].shape[0]
            theirs = o_refs[q // DEPTH].at[q % DEPTH, pl.ds((1 - c) * hr, hr), :]
            pltpu.make_async_remote_copy(theirs, theirs, send_sem.at[q], recv_sem.at[q], (x, y, 1 - c), MESH).wait_recv()
        for cp in cps:
            cp.wait_send()
        for lc in loc:
            lc.wait()

    outs = pl.pallas_call(
        body, out_shape=[jax.ShapeDtypeStruct((DEPTH, 2 * halves[nm][0].shape[0], halves[nm][0].shape[1]), F32) for nm in names],
        in_specs=[ANY] * n, out_specs=[ANY] * len(names),
        scratch_shapes=[pltpu.SemaphoreType.DMA((n,))] * 3, name="grad_sibling_assemble")(*flat)
    return dict(zip(names, outs))


def _adamw_math(w, g, m, v):
    m = ADAM_B1 * m + (1.0 - ADAM_B1) * g
    v = ADAM_B2 * v + (1.0 - ADAM_B2) * jnp.square(g)
    m_hat = m / (1.0 - ADAM_B1 ** ADAM_STEP)
    v_hat = v / (1.0 - ADAM_B2 ** ADAM_STEP)
    delta = -ADAM_LR * (m_hat / (jnp.sqrt(v_hat) + ADAM_EPS) + ADAM_WD * w)
    return delta, m, v


def _adamw(w, g, m, v):
    shape = w.shape
    C = shape[-1]
    R = int(np.prod(shape[:-1]))
    tr = min(256, R)
    args = [a.reshape(R, C) for a in (w, g, m, v)]

    def kern(w_ref, g_ref, m_ref, v_ref, d_ref, nm_ref, nv_ref):
        d, nm, nv = _adamw_math(w_ref[...], g_ref[...], m_ref[...], v_ref[...])
        d_ref[...] = d
        nm_ref[...] = nm
        nv_ref[...] = nv

    spec = pl.BlockSpec((tr, C), lambda i: (i, 0))
    outs = pl.pallas_call(
        kern, out_shape=[jax.ShapeDtypeStruct((R, C), F32)] * 3, grid=(R // tr,), in_specs=[spec] * 4, out_specs=[spec] * 3,
        name="adamw", compiler_params=_params("parallel"))(*args)
    return [o.reshape(shape) for o in outs]


_ROW_G_MIX, _ROW_B_GATES, _ROW_SINKS, _ROW_CONV, _ROW_G_MLP, _ROW_G_FINAL, _ROW_LOSS = 0, 2, 6, 8, 16, 18, 19


def _small_step(parts, params, moms, vels):
    names = ["g_mix", "b_gates", "sinks", "conv_w", "conv_b", "g_mlp", "g_final"]
    D = D_MODEL
    QW = D // N_CHIP
    n_dev = 8

    def body(*refs):
        it = iter(refs)
        dgmix = [next(it) for _ in range(DEPTH)]
        dbg = [next(it) for _ in range(DEPTH)]
        dsk = [next(it) for _ in range(DEPTH)]
        dwb = [next(it) for _ in range(DEPTH)]
        dgmlp = [next(it) for _ in range(DEPTH)]
        lst = next(it)
        p_refs = {n: next(it) for n in names}
        m_refs = {n: next(it) for n in names}
        v_refs = {n: next(it) for n in names}
        loss_ref = next(it)
        outs = {n: [next(it) for _ in range(4)] for n in names}
        pack_ref, all_ref, send_sem, recv_sem = next(it), next(it), next(it), next(it)

        x, y, c, j = _mesh_pos()
        me = 4 * x + 2 * y + c
        pack_ref[...] = jnp.zeros_like(pack_ref)
        for l in range(DEPTH):
            pack_ref[_ROW_G_MIX + l:_ROW_G_MIX + l + 1, :] = dgmix[l][...]
            pack_ref[_ROW_B_GATES + 2 * l:_ROW_B_GATES + 2 * l + 1, :] = dbg[l][:, 0:D]
            pack_ref[_ROW_B_GATES + 2 * l + 1:_ROW_B_GATES + 2 * l + 2, :] = dbg[l][:, D:2 * D]
            pack_ref[_ROW_SINKS + l:_ROW_SINKS + l + 1, 0:128] = dsk[l][0:1, :]
            pack_ref[_ROW_CONV + 4 * l:_ROW_CONV + 4 * l + 4, :] = dwb[l][0:4, :]
            pack_ref[_ROW_G_MLP + l:_ROW_G_MLP + l + 1, :] = dgmlp[l][...]
        pack_ref[_ROW_G_FINAL:_ROW_G_FINAL + 1, :] = lst[0:1, :]
        pack_ref[_ROW_LOSS:_ROW_LOSS + 1, :] = lst[1:2, :]

        all_ref[me] = pack_ref[...]
        cps = []
        for k in range(1, n_dev):
            dx_, dy_, dc_ = (k >> 2) & 1, (k >> 1) & 1, k & 1
            peer = (x ^ dx_, y ^ dy_, c ^ dc_)
            cp = pltpu.make_async_remote_copy(pack_ref, all_ref.at[me], send_sem.at[k - 1], recv_sem.at[k - 1], peer, MESH)
            cp.start()
            cps.append(cp)
        for cp in cps:
            cp.wait()

        tot = all_ref[0]
        for d in range(1, n_dev):
            tot = tot + all_ref[d]
        pack_ref[...] = tot

        loss_ref[...] = pack_ref[_ROW_LOSS:_ROW_LOSS + 1, 0:1]

        def finish(name, idx, g):
            w, m, v = p_refs[name][idx], m_refs[name][idx], v_refs[name][idx]
            d, nm, nv = _adamw_math(w, g, m, v)
            for ref, val in zip(outs[name], (g, d, nm, nv)):
                ref[idx] = val

        for l in range(DEPTH):
            finish("g_mix", (slice(l, l + 1), slice(None)), pack_ref[_ROW_G_MIX + l:_ROW_G_MIX + l + 1, :])
            finish("g_mlp", (slice(l, l + 1), slice(None)), pack_ref[_ROW_G_MLP + l:_ROW_G_MLP + l + 1, :])
            finish("conv_b", (slice(l, l + 1), slice(None)), pack_ref[_ROW_CONV + 4 * l + 3:_ROW_CONV + 4 * l + 4, :])
            finish("sinks", (slice(l, l + 1), slice(None)), pack_ref[_ROW_SINKS + l:_ROW_SINKS + l + 1, 0:N_Q_HEADS])
            for hf in range(2):
                finish("b_gates", (slice(l, l + 1), slice(hf * D, (hf + 1) * D)),
                       pack_ref[_ROW_B_GATES + 2 * l + hf:_ROW_B_GATES + 2 * l + hf + 1, :])
        finish("g_final", (slice(0, 1), slice(None)), pack_ref[_ROW_G_FINAL:_ROW_G_FINAL + 1, :])

        def conv_w_chip(js):
            for l in range(DEPTH):
                for k in range(3):
                    row = _ROW_CONV + 4 * l + k
                    finish("conv_w", (l, slice(k, k + 1), slice(None)), pack_ref[row:row + 1, js * QW:(js + 1) * QW])

        _for_my_chip(j, conv_w_chip)

    vm = pl.BlockSpec(memory_space=pltpu.VMEM)
    ins = (parts["g_mix"] + parts["b_gates"] + parts["sinks"] + parts["conv"] + parts["g_mlp"] + [parts["loss"]]
           + [params[n] for n in names] + [moms[n] for n in names] + [vels[n] for n in names])
    out_shape = [jax.ShapeDtypeStruct((1, 1), F32)]
    for n in names:
        out_shape += [jax.ShapeDtypeStruct(params[n].shape, F32)] * 4
    res = pl.pallas_call(
        body, out_shape=out_shape, in_specs=[vm] * len(ins), out_specs=[vm] * len(out_shape),
        scratch_shapes=[pltpu.VMEM((SMALL_ROWS, D), F32), pltpu.VMEM((n_dev, SMALL_ROWS, D), F32),
                        pltpu.SemaphoreType.DMA((n_dev - 1,)), pltpu.SemaphoreType.DMA((n_dev - 1,))],
        name="small_allreduce_adamw")(*ins)
    loss = res[0]
    out = {n: res[1 + 4 * i:5 + 4 * i] for i, n in enumerate(names)}
    return loss, out


def kernel(x, g_mix, w_in, b_gates, sinks, w_attn_out, conv_w, conv_b, w_conv_out, w_o, g_mlp, w_up, w_down, g_final, loss_target, m_g_mix, m_w_in, m_b_gates, m_sinks, m_w_attn_out, m_conv_w, m_conv_b, m_w_conv_out, m_w_o, m_g_mlp, m_w_up, m_w_down, m_g_final, v_g_mix, v_w_in, v_b_gates, v_sinks, v_w_attn_out, v_conv_w, v_conv_b, v_w_conv_out, v_w_o, v_g_mlp, v_w_up, v_w_down, v_g_final):
    B, S, D = x.shape
    T = B * S
    big_w = dict(w_in=w_in, w_attn_out=w_attn_out, w_conv_out=w_conv_out, w_o=w_o, w_up=w_up, w_down=w_down)
    big_m = dict(w_in=m_w_in, w_attn_out=m_w_attn_out, w_conv_out=m_w_conv_out, w_o=m_w_o, w_up=m_w_up, w_down=m_w_down)
    big_v = dict(w_in=v_w_in, w_attn_out=v_w_attn_out, w_conv_out=v_w_conv_out, w_o=v_w_o, w_up=v_w_up, w_down=v_w_down)

    full, cw_full = _all_gather_weights({n: a.astype(BF16) for n, a in big_w.items()}, conv_w)
    conv_w_full = jnp.transpose(cw_full, (0, 2, 1, 3)).reshape(DEPTH, 3, D)

    xs = [x.reshape(T, D)]
    saved = []
    for l in range(DEPTH):
        W = {n: full[(n, l)] for n in big_w}
        h = _rms_fwd(xs[-1], g_mix[l:l + 1])
        pqkv = _proj(h, W["w_in"], 0, QKV_W, QKV_W)
        pconv = _proj(h, W["w_in"], QKV_W, CONV_W, QKV_W)
        pgate = _proj(h, W["w_in"], QKV_W + CONV_W, GATE_W, COL_TILE)
        att, lse = _attn_fwd(pqkv, sinks[l], S)
        cv = _conv_fwd(pconv, conv_w_full[l], conv_b[l:l + 1], S)
        x1, ya, yc, mg = _mix_fwd(xs[-1], att, cv, pgate, b_gates[l:l + 1], W["w_attn_out"], W["w_conv_out"], W["w_o"])
        x2, a = _mlp_fwd(x1, g_mlp[l:l + 1], W["w_up"], W["w_down"])
        saved.append(dict(h=h, pqkv=pqkv, pconv=pconv, pgate=pgate, att=att, lse=lse, cv=cv, x1=x1, ya=ya, yc=yc, mg=mg, a=a))
        xs.append(x2)

    loss_stats, dx = _loss_bwd(xs[-1], g_final.reshape(1, D), loss_target.reshape(T, D))

    parts = dict(g_mix=[None] * DEPTH, b_gates=[None] * DEPTH, sinks=[None] * DEPTH, conv=[None] * DEPTH,
                 g_mlp=[None] * DEPTH, loss=loss_stats)
    gf, gb = {}, {}
    for l in reversed(range(DEPTH)):
        W = {n: full[(n, l)] for n in big_w}
        sv = saved[l]
        dx1, da, h2, parts["g_mlp"][l] = _mlp_bwd(dx, sv["x1"], sv["a"], g_mlp[l:l + 1], W["w_up"], W["w_down"])
        gf["w_up", l], gb["w_up", l] = _dw(h2, da)
        gf["w_down", l], gb["w_down", l] = _dw(sv["a"], dx, lhs_by_j=True, relu2_lhs=True)
        datt, dcv, dya, dyc, dgate, parts["b_gates"][l] = _mix_bwd(
            dx1, sv["ya"], sv["yc"], sv["pgate"], b_gates[l:l + 1], W["w_attn_out"], W["w_conv_out"], W["w_o"])
        gf["w_o", l], gb["w_o", l] = _dw(sv["mg"], dx1)
        gf["w_attn_out", l], gb["w_attn_out", l] = _dw(sv["att"], dya)
        gf["w_conv_out", l], gb["w_conv_out", l] = _dw(sv["cv"], dyc)
        dconv, parts["conv"][l] = _conv_bwd(dcv, sv["pconv"], conv_w_full[l], conv_b[l:l + 1], S)
        dq, dkv, parts["sinks"][l] = _attn_bwd(sv["pqkv"], sv["att"], datt, sv["lse"], sinks[l], S)
        pieces = [(dq, D), (dkv, QKV_W - D), (dconv, CONV_W), (dgate, GATE_W)]
        gf["w_in", l], gb["w_in", l] = _dw_pieces(sv["h"], pieces)
        dx, parts["g_mix"][l] = _inproj_bwd(pieces, W["w_in"], xs[l], dx1, g_mix[l:l + 1])

    c_arr = lax.axis_index("c").astype(jnp.int32).reshape(1)
    j_arr = (2 * lax.axis_index("x") + lax.axis_index("y")).astype(jnp.int32).reshape(1)
    order = [(n, l) for n, _, _, _ in BIG for l in range(DEPTH)]
    dims = {n: (r, c_, k) for n, r, c_, k in BIG}
    sib = _sibling_exchange([(gb[key],) + dims[key[0]] for key in order])
    pre = [_half_add(c_arr, gf[key], s, *dims[key[0]]) for key, s in zip(order, sib)]
    got = _chip_exchange([p[0] for p in pre])
    halves = {}
    for key, p, r in zip(order, pre, got):
        halves.setdefault(key[0], []).append(_owner_sum(j_arr, p[1], r))
    grads = _sibling_assemble(halves)

    res = {}
    for n in big_w:
        d, nm, nv = _adamw(big_w[n], grads[n], big_m[n], big_v[n])
        res[n] = (grads[n], d, nm, nv)

    small_p = dict(g_mix=g_mix, b_gates=b_gates, sinks=sinks, conv_w=conv_w, conv_b=conv_b, g_mlp=g_mlp, g_final=g_final.reshape(1, D))
    small_m = dict(g_mix=m_g_mix, b_gates=m_b_gates, sinks=m_sinks, conv_w=m_conv_w, conv_b=m_conv_b, g_mlp=m_g_mlp,
                   g_final=m_g_final.reshape(1, D))
    small_v = dict(g_mix=v_g_mix, b_gates=v_b_gates, sinks=v_sinks, conv_w=v_conv_w, conv_b=v_conv_b, g_mlp=v_g_mlp,
                   g_final=v_g_final.reshape(1, D))
    loss, small = _small_step(parts, small_p, small_m, small_v)
    for n, vals in small.items():
        res[n] = tuple(v.reshape(D) for v in vals) if n == "g_final" else tuple(vals)

    weights = ["g_mix", "w_in", "b_gates", "sinks", "w_attn_out", "conv_w", "conv_b", "w_conv_out", "w_o", "g_mlp", "w_up",
               "w_down", "g_final"]
    out = [loss.reshape(()), dx.reshape(B, S, D)]
    for k in range(4):
        out += [res[n][k] for n in weights]
    return tuple(out)
```

```python
import functools

import numpy as np
import jax
import jax.numpy as jnp
from jax import lax
from jax.experimental import pallas as pl
from jax.experimental.pallas import tpu as pltpu

F32 = jnp.float32
BF16 = jnp.bfloat16

D_MODEL = 1024
HEAD_DIM = 64
N_Q_HEADS = 16
N_KV_HEADS = 4
GQA_GROUP = 4
WINDOW = 128
D_FF = 4096
DEPTH = 2
RMS_EPS = 1e-6
NEG_INF = -1e30
ATTN_SCALE = HEAD_DIM ** -0.5
QKV_W = 1536
CONV_W = 3072
GATE_W = 2048
IN_COLS = QKV_W + CONV_W + GATE_W
COL_TILE = 512
N_CHIP = 4
ADAM_LR = 0.001
ADAM_B1 = 0.9
ADAM_B2 = 0.999
ADAM_EPS = 1e-08
ADAM_WD = 0.01
ADAM_STEP = 10
V7X_VMEM_BYTES = 64 * 2 ** 20
VMEM_LIMIT = V7X_VMEM_BYTES - 8 * 2 ** 20
MESH = pl.DeviceIdType.MESH
ANY = pl.BlockSpec(memory_space=pl.ANY)
SMALL_ROWS = 24

_SLOPES = [float(v) for v in np.power(np.float32(2.0), -8.0 * np.arange(1, N_Q_HEADS + 1, dtype=np.float32) / N_Q_HEADS)]


def _params(*sem):
    return pltpu.CompilerParams(dimension_semantics=sem, vmem_limit_bytes=VMEM_LIMIT)


def _nt(a, b):
    return lax.dot_general(a, b, (((1,), (1,)), ((), ())), preferred_element_type=F32)


def _tn(a, b):
    return lax.dot_general(a, b, (((0,), (0,)), ((), ())), preferred_element_type=F32)


def _nn(a, b):
    return jnp.dot(a, b, preferred_element_type=F32)


def _rms_stats(xf):
    r = lax.rsqrt(jnp.mean(xf * xf, axis=-1, keepdims=True) + RMS_EPS)
    return r, xf * r


def _rms_bwd(dh, xh, r, g):
    dxh = dh * g
    dx = r * (dxh - xh * jnp.mean(dxh * xh, axis=-1, keepdims=True))
    dg = jnp.sum(dh * xh, axis=0, keepdims=True)
    return dx, dg


def _rms_fwd(x, g):
    T, D = x.shape
    tm = min(512, T)

    def kern(x_ref, g_ref, h_ref):
        _, xh = _rms_stats(x_ref[...])
        h_ref[...] = (xh * g_ref[...]).astype(BF16)

    return pl.pallas_call(
        kern, out_shape=jax.ShapeDtypeStruct((T, D), BF16), grid=(T // tm,),
        in_specs=[pl.BlockSpec((tm, D), lambda i: (i, 0)), pl.BlockSpec((1, D), lambda i: (0, 0))],
        out_specs=pl.BlockSpec((tm, D), lambda i: (i, 0)),
        name="rms_fwd", compiler_params=_params("parallel"))(x, g)


def _proj(h, w, col_off, ncols, tn):
    T, K = h.shape
    tm = min(512, T)
    off = col_off // tn

    def kern(a_ref, w_ref, o_ref):
        o_ref[...] = _nn(a_ref[...], w_ref[...]).astype(BF16)

    return pl.pallas_call(
        kern, out_shape=jax.ShapeDtypeStruct((T, ncols), BF16), grid=(T // tm, ncols // tn),
        in_specs=[pl.BlockSpec((tm, K), lambda i, j: (i, 0)), pl.BlockSpec((K, tn), lambda i, j: (0, off + j))],
        out_specs=pl.BlockSpec((tm, tn), lambda i, j: (i, j)),
        name="proj", compiler_params=_params("parallel", "arbitrary"))(h, w)


def _band_masks(first_i):
    r = lax.broadcasted_iota(jnp.int32, (GQA_GROUP * WINDOW, 2 * WINDOW), 0)
    jj = lax.broadcasted_iota(jnp.int32, (GQA_GROUP * WINDOW, 2 * WINDOW), 1)
    dist = WINDOW + (r & (WINDOW - 1)) - jj
    valid = (dist >= 0) & (dist < WINDOW) & ((jj + (1 - first_i) * WINDOW) >= WINDOW)
    return dist.astype(F32), valid


def _per_head_col(vals):
    rb = lax.broadcasted_iota(jnp.int32, (GQA_GROUP * WINDOW, 1), 0) >> 7
    col = jnp.full((GQA_GROUP * WINDOW, 1), vals[3], F32)
    for g in (2, 1, 0):
        col = jnp.where(rb == g, vals[g], col)
    return col


def _stack_heads(ref, base, hk):
    return jnp.concatenate(
        [ref[:, base + HEAD_DIM * (GQA_GROUP * hk + g): base + HEAD_DIM * (GQA_GROUP * hk + g + 1)] for g in range(GQA_GROUP)],
        axis=0)


def _attn_fwd(pqkv, sinks, seq):
    T = pqkv.shape[0]
    nblk = seq // WINDOW
    K0, V0 = N_Q_HEADS * HEAD_DIM, N_Q_HEADS * HEAD_DIM + N_KV_HEADS * HEAD_DIM

    def kern(sink_ref, cur_ref, prev_ref, o_ref, lse_ref):
        i = pl.program_id(0)
        first_i = ((i % nblk) == 0).astype(jnp.int32)
        distf, valid = _band_masks(first_i)
        outs = []
        for hk in range(N_KV_HEADS):
            k_band = jnp.concatenate([prev_ref[:, HEAD_DIM * hk:HEAD_DIM * (hk + 1)],
                                      cur_ref[:, K0 + HEAD_DIM * hk:K0 + HEAD_DIM * (hk + 1)]], axis=0)
            v_band = jnp.concatenate([prev_ref[:, 256 + HEAD_DIM * hk:256 + HEAD_DIM * (hk + 1)],
                                      cur_ref[:, V0 + HEAD_DIM * hk:V0 + HEAD_DIM * (hk + 1)]], axis=0)
            q_g = _stack_heads(cur_ref, 0, hk)
            slope = _per_head_col([_SLOPES[GQA_GROUP * hk + g] for g in range(GQA_GROUP)])
            sink = _per_head_col([sink_ref[GQA_GROUP * hk + g] for g in range(GQA_GROUP)])
            s = _nt(q_g, k_band) * ATTN_SCALE + (-slope) * distf
            s = jnp.where(valid, s, NEG_INF)
            m = jnp.maximum(jnp.max(s, axis=-1, keepdims=True), sink)
            p = jnp.exp(s - m)
            denom = jnp.sum(p, axis=-1, keepdims=True) + jnp.exp(sink - m)
            o_g = _nn((p / denom).astype(BF16), v_band)
            lse_g = m + jnp.log(denom)
            for g in range(GQA_GROUP):
                h = GQA_GROUP * hk + g
                outs.append(o_g[WINDOW * g:WINDOW * (g + 1)].astype(BF16))
                lse_ref[:, h:h + 1] = lse_g[WINDOW * g:WINDOW * (g + 1)]
        o_ref[...] = jnp.concatenate(outs, axis=1)

    return pl.pallas_call(
        kern,
        out_shape=(jax.ShapeDtypeStruct((T, D_MODEL), BF16), jax.ShapeDtypeStruct((T, N_Q_HEADS), F32)),
        grid=(T // WINDOW,),
        in_specs=[pl.BlockSpec(memory_space=pltpu.SMEM),
                  pl.BlockSpec((WINDOW, QKV_W), lambda i: (i, 0)),
                  pl.BlockSpec((WINDOW, 512), lambda i: (jnp.maximum(i - 1, 0), 2))],
        out_specs=(pl.BlockSpec((WINDOW, D_MODEL), lambda i: (i, 0)), pl.BlockSpec((WINDOW, N_Q_HEADS), lambda i: (i, 0))),
        name="attn_fwd", compiler_params=_params("parallel"))(sinks, pqkv, pqkv)


def _pick_row(a, row):
    rid = lax.broadcasted_iota(jnp.int32, a.shape, 0)
    return jnp.sum(jnp.where(rid == row, a, 0.0), axis=0, keepdims=True)


def _conv_taps(yc, halo_yc, first_i):
    keep = (1 - first_i).astype(F32)
    p1 = _pick_row(halo_yc, 15) * keep
    p2 = _pick_row(halo_yc, 14) * keep
    rowid = lax.broadcasted_iota(jnp.int32, yc.shape, 0)
    s1 = jnp.where(rowid == 0, p1, pltpu.roll(yc, 1, 0))
    s2 = jnp.where(rowid == 0, p2, jnp.where(rowid == 1, p1, pltpu.roll(yc, 2, 0)))
    return s1, s2


def _conv_fwd(pconv, conv_w, conv_b, seq):
    T = pconv.shape[0]
    tm = min(256, seq)
    per_seq = seq // tm
    D = D_MODEL

    def kern(cur_ref, halo_ref, w_ref, b_ref, o_ref):
        i = pl.program_id(0)
        first_i = ((i % per_seq) == 0).astype(jnp.int32)
        cb = cur_ref[:, 0:D].astype(F32)
        yc = cur_ref[:, D:2 * D].astype(F32) * cur_ref[:, 2 * D:3 * D].astype(F32)
        halo_yc = halo_ref[:, D:2 * D].astype(F32) * halo_ref[:, 2 * D:3 * D].astype(F32)
        s1, s2 = _conv_taps(yc, halo_yc, first_i)
        z = w_ref[0:1, :] * s2 + w_ref[1:2, :] * s1 + w_ref[2:3, :] * yc
        o_ref[...] = (cb * (z + b_ref[...])).astype(BF16)

    return pl.pallas_call(
        kern, out_shape=jax.ShapeDtypeStruct((T, D), BF16), grid=(T // tm,),
        in_specs=[pl.BlockSpec((tm, CONV_W), lambda i: (i, 0)),
                  pl.BlockSpec((16, CONV_W), lambda i: (jnp.maximum(i * (tm // 16) - 1, 0), 0)),
                  pl.BlockSpec((3, D), lambda i: (0, 0)), pl.BlockSpec((1, D), lambda i: (0, 0))],
        out_specs=pl.BlockSpec((tm, D), lambda i: (i, 0)),
        name="conv_fwd", compiler_params=_params("parallel"))(pconv, pconv, conv_w, conv_b)


def _mix_fwd(x, att, cv, pgate, b_gates, wao, wco, wo):
    T, D = x.shape
    tm = min(512, T)

    def kern(x_ref, att_ref, cv_ref, pg_ref, bg_ref, wao_ref, wco_ref, wo_ref, x1_ref, ya_ref, yc_ref, mg_ref):
        ya = _nn(att_ref[...], wao_ref[...])
        yc = _nn(cv_ref[...], wco_ref[...])
        sa = jax.nn.sigmoid(pg_ref[:, 0:D].astype(F32) + bg_ref[:, 0:D])
        sc = jax.nn.sigmoid(pg_ref[:, D:2 * D].astype(F32) + bg_ref[:, D:2 * D])
        mg = (sa * ya + sc * yc).astype(BF16)
        ya_ref[...] = ya.astype(BF16)
        yc_ref[...] = yc.astype(BF16)
        mg_ref[...] = mg
        x1_ref[...] = x_ref[...] + _nn(mg, wo_ref[...])

    row = lambda w: pl.BlockSpec((tm, w), lambda i: (i, 0))
    full = lambda a, b: pl.BlockSpec((a, b), lambda i: (0, 0))
    bf = jax.ShapeDtypeStruct((T, D), BF16)
    return pl.pallas_call(
        kern, out_shape=(jax.ShapeDtypeStruct((T, D), F32), bf, bf, bf), grid=(T // tm,),
        in_specs=[row(D), row(D), row(D), row(GATE_W), full(1, GATE_W), full(D, D), full(D, D), full(D, D)],
        out_specs=(row(D), row(D), row(D), row(D)),
        name="mix_fwd", compiler_params=_params("parallel"))(x, att, cv, pgate, b_gates, wao, wco, wo)


def _mlp_fwd(x1, g, wup, wdn):
    T, D = x1.shape
    tm = min(512, T)
    nj = D_FF // D

    def kern(x_ref, g_ref, wup_ref, wdn_ref, x2_ref, a_ref, h_scr, acc_scr):
        j = pl.program_id(1)

        @pl.when(j == 0)
        def _():
            xf = x_ref[...]
            _, xh = _rms_stats(xf)
            h_scr[...] = (xh * g_ref[...]).astype(BF16)
            acc_scr[...] = xf

        a = _nn(h_scr[...], wup_ref[...])
        a_ref[...] = a.astype(BF16)
        u = jnp.square(jnp.maximum(a, 0.0)).astype(BF16)
        acc_scr[...] += _nn(u, wdn_ref[...])

        @pl.when(j == nj - 1)
        def _():
            x2_ref[...] = acc_scr[...]

    return pl.pallas_call(
        kern, out_shape=(jax.ShapeDtypeStruct((T, D), F32), jax.ShapeDtypeStruct((T, D_FF), BF16)), grid=(T // tm, nj),
        in_specs=[pl.BlockSpec((tm, D), lambda i, j: (i, 0)), pl.BlockSpec((1, D), lambda i, j: (0, 0)),
                  pl.BlockSpec((D, D), lambda i, j: (0, j)), pl.BlockSpec((D, D), lambda i, j: (j, 0))],
        out_specs=(pl.BlockSpec((tm, D), lambda i, j: (i, 0)), pl.BlockSpec((tm, D), lambda i, j: (i, j))),
        scratch_shapes=[pltpu.VMEM((tm, D), BF16), pltpu.VMEM((tm, D), F32)],
        name="mlp_fwd", compiler_params=_params("parallel", "arbitrary"))(x1, g, wup, wdn)


def _loss_bwd(x, g, tgt):
    T, D = x.shape
    tm = min(512, T)

    def kern(x_ref, g_ref, t_ref, st_ref, dx_ref):
        i = pl.program_id(0)

        @pl.when(i == 0)
        def _():
            st_ref[...] = jnp.zeros_like(st_ref)

        gg = g_ref[...]
        r, xh = _rms_stats(x_ref[...])
        e = xh * gg - t_ref[...]
        part = 0.5 * jnp.sum(jnp.mean(e * e, axis=-1, keepdims=True), axis=0, keepdims=True)
        dx, dg = _rms_bwd(e * (1.0 / D), xh, r, gg)
        dx_ref[...] = dx
        st_ref[0:1, :] += dg
        st_ref[1:2, 0:1] += part

    return pl.pallas_call(
        kern, out_shape=(jax.ShapeDtypeStruct((8, D), F32), jax.ShapeDtypeStruct((T, D), F32)), grid=(T // tm,),
        in_specs=[pl.BlockSpec((tm, D), lambda i: (i, 0)), pl.BlockSpec((1, D), lambda i: (0, 0)),
                  pl.BlockSpec((tm, D), lambda i: (i, 0))],
        out_specs=(pl.BlockSpec((8, D), lambda i: (0, 0)), pl.BlockSpec((tm, D), lambda i: (i, 0))),
        name="loss_bwd", compiler_params=_params("arbitrary"))(x, g, tgt)


def _mlp_bwd(dx2, x1, a, g, wup, wdn):
    T, D = x1.shape
    tm = min(512, T)
    nj = D_FF // D

    def kern(dx2_ref, x1_ref, a_ref, g_ref, wup_ref, wdn_ref, dx1_ref, da_ref, h2_ref, dg_ref, dyb_scr, acc_scr):
        i, j = pl.program_id(0), pl.program_id(1)

        @pl.when((i == 0) & (j == 0))
        def _():
            dg_ref[...] = jnp.zeros_like(dg_ref)

        @pl.when(j == 0)
        def _():
            dyb_scr[...] = dx2_ref[...].astype(BF16)
            acc_scr[...] = jnp.zeros_like(acc_scr)

        du = _nt(dyb_scr[...], wdn_ref[...])
        da = (du * (2.0 * jnp.maximum(a_ref[...].astype(F32), 0.0))).astype(BF16)
        da_ref[...] = da
        acc_scr[...] += _nt(da, wup_ref[...])

        @pl.when(j == nj - 1)
        def _():
            gg = g_ref[...]
            r, xh = _rms_stats(x1_ref[...])
            h2_ref[...] = (xh * gg).astype(BF16)
            dx, dg = _rms_bwd(acc_scr[...], xh, r, gg)
            dx1_ref[...] = dx2_ref[...] + dx
            dg_ref[...] += dg

    return pl.pallas_call(
        kern,
        out_shape=(jax.ShapeDtypeStruct((T, D), F32), jax.ShapeDtypeStruct((T, D_FF), BF16),
                   jax.ShapeDtypeStruct((T, D), BF16), jax.ShapeDtypeStruct((1, D), F32)),
        grid=(T // tm, nj),
        in_specs=[pl.BlockSpec((tm, D), lambda i, j: (i, 0)), pl.BlockSpec((tm, D), lambda i, j: (i, 0)),
                  pl.BlockSpec((tm, D), lambda i, j: (i, j)), pl.BlockSpec((1, D), lambda i, j: (0, 0)),
                  pl.BlockSpec((D, D), lambda i, j: (0, j)), pl.BlockSpec((D, D), lambda i, j: (j, 0))],
        out_specs=(pl.BlockSpec((tm, D), lambda i, j: (i, 0)), pl.BlockSpec((tm, D), lambda i, j: (i, j)),
                   pl.BlockSpec((tm, D), lambda i, j: (i, 0)), pl.BlockSpec((1, D), lambda i, j: (0, 0))),
        scratch_shapes=[pltpu.VMEM((tm, D), BF16), pltpu.VMEM((tm, D), F32)],
        name="mlp_bwd", compiler_params=_params("arbitrary", "arbitrary"))(dx2, x1, a, g, wup, wdn)


def _mix_bwd(dx1, ya, yc, pgate, b_gates, wao, wco, wo):
    T, D = dx1.shape
    tm = min(256, T)

    def kern(dx_ref, ya_ref, yc_ref, pg_ref, bg_ref, wao_ref, wco_ref, wo_ref,
             datt_ref, dcv_ref, dya_ref, dyc_ref, dgt_ref, dbg_ref):
        @pl.when(pl.program_id(0) == 0)
        def _():
            dbg_ref[...] = jnp.zeros_like(dbg_ref)

        dm = _nt(dx_ref[...].astype(BF16), wo_ref[...])
        sa = jax.nn.sigmoid(pg_ref[:, 0:D].astype(F32) + bg_ref[:, 0:D])
        sc = jax.nn.sigmoid(pg_ref[:, D:2 * D].astype(F32) + bg_ref[:, D:2 * D])
        dya = (dm * sa).astype(BF16)
        dyc = (dm * sc).astype(BF16)
        dga = dm * ya_ref[...].astype(F32) * (sa * (1.0 - sa))
        dgc = dm * yc_ref[...].astype(F32) * (sc * (1.0 - sc))
        dya_ref[...] = dya
        dyc_ref[...] = dyc
        dgt_ref[:, 0:D] = dga.astype(BF16)
        dgt_ref[:, D:2 * D] = dgc.astype(BF16)
        dbg_ref[:, 0:D] += jnp.sum(dga, axis=0, keepdims=True)
        dbg_ref[:, D:2 * D] += jnp.sum(dgc, axis=0, keepdims=True)
        datt_ref[...] = _nt(dya, wao_ref[...]).astype(BF16)
        dcv_ref[...] = _nt(dyc, wco_ref[...]).astype(BF16)

    row = lambda w: pl.BlockSpec((tm, w), lambda i: (i, 0))
    full = lambda a, b: pl.BlockSpec((a, b), lambda i: (0, 0))
    bf = jax.ShapeDtypeStruct((T, D), BF16)
    return pl.pallas_call(
        kern,
        out_shape=(bf, bf, bf, bf, jax.ShapeDtypeStruct((T, GATE_W), BF16), jax.ShapeDtypeStruct((1, GATE_W), F32)),
        grid=(T // tm,),
        in_specs=[row(D), row(D), row(D), row(GATE_W), full(1, GATE_W), full(D, D), full(D, D), full(D, D)],
        out_specs=(row(D), row(D), row(D), row(D), row(GATE_W), full(1, GATE_W)),
        name="mix_bwd", compiler_params=_params("arbitrary"))(dx1, ya, yc, pgate, b_gates, wao, wco, wo)


def _conv_bwd(dcv, pconv, conv_w, conv_b, seq):
    T = pconv.shape[0]
    tm = min(256, seq)
    per_seq = seq // tm
    D = D_MODEL
    nb16 = T // 16

    def kern(dcv_ref, dcvn_ref, cur_ref, prev_ref, next_ref, w_ref, b_ref, o_ref, dwb_ref):
        i = pl.program_id(0)

        @pl.when(i == 0)
        def _():
            dwb_ref[...] = jnp.zeros_like(dwb_ref)

        first_i = ((i % per_seq) == 0).astype(jnp.int32)
        keep_next = 1.0 - (((i + 1) % per_seq) == 0).astype(F32)
        cb = cur_ref[:, 0:D].astype(F32)
        cc = cur_ref[:, D:2 * D].astype(F32)
        cu = cur_ref[:, 2 * D:3 * D].astype(F32)
        yc = cc * cu
        halo_yc = prev_ref[:, D:2 * D].astype(F32) * prev_ref[:, 2 * D:3 * D].astype(F32)
        s1, s2 = _conv_taps(yc, halo_yc, first_i)
        w0, w1, w2 = w_ref[0:1, :], w_ref[1:2, :], w_ref[2:3, :]
        z = w0 * s2 + w1 * s1 + w2 * yc
        dcv = dcv_ref[...].astype(F32)
        dz = dcv * cb
        dzn = dcvn_ref[...].astype(F32) * next_ref[:, 0:D].astype(F32) * keep_next
        n1, n2 = _pick_row(dzn, 0), _pick_row(dzn, 1)
        rowid = lax.broadcasted_iota(jnp.int32, dz.shape, 0)
        u1 = jnp.where(rowid == tm - 1, n1, pltpu.roll(dz, tm - 1, 0))
        u2 = jnp.where(rowid == tm - 1, n2, jnp.where(rowid == tm - 2, n1, pltpu.roll(dz, tm - 2, 0)))
        dyc = w2 * dz + w1 * u1 + w0 * u2
        o_ref[:, 0:D] = (dcv * (z + b_ref[...])).astype(BF16)
        o_ref[:, D:2 * D] = (dyc * cu).astype(BF16)
        o_ref[:, 2 * D:3 * D] = (dyc * cc).astype(BF16)
        dwb_ref[0:1, :] += jnp.sum(dz * s2, axis=0, keepdims=True)
        dwb_ref[1:2, :] += jnp.sum(dz * s1, axis=0, keepdims=True)
        dwb_ref[2:3, :] += jnp.sum(dz * yc, axis=0, keepdims=True)
        dwb_ref[3:4, :] += jnp.sum(dz, axis=0, keepdims=True)

    prev_map = lambda i: (jnp.maximum(i * (tm // 16) - 1, 0), 0)
    next_map = lambda i: (jnp.minimum((i + 1) * (tm // 16), nb16 - 1), 0)
    return pl.pallas_call(
        kern, out_shape=(jax.ShapeDtypeStruct((T, CONV_W), BF16), jax.ShapeDtypeStruct((8, D), F32)), grid=(T // tm,),
        in_specs=[pl.BlockSpec((tm, D), lambda i: (i, 0)), pl.BlockSpec((16, D), next_map),
                  pl.BlockSpec((tm, CONV_W), lambda i: (i, 0)), pl.BlockSpec((16, CONV_W), prev_map),
                  pl.BlockSpec((16, CONV_W), next_map),
                  pl.BlockSpec((3, D), lambda i: (0, 0)), pl.BlockSpec((1, D), lambda i: (0, 0))],
        out_specs=(pl.BlockSpec((tm, CONV_W), lambda i: (i, 0)), pl.BlockSpec((8, D), lambda i: (0, 0))),
        name="conv_bwd", compiler_params=_params("arbitrary"))(dcv, dcv, pconv, pconv, pconv, conv_w, conv_b)


def _attn_bwd(pqkv, att, datt, lse, sinks, seq):
    T = pqkv.shape[0]
    nblk = seq // WINDOW
    nseq = T // seq
    K0, V0 = N_Q_HEADS * HEAD_DIM, N_Q_HEADS * HEAD_DIM + N_KV_HEADS * HEAD_DIM
    KVW = N_KV_HEADS * HEAD_DIM

    def kern(sink_ref, cur_ref, prev_ref, o_ref, do_ref, lse_ref, dq_ref, dkv_ref, ds_ref, kc_scr, vc_scr):
        b, st = pl.program_id(0), pl.program_id(1)

        @pl.when((b == 0) & (st == 0))
        def _():
            ds_ref[...] = jnp.zeros_like(ds_ref)

        @pl.when(st == 0)
        def _():
            kc_scr[...] = jnp.zeros_like(kc_scr)
            vc_scr[...] = jnp.zeros_like(vc_scr)

        @pl.when(st < nblk)
        def _():
            first_i = (st == 0).astype(jnp.int32)
            distf, valid = _band_masks(first_i)
            od = o_ref[...].astype(F32) * do_ref[...].astype(F32)
            dqs = []
            for hk in range(N_KV_HEADS):
                k_band = jnp.concatenate([prev_ref[:, HEAD_DIM * hk:HEAD_DIM * (hk + 1)],
                                          cur_ref[:, K0 + HEAD_DIM * hk:K0 + HEAD_DIM * (hk + 1)]], axis=0)
                v_band = jnp.concatenate([prev_ref[:, KVW + HEAD_DIM * hk:KVW + HEAD_DIM * (hk + 1)],
                                          cur_ref[:, V0 + HEAD_DIM * hk:V0 + HEAD_DIM * (hk + 1)]], axis=0)
                q_g = _stack_heads(cur_ref, 0, hk)
                do_g = _stack_heads(do_ref, 0, hk)
                heads = [GQA_GROUP * hk + g for g in range(GQA_GROUP)]
                slope = _per_head_col([_SLOPES[h] for h in heads])
                sink = _per_head_col([sink_ref[h] for h in heads])
                lse_g = jnp.concatenate([lse_ref[:, h:h + 1] for h in heads], axis=0)
                d_g = jnp.concatenate(
                    [jnp.sum(od[:, HEAD_DIM * h:HEAD_DIM * (h + 1)], axis=-1, keepdims=True) for h in heads], axis=0)
                s = _nt(q_g, k_band) * ATTN_SCALE + (-slope) * distf
                p = jnp.where(valid, jnp.exp(s - lse_g), 0.0)
                dp = _nt(do_g, v_band)
                dsb = (p * (dp - d_g)).astype(BF16)
                dq_g = _nn(dsb, k_band) * ATTN_SCALE
                dk_b = _tn(dsb, q_g) * ATTN_SCALE
                dv_b = _tn(p.astype(BF16), do_g)
                psd = jnp.exp(sink - lse_g) * d_g
                for g in range(GQA_GROUP):
                    h = heads[g]
                    dqs.append(dq_g[WINDOW * g:WINDOW * (g + 1)].astype(BF16))
                    ds_ref[0:1, h:h + 1] -= jnp.sum(psd[WINDOW * g:WINDOW * (g + 1)], axis=0, keepdims=True)
                ksl = slice(HEAD_DIM * hk, HEAD_DIM * (hk + 1))
                vsl = slice(KVW + HEAD_DIM * hk, KVW + HEAD_DIM * (hk + 1))
                dkv_ref[:, ksl] = (kc_scr[:, ksl] + dk_b[0:WINDOW]).astype(BF16)
                dkv_ref[:, vsl] = (vc_scr[:, ksl] + dv_b[0:WINDOW]).astype(BF16)
                kc_scr[:, ksl] = dk_b[WINDOW:2 * WINDOW]
                vc_scr[:, ksl] = dv_b[WINDOW:2 * WINDOW]
            dq_ref[...] = jnp.concatenate(dqs, axis=1)

        @pl.when(st == nblk)
        def _():
            dkv_ref[:, 0:KVW] = kc_scr[...].astype(BF16)
            dkv_ref[:, KVW:2 * KVW] = vc_scr[...].astype(BF16)

    cur_map = lambda b, s: (b * nblk + jnp.minimum(s, nblk - 1), 0)
    prev_row = lambda b, s: b * nblk + jnp.clip(s - 1, 0, nblk - 1)
    return pl.pallas_call(
        kern,
        out_shape=(jax.ShapeDtypeStruct((T, D_MODEL), BF16), jax.ShapeDtypeStruct((T, 2 * KVW), BF16),
                   jax.ShapeDtypeStruct((8, 128), F32)),
        grid=(nseq, nblk + 1),
        in_specs=[pl.BlockSpec(memory_space=pltpu.SMEM),
                  pl.BlockSpec((WINDOW, QKV_W), cur_map),
                  pl.BlockSpec((WINDOW, 2 * KVW), lambda b, s: (prev_row(b, s), 2)),
                  pl.BlockSpec((WINDOW, D_MODEL), cur_map), pl.BlockSpec((WINDOW, D_MODEL), cur_map),
                  pl.BlockSpec((WINDOW, N_Q_HEADS), cur_map)],
        out_specs=(pl.BlockSpec((WINDOW, D_MODEL), cur_map),
                   pl.BlockSpec((WINDOW, 2 * KVW), lambda b, s: (prev_row(b, s), 0)),
                   pl.BlockSpec((8, 128), lambda b, s: (0, 0))),
        scratch_shapes=[pltpu.VMEM((WINDOW, KVW), F32), pltpu.VMEM((WINDOW, KVW), F32)],
        name="attn_bwd", compiler_params=_params("arbitrary", "arbitrary"))(sinks, pqkv, pqkv, att, datt, lse)


def _piece_tiles(pieces):
    out, start = [], 0
    for arr, width in pieces:
        out.append((arr, start, width // COL_TILE))
        start += width // COL_TILE
    return out, start


def _inproj_bwd(pieces, w_in, x, dx_in, g):
    T, D = x.shape
    tm = min(512, T)
    tiles, nk = _piece_tiles(pieces)

    def kern(*refs):
        p_refs = refs[:len(tiles)]
        w_ref, x_ref, dxin_ref, g_ref, dx_ref, dg_ref, acc_scr = refs[len(tiles):]
        i, k = pl.program_id(0), pl.program_id(1)

        @pl.when((i == 0) & (k == 0))
        def _():
            dg_ref[...] = jnp.zeros_like(dg_ref)

        @pl.when(k == 0)
        def _():
            acc_scr[...] = jnp.zeros_like(acc_scr)

        for p_ref, (_, start, n) in zip(p_refs, tiles):
            @pl.when((k >= start) & (k < start + n))
            def _(p_ref=p_ref):
                acc_scr[...] += _nt(p_ref[...], w_ref[...])

        @pl.when(k == nk - 1)
        def _():
            gg = g_ref[...]
            r, xh = _rms_stats(x_ref[...])
            dx, dg = _rms_bwd(acc_scr[...], xh, r, gg)
            dx_ref[...] = dxin_ref[...] + dx
            dg_ref[...] += dg

    p_specs = [pl.BlockSpec((tm, COL_TILE), lambda i, k, s=start, n=n: (i, jnp.clip(k - s, 0, n - 1))) for _, start, n in tiles]
    return pl.pallas_call(
        kern, out_shape=(jax.ShapeDtypeStruct((T, D), F32), jax.ShapeDtypeStruct((1, D), F32)), grid=(T // tm, nk),
        in_specs=p_specs + [pl.BlockSpec((D, COL_TILE), lambda i, k: (0, k)),
                            pl.BlockSpec((tm, D), lambda i, k: (i, 0)), pl.BlockSpec((tm, D), lambda i, k: (i, 0)),
                            pl.BlockSpec((1, D), lambda i, k: (0, 0))],
        out_specs=(pl.BlockSpec((tm, D), lambda i, k: (i, 0)), pl.BlockSpec((1, D), lambda i, k: (0, 0))),
        scratch_shapes=[pltpu.VMEM((tm, D), F32)],
        name="inproj_bwd", compiler_params=_params("arbitrary", "arbitrary"))(*[a for a, _, _ in tiles], w_in, x, dx_in, g)


def _dw_pieces(lhs, pieces):
    T, K = lhs.shape
    tk = min(1024, T)
    nt = T // tk
    tiles, nj = _piece_tiles(pieces)

    def kern(*refs):
        lhs_ref = refs[0]
        p_refs = refs[1:1 + len(tiles)]
        o_ref, ob_ref = refs[1 + len(tiles):]
        j, t = pl.program_id(0), pl.program_id(1)

        @pl.when(t == 0)
        def _():
            o_ref[...] = jnp.zeros_like(o_ref)

        for p_ref, (_, start, n) in zip(p_refs, tiles):
            @pl.when((j >= start) & (j < start + n))
            def _(p_ref=p_ref):
                o_ref[...] += _tn(lhs_ref[...], p_ref[...])

        @pl.when(t == nt - 1)
        def _():
            ob_ref[...] = o_ref[...].astype(BF16)

    def p_map(start, n):
        return lambda j, t: (jnp.where((j >= start) & (j < start + n), t, 0), jnp.clip(j - start, 0, n - 1))

    N = nj * COL_TILE
    return pl.pallas_call(
        kern, out_shape=(jax.ShapeDtypeStruct((K, N), F32), jax.ShapeDtypeStruct((K, N), BF16)), grid=(nj, nt),
        in_specs=[pl.BlockSpec((tk, K), lambda j, t: (t, 0))] + [pl.BlockSpec((tk, COL_TILE), p_map(s, n)) for _, s, n in tiles],
        out_specs=(pl.BlockSpec((K, COL_TILE), lambda j, t: (0, j)), pl.BlockSpec((K, COL_TILE), lambda j, t: (0, j))),
        name="dw_pieces", compiler_params=_params("arbitrary", "arbitrary"))(lhs, *[a for a, _, _ in tiles])


def _dw(lhs, rhs, lhs_by_j=False, relu2_lhs=False):
    T = lhs.shape[0]
    tk = min(1024, T)
    nt = T // tk
    W = D_MODEL
    nj = (lhs.shape[1] if lhs_by_j else rhs.shape[1]) // W

    def kern(lhs_ref, rhs_ref, o_ref, ob_ref):
        t = pl.program_id(1)

        @pl.when(t == 0)
        def _():
            o_ref[...] = jnp.zeros_like(o_ref)

        a = lhs_ref[...]
        if relu2_lhs:
            a = jnp.square(jnp.maximum(a.astype(F32), 0.0)).astype(BF16)
        o_ref[...] += _tn(a, rhs_ref[...].astype(BF16))

        @pl.when(t == nt - 1)
        def _():
            ob_ref[...] = o_ref[...].astype(BF16)

    if lhs_by_j:
        shape, lmap, rmap, omap = (nj * W, W), (lambda j, t: (t, j)), (lambda j, t: (t, 0)), (lambda j, t: (j, 0))
    else:
        shape, lmap, rmap, omap = (W, nj * W), (lambda j, t: (t, 0)), (lambda j, t: (t, j)), (lambda j, t: (0, j))
    return pl.pallas_call(
        kern, out_shape=(jax.ShapeDtypeStruct(shape, F32), jax.ShapeDtypeStruct(shape, BF16)), grid=(nj, nt),
        in_specs=[pl.BlockSpec((tk, W), lmap), pl.BlockSpec((tk, W), rmap)],
        out_specs=(pl.BlockSpec((W, W), omap), pl.BlockSpec((W, W), omap)),
        name="dw", compiler_params=_params("arbitrary", "arbitrary"))(lhs, rhs)


BIG = (("w_in", D_MODEL, IN_COLS, "col"), ("w_attn_out", D_MODEL, D_MODEL, "row"), ("w_conv_out", D_MODEL, D_MODEL, "row"),
       ("w_o", D_MODEL, D_MODEL, "row"), ("w_up", D_MODEL, D_FF, "col"), ("w_down", D_FF, D_MODEL, "row"))


def _shard_dims(rows, cols, kind):
    return (rows, cols // N_CHIP) if kind == "col" else (rows // N_CHIP, cols)


def _window(ref, rows, cols, kind, chip, half):
    sr, sc = _shard_dims(rows, cols, kind)
    hr = sr // 2
    if kind == "col":
        return ref.at[pl.ds(half * hr, hr), pl.ds(chip * sc, sc)]
    return ref.at[pl.ds(chip * sr + half * hr, hr), :]


def _mesh_pos():
    x, y, c = lax.axis_index("x"), lax.axis_index("y"), lax.axis_index("c")
    return x, y, c, 2 * x + y


_REL_BITS = (2, 1, 3)


def _rel_dev(x, y, c, r):
    return ((1 - x, y, c), (x, 1 - y, c), (1 - x, 1 - y, c))[r]


def _for_my_chip(j, fn):
    for js in range(N_CHIP):
        pl.when(j == js)(functools.partial(fn, js))


def _cast_into_full(j_arr, shard, l, rows, cols, kind):
    sr, sc = _shard_dims(rows, cols, kind)
    tr = min(256, sr)

    def kern(j_ref, s_ref, o_ref):
        o_ref[...] = s_ref[...].astype(BF16)

    if kind == "col":
        omap = lambda i, j_ref: (i, j_ref[0])
    else:
        omap = lambda i, j_ref: (j_ref[0] * (sr // tr) + i, 0)
    gs = pltpu.PrefetchScalarGridSpec(
        num_scalar_prefetch=1, grid=(sr // tr,),
        in_specs=[pl.BlockSpec((None, tr, sc), lambda i, j_ref: (l, i, 0))], out_specs=pl.BlockSpec((tr, sc), omap))
    return pl.pallas_call(kern, out_shape=jax.ShapeDtypeStruct((rows, cols), BF16), grid_spec=gs, name="cast_into_full",
                          compiler_params=_params("arbitrary"))(j_arr, shard)


def _gather(fulls, cw=None):
    n_big = len(fulls)
    n_piece = n_big + (0 if cw is None else 1)

    def body(*refs):
        in_refs, o_refs = refs[:n_piece], refs[n_piece:2 * n_piece]
        send_sem, recv_sem, fsend_sem, frecv_sem, loc_sem = refs[2 * n_piece:]
        x, y, c, j = _mesh_pos()

        def run(js):
            def piece(p, chip, half):
                if p == n_big:
                    return o_refs[p].at[half, chip]
                _, rows, cols, kind = fulls[p]
                return _window(o_refs[p], rows, cols, kind, chip, half)

            def mine(p):
                return in_refs[p].at[c] if p == n_big else piece(p, js, c)

            local = []
            if cw is not None:
                for half in range(2):
                    cp = pltpu.make_async_copy(in_refs[n_big].at[half], piece(n_big, js, half), loc_sem.at[half])
                    cp.start()
                    local.append(cp)
            sends = []
            for r in range(3):
                for p in range(n_piece):
                    cp = pltpu.make_async_remote_copy(mine(p), piece(p, js, c), send_sem.at[r * n_piece + p],
                                                      recv_sem.at[r * n_piece + p], _rel_dev(x, y, c, r), MESH)
                    cp.start()
                    sends.append(cp)
            fwds = []
            for r in range(3):
                ks = js ^ _REL_BITS[r]
                for p in range(n_piece):
                    got = piece(p, ks, c)
                    pltpu.make_async_remote_copy(got, got, send_sem.at[r * n_piece + p], recv_sem.at[r * n_piece + p],
                                                 _rel_dev(x, y, c, r), MESH).wait_recv()
                    cp = pltpu.make_async_remote_copy(got, got, fsend_sem.at[r * n_piece + p], frecv_sem.at[r * n_piece + p],
                                                      (x, y, 1 - c), MESH)
                    cp.start()
                    fwds.append(cp)
            for r in range(3):
                ks = js ^ _REL_BITS[r]
                for p in range(n_piece):
                    got = piece(p, ks, 1 - c)
                    pltpu.make_async_remote_copy(got, got, fsend_sem.at[r * n_piece + p], frecv_sem.at[r * n_piece + p],
                                                 (x, y, 1 - c), MESH).wait_recv()
            for cp in sends + fwds:
                cp.wait_send()
            for cp in local:
                cp.wait()

        _for_my_chip(j, run)

    out_shape = [jax.ShapeDtypeStruct(a.shape, BF16) for a, _, _, _ in fulls]
    ins = [a for a, _, _, _ in fulls]
    if cw is not None:
        out_shape.append(jax.ShapeDtypeStruct((DEPTH, N_CHIP, 3, D_MODEL // N_CHIP), F32))
        ins.append(cw)
    outs = pl.pallas_call(
        body, out_shape=out_shape, in_specs=[ANY] * n_piece, out_specs=[ANY] * n_piece,
        input_output_aliases={p: p for p in range(n_big)},
        scratch_shapes=[pltpu.SemaphoreType.DMA((3 * n_piece,))] * 4 + [pltpu.SemaphoreType.DMA((2,))],
        name="gather_weights")(*ins)
    return outs


def _sibling_exchange(gb):
    n = len(gb)

    def body(*refs):
        g_refs, o_refs = refs[:n], refs[n:2 * n]
        send_sem, recv_sem = refs[2 * n:]
        x, y, c, _ = _mesh_pos()
        cps = []
        for t in range(n):
            _, rows, cols, kind = gb[t]
            for chip in range(N_CHIP):
                out_w = _window(g_refs[t], rows, cols, kind, chip, 1 - c)
                dst_w = _window(o_refs[t], rows, cols, kind, chip, 1 - c)
                cp = pltpu.make_async_remote_copy(out_w, dst_w, send_sem.at[N_CHIP * t + chip], recv_sem.at[N_CHIP * t + chip],
                                                  (x, y, 1 - c), MESH)
                cp.start()
                cps.append(cp)
        for t in range(n):
            _, rows, cols, kind = gb[t]
            for chip in range(N_CHIP):
                w = _window(o_refs[t], rows, cols, kind, chip, c)
                pltpu.make_async_remote_copy(w, w, send_sem.at[N_CHIP * t + chip], recv_sem.at[N_CHIP * t + chip],
                                             (x, y, 1 - c), MESH).wait_recv()
        for cp in cps:
            cp.wait_send()

    return pl.pallas_call(
        body, out_shape=[jax.ShapeDtypeStruct(a.shape, BF16) for a, _, _, _ in gb], in_specs=[ANY] * n, out_specs=[ANY] * n,
        scratch_shapes=[pltpu.SemaphoreType.DMA((N_CHIP * n,))] * 2, name="grad_sibling_exchange")(*[a for a, _, _, _ in gb])


def _half_add(c_arr, g, sib, rows, cols, kind):
    sr, sc = _shard_dims(rows, cols, kind)
    hr = sr // 2

    def kern(c_ref, g_ref, s_ref, ob_ref, of_ref):
        v = g_ref[...] + s_ref[...].astype(F32)
        of_ref[...] = v
        ob_ref[...] = v.astype(BF16)

    if kind == "col":
        imap = lambda j, c_ref: (c_ref[0], j)
    else:
        imap = lambda j, c_ref: (2 * j + c_ref[0], 0)
    gs = pltpu.PrefetchScalarGridSpec(
        num_scalar_prefetch=1, grid=(N_CHIP,),
        in_specs=[pl.BlockSpec((hr, sc), imap), pl.BlockSpec((hr, sc), imap)],
        out_specs=[pl.BlockSpec((None, hr, sc), lambda j, c_ref: (j, 0, 0))] * 2)
    return pl.pallas_call(
        kern, out_shape=(jax.ShapeDtypeStruct((N_CHIP, hr, sc), BF16), jax.ShapeDtypeStruct((N_CHIP, hr, sc), F32)),
        grid_spec=gs, name="grad_half_add", compiler_params=_params("arbitrary"))(c_arr, g, sib)


def _chip_exchange(sbs):
    n = len(sbs)

    def body(*refs):
        s_refs, o_refs = refs[:n], refs[n:2 * n]
        send_sem, recv_sem = refs[2 * n:]
        x, y, c, j = _mesh_pos()
        cps = []
        for r in range(3):
            k = j ^ _REL_BITS[r]
            for t in range(n):
                cp = pltpu.make_async_remote_copy(s_refs[t].at[k], o_refs[t].at[r], send_sem.at[n * r + t], recv_sem.at[n * r + t],
                                                  _rel_dev(x, y, c, r), MESH)
                cp.start()
                cps.append(cp)
        for cp in cps:
            cp.wait()

    return pl.pallas_call(
        body, out_shape=[jax.ShapeDtypeStruct((3,) + a.shape[1:], BF16) for a in sbs], in_specs=[ANY] * n, out_specs=[ANY] * n,
        scratch_shapes=[pltpu.SemaphoreType.DMA((3 * n,))] * 2, name="grad_chip_exchange")(*sbs)


def _owner_sum(jc_arr, sf, rb, l, into=None):
    _, hr, sc = sf.shape

    def kern(jc_ref, s_ref, r0_ref, r1_ref, r2_ref, *rest):
        o_ref = rest[-1]
        o_ref[...] = ((s_ref[...] + r0_ref[...].astype(F32)) + r1_ref[...].astype(F32)) + r2_ref[...].astype(F32)

    in_specs = [pl.BlockSpec((None, hr, sc), lambda i, jc_ref: (jc_ref[0], 0, 0))]
    in_specs += [pl.BlockSpec((None, hr, sc), lambda i, jc_ref, r=r: (r, 0, 0)) for r in range(3)]
    args = [jc_arr, sf, rb, rb, rb]
    aliases = {}
    if into is not None:
        in_specs.append(ANY)
        args.append(into)
        aliases = {len(args) - 1: 0}
    gs = pltpu.PrefetchScalarGridSpec(
        num_scalar_prefetch=1, grid=(1,), in_specs=in_specs,
        out_specs=pl.BlockSpec((None, hr, sc), lambda i, jc_ref: (l, jc_ref[1], 0)))
    return pl.pallas_call(kern, out_shape=jax.ShapeDtypeStruct((DEPTH, 2 * hr, sc), F32), grid_spec=gs,
                          input_output_aliases=aliases, name="grad_owner_sum", compiler_params=_params("arbitrary"))(*args)


def _sibling_assemble(grads):
    n = len(grads)

    def body(*refs):
        o_refs = refs[n:2 * n]
        send_sem, recv_sem = refs[2 * n:]
        x, y, c, _ = _mesh_pos()
        cps = []
        for q in range(n):
            hr = grads[q].shape[1] // 2
            for l in range(DEPTH):
                mine = o_refs[q].at[l, pl.ds(c * hr, hr), :]
                cp = pltpu.make_async_remote_copy(mine, mine, send_sem.at[DEPTH * q + l], recv_sem.at[DEPTH * q + l],
                                                  (x, y, 1 - c), MESH)
                cp.start()
                cps.append(cp)
        for q in range(n):
            hr = grads[q].shape[1] // 2
            for l in range(DEPTH):
                theirs = o_refs[q].at[l, pl.ds((1 - c) * hr, hr), :]
                pltpu.make_async_remote_copy(theirs, theirs, send_sem.at[DEPTH * q + l], recv_sem.at[DEPTH * q + l],
                                             (x, y, 1 - c), MESH).wait_recv()
        for cp in cps:
            cp.wait_send()

    return pl.pallas_call(
        body, out_shape=[jax.ShapeDtypeStruct(g.shape, F32) for g in grads], in_specs=[ANY] * n, out_specs=[ANY] * n,
        input_output_aliases={q: q for q in range(n)},
        scratch_shapes=[pltpu.SemaphoreType.DMA((DEPTH * n,))] * 2, name="grad_sibling_assemble")(*grads)


def _adamw_math(w, g, m, v):
    m = ADAM_B1 * m + (1.0 - ADAM_B1) * g
    v = ADAM_B2 * v + (1.0 - ADAM_B2) * jnp.square(g)
    m_hat = m / (1.0 - ADAM_B1 ** ADAM_STEP)
    v_hat = v / (1.0 - ADAM_B2 ** ADAM_STEP)
    delta = -ADAM_LR * (m_hat / (jnp.sqrt(v_hat) + ADAM_EPS) + ADAM_WD * w)
    return delta, m, v


def _adamw(w, g, m, v):
    shape = w.shape
    C = shape[-1]
    R = int(np.prod(shape[:-1]))
    tr = min(256, R)
    args = [a.reshape(R, C) for a in (w, g, m, v)]

    def kern(w_ref, g_ref, m_ref, v_ref, d_ref, nm_ref, nv_ref):
        d, nm, nv = _adamw_math(w_ref[...], g_ref[...], m_ref[...], v_ref[...])
        d_ref[...] = d
        nm_ref[...] = nm
        nv_ref[...] = nv

    spec = pl.BlockSpec((tr, C), lambda i: (i, 0))
    outs = pl.pallas_call(
        kern, out_shape=[jax.ShapeDtypeStruct((R, C), F32)] * 3, grid=(R // tr,), in_specs=[spec] * 4, out_specs=[spec] * 3,
        name="adamw", compiler_params=_params("parallel"))(*args)
    return [o.reshape(shape) for o in outs]


_ROW_G_MIX, _ROW_B_GATES, _ROW_SINKS, _ROW_CONV, _ROW_G_MLP, _ROW_G_FINAL, _ROW_LOSS = 0, 2, 6, 8, 16, 18, 19


def _small_step(parts, params, moms, vels):
    names = ["g_mix", "b_gates", "sinks", "conv_w", "conv_b", "g_mlp", "g_final"]
    D = D_MODEL
    QW = D // N_CHIP
    n_dev = 8

    def body(*refs):
        it = iter(refs)
        dgmix = [next(it) for _ in range(DEPTH)]
        dbg = [next(it) for _ in range(DEPTH)]
        dsk = [next(it) for _ in range(DEPTH)]
        dwb = [next(it) for _ in range(DEPTH)]
        dgmlp = [next(it) for _ in range(DEPTH)]
        lst = next(it)
        p_refs = {n: next(it) for n in names}
        m_refs = {n: next(it) for n in names}
        v_refs = {n: next(it) for n in names}
        loss_ref = next(it)
        outs = {n: [next(it) for _ in range(4)] for n in names}
        pack_ref, all_ref, send_sem, recv_sem = next(it), next(it), next(it), next(it)

        x, y, c, j = _mesh_pos()
        me = 4 * x + 2 * y + c
        pack_ref[...] = jnp.zeros_like(pack_ref)
        for l in range(DEPTH):
            pack_ref[_ROW_G_MIX + l:_ROW_G_MIX + l + 1, :] = dgmix[l][...]
            pack_ref[_ROW_B_GATES + 2 * l:_ROW_B_GATES + 2 * l + 1, :] = dbg[l][:, 0:D]
            pack_ref[_ROW_B_GATES + 2 * l + 1:_ROW_B_GATES + 2 * l + 2, :] = dbg[l][:, D:2 * D]
            pack_ref[_ROW_SINKS + l:_ROW_SINKS + l + 1, 0:128] = dsk[l][0:1, :]
            pack_ref[_ROW_CONV + 4 * l:_ROW_CONV + 4 * l + 4, :] = dwb[l][0:4, :]
            pack_ref[_ROW_G_MLP + l:_ROW_G_MLP + l + 1, :] = dgmlp[l][...]
        pack_ref[_ROW_G_FINAL:_ROW_G_FINAL + 1, :] = lst[0:1, :]
        pack_ref[_ROW_LOSS:_ROW_LOSS + 1, :] = lst[1:2, :]

        all_ref[me] = pack_ref[...]
        cps = []
        for k in range(1, n_dev):
            dx_, dy_, dc_ = (k >> 2) & 1, (k >> 1) & 1, k & 1
            peer = (x ^ dx_, y ^ dy_, c ^ dc_)
            cp = pltpu.make_async_remote_copy(pack_ref, all_ref.at[me], send_sem.at[k - 1], recv_sem.at[k - 1], peer, MESH)
            cp.start()
            cps.append(cp)
        for cp in cps:
            cp.wait()

        tot = all_ref[0]
        for d in range(1, n_dev):
            tot = tot + all_ref[d]
        pack_ref[...] = tot

        loss_ref[...] = pack_ref[_ROW_LOSS:_ROW_LOSS + 1, 0:1]

        def finish(name, idx, g):
            w, m, v = p_refs[name][idx], m_refs[name][idx], v_refs[name][idx]
            d, nm, nv = _adamw_math(w, g, m, v)
            for ref, val in zip(outs[name], (g, d, nm, nv)):
                ref[idx] = val

        for l in range(DEPTH):
            finish("g_mix", (slice(l, l + 1), slice(None)), pack_ref[_ROW_G_MIX + l:_ROW_G_MIX + l + 1, :])
            finish("g_mlp", (slice(l, l + 1), slice(None)), pack_ref[_ROW_G_MLP + l:_ROW_G_MLP + l + 1, :])
            finish("conv_b", (slice(l, l + 1), slice(None)), pack_ref[_ROW_CONV + 4 * l + 3:_ROW_CONV + 4 * l + 4, :])
            finish("sinks", (slice(l, l + 1), slice(None)), pack_ref[_ROW_SINKS + l:_ROW_SINKS + l + 1, 0:N_Q_HEADS])
            for hf in range(2):
                finish("b_gates", (slice(l, l + 1), slice(hf * D, (hf + 1) * D)),
                       pack_ref[_ROW_B_GATES + 2 * l + hf:_ROW_B_GATES + 2 * l + hf + 1, :])
        finish("g_final", (slice(0, 1), slice(None)), pack_ref[_ROW_G_FINAL:_ROW_G_FINAL + 1, :])

        def conv_w_chip(js):
            for l in range(DEPTH):
                for k in range(3):
                    row = _ROW_CONV + 4 * l + k
                    finish("conv_w", (l, slice(k, k + 1), slice(None)), pack_ref[row:row + 1, js * QW:(js + 1) * QW])

        _for_my_chip(j, conv_w_chip)

    vm = pl.BlockSpec(memory_space=pltpu.VMEM)
    ins = (parts["g_mix"] + parts["b_gates"] + parts["sinks"] + parts["conv"] + parts["g_mlp"] + [parts["loss"]]
           + [params[n] for n in names] + [moms[n] for n in names] + [vels[n] for n in names])
    out_shape = [jax.ShapeDtypeStruct((1, 1), F32)]
    for n in names:
        out_shape += [jax.ShapeDtypeStruct(params[n].shape, F32)] * 4
    res = pl.pallas_call(
        body, out_shape=out_shape, in_specs=[vm] * len(ins), out_specs=[vm] * len(out_shape),
        scratch_shapes=[pltpu.VMEM((SMALL_ROWS, D), F32), pltpu.VMEM((n_dev, SMALL_ROWS, D), F32),
                        pltpu.SemaphoreType.DMA((n_dev - 1,)), pltpu.SemaphoreType.DMA((n_dev - 1,))],
        name="small_allreduce_adamw")(*ins)
    loss = res[0]
    out = {n: res[1 + 4 * i:5 + 4 * i] for i, n in enumerate(names)}
    return loss, out


def kernel(x, g_mix, w_in, b_gates, sinks, w_attn_out, conv_w, conv_b, w_conv_out, w_o, g_mlp, w_up, w_down, g_final, loss_target, m_g_mix, m_w_in, m_b_gates, m_sinks, m_w_attn_out, m_conv_w, m_conv_b, m_w_conv_out, m_w_o, m_g_mlp, m_w_up, m_w_down, m_g_final, v_g_mix, v_w_in, v_b_gates, v_sinks, v_w_attn_out, v_conv_w, v_conv_b, v_w_conv_out, v_w_o, v_g_mlp, v_w_up, v_w_down, v_g_final):
    B, S, D = x.shape
    T = B * S
    big_w = dict(w_in=w_in, w_attn_out=w_attn_out, w_conv_out=w_conv_out, w_o=w_o, w_up=w_up, w_down=w_down)
    big_m = dict(w_in=m_w_in, w_attn_out=m_w_attn_out, w_conv_out=m_w_conv_out, w_o=m_w_o, w_up=m_w_up, w_down=m_w_down)
    big_v = dict(w_in=v_w_in, w_attn_out=v_w_attn_out, w_conv_out=v_w_conv_out, w_o=v_w_o, w_up=v_w_up, w_down=v_w_down)

    c_arr = lax.axis_index("c").astype(jnp.int32).reshape(1)
    j_arr = (2 * lax.axis_index("x") + lax.axis_index("y")).astype(jnp.int32).reshape(1)
    jc_arr = jnp.concatenate([j_arr, c_arr])
    order = [(n, l) for n, _, _, _ in BIG for l in range(DEPTH)]
    dims = {n: (r, c_, k) for n, r, c_, k in BIG}

    own = [(_cast_into_full(j_arr, big_w[n], l, *dims[n]),) + dims[n] for n, l in order]
    gathered = _gather(own, conv_w)
    full = dict(zip(order, gathered[:len(order)]))
    conv_w_full = jnp.transpose(gathered[len(order)], (0, 2, 1, 3)).reshape(DEPTH, 3, D)

    xs = [x.reshape(T, D)]
    saved = []
    for l in range(DEPTH):
        W = {n: full[(n, l)] for n in big_w}
        h = _rms_fwd(xs[-1], g_mix[l:l + 1])
        pqkv = _proj(h, W["w_in"], 0, QKV_W, QKV_W)
        pconv = _proj(h, W["w_in"], QKV_W, CONV_W, QKV_W)
        pgate = _proj(h, W["w_in"], QKV_W + CONV_W, GATE_W, COL_TILE)
        att, lse = _attn_fwd(pqkv, sinks[l], S)
        cv = _conv_fwd(pconv, conv_w_full[l], conv_b[l:l + 1], S)
        x1, ya, yc, mg = _mix_fwd(xs[-1], att, cv, pgate, b_gates[l:l + 1], W["w_attn_out"], W["w_conv_out"], W["w_o"])
        x2, a = _mlp_fwd(x1, g_mlp[l:l + 1], W["w_up"], W["w_down"])
        saved.append(dict(h=h, pqkv=pqkv, pconv=pconv, pgate=pgate, att=att, lse=lse, cv=cv, x1=x1, ya=ya, yc=yc, mg=mg, a=a))
        xs.append(x2)

    loss_stats, dx = _loss_bwd(xs[-1], g_final.reshape(1, D), loss_target.reshape(T, D))

    parts = dict(g_mix=[None] * DEPTH, b_gates=[None] * DEPTH, sinks=[None] * DEPTH, conv=[None] * DEPTH,
                 g_mlp=[None] * DEPTH, loss=loss_stats)
    gf, gb = {}, {}
    for l in reversed(range(DEPTH)):
        W = {n: full[(n, l)] for n in big_w}
        sv = saved[l]
        dx1, da, h2, parts["g_mlp"][l] = _mlp_bwd(dx, sv["x1"], sv["a"], g_mlp[l:l + 1], W["w_up"], W["w_down"])
        gf["w_up", l], gb["w_up", l] = _dw(h2, da)
        gf["w_down", l], gb["w_down", l] = _dw(sv["a"], dx, lhs_by_j=True, relu2_lhs=True)
        datt, dcv, dya, dyc, dgate, parts["b_gates"][l] = _mix_bwd(
            dx1, sv["ya"], sv["yc"], sv["pgate"], b_gates[l:l + 1], W["w_attn_out"], W["w_conv_out"], W["w_o"])
        gf["w_o", l], gb["w_o", l] = _dw(sv["mg"], dx1)
        gf["w_attn_out", l], gb["w_attn_out", l] = _dw(sv["att"], dya)
        gf["w_conv_out", l], gb["w_conv_out", l] = _dw(sv["cv"], dyc)
        dconv, parts["conv"][l] = _conv_bwd(dcv, sv["pconv"], conv_w_full[l], conv_b[l:l + 1], S)
        dq, dkv, parts["sinks"][l] = _attn_bwd(sv["pqkv"], sv["att"], datt, sv["lse"], sinks[l], S)
        pieces = [(dq, D), (dkv, QKV_W - D), (dconv, CONV_W), (dgate, GATE_W)]
        gf["w_in", l], gb["w_in", l] = _dw_pieces(sv["h"], pieces)
        dx, parts["g_mix"][l] = _inproj_bwd(pieces, W["w_in"], xs[l], dx1, g_mix[l:l + 1])

    sib = _sibling_exchange([(gb[key],) + dims[key[0]] for key in order])
    pre = [_half_add(c_arr, gf[key], s, *dims[key[0]]) for key, s in zip(order, sib)]
    got = _chip_exchange([p[0] for p in pre])
    mine = {}
    for (n, l), p, r in zip(order, pre, got):
        mine[n] = _owner_sum(jc_arr, p[1], r, l, mine.get(n))
    grads = dict(zip(mine, _sibling_assemble(list(mine.values()))))

    res = {}
    for n in big_w:
        d, nm, nv = _adamw(big_w[n], grads[n], big_m[n], big_v[n])
        res[n] = (grads[n], d, nm, nv)

    small_p = dict(g_mix=g_mix, b_gates=b_gates, sinks=sinks, conv_w=conv_w, conv_b=conv_b, g_mlp=g_mlp, g_final=g_final.reshape(1, D))
    small_m = dict(g_mix=m_g_mix, b_gates=m_b_gates, sinks=m_sinks, conv_w=m_conv_w, conv_b=m_conv_b, g_mlp=m_g_mlp,
                   g_final=m_g_final.reshape(1, D))
    small_v = dict(g_mix=v_g_mix, b_gates=v_b_gates, sinks=v_sinks, conv_w=v_conv_w, conv_b=v_conv_b, g_mlp=v_g_mlp,
                   g_final=v_g_final.reshape(1, D))
    loss, small = _small_step(parts, small_p, small_m, small_v)
    for n, vals in small.items():
        res[n] = tuple(v.reshape(D) for v in vals) if n == "g_final" else tuple(vals)

    weights = ["g_mix", "w_in", "b_gates", "sinks", "w_attn_out", "conv_w", "conv_b", "w_conv_out", "w_o", "g_mlp", "w_up",
               "w_down", "g_final"]
    out = [loss.reshape(()), dx.reshape(B, S, D)]
    for k in range(4):
        out += [res[n][k] for n in weights]
    return tuple(out)
```

```python
import functools

import numpy as np
import jax
import jax.numpy as jnp
from jax import lax
from jax.experimental import pallas as pl
from jax.experimental.pallas import tpu as pltpu

F32 = jnp.float32
BF16 = jnp.bfloat16

D_MODEL = 1024
HEAD_DIM = 64
N_Q_HEADS = 16
N_KV_HEADS = 4
GQA_GROUP = 4
WINDOW = 128
D_FF = 4096
DEPTH = 2
RMS_EPS = 1e-6
NEG_INF = -1e30
ATTN_SCALE = HEAD_DIM ** -0.5
QKV_W = 1536
CONV_W = 3072
GATE_W = 2048
IN_COLS = QKV_W + CONV_W + GATE_W
COL_TILE = 512
N_CHIP = 4
ADAM_LR = 0.001
ADAM_B1 = 0.9
ADAM_B2 = 0.999
ADAM_EPS = 1e-08
ADAM_WD = 0.01
ADAM_STEP = 10
V7X_VMEM_BYTES = 64 * 2 ** 20
VMEM_LIMIT = V7X_VMEM_BYTES - 8 * 2 ** 20
MESH = pl.DeviceIdType.MESH
ANY = pl.BlockSpec(memory_space=pl.ANY)
SMALL_ROWS = 24

_SLOPES = [float(v) for v in np.power(np.float32(2.0), -8.0 * np.arange(1, N_Q_HEADS + 1, dtype=np.float32) / N_Q_HEADS)]


def _params(*sem):
    return pltpu.CompilerParams(dimension_semantics=sem, vmem_limit_bytes=VMEM_LIMIT)


class _Hosted:
    def __init__(self, inputs, out_shape, aliases, scratch, start, finish):
        self.inputs, self.out_shape, self.aliases, self.scratch = list(inputs), list(out_shape), dict(aliases), list(scratch)
        self.start, self.finish = start, finish


def _hosted_call(comm, kern, *, out_shape, grid, in_specs, out_specs, args, name, sem, scratch_shapes=()):
    single = not isinstance(out_shape, (tuple, list))
    outs = [out_shape] if single else list(out_shape)
    ospecs = [out_specs] if single else list(out_specs)
    if comm is None:
        res = pl.pallas_call(kern, out_shape=outs, grid=grid, in_specs=list(in_specs), out_specs=ospecs,
                             scratch_shapes=list(scratch_shapes), name=name, compiler_params=_params(*sem))(*args)
        return res[0] if single else res
    n_in, n_out, n_scr = len(args), len(outs), len(scratch_shapes)
    ci, co, cs = len(comm.inputs), len(comm.out_shape), len(comm.scratch)

    def body(*refs):
        cuts = np.cumsum([0, n_in, ci, n_out, co, n_scr, cs])
        a, b, c, d, e, f = [refs[lo:hi] for lo, hi in zip(cuts[:-1], cuts[1:])]
        ids = [pl.program_id(k) for k in range(len(grid))]
        first = functools.reduce(jnp.logical_and, [i == 0 for i in ids])
        last = functools.reduce(jnp.logical_and, [i == n - 1 for i, n in zip(ids, grid)])
        pl.when(first)(lambda: comm.start(b, d, f))
        kern(*a, *c, *e)
        pl.when(last)(lambda: comm.finish(b, d, f))

    res = pl.pallas_call(
        body, out_shape=outs + comm.out_shape, grid=grid, in_specs=list(in_specs) + [ANY] * ci, out_specs=ospecs + [ANY] * co,
        scratch_shapes=list(scratch_shapes) + comm.scratch,
        input_output_aliases={n_in + i: n_out + o for i, o in comm.aliases.items()},
        name=name + "_carrier", compiler_params=_params(*(["arbitrary"] * len(grid))))(*args, *comm.inputs)
    main = res[:n_out]
    return (main[0] if single else main), res[n_out:]


def _nt(a, b):
    return lax.dot_general(a, b, (((1,), (1,)), ((), ())), preferred_element_type=F32)


def _tn(a, b):
    return lax.dot_general(a, b, (((0,), (0,)), ((), ())), preferred_element_type=F32)


def _nn(a, b):
    return jnp.dot(a, b, preferred_element_type=F32)


def _rms_stats(xf):
    r = lax.rsqrt(jnp.mean(xf * xf, axis=-1, keepdims=True) + RMS_EPS)
    return r, xf * r


def _rms_bwd(dh, xh, r, g):
    dxh = dh * g
    dx = r * (dxh - xh * jnp.mean(dxh * xh, axis=-1, keepdims=True))
    dg = jnp.sum(dh * xh, axis=0, keepdims=True)
    return dx, dg


def _rms_fwd(x, g):
    T, D = x.shape
    tm = min(512, T)

    def kern(x_ref, g_ref, h_ref):
        _, xh = _rms_stats(x_ref[...])
        h_ref[...] = (xh * g_ref[...]).astype(BF16)

    return pl.pallas_call(
        kern, out_shape=jax.ShapeDtypeStruct((T, D), BF16), grid=(T // tm,),
        in_specs=[pl.BlockSpec((tm, D), lambda i: (i, 0)), pl.BlockSpec((1, D), lambda i: (0, 0))],
        out_specs=pl.BlockSpec((tm, D), lambda i: (i, 0)),
        name="rms_fwd", compiler_params=_params("parallel"))(x, g)


def _proj(h, w, col_off, ncols, tn, comm=None):
    T, K = h.shape
    tm = min(512, T)
    off = col_off // tn

    def kern(a_ref, w_ref, o_ref):
        o_ref[...] = _nn(a_ref[...], w_ref[...]).astype(BF16)

    return _hosted_call(
        comm, kern, out_shape=jax.ShapeDtypeStruct((T, ncols), BF16), grid=(T // tm, ncols // tn),
        in_specs=[pl.BlockSpec((tm, K), lambda i, j: (i, 0)), pl.BlockSpec((K, tn), lambda i, j: (0, off + j))],
        out_specs=pl.BlockSpec((tm, tn), lambda i, j: (i, j)),
        name="proj", sem=("parallel", "arbitrary"), args=(h, w))


def _band_masks(first_i):
    r = lax.broadcasted_iota(jnp.int32, (GQA_GROUP * WINDOW, 2 * WINDOW), 0)
    jj = lax.broadcasted_iota(jnp.int32, (GQA_GROUP * WINDOW, 2 * WINDOW), 1)
    dist = WINDOW + (r & (WINDOW - 1)) - jj
    valid = (dist >= 0) & (dist < WINDOW) & ((jj + (1 - first_i) * WINDOW) >= WINDOW)
    return dist.astype(F32), valid


def _per_head_col(vals):
    rb = lax.broadcasted_iota(jnp.int32, (GQA_GROUP * WINDOW, 1), 0) >> 7
    col = jnp.full((GQA_GROUP * WINDOW, 1), vals[3], F32)
    for g in (2, 1, 0):
        col = jnp.where(rb == g, vals[g], col)
    return col


def _stack_heads(ref, base, hk):
    return jnp.concatenate(
        [ref[:, base + HEAD_DIM * (GQA_GROUP * hk + g): base + HEAD_DIM * (GQA_GROUP * hk + g + 1)] for g in range(GQA_GROUP)],
        axis=0)


def _attn_fwd(pqkv, sinks, seq, comm=None):
    T = pqkv.shape[0]
    nblk = seq // WINDOW
    K0, V0 = N_Q_HEADS * HEAD_DIM, N_Q_HEADS * HEAD_DIM + N_KV_HEADS * HEAD_DIM

    def kern(sink_ref, cur_ref, prev_ref, o_ref, lse_ref):
        i = pl.program_id(0)
        first_i = ((i % nblk) == 0).astype(jnp.int32)
        distf, valid = _band_masks(first_i)
        outs = []
        for hk in range(N_KV_HEADS):
            k_band = jnp.concatenate([prev_ref[:, HEAD_DIM * hk:HEAD_DIM * (hk + 1)],
                                      cur_ref[:, K0 + HEAD_DIM * hk:K0 + HEAD_DIM * (hk + 1)]], axis=0)
            v_band = jnp.concatenate([prev_ref[:, 256 + HEAD_DIM * hk:256 + HEAD_DIM * (hk + 1)],
                                      cur_ref[:, V0 + HEAD_DIM * hk:V0 + HEAD_DIM * (hk + 1)]], axis=0)
            q_g = _stack_heads(cur_ref, 0, hk)
            slope = _per_head_col([_SLOPES[GQA_GROUP * hk + g] for g in range(GQA_GROUP)])
            sink = _per_head_col([sink_ref[GQA_GROUP * hk + g] for g in range(GQA_GROUP)])
            s = _nt(q_g, k_band) * ATTN_SCALE + (-slope) * distf
            s = jnp.where(valid, s, NEG_INF)
            m = jnp.maximum(jnp.max(s, axis=-1, keepdims=True), sink)
            p = jnp.exp(s - m)
            denom = jnp.sum(p, axis=-1, keepdims=True) + jnp.exp(sink - m)
            o_g = _nn((p / denom).astype(BF16), v_band)
            lse_g = m + jnp.log(denom)
            for g in range(GQA_GROUP):
                h = GQA_GROUP * hk + g
                outs.append(o_g[WINDOW * g:WINDOW * (g + 1)].astype(BF16))
                lse_ref[:, h:h + 1] = lse_g[WINDOW * g:WINDOW * (g + 1)]
        o_ref[...] = jnp.concatenate(outs, axis=1)

    return _hosted_call(
        comm, kern,
        out_shape=(jax.ShapeDtypeStruct((T, D_MODEL), BF16), jax.ShapeDtypeStruct((T, N_Q_HEADS), F32)),
        grid=(T // WINDOW,),
        in_specs=[pl.BlockSpec(memory_space=pltpu.SMEM),
                  pl.BlockSpec((WINDOW, QKV_W), lambda i: (i, 0)),
                  pl.BlockSpec((WINDOW, 512), lambda i: (jnp.maximum(i - 1, 0), 2))],
        out_specs=(pl.BlockSpec((WINDOW, D_MODEL), lambda i: (i, 0)), pl.BlockSpec((WINDOW, N_Q_HEADS), lambda i: (i, 0))),
        name="attn_fwd", sem=("parallel",), args=(sinks, pqkv, pqkv))


def _pick_row(a, row):
    rid = lax.broadcasted_iota(jnp.int32, a.shape, 0)
    return jnp.sum(jnp.where(rid == row, a, 0.0), axis=0, keepdims=True)


def _conv_taps(yc, halo_yc, first_i):
    keep = (1 - first_i).astype(F32)
    p1 = _pick_row(halo_yc, 15) * keep
    p2 = _pick_row(halo_yc, 14) * keep
    rowid = lax.broadcasted_iota(jnp.int32, yc.shape, 0)
    s1 = jnp.where(rowid == 0, p1, pltpu.roll(yc, 1, 0))
    s2 = jnp.where(rowid == 0, p2, jnp.where(rowid == 1, p1, pltpu.roll(yc, 2, 0)))
    return s1, s2


def _conv_fwd(pconv, conv_w, conv_b, seq):
    T = pconv.shape[0]
    tm = min(256, seq)
    per_seq = seq // tm
    D = D_MODEL

    def kern(cur_ref, halo_ref, w_ref, b_ref, o_ref):
        i = pl.program_id(0)
        first_i = ((i % per_seq) == 0).astype(jnp.int32)
        cb = cur_ref[:, 0:D].astype(F32)
        yc = cur_ref[:, D:2 * D].astype(F32) * cur_ref[:, 2 * D:3 * D].astype(F32)
        halo_yc = halo_ref[:, D:2 * D].astype(F32) * halo_ref[:, 2 * D:3 * D].astype(F32)
        s1, s2 = _conv_taps(yc, halo_yc, first_i)
        z = w_ref[0:1, :] * s2 + w_ref[1:2, :] * s1 + w_ref[2:3, :] * yc
        o_ref[...] = (cb * (z + b_ref[...])).astype(BF16)

    return pl.pallas_call(
        kern, out_shape=jax.ShapeDtypeStruct((T, D), BF16), grid=(T // tm,),
        in_specs=[pl.BlockSpec((tm, CONV_W), lambda i: (i, 0)),
                  pl.BlockSpec((16, CONV_W), lambda i: (jnp.maximum(i * (tm // 16) - 1, 0), 0)),
                  pl.BlockSpec((3, D), lambda i: (0, 0)), pl.BlockSpec((1, D), lambda i: (0, 0))],
        out_specs=pl.BlockSpec((tm, D), lambda i: (i, 0)),
        name="conv_fwd", compiler_params=_params("parallel"))(pconv, pconv, conv_w, conv_b)


def _mix_fwd(x, att, cv, pgate, b_gates, wao, wco, wo):
    T, D = x.shape
    tm = min(512, T)

    def kern(x_ref, att_ref, cv_ref, pg_ref, bg_ref, wao_ref, wco_ref, wo_ref, x1_ref, ya_ref, yc_ref, mg_ref):
        ya = _nn(att_ref[...], wao_ref[...])
        yc = _nn(cv_ref[...], wco_ref[...])
        sa = jax.nn.sigmoid(pg_ref[:, 0:D].astype(F32) + bg_ref[:, 0:D])
        sc = jax.nn.sigmoid(pg_ref[:, D:2 * D].astype(F32) + bg_ref[:, D:2 * D])
        mg = (sa * ya + sc * yc).astype(BF16)
        ya_ref[...] = ya.astype(BF16)
        yc_ref[...] = yc.astype(BF16)
        mg_ref[...] = mg
        x1_ref[...] = x_ref[...] + _nn(mg, wo_ref[...])

    row = lambda w: pl.BlockSpec((tm, w), lambda i: (i, 0))
    full = lambda a, b: pl.BlockSpec((a, b), lambda i: (0, 0))
    bf = jax.ShapeDtypeStruct((T, D), BF16)
    return pl.pallas_call(
        kern, out_shape=(jax.ShapeDtypeStruct((T, D), F32), bf, bf, bf), grid=(T // tm,),
        in_specs=[row(D), row(D), row(D), row(GATE_W), full(1, GATE_W), full(D, D), full(D, D), full(D, D)],
        out_specs=(row(D), row(D), row(D), row(D)),
        name="mix_fwd", compiler_params=_params("parallel"))(x, att, cv, pgate, b_gates, wao, wco, wo)


def _mlp_fwd(x1, g, wup, wdn, comm=None):
    T, D = x1.shape
    tm = min(512, T)
    nj = D_FF // D

    def kern(x_ref, g_ref, wup_ref, wdn_ref, x2_ref, a_ref, h_scr, acc_scr):
        j = pl.program_id(1)

        @pl.when(j == 0)
        def _():
            xf = x_ref[...]
            _, xh = _rms_stats(xf)
            h_scr[...] = (xh * g_ref[...]).astype(BF16)
            acc_scr[...] = xf

        a = _nn(h_scr[...], wup_ref[...])
        a_ref[...] = a.astype(BF16)
        u = jnp.square(jnp.maximum(a, 0.0)).astype(BF16)
        acc_scr[...] += _nn(u, wdn_ref[...])

        @pl.when(j == nj - 1)
        def _():
            x2_ref[...] = acc_scr[...]

    return _hosted_call(
        comm, kern, out_shape=(jax.ShapeDtypeStruct((T, D), F32), jax.ShapeDtypeStruct((T, D_FF), BF16)), grid=(T // tm, nj),
        in_specs=[pl.BlockSpec((tm, D), lambda i, j: (i, 0)), pl.BlockSpec((1, D), lambda i, j: (0, 0)),
                  pl.BlockSpec((D, D), lambda i, j: (0, j)), pl.BlockSpec((D, D), lambda i, j: (j, 0))],
        out_specs=(pl.BlockSpec((tm, D), lambda i, j: (i, 0)), pl.BlockSpec((tm, D), lambda i, j: (i, j))),
        scratch_shapes=[pltpu.VMEM((tm, D), BF16), pltpu.VMEM((tm, D), F32)],
        name="mlp_fwd", sem=("parallel", "arbitrary"), args=(x1, g, wup, wdn))


def _loss_bwd(x, g, tgt):
    T, D = x.shape
    tm = min(512, T)

    def kern(x_ref, g_ref, t_ref, st_ref, dx_ref):
        i = pl.program_id(0)

        @pl.when(i == 0)
        def _():
            st_ref[...] = jnp.zeros_like(st_ref)

        gg = g_ref[...]
        r, xh = _rms_stats(x_ref[...])
        e = xh * gg - t_ref[...]
        part = 0.5 * jnp.sum(jnp.mean(e * e, axis=-1, keepdims=True), axis=0, keepdims=True)
        dx, dg = _rms_bwd(e * (1.0 / D), xh, r, gg)
        dx_ref[...] = dx
        st_ref[0:1, :] += dg
        st_ref[1:2, 0:1] += part

    return pl.pallas_call(
        kern, out_shape=(jax.ShapeDtypeStruct((8, D), F32), jax.ShapeDtypeStruct((T, D), F32)), grid=(T // tm,),
        in_specs=[pl.BlockSpec((tm, D), lambda i: (i, 0)), pl.BlockSpec((1, D), lambda i: (0, 0)),
                  pl.BlockSpec((tm, D), lambda i: (i, 0))],
        out_specs=(pl.BlockSpec((8, D), lambda i: (0, 0)), pl.BlockSpec((tm, D), lambda i: (i, 0))),
        name="loss_bwd", compiler_params=_params("arbitrary"))(x, g, tgt)


def _mlp_bwd(dx2, x1, a, g, wup, wdn, comm=None):
    T, D = x1.shape
    tm = min(512, T)
    nj = D_FF // D

    def kern(dx2_ref, x1_ref, a_ref, g_ref, wup_ref, wdn_ref, dx1_ref, da_ref, h2_ref, dg_ref, dyb_scr, acc_scr):
        i, j = pl.program_id(0), pl.program_id(1)

        @pl.when((i == 0) & (j == 0))
        def _():
            dg_ref[...] = jnp.zeros_like(dg_ref)

        @pl.when(j == 0)
        def _():
            dyb_scr[...] = dx2_ref[...].astype(BF16)
            acc_scr[...] = jnp.zeros_like(acc_scr)

        du = _nt(dyb_scr[...], wdn_ref[...])
        da = (du * (2.0 * jnp.maximum(a_ref[...].astype(F32), 0.0))).astype(BF16)
        da_ref[...] = da
        acc_scr[...] += _nt(da, wup_ref[...])

        @pl.when(j == nj - 1)
        def _():
            gg = g_ref[...]
            r, xh = _rms_stats(x1_ref[...])
            h2_ref[...] = (xh * gg).astype(BF16)
            dx, dg = _rms_bwd(acc_scr[...], xh, r, gg)
            dx1_ref[...] = dx2_ref[...] + dx
            dg_ref[...] += dg

    return _hosted_call(
        comm, kern,
        out_shape=(jax.ShapeDtypeStruct((T, D), F32), jax.ShapeDtypeStruct((T, D_FF), BF16),
                   jax.ShapeDtypeStruct((T, D), BF16), jax.ShapeDtypeStruct((1, D), F32)),
        grid=(T // tm, nj),
        in_specs=[pl.BlockSpec((tm, D), lambda i, j: (i, 0)), pl.BlockSpec((tm, D), lambda i, j: (i, 0)),
                  pl.BlockSpec((tm, D), lambda i, j: (i, j)), pl.BlockSpec((1, D), lambda i, j: (0, 0)),
                  pl.BlockSpec((D, D), lambda i, j: (0, j)), pl.BlockSpec((D, D), lambda i, j: (j, 0))],
        out_specs=(pl.BlockSpec((tm, D), lambda i, j: (i, 0)), pl.BlockSpec((tm, D), lambda i, j: (i, j)),
                   pl.BlockSpec((tm, D), lambda i, j: (i, 0)), pl.BlockSpec((1, D), lambda i, j: (0, 0))),
        scratch_shapes=[pltpu.VMEM((tm, D), BF16), pltpu.VMEM((tm, D), F32)],
        name="mlp_bwd", sem=("arbitrary", "arbitrary"), args=(dx2, x1, a, g, wup, wdn))


def _mix_bwd(dx1, ya, yc, pgate, b_gates, wao, wco, wo):
    T, D = dx1.shape
    tm = min(256, T)

    def kern(dx_ref, ya_ref, yc_ref, pg_ref, bg_ref, wao_ref, wco_ref, wo_ref,
             datt_ref, dcv_ref, dya_ref, dyc_ref, dgt_ref, dbg_ref):
        @pl.when(pl.program_id(0) == 0)
        def _():
            dbg_ref[...] = jnp.zeros_like(dbg_ref)

        dm = _nt(dx_ref[...].astype(BF16), wo_ref[...])
        sa = jax.nn.sigmoid(pg_ref[:, 0:D].astype(F32) + bg_ref[:, 0:D])
        sc = jax.nn.sigmoid(pg_ref[:, D:2 * D].astype(F32) + bg_ref[:, D:2 * D])
        dya = (dm * sa).astype(BF16)
        dyc = (dm * sc).astype(BF16)
        dga = dm * ya_ref[...].astype(F32) * (sa * (1.0 - sa))
        dgc = dm * yc_ref[...].astype(F32) * (sc * (1.0 - sc))
        dya_ref[...] = dya
        dyc_ref[...] = dyc
        dgt_ref[:, 0:D] = dga.astype(BF16)
        dgt_ref[:, D:2 * D] = dgc.astype(BF16)
        dbg_ref[:, 0:D] += jnp.sum(dga, axis=0, keepdims=True)
        dbg_ref[:, D:2 * D] += jnp.sum(dgc, axis=0, keepdims=True)
        datt_ref[...] = _nt(dya, wao_ref[...]).astype(BF16)
        dcv_ref[...] = _nt(dyc, wco_ref[...]).astype(BF16)

    row = lambda w: pl.BlockSpec((tm, w), lambda i: (i, 0))
    full = lambda a, b: pl.BlockSpec((a, b), lambda i: (0, 0))
    bf = jax.ShapeDtypeStruct((T, D), BF16)
    return pl.pallas_call(
        kern,
        out_shape=(bf, bf, bf, bf, jax.ShapeDtypeStruct((T, GATE_W), BF16), jax.ShapeDtypeStruct((1, GATE_W), F32)),
        grid=(T // tm,),
        in_specs=[row(D), row(D), row(D), row(GATE_W), full(1, GATE_W), full(D, D), full(D, D), full(D, D)],
        out_specs=(row(D), row(D), row(D), row(D), row(GATE_W), full(1, GATE_W)),
        name="mix_bwd", compiler_params=_params("arbitrary"))(dx1, ya, yc, pgate, b_gates, wao, wco, wo)


def _conv_bwd(dcv, pconv, conv_w, conv_b, seq):
    T = pconv.shape[0]
    tm = min(256, seq)
    per_seq = seq // tm
    D = D_MODEL
    nb16 = T // 16

    def kern(dcv_ref, dcvn_ref, cur_ref, prev_ref, next_ref, w_ref, b_ref, o_ref, dwb_ref):
        i = pl.program_id(0)

        @pl.when(i == 0)
        def _():
            dwb_ref[...] = jnp.zeros_like(dwb_ref)

        first_i = ((i % per_seq) == 0).astype(jnp.int32)
        keep_next = 1.0 - (((i + 1) % per_seq) == 0).astype(F32)
        cb = cur_ref[:, 0:D].astype(F32)
        cc = cur_ref[:, D:2 * D].astype(F32)
        cu = cur_ref[:, 2 * D:3 * D].astype(F32)
        yc = cc * cu
        halo_yc = prev_ref[:, D:2 * D].astype(F32) * prev_ref[:, 2 * D:3 * D].astype(F32)
        s1, s2 = _conv_taps(yc, halo_yc, first_i)
        w0, w1, w2 = w_ref[0:1, :], w_ref[1:2, :], w_ref[2:3, :]
        z = w0 * s2 + w1 * s1 + w2 * yc
        dcv = dcv_ref[...].astype(F32)
        dz = dcv * cb
        dzn = dcvn_ref[...].astype(F32) * next_ref[:, 0:D].astype(F32) * keep_next
        n1, n2 = _pick_row(dzn, 0), _pick_row(dzn, 1)
        rowid = lax.broadcasted_iota(jnp.int32, dz.shape, 0)
        u1 = jnp.where(rowid == tm - 1, n1, pltpu.roll(dz, tm - 1, 0))
        u2 = jnp.where(rowid == tm - 1, n2, jnp.where(rowid == tm - 2, n1, pltpu.roll(dz, tm - 2, 0)))
        dyc = w2 * dz + w1 * u1 + w0 * u2
        o_ref[:, 0:D] = (dcv * (z + b_ref[...])).astype(BF16)
        o_ref[:, D:2 * D] = (dyc * cu).astype(BF16)
        o_ref[:, 2 * D:3 * D] = (dyc * cc).astype(BF16)
        dwb_ref[0:1, :] += jnp.sum(dz * s2, axis=0, keepdims=True)
        dwb_ref[1:2, :] += jnp.sum(dz * s1, axis=0, keepdims=True)
        dwb_ref[2:3, :] += jnp.sum(dz * yc, axis=0, keepdims=True)
        dwb_ref[3:4, :] += jnp.sum(dz, axis=0, keepdims=True)

    prev_map = lambda i: (jnp.maximum(i * (tm // 16) - 1, 0), 0)
    next_map = lambda i: (jnp.minimum((i + 1) * (tm // 16), nb16 - 1), 0)
    return pl.pallas_call(
        kern, out_shape=(jax.ShapeDtypeStruct((T, CONV_W), BF16), jax.ShapeDtypeStruct((8, D), F32)), grid=(T // tm,),
        in_specs=[pl.BlockSpec((tm, D), lambda i: (i, 0)), pl.BlockSpec((16, D), next_map),
                  pl.BlockSpec((tm, CONV_W), lambda i: (i, 0)), pl.BlockSpec((16, CONV_W), prev_map),
                  pl.BlockSpec((16, CONV_W), next_map),
                  pl.BlockSpec((3, D), lambda i: (0, 0)), pl.BlockSpec((1, D), lambda i: (0, 0))],
        out_specs=(pl.BlockSpec((tm, CONV_W), lambda i: (i, 0)), pl.BlockSpec((8, D), lambda i: (0, 0))),
        name="conv_bwd", compiler_params=_params("arbitrary"))(dcv, dcv, pconv, pconv, pconv, conv_w, conv_b)


def _attn_bwd(pqkv, att, datt, lse, sinks, seq, comm=None):
    T = pqkv.shape[0]
    nblk = seq // WINDOW
    nseq = T // seq
    K0, V0 = N_Q_HEADS * HEAD_DIM, N_Q_HEADS * HEAD_DIM + N_KV_HEADS * HEAD_DIM
    KVW = N_KV_HEADS * HEAD_DIM

    def kern(sink_ref, cur_ref, prev_ref, o_ref, do_ref, lse_ref, dq_ref, dkv_ref, ds_ref, kc_scr, vc_scr):
        b, st = pl.program_id(0), pl.program_id(1)

        @pl.when((b == 0) & (st == 0))
        def _():
            ds_ref[...] = jnp.zeros_like(ds_ref)

        @pl.when(st == 0)
        def _():
            kc_scr[...] = jnp.zeros_like(kc_scr)
            vc_scr[...] = jnp.zeros_like(vc_scr)

        @pl.when(st < nblk)
        def _():
            first_i = (st == 0).astype(jnp.int32)
            distf, valid = _band_masks(first_i)
            od = o_ref[...].astype(F32) * do_ref[...].astype(F32)
            dqs = []
            for hk in range(N_KV_HEADS):
                k_band = jnp.concatenate([prev_ref[:, HEAD_DIM * hk:HEAD_DIM * (hk + 1)],
                                          cur_ref[:, K0 + HEAD_DIM * hk:K0 + HEAD_DIM * (hk + 1)]], axis=0)
                v_band = jnp.concatenate([prev_ref[:, KVW + HEAD_DIM * hk:KVW + HEAD_DIM * (hk + 1)],
                                          cur_ref[:, V0 + HEAD_DIM * hk:V0 + HEAD_DIM * (hk + 1)]], axis=0)
                q_g = _stack_heads(cur_ref, 0, hk)
                do_g = _stack_heads(do_ref, 0, hk)
                heads = [GQA_GROUP * hk + g for g in range(GQA_GROUP)]
                slope = _per_head_col([_SLOPES[h] for h in heads])
                sink = _per_head_col([sink_ref[h] for h in heads])
                lse_g = jnp.concatenate([lse_ref[:, h:h + 1] for h in heads], axis=0)
                d_g = jnp.concatenate(
                    [jnp.sum(od[:, HEAD_DIM * h:HEAD_DIM * (h + 1)], axis=-1, keepdims=True) for h in heads], axis=0)
                s = _nt(q_g, k_band) * ATTN_SCALE + (-slope) * distf
                p = jnp.where(valid, jnp.exp(s - lse_g), 0.0)
                dp = _nt(do_g, v_band)
                dsb = (p * (dp - d_g)).astype(BF16)
                dq_g = _nn(dsb, k_band) * ATTN_SCALE
                dk_b = _tn(dsb, q_g) * ATTN_SCALE
                dv_b = _tn(p.astype(BF16), do_g)
                psd = jnp.exp(sink - lse_g) * d_g
                for g in range(GQA_GROUP):
                    h = heads[g]
                    dqs.append(dq_g[WINDOW * g:WINDOW * (g + 1)].astype(BF16))
                    ds_ref[0:1, h:h + 1] -= jnp.sum(psd[WINDOW * g:WINDOW * (g + 1)], axis=0, keepdims=True)
                ksl = slice(HEAD_DIM * hk, HEAD_DIM * (hk + 1))
                vsl = slice(KVW + HEAD_DIM * hk, KVW + HEAD_DIM * (hk + 1))
                dkv_ref[:, ksl] = (kc_scr[:, ksl] + dk_b[0:WINDOW]).astype(BF16)
                dkv_ref[:, vsl] = (vc_scr[:, ksl] + dv_b[0:WINDOW]).astype(BF16)
                kc_scr[:, ksl] = dk_b[WINDOW:2 * WINDOW]
                vc_scr[:, ksl] = dv_b[WINDOW:2 * WINDOW]
            dq_ref[...] = jnp.concatenate(dqs, axis=1)

        @pl.when(st == nblk)
        def _():
            dkv_ref[:, 0:KVW] = kc_scr[...].astype(BF16)
            dkv_ref[:, KVW:2 * KVW] = vc_scr[...].astype(BF16)

    cur_map = lambda b, s: (b * nblk + jnp.minimum(s, nblk - 1), 0)
    prev_row = lambda b, s: b * nblk + jnp.clip(s - 1, 0, nblk - 1)
    return _hosted_call(
        comm, kern,
        out_shape=(jax.ShapeDtypeStruct((T, D_MODEL), BF16), jax.ShapeDtypeStruct((T, 2 * KVW), BF16),
                   jax.ShapeDtypeStruct((8, 128), F32)),
        grid=(nseq, nblk + 1),
        in_specs=[pl.BlockSpec(memory_space=pltpu.SMEM),
                  pl.BlockSpec((WINDOW, QKV_W), cur_map),
                  pl.BlockSpec((WINDOW, 2 * KVW), lambda b, s: (prev_row(b, s), 2)),
                  pl.BlockSpec((WINDOW, D_MODEL), cur_map), pl.BlockSpec((WINDOW, D_MODEL), cur_map),
                  pl.BlockSpec((WINDOW, N_Q_HEADS), cur_map)],
        out_specs=(pl.BlockSpec((WINDOW, D_MODEL), cur_map),
                   pl.BlockSpec((WINDOW, 2 * KVW), lambda b, s: (prev_row(b, s), 0)),
                   pl.BlockSpec((8, 128), lambda b, s: (0, 0))),
        scratch_shapes=[pltpu.VMEM((WINDOW, KVW), F32), pltpu.VMEM((WINDOW, KVW), F32)],
        name="attn_bwd", sem=("arbitrary", "arbitrary"), args=(sinks, pqkv, pqkv, att, datt, lse))


def _piece_tiles(pieces):
    out, start = [], 0
    for arr, width in pieces:
        out.append((arr, start, width // COL_TILE))
        start += width // COL_TILE
    return out, start


def _inproj_bwd(pieces, w_in, x, dx_in, g, comm=None):
    T, D = x.shape
    tm = min(512, T)
    tiles, nk = _piece_tiles(pieces)

    def kern(*refs):
        p_refs = refs[:len(tiles)]
        w_ref, x_ref, dxin_ref, g_ref, dx_ref, dg_ref, acc_scr = refs[len(tiles):]
        i, k = pl.program_id(0), pl.program_id(1)

        @pl.when((i == 0) & (k == 0))
        def _():
            dg_ref[...] = jnp.zeros_like(dg_ref)

        @pl.when(k == 0)
        def _():
            acc_scr[...] = jnp.zeros_like(acc_scr)

        for p_ref, (_, start, n) in zip(p_refs, tiles):
            @pl.when((k >= start) & (k < start + n))
            def _(p_ref=p_ref):
                acc_scr[...] += _nt(p_ref[...], w_ref[...])

        @pl.when(k == nk - 1)
        def _():
            gg = g_ref[...]
            r, xh = _rms_stats(x_ref[...])
            dx, dg = _rms_bwd(acc_scr[...], xh, r, gg)
            dx_ref[...] = dxin_ref[...] + dx
            dg_ref[...] += dg

    p_specs = [pl.BlockSpec((tm, COL_TILE), lambda i, k, s=start, n=n: (i, jnp.clip(k - s, 0, n - 1))) for _, start, n in tiles]
    return _hosted_call(
        comm, kern, out_shape=(jax.ShapeDtypeStruct((T, D), F32), jax.ShapeDtypeStruct((1, D), F32)), grid=(T // tm, nk),
        in_specs=p_specs + [pl.BlockSpec((D, COL_TILE), lambda i, k: (0, k)),
                            pl.BlockSpec((tm, D), lambda i, k: (i, 0)), pl.BlockSpec((tm, D), lambda i, k: (i, 0)),
                            pl.BlockSpec((1, D), lambda i, k: (0, 0))],
        out_specs=(pl.BlockSpec((tm, D), lambda i, k: (i, 0)), pl.BlockSpec((1, D), lambda i, k: (0, 0))),
        scratch_shapes=[pltpu.VMEM((tm, D), F32)],
        name="inproj_bwd", sem=("arbitrary", "arbitrary"), args=(*[a for a, _, _ in tiles], w_in, x, dx_in, g))


def _dw_pieces(lhs, pieces):
    T, K = lhs.shape
    tk = min(1024, T)
    nt = T // tk
    tiles, nj = _piece_tiles(pieces)

    def kern(*refs):
        lhs_ref = refs[0]
        p_refs = refs[1:1 + len(tiles)]
        o_ref, ob_ref = refs[1 + len(tiles):]
        j, t = pl.program_id(0), pl.program_id(1)

        @pl.when(t == 0)
        def _():
            o_ref[...] = jnp.zeros_like(o_ref)

        for p_ref, (_, start, n) in zip(p_refs, tiles):
            @pl.when((j >= start) & (j < start + n))
            def _(p_ref=p_ref):
                o_ref[...] += _tn(lhs_ref[...], p_ref[...])

        @pl.when(t == nt - 1)
        def _():
            ob_ref[...] = o_ref[...].astype(BF16)

    def p_map(start, n):
        return lambda j, t: (jnp.where((j >= start) & (j < start + n), t, 0), jnp.clip(j - start, 0, n - 1))

    N = nj * COL_TILE
    return pl.pallas_call(
        kern, out_shape=(jax.ShapeDtypeStruct((K, N), F32), jax.ShapeDtypeStruct((K, N), BF16)), grid=(nj, nt),
        in_specs=[pl.BlockSpec((tk, K), lambda j, t: (t, 0))] + [pl.BlockSpec((tk, COL_TILE), p_map(s, n)) for _, s, n in tiles],
        out_specs=(pl.BlockSpec((K, COL_TILE), lambda j, t: (0, j)), pl.BlockSpec((K, COL_TILE), lambda j, t: (0, j))),
        name="dw_pieces", compiler_params=_params("arbitrary", "arbitrary"))(lhs, *[a for a, _, _ in tiles])


def _dw(lhs, rhs, lhs_by_j=False, relu2_lhs=False):
    T = lhs.shape[0]
    tk = min(1024, T)
    nt = T // tk
    W = D_MODEL
    nj = (lhs.shape[1] if lhs_by_j else rhs.shape[1]) // W

    def kern(lhs_ref, rhs_ref, o_ref, ob_ref):
        t = pl.program_id(1)

        @pl.when(t == 0)
        def _():
            o_ref[...] = jnp.zeros_like(o_ref)

        a = lhs_ref[...]
        if relu2_lhs:
            a = jnp.square(jnp.maximum(a.astype(F32), 0.0)).astype(BF16)
        o_ref[...] += _tn(a, rhs_ref[...].astype(BF16))

        @pl.when(t == nt - 1)
        def _():
            ob_ref[...] = o_ref[...].astype(BF16)

    if lhs_by_j:
        shape, lmap, rmap, omap = (nj * W, W), (lambda j, t: (t, j)), (lambda j, t: (t, 0)), (lambda j, t: (j, 0))
    else:
        shape, lmap, rmap, omap = (W, nj * W), (lambda j, t: (t, 0)), (lambda j, t: (t, j)), (lambda j, t: (0, j))
    return pl.pallas_call(
        kern, out_shape=(jax.ShapeDtypeStruct(shape, F32), jax.ShapeDtypeStruct(shape, BF16)), grid=(nj, nt),
        in_specs=[pl.BlockSpec((tk, W), lmap), pl.BlockSpec((tk, W), rmap)],
        out_specs=(pl.BlockSpec((W, W), omap), pl.BlockSpec((W, W), omap)),
        name="dw", compiler_params=_params("arbitrary", "arbitrary"))(lhs, rhs)


BIG = (("w_in", D_MODEL, IN_COLS, "col"), ("w_attn_out", D_MODEL, D_MODEL, "row"), ("w_conv_out", D_MODEL, D_MODEL, "row"),
       ("w_o", D_MODEL, D_MODEL, "row"), ("w_up", D_MODEL, D_FF, "col"), ("w_down", D_FF, D_MODEL, "row"))


def _shard_dims(rows, cols, kind):
    return (rows, cols // N_CHIP) if kind == "col" else (rows // N_CHIP, cols)


def _window(ref, rows, cols, kind, chip, half):
    sr, sc = _shard_dims(rows, cols, kind)
    hr = sr // 2
    if kind == "col":
        return ref.at[pl.ds(half * hr, hr), pl.ds(chip * sc, sc)]
    return ref.at[pl.ds(chip * sr + half * hr, hr), :]


def _mesh_pos():
    x, y, c = lax.axis_index("x"), lax.axis_index("y"), lax.axis_index("c")
    return x, y, c, 2 * x + y


_REL_BITS = (2, 1, 3)


def _rel_dev(x, y, c, r):
    return ((1 - x, y, c), (x, 1 - y, c), (1 - x, 1 - y, c))[r]


def _for_my_chip(j, fn):
    for js in range(N_CHIP):
        pl.when(j == js)(functools.partial(fn, js))


def _cast_into_full(j_arr, shard, l, rows, cols, kind):
    sr, sc = _shard_dims(rows, cols, kind)
    tr = min(256, sr)

    def kern(j_ref, s_ref, o_ref):
        o_ref[...] = s_ref[...].astype(BF16)

    if kind == "col":
        omap = lambda i, j_ref: (i, j_ref[0])
    else:
        omap = lambda i, j_ref: (j_ref[0] * (sr // tr) + i, 0)
    gs = pltpu.PrefetchScalarGridSpec(
        num_scalar_prefetch=1, grid=(sr // tr,),
        in_specs=[pl.BlockSpec((None, tr, sc), lambda i, j_ref: (l, i, 0))], out_specs=pl.BlockSpec((tr, sc), omap))
    return pl.pallas_call(kern, out_shape=jax.ShapeDtypeStruct((rows, cols), BF16), grid_spec=gs, name="cast_into_full",
                          compiler_params=_params("arbitrary"))(j_arr, shard)


def _gather(fulls, cw=None):
    comm = _gather_comm(fulls, cw)
    n = len(comm.inputs)

    def body(*refs):
        comm.start(refs[:n], refs[n:2 * n], refs[2 * n:])
        comm.finish(refs[:n], refs[n:2 * n], refs[2 * n:])

    return pl.pallas_call(
        body, out_shape=comm.out_shape, in_specs=[ANY] * n, out_specs=[ANY] * n, input_output_aliases=comm.aliases,
        scratch_shapes=comm.scratch, name="gather_weights")(*comm.inputs)


def _gather_comm(fulls, cw=None):
    n_big = len(fulls)
    n_piece = n_big + (0 if cw is None else 1)

    def pieces(in_refs, o_refs, js, c):
        def piece(p, chip, half):
            if p == n_big:
                return o_refs[p].at[half, chip]
            _, rows, cols, kind = fulls[p]
            return _window(o_refs[p], rows, cols, kind, chip, half)

        def mine(p):
            return in_refs[p].at[c] if p == n_big else piece(p, js, c)

        return piece, mine

    def local_copies(in_refs, piece, js, loc_sem):
        if cw is None:
            return []
        return [pltpu.make_async_copy(in_refs[n_big].at[half], piece(n_big, js, half), loc_sem.at[half]) for half in range(2)]

    def ici_copy(piece, mine, js, x, y, c, r, p, send_sem, recv_sem):
        return pltpu.make_async_remote_copy(mine(p), piece(p, js, c), send_sem.at[r * n_piece + p], recv_sem.at[r * n_piece + p],
                                            _rel_dev(x, y, c, r), MESH)

    def start(in_refs, o_refs, sems):
        send_sem, recv_sem, _, _, loc_sem = sems
        x, y, c, j = _mesh_pos()

        def run(js):
            piece, mine = pieces(in_refs, o_refs, js, c)
            for cp in local_copies(in_refs, piece, js, loc_sem):
                cp.start()
            for r in range(3):
                for p in range(n_piece):
                    ici_copy(piece, mine, js, x, y, c, r, p, send_sem, recv_sem).start()

        _for_my_chip(j, run)

    def finish(in_refs, o_refs, sems):
        send_sem, recv_sem, fsend_sem, frecv_sem, loc_sem = sems
        x, y, c, j = _mesh_pos()

        def run(js):
            piece, mine = pieces(in_refs, o_refs, js, c)
            fwds = []
            for r in range(3):
                ks = js ^ _REL_BITS[r]
                for p in range(n_piece):
                    got = piece(p, ks, c)
                    pltpu.make_async_remote_copy(got, got, send_sem.at[r * n_piece + p], recv_sem.at[r * n_piece + p],
                                                 _rel_dev(x, y, c, r), MESH).wait_recv()
                    cp = pltpu.make_async_remote_copy(got, got, fsend_sem.at[r * n_piece + p], frecv_sem.at[r * n_piece + p],
                                                      (x, y, 1 - c), MESH)
                    cp.start()
                    fwds.append(cp)
            for r in range(3):
                ks = js ^ _REL_BITS[r]
                for p in range(n_piece):
                    got = piece(p, ks, 1 - c)
                    pltpu.make_async_remote_copy(got, got, fsend_sem.at[r * n_piece + p], frecv_sem.at[r * n_piece + p],
                                                 (x, y, 1 - c), MESH).wait_recv()
            for r in range(3):
                for p in range(n_piece):
                    ici_copy(piece, mine, js, x, y, c, r, p, send_sem, recv_sem).wait_send()
            for cp in fwds:
                cp.wait_send()
            for cp in local_copies(in_refs, piece, js, loc_sem):
                cp.wait()

        _for_my_chip(j, run)

    out_shape = [jax.ShapeDtypeStruct(a.shape, BF16) for a, _, _, _ in fulls]
    ins = [a for a, _, _, _ in fulls]
    if cw is not None:
        out_shape.append(jax.ShapeDtypeStruct((DEPTH, N_CHIP, 3, D_MODEL // N_CHIP), F32))
        ins.append(cw)
    scratch = [pltpu.SemaphoreType.DMA((3 * n_piece,))] * 4 + [pltpu.SemaphoreType.DMA((2,))]
    return _Hosted(ins, out_shape, {p: p for p in range(n_big)}, scratch, start, finish)


def _sibling_exchange(gb):
    n = len(gb)

    def body(*refs):
        g_refs, o_refs = refs[:n], refs[n:2 * n]
        send_sem, recv_sem = refs[2 * n:]
        x, y, c, _ = _mesh_pos()
        cps = []
        for t in range(n):
            _, rows, cols, kind = gb[t]
            for chip in range(N_CHIP):
                out_w = _window(g_refs[t], rows, cols, kind, chip, 1 - c)
                dst_w = _window(o_refs[t], rows, cols, kind, chip, 1 - c)
                cp = pltpu.make_async_remote_copy(out_w, dst_w, send_sem.at[N_CHIP * t + chip], recv_sem.at[N_CHIP * t + chip],
                                                  (x, y, 1 - c), MESH)
                cp.start()
                cps.append(cp)
        for t in range(n):
            _, rows, cols, kind = gb[t]
            for chip in range(N_CHIP):
                w = _window(o_refs[t], rows, cols, kind, chip, c)
                pltpu.make_async_remote_copy(w, w, send_sem.at[N_CHIP * t + chip], recv_sem.at[N_CHIP * t + chip],
                                             (x, y, 1 - c), MESH).wait_recv()
        for cp in cps:
            cp.wait_send()

    return pl.pallas_call(
        body, out_shape=[jax.ShapeDtypeStruct(a.shape, BF16) for a, _, _, _ in gb], in_specs=[ANY] * n, out_specs=[ANY] * n,
        scratch_shapes=[pltpu.SemaphoreType.DMA((N_CHIP * n,))] * 2, name="grad_sibling_exchange")(*[a for a, _, _, _ in gb])


def _half_add(c_arr, g, sib, rows, cols, kind):
    sr, sc = _shard_dims(rows, cols, kind)
    hr = sr // 2

    def kern(c_ref, g_ref, s_ref, ob_ref, of_ref):
        v = g_ref[...] + s_ref[...].astype(F32)
        of_ref[...] = v
        ob_ref[...] = v.astype(BF16)

    if kind == "col":
        imap = lambda j, c_ref: (c_ref[0], j)
    else:
        imap = lambda j, c_ref: (2 * j + c_ref[0], 0)
    gs = pltpu.PrefetchScalarGridSpec(
        num_scalar_prefetch=1, grid=(N_CHIP,),
        in_specs=[pl.BlockSpec((hr, sc), imap), pl.BlockSpec((hr, sc), imap)],
        out_specs=[pl.BlockSpec((None, hr, sc), lambda j, c_ref: (j, 0, 0))] * 2)
    return pl.pallas_call(
        kern, out_shape=(jax.ShapeDtypeStruct((N_CHIP, hr, sc), BF16), jax.ShapeDtypeStruct((N_CHIP, hr, sc), F32)),
        grid_spec=gs, name="grad_half_add", compiler_params=_params("arbitrary"))(c_arr, g, sib)


def _chip_exchange(sbs):
    comm = _chip_exchange_comm(sbs)
    n = len(sbs)

    def body(*refs):
        comm.start(refs[:n], refs[n:2 * n], refs[2 * n:])
        comm.finish(refs[:n], refs[n:2 * n], refs[2 * n:])

    return pl.pallas_call(
        body, out_shape=comm.out_shape, in_specs=[ANY] * n, out_specs=[ANY] * n, scratch_shapes=comm.scratch,
        name="grad_chip_exchange")(*sbs)


def _chip_exchange_comm(sbs):
    n = len(sbs)

    def copies(s_refs, o_refs, sems):
        send_sem, recv_sem = sems
        x, y, c, j = _mesh_pos()
        return [pltpu.make_async_remote_copy(s_refs[t].at[j ^ _REL_BITS[r]], o_refs[t].at[r], send_sem.at[n * r + t],
                                             recv_sem.at[n * r + t], _rel_dev(x, y, c, r), MESH)
                for r in range(3) for t in range(n)]

    def start(s_refs, o_refs, sems):
        for cp in copies(s_refs, o_refs, sems):
            cp.start()

    def finish(s_refs, o_refs, sems):
        for cp in copies(s_refs, o_refs, sems):
            cp.wait()

    return _Hosted(sbs, [jax.ShapeDtypeStruct((3,) + a.shape[1:], BF16) for a in sbs], {},
                   [pltpu.SemaphoreType.DMA((3 * n,))] * 2, start, finish)


def _owner_sum(jc_arr, sf, rb, l, into=None):
    _, hr, sc = sf.shape

    def kern(jc_ref, s_ref, r0_ref, r1_ref, r2_ref, *rest):
        o_ref = rest[-1]
        o_ref[...] = ((s_ref[...] + r0_ref[...].astype(F32)) + r1_ref[...].astype(F32)) + r2_ref[...].astype(F32)

    in_specs = [pl.BlockSpec((None, hr, sc), lambda i, jc_ref: (jc_ref[0], 0, 0))]
    in_specs += [pl.BlockSpec((None, hr, sc), lambda i, jc_ref, r=r: (r, 0, 0)) for r in range(3)]
    args = [jc_arr, sf, rb, rb, rb]
    aliases = {}
    if into is not None:
        in_specs.append(ANY)
        args.append(into)
        aliases = {len(args) - 1: 0}
    gs = pltpu.PrefetchScalarGridSpec(
        num_scalar_prefetch=1, grid=(1,), in_specs=in_specs,
        out_specs=pl.BlockSpec((None, hr, sc), lambda i, jc_ref: (l, jc_ref[1], 0)))
    return pl.pallas_call(kern, out_shape=jax.ShapeDtypeStruct((DEPTH, 2 * hr, sc), F32), grid_spec=gs,
                          input_output_aliases=aliases, name="grad_owner_sum", compiler_params=_params("arbitrary"))(*args)


def _sibling_assemble(grads):
    n = len(grads)

    def body(*refs):
        o_refs = refs[n:2 * n]
        send_sem, recv_sem = refs[2 * n:]
        x, y, c, _ = _mesh_pos()
        cps = []
        for q in range(n):
            hr = grads[q].shape[1] // 2
            for l in range(DEPTH):
                mine = o_refs[q].at[l, pl.ds(c * hr, hr), :]
                cp = pltpu.make_async_remote_copy(mine, mine, send_sem.at[DEPTH * q + l], recv_sem.at[DEPTH * q + l],
                                                  (x, y, 1 - c), MESH)
                cp.start()
                cps.append(cp)
        for q in range(n):
            hr = grads[q].shape[1] // 2
            for l in range(DEPTH):
                theirs = o_refs[q].at[l, pl.ds((1 - c) * hr, hr), :]
                pltpu.make_async_remote_copy(theirs, theirs, send_sem.at[DEPTH * q + l], recv_sem.at[DEPTH * q + l],
                                             (x, y, 1 - c), MESH).wait_recv()
        for cp in cps:
            cp.wait_send()

    return pl.pallas_call(
        body, out_shape=[jax.ShapeDtypeStruct(g.shape, F32) for g in grads], in_specs=[ANY] * n, out_specs=[ANY] * n,
        input_output_aliases={q: q for q in range(n)},
        scratch_shapes=[pltpu.SemaphoreType.DMA((DEPTH * n,))] * 2, name="grad_sibling_assemble")(*grads)


def _adamw_math(w, g, m, v):
    m = ADAM_B1 * m + (1.0 - ADAM_B1) * g
    v = ADAM_B2 * v + (1.0 - ADAM_B2) * jnp.square(g)
    m_hat = m / (1.0 - ADAM_B1 ** ADAM_STEP)
    v_hat = v / (1.0 - ADAM_B2 ** ADAM_STEP)
    delta = -ADAM_LR * (m_hat / (jnp.sqrt(v_hat) + ADAM_EPS) + ADAM_WD * w)
    return delta, m, v


def _adamw(w, g, m, v):
    shape = w.shape
    C = shape[-1]
    R = int(np.prod(shape[:-1]))
    tr = min(256, R)
    args = [a.reshape(R, C) for a in (w, g, m, v)]

    def kern(w_ref, g_ref, m_ref, v_ref, d_ref, nm_ref, nv_ref):
        d, nm, nv = _adamw_math(w_ref[...], g_ref[...], m_ref[...], v_ref[...])
        d_ref[...] = d
        nm_ref[...] = nm
        nv_ref[...] = nv

    spec = pl.BlockSpec((tr, C), lambda i: (i, 0))
    outs = pl.pallas_call(
        kern, out_shape=[jax.ShapeDtypeStruct((R, C), F32)] * 3, grid=(R // tr,), in_specs=[spec] * 4, out_specs=[spec] * 3,
        name="adamw", compiler_params=_params("parallel"))(*args)
    return [o.reshape(shape) for o in outs]


_ROW_G_MIX, _ROW_B_GATES, _ROW_SINKS, _ROW_CONV, _ROW_G_MLP, _ROW_G_FINAL, _ROW_LOSS = 0, 2, 6, 8, 16, 18, 19


def _small_step(parts, params, moms, vels):
    names = ["g_mix", "b_gates", "sinks", "conv_w", "conv_b", "g_mlp", "g_final"]
    D = D_MODEL
    QW = D // N_CHIP
    n_dev = 8

    def body(*refs):
        it = iter(refs)
        dgmix = [next(it) for _ in range(DEPTH)]
        dbg = [next(it) for _ in range(DEPTH)]
        dsk = [next(it) for _ in range(DEPTH)]
        dwb = [next(it) for _ in range(DEPTH)]
        dgmlp = [next(it) for _ in range(DEPTH)]
        lst = next(it)
        p_refs = {n: next(it) for n in names}
        m_refs = {n: next(it) for n in names}
        v_refs = {n: next(it) for n in names}
        loss_ref = next(it)
        outs = {n: [next(it) for _ in range(4)] for n in names}
        pack_ref, all_ref, send_sem, recv_sem = next(it), next(it), next(it), next(it)

        x, y, c, j = _mesh_pos()
        me = 4 * x + 2 * y + c
        pack_ref[...] = jnp.zeros_like(pack_ref)
        for l in range(DEPTH):
            pack_ref[_ROW_G_MIX + l:_ROW_G_MIX + l + 1, :] = dgmix[l][...]
            pack_ref[_ROW_B_GATES + 2 * l:_ROW_B_GATES + 2 * l + 1, :] = dbg[l][:, 0:D]
            pack_ref[_ROW_B_GATES + 2 * l + 1:_ROW_B_GATES + 2 * l + 2, :] = dbg[l][:, D:2 * D]
            pack_ref[_ROW_SINKS + l:_ROW_SINKS + l + 1, 0:128] = dsk[l][0:1, :]
            pack_ref[_ROW_CONV + 4 * l:_ROW_CONV + 4 * l + 4, :] = dwb[l][0:4, :]
            pack_ref[_ROW_G_MLP + l:_ROW_G_MLP + l + 1, :] = dgmlp[l][...]
        pack_ref[_ROW_G_FINAL:_ROW_G_FINAL + 1, :] = lst[0:1, :]
        pack_ref[_ROW_LOSS:_ROW_LOSS + 1, :] = lst[1:2, :]

        all_ref[me] = pack_ref[...]
        cps = []
        for k in range(1, n_dev):
            dx_, dy_, dc_ = (k >> 2) & 1, (k >> 1) & 1, k & 1
            peer = (x ^ dx_, y ^ dy_, c ^ dc_)
            cp = pltpu.make_async_remote_copy(pack_ref, all_ref.at[me], send_sem.at[k - 1], recv_sem.at[k - 1], peer, MESH)
            cp.start()
            cps.append(cp)
        for cp in cps:
            cp.wait()

        tot = all_ref[0]
        for d in range(1, n_dev):
            tot = tot + all_ref[d]
        pack_ref[...] = tot

        loss_ref[...] = pack_ref[_ROW_LOSS:_ROW_LOSS + 1, 0:1]

        def finish(name, idx, g):
            w, m, v = p_refs[name][idx], m_refs[name][idx], v_refs[name][idx]
            d, nm, nv = _adamw_math(w, g, m, v)
            for ref, val in zip(outs[name], (g, d, nm, nv)):
                ref[idx] = val

        for l in range(DEPTH):
            finish("g_mix", (slice(l, l + 1), slice(None)), pack_ref[_ROW_G_MIX + l:_ROW_G_MIX + l + 1, :])
            finish("g_mlp", (slice(l, l + 1), slice(None)), pack_ref[_ROW_G_MLP + l:_ROW_G_MLP + l + 1, :])
            finish("conv_b", (slice(l, l + 1), slice(None)), pack_ref[_ROW_CONV + 4 * l + 3:_ROW_CONV + 4 * l + 4, :])
            finish("sinks", (slice(l, l + 1), slice(None)), pack_ref[_ROW_SINKS + l:_ROW_SINKS + l + 1, 0:N_Q_HEADS])
            for hf in range(2):
                finish("b_gates", (slice(l, l + 1), slice(hf * D, (hf + 1) * D)),
                       pack_ref[_ROW_B_GATES + 2 * l + hf:_ROW_B_GATES + 2 * l + hf + 1, :])
        finish("g_final", (slice(0, 1), slice(None)), pack_ref[_ROW_G_FINAL:_ROW_G_FINAL + 1, :])

        def conv_w_chip(js):
            for l in range(DEPTH):
                for k in range(3):
                    row = _ROW_CONV + 4 * l + k
                    finish("conv_w", (l, slice(k, k + 1), slice(None)), pack_ref[row:row + 1, js * QW:(js + 1) * QW])

        _for_my_chip(j, conv_w_chip)

    vm = pl.BlockSpec(memory_space=pltpu.VMEM)
    ins = (parts["g_mix"] + parts["b_gates"] + parts["sinks"] + parts["conv"] + parts["g_mlp"] + [parts["loss"]]
           + [params[n] for n in names] + [moms[n] for n in names] + [vels[n] for n in names])
    out_shape = [jax.ShapeDtypeStruct((1, 1), F32)]
    for n in names:
        out_shape += [jax.ShapeDtypeStruct(params[n].shape, F32)] * 4
    res = pl.pallas_call(
        body, out_shape=out_shape, in_specs=[vm] * len(ins), out_specs=[vm] * len(out_shape),
        scratch_shapes=[pltpu.VMEM((SMALL_ROWS, D), F32), pltpu.VMEM((n_dev, SMALL_ROWS, D), F32),
                        pltpu.SemaphoreType.DMA((n_dev - 1,)), pltpu.SemaphoreType.DMA((n_dev - 1,))],
        name="small_allreduce_adamw")(*ins)
    loss = res[0]
    out = {n: res[1 + 4 * i:5 + 4 * i] for i, n in enumerate(names)}
    return loss, out


def kernel(x, g_mix, w_in, b_gates, sinks, w_attn_out, conv_w, conv_b, w_conv_out, w_o, g_mlp, w_up, w_down, g_final, loss_target, m_g_mix, m_w_in, m_b_gates, m_sinks, m_w_attn_out, m_conv_w, m_conv_b, m_w_conv_out, m_w_o, m_g_mlp, m_w_up, m_w_down, m_g_final, v_g_mix, v_w_in, v_b_gates, v_sinks, v_w_attn_out, v_conv_w, v_conv_b, v_w_conv_out, v_w_o, v_g_mlp, v_w_up, v_w_down, v_g_final):
    B, S, D = x.shape
    T = B * S
    big_w = dict(w_in=w_in, w_attn_out=w_attn_out, w_conv_out=w_conv_out, w_o=w_o, w_up=w_up, w_down=w_down)
    big_m = dict(w_in=m_w_in, w_attn_out=m_w_attn_out, w_conv_out=m_w_conv_out, w_o=m_w_o, w_up=m_w_up, w_down=m_w_down)
    big_v = dict(w_in=v_w_in, w_attn_out=v_w_attn_out, w_conv_out=v_w_conv_out, w_o=v_w_o, w_up=v_w_up, w_down=v_w_down)

    c_arr = lax.axis_index("c").astype(jnp.int32).reshape(1)
    j_arr = (2 * lax.axis_index("x") + lax.axis_index("y")).astype(jnp.int32).reshape(1)
    jc_arr = jnp.concatenate([j_arr, c_arr])
    order = [(n, l) for n, _, _, _ in BIG for l in range(DEPTH)]
    dims = {n: (r, c_, k) for n, r, c_, k in BIG}

    full = {(n, l): _cast_into_full(j_arr, big_w[n], l, *dims[n]) for n, l in order}
    mixers = ("w_attn_out", "w_conv_out", "w_o")

    def gather_of(keys):
        return _gather_comm([(full[k],) + dims[k[0]] for k in keys])

    carried = {("proj", 0): [(n, 0) for n in mixers], ("attn", 0): [("w_up", 0), ("w_down", 0)],
               ("mlp", 0): [("w_in", 1)] + [(n, 1) for n in mixers], ("attn", 1): [("w_up", 1), ("w_down", 1)]}

    def carry(fn, where, *args):
        keys = carried.get(where)
        if keys is None:
            return fn(*args)
        res, got = fn(*args, comm=gather_of(keys))
        full.update(zip(keys, got))
        return res

    first = _gather([(full["w_in", 0],) + dims["w_in"]], conv_w)
    full["w_in", 0] = first[0]
    conv_w_full = jnp.transpose(first[1], (0, 2, 1, 3)).reshape(DEPTH, 3, D)

    xs = [x.reshape(T, D)]
    saved = []
    for l in range(DEPTH):
        h = _rms_fwd(xs[-1], g_mix[l:l + 1])
        pqkv = _proj(h, full["w_in", l], 0, QKV_W, QKV_W)
        pconv = carry(_proj, ("proj", l), h, full["w_in", l], QKV_W, CONV_W, QKV_W)
        pgate = _proj(h, full["w_in", l], QKV_W + CONV_W, GATE_W, COL_TILE)
        att, lse = carry(_attn_fwd, ("attn", l), pqkv, sinks[l], S)
        cv = _conv_fwd(pconv, conv_w_full[l], conv_b[l:l + 1], S)
        x1, ya, yc, mg = _mix_fwd(xs[-1], att, cv, pgate, b_gates[l:l + 1], full["w_attn_out", l], full["w_conv_out", l],
                                  full["w_o", l])
        x2, a = carry(_mlp_fwd, ("mlp", l), x1, g_mlp[l:l + 1], full["w_up", l], full["w_down", l])
        saved.append(dict(h=h, pqkv=pqkv, pconv=pconv, pgate=pgate, att=att, lse=lse, cv=cv, x1=x1, ya=ya, yc=yc, mg=mg, a=a))
        xs.append(x2)

    loss_stats, dx = _loss_bwd(xs[-1], g_final.reshape(1, D), loss_target.reshape(T, D))

    parts = dict(g_mix=[None] * DEPTH, b_gates=[None] * DEPTH, sinks=[None] * DEPTH, conv=[None] * DEPTH,
                 g_mlp=[None] * DEPTH, loss=loss_stats)
    gf, gb = {}, {}
    pre, got = {}, {}

    def pre_reduce(keys):
        sib = _sibling_exchange([(gb[k],) + dims[k[0]] for k in keys])
        for k, s in zip(keys, sib):
            pre[k] = _half_add(c_arr, gf[k], s, *dims[k[0]])
        return keys

    def exchange_in(fn, keys, *args):
        res, arrived = fn(*args, comm=_chip_exchange_comm([pre[k][0] for k in keys]))
        got.update(zip(keys, arrived))
        return res

    pending = None
    for l in reversed(range(DEPTH)):
        W = {n: full[(n, l)] for n in big_w}
        sv = saved[l]
        mlp_args = (dx, sv["x1"], sv["a"], g_mlp[l:l + 1], W["w_up"], W["w_down"])
        dx1, da, h2, parts["g_mlp"][l] = exchange_in(_mlp_bwd, pending, *mlp_args) if pending else _mlp_bwd(*mlp_args)
        gf["w_up", l], gb["w_up", l] = _dw(h2, da)
        gf["w_down", l], gb["w_down", l] = _dw(sv["a"], dx, lhs_by_j=True, relu2_lhs=True)
        datt, dcv, dya, dyc, dgate, parts["b_gates"][l] = _mix_bwd(
            dx1, sv["ya"], sv["yc"], sv["pgate"], b_gates[l:l + 1], W["w_attn_out"], W["w_conv_out"], W["w_o"])
        gf["w_o", l], gb["w_o", l] = _dw(sv["mg"], dx1)
        gf["w_attn_out", l], gb["w_attn_out", l] = _dw(sv["att"], dya)
        gf["w_conv_out", l], gb["w_conv_out", l] = _dw(sv["cv"], dyc)
        dconv, parts["conv"][l] = _conv_bwd(dcv, sv["pconv"], conv_w_full[l], conv_b[l:l + 1], S)
        attn_args = (sv["pqkv"], sv["att"], datt, sv["lse"], sinks[l], S)
        last = l == 0
        if last:
            keys = pre_reduce([(n, l) for n in mixers + ("w_up", "w_down")])
            dq, dkv, parts["sinks"][l] = exchange_in(_attn_bwd, keys, *attn_args)
        else:
            dq, dkv, parts["sinks"][l] = _attn_bwd(*attn_args)
        pieces = [(dq, D), (dkv, QKV_W - D), (dconv, CONV_W), (dgate, GATE_W)]
        gf["w_in", l], gb["w_in", l] = _dw_pieces(sv["h"], pieces)
        in_args = (pieces, W["w_in"], xs[l], dx1, g_mix[l:l + 1])
        if last:
            dx, parts["g_mix"][l] = exchange_in(_inproj_bwd, pre_reduce([("w_in", l)]), *in_args)
        else:
            dx, parts["g_mix"][l] = _inproj_bwd(*in_args)
            pending = pre_reduce([(n, l) for n, _, _, _ in BIG])

    mine = {}
    for n, l in order:
        mine[n] = _owner_sum(jc_arr, pre[n, l][1], got[n, l], l, mine.get(n))
    grads = dict(zip(mine, _sibling_assemble(list(mine.values()))))

    res = {}
    for n in big_w:
        d, nm, nv = _adamw(big_w[n], grads[n], big_m[n], big_v[n])
        res[n] = (grads[n], d, nm, nv)

    small_p = dict(g_mix=g_mix, b_gates=b_gates, sinks=sinks, conv_w=conv_w, conv_b=conv_b, g_mlp=g_mlp, g_final=g_final.reshape(1, D))
    small_m = dict(g_mix=m_g_mix, b_gates=m_b_gates, sinks=m_sinks, conv_w=m_conv_w, conv_b=m_conv_b, g_mlp=m_g_mlp,
                   g_final=m_g_final.reshape(1, D))
    small_v = dict(g_mix=v_g_mix, b_gates=v_b_gates, sinks=v_sinks, conv_w=v_conv_w, conv_b=v_conv_b, g_mlp=v_g_mlp,
                   g_final=v_g_final.reshape(1, D))
    loss, small = _small_step(parts, small_p, small_m, small_v)
    for n, vals in small.items():
        res[n] = tuple(v.reshape(D) for v in vals) if n == "g_final" else tuple(vals)

    weights = ["g_mix", "w_in", "b_gates", "sinks", "w_attn_out", "conv_w", "conv_b", "w_conv_out", "w_o", "g_mlp", "w_up",
               "w_down", "g_final"]
    out = [loss.reshape(()), dx.reshape(B, S, D)]
    for k in range(4):
        out += [res[n][k] for n in weights]
    return tuple(out)
```

```python
import functools

import numpy as np
import jax
import jax.numpy as jnp
from jax import lax
from jax.experimental import pallas as pl
from jax.experimental.pallas import tpu as pltpu

F32 = jnp.float32
BF16 = jnp.bfloat16

D_MODEL = 1024
HEAD_DIM = 64
N_Q_HEADS = 16
N_KV_HEADS = 4
GQA_GROUP = 4
WINDOW = 128
D_FF = 4096
DEPTH = 2
RMS_EPS = 1e-6
NEG_INF = -1e30
ATTN_SCALE = HEAD_DIM ** -0.5
QKV_W = 1536
CONV_W = 3072
GATE_W = 2048
IN_COLS = QKV_W + CONV_W + GATE_W
COL_TILE = 512
N_CHIP = 4
ADAM_LR = 0.001
ADAM_B1 = 0.9
ADAM_B2 = 0.999
ADAM_EPS = 1e-08
ADAM_WD = 0.01
ADAM_STEP = 10
V7X_VMEM_BYTES = 64 * 2 ** 20
VMEM_LIMIT = V7X_VMEM_BYTES - 8 * 2 ** 20
MESH = pl.DeviceIdType.MESH
ANY = pl.BlockSpec(memory_space=pl.ANY)
SMALL_ROWS = 24

_SLOPES = [float(v) for v in np.power(np.float32(2.0), -8.0 * np.arange(1, N_Q_HEADS + 1, dtype=np.float32) / N_Q_HEADS)]


def _params(*sem):
    return pltpu.CompilerParams(dimension_semantics=sem, vmem_limit_bytes=VMEM_LIMIT)


class _Hosted:
    def __init__(self, inputs, out_shape, aliases, scratch, start, finish):
        self.inputs, self.out_shape, self.aliases, self.scratch = list(inputs), list(out_shape), dict(aliases), list(scratch)
        self.start, self.finish = start, finish


def _hosted_call(comm, kern, *, out_shape, grid, in_specs, out_specs, args, name, sem, scratch_shapes=()):
    single = not isinstance(out_shape, (tuple, list))
    outs = [out_shape] if single else list(out_shape)
    ospecs = [out_specs] if single else list(out_specs)
    if comm is None:
        res = pl.pallas_call(kern, out_shape=outs, grid=grid, in_specs=list(in_specs), out_specs=ospecs,
                             scratch_shapes=list(scratch_shapes), name=name, compiler_params=_params(*sem))(*args)
        return res[0] if single else res
    n_in, n_out, n_scr = len(args), len(outs), len(scratch_shapes)
    ci, co, cs = len(comm.inputs), len(comm.out_shape), len(comm.scratch)

    def body(*refs):
        cuts = np.cumsum([0, n_in, ci, n_out, co, n_scr, cs])
        a, b, c, d, e, f = [refs[lo:hi] for lo, hi in zip(cuts[:-1], cuts[1:])]
        ids = [pl.program_id(k) for k in range(len(grid))]
        first = functools.reduce(jnp.logical_and, [i == 0 for i in ids])
        last = functools.reduce(jnp.logical_and, [i == n - 1 for i, n in zip(ids, grid)])
        pl.when(first)(lambda: comm.start(b, d, f))
        kern(*a, *c, *e)
        pl.when(last)(lambda: comm.finish(b, d, f))

    res = pl.pallas_call(
        body, out_shape=outs + comm.out_shape, grid=grid, in_specs=list(in_specs) + [ANY] * ci, out_specs=ospecs + [ANY] * co,
        scratch_shapes=list(scratch_shapes) + comm.scratch,
        input_output_aliases={n_in + i: n_out + o for i, o in comm.aliases.items()},
        name=name + "_carrier", compiler_params=_params(*(["arbitrary"] * len(grid))))(*args, *comm.inputs)
    main = res[:n_out]
    return (main[0] if single else main), res[n_out:]


def _nt(a, b):
    return lax.dot_general(a, b, (((1,), (1,)), ((), ())), preferred_element_type=F32)


def _tn(a, b):
    return lax.dot_general(a, b, (((0,), (0,)), ((), ())), preferred_element_type=F32)


def _nn(a, b):
    return jnp.dot(a, b, preferred_element_type=F32)


def _rms_stats(xf):
    r = lax.rsqrt(jnp.mean(xf * xf, axis=-1, keepdims=True) + RMS_EPS)
    return r, xf * r


def _rms_bwd(dh, xh, r, g):
    dxh = dh * g
    dx = r * (dxh - xh * jnp.mean(dxh * xh, axis=-1, keepdims=True))
    dg = jnp.sum(dh * xh, axis=0, keepdims=True)
    return dx, dg


def _resident(shape):
    return pl.BlockSpec(shape, lambda *_: (0,) * len(shape), pipeline_mode=pl.Buffered(1))


def _norm_proj(x, g, w, comm=None):
    T, D = x.shape
    tm = min(256, T)
    widths = (QKV_W, CONV_W, GATE_W)

    def kern(x_ref, g_ref, w_ref, ht_ref, *o_refs):
        _, xh = _rms_stats(x_ref[...])
        h = (xh * g_ref[...]).astype(BF16)
        ht_ref[...] = h.T
        off = 0
        for o_ref, wd in zip(o_refs, widths):
            o_ref[...] = _nn(h, w_ref[:, off:off + wd]).astype(BF16)
            off += wd

    row = lambda wd: pl.BlockSpec((tm, wd), lambda i: (i, 0))
    return _hosted_call(
        comm, kern,
        out_shape=[jax.ShapeDtypeStruct((D, T), BF16)] + [jax.ShapeDtypeStruct((T, wd), BF16) for wd in widths],
        grid=(T // tm,), in_specs=[row(D), pl.BlockSpec((1, D), lambda i: (0, 0)), _resident((D, IN_COLS))],
        out_specs=[pl.BlockSpec((D, tm), lambda i: (0, i))] + [row(wd) for wd in widths],
        name="norm_proj", sem=("parallel",), args=(x, g, w))


def _band_masks(first_i):
    r = lax.broadcasted_iota(jnp.int32, (GQA_GROUP * WINDOW, 2 * WINDOW), 0)
    jj = lax.broadcasted_iota(jnp.int32, (GQA_GROUP * WINDOW, 2 * WINDOW), 1)
    dist = WINDOW + (r & (WINDOW - 1)) - jj
    valid = (dist >= 0) & (dist < WINDOW) & ((jj + (1 - first_i) * WINDOW) >= WINDOW)
    return dist.astype(F32), valid


def _per_head_col(vals):
    rb = lax.broadcasted_iota(jnp.int32, (GQA_GROUP * WINDOW, 1), 0) >> 7
    col = jnp.full((GQA_GROUP * WINDOW, 1), vals[3], F32)
    for g in (2, 1, 0):
        col = jnp.where(rb == g, vals[g], col)
    return col


def _stack_heads(ref, base, hk):
    return jnp.concatenate(
        [ref[:, base + HEAD_DIM * (GQA_GROUP * hk + g): base + HEAD_DIM * (GQA_GROUP * hk + g + 1)] for g in range(GQA_GROUP)],
        axis=0)


def _attn_fwd(pqkv, sinks, seq, comm=None):
    T = pqkv.shape[0]
    nblk = seq // WINDOW
    K0, V0 = N_Q_HEADS * HEAD_DIM, N_Q_HEADS * HEAD_DIM + N_KV_HEADS * HEAD_DIM

    def kern(sink_ref, cur_ref, prev_ref, o_ref, ot_ref, lse_ref):
        i = pl.program_id(0)
        first_i = ((i % nblk) == 0).astype(jnp.int32)
        distf, valid = _band_masks(first_i)
        outs = []
        for hk in range(N_KV_HEADS):
            k_band = jnp.concatenate([prev_ref[:, HEAD_DIM * hk:HEAD_DIM * (hk + 1)],
                                      cur_ref[:, K0 + HEAD_DIM * hk:K0 + HEAD_DIM * (hk + 1)]], axis=0)
            v_band = jnp.concatenate([prev_ref[:, 256 + HEAD_DIM * hk:256 + HEAD_DIM * (hk + 1)],
                                      cur_ref[:, V0 + HEAD_DIM * hk:V0 + HEAD_DIM * (hk + 1)]], axis=0)
            q_g = _stack_heads(cur_ref, 0, hk)
            slope = _per_head_col([_SLOPES[GQA_GROUP * hk + g] for g in range(GQA_GROUP)])
            sink = _per_head_col([sink_ref[GQA_GROUP * hk + g] for g in range(GQA_GROUP)])
            s = _nt(q_g, k_band) * ATTN_SCALE + (-slope) * distf
            s = jnp.where(valid, s, NEG_INF)
            m = jnp.maximum(jnp.max(s, axis=-1, keepdims=True), sink)
            p = jnp.exp(s - m)
            denom = jnp.sum(p, axis=-1, keepdims=True) + jnp.exp(sink - m)
            o_g = _nn((p / denom).astype(BF16), v_band)
            lse_g = m + jnp.log(denom)
            for g in range(GQA_GROUP):
                h = GQA_GROUP * hk + g
                outs.append(o_g[WINDOW * g:WINDOW * (g + 1)].astype(BF16))
                lse_ref[:, h:h + 1] = lse_g[WINDOW * g:WINDOW * (g + 1)]
        o = jnp.concatenate(outs, axis=1)
        o_ref[...] = o
        ot_ref[...] = o.T

    return _hosted_call(
        comm, kern,
        out_shape=(jax.ShapeDtypeStruct((T, D_MODEL), BF16), jax.ShapeDtypeStruct((D_MODEL, T), BF16),
                   jax.ShapeDtypeStruct((T, N_Q_HEADS), F32)),
        grid=(T // WINDOW,),
        in_specs=[pl.BlockSpec(memory_space=pltpu.SMEM),
                  pl.BlockSpec((WINDOW, QKV_W), lambda i: (i, 0)),
                  pl.BlockSpec((WINDOW, 512), lambda i: (jnp.maximum(i - 1, 0), 2))],
        out_specs=(pl.BlockSpec((WINDOW, D_MODEL), lambda i: (i, 0)), pl.BlockSpec((D_MODEL, WINDOW), lambda i: (0, i)),
                   pl.BlockSpec((WINDOW, N_Q_HEADS), lambda i: (i, 0))),
        name="attn_fwd", sem=("parallel",), args=(sinks, pqkv, pqkv))


def _pick_row(a, row):
    rid = lax.broadcasted_iota(jnp.int32, a.shape, 0)
    return jnp.sum(jnp.where(rid == row, a, 0.0), axis=0, keepdims=True)


def _conv_taps(yc, halo_yc, first_i):
    keep = (1 - first_i).astype(F32)
    p1 = _pick_row(halo_yc, 15) * keep
    p2 = _pick_row(halo_yc, 14) * keep
    rowid = lax.broadcasted_iota(jnp.int32, yc.shape, 0)
    s1 = jnp.where(rowid == 0, p1, pltpu.roll(yc, 1, 0))
    s2 = jnp.where(rowid == 0, p2, jnp.where(rowid == 1, p1, pltpu.roll(yc, 2, 0)))
    return s1, s2


def _conv_fwd(pconv, conv_w, conv_b, seq):
    T = pconv.shape[0]
    tm = min(256, seq)
    per_seq = seq // tm
    D = D_MODEL

    def kern(cur_ref, halo_ref, w_ref, b_ref, o_ref, ot_ref):
        i = pl.program_id(0)
        first_i = ((i % per_seq) == 0).astype(jnp.int32)
        cb = cur_ref[:, 0:D].astype(F32)
        yc = cur_ref[:, D:2 * D].astype(F32) * cur_ref[:, 2 * D:3 * D].astype(F32)
        halo_yc = halo_ref[:, D:2 * D].astype(F32) * halo_ref[:, 2 * D:3 * D].astype(F32)
        s1, s2 = _conv_taps(yc, halo_yc, first_i)
        z = w_ref[0:1, :] * s2 + w_ref[1:2, :] * s1 + w_ref[2:3, :] * yc
        cv = (cb * (z + b_ref[...])).astype(BF16)
        o_ref[...] = cv
        ot_ref[...] = cv.T

    return pl.pallas_call(
        kern, out_shape=(jax.ShapeDtypeStruct((T, D), BF16), jax.ShapeDtypeStruct((D, T), BF16)), grid=(T // tm,),
        in_specs=[pl.BlockSpec((tm, CONV_W), lambda i: (i, 0)),
                  pl.BlockSpec((16, CONV_W), lambda i: (jnp.maximum(i * (tm // 16) - 1, 0), 0)),
                  pl.BlockSpec((3, D), lambda i: (0, 0)), pl.BlockSpec((1, D), lambda i: (0, 0))],
        out_specs=(pl.BlockSpec((tm, D), lambda i: (i, 0)), pl.BlockSpec((D, tm), lambda i: (0, i))),
        name="conv_fwd", compiler_params=_params("parallel"))(pconv, pconv, conv_w, conv_b)


def _mix_fwd(x, att, cv, pgate, b_gates, wao, wco, wo):
    T, D = x.shape
    tm = min(512, T)

    def kern(x_ref, att_ref, cv_ref, pg_ref, bg_ref, wao_ref, wco_ref, wo_ref, x1_ref, ya_ref, yc_ref, mgt_ref):
        ya = _nn(att_ref[...], wao_ref[...])
        yc = _nn(cv_ref[...], wco_ref[...])
        sa = jax.nn.sigmoid(pg_ref[:, 0:D].astype(F32) + bg_ref[:, 0:D])
        sc = jax.nn.sigmoid(pg_ref[:, D:2 * D].astype(F32) + bg_ref[:, D:2 * D])
        mg = (sa * ya + sc * yc).astype(BF16)
        ya_ref[...] = ya.astype(BF16)
        yc_ref[...] = yc.astype(BF16)
        mgt_ref[...] = mg.T
        x1_ref[...] = x_ref[...] + _nn(mg, wo_ref[...])

    row = lambda w: pl.BlockSpec((tm, w), lambda i: (i, 0))
    full = lambda a, b: pl.BlockSpec((a, b), lambda i: (0, 0))
    bf = jax.ShapeDtypeStruct((T, D), BF16)
    return pl.pallas_call(
        kern, out_shape=(jax.ShapeDtypeStruct((T, D), F32), bf, bf, jax.ShapeDtypeStruct((D, T), BF16)), grid=(T // tm,),
        in_specs=[row(D), row(D), row(D), row(GATE_W), full(1, GATE_W), full(D, D), full(D, D), full(D, D)],
        out_specs=(row(D), row(D), row(D), pl.BlockSpec((D, tm), lambda i: (0, i))),
        name="mix_fwd", compiler_params=_params("parallel"))(x, att, cv, pgate, b_gates, wao, wco, wo)


def _mlp_fwd(x1, g, wup, wdn, comm=None):
    T, D = x1.shape
    tm = min(512, T)
    nj = D_FF // D

    def kern(x_ref, g_ref, wup_ref, wdn_ref, x2_ref, a_ref, h_scr, acc_scr):
        j = pl.program_id(1)

        @pl.when(j == 0)
        def _():
            xf = x_ref[...]
            _, xh = _rms_stats(xf)
            h_scr[...] = (xh * g_ref[...]).astype(BF16)
            acc_scr[...] = xf

        a = _nn(h_scr[...], wup_ref[...])
        a_ref[...] = a.astype(BF16)
        u = jnp.square(jnp.maximum(a, 0.0)).astype(BF16)
        acc_scr[...] += _nn(u, wdn_ref[...])

        @pl.when(j == nj - 1)
        def _():
            x2_ref[...] = acc_scr[...]

    return _hosted_call(
        comm, kern, out_shape=(jax.ShapeDtypeStruct((T, D), F32), jax.ShapeDtypeStruct((T, D_FF), BF16)), grid=(T // tm, nj),
        in_specs=[pl.BlockSpec((tm, D), lambda i, j: (i, 0)), pl.BlockSpec((1, D), lambda i, j: (0, 0)),
                  pl.BlockSpec((D, D), lambda i, j: (0, j)), pl.BlockSpec((D, D), lambda i, j: (j, 0))],
        out_specs=(pl.BlockSpec((tm, D), lambda i, j: (i, 0)), pl.BlockSpec((tm, D), lambda i, j: (i, j))),
        scratch_shapes=[pltpu.VMEM((tm, D), BF16), pltpu.VMEM((tm, D), F32)],
        name="mlp_fwd", sem=("parallel", "arbitrary"), args=(x1, g, wup, wdn))


def _loss_bwd(x, g, tgt):
    T, D = x.shape
    tm = min(512, T)

    def kern(x_ref, g_ref, t_ref, st_ref, dx_ref):
        i = pl.program_id(0)

        @pl.when(i == 0)
        def _():
            st_ref[...] = jnp.zeros_like(st_ref)

        gg = g_ref[...]
        r, xh = _rms_stats(x_ref[...])
        e = xh * gg - t_ref[...]
        part = 0.5 * jnp.sum(jnp.mean(e * e, axis=-1, keepdims=True), axis=0, keepdims=True)
        dx, dg = _rms_bwd(e * (1.0 / D), xh, r, gg)
        dx_ref[...] = dx
        st_ref[0:1, :] += dg
        st_ref[1:2, 0:1] += part

    return pl.pallas_call(
        kern, out_shape=(jax.ShapeDtypeStruct((8, D), F32), jax.ShapeDtypeStruct((T, D), F32)), grid=(T // tm,),
        in_specs=[pl.BlockSpec((tm, D), lambda i: (i, 0)), pl.BlockSpec((1, D), lambda i: (0, 0)),
                  pl.BlockSpec((tm, D), lambda i: (i, 0))],
        out_specs=(pl.BlockSpec((8, D), lambda i: (0, 0)), pl.BlockSpec((tm, D), lambda i: (i, 0))),
        name="loss_bwd", compiler_params=_params("arbitrary"))(x, g, tgt)


def _mlp_bwd(dx2, x1, a, g, wup, wdn, comm=None):
    T, D = x1.shape
    tm = min(512, T)
    nj = D_FF // D

    def kern(dx2_ref, x1_ref, a_ref, g_ref, wup_ref, wdn_ref, dx1_ref, da_ref, ut_ref, h2t_ref, dg_ref, dyb_scr, acc_scr):
        i, j = pl.program_id(0), pl.program_id(1)

        @pl.when((i == 0) & (j == 0))
        def _():
            dg_ref[...] = jnp.zeros_like(dg_ref)

        @pl.when(j == 0)
        def _():
            dyb_scr[...] = dx2_ref[...].astype(BF16)
            acc_scr[...] = jnp.zeros_like(acc_scr)

        du = _nt(dyb_scr[...], wdn_ref[...])
        relu = jnp.maximum(a_ref[...].astype(F32), 0.0)
        da = (du * (2.0 * relu)).astype(BF16)
        da_ref[...] = da
        ut_ref[...] = jnp.square(relu).astype(BF16).T
        acc_scr[...] += _nt(da, wup_ref[...])

        @pl.when(j == nj - 1)
        def _():
            gg = g_ref[...]
            r, xh = _rms_stats(x1_ref[...])
            h2t_ref[...] = (xh * gg).astype(BF16).T
            dx, dg = _rms_bwd(acc_scr[...], xh, r, gg)
            dx1_ref[...] = dx2_ref[...] + dx
            dg_ref[...] += dg

    return _hosted_call(
        comm, kern,
        out_shape=(jax.ShapeDtypeStruct((T, D), F32), jax.ShapeDtypeStruct((T, D_FF), BF16),
                   jax.ShapeDtypeStruct((D_FF, T), BF16), jax.ShapeDtypeStruct((D, T), BF16), jax.ShapeDtypeStruct((1, D), F32)),
        grid=(T // tm, nj),
        in_specs=[pl.BlockSpec((tm, D), lambda i, j: (i, 0)), pl.BlockSpec((tm, D), lambda i, j: (i, 0)),
                  pl.BlockSpec((tm, D), lambda i, j: (i, j)), pl.BlockSpec((1, D), lambda i, j: (0, 0)),
                  pl.BlockSpec((D, D), lambda i, j: (0, j)), pl.BlockSpec((D, D), lambda i, j: (j, 0))],
        out_specs=(pl.BlockSpec((tm, D), lambda i, j: (i, 0)), pl.BlockSpec((tm, D), lambda i, j: (i, j)),
                   pl.BlockSpec((D, tm), lambda i, j: (j, i)), pl.BlockSpec((D, tm), lambda i, j: (0, i)),
                   pl.BlockSpec((1, D), lambda i, j: (0, 0))),
        scratch_shapes=[pltpu.VMEM((tm, D), BF16), pltpu.VMEM((tm, D), F32)],
        name="mlp_bwd", sem=("arbitrary", "arbitrary"), args=(dx2, x1, a, g, wup, wdn))


def _mix_bwd(dx1, ya, yc, pgate, b_gates, wao, wco, wo):
    T, D = dx1.shape
    tm = min(256, T)

    def kern(dx_ref, ya_ref, yc_ref, pg_ref, bg_ref, wao_ref, wco_ref, wo_ref,
             datt_ref, dcv_ref, dya_ref, dyc_ref, dgt_ref, dbg_ref):
        @pl.when(pl.program_id(0) == 0)
        def _():
            dbg_ref[...] = jnp.zeros_like(dbg_ref)

        dm = _nt(dx_ref[...].astype(BF16), wo_ref[...])
        sa = jax.nn.sigmoid(pg_ref[:, 0:D].astype(F32) + bg_ref[:, 0:D])
        sc = jax.nn.sigmoid(pg_ref[:, D:2 * D].astype(F32) + bg_ref[:, D:2 * D])
        dya = (dm * sa).astype(BF16)
        dyc = (dm * sc).astype(BF16)
        dga = dm * ya_ref[...].astype(F32) * (sa * (1.0 - sa))
        dgc = dm * yc_ref[...].astype(F32) * (sc * (1.0 - sc))
        dya_ref[...] = dya
        dyc_ref[...] = dyc
        dgt_ref[:, 0:D] = dga.astype(BF16)
        dgt_ref[:, D:2 * D] = dgc.astype(BF16)
        dbg_ref[:, 0:D] += jnp.sum(dga, axis=0, keepdims=True)
        dbg_ref[:, D:2 * D] += jnp.sum(dgc, axis=0, keepdims=True)
        datt_ref[...] = _nt(dya, wao_ref[...]).astype(BF16)
        dcv_ref[...] = _nt(dyc, wco_ref[...]).astype(BF16)

    row = lambda w: pl.BlockSpec((tm, w), lambda i: (i, 0))
    full = lambda a, b: pl.BlockSpec((a, b), lambda i: (0, 0))
    bf = jax.ShapeDtypeStruct((T, D), BF16)
    return pl.pallas_call(
        kern,
        out_shape=(bf, bf, bf, bf, jax.ShapeDtypeStruct((T, GATE_W), BF16), jax.ShapeDtypeStruct((1, GATE_W), F32)),
        grid=(T // tm,),
        in_specs=[row(D), row(D), row(D), row(GATE_W), full(1, GATE_W), full(D, D), full(D, D), full(D, D)],
        out_specs=(row(D), row(D), row(D), row(D), row(GATE_W), full(1, GATE_W)),
        name="mix_bwd", compiler_params=_params("arbitrary"))(dx1, ya, yc, pgate, b_gates, wao, wco, wo)


def _conv_bwd(dcv, pconv, conv_w, conv_b, seq):
    T = pconv.shape[0]
    tm = min(256, seq)
    per_seq = seq // tm
    D = D_MODEL
    nb16 = T // 16

    def kern(dcv_ref, dcvn_ref, cur_ref, prev_ref, next_ref, w_ref, b_ref, o_ref, dwb_ref):
        i = pl.program_id(0)

        @pl.when(i == 0)
        def _():
            dwb_ref[...] = jnp.zeros_like(dwb_ref)

        first_i = ((i % per_seq) == 0).astype(jnp.int32)
        keep_next = 1.0 - (((i + 1) % per_seq) == 0).astype(F32)
        cb = cur_ref[:, 0:D].astype(F32)
        cc = cur_ref[:, D:2 * D].astype(F32)
        cu = cur_ref[:, 2 * D:3 * D].astype(F32)
        yc = cc * cu
        halo_yc = prev_ref[:, D:2 * D].astype(F32) * prev_ref[:, 2 * D:3 * D].astype(F32)
        s1, s2 = _conv_taps(yc, halo_yc, first_i)
        w0, w1, w2 = w_ref[0:1, :], w_ref[1:2, :], w_ref[2:3, :]
        z = w0 * s2 + w1 * s1 + w2 * yc
        dcv = dcv_ref[...].astype(F32)
        dz = dcv * cb
        dzn = dcvn_ref[...].astype(F32) * next_ref[:, 0:D].astype(F32) * keep_next
        n1, n2 = _pick_row(dzn, 0), _pick_row(dzn, 1)
        rowid = lax.broadcasted_iota(jnp.int32, dz.shape, 0)
        u1 = jnp.where(rowid == tm - 1, n1, pltpu.roll(dz, tm - 1, 0))
        u2 = jnp.where(rowid == tm - 1, n2, jnp.where(rowid == tm - 2, n1, pltpu.roll(dz, tm - 2, 0)))
        dyc = w2 * dz + w1 * u1 + w0 * u2
        o_ref[:, 0:D] = (dcv * (z + b_ref[...])).astype(BF16)
        o_ref[:, D:2 * D] = (dyc * cu).astype(BF16)
        o_ref[:, 2 * D:3 * D] = (dyc * cc).astype(BF16)
        dwb_ref[0:1, :] += jnp.sum(dz * s2, axis=0, keepdims=True)
        dwb_ref[1:2, :] += jnp.sum(dz * s1, axis=0, keepdims=True)
        dwb_ref[2:3, :] += jnp.sum(dz * yc, axis=0, keepdims=True)
        dwb_ref[3:4, :] += jnp.sum(dz, axis=0, keepdims=True)

    prev_map = lambda i: (jnp.maximum(i * (tm // 16) - 1, 0), 0)
    next_map = lambda i: (jnp.minimum((i + 1) * (tm // 16), nb16 - 1), 0)
    return pl.pallas_call(
        kern, out_shape=(jax.ShapeDtypeStruct((T, CONV_W), BF16), jax.ShapeDtypeStruct((8, D), F32)), grid=(T // tm,),
        in_specs=[pl.BlockSpec((tm, D), lambda i: (i, 0)), pl.BlockSpec((16, D), next_map),
                  pl.BlockSpec((tm, CONV_W), lambda i: (i, 0)), pl.BlockSpec((16, CONV_W), prev_map),
                  pl.BlockSpec((16, CONV_W), next_map),
                  pl.BlockSpec((3, D), lambda i: (0, 0)), pl.BlockSpec((1, D), lambda i: (0, 0))],
        out_specs=(pl.BlockSpec((tm, CONV_W), lambda i: (i, 0)), pl.BlockSpec((8, D), lambda i: (0, 0))),
        name="conv_bwd", compiler_params=_params("arbitrary"))(dcv, dcv, pconv, pconv, pconv, conv_w, conv_b)


def _attn_bwd(pqkv, att, datt, lse, sinks, seq, comm=None):
    T = pqkv.shape[0]
    nblk = seq // WINDOW
    nseq = T // seq
    K0, V0 = N_Q_HEADS * HEAD_DIM, N_Q_HEADS * HEAD_DIM + N_KV_HEADS * HEAD_DIM
    KVW = N_KV_HEADS * HEAD_DIM

    def kern(sink_ref, cur_ref, prev_ref, o_ref, do_ref, lse_ref, dq_ref, dkv_ref, ds_ref, kc_scr, vc_scr):
        b, st = pl.program_id(0), pl.program_id(1)

        @pl.when((b == 0) & (st == 0))
        def _():
            ds_ref[...] = jnp.zeros_like(ds_ref)

        @pl.when(st == 0)
        def _():
            kc_scr[...] = jnp.zeros_like(kc_scr)
            vc_scr[...] = jnp.zeros_like(vc_scr)

        @pl.when(st < nblk)
        def _():
            first_i = (st == 0).astype(jnp.int32)
            distf, valid = _band_masks(first_i)
            od = o_ref[...].astype(F32) * do_ref[...].astype(F32)
            dqs = []
            for hk in range(N_KV_HEADS):
                k_band = jnp.concatenate([prev_ref[:, HEAD_DIM * hk:HEAD_DIM * (hk + 1)],
                                          cur_ref[:, K0 + HEAD_DIM * hk:K0 + HEAD_DIM * (hk + 1)]], axis=0)
                v_band = jnp.concatenate([prev_ref[:, KVW + HEAD_DIM * hk:KVW + HEAD_DIM * (hk + 1)],
                                          cur_ref[:, V0 + HEAD_DIM * hk:V0 + HEAD_DIM * (hk + 1)]], axis=0)
                q_g = _stack_heads(cur_ref, 0, hk)
                do_g = _stack_heads(do_ref, 0, hk)
                heads = [GQA_GROUP * hk + g for g in range(GQA_GROUP)]
                slope = _per_head_col([_SLOPES[h] for h in heads])
                sink = _per_head_col([sink_ref[h] for h in heads])
                lse_g = jnp.concatenate([lse_ref[:, h:h + 1] for h in heads], axis=0)
                d_g = jnp.concatenate(
                    [jnp.sum(od[:, HEAD_DIM * h:HEAD_DIM * (h + 1)], axis=-1, keepdims=True) for h in heads], axis=0)
                s = _nt(q_g, k_band) * ATTN_SCALE + (-slope) * distf
                p = jnp.where(valid, jnp.exp(s - lse_g), 0.0)
                dp = _nt(do_g, v_band)
                dsb = (p * (dp - d_g)).astype(BF16)
                dq_g = _nn(dsb, k_band) * ATTN_SCALE
                dk_b = _tn(dsb, q_g) * ATTN_SCALE
                dv_b = _tn(p.astype(BF16), do_g)
                psd = jnp.exp(sink - lse_g) * d_g
                for g in range(GQA_GROUP):
                    h = heads[g]
                    dqs.append(dq_g[WINDOW * g:WINDOW * (g + 1)].astype(BF16))
                    ds_ref[0:1, h:h + 1] -= jnp.sum(psd[WINDOW * g:WINDOW * (g + 1)], axis=0, keepdims=True)
                ksl = slice(HEAD_DIM * hk, HEAD_DIM * (hk + 1))
                vsl = slice(KVW + HEAD_DIM * hk, KVW + HEAD_DIM * (hk + 1))
                dkv_ref[:, ksl] = (kc_scr[:, ksl] + dk_b[0:WINDOW]).astype(BF16)
                dkv_ref[:, vsl] = (vc_scr[:, ksl] + dv_b[0:WINDOW]).astype(BF16)
                kc_scr[:, ksl] = dk_b[WINDOW:2 * WINDOW]
                vc_scr[:, ksl] = dv_b[WINDOW:2 * WINDOW]
            dq_ref[...] = jnp.concatenate(dqs, axis=1)

        @pl.when(st == nblk)
        def _():
            dkv_ref[:, 0:KVW] = kc_scr[...].astype(BF16)
            dkv_ref[:, KVW:2 * KVW] = vc_scr[...].astype(BF16)

    cur_map = lambda b, s: (b * nblk + jnp.minimum(s, nblk - 1), 0)
    prev_row = lambda b, s: b * nblk + jnp.clip(s - 1, 0, nblk - 1)
    return _hosted_call(
        comm, kern,
        out_shape=(jax.ShapeDtypeStruct((T, D_MODEL), BF16), jax.ShapeDtypeStruct((T, 2 * KVW), BF16),
                   jax.ShapeDtypeStruct((8, 128), F32)),
        grid=(nseq, nblk + 1),
        in_specs=[pl.BlockSpec(memory_space=pltpu.SMEM),
                  pl.BlockSpec((WINDOW, QKV_W), cur_map),
                  pl.BlockSpec((WINDOW, 2 * KVW), lambda b, s: (prev_row(b, s), 2)),
                  pl.BlockSpec((WINDOW, D_MODEL), cur_map), pl.BlockSpec((WINDOW, D_MODEL), cur_map),
                  pl.BlockSpec((WINDOW, N_Q_HEADS), cur_map)],
        out_specs=(pl.BlockSpec((WINDOW, D_MODEL), cur_map),
                   pl.BlockSpec((WINDOW, 2 * KVW), lambda b, s: (prev_row(b, s), 0)),
                   pl.BlockSpec((8, 128), lambda b, s: (0, 0))),
        scratch_shapes=[pltpu.VMEM((WINDOW, KVW), F32), pltpu.VMEM((WINDOW, KVW), F32)],
        name="attn_bwd", sem=("arbitrary", "arbitrary"), args=(sinks, pqkv, pqkv, att, datt, lse))


def _piece_tiles(pieces):
    out, start = [], 0
    for arr, width in pieces:
        out.append((arr, start, width // COL_TILE))
        start += width // COL_TILE
    return out, start


def _inproj_bwd(pieces, w_in, x, dx_in, g, comm=None):
    T, D = x.shape
    tm = min(256, T)

    def kern(*refs):
        p_refs = refs[:len(pieces)]
        w_ref, x_ref, dxin_ref, g_ref, dx_ref, dg_ref = refs[len(pieces):]

        @pl.when(pl.program_id(0) == 0)
        def _():
            dg_ref[...] = jnp.zeros_like(dg_ref)

        dh, off = None, 0
        for p_ref, (_, width) in zip(p_refs, pieces):
            part = _nt(p_ref[...], w_ref[:, off:off + width])
            dh = part if dh is None else dh + part
            off += width
        gg = g_ref[...]
        r, xh = _rms_stats(x_ref[...])
        dx, dg = _rms_bwd(dh, xh, r, gg)
        dx_ref[...] = dxin_ref[...] + dx
        dg_ref[...] += dg

    row = lambda wd: pl.BlockSpec((tm, wd), lambda i: (i, 0))
    return _hosted_call(
        comm, kern, out_shape=(jax.ShapeDtypeStruct((T, D), F32), jax.ShapeDtypeStruct((1, D), F32)), grid=(T // tm,),
        in_specs=[row(wd) for _, wd in pieces] + [_resident((D, IN_COLS)), row(D), row(D), pl.BlockSpec((1, D), lambda i: (0, 0))],
        out_specs=(row(D), pl.BlockSpec((1, D), lambda i: (0, 0))),
        name="inproj_bwd", sem=("arbitrary",), args=(*[a for a, _ in pieces], w_in, x, dx_in, g))


def _dw_pieces(lhs_t, pieces):
    K, T = lhs_t.shape
    tk = min(2048, T)
    nt = T // tk
    tiles, nj = _piece_tiles(pieces)

    def kern(*refs):
        lhs_ref = refs[0]
        p_refs = refs[1:1 + len(tiles)]
        o_ref, ob_ref = refs[1 + len(tiles):]
        j, t = pl.program_id(0), pl.program_id(1)

        @pl.when(t == 0)
        def _():
            o_ref[...] = jnp.zeros_like(o_ref)

        for p_ref, (_, start, n) in zip(p_refs, tiles):
            @pl.when((j >= start) & (j < start + n))
            def _(p_ref=p_ref):
                o_ref[...] += _nn(lhs_ref[...], p_ref[...])

        @pl.when(t == nt - 1)
        def _():
            ob_ref[...] = o_ref[...].astype(BF16)

    def p_map(start, n):
        return lambda j, t: (jnp.where((j >= start) & (j < start + n), t, 0), jnp.clip(j - start, 0, n - 1))

    N = nj * COL_TILE
    return pl.pallas_call(
        kern, out_shape=(jax.ShapeDtypeStruct((K, N), F32), jax.ShapeDtypeStruct((K, N), BF16)), grid=(nj, nt),
        in_specs=[pl.BlockSpec((K, tk), lambda j, t: (0, t))] + [pl.BlockSpec((tk, COL_TILE), p_map(s, n)) for _, s, n in tiles],
        out_specs=(pl.BlockSpec((K, COL_TILE), lambda j, t: (0, j)), pl.BlockSpec((K, COL_TILE), lambda j, t: (0, j))),
        name="dw_pieces", compiler_params=_params("arbitrary", "arbitrary"))(lhs_t, *[a for a, _, _ in tiles])


def _dw(lhs_t, rhs):
    K, T = lhs_t.shape
    N = rhs.shape[1]
    tk = min(1024, T)
    nt = T // tk
    W = D_MODEL

    def kern(lhs_ref, rhs_ref, o_ref, ob_ref):
        t = pl.program_id(2)

        @pl.when(t == 0)
        def _():
            o_ref[...] = jnp.zeros_like(o_ref)

        o_ref[...] += _nn(lhs_ref[...], rhs_ref[...].astype(BF16))

        @pl.when(t == nt - 1)
        def _():
            ob_ref[...] = o_ref[...].astype(BF16)

    omap = lambda i, j, t: (i, j)
    return pl.pallas_call(
        kern, out_shape=(jax.ShapeDtypeStruct((K, N), F32), jax.ShapeDtypeStruct((K, N), BF16)), grid=(K // W, N // W, nt),
        in_specs=[pl.BlockSpec((W, tk), lambda i, j, t: (i, t)), pl.BlockSpec((tk, W), lambda i, j, t: (t, j))],
        out_specs=(pl.BlockSpec((W, W), omap), pl.BlockSpec((W, W), omap)),
        name="dw", compiler_params=_params("arbitrary", "arbitrary", "arbitrary"))(lhs_t, rhs)


BIG = (("w_in", D_MODEL, IN_COLS, "col"), ("w_attn_out", D_MODEL, D_MODEL, "row"), ("w_conv_out", D_MODEL, D_MODEL, "row"),
       ("w_o", D_MODEL, D_MODEL, "row"), ("w_up", D_MODEL, D_FF, "col"), ("w_down", D_FF, D_MODEL, "row"))


def _shard_dims(rows, cols, kind):
    return (rows, cols // N_CHIP) if kind == "col" else (rows // N_CHIP, cols)


def _window(ref, rows, cols, kind, chip, half):
    sr, sc = _shard_dims(rows, cols, kind)
    hr = sr // 2
    if kind == "col":
        return ref.at[pl.ds(half * hr, hr), pl.ds(chip * sc, sc)]
    return ref.at[pl.ds(chip * sr + half * hr, hr), :]


def _mesh_pos():
    x, y, c = lax.axis_index("x"), lax.axis_index("y"), lax.axis_index("c")
    return x, y, c, 2 * x + y


_REL_BITS = (2, 1, 3)


def _rel_dev(x, y, c, r):
    return ((1 - x, y, c), (x, 1 - y, c), (1 - x, 1 - y, c))[r]


def _for_my_chip(j, fn):
    for js in range(N_CHIP):
        pl.when(j == js)(functools.partial(fn, js))


def _cast_into_full(j_arr, shard, l, rows, cols, kind):
    sr, sc = _shard_dims(rows, cols, kind)
    tr = min(256, sr)

    def kern(j_ref, s_ref, o_ref):
        o_ref[...] = s_ref[...].astype(BF16)

    if kind == "col":
        omap = lambda i, j_ref: (i, j_ref[0])
    else:
        omap = lambda i, j_ref: (j_ref[0] * (sr // tr) + i, 0)
    gs = pltpu.PrefetchScalarGridSpec(
        num_scalar_prefetch=1, grid=(sr // tr,),
        in_specs=[pl.BlockSpec((None, tr, sc), lambda i, j_ref: (l, i, 0))], out_specs=pl.BlockSpec((tr, sc), omap))
    return pl.pallas_call(kern, out_shape=jax.ShapeDtypeStruct((rows, cols), BF16), grid_spec=gs, name="cast_into_full",
                          compiler_params=_params("arbitrary"))(j_arr, shard)


def _gather(fulls, cw=None):
    comm = _gather_comm(fulls, cw)
    n = len(comm.inputs)

    def body(*refs):
        comm.start(refs[:n], refs[n:2 * n], refs[2 * n:])
        comm.finish(refs[:n], refs[n:2 * n], refs[2 * n:])

    return pl.pallas_call(
        body, out_shape=comm.out_shape, in_specs=[ANY] * n, out_specs=[ANY] * n, input_output_aliases=comm.aliases,
        scratch_shapes=comm.scratch, name="gather_weights")(*comm.inputs)


def _gather_comm(fulls, cw=None):
    n_big = len(fulls)
    n_piece = n_big + (0 if cw is None else 1)

    def pieces(in_refs, o_refs, js, c):
        def piece(p, chip, half):
            if p == n_big:
                return o_refs[p].at[half, chip]
            _, rows, cols, kind = fulls[p]
            return _window(o_refs[p], rows, cols, kind, chip, half)

        def mine(p):
            return in_refs[p].at[c] if p == n_big else piece(p, js, c)

        return piece, mine

    def local_copies(in_refs, piece, js, loc_sem):
        if cw is None:
            return []
        return [pltpu.make_async_copy(in_refs[n_big].at[half], piece(n_big, js, half), loc_sem.at[half]) for half in range(2)]

    def ici_copy(piece, mine, js, x, y, c, r, p, send_sem, recv_sem):
        return pltpu.make_async_remote_copy(mine(p), piece(p, js, c), send_sem.at[r * n_piece + p], recv_sem.at[r * n_piece + p],
                                            _rel_dev(x, y, c, r), MESH)

    def start(in_refs, o_refs, sems):
        send_sem, recv_sem, _, _, loc_sem = sems
        x, y, c, j = _mesh_pos()

        def run(js):
            piece, mine = pieces(in_refs, o_refs, js, c)
            for cp in local_copies(in_refs, piece, js, loc_sem):
                cp.start()
            for r in range(3):
                for p in range(n_piece):
                    ici_copy(piece, mine, js, x, y, c, r, p, send_sem, recv_sem).start()

        _for_my_chip(j, run)

    def finish(in_refs, o_refs, sems):
        send_sem, recv_sem, fsend_sem, frecv_sem, loc_sem = sems
        x, y, c, j = _mesh_pos()

        def run(js):
            piece, mine = pieces(in_refs, o_refs, js, c)
            fwds = []
            for r in range(3):
                ks = js ^ _REL_BITS[r]
                for p in range(n_piece):
                    got = piece(p, ks, c)
                    pltpu.make_async_remote_copy(got, got, send_sem.at[r * n_piece + p], recv_sem.at[r * n_piece + p],
                                                 _rel_dev(x, y, c, r), MESH).wait_recv()
                    cp = pltpu.make_async_remote_copy(got, got, fsend_sem.at[r * n_piece + p], frecv_sem.at[r * n_piece + p],
                                                      (x, y, 1 - c), MESH)
                    cp.start()
                    fwds.append(cp)
            for r in range(3):
                ks = js ^ _REL_BITS[r]
                for p in range(n_piece):
                    got = piece(p, ks, 1 - c)
                    pltpu.make_async_remote_copy(got, got, fsend_sem.at[r * n_piece + p], frecv_sem.at[r * n_piece + p],
                                                 (x, y, 1 - c), MESH).wait_recv()
            for r in range(3):
                for p in range(n_piece):
                    ici_copy(piece, mine, js, x, y, c, r, p, send_sem, recv_sem).wait_send()
            for cp in fwds:
                cp.wait_send()
            for cp in local_copies(in_refs, piece, js, loc_sem):
                cp.wait()

        _for_my_chip(j, run)

    out_shape = [jax.ShapeDtypeStruct(a.shape, BF16) for a, _, _, _ in fulls]
    ins = [a for a, _, _, _ in fulls]
    if cw is not None:
        out_shape.append(jax.ShapeDtypeStruct((DEPTH, N_CHIP, 3, D_MODEL // N_CHIP), F32))
        ins.append(cw)
    scratch = [pltpu.SemaphoreType.DMA((3 * n_piece,))] * 4 + [pltpu.SemaphoreType.DMA((2,))]
    return _Hosted(ins, out_shape, {p: p for p in range(n_big)}, scratch, start, finish)


def _sibling_exchange(gb):
    n = len(gb)

    def body(*refs):
        g_refs, o_refs = refs[:n], refs[n:2 * n]
        send_sem, recv_sem = refs[2 * n:]
        x, y, c, _ = _mesh_pos()
        cps = []
        for t in range(n):
            _, rows, cols, kind = gb[t]
            for chip in range(N_CHIP):
                out_w = _window(g_refs[t], rows, cols, kind, chip, 1 - c)
                dst_w = _window(o_refs[t], rows, cols, kind, chip, 1 - c)
                cp = pltpu.make_async_remote_copy(out_w, dst_w, send_sem.at[N_CHIP * t + chip], recv_sem.at[N_CHIP * t + chip],
                                                  (x, y, 1 - c), MESH)
                cp.start()
                cps.append(cp)
        for t in range(n):
            _, rows, cols, kind = gb[t]
            for chip in range(N_CHIP):
                w = _window(o_refs[t], rows, cols, kind, chip, c)
                pltpu.make_async_remote_copy(w, w, send_sem.at[N_CHIP * t + chip], recv_sem.at[N_CHIP * t + chip],
                                             (x, y, 1 - c), MESH).wait_recv()
        for cp in cps:
            cp.wait_send()

    return pl.pallas_call(
        body, out_shape=[jax.ShapeDtypeStruct(a.shape, BF16) for a, _, _, _ in gb], in_specs=[ANY] * n, out_specs=[ANY] * n,
        scratch_shapes=[pltpu.SemaphoreType.DMA((N_CHIP * n,))] * 2, name="grad_sibling_exchange")(*[a for a, _, _, _ in gb])


def _half_add(c_arr, g, sib, rows, cols, kind):
    sr, sc = _shard_dims(rows, cols, kind)
    hr = sr // 2

    def kern(c_ref, g_ref, s_ref, ob_ref, of_ref):
        v = g_ref[...] + s_ref[...].astype(F32)
        of_ref[...] = v
        ob_ref[...] = v.astype(BF16)

    if kind == "col":
        imap = lambda j, c_ref: (c_ref[0], j)
    else:
        imap = lambda j, c_ref: (2 * j + c_ref[0], 0)
    gs = pltpu.PrefetchScalarGridSpec(
        num_scalar_prefetch=1, grid=(N_CHIP,),
        in_specs=[pl.BlockSpec((hr, sc), imap), pl.BlockSpec((hr, sc), imap)],
        out_specs=[pl.BlockSpec((None, hr, sc), lambda j, c_ref: (j, 0, 0))] * 2)
    return pl.pallas_call(
        kern, out_shape=(jax.ShapeDtypeStruct((N_CHIP, hr, sc), BF16), jax.ShapeDtypeStruct((N_CHIP, hr, sc), F32)),
        grid_spec=gs, name="grad_half_add", compiler_params=_params("arbitrary"))(c_arr, g, sib)


def _chip_exchange(sbs):
    comm = _chip_exchange_comm(sbs)
    n = len(sbs)

    def body(*refs):
        comm.start(refs[:n], refs[n:2 * n], refs[2 * n:])
        comm.finish(refs[:n], refs[n:2 * n], refs[2 * n:])

    return pl.pallas_call(
        body, out_shape=comm.out_shape, in_specs=[ANY] * n, out_specs=[ANY] * n, scratch_shapes=comm.scratch,
        name="grad_chip_exchange")(*sbs)


def _chip_exchange_comm(sbs):
    n = len(sbs)

    def copies(s_refs, o_refs, sems):
        send_sem, recv_sem = sems
        x, y, c, j = _mesh_pos()
        return [pltpu.make_async_remote_copy(s_refs[t].at[j ^ _REL_BITS[r]], o_refs[t].at[r], send_sem.at[n * r + t],
                                             recv_sem.at[n * r + t], _rel_dev(x, y, c, r), MESH)
                for r in range(3) for t in range(n)]

    def start(s_refs, o_refs, sems):
        for cp in copies(s_refs, o_refs, sems):
            cp.start()

    def finish(s_refs, o_refs, sems):
        for cp in copies(s_refs, o_refs, sems):
            cp.wait()

    return _Hosted(sbs, [jax.ShapeDtypeStruct((3,) + a.shape[1:], BF16) for a in sbs], {},
                   [pltpu.SemaphoreType.DMA((3 * n,))] * 2, start, finish)


def _owner_sum(jc_arr, sf, rb, l, into=None):
    _, hr, sc = sf.shape

    def kern(jc_ref, s_ref, r0_ref, r1_ref, r2_ref, *rest):
        o_ref = rest[-1]
        o_ref[...] = ((s_ref[...] + r0_ref[...].astype(F32)) + r1_ref[...].astype(F32)) + r2_ref[...].astype(F32)

    in_specs = [pl.BlockSpec((None, hr, sc), lambda i, jc_ref: (jc_ref[0], 0, 0))]
    in_specs += [pl.BlockSpec((None, hr, sc), lambda i, jc_ref, r=r: (r, 0, 0)) for r in range(3)]
    args = [jc_arr, sf, rb, rb, rb]
    aliases = {}
    if into is not None:
        in_specs.append(ANY)
        args.append(into)
        aliases = {len(args) - 1: 0}
    gs = pltpu.PrefetchScalarGridSpec(
        num_scalar_prefetch=1, grid=(1,), in_specs=in_specs,
        out_specs=pl.BlockSpec((None, hr, sc), lambda i, jc_ref: (l, jc_ref[1], 0)))
    return pl.pallas_call(kern, out_shape=jax.ShapeDtypeStruct((DEPTH, 2 * hr, sc), F32), grid_spec=gs,
                          input_output_aliases=aliases, name="grad_owner_sum", compiler_params=_params("arbitrary"))(*args)


def _sibling_assemble(grads):
    n = len(grads)

    def body(*refs):
        o_refs = refs[n:2 * n]
        send_sem, recv_sem = refs[2 * n:]
        x, y, c, _ = _mesh_pos()
        cps = []
        for q in range(n):
            hr = grads[q].shape[1] // 2
            for l in range(DEPTH):
                mine = o_refs[q].at[l, pl.ds(c * hr, hr), :]
                cp = pltpu.make_async_remote_copy(mine, mine, send_sem.at[DEPTH * q + l], recv_sem.at[DEPTH * q + l],
                                                  (x, y, 1 - c), MESH)
                cp.start()
                cps.append(cp)
        for q in range(n):
            hr = grads[q].shape[1] // 2
            for l in range(DEPTH):
                theirs = o_refs[q].at[l, pl.ds((1 - c) * hr, hr), :]
                pltpu.make_async_remote_copy(theirs, theirs, send_sem.at[DEPTH * q + l], recv_sem.at[DEPTH * q + l],
                                             (x, y, 1 - c), MESH).wait_recv()
        for cp in cps:
            cp.wait_send()

    return pl.pallas_call(
        body, out_shape=[jax.ShapeDtypeStruct(g.shape, F32) for g in grads], in_specs=[ANY] * n, out_specs=[ANY] * n,
        input_output_aliases={q: q for q in range(n)},
        scratch_shapes=[pltpu.SemaphoreType.DMA((DEPTH * n,))] * 2, name="grad_sibling_assemble")(*grads)


def _adamw_math(w, g, m, v):
    m = ADAM_B1 * m + (1.0 - ADAM_B1) * g
    v = ADAM_B2 * v + (1.0 - ADAM_B2) * jnp.square(g)
    m_hat = m / (1.0 - ADAM_B1 ** ADAM_STEP)
    v_hat = v / (1.0 - ADAM_B2 ** ADAM_STEP)
    delta = -ADAM_LR * (m_hat / (jnp.sqrt(v_hat) + ADAM_EPS) + ADAM_WD * w)
    return delta, m, v


def _adamw(w, g, m, v):
    shape = w.shape
    C = shape[-1]
    R = int(np.prod(shape[:-1]))
    tr = min(256, R)
    args = [a.reshape(R, C) for a in (w, g, m, v)]

    def kern(w_ref, g_ref, m_ref, v_ref, d_ref, nm_ref, nv_ref):
        d, nm, nv = _adamw_math(w_ref[...], g_ref[...], m_ref[...], v_ref[...])
        d_ref[...] = d
        nm_ref[...] = nm
        nv_ref[...] = nv

    spec = pl.BlockSpec((tr, C), lambda i: (i, 0))
    outs = pl.pallas_call(
        kern, out_shape=[jax.ShapeDtypeStruct((R, C), F32)] * 3, grid=(R // tr,), in_specs=[spec] * 4, out_specs=[spec] * 3,
        name="adamw", compiler_params=_params("parallel"))(*args)
    return [o.reshape(shape) for o in outs]


_ROW_G_MIX, _ROW_B_GATES, _ROW_SINKS, _ROW_CONV, _ROW_G_MLP, _ROW_G_FINAL, _ROW_LOSS = 0, 2, 6, 8, 16, 18, 19


def _small_step(parts, params, moms, vels):
    names = ["g_mix", "b_gates", "sinks", "conv_w", "conv_b", "g_mlp", "g_final"]
    D = D_MODEL
    QW = D // N_CHIP
    n_dev = 8

    def body(*refs):
        it = iter(refs)
        dgmix = [next(it) for _ in range(DEPTH)]
        dbg = [next(it) for _ in range(DEPTH)]
        dsk = [next(it) for _ in range(DEPTH)]
        dwb = [next(it) for _ in range(DEPTH)]
        dgmlp = [next(it) for _ in range(DEPTH)]
        lst = next(it)
        p_refs = {n: next(it) for n in names}
        m_refs = {n: next(it) for n in names}
        v_refs = {n: next(it) for n in names}
        loss_ref = next(it)
        outs = {n: [next(it) for _ in range(4)] for n in names}
        pack_ref, all_ref, send_sem, recv_sem = next(it), next(it), next(it), next(it)

        x, y, c, j = _mesh_pos()
        me = 4 * x + 2 * y + c
        pack_ref[...] = jnp.zeros_like(pack_ref)
        for l in range(DEPTH):
            pack_ref[_ROW_G_MIX + l:_ROW_G_MIX + l + 1, :] = dgmix[l][...]
            pack_ref[_ROW_B_GATES + 2 * l:_ROW_B_GATES + 2 * l + 1, :] = dbg[l][:, 0:D]
            pack_ref[_ROW_B_GATES + 2 * l + 1:_ROW_B_GATES + 2 * l + 2, :] = dbg[l][:, D:2 * D]
            pack_ref[_ROW_SINKS + l:_ROW_SINKS + l + 1, 0:128] = dsk[l][0:1, :]
            pack_ref[_ROW_CONV + 4 * l:_ROW_CONV + 4 * l + 4, :] = dwb[l][0:4, :]
            pack_ref[_ROW_G_MLP + l:_ROW_G_MLP + l + 1, :] = dgmlp[l][...]
        pack_ref[_ROW_G_FINAL:_ROW_G_FINAL + 1, :] = lst[0:1, :]
        pack_ref[_ROW_LOSS:_ROW_LOSS + 1, :] = lst[1:2, :]

        all_ref[me] = pack_ref[...]
        cps = []
        for k in range(1, n_dev):
            dx_, dy_, dc_ = (k >> 2) & 1, (k >> 1) & 1, k & 1
            peer = (x ^ dx_, y ^ dy_, c ^ dc_)
            cp = pltpu.make_async_remote_copy(pack_ref, all_ref.at[me], send_sem.at[k - 1], recv_sem.at[k - 1], peer, MESH)
            cp.start()
            cps.append(cp)
        for cp in cps:
            cp.wait()

        tot = all_ref[0]
        for d in range(1, n_dev):
            tot = tot + all_ref[d]
        pack_ref[...] = tot

        loss_ref[...] = pack_ref[_ROW_LOSS:_ROW_LOSS + 1, 0:1]

        def finish(name, idx, g):
            w, m, v = p_refs[name][idx], m_refs[name][idx], v_refs[name][idx]
            d, nm, nv = _adamw_math(w, g, m, v)
            for ref, val in zip(outs[name], (g, d, nm, nv)):
                ref[idx] = val

        for l in range(DEPTH):
            finish("g_mix", (slice(l, l + 1), slice(None)), pack_ref[_ROW_G_MIX + l:_ROW_G_MIX + l + 1, :])
            finish("g_mlp", (slice(l, l + 1), slice(None)), pack_ref[_ROW_G_MLP + l:_ROW_G_MLP + l + 1, :])
            finish("conv_b", (slice(l, l + 1), slice(None)), pack_ref[_ROW_CONV + 4 * l + 3:_ROW_CONV + 4 * l + 4, :])
            finish("sinks", (slice(l, l + 1), slice(None)), pack_ref[_ROW_SINKS + l:_ROW_SINKS + l + 1, 0:N_Q_HEADS])
            for hf in range(2):
                finish("b_gates", (slice(l, l + 1), slice(hf * D, (hf + 1) * D)),
                       pack_ref[_ROW_B_GATES + 2 * l + hf:_ROW_B_GATES + 2 * l + hf + 1, :])
        finish("g_final", (slice(0, 1), slice(None)), pack_ref[_ROW_G_FINAL:_ROW_G_FINAL + 1, :])

        def conv_w_chip(js):
            for l in range(DEPTH):
                for k in range(3):
                    row = _ROW_CONV + 4 * l + k
                    finish("conv_w", (l, slice(k, k + 1), slice(None)), pack_ref[row:row + 1, js * QW:(js + 1) * QW])

        _for_my_chip(j, conv_w_chip)

    vm = pl.BlockSpec(memory_space=pltpu.VMEM)
    ins = (parts["g_mix"] + parts["b_gates"] + parts["sinks"] + parts["conv"] + parts["g_mlp"] + [parts["loss"]]
           + [params[n] for n in names] + [moms[n] for n in names] + [vels[n] for n in names])
    out_shape = [jax.ShapeDtypeStruct((1, 1), F32)]
    for n in names:
        out_shape += [jax.ShapeDtypeStruct(params[n].shape, F32)] * 4
    res = pl.pallas_call(
        body, out_shape=out_shape, in_specs=[vm] * len(ins), out_specs=[vm] * len(out_shape),
        scratch_shapes=[pltpu.VMEM((SMALL_ROWS, D), F32), pltpu.VMEM((n_dev, SMALL_ROWS, D), F32),
                        pltpu.SemaphoreType.DMA((n_dev - 1,)), pltpu.SemaphoreType.DMA((n_dev - 1,))],
        name="small_allreduce_adamw")(*ins)
    loss = res[0]
    out = {n: res[1 + 4 * i:5 + 4 * i] for i, n in enumerate(names)}
    return loss, out


def kernel(x, g_mix, w_in, b_gates, sinks, w_attn_out, conv_w, conv_b, w_conv_out, w_o, g_mlp, w_up, w_down, g_final, loss_target, m_g_mix, m_w_in, m_b_gates, m_sinks, m_w_attn_out, m_conv_w, m_conv_b, m_w_conv_out, m_w_o, m_g_mlp, m_w_up, m_w_down, m_g_final, v_g_mix, v_w_in, v_b_gates, v_sinks, v_w_attn_out, v_conv_w, v_conv_b, v_w_conv_out, v_w_o, v_g_mlp, v_w_up, v_w_down, v_g_final):
    B, S, D = x.shape
    T = B * S
    big_w = dict(w_in=w_in, w_attn_out=w_attn_out, w_conv_out=w_conv_out, w_o=w_o, w_up=w_up, w_down=w_down)
    big_m = dict(w_in=m_w_in, w_attn_out=m_w_attn_out, w_conv_out=m_w_conv_out, w_o=m_w_o, w_up=m_w_up, w_down=m_w_down)
    big_v = dict(w_in=v_w_in, w_attn_out=v_w_attn_out, w_conv_out=v_w_conv_out, w_o=v_w_o, w_up=v_w_up, w_down=v_w_down)

    c_arr = lax.axis_index("c").astype(jnp.int32).reshape(1)
    j_arr = (2 * lax.axis_index("x") + lax.axis_index("y")).astype(jnp.int32).reshape(1)
    jc_arr = jnp.concatenate([j_arr, c_arr])
    order = [(n, l) for n, _, _, _ in BIG for l in range(DEPTH)]
    dims = {n: (r, c_, k) for n, r, c_, k in BIG}

    full = {(n, l): _cast_into_full(j_arr, big_w[n], l, *dims[n]) for n, l in order}
    mixers = ("w_attn_out", "w_conv_out", "w_o")

    def gather_of(keys):
        return _gather_comm([(full[k],) + dims[k[0]] for k in keys])

    carried = {("proj", 0): [(n, 0) for n in mixers], ("attn", 0): [("w_up", 0), ("w_down", 0)],
               ("mlp", 0): [("w_in", 1)] + [(n, 1) for n in mixers], ("attn", 1): [("w_up", 1), ("w_down", 1)]}

    def carry(fn, where, *args):
        keys = carried.get(where)
        if keys is None:
            return fn(*args)
        res, got = fn(*args, comm=gather_of(keys))
        full.update(zip(keys, got))
        return res

    first = _gather([(full["w_in", 0],) + dims["w_in"]], conv_w)
    full["w_in", 0] = first[0]
    conv_w_full = jnp.transpose(first[1], (0, 2, 1, 3)).reshape(DEPTH, 3, D)

    xs = [x.reshape(T, D)]
    saved = []
    for l in range(DEPTH):
        ht, pqkv, pconv, pgate = carry(_norm_proj, ("proj", l), xs[-1], g_mix[l:l + 1], full["w_in", l])
        att, att_t, lse = carry(_attn_fwd, ("attn", l), pqkv, sinks[l], S)
        cv, cv_t = _conv_fwd(pconv, conv_w_full[l], conv_b[l:l + 1], S)
        x1, ya, yc, mg_t = _mix_fwd(xs[-1], att, cv, pgate, b_gates[l:l + 1], full["w_attn_out", l], full["w_conv_out", l],
                                    full["w_o", l])
        x2, a = carry(_mlp_fwd, ("mlp", l), x1, g_mlp[l:l + 1], full["w_up", l], full["w_down", l])
        saved.append(dict(ht=ht, pqkv=pqkv, pconv=pconv, pgate=pgate, att=att, att_t=att_t, lse=lse, cv_t=cv_t, x1=x1, ya=ya,
                          yc=yc, mg_t=mg_t, a=a))
        xs.append(x2)

    loss_stats, dx = _loss_bwd(xs[-1], g_final.reshape(1, D), loss_target.reshape(T, D))

    parts = dict(g_mix=[None] * DEPTH, b_gates=[None] * DEPTH, sinks=[None] * DEPTH, conv=[None] * DEPTH,
                 g_mlp=[None] * DEPTH, loss=loss_stats)
    gf, gb = {}, {}
    pre, got = {}, {}

    def pre_reduce(keys):
        sib = _sibling_exchange([(gb[k],) + dims[k[0]] for k in keys])
        for k, s in zip(keys, sib):
            pre[k] = _half_add(c_arr, gf[k], s, *dims[k[0]])
        return keys

    def exchange_in(fn, keys, *args):
        res, arrived = fn(*args, comm=_chip_exchange_comm([pre[k][0] for k in keys]))
        got.update(zip(keys, arrived))
        return res

    pending = None
    for l in reversed(range(DEPTH)):
        W = {n: full[(n, l)] for n in big_w}
        sv = saved[l]
        mlp_args = (dx, sv["x1"], sv["a"], g_mlp[l:l + 1], W["w_up"], W["w_down"])
        dx1, da, u_t, h2_t, parts["g_mlp"][l] = exchange_in(_mlp_bwd, pending, *mlp_args) if pending else _mlp_bwd(*mlp_args)
        gf["w_up", l], gb["w_up", l] = _dw(h2_t, da)
        gf["w_down", l], gb["w_down", l] = _dw(u_t, dx)
        datt, dcv, dya, dyc, dgate, parts["b_gates"][l] = _mix_bwd(
            dx1, sv["ya"], sv["yc"], sv["pgate"], b_gates[l:l + 1], W["w_attn_out"], W["w_conv_out"], W["w_o"])
        gf["w_o", l], gb["w_o", l] = _dw(sv["mg_t"], dx1)
        gf["w_attn_out", l], gb["w_attn_out", l] = _dw(sv["att_t"], dya)
        gf["w_conv_out", l], gb["w_conv_out", l] = _dw(sv["cv_t"], dyc)
        dconv, parts["conv"][l] = _conv_bwd(dcv, sv["pconv"], conv_w_full[l], conv_b[l:l + 1], S)
        attn_args = (sv["pqkv"], sv["att"], datt, sv["lse"], sinks[l], S)
        last = l == 0
        if last:
            keys = pre_reduce([(n, l) for n in mixers + ("w_up", "w_down")])
            dq, dkv, parts["sinks"][l] = exchange_in(_attn_bwd, keys, *attn_args)
        else:
            dq, dkv, parts["sinks"][l] = _attn_bwd(*attn_args)
        pieces = [(dq, D), (dkv, QKV_W - D), (dconv, CONV_W), (dgate, GATE_W)]
        gf["w_in", l], gb["w_in", l] = _dw_pieces(sv["ht"], pieces)
        in_args = (pieces, W["w_in"], xs[l], dx1, g_mix[l:l + 1])
        if last:
            dx, parts["g_mix"][l] = exchange_in(_inproj_bwd, pre_reduce([("w_in", l)]), *in_args)
        else:
            dx, parts["g_mix"][l] = _inproj_bwd(*in_args)
            pending = pre_reduce([(n, l) for n, _, _, _ in BIG])

    mine = {}
    for n, l in order:
        mine[n] = _owner_sum(jc_arr, pre[n, l][1], got[n, l], l, mine.get(n))
    grads = dict(zip(mine, _sibling_assemble(list(mine.values()))))

    res = {}
    for n in big_w:
        d, nm, nv = _adamw(big_w[n], grads[n], big_m[n], big_v[n])
        res[n] = (grads[n], d, nm, nv)

    small_p = dict(g_mix=g_mix, b_gates=b_gates, sinks=sinks, conv_w=conv_w, conv_b=conv_b, g_mlp=g_mlp, g_final=g_final.reshape(1, D))
    small_m = dict(g_mix=m_g_mix, b_gates=m_b_gates, sinks=m_sinks, conv_w=m_conv_w, conv_b=m_conv_b, g_mlp=m_g_mlp,
                   g_final=m_g_final.reshape(1, D))
    small_v = dict(g_mix=v_g_mix, b_gates=v_b_gates, sinks=v_sinks, conv_w=v_conv_w, conv_b=v_conv_b, g_mlp=v_g_mlp,
                   g_final=v_g_final.reshape(1, D))
    loss, small = _small_step(parts, small_p, small_m, small_v)
    for n, vals in small.items():
        res[n] = tuple(v.reshape(D) for v in vals) if n == "g_final" else tuple(vals)

    weights = ["g_mix", "w_in", "b_gates", "sinks", "w_attn_out", "conv_w", "conv_b", "w_conv_out", "w_o", "g_mlp", "w_up",
               "w_down", "g_final"]
    out = [loss.reshape(()), dx.reshape(B, S, D)]
    for k in range(4):
        out += [res[n][k] for n in weights]
    return tuple(out)
```

```python
import functools

import numpy as np
import jax
import jax.numpy as jnp
from jax import lax
from jax.experimental import pallas as pl
from jax.experimental.pallas import tpu as pltpu

F32 = jnp.float32
BF16 = jnp.bfloat16

D_MODEL = 1024
HEAD_DIM = 64
N_Q_HEADS = 16
N_KV_HEADS = 4
GQA_GROUP = 4
WINDOW = 128
D_FF = 4096
DEPTH = 2
RMS_EPS = 1e-6
NEG_INF = -1e30
ATTN_SCALE = HEAD_DIM ** -0.5
QKV_W = 1536
CONV_W = 3072
GATE_W = 2048
IN_COLS = QKV_W + CONV_W + GATE_W
COL_TILE = 512
N_CHIP = 4
ADAM_LR = 0.001
ADAM_B1 = 0.9
ADAM_B2 = 0.999
ADAM_EPS = 1e-08
ADAM_WD = 0.01
ADAM_STEP = 10
V7X_VMEM_BYTES = 64 * 2 ** 20
VMEM_LIMIT = V7X_VMEM_BYTES - 8 * 2 ** 20
MESH = pl.DeviceIdType.MESH
ANY = pl.BlockSpec(memory_space=pl.ANY)
SMALL_ROWS = 24

_SLOPES = [float(v) for v in np.power(np.float32(2.0), -8.0 * np.arange(1, N_Q_HEADS + 1, dtype=np.float32) / N_Q_HEADS)]


def _params(*sem):
    return pltpu.CompilerParams(dimension_semantics=sem, vmem_limit_bytes=VMEM_LIMIT)


class _Hosted:
    def __init__(self, inputs, out_shape, aliases, scratch, start, finish):
        self.inputs, self.out_shape, self.aliases, self.scratch = list(inputs), list(out_shape), dict(aliases), list(scratch)
        self.start, self.finish = start, finish


def _hosted_call(comm, kern, *, out_shape, grid, in_specs, out_specs, args, name, sem, scratch_shapes=()):
    single = not isinstance(out_shape, (tuple, list))
    outs = [out_shape] if single else list(out_shape)
    ospecs = [out_specs] if single else list(out_specs)
    if comm is None:
        res = pl.pallas_call(kern, out_shape=outs, grid=grid, in_specs=list(in_specs), out_specs=ospecs,
                             scratch_shapes=list(scratch_shapes), name=name, compiler_params=_params(*sem))(*args)
        return res[0] if single else res
    n_in, n_out, n_scr = len(args), len(outs), len(scratch_shapes)
    ci, co, cs = len(comm.inputs), len(comm.out_shape), len(comm.scratch)

    def body(*refs):
        cuts = np.cumsum([0, n_in, ci, n_out, co, n_scr, cs])
        a, b, c, d, e, f = [refs[lo:hi] for lo, hi in zip(cuts[:-1], cuts[1:])]
        ids = [pl.program_id(k) for k in range(len(grid))]
        first = functools.reduce(jnp.logical_and, [i == 0 for i in ids])
        last = functools.reduce(jnp.logical_and, [i == n - 1 for i, n in zip(ids, grid)])
        pl.when(first)(lambda: comm.start(b, d, f))
        kern(*a, *c, *e)
        pl.when(last)(lambda: comm.finish(b, d, f))

    res = pl.pallas_call(
        body, out_shape=outs + comm.out_shape, grid=grid, in_specs=list(in_specs) + [ANY] * ci, out_specs=ospecs + [ANY] * co,
        scratch_shapes=list(scratch_shapes) + comm.scratch,
        input_output_aliases={n_in + i: n_out + o for i, o in comm.aliases.items()},
        name=name + "_carrier", compiler_params=_params(*(["arbitrary"] * len(grid))))(*args, *comm.inputs)
    main = res[:n_out]
    return (main[0] if single else main), res[n_out:]


def _nt(a, b):
    return lax.dot_general(a, b, (((1,), (1,)), ((), ())), preferred_element_type=F32)


def _tn(a, b):
    return lax.dot_general(a, b, (((0,), (0,)), ((), ())), preferred_element_type=F32)


def _nn(a, b):
    return jnp.dot(a, b, preferred_element_type=F32)


def _rms_stats(xf):
    r = lax.rsqrt(jnp.mean(xf * xf, axis=-1, keepdims=True) + RMS_EPS)
    return r, xf * r


def _rms_bwd(dh, xh, r, g):
    dxh = dh * g
    dx = r * (dxh - xh * jnp.mean(dxh * xh, axis=-1, keepdims=True))
    dg = jnp.sum(dh * xh, axis=0, keepdims=True)
    return dx, dg


def _resident(shape):
    return pl.BlockSpec(shape, lambda *_: (0,) * len(shape), pipeline_mode=pl.Buffered(1))


def _norm_proj(x, g, w, comm=None):
    T, D = x.shape
    tm = min(256, T)
    widths = (QKV_W, CONV_W, GATE_W)

    def kern(x_ref, g_ref, w_ref, ht_ref, *o_refs):
        _, xh = _rms_stats(x_ref[...])
        h = (xh * g_ref[...]).astype(BF16)
        ht_ref[...] = h.T
        off = 0
        for o_ref, wd in zip(o_refs, widths):
            o_ref[...] = _nn(h, w_ref[:, off:off + wd]).astype(BF16)
            off += wd

    row = lambda wd: pl.BlockSpec((tm, wd), lambda i: (i, 0))
    return _hosted_call(
        comm, kern,
        out_shape=[jax.ShapeDtypeStruct((D, T), BF16)] + [jax.ShapeDtypeStruct((T, wd), BF16) for wd in widths],
        grid=(T // tm,), in_specs=[row(D), pl.BlockSpec((1, D), lambda i: (0, 0)), _resident((D, IN_COLS))],
        out_specs=[pl.BlockSpec((D, tm), lambda i: (0, i))] + [row(wd) for wd in widths],
        name="norm_proj", sem=("parallel",), args=(x, g, w))


def _band_masks(first_i):
    r = lax.broadcasted_iota(jnp.int32, (GQA_GROUP * WINDOW, 2 * WINDOW), 0)
    jj = lax.broadcasted_iota(jnp.int32, (GQA_GROUP * WINDOW, 2 * WINDOW), 1)
    dist = WINDOW + (r & (WINDOW - 1)) - jj
    valid = (dist >= 0) & (dist < WINDOW) & ((jj + (1 - first_i) * WINDOW) >= WINDOW)
    return dist.astype(F32), valid


def _per_head_col(vals):
    rb = lax.broadcasted_iota(jnp.int32, (GQA_GROUP * WINDOW, 1), 0) >> 7
    col = jnp.full((GQA_GROUP * WINDOW, 1), vals[3], F32)
    for g in (2, 1, 0):
        col = jnp.where(rb == g, vals[g], col)
    return col


def _stack_heads(ref, base, hk):
    return jnp.concatenate(
        [ref[:, base + HEAD_DIM * (GQA_GROUP * hk + g): base + HEAD_DIM * (GQA_GROUP * hk + g + 1)] for g in range(GQA_GROUP)],
        axis=0)


def _attn_fwd(pqkv, sinks, seq, comm=None):
    T = pqkv.shape[0]
    nblk = seq // WINDOW
    K0, V0 = N_Q_HEADS * HEAD_DIM, N_Q_HEADS * HEAD_DIM + N_KV_HEADS * HEAD_DIM

    def kern(sink_ref, cur_ref, prev_ref, o_ref, ot_ref, lse_ref):
        i = pl.program_id(0)
        first_i = ((i % nblk) == 0).astype(jnp.int32)
        distf, valid = _band_masks(first_i)
        outs = []
        for hk in range(N_KV_HEADS):
            k_band = jnp.concatenate([prev_ref[:, HEAD_DIM * hk:HEAD_DIM * (hk + 1)],
                                      cur_ref[:, K0 + HEAD_DIM * hk:K0 + HEAD_DIM * (hk + 1)]], axis=0)
            v_band = jnp.concatenate([prev_ref[:, 256 + HEAD_DIM * hk:256 + HEAD_DIM * (hk + 1)],
                                      cur_ref[:, V0 + HEAD_DIM * hk:V0 + HEAD_DIM * (hk + 1)]], axis=0)
            q_g = _stack_heads(cur_ref, 0, hk)
            slope = _per_head_col([_SLOPES[GQA_GROUP * hk + g] for g in range(GQA_GROUP)])
            sink = _per_head_col([sink_ref[GQA_GROUP * hk + g] for g in range(GQA_GROUP)])
            s = _nt(q_g, k_band) * ATTN_SCALE + (-slope) * distf
            s = jnp.where(valid, s, NEG_INF)
            m = jnp.maximum(jnp.max(s, axis=-1, keepdims=True), sink)
            p = jnp.exp(s - m)
            denom = jnp.sum(p, axis=-1, keepdims=True) + jnp.exp(sink - m)
            o_g = _nn((p / denom).astype(BF16), v_band)
            lse_g = m + jnp.log(denom)
            for g in range(GQA_GROUP):
                h = GQA_GROUP * hk + g
                outs.append(o_g[WINDOW * g:WINDOW * (g + 1)].astype(BF16))
                lse_ref[:, h:h + 1] = lse_g[WINDOW * g:WINDOW * (g + 1)]
        o = jnp.concatenate(outs, axis=1)
        o_ref[...] = o
        ot_ref[...] = o.T

    return _hosted_call(
        comm, kern,
        out_shape=(jax.ShapeDtypeStruct((T, D_MODEL), BF16), jax.ShapeDtypeStruct((D_MODEL, T), BF16),
                   jax.ShapeDtypeStruct((T, N_Q_HEADS), F32)),
        grid=(T // WINDOW,),
        in_specs=[pl.BlockSpec(memory_space=pltpu.SMEM),
                  pl.BlockSpec((WINDOW, QKV_W), lambda i: (i, 0)),
                  pl.BlockSpec((WINDOW, 512), lambda i: (jnp.maximum(i - 1, 0), 2))],
        out_specs=(pl.BlockSpec((WINDOW, D_MODEL), lambda i: (i, 0)), pl.BlockSpec((D_MODEL, WINDOW), lambda i: (0, i)),
                   pl.BlockSpec((WINDOW, N_Q_HEADS), lambda i: (i, 0))),
        name="attn_fwd", sem=("parallel",), args=(sinks, pqkv, pqkv))


def _pick_row(a, row):
    rid = lax.broadcasted_iota(jnp.int32, a.shape, 0)
    return jnp.sum(jnp.where(rid == row, a, 0.0), axis=0, keepdims=True)


def _conv_taps(yc, halo_yc, first_i):
    keep = (1 - first_i).astype(F32)
    p1 = _pick_row(halo_yc, 15) * keep
    p2 = _pick_row(halo_yc, 14) * keep
    rowid = lax.broadcasted_iota(jnp.int32, yc.shape, 0)
    s1 = jnp.where(rowid == 0, p1, pltpu.roll(yc, 1, 0))
    s2 = jnp.where(rowid == 0, p2, jnp.where(rowid == 1, p1, pltpu.roll(yc, 2, 0)))
    return s1, s2


def _conv_fwd(pconv, conv_w, conv_b, seq):
    T = pconv.shape[0]
    tm = min(256, seq)
    per_seq = seq // tm
    D = D_MODEL

    def kern(cur_ref, halo_ref, w_ref, b_ref, o_ref, ot_ref):
        i = pl.program_id(0)
        first_i = ((i % per_seq) == 0).astype(jnp.int32)
        cb = cur_ref[:, 0:D].astype(F32)
        yc = cur_ref[:, D:2 * D].astype(F32) * cur_ref[:, 2 * D:3 * D].astype(F32)
        halo_yc = halo_ref[:, D:2 * D].astype(F32) * halo_ref[:, 2 * D:3 * D].astype(F32)
        s1, s2 = _conv_taps(yc, halo_yc, first_i)
        z = w_ref[0:1, :] * s2 + w_ref[1:2, :] * s1 + w_ref[2:3, :] * yc
        cv = (cb * (z + b_ref[...])).astype(BF16)
        o_ref[...] = cv
        ot_ref[...] = cv.T

    return pl.pallas_call(
        kern, out_shape=(jax.ShapeDtypeStruct((T, D), BF16), jax.ShapeDtypeStruct((D, T), BF16)), grid=(T // tm,),
        in_specs=[pl.BlockSpec((tm, CONV_W), lambda i: (i, 0)),
                  pl.BlockSpec((16, CONV_W), lambda i: (jnp.maximum(i * (tm // 16) - 1, 0), 0)),
                  pl.BlockSpec((3, D), lambda i: (0, 0)), pl.BlockSpec((1, D), lambda i: (0, 0))],
        out_specs=(pl.BlockSpec((tm, D), lambda i: (i, 0)), pl.BlockSpec((D, tm), lambda i: (0, i))),
        name="conv_fwd", compiler_params=_params("parallel"))(pconv, pconv, conv_w, conv_b)


def _mix_fwd(x, att, cv, pgate, b_gates, wao, wco, wo):
    T, D = x.shape
    tm = min(512, T)

    def kern(x_ref, att_ref, cv_ref, pg_ref, bg_ref, wao_ref, wco_ref, wo_ref, x1_ref, ya_ref, yc_ref, mgt_ref):
        ya = _nn(att_ref[...], wao_ref[...])
        yc = _nn(cv_ref[...], wco_ref[...])
        sa = jax.nn.sigmoid(pg_ref[:, 0:D].astype(F32) + bg_ref[:, 0:D])
        sc = jax.nn.sigmoid(pg_ref[:, D:2 * D].astype(F32) + bg_ref[:, D:2 * D])
        mg = (sa * ya + sc * yc).astype(BF16)
        ya_ref[...] = ya.astype(BF16)
        yc_ref[...] = yc.astype(BF16)
        mgt_ref[...] = mg.T
        x1_ref[...] = x_ref[...] + _nn(mg, wo_ref[...])

    row = lambda w: pl.BlockSpec((tm, w), lambda i: (i, 0))
    full = lambda a, b: pl.BlockSpec((a, b), lambda i: (0, 0))
    bf = jax.ShapeDtypeStruct((T, D), BF16)
    return pl.pallas_call(
        kern, out_shape=(jax.ShapeDtypeStruct((T, D), F32), bf, bf, jax.ShapeDtypeStruct((D, T), BF16)), grid=(T // tm,),
        in_specs=[row(D), row(D), row(D), row(GATE_W), full(1, GATE_W)] + [_resident((D, D))] * 3,
        out_specs=(row(D), row(D), row(D), pl.BlockSpec((D, tm), lambda i: (0, i))),
        name="mix_fwd", compiler_params=_params("parallel"))(x, att, cv, pgate, b_gates, wao, wco, wo)


def _mlp_fwd(x1, g, wup, wdn, comm=None):
    T, D = x1.shape
    tm = min(1024, T)
    nj = D_FF // D

    def kern(x_ref, g_ref, wup_ref, wdn_ref, x2_ref, a_ref, h_scr, acc_scr):
        j = pl.program_id(1)

        @pl.when(j == 0)
        def _():
            xf = x_ref[...]
            _, xh = _rms_stats(xf)
            h_scr[...] = (xh * g_ref[...]).astype(BF16)
            acc_scr[...] = xf

        a = _nn(h_scr[...], wup_ref[...])
        a_ref[...] = a.astype(BF16)
        u = jnp.square(jnp.maximum(a, 0.0)).astype(BF16)
        acc_scr[...] += _nn(u, wdn_ref[...])

        @pl.when(j == nj - 1)
        def _():
            x2_ref[...] = acc_scr[...]

    return _hosted_call(
        comm, kern, out_shape=(jax.ShapeDtypeStruct((T, D), F32), jax.ShapeDtypeStruct((T, D_FF), BF16)), grid=(T // tm, nj),
        in_specs=[pl.BlockSpec((tm, D), lambda i, j: (i, 0)), pl.BlockSpec((1, D), lambda i, j: (0, 0)),
                  pl.BlockSpec((D, D), lambda i, j: (0, j)), pl.BlockSpec((D, D), lambda i, j: (j, 0))],
        out_specs=(pl.BlockSpec((tm, D), lambda i, j: (i, 0)), pl.BlockSpec((tm, D), lambda i, j: (i, j))),
        scratch_shapes=[pltpu.VMEM((tm, D), BF16), pltpu.VMEM((tm, D), F32)],
        name="mlp_fwd", sem=("parallel", "arbitrary"), args=(x1, g, wup, wdn))


def _loss_bwd(x, g, tgt):
    T, D = x.shape
    tm = min(512, T)

    def kern(x_ref, g_ref, t_ref, st_ref, dx_ref):
        i = pl.program_id(0)

        @pl.when(i == 0)
        def _():
            st_ref[...] = jnp.zeros_like(st_ref)

        gg = g_ref[...]
        r, xh = _rms_stats(x_ref[...])
        e = xh * gg - t_ref[...]
        part = 0.5 * jnp.sum(jnp.mean(e * e, axis=-1, keepdims=True), axis=0, keepdims=True)
        dx, dg = _rms_bwd(e * (1.0 / D), xh, r, gg)
        dx_ref[...] = dx
        st_ref[0:1, :] += dg
        st_ref[1:2, 0:1] += part

    return pl.pallas_call(
        kern, out_shape=(jax.ShapeDtypeStruct((8, D), F32), jax.ShapeDtypeStruct((T, D), F32)), grid=(T // tm,),
        in_specs=[pl.BlockSpec((tm, D), lambda i: (i, 0)), pl.BlockSpec((1, D), lambda i: (0, 0)),
                  pl.BlockSpec((tm, D), lambda i: (i, 0))],
        out_specs=(pl.BlockSpec((8, D), lambda i: (0, 0)), pl.BlockSpec((tm, D), lambda i: (i, 0))),
        name="loss_bwd", compiler_params=_params("arbitrary"))(x, g, tgt)


def _mlp_bwd(dx2, x1, a, g, wup, wdn, comm=None):
    T, D = x1.shape
    tm = min(512, T)
    nj = D_FF // D

    def kern(dx2_ref, x1_ref, a_ref, g_ref, wup_ref, wdn_ref, dx1_ref, da_ref, ut_ref, h2t_ref, dg_ref, dyb_scr, acc_scr):
        i, j = pl.program_id(0), pl.program_id(1)

        @pl.when((i == 0) & (j == 0))
        def _():
            dg_ref[...] = jnp.zeros_like(dg_ref)

        @pl.when(j == 0)
        def _():
            dyb_scr[...] = dx2_ref[...].astype(BF16)
            acc_scr[...] = jnp.zeros_like(acc_scr)

        du = _nt(dyb_scr[...], wdn_ref[...])
        relu = jnp.maximum(a_ref[...].astype(F32), 0.0)
        da = (du * (2.0 * relu)).astype(BF16)
        da_ref[...] = da
        ut_ref[...] = jnp.square(relu).astype(BF16).T
        acc_scr[...] += _nt(da, wup_ref[...])

        @pl.when(j == nj - 1)
        def _():
            gg = g_ref[...]
            r, xh = _rms_stats(x1_ref[...])
            h2t_ref[...] = (xh * gg).astype(BF16).T
            dx, dg = _rms_bwd(acc_scr[...], xh, r, gg)
            dx1_ref[...] = dx2_ref[...] + dx
            dg_ref[...] += dg

    return _hosted_call(
        comm, kern,
        out_shape=(jax.ShapeDtypeStruct((T, D), F32), jax.ShapeDtypeStruct((T, D_FF), BF16),
                   jax.ShapeDtypeStruct((D_FF, T), BF16), jax.ShapeDtypeStruct((D, T), BF16), jax.ShapeDtypeStruct((1, D), F32)),
        grid=(T // tm, nj),
        in_specs=[pl.BlockSpec((tm, D), lambda i, j: (i, 0)), pl.BlockSpec((tm, D), lambda i, j: (i, 0)),
                  pl.BlockSpec((tm, D), lambda i, j: (i, j)), pl.BlockSpec((1, D), lambda i, j: (0, 0)),
                  pl.BlockSpec((D, D), lambda i, j: (0, j)), pl.BlockSpec((D, D), lambda i, j: (j, 0))],
        out_specs=(pl.BlockSpec((tm, D), lambda i, j: (i, 0)), pl.BlockSpec((tm, D), lambda i, j: (i, j)),
                   pl.BlockSpec((D, tm), lambda i, j: (j, i)), pl.BlockSpec((D, tm), lambda i, j: (0, i)),
                   pl.BlockSpec((1, D), lambda i, j: (0, 0))),
        scratch_shapes=[pltpu.VMEM((tm, D), BF16), pltpu.VMEM((tm, D), F32)],
        name="mlp_bwd", sem=("arbitrary", "arbitrary"), args=(dx2, x1, a, g, wup, wdn))


def _mix_bwd(dx1, ya, yc, pgate, b_gates, wao, wco, wo):
    T, D = dx1.shape
    tm = min(512, T)

    def kern(dx_ref, ya_ref, yc_ref, pg_ref, bg_ref, wao_ref, wco_ref, wo_ref,
             datt_ref, dcv_ref, dya_ref, dyc_ref, dgt_ref, dbg_ref):
        @pl.when(pl.program_id(0) == 0)
        def _():
            dbg_ref[...] = jnp.zeros_like(dbg_ref)

        dm = _nt(dx_ref[...].astype(BF16), wo_ref[...])
        sa = jax.nn.sigmoid(pg_ref[:, 0:D].astype(F32) + bg_ref[:, 0:D])
        sc = jax.nn.sigmoid(pg_ref[:, D:2 * D].astype(F32) + bg_ref[:, D:2 * D])
        dya = (dm * sa).astype(BF16)
        dyc = (dm * sc).astype(BF16)
        dga = dm * ya_ref[...].astype(F32) * (sa * (1.0 - sa))
        dgc = dm * yc_ref[...].astype(F32) * (sc * (1.0 - sc))
        dya_ref[...] = dya
        dyc_ref[...] = dyc
        dgt_ref[:, 0:D] = dga.astype(BF16)
        dgt_ref[:, D:2 * D] = dgc.astype(BF16)
        dbg_ref[:, 0:D] += jnp.sum(dga, axis=0, keepdims=True)
        dbg_ref[:, D:2 * D] += jnp.sum(dgc, axis=0, keepdims=True)
        datt_ref[...] = _nt(dya, wao_ref[...]).astype(BF16)
        dcv_ref[...] = _nt(dyc, wco_ref[...]).astype(BF16)

    row = lambda w: pl.BlockSpec((tm, w), lambda i: (i, 0))
    full = lambda a, b: pl.BlockSpec((a, b), lambda i: (0, 0))
    bf = jax.ShapeDtypeStruct((T, D), BF16)
    return pl.pallas_call(
        kern,
        out_shape=(bf, bf, bf, bf, jax.ShapeDtypeStruct((T, GATE_W), BF16), jax.ShapeDtypeStruct((1, GATE_W), F32)),
        grid=(T // tm,),
        in_specs=[row(D), row(D), row(D), row(GATE_W), full(1, GATE_W)] + [_resident((D, D))] * 3,
        out_specs=(row(D), row(D), row(D), row(D), row(GATE_W), full(1, GATE_W)),
        name="mix_bwd", compiler_params=_params("arbitrary"))(dx1, ya, yc, pgate, b_gates, wao, wco, wo)


def _conv_bwd(dcv, pconv, conv_w, conv_b, seq):
    T = pconv.shape[0]
    tm = min(256, seq)
    per_seq = seq // tm
    D = D_MODEL
    nb16 = T // 16

    def kern(dcv_ref, dcvn_ref, cur_ref, prev_ref, next_ref, w_ref, b_ref, o_ref, dwb_ref):
        i = pl.program_id(0)

        @pl.when(i == 0)
        def _():
            dwb_ref[...] = jnp.zeros_like(dwb_ref)

        first_i = ((i % per_seq) == 0).astype(jnp.int32)
        keep_next = 1.0 - (((i + 1) % per_seq) == 0).astype(F32)
        cb = cur_ref[:, 0:D].astype(F32)
        cc = cur_ref[:, D:2 * D].astype(F32)
        cu = cur_ref[:, 2 * D:3 * D].astype(F32)
        yc = cc * cu
        halo_yc = prev_ref[:, D:2 * D].astype(F32) * prev_ref[:, 2 * D:3 * D].astype(F32)
        s1, s2 = _conv_taps(yc, halo_yc, first_i)
        w0, w1, w2 = w_ref[0:1, :], w_ref[1:2, :], w_ref[2:3, :]
        z = w0 * s2 + w1 * s1 + w2 * yc
        dcv = dcv_ref[...].astype(F32)
        dz = dcv * cb
        dzn = dcvn_ref[...].astype(F32) * next_ref[:, 0:D].astype(F32) * keep_next
        n1, n2 = _pick_row(dzn, 0), _pick_row(dzn, 1)
        rowid = lax.broadcasted_iota(jnp.int32, dz.shape, 0)
        u1 = jnp.where(rowid == tm - 1, n1, pltpu.roll(dz, tm - 1, 0))
        u2 = jnp.where(rowid == tm - 1, n2, jnp.where(rowid == tm - 2, n1, pltpu.roll(dz, tm - 2, 0)))
        dyc = w2 * dz + w1 * u1 + w0 * u2
        o_ref[:, 0:D] = (dcv * (z + b_ref[...])).astype(BF16)
        o_ref[:, D:2 * D] = (dyc * cu).astype(BF16)
        o_ref[:, 2 * D:3 * D] = (dyc * cc).astype(BF16)
        dwb_ref[0:1, :] += jnp.sum(dz * s2, axis=0, keepdims=True)
        dwb_ref[1:2, :] += jnp.sum(dz * s1, axis=0, keepdims=True)
        dwb_ref[2:3, :] += jnp.sum(dz * yc, axis=0, keepdims=True)
        dwb_ref[3:4, :] += jnp.sum(dz, axis=0, keepdims=True)

    prev_map = lambda i: (jnp.maximum(i * (tm // 16) - 1, 0), 0)
    next_map = lambda i: (jnp.minimum((i + 1) * (tm // 16), nb16 - 1), 0)
    return pl.pallas_call(
        kern, out_shape=(jax.ShapeDtypeStruct((T, CONV_W), BF16), jax.ShapeDtypeStruct((8, D), F32)), grid=(T // tm,),
        in_specs=[pl.BlockSpec((tm, D), lambda i: (i, 0)), pl.BlockSpec((16, D), next_map),
                  pl.BlockSpec((tm, CONV_W), lambda i: (i, 0)), pl.BlockSpec((16, CONV_W), prev_map),
                  pl.BlockSpec((16, CONV_W), next_map),
                  pl.BlockSpec((3, D), lambda i: (0, 0)), pl.BlockSpec((1, D), lambda i: (0, 0))],
        out_specs=(pl.BlockSpec((tm, CONV_W), lambda i: (i, 0)), pl.BlockSpec((8, D), lambda i: (0, 0))),
        name="conv_bwd", compiler_params=_params("arbitrary"))(dcv, dcv, pconv, pconv, pconv, conv_w, conv_b)


def _attn_bwd(pqkv, att, datt, lse, sinks, seq, comm=None):
    T = pqkv.shape[0]
    nblk = seq // WINDOW
    nseq = T // seq
    K0, V0 = N_Q_HEADS * HEAD_DIM, N_Q_HEADS * HEAD_DIM + N_KV_HEADS * HEAD_DIM
    KVW = N_KV_HEADS * HEAD_DIM

    def kern(sink_ref, cur_ref, prev_ref, o_ref, do_ref, lse_ref, dq_ref, dkv_ref, ds_ref, kc_scr, vc_scr):
        b, st = pl.program_id(0), pl.program_id(1)

        @pl.when((b == 0) & (st == 0))
        def _():
            ds_ref[...] = jnp.zeros_like(ds_ref)

        @pl.when(st == 0)
        def _():
            kc_scr[...] = jnp.zeros_like(kc_scr)
            vc_scr[...] = jnp.zeros_like(vc_scr)

        @pl.when(st < nblk)
        def _():
            first_i = (st == 0).astype(jnp.int32)
            distf, valid = _band_masks(first_i)
            od = o_ref[...].astype(F32) * do_ref[...].astype(F32)
            dqs = []
            for hk in range(N_KV_HEADS):
                k_band = jnp.concatenate([prev_ref[:, HEAD_DIM * hk:HEAD_DIM * (hk + 1)],
                                          cur_ref[:, K0 + HEAD_DIM * hk:K0 + HEAD_DIM * (hk + 1)]], axis=0)
                v_band = jnp.concatenate([prev_ref[:, KVW + HEAD_DIM * hk:KVW + HEAD_DIM * (hk + 1)],
                                          cur_ref[:, V0 + HEAD_DIM * hk:V0 + HEAD_DIM * (hk + 1)]], axis=0)
                q_g = _stack_heads(cur_ref, 0, hk)
                do_g = _stack_heads(do_ref, 0, hk)
                heads = [GQA_GROUP * hk + g for g in range(GQA_GROUP)]
                slope = _per_head_col([_SLOPES[h] for h in heads])
                sink = _per_head_col([sink_ref[h] for h in heads])
                lse_g = jnp.concatenate([lse_ref[:, h:h + 1] for h in heads], axis=0)
                d_g = jnp.concatenate(
                    [jnp.sum(od[:, HEAD_DIM * h:HEAD_DIM * (h + 1)], axis=-1, keepdims=True) for h in heads], axis=0)
                s = _nt(q_g, k_band) * ATTN_SCALE + (-slope) * distf
                p = jnp.where(valid, jnp.exp(s - lse_g), 0.0)
                dp = _nt(do_g, v_band)
                dsb = (p * (dp - d_g)).astype(BF16)
                dq_g = _nn(dsb, k_band) * ATTN_SCALE
                dk_b = _tn(dsb, q_g) * ATTN_SCALE
                dv_b = _tn(p.astype(BF16), do_g)
                psd = jnp.exp(sink - lse_g) * d_g
                for g in range(GQA_GROUP):
                    h = heads[g]
                    dqs.append(dq_g[WINDOW * g:WINDOW * (g + 1)].astype(BF16))
                    ds_ref[0:1, h:h + 1] -= jnp.sum(psd[WINDOW * g:WINDOW * (g + 1)], axis=0, keepdims=True)
                ksl = slice(HEAD_DIM * hk, HEAD_DIM * (hk + 1))
                vsl = slice(KVW + HEAD_DIM * hk, KVW + HEAD_DIM * (hk + 1))
                dkv_ref[:, ksl] = (kc_scr[:, ksl] + dk_b[0:WINDOW]).astype(BF16)
                dkv_ref[:, vsl] = (vc_scr[:, ksl] + dv_b[0:WINDOW]).astype(BF16)
                kc_scr[:, ksl] = dk_b[WINDOW:2 * WINDOW]
                vc_scr[:, ksl] = dv_b[WINDOW:2 * WINDOW]
            dq_ref[...] = jnp.concatenate(dqs, axis=1)

        @pl.when(st == nblk)
        def _():
            dkv_ref[:, 0:KVW] = kc_scr[...].astype(BF16)
            dkv_ref[:, KVW:2 * KVW] = vc_scr[...].astype(BF16)

    cur_map = lambda b, s: (b * nblk + jnp.minimum(s, nblk - 1), 0)
    prev_row = lambda b, s: b * nblk + jnp.clip(s - 1, 0, nblk - 1)
    return _hosted_call(
        comm, kern,
        out_shape=(jax.ShapeDtypeStruct((T, D_MODEL), BF16), jax.ShapeDtypeStruct((T, 2 * KVW), BF16),
                   jax.ShapeDtypeStruct((8, 128), F32)),
        grid=(nseq, nblk + 1),
        in_specs=[pl.BlockSpec(memory_space=pltpu.SMEM),
                  pl.BlockSpec((WINDOW, QKV_W), cur_map),
                  pl.BlockSpec((WINDOW, 2 * KVW), lambda b, s: (prev_row(b, s), 2)),
                  pl.BlockSpec((WINDOW, D_MODEL), cur_map), pl.BlockSpec((WINDOW, D_MODEL), cur_map),
                  pl.BlockSpec((WINDOW, N_Q_HEADS), cur_map)],
        out_specs=(pl.BlockSpec((WINDOW, D_MODEL), cur_map),
                   pl.BlockSpec((WINDOW, 2 * KVW), lambda b, s: (prev_row(b, s), 0)),
                   pl.BlockSpec((8, 128), lambda b, s: (0, 0))),
        scratch_shapes=[pltpu.VMEM((WINDOW, KVW), F32), pltpu.VMEM((WINDOW, KVW), F32)],
        name="attn_bwd", sem=("arbitrary", "arbitrary"), args=(sinks, pqkv, pqkv, att, datt, lse))


def _piece_tiles(pieces):
    out, start = [], 0
    for arr, width in pieces:
        out.append((arr, start, width // COL_TILE))
        start += width // COL_TILE
    return out, start


def _inproj_bwd(pieces, w_in, x, dx_in, g, comm=None):
    T, D = x.shape
    tm = min(256, T)

    def kern(*refs):
        p_refs = refs[:len(pieces)]
        w_ref, x_ref, dxin_ref, g_ref, dx_ref, dg_ref = refs[len(pieces):]

        @pl.when(pl.program_id(0) == 0)
        def _():
            dg_ref[...] = jnp.zeros_like(dg_ref)

        dh, off = None, 0
        for p_ref, (_, width) in zip(p_refs, pieces):
            part = _nt(p_ref[...], w_ref[:, off:off + width])
            dh = part if dh is None else dh + part
            off += width
        gg = g_ref[...]
        r, xh = _rms_stats(x_ref[...])
        dx, dg = _rms_bwd(dh, xh, r, gg)
        dx_ref[...] = dxin_ref[...] + dx
        dg_ref[...] += dg

    row = lambda wd: pl.BlockSpec((tm, wd), lambda i: (i, 0))
    return _hosted_call(
        comm, kern, out_shape=(jax.ShapeDtypeStruct((T, D), F32), jax.ShapeDtypeStruct((1, D), F32)), grid=(T // tm,),
        in_specs=[row(wd) for _, wd in pieces] + [_resident((D, IN_COLS)), row(D), row(D), pl.BlockSpec((1, D), lambda i: (0, 0))],
        out_specs=(row(D), pl.BlockSpec((1, D), lambda i: (0, 0))),
        name="inproj_bwd", sem=("arbitrary",), args=(*[a for a, _ in pieces], w_in, x, dx_in, g))


def _dw_pieces(lhs_t, pieces):
    K, T = lhs_t.shape
    tk = min(2048, T)
    nt = T // tk
    tiles, nj = _piece_tiles(pieces)

    def kern(*refs):
        lhs_ref = refs[0]
        p_refs = refs[1:1 + len(tiles)]
        o_ref, ob_ref = refs[1 + len(tiles):]
        j, t = pl.program_id(0), pl.program_id(1)

        @pl.when(t == 0)
        def _():
            o_ref[...] = jnp.zeros_like(o_ref)

        for p_ref, (_, start, n) in zip(p_refs, tiles):
            @pl.when((j >= start) & (j < start + n))
            def _(p_ref=p_ref):
                o_ref[...] += _nn(lhs_ref[...], p_ref[...])

        @pl.when(t == nt - 1)
        def _():
            ob_ref[...] = o_ref[...].astype(BF16)

    def p_map(start, n):
        return lambda j, t: (jnp.where((j >= start) & (j < start + n), t, 0), jnp.clip(j - start, 0, n - 1))

    N = nj * COL_TILE
    return pl.pallas_call(
        kern, out_shape=(jax.ShapeDtypeStruct((K, N), F32), jax.ShapeDtypeStruct((K, N), BF16)), grid=(nj, nt),
        in_specs=[pl.BlockSpec((K, tk), lambda j, t: (0, t))] + [pl.BlockSpec((tk, COL_TILE), p_map(s, n)) for _, s, n in tiles],
        out_specs=(pl.BlockSpec((K, COL_TILE), lambda j, t: (0, j)), pl.BlockSpec((K, COL_TILE), lambda j, t: (0, j))),
        name="dw_pieces", compiler_params=_params("arbitrary", "arbitrary"))(lhs_t, *[a for a, _, _ in tiles])


def _dw(lhs_t, rhs):
    K, T = lhs_t.shape
    N = rhs.shape[1]
    tk = min(2048, T)
    nt = T // tk
    W = D_MODEL

    def kern(lhs_ref, rhs_ref, o_ref, ob_ref):
        t = pl.program_id(2)

        @pl.when(t == 0)
        def _():
            o_ref[...] = jnp.zeros_like(o_ref)

        o_ref[...] += _nn(lhs_ref[...], rhs_ref[...].astype(BF16))

        @pl.when(t == nt - 1)
        def _():
            ob_ref[...] = o_ref[...].astype(BF16)

    omap = lambda i, j, t: (i, j)
    return pl.pallas_call(
        kern, out_shape=(jax.ShapeDtypeStruct((K, N), F32), jax.ShapeDtypeStruct((K, N), BF16)), grid=(K // W, N // W, nt),
        in_specs=[pl.BlockSpec((W, tk), lambda i, j, t: (i, t)), pl.BlockSpec((tk, W), lambda i, j, t: (t, j))],
        out_specs=(pl.BlockSpec((W, W), omap), pl.BlockSpec((W, W), omap)),
        name="dw", compiler_params=_params("arbitrary", "arbitrary", "arbitrary"))(lhs_t, rhs)


BIG = (("w_in", D_MODEL, IN_COLS, "col"), ("w_attn_out", D_MODEL, D_MODEL, "row"), ("w_conv_out", D_MODEL, D_MODEL, "row"),
       ("w_o", D_MODEL, D_MODEL, "row"), ("w_up", D_MODEL, D_FF, "col"), ("w_down", D_FF, D_MODEL, "row"))


def _shard_dims(rows, cols, kind):
    return (rows, cols // N_CHIP) if kind == "col" else (rows // N_CHIP, cols)


def _window(ref, rows, cols, kind, chip, half):
    sr, sc = _shard_dims(rows, cols, kind)
    hr = sr // 2
    if kind == "col":
        return ref.at[pl.ds(half * hr, hr), pl.ds(chip * sc, sc)]
    return ref.at[pl.ds(chip * sr + half * hr, hr), :]


def _mesh_pos():
    x, y, c = lax.axis_index("x"), lax.axis_index("y"), lax.axis_index("c")
    return x, y, c, 2 * x + y


_REL_BITS = (2, 1, 3)


def _rel_dev(x, y, c, r):
    return ((1 - x, y, c), (x, 1 - y, c), (1 - x, 1 - y, c))[r]


def _for_my_chip(j, fn):
    for js in range(N_CHIP):
        pl.when(j == js)(functools.partial(fn, js))


def _cast_into_full(j_arr, shard, l, rows, cols, kind):
    sr, sc = _shard_dims(rows, cols, kind)
    tr = min(256, sr)

    def kern(j_ref, s_ref, o_ref):
        o_ref[...] = s_ref[...].astype(BF16)

    if kind == "col":
        omap = lambda i, j_ref: (i, j_ref[0])
    else:
        omap = lambda i, j_ref: (j_ref[0] * (sr // tr) + i, 0)
    gs = pltpu.PrefetchScalarGridSpec(
        num_scalar_prefetch=1, grid=(sr // tr,),
        in_specs=[pl.BlockSpec((None, tr, sc), lambda i, j_ref: (l, i, 0))], out_specs=pl.BlockSpec((tr, sc), omap))
    return pl.pallas_call(kern, out_shape=jax.ShapeDtypeStruct((rows, cols), BF16), grid_spec=gs, name="cast_into_full",
                          compiler_params=_params("arbitrary"))(j_arr, shard)


def _gather(fulls, cw=None):
    comm = _gather_comm(fulls, cw)
    n = len(comm.inputs)

    def body(*refs):
        comm.start(refs[:n], refs[n:2 * n], refs[2 * n:])
        comm.finish(refs[:n], refs[n:2 * n], refs[2 * n:])

    return pl.pallas_call(
        body, out_shape=comm.out_shape, in_specs=[ANY] * n, out_specs=[ANY] * n, input_output_aliases=comm.aliases,
        scratch_shapes=comm.scratch, name="gather_weights")(*comm.inputs)


def _gather_comm(fulls, cw=None):
    n_big = len(fulls)
    n_piece = n_big + (0 if cw is None else 1)

    def pieces(in_refs, o_refs, js, c):
        def piece(p, chip, half):
            if p == n_big:
                return o_refs[p].at[half, chip]
            _, rows, cols, kind = fulls[p]
            return _window(o_refs[p], rows, cols, kind, chip, half)

        def mine(p):
            return in_refs[p].at[c] if p == n_big else piece(p, js, c)

        return piece, mine

    def local_copies(in_refs, piece, js, loc_sem):
        if cw is None:
            return []
        return [pltpu.make_async_copy(in_refs[n_big].at[half], piece(n_big, js, half), loc_sem.at[half]) for half in range(2)]

    def ici_copy(piece, mine, js, x, y, c, r, p, send_sem, recv_sem):
        return pltpu.make_async_remote_copy(mine(p), piece(p, js, c), send_sem.at[r * n_piece + p], recv_sem.at[r * n_piece + p],
                                            _rel_dev(x, y, c, r), MESH)

    def start(in_refs, o_refs, sems):
        send_sem, recv_sem, _, _, loc_sem = sems
        x, y, c, j = _mesh_pos()

        def run(js):
            piece, mine = pieces(in_refs, o_refs, js, c)
            for cp in local_copies(in_refs, piece, js, loc_sem):
                cp.start()
            for r in range(3):
                for p in range(n_piece):
                    ici_copy(piece, mine, js, x, y, c, r, p, send_sem, recv_sem).start()

        _for_my_chip(j, run)

    def finish(in_refs, o_refs, sems):
        send_sem, recv_sem, fsend_sem, frecv_sem, loc_sem = sems
        x, y, c, j = _mesh_pos()

        def run(js):
            piece, mine = pieces(in_refs, o_refs, js, c)
            fwds = []
            for r in range(3):
                ks = js ^ _REL_BITS[r]
                for p in range(n_piece):
                    got = piece(p, ks, c)
                    pltpu.make_async_remote_copy(got, got, send_sem.at[r * n_piece + p], recv_sem.at[r * n_piece + p],
                                                 _rel_dev(x, y, c, r), MESH).wait_recv()
                    cp = pltpu.make_async_remote_copy(got, got, fsend_sem.at[r * n_piece + p], frecv_sem.at[r * n_piece + p],
                                                      (x, y, 1 - c), MESH)
                    cp.start()
                    fwds.append(cp)
            for r in range(3):
                ks = js ^ _REL_BITS[r]
                for p in range(n_piece):
                    got = piece(p, ks, 1 - c)
                    pltpu.make_async_remote_copy(got, got, fsend_sem.at[r * n_piece + p], frecv_sem.at[r * n_piece + p],
                                                 (x, y, 1 - c), MESH).wait_recv()
            for r in range(3):
                for p in range(n_piece):
                    ici_copy(piece, mine, js, x, y, c, r, p, send_sem, recv_sem).wait_send()
            for cp in fwds:
                cp.wait_send()
            for cp in local_copies(in_refs, piece, js, loc_sem):
                cp.wait()

        _for_my_chip(j, run)

    out_shape = [jax.ShapeDtypeStruct(a.shape, BF16) for a, _, _, _ in fulls]
    ins = [a for a, _, _, _ in fulls]
    if cw is not None:
        out_shape.append(jax.ShapeDtypeStruct((DEPTH, N_CHIP, 3, D_MODEL // N_CHIP), F32))
        ins.append(cw)
    scratch = [pltpu.SemaphoreType.DMA((3 * n_piece,))] * 4 + [pltpu.SemaphoreType.DMA((2,))]
    return _Hosted(ins, out_shape, {p: p for p in range(n_big)}, scratch, start, finish)


def _sibling_exchange(gb):
    n = len(gb)

    def body(*refs):
        g_refs, o_refs = refs[:n], refs[n:2 * n]
        send_sem, recv_sem = refs[2 * n:]
        x, y, c, _ = _mesh_pos()
        cps = []
        for t in range(n):
            _, rows, cols, kind = gb[t]
            for chip in range(N_CHIP):
                out_w = _window(g_refs[t], rows, cols, kind, chip, 1 - c)
                dst_w = _window(o_refs[t], rows, cols, kind, chip, 1 - c)
                cp = pltpu.make_async_remote_copy(out_w, dst_w, send_sem.at[N_CHIP * t + chip], recv_sem.at[N_CHIP * t + chip],
                                                  (x, y, 1 - c), MESH)
                cp.start()
                cps.append(cp)
        for t in range(n):
            _, rows, cols, kind = gb[t]
            for chip in range(N_CHIP):
                w = _window(o_refs[t], rows, cols, kind, chip, c)
                pltpu.make_async_remote_copy(w, w, send_sem.at[N_CHIP * t + chip], recv_sem.at[N_CHIP * t + chip],
                                             (x, y, 1 - c), MESH).wait_recv()
        for cp in cps:
            cp.wait_send()

    return pl.pallas_call(
        body, out_shape=[jax.ShapeDtypeStruct(a.shape, BF16) for a, _, _, _ in gb], in_specs=[ANY] * n, out_specs=[ANY] * n,
        scratch_shapes=[pltpu.SemaphoreType.DMA((N_CHIP * n,))] * 2, name="grad_sibling_exchange")(*[a for a, _, _, _ in gb])


def _half_add(c_arr, g, sib, rows, cols, kind):
    sr, sc = _shard_dims(rows, cols, kind)
    hr = sr // 2

    def kern(c_ref, g_ref, s_ref, ob_ref, of_ref):
        v = g_ref[...] + s_ref[...].astype(F32)
        of_ref[...] = v
        ob_ref[...] = v.astype(BF16)

    if kind == "col":
        imap = lambda j, c_ref: (c_ref[0], j)
    else:
        imap = lambda j, c_ref: (2 * j + c_ref[0], 0)
    gs = pltpu.PrefetchScalarGridSpec(
        num_scalar_prefetch=1, grid=(N_CHIP,),
        in_specs=[pl.BlockSpec((hr, sc), imap), pl.BlockSpec((hr, sc), imap)],
        out_specs=[pl.BlockSpec((None, hr, sc), lambda j, c_ref: (j, 0, 0))] * 2)
    return pl.pallas_call(
        kern, out_shape=(jax.ShapeDtypeStruct((N_CHIP, hr, sc), BF16), jax.ShapeDtypeStruct((N_CHIP, hr, sc), F32)),
        grid_spec=gs, name="grad_half_add", compiler_params=_params("arbitrary"))(c_arr, g, sib)


def _chip_exchange(sbs):
    comm = _chip_exchange_comm(sbs)
    n = len(sbs)

    def body(*refs):
        comm.start(refs[:n], refs[n:2 * n], refs[2 * n:])
        comm.finish(refs[:n], refs[n:2 * n], refs[2 * n:])

    return pl.pallas_call(
        body, out_shape=comm.out_shape, in_specs=[ANY] * n, out_specs=[ANY] * n, scratch_shapes=comm.scratch,
        name="grad_chip_exchange")(*sbs)


def _chip_exchange_comm(sbs):
    n = len(sbs)

    def copies(s_refs, o_refs, sems):
        send_sem, recv_sem = sems
        x, y, c, j = _mesh_pos()
        return [pltpu.make_async_remote_copy(s_refs[t].at[j ^ _REL_BITS[r]], o_refs[t].at[r], send_sem.at[n * r + t],
                                             recv_sem.at[n * r + t], _rel_dev(x, y, c, r), MESH)
                for r in range(3) for t in range(n)]

    def start(s_refs, o_refs, sems):
        for cp in copies(s_refs, o_refs, sems):
            cp.start()

    def finish(s_refs, o_refs, sems):
        for cp in copies(s_refs, o_refs, sems):
            cp.wait()

    return _Hosted(sbs, [jax.ShapeDtypeStruct((3,) + a.shape[1:], BF16) for a in sbs], {},
                   [pltpu.SemaphoreType.DMA((3 * n,))] * 2, start, finish)


def _owner_sum(jc_arr, sf, rb, l, into=None):
    _, hr, sc = sf.shape

    def kern(jc_ref, s_ref, r0_ref, r1_ref, r2_ref, *rest):
        o_ref = rest[-1]
        o_ref[...] = ((s_ref[...] + r0_ref[...].astype(F32)) + r1_ref[...].astype(F32)) + r2_ref[...].astype(F32)

    in_specs = [pl.BlockSpec((None, hr, sc), lambda i, jc_ref: (jc_ref[0], 0, 0))]
    in_specs += [pl.BlockSpec((None, hr, sc), lambda i, jc_ref, r=r: (r, 0, 0)) for r in range(3)]
    args = [jc_arr, sf, rb, rb, rb]
    aliases = {}
    if into is not None:
        in_specs.append(ANY)
        args.append(into)
        aliases = {len(args) - 1: 0}
    gs = pltpu.PrefetchScalarGridSpec(
        num_scalar_prefetch=1, grid=(1,), in_specs=in_specs,
        out_specs=pl.BlockSpec((None, hr, sc), lambda i, jc_ref: (l, jc_ref[1], 0)))
    return pl.pallas_call(kern, out_shape=jax.ShapeDtypeStruct((DEPTH, 2 * hr, sc), F32), grid_spec=gs,
                          input_output_aliases=aliases, name="grad_owner_sum", compiler_params=_params("arbitrary"))(*args)


def _sibling_assemble(grads):
    n = len(grads)

    def body(*refs):
        o_refs = refs[n:2 * n]
        send_sem, recv_sem = refs[2 * n:]
        x, y, c, _ = _mesh_pos()
        cps = []
        for q in range(n):
            hr = grads[q].shape[1] // 2
            for l in range(DEPTH):
                mine = o_refs[q].at[l, pl.ds(c * hr, hr), :]
                cp = pltpu.make_async_remote_copy(mine, mine, send_sem.at[DEPTH * q + l], recv_sem.at[DEPTH * q + l],
                                                  (x, y, 1 - c), MESH)
                cp.start()
                cps.append(cp)
        for q in range(n):
            hr = grads[q].shape[1] // 2
            for l in range(DEPTH):
                theirs = o_refs[q].at[l, pl.ds((1 - c) * hr, hr), :]
                pltpu.make_async_remote_copy(theirs, theirs, send_sem.at[DEPTH * q + l], recv_sem.at[DEPTH * q + l],
                                             (x, y, 1 - c), MESH).wait_recv()
        for cp in cps:
            cp.wait_send()

    return pl.pallas_call(
        body, out_shape=[jax.ShapeDtypeStruct(g.shape, F32) for g in grads], in_specs=[ANY] * n, out_specs=[ANY] * n,
        input_output_aliases={q: q for q in range(n)},
        scratch_shapes=[pltpu.SemaphoreType.DMA((DEPTH * n,))] * 2, name="grad_sibling_assemble")(*grads)


def _adamw_math(w, g, m, v):
    m = ADAM_B1 * m + (1.0 - ADAM_B1) * g
    v = ADAM_B2 * v + (1.0 - ADAM_B2) * jnp.square(g)
    m_hat = m / (1.0 - ADAM_B1 ** ADAM_STEP)
    v_hat = v / (1.0 - ADAM_B2 ** ADAM_STEP)
    delta = -ADAM_LR * (m_hat / (jnp.sqrt(v_hat) + ADAM_EPS) + ADAM_WD * w)
    return delta, m, v


def _adamw(w, g, m, v):
    shape = w.shape
    C = shape[-1]
    R = int(np.prod(shape[:-1]))
    tr = min(256, R)
    args = [a.reshape(R, C) for a in (w, g, m, v)]

    def kern(w_ref, g_ref, m_ref, v_ref, d_ref, nm_ref, nv_ref):
        d, nm, nv = _adamw_math(w_ref[...], g_ref[...], m_ref[...], v_ref[...])
        d_ref[...] = d
        nm_ref[...] = nm
        nv_ref[...] = nv

    spec = pl.BlockSpec((tr, C), lambda i: (i, 0))
    outs = pl.pallas_call(
        kern, out_shape=[jax.ShapeDtypeStruct((R, C), F32)] * 3, grid=(R // tr,), in_specs=[spec] * 4, out_specs=[spec] * 3,
        name="adamw", compiler_params=_params("parallel"))(*args)
    return [o.reshape(shape) for o in outs]


_ROW_G_MIX, _ROW_B_GATES, _ROW_SINKS, _ROW_CONV, _ROW_G_MLP, _ROW_G_FINAL, _ROW_LOSS = 0, 2, 6, 8, 16, 18, 19


def _small_step(parts, params, moms, vels):
    names = ["g_mix", "b_gates", "sinks", "conv_w", "conv_b", "g_mlp", "g_final"]
    D = D_MODEL
    QW = D // N_CHIP
    n_dev = 8

    def body(*refs):
        it = iter(refs)
        dgmix = [next(it) for _ in range(DEPTH)]
        dbg = [next(it) for _ in range(DEPTH)]
        dsk = [next(it) for _ in range(DEPTH)]
        dwb = [next(it) for _ in range(DEPTH)]
        dgmlp = [next(it) for _ in range(DEPTH)]
        lst = next(it)
        p_refs = {n: next(it) for n in names}
        m_refs = {n: next(it) for n in names}
        v_refs = {n: next(it) for n in names}
        loss_ref = next(it)
        outs = {n: [next(it) for _ in range(4)] for n in names}
        pack_ref, all_ref, send_sem, recv_sem = next(it), next(it), next(it), next(it)

        x, y, c, j = _mesh_pos()
        me = 4 * x + 2 * y + c
        pack_ref[...] = jnp.zeros_like(pack_ref)
        for l in range(DEPTH):
            pack_ref[_ROW_G_MIX + l:_ROW_G_MIX + l + 1, :] = dgmix[l][...]
            pack_ref[_ROW_B_GATES + 2 * l:_ROW_B_GATES + 2 * l + 1, :] = dbg[l][:, 0:D]
            pack_ref[_ROW_B_GATES + 2 * l + 1:_ROW_B_GATES + 2 * l + 2, :] = dbg[l][:, D:2 * D]
            pack_ref[_ROW_SINKS + l:_ROW_SINKS + l + 1, 0:128] = dsk[l][0:1, :]
            pack_ref[_ROW_CONV + 4 * l:_ROW_CONV + 4 * l + 4, :] = dwb[l][0:4, :]
            pack_ref[_ROW_G_MLP + l:_ROW_G_MLP + l + 1, :] = dgmlp[l][...]
        pack_ref[_ROW_G_FINAL:_ROW_G_FINAL + 1, :] = lst[0:1, :]
        pack_ref[_ROW_LOSS:_ROW_LOSS + 1, :] = lst[1:2, :]

        all_ref[me] = pack_ref[...]
        cps = []
        for k in range(1, n_dev):
            dx_, dy_, dc_ = (k >> 2) & 1, (k >> 1) & 1, k & 1
            peer = (x ^ dx_, y ^ dy_, c ^ dc_)
            cp = pltpu.make_async_remote_copy(pack_ref, all_ref.at[me], send_sem.at[k - 1], recv_sem.at[k - 1], peer, MESH)
            cp.start()
            cps.append(cp)
        for cp in cps:
            cp.wait()

        tot = all_ref[0]
        for d in range(1, n_dev):
            tot = tot + all_ref[d]
        pack_ref[...] = tot

        loss_ref[...] = pack_ref[_ROW_LOSS:_ROW_LOSS + 1, 0:1]

        def finish(name, idx, g):
            w, m, v = p_refs[name][idx], m_refs[name][idx], v_refs[name][idx]
            d, nm, nv = _adamw_math(w, g, m, v)
            for ref, val in zip(outs[name], (g, d, nm, nv)):
                ref[idx] = val

        for l in range(DEPTH):
            finish("g_mix", (slice(l, l + 1), slice(None)), pack_ref[_ROW_G_MIX + l:_ROW_G_MIX + l + 1, :])
            finish("g_mlp", (slice(l, l + 1), slice(None)), pack_ref[_ROW_G_MLP + l:_ROW_G_MLP + l + 1, :])
            finish("conv_b", (slice(l, l + 1), slice(None)), pack_ref[_ROW_CONV + 4 * l + 3:_ROW_CONV + 4 * l + 4, :])
            finish("sinks", (slice(l, l + 1), slice(None)), pack_ref[_ROW_SINKS + l:_ROW_SINKS + l + 1, 0:N_Q_HEADS])
            for hf in range(2):
                finish("b_gates", (slice(l, l + 1), slice(hf * D, (hf + 1) * D)),
                       pack_ref[_ROW_B_GATES + 2 * l + hf:_ROW_B_GATES + 2 * l + hf + 1, :])
        finish("g_final", (slice(0, 1), slice(None)), pack_ref[_ROW_G_FINAL:_ROW_G_FINAL + 1, :])

        def conv_w_chip(js):
            for l in range(DEPTH):
                for k in range(3):
                    row = _ROW_CONV + 4 * l + k
                    finish("conv_w", (l, slice(k, k + 1), slice(None)), pack_ref[row:row + 1, js * QW:(js + 1) * QW])

        _for_my_chip(j, conv_w_chip)

    vm = pl.BlockSpec(memory_space=pltpu.VMEM)
    ins = (parts["g_mix"] + parts["b_gates"] + parts["sinks"] + parts["conv"] + parts["g_mlp"] + [parts["loss"]]
           + [params[n] for n in names] + [moms[n] for n in names] + [vels[n] for n in names])
    out_shape = [jax.ShapeDtypeStruct((1, 1), F32)]
    for n in names:
        out_shape += [jax.ShapeDtypeStruct(params[n].shape, F32)] * 4
    res = pl.pallas_call(
        body, out_shape=out_shape, in_specs=[vm] * len(ins), out_specs=[vm] * len(out_shape),
        scratch_shapes=[pltpu.VMEM((SMALL_ROWS, D), F32), pltpu.VMEM((n_dev, SMALL_ROWS, D), F32),
                        pltpu.SemaphoreType.DMA((n_dev - 1,)), pltpu.SemaphoreType.DMA((n_dev - 1,))],
        name="small_allreduce_adamw")(*ins)
    loss = res[0]
    out = {n: res[1 + 4 * i:5 + 4 * i] for i, n in enumerate(names)}
    return loss, out


def kernel(x, g_mix, w_in, b_gates, sinks, w_attn_out, conv_w, conv_b, w_conv_out, w_o, g_mlp, w_up, w_down, g_final, loss_target, m_g_mix, m_w_in, m_b_gates, m_sinks, m_w_attn_out, m_conv_w, m_conv_b, m_w_conv_out, m_w_o, m_g_mlp, m_w_up, m_w_down, m_g_final, v_g_mix, v_w_in, v_b_gates, v_sinks, v_w_attn_out, v_conv_w, v_conv_b, v_w_conv_out, v_w_o, v_g_mlp, v_w_up, v_w_down, v_g_final):
    B, S, D = x.shape
    T = B * S
    big_w = dict(w_in=w_in, w_attn_out=w_attn_out, w_conv_out=w_conv_out, w_o=w_o, w_up=w_up, w_down=w_down)
    big_m = dict(w_in=m_w_in, w_attn_out=m_w_attn_out, w_conv_out=m_w_conv_out, w_o=m_w_o, w_up=m_w_up, w_down=m_w_down)
    big_v = dict(w_in=v_w_in, w_attn_out=v_w_attn_out, w_conv_out=v_w_conv_out, w_o=v_w_o, w_up=v_w_up, w_down=v_w_down)

    c_arr = lax.axis_index("c").astype(jnp.int32).reshape(1)
    j_arr = (2 * lax.axis_index("x") + lax.axis_index("y")).astype(jnp.int32).reshape(1)
    jc_arr = jnp.concatenate([j_arr, c_arr])
    order = [(n, l) for n, _, _, _ in BIG for l in range(DEPTH)]
    dims = {n: (r, c_, k) for n, r, c_, k in BIG}

    full = {(n, l): _cast_into_full(j_arr, big_w[n], l, *dims[n]) for n, l in order}
    mixers = ("w_attn_out", "w_conv_out", "w_o")

    def gather_of(keys):
        return _gather_comm([(full[k],) + dims[k[0]] for k in keys])

    carried = {("proj", 0): [(n, 0) for n in mixers], ("attn", 0): [("w_up", 0), ("w_down", 0)],
               ("mlp", 0): [("w_in", 1)] + [(n, 1) for n in mixers], ("attn", 1): [("w_up", 1), ("w_down", 1)]}

    def carry(fn, where, *args):
        keys = carried.get(where)
        if keys is None:
            return fn(*args)
        res, got = fn(*args, comm=gather_of(keys))
        full.update(zip(keys, got))
        return res

    first = _gather([(full["w_in", 0],) + dims["w_in"]], conv_w)
    full["w_in", 0] = first[0]
    conv_w_full = jnp.transpose(first[1], (0, 2, 1, 3)).reshape(DEPTH, 3, D)

    xs = [x.reshape(T, D)]
    saved = []
    for l in range(DEPTH):
        ht, pqkv, pconv, pgate = carry(_norm_proj, ("proj", l), xs[-1], g_mix[l:l + 1], full["w_in", l])
        att, att_t, lse = carry(_attn_fwd, ("attn", l), pqkv, sinks[l], S)
        cv, cv_t = _conv_fwd(pconv, conv_w_full[l], conv_b[l:l + 1], S)
        x1, ya, yc, mg_t = _mix_fwd(xs[-1], att, cv, pgate, b_gates[l:l + 1], full["w_attn_out", l], full["w_conv_out", l],
                                    full["w_o", l])
        x2, a = carry(_mlp_fwd, ("mlp", l), x1, g_mlp[l:l + 1], full["w_up", l], full["w_down", l])
        saved.append(dict(ht=ht, pqkv=pqkv, pconv=pconv, pgate=pgate, att=att, att_t=att_t, lse=lse, cv_t=cv_t, x1=x1, ya=ya,
                          yc=yc, mg_t=mg_t, a=a))
        xs.append(x2)

    loss_stats, dx = _loss_bwd(xs[-1], g_final.reshape(1, D), loss_target.reshape(T, D))

    parts = dict(g_mix=[None] * DEPTH, b_gates=[None] * DEPTH, sinks=[None] * DEPTH, conv=[None] * DEPTH,
                 g_mlp=[None] * DEPTH, loss=loss_stats)
    gf, gb = {}, {}
    pre, got = {}, {}

    def pre_reduce(keys):
        sib = _sibling_exchange([(gb[k],) + dims[k[0]] for k in keys])
        for k, s in zip(keys, sib):
            pre[k] = _half_add(c_arr, gf[k], s, *dims[k[0]])
        return keys

    def exchange_in(fn, keys, *args):
        res, arrived = fn(*args, comm=_chip_exchange_comm([pre[k][0] for k in keys]))
        got.update(zip(keys, arrived))
        return res

    pending = None
    for l in reversed(range(DEPTH)):
        W = {n: full[(n, l)] for n in big_w}
        sv = saved[l]
        mlp_args = (dx, sv["x1"], sv["a"], g_mlp[l:l + 1], W["w_up"], W["w_down"])
        dx1, da, u_t, h2_t, parts["g_mlp"][l] = exchange_in(_mlp_bwd, pending, *mlp_args) if pending else _mlp_bwd(*mlp_args)
        gf["w_up", l], gb["w_up", l] = _dw(h2_t, da)
        gf["w_down", l], gb["w_down", l] = _dw(u_t, dx)
        datt, dcv, dya, dyc, dgate, parts["b_gates"][l] = _mix_bwd(
            dx1, sv["ya"], sv["yc"], sv["pgate"], b_gates[l:l + 1], W["w_attn_out"], W["w_conv_out"], W["w_o"])
        gf["w_o", l], gb["w_o", l] = _dw(sv["mg_t"], dx1)
        gf["w_attn_out", l], gb["w_attn_out", l] = _dw(sv["att_t"], dya)
        gf["w_conv_out", l], gb["w_conv_out", l] = _dw(sv["cv_t"], dyc)
        dconv, parts["conv"][l] = _conv_bwd(dcv, sv["pconv"], conv_w_full[l], conv_b[l:l + 1], S)
        attn_args = (sv["pqkv"], sv["att"], datt, sv["lse"], sinks[l], S)
        last = l == 0
        if last:
            keys = pre_reduce([(n, l) for n in mixers + ("w_up", "w_down")])
            dq, dkv, parts["sinks"][l] = exchange_in(_attn_bwd, keys, *attn_args)
        else:
            dq, dkv, parts["sinks"][l] = _attn_bwd(*attn_args)
        pieces = [(dq, D), (dkv, QKV_W - D), (dconv, CONV_W), (dgate, GATE_W)]
        gf["w_in", l], gb["w_in", l] = _dw_pieces(sv["ht"], pieces)
        in_args = (pieces, W["w_in"], xs[l], dx1, g_mix[l:l + 1])
        if last:
            dx, parts["g_mix"][l] = exchange_in(_inproj_bwd, pre_reduce([("w_in", l)]), *in_args)
        else:
            dx, parts["g_mix"][l] = _inproj_bwd(*in_args)
            pending = pre_reduce([(n, l) for n, _, _, _ in BIG])

    mine = {}
    for n, l in order:
        mine[n] = _owner_sum(jc_arr, pre[n, l][1], got[n, l], l, mine.get(n))
    grads = dict(zip(mine, _sibling_assemble(list(mine.values()))))

    res = {}
    for n in big_w:
        d, nm, nv = _adamw(big_w[n], grads[n], big_m[n], big_v[n])
        res[n] = (grads[n], d, nm, nv)

    small_p = dict(g_mix=g_mix, b_gates=b_gates, sinks=sinks, conv_w=conv_w, conv_b=conv_b, g_mlp=g_mlp, g_final=g_final.reshape(1, D))
    small_m = dict(g_mix=m_g_mix, b_gates=m_b_gates, sinks=m_sinks, conv_w=m_conv_w, conv_b=m_conv_b, g_mlp=m_g_mlp,
                   g_final=m_g_final.reshape(1, D))
    small_v = dict(g_mix=v_g_mix, b_gates=v_b_gates, sinks=v_sinks, conv_w=v_conv_w, conv_b=v_conv_b, g_mlp=v_g_mlp,
                   g_final=v_g_final.reshape(1, D))
    loss, small = _small_step(parts, small_p, small_m, small_v)
    for n, vals in small.items():
        res[n] = tuple(v.reshape(D) for v in vals) if n == "g_final" else tuple(vals)

    weights = ["g_mix", "w_in", "b_gates", "sinks", "w_attn_out", "conv_w", "conv_b", "w_conv_out", "w_o", "g_mlp", "w_up",
               "w_down", "g_final"]
    out = [loss.reshape(()), dx.reshape(B, S, D)]
    for k in range(4):
        out += [res[n][k] for n in weights]
    return tuple(out)
```

```python
import functools

import numpy as np
import jax
import jax.numpy as jnp
from jax import lax
from jax.experimental import pallas as pl
from jax.experimental.pallas import tpu as pltpu

F32 = jnp.float32
BF16 = jnp.bfloat16

D_MODEL = 1024
HEAD_DIM = 64
N_Q_HEADS = 16
N_KV_HEADS = 4
GQA_GROUP = 4
WINDOW = 128
D_FF = 4096
DEPTH = 2
RMS_EPS = 1e-6
NEG_INF = -1e30
ATTN_SCALE = HEAD_DIM ** -0.5
QKV_W = 1536
CONV_W = 3072
GATE_W = 2048
IN_COLS = QKV_W + CONV_W + GATE_W
COL_TILE = 512
N_CHIP = 4
ADAM_LR = 0.001
ADAM_B1 = 0.9
ADAM_B2 = 0.999
ADAM_EPS = 1e-08
ADAM_WD = 0.01
ADAM_STEP = 10
V7X_VMEM_BYTES = 64 * 2 ** 20
VMEM_LIMIT = V7X_VMEM_BYTES - 8 * 2 ** 20
MESH = pl.DeviceIdType.MESH
ANY = pl.BlockSpec(memory_space=pl.ANY)
SMALL_ROWS = 24

_SLOPES = [float(v) for v in np.power(np.float32(2.0), -8.0 * np.arange(1, N_Q_HEADS + 1, dtype=np.float32) / N_Q_HEADS)]


def _params(*sem):
    return pltpu.CompilerParams(dimension_semantics=sem, vmem_limit_bytes=VMEM_LIMIT)


class _Hosted:
    def __init__(self, inputs, out_shape, aliases, scratch, start, finish):
        self.inputs, self.out_shape, self.aliases, self.scratch = list(inputs), list(out_shape), dict(aliases), list(scratch)
        self.start, self.finish = start, finish


def _hosted_call(comm, kern, *, out_shape, grid, in_specs, out_specs, args, name, sem, scratch_shapes=()):
    single = not isinstance(out_shape, (tuple, list))
    outs = [out_shape] if single else list(out_shape)
    ospecs = [out_specs] if single else list(out_specs)
    if comm is None:
        res = pl.pallas_call(kern, out_shape=outs, grid=grid, in_specs=list(in_specs), out_specs=ospecs,
                             scratch_shapes=list(scratch_shapes), name=name, compiler_params=_params(*sem))(*args)
        return res[0] if single else res
    n_in, n_out, n_scr = len(args), len(outs), len(scratch_shapes)
    ci, co, cs = len(comm.inputs), len(comm.out_shape), len(comm.scratch)

    def body(*refs):
        cuts = np.cumsum([0, n_in, ci, n_out, co, n_scr, cs])
        a, b, c, d, e, f = [refs[lo:hi] for lo, hi in zip(cuts[:-1], cuts[1:])]
        ids = [pl.program_id(k) for k in range(len(grid))]
        first = functools.reduce(jnp.logical_and, [i == 0 for i in ids])
        last = functools.reduce(jnp.logical_and, [i == n - 1 for i, n in zip(ids, grid)])
        pl.when(first)(lambda: comm.start(b, d, f))
        kern(*a, *c, *e)
        pl.when(last)(lambda: comm.finish(b, d, f))

    res = pl.pallas_call(
        body, out_shape=outs + comm.out_shape, grid=grid, in_specs=list(in_specs) + [ANY] * ci, out_specs=ospecs + [ANY] * co,
        scratch_shapes=list(scratch_shapes) + comm.scratch,
        input_output_aliases={n_in + i: n_out + o for i, o in comm.aliases.items()},
        name=name + "_carrier", compiler_params=_params(*(["arbitrary"] * len(grid))))(*args, *comm.inputs)
    main = res[:n_out]
    return (main[0] if single else main), res[n_out:]


def _nt(a, b):
    return lax.dot_general(a, b, (((1,), (1,)), ((), ())), preferred_element_type=F32)


def _tn(a, b):
    return lax.dot_general(a, b, (((0,), (0,)), ((), ())), preferred_element_type=F32)


def _nn(a, b):
    return jnp.dot(a, b, preferred_element_type=F32)


def _rms_stats(xf):
    r = lax.rsqrt(jnp.mean(xf * xf, axis=-1, keepdims=True) + RMS_EPS)
    return r, xf * r


def _rms_bwd(dh, xh, r, g):
    dxh = dh * g
    dx = r * (dxh - xh * jnp.mean(dxh * xh, axis=-1, keepdims=True))
    dg = jnp.sum(dh * xh, axis=0, keepdims=True)
    return dx, dg


def _resident(shape):
    return pl.BlockSpec(shape, lambda *_: (0,) * len(shape), pipeline_mode=pl.Buffered(1))


def _norm_proj(x, g, w, comm=None):
    T, D = x.shape
    tm = min(256, T)
    widths = (QKV_W, CONV_W, GATE_W)

    def kern(x_ref, g_ref, w_ref, ht_ref, *o_refs):
        _, xh = _rms_stats(x_ref[...])
        h = (xh * g_ref[...]).astype(BF16)
        ht_ref[...] = h.T
        off = 0
        for o_ref, wd in zip(o_refs, widths):
            o_ref[...] = _nn(h, w_ref[:, off:off + wd]).astype(BF16)
            off += wd

    row = lambda wd: pl.BlockSpec((tm, wd), lambda i: (i, 0))
    return _hosted_call(
        comm, kern,
        out_shape=[jax.ShapeDtypeStruct((D, T), BF16)] + [jax.ShapeDtypeStruct((T, wd), BF16) for wd in widths],
        grid=(T // tm,), in_specs=[row(D), pl.BlockSpec((1, D), lambda i: (0, 0)), _resident((D, IN_COLS))],
        out_specs=[pl.BlockSpec((D, tm), lambda i: (0, i))] + [row(wd) for wd in widths],
        name="norm_proj", sem=("parallel",), args=(x, g, w))


GW = GQA_GROUP * WINDOW
BAND = 2 * WINDOW
KV_W = N_KV_HEADS * HEAD_DIM


def _attn_bias_table():
    jj = np.arange(BAND)[:, None]
    col = np.arange(GW)[None, :]
    dist = WINDOW + (col % WINDOW) - jj
    valid = (dist >= 0) & (dist < WINDOW)
    slopes = np.asarray(_SLOPES, np.float32).reshape(N_KV_HEADS, GQA_GROUP)
    tab = np.empty((2, N_KV_HEADS, BAND, GW), np.float32)
    for hk in range(N_KV_HEADS):
        bias = -slopes[hk][col // WINDOW] * dist.astype(np.float32)
        tab[0, hk] = np.where(valid, bias, np.float32(NEG_INF))
        tab[1, hk] = np.where(valid & (jj >= WINDOW), bias, np.float32(NEG_INF))
    return jnp.asarray(tab)


def _stack_heads(ref, hk):
    return jnp.concatenate(
        [ref[:, HEAD_DIM * (GQA_GROUP * hk + g): HEAD_DIM * (GQA_GROUP * hk + g + 1)] for g in range(GQA_GROUP)], axis=0)


def _kv_band(cur_ref, prev_ref, hk):
    k0 = N_Q_HEADS * HEAD_DIM
    sl = slice(HEAD_DIM * hk, HEAD_DIM * (hk + 1))
    ksl, vsl = slice(k0 + sl.start, k0 + sl.stop), slice(k0 + KV_W + sl.start, k0 + KV_W + sl.stop)
    k_band = jnp.concatenate([prev_ref[:, sl], cur_ref[:, ksl]], axis=0)
    v_band = jnp.concatenate([prev_ref[:, KV_W + sl.start:KV_W + sl.stop], cur_ref[:, vsl]], axis=0)
    return k_band, v_band


def _lane_row(vals):
    return jnp.concatenate([jnp.full((1, WINDOW), v, F32) for v in vals], axis=1)


def _attn_fwd(pqkv, sinks, bias, seq, comm=None):
    T = pqkv.shape[0]
    nblk = seq // WINDOW

    def kern(sink_ref, cur_ref, prev_ref, bias_ref, o_ref, ot_ref, lse_ref):
        i = pl.program_id(0)
        first_i = ((i % nblk) == 0).astype(jnp.int32)
        for hk in range(N_KV_HEADS):
            heads = [GQA_GROUP * hk + g for g in range(GQA_GROUP)]
            k_band, v_band = _kv_band(cur_ref, prev_ref, hk)
            q_g = _stack_heads(cur_ref, hk) * ATTN_SCALE
            sink = _lane_row([sink_ref[h] for h in heads])
            st = _nt(k_band, q_g) + bias_ref[first_i, hk]
            m = jnp.maximum(jnp.max(st, axis=0, keepdims=True), sink)
            p = jnp.exp(st - m)
            den = jnp.sum(p, axis=0, keepdims=True) + jnp.exp(sink - m)
            ot = _tn(v_band, p.astype(BF16)) * (1.0 / den)
            lse = m + jnp.log(den)
            for g, h in enumerate(heads):
                ot_ref[HEAD_DIM * h:HEAD_DIM * (h + 1), :] = ot[:, WINDOW * g:WINDOW * (g + 1)].astype(BF16)
                lse_ref[h:h + 1, :] = lse[:, WINDOW * g:WINDOW * (g + 1)]
        o_ref[...] = ot_ref[...].T

    return _hosted_call(
        comm, kern,
        out_shape=(jax.ShapeDtypeStruct((T, D_MODEL), BF16), jax.ShapeDtypeStruct((D_MODEL, T), BF16),
                   jax.ShapeDtypeStruct((N_Q_HEADS, T), F32)),
        grid=(T // WINDOW,),
        in_specs=[pl.BlockSpec(memory_space=pltpu.SMEM),
                  pl.BlockSpec((WINDOW, QKV_W), lambda i: (i, 0)),
                  pl.BlockSpec((WINDOW, 2 * KV_W), lambda i: (jnp.maximum(i - 1, 0), 2)),
                  _resident(bias.shape)],
        out_specs=(pl.BlockSpec((WINDOW, D_MODEL), lambda i: (i, 0)), pl.BlockSpec((D_MODEL, WINDOW), lambda i: (0, i)),
                   pl.BlockSpec((N_Q_HEADS, WINDOW), lambda i: (0, i))),
        name="attn_fwd", sem=("parallel",), args=(sinks, pqkv, pqkv, bias))


def _pick_row(a, row):
    rid = lax.broadcasted_iota(jnp.int32, a.shape, 0)
    return jnp.sum(jnp.where(rid == row, a, 0.0), axis=0, keepdims=True)


def _conv_taps(yc, halo_yc, first_i):
    keep = (1 - first_i).astype(F32)
    p1 = _pick_row(halo_yc, 15) * keep
    p2 = _pick_row(halo_yc, 14) * keep
    rowid = lax.broadcasted_iota(jnp.int32, yc.shape, 0)
    s1 = jnp.where(rowid == 0, p1, pltpu.roll(yc, 1, 0))
    s2 = jnp.where(rowid == 0, p2, jnp.where(rowid == 1, p1, pltpu.roll(yc, 2, 0)))
    return s1, s2


def _conv_fwd(pconv, conv_w, conv_b, seq):
    T = pconv.shape[0]
    tm = min(256, seq)
    per_seq = seq // tm
    D = D_MODEL

    def kern(cur_ref, halo_ref, w_ref, b_ref, o_ref, ot_ref):
        i = pl.program_id(0)
        first_i = ((i % per_seq) == 0).astype(jnp.int32)
        cb = cur_ref[:, 0:D].astype(F32)
        yc = cur_ref[:, D:2 * D].astype(F32) * cur_ref[:, 2 * D:3 * D].astype(F32)
        halo_yc = halo_ref[:, D:2 * D].astype(F32) * halo_ref[:, 2 * D:3 * D].astype(F32)
        s1, s2 = _conv_taps(yc, halo_yc, first_i)
        z = w_ref[0:1, :] * s2 + w_ref[1:2, :] * s1 + w_ref[2:3, :] * yc
        cv = (cb * (z + b_ref[...])).astype(BF16)
        o_ref[...] = cv
        ot_ref[...] = cv.T

    return pl.pallas_call(
        kern, out_shape=(jax.ShapeDtypeStruct((T, D), BF16), jax.ShapeDtypeStruct((D, T), BF16)), grid=(T // tm,),
        in_specs=[pl.BlockSpec((tm, CONV_W), lambda i: (i, 0)),
                  pl.BlockSpec((16, CONV_W), lambda i: (jnp.maximum(i * (tm // 16) - 1, 0), 0)),
                  pl.BlockSpec((3, D), lambda i: (0, 0)), pl.BlockSpec((1, D), lambda i: (0, 0))],
        out_specs=(pl.BlockSpec((tm, D), lambda i: (i, 0)), pl.BlockSpec((D, tm), lambda i: (0, i))),
        name="conv_fwd", compiler_params=_params("parallel"))(pconv, pconv, conv_w, conv_b)


def _mix_fwd(x, att, cv, pgate, b_gates, wao, wco, wo):
    T, D = x.shape
    tm = min(512, T)

    def kern(x_ref, att_ref, cv_ref, pg_ref, bg_ref, wao_ref, wco_ref, wo_ref, x1_ref, ya_ref, yc_ref, mgt_ref):
        ya = _nn(att_ref[...], wao_ref[...])
        yc = _nn(cv_ref[...], wco_ref[...])
        sa = jax.nn.sigmoid(pg_ref[:, 0:D].astype(F32) + bg_ref[:, 0:D])
        sc = jax.nn.sigmoid(pg_ref[:, D:2 * D].astype(F32) + bg_ref[:, D:2 * D])
        mg = (sa * ya + sc * yc).astype(BF16)
        ya_ref[...] = ya.astype(BF16)
        yc_ref[...] = yc.astype(BF16)
        mgt_ref[...] = mg.T
        x1_ref[...] = x_ref[...] + _nn(mg, wo_ref[...])

    row = lambda w: pl.BlockSpec((tm, w), lambda i: (i, 0))
    full = lambda a, b: pl.BlockSpec((a, b), lambda i: (0, 0))
    bf = jax.ShapeDtypeStruct((T, D), BF16)
    return pl.pallas_call(
        kern, out_shape=(jax.ShapeDtypeStruct((T, D), F32), bf, bf, jax.ShapeDtypeStruct((D, T), BF16)), grid=(T // tm,),
        in_specs=[row(D), row(D), row(D), row(GATE_W), full(1, GATE_W)] + [_resident((D, D))] * 3,
        out_specs=(row(D), row(D), row(D), pl.BlockSpec((D, tm), lambda i: (0, i))),
        name="mix_fwd", compiler_params=_params("parallel"))(x, att, cv, pgate, b_gates, wao, wco, wo)


def _mlp_fwd(x1, g, wup, wdn, comm=None):
    T, D = x1.shape
    tm = min(1024, T)
    nj = D_FF // D

    def kern(x_ref, g_ref, wup_ref, wdn_ref, x2_ref, a_ref, h_scr, acc_scr):
        j = pl.program_id(1)

        @pl.when(j == 0)
        def _():
            xf = x_ref[...]
            _, xh = _rms_stats(xf)
            h_scr[...] = (xh * g_ref[...]).astype(BF16)
            acc_scr[...] = xf

        a = _nn(h_scr[...], wup_ref[...])
        a_ref[...] = a.astype(BF16)
        u = jnp.square(jnp.maximum(a, 0.0)).astype(BF16)
        acc_scr[...] += _nn(u, wdn_ref[...])

        @pl.when(j == nj - 1)
        def _():
            x2_ref[...] = acc_scr[...]

    return _hosted_call(
        comm, kern, out_shape=(jax.ShapeDtypeStruct((T, D), F32), jax.ShapeDtypeStruct((T, D_FF), BF16)), grid=(T // tm, nj),
        in_specs=[pl.BlockSpec((tm, D), lambda i, j: (i, 0)), pl.BlockSpec((1, D), lambda i, j: (0, 0)),
                  pl.BlockSpec((D, D), lambda i, j: (0, j)), pl.BlockSpec((D, D), lambda i, j: (j, 0))],
        out_specs=(pl.BlockSpec((tm, D), lambda i, j: (i, 0)), pl.BlockSpec((tm, D), lambda i, j: (i, j))),
        scratch_shapes=[pltpu.VMEM((tm, D), BF16), pltpu.VMEM((tm, D), F32)],
        name="mlp_fwd", sem=("parallel", "arbitrary"), args=(x1, g, wup, wdn))


def _loss_bwd(x, g, tgt):
    T, D = x.shape
    tm = min(512, T)

    def kern(x_ref, g_ref, t_ref, st_ref, dx_ref):
        i = pl.program_id(0)

        @pl.when(i == 0)
        def _():
            st_ref[...] = jnp.zeros_like(st_ref)

        gg = g_ref[...]
        r, xh = _rms_stats(x_ref[...])
        e = xh * gg - t_ref[...]
        part = 0.5 * jnp.sum(jnp.mean(e * e, axis=-1, keepdims=True), axis=0, keepdims=True)
        dx, dg = _rms_bwd(e * (1.0 / D), xh, r, gg)
        dx_ref[...] = dx
        st_ref[0:1, :] += dg
        st_ref[1:2, 0:1] += part

    return pl.pallas_call(
        kern, out_shape=(jax.ShapeDtypeStruct((8, D), F32), jax.ShapeDtypeStruct((T, D), F32)), grid=(T // tm,),
        in_specs=[pl.BlockSpec((tm, D), lambda i: (i, 0)), pl.BlockSpec((1, D), lambda i: (0, 0)),
                  pl.BlockSpec((tm, D), lambda i: (i, 0))],
        out_specs=(pl.BlockSpec((8, D), lambda i: (0, 0)), pl.BlockSpec((tm, D), lambda i: (i, 0))),
        name="loss_bwd", compiler_params=_params("arbitrary"))(x, g, tgt)


def _mlp_bwd(dx2, x1, a, g, wup, wdn, comm=None):
    T, D = x1.shape
    tm = min(512, T)
    nj = D_FF // D

    def kern(dx2_ref, x1_ref, a_ref, g_ref, wup_ref, wdn_ref, dx1_ref, da_ref, ut_ref, h2t_ref, dg_ref, dyb_scr, acc_scr):
        i, j = pl.program_id(0), pl.program_id(1)

        @pl.when((i == 0) & (j == 0))
        def _():
            dg_ref[...] = jnp.zeros_like(dg_ref)

        @pl.when(j == 0)
        def _():
            dyb_scr[...] = dx2_ref[...].astype(BF16)
            acc_scr[...] = jnp.zeros_like(acc_scr)

        du = _nt(dyb_scr[...], wdn_ref[...])
        relu = jnp.maximum(a_ref[...].astype(F32), 0.0)
        da = (du * (2.0 * relu)).astype(BF16)
        da_ref[...] = da
        ut_ref[...] = jnp.square(relu).astype(BF16).T
        acc_scr[...] += _nt(da, wup_ref[...])

        @pl.when(j == nj - 1)
        def _():
            gg = g_ref[...]
            r, xh = _rms_stats(x1_ref[...])
            h2t_ref[...] = (xh * gg).astype(BF16).T
            dx, dg = _rms_bwd(acc_scr[...], xh, r, gg)
            dx1_ref[...] = dx2_ref[...] + dx
            dg_ref[...] += dg

    return _hosted_call(
        comm, kern,
        out_shape=(jax.ShapeDtypeStruct((T, D), F32), jax.ShapeDtypeStruct((T, D_FF), BF16),
                   jax.ShapeDtypeStruct((D_FF, T), BF16), jax.ShapeDtypeStruct((D, T), BF16), jax.ShapeDtypeStruct((1, D), F32)),
        grid=(T // tm, nj),
        in_specs=[pl.BlockSpec((tm, D), lambda i, j: (i, 0)), pl.BlockSpec((tm, D), lambda i, j: (i, 0)),
                  pl.BlockSpec((tm, D), lambda i, j: (i, j)), pl.BlockSpec((1, D), lambda i, j: (0, 0)),
                  pl.BlockSpec((D, D), lambda i, j: (0, j)), pl.BlockSpec((D, D), lambda i, j: (j, 0))],
        out_specs=(pl.BlockSpec((tm, D), lambda i, j: (i, 0)), pl.BlockSpec((tm, D), lambda i, j: (i, j)),
                   pl.BlockSpec((D, tm), lambda i, j: (j, i)), pl.BlockSpec((D, tm), lambda i, j: (0, i)),
                   pl.BlockSpec((1, D), lambda i, j: (0, 0))),
        scratch_shapes=[pltpu.VMEM((tm, D), BF16), pltpu.VMEM((tm, D), F32)],
        name="mlp_bwd", sem=("arbitrary", "arbitrary"), args=(dx2, x1, a, g, wup, wdn))


def _mix_bwd(dx1, ya, yc, pgate, b_gates, wao, wco, wo):
    T, D = dx1.shape
    tm = min(512, T)

    def kern(dx_ref, ya_ref, yc_ref, pg_ref, bg_ref, wao_ref, wco_ref, wo_ref,
             datt_ref, dcv_ref, dya_ref, dyc_ref, dgt_ref, dbg_ref):
        @pl.when(pl.program_id(0) == 0)
        def _():
            dbg_ref[...] = jnp.zeros_like(dbg_ref)

        dm = _nt(dx_ref[...].astype(BF16), wo_ref[...])
        sa = jax.nn.sigmoid(pg_ref[:, 0:D].astype(F32) + bg_ref[:, 0:D])
        sc = jax.nn.sigmoid(pg_ref[:, D:2 * D].astype(F32) + bg_ref[:, D:2 * D])
        dya = (dm * sa).astype(BF16)
        dyc = (dm * sc).astype(BF16)
        dga = dm * ya_ref[...].astype(F32) * (sa * (1.0 - sa))
        dgc = dm * yc_ref[...].astype(F32) * (sc * (1.0 - sc))
        dya_ref[...] = dya
        dyc_ref[...] = dyc
        dgt_ref[:, 0:D] = dga.astype(BF16)
        dgt_ref[:, D:2 * D] = dgc.astype(BF16)
        dbg_ref[:, 0:D] += jnp.sum(dga, axis=0, keepdims=True)
        dbg_ref[:, D:2 * D] += jnp.sum(dgc, axis=0, keepdims=True)
        datt_ref[...] = _nt(dya, wao_ref[...]).astype(BF16)
        dcv_ref[...] = _nt(dyc, wco_ref[...]).astype(BF16)

    row = lambda w: pl.BlockSpec((tm, w), lambda i: (i, 0))
    full = lambda a, b: pl.BlockSpec((a, b), lambda i: (0, 0))
    bf = jax.ShapeDtypeStruct((T, D), BF16)
    return pl.pallas_call(
        kern,
        out_shape=(bf, bf, bf, bf, jax.ShapeDtypeStruct((T, GATE_W), BF16), jax.ShapeDtypeStruct((1, GATE_W), F32)),
        grid=(T // tm,),
        in_specs=[row(D), row(D), row(D), row(GATE_W), full(1, GATE_W)] + [_resident((D, D))] * 3,
        out_specs=(row(D), row(D), row(D), row(D), row(GATE_W), full(1, GATE_W)),
        name="mix_bwd", compiler_params=_params("arbitrary"))(dx1, ya, yc, pgate, b_gates, wao, wco, wo)


def _conv_bwd(dcv, pconv, conv_w, conv_b, seq):
    T = pconv.shape[0]
    tm = min(256, seq)
    per_seq = seq // tm
    D = D_MODEL
    nb16 = T // 16

    def kern(dcv_ref, dcvn_ref, cur_ref, prev_ref, next_ref, w_ref, b_ref, o_ref, dwb_ref):
        i = pl.program_id(0)

        @pl.when(i == 0)
        def _():
            dwb_ref[...] = jnp.zeros_like(dwb_ref)

        first_i = ((i % per_seq) == 0).astype(jnp.int32)
        keep_next = 1.0 - (((i + 1) % per_seq) == 0).astype(F32)
        cb = cur_ref[:, 0:D].astype(F32)
        cc = cur_ref[:, D:2 * D].astype(F32)
        cu = cur_ref[:, 2 * D:3 * D].astype(F32)
        yc = cc * cu
        halo_yc = prev_ref[:, D:2 * D].astype(F32) * prev_ref[:, 2 * D:3 * D].astype(F32)
        s1, s2 = _conv_taps(yc, halo_yc, first_i)
        w0, w1, w2 = w_ref[0:1, :], w_ref[1:2, :], w_ref[2:3, :]
        z = w0 * s2 + w1 * s1 + w2 * yc
        dcv = dcv_ref[...].astype(F32)
        dz = dcv * cb
        dzn = dcvn_ref[...].astype(F32) * next_ref[:, 0:D].astype(F32) * keep_next
        n1, n2 = _pick_row(dzn, 0), _pick_row(dzn, 1)
        rowid = lax.broadcasted_iota(jnp.int32, dz.shape, 0)
        u1 = jnp.where(rowid == tm - 1, n1, pltpu.roll(dz, tm - 1, 0))
        u2 = jnp.where(rowid == tm - 1, n2, jnp.where(rowid == tm - 2, n1, pltpu.roll(dz, tm - 2, 0)))
        dyc = w2 * dz + w1 * u1 + w0 * u2
        o_ref[:, 0:D] = (dcv * (z + b_ref[...])).astype(BF16)
        o_ref[:, D:2 * D] = (dyc * cu).astype(BF16)
        o_ref[:, 2 * D:3 * D] = (dyc * cc).astype(BF16)
        dwb_ref[0:1, :] += jnp.sum(dz * s2, axis=0, keepdims=True)
        dwb_ref[1:2, :] += jnp.sum(dz * s1, axis=0, keepdims=True)
        dwb_ref[2:3, :] += jnp.sum(dz * yc, axis=0, keepdims=True)
        dwb_ref[3:4, :] += jnp.sum(dz, axis=0, keepdims=True)

    prev_map = lambda i: (jnp.maximum(i * (tm // 16) - 1, 0), 0)
    next_map = lambda i: (jnp.minimum((i + 1) * (tm // 16), nb16 - 1), 0)
    return pl.pallas_call(
        kern, out_shape=(jax.ShapeDtypeStruct((T, CONV_W), BF16), jax.ShapeDtypeStruct((8, D), F32)), grid=(T // tm,),
        in_specs=[pl.BlockSpec((tm, D), lambda i: (i, 0)), pl.BlockSpec((16, D), next_map),
                  pl.BlockSpec((tm, CONV_W), lambda i: (i, 0)), pl.BlockSpec((16, CONV_W), prev_map),
                  pl.BlockSpec((16, CONV_W), next_map),
                  pl.BlockSpec((3, D), lambda i: (0, 0)), pl.BlockSpec((1, D), lambda i: (0, 0))],
        out_specs=(pl.BlockSpec((tm, CONV_W), lambda i: (i, 0)), pl.BlockSpec((8, D), lambda i: (0, 0))),
        name="conv_bwd", compiler_params=_params("arbitrary"))(dcv, dcv, pconv, pconv, pconv, conv_w, conv_b)


def _attn_bwd(pqkv, datt, lse, sinks, bias, seq, comm=None):
    T = pqkv.shape[0]
    nblk = seq // WINDOW
    nseq = T // seq
    KVW = KV_W

    def kern(sink_ref, cur_ref, prev_ref, do_ref, lse_ref, bias_ref, dq_ref, dkv_ref, ds_ref, kc_scr, vc_scr, dqt_scr):
        b, st = pl.program_id(0), pl.program_id(1)

        @pl.when((b == 0) & (st == 0))
        def _():
            ds_ref[...] = jnp.zeros_like(ds_ref)

        @pl.when(st == 0)
        def _():
            kc_scr[...] = jnp.zeros_like(kc_scr)
            vc_scr[...] = jnp.zeros_like(vc_scr)

        @pl.when(st < nblk)
        def _():
            first_i = (st == 0).astype(jnp.int32)
            for hk in range(N_KV_HEADS):
                heads = [GQA_GROUP * hk + g for g in range(GQA_GROUP)]
                k_band, v_band = _kv_band(cur_ref, prev_ref, hk)
                q_g = _stack_heads(cur_ref, hk)
                do_g = _stack_heads(do_ref, hk)
                sink = _lane_row([sink_ref[h] for h in heads])
                lse_g = jnp.concatenate([lse_ref[h:h + 1, :] for h in heads], axis=1)
                p = jnp.exp(_nt(k_band, q_g * ATTN_SCALE) + bias_ref[first_i, hk] - lse_g)
                dp = _nt(v_band, do_g)
                d_row = jnp.sum(p * dp, axis=0, keepdims=True)
                dss = (p * (dp - d_row) * ATTN_SCALE).astype(BF16)
                dqt = _tn(k_band, dss)
                dk_b = _nn(dss, q_g)
                dv_b = _nn(p.astype(BF16), do_g)
                psd = jnp.exp(sink - lse_g) * d_row
                for g, h in enumerate(heads):
                    dqt_scr[HEAD_DIM * h:HEAD_DIM * (h + 1), :] = dqt[:, WINDOW * g:WINDOW * (g + 1)].astype(BF16)
                    ds_ref[0:1, h:h + 1] -= jnp.sum(psd[:, WINDOW * g:WINDOW * (g + 1)], axis=1, keepdims=True)
                ksl = slice(HEAD_DIM * hk, HEAD_DIM * (hk + 1))
                vsl = slice(KVW + HEAD_DIM * hk, KVW + HEAD_DIM * (hk + 1))
                dkv_ref[:, ksl] = (kc_scr[:, ksl] + dk_b[0:WINDOW]).astype(BF16)
                dkv_ref[:, vsl] = (vc_scr[:, ksl] + dv_b[0:WINDOW]).astype(BF16)
                kc_scr[:, ksl] = dk_b[WINDOW:2 * WINDOW]
                vc_scr[:, ksl] = dv_b[WINDOW:2 * WINDOW]
            dq_ref[...] = dqt_scr[...].T

        @pl.when(st == nblk)
        def _():
            dkv_ref[:, 0:KVW] = kc_scr[...].astype(BF16)
            dkv_ref[:, KVW:2 * KVW] = vc_scr[...].astype(BF16)

    cur_map = lambda b, s: (b * nblk + jnp.minimum(s, nblk - 1), 0)
    prev_row = lambda b, s: b * nblk + jnp.clip(s - 1, 0, nblk - 1)
    return _hosted_call(
        comm, kern,
        out_shape=(jax.ShapeDtypeStruct((T, D_MODEL), BF16), jax.ShapeDtypeStruct((T, 2 * KVW), BF16),
                   jax.ShapeDtypeStruct((8, 128), F32)),
        grid=(nseq, nblk + 1),
        in_specs=[pl.BlockSpec(memory_space=pltpu.SMEM),
                  pl.BlockSpec((WINDOW, QKV_W), cur_map),
                  pl.BlockSpec((WINDOW, 2 * KVW), lambda b, s: (prev_row(b, s), 2)),
                  pl.BlockSpec((WINDOW, D_MODEL), cur_map),
                  pl.BlockSpec((N_Q_HEADS, WINDOW), lambda b, s: (0, b * nblk + jnp.minimum(s, nblk - 1))),
                  _resident(bias.shape)],
        out_specs=(pl.BlockSpec((WINDOW, D_MODEL), cur_map),
                   pl.BlockSpec((WINDOW, 2 * KVW), lambda b, s: (prev_row(b, s), 0)),
                   pl.BlockSpec((8, 128), lambda b, s: (0, 0))),
        scratch_shapes=[pltpu.VMEM((WINDOW, KVW), F32), pltpu.VMEM((WINDOW, KVW), F32), pltpu.VMEM((D_MODEL, WINDOW), BF16)],
        name="attn_bwd", sem=("arbitrary", "arbitrary"), args=(sinks, pqkv, pqkv, datt, lse, bias))


def _piece_tiles(pieces):
    out, start = [], 0
    for arr, width in pieces:
        out.append((arr, start, width // COL_TILE))
        start += width // COL_TILE
    return out, start


def _inproj_bwd(pieces, w_in, x, dx_in, g, comm=None):
    T, D = x.shape
    tm = min(256, T)

    def kern(*refs):
        p_refs = refs[:len(pieces)]
        w_ref, x_ref, dxin_ref, g_ref, dx_ref, dg_ref = refs[len(pieces):]

        @pl.when(pl.program_id(0) == 0)
        def _():
            dg_ref[...] = jnp.zeros_like(dg_ref)

        dh, off = None, 0
        for p_ref, (_, width) in zip(p_refs, pieces):
            part = _nt(p_ref[...], w_ref[:, off:off + width])
            dh = part if dh is None else dh + part
            off += width
        gg = g_ref[...]
        r, xh = _rms_stats(x_ref[...])
        dx, dg = _rms_bwd(dh, xh, r, gg)
        dx_ref[...] = dxin_ref[...] + dx
        dg_ref[...] += dg

    row = lambda wd: pl.BlockSpec((tm, wd), lambda i: (i, 0))
    return _hosted_call(
        comm, kern, out_shape=(jax.ShapeDtypeStruct((T, D), F32), jax.ShapeDtypeStruct((1, D), F32)), grid=(T // tm,),
        in_specs=[row(wd) for _, wd in pieces] + [_resident((D, IN_COLS)), row(D), row(D), pl.BlockSpec((1, D), lambda i: (0, 0))],
        out_specs=(row(D), pl.BlockSpec((1, D), lambda i: (0, 0))),
        name="inproj_bwd", sem=("arbitrary",), args=(*[a for a, _ in pieces], w_in, x, dx_in, g))


def _dw_pieces(lhs_t, pieces):
    K, T = lhs_t.shape
    tk = min(2048, T)
    nt = T // tk
    tiles, nj = _piece_tiles(pieces)

    def kern(*refs):
        lhs_ref = refs[0]
        p_refs = refs[1:1 + len(tiles)]
        o_ref, ob_ref = refs[1 + len(tiles):]
        j, t = pl.program_id(0), pl.program_id(1)

        @pl.when(t == 0)
        def _():
            o_ref[...] = jnp.zeros_like(o_ref)

        for p_ref, (_, start, n) in zip(p_refs, tiles):
            @pl.when((j >= start) & (j < start + n))
            def _(p_ref=p_ref):
                o_ref[...] += _nn(lhs_ref[...], p_ref[...])

        @pl.when(t == nt - 1)
        def _():
            ob_ref[...] = o_ref[...].astype(BF16)

    def p_map(start, n):
        return lambda j, t: (jnp.where((j >= start) & (j < start + n), t, 0), jnp.clip(j - start, 0, n - 1))

    N = nj * COL_TILE
    return pl.pallas_call(
        kern, out_shape=(jax.ShapeDtypeStruct((K, N), F32), jax.ShapeDtypeStruct((K, N), BF16)), grid=(nj, nt),
        in_specs=[pl.BlockSpec((K, tk), lambda j, t: (0, t))] + [pl.BlockSpec((tk, COL_TILE), p_map(s, n)) for _, s, n in tiles],
        out_specs=(pl.BlockSpec((K, COL_TILE), lambda j, t: (0, j)), pl.BlockSpec((K, COL_TILE), lambda j, t: (0, j))),
        name="dw_pieces", compiler_params=_params("arbitrary", "arbitrary"))(lhs_t, *[a for a, _, _ in tiles])


def _dw(lhs_t, rhs):
    K, T = lhs_t.shape
    N = rhs.shape[1]
    tk = min(2048, T)
    nt = T // tk
    W = D_MODEL

    def kern(lhs_ref, rhs_ref, o_ref, ob_ref):
        t = pl.program_id(2)

        @pl.when(t == 0)
        def _():
            o_ref[...] = jnp.zeros_like(o_ref)

        o_ref[...] += _nn(lhs_ref[...], rhs_ref[...].astype(BF16))

        @pl.when(t == nt - 1)
        def _():
            ob_ref[...] = o_ref[...].astype(BF16)

    omap = lambda i, j, t: (i, j)
    return pl.pallas_call(
        kern, out_shape=(jax.ShapeDtypeStruct((K, N), F32), jax.ShapeDtypeStruct((K, N), BF16)), grid=(K // W, N // W, nt),
        in_specs=[pl.BlockSpec((W, tk), lambda i, j, t: (i, t)), pl.BlockSpec((tk, W), lambda i, j, t: (t, j))],
        out_specs=(pl.BlockSpec((W, W), omap), pl.BlockSpec((W, W), omap)),
        name="dw", compiler_params=_params("arbitrary", "arbitrary", "arbitrary"))(lhs_t, rhs)


BIG = (("w_in", D_MODEL, IN_COLS, "col"), ("w_attn_out", D_MODEL, D_MODEL, "row"), ("w_conv_out", D_MODEL, D_MODEL, "row"),
       ("w_o", D_MODEL, D_MODEL, "row"), ("w_up", D_MODEL, D_FF, "col"), ("w_down", D_FF, D_MODEL, "row"))


def _shard_dims(rows, cols, kind):
    return (rows, cols // N_CHIP) if kind == "col" else (rows // N_CHIP, cols)


def _window(ref, rows, cols, kind, chip, half):
    sr, sc = _shard_dims(rows, cols, kind)
    hr = sr // 2
    if kind == "col":
        return ref.at[pl.ds(half * hr, hr), pl.ds(chip * sc, sc)]
    return ref.at[pl.ds(chip * sr + half * hr, hr), :]


def _mesh_pos():
    x, y, c = lax.axis_index("x"), lax.axis_index("y"), lax.axis_index("c")
    return x, y, c, 2 * x + y


_REL_BITS = (2, 1, 3)


def _rel_dev(x, y, c, r):
    return ((1 - x, y, c), (x, 1 - y, c), (1 - x, 1 - y, c))[r]


def _for_my_chip(j, fn):
    for js in range(N_CHIP):
        pl.when(j == js)(functools.partial(fn, js))


def _cast_into_full(j_arr, shard, l, rows, cols, kind):
    sr, sc = _shard_dims(rows, cols, kind)
    tr = min(256, sr)

    def kern(j_ref, s_ref, o_ref):
        o_ref[...] = s_ref[...].astype(BF16)

    if kind == "col":
        omap = lambda i, j_ref: (i, j_ref[0])
    else:
        omap = lambda i, j_ref: (j_ref[0] * (sr // tr) + i, 0)
    gs = pltpu.PrefetchScalarGridSpec(
        num_scalar_prefetch=1, grid=(sr // tr,),
        in_specs=[pl.BlockSpec((None, tr, sc), lambda i, j_ref: (l, i, 0))], out_specs=pl.BlockSpec((tr, sc), omap))
    return pl.pallas_call(kern, out_shape=jax.ShapeDtypeStruct((rows, cols), BF16), grid_spec=gs, name="cast_into_full",
                          compiler_params=_params("arbitrary"))(j_arr, shard)


def _gather(fulls, cw=None):
    comm = _gather_comm(fulls, cw)
    n = len(comm.inputs)

    def body(*refs):
        comm.start(refs[:n], refs[n:2 * n], refs[2 * n:])
        comm.finish(refs[:n], refs[n:2 * n], refs[2 * n:])

    return pl.pallas_call(
        body, out_shape=comm.out_shape, in_specs=[ANY] * n, out_specs=[ANY] * n, input_output_aliases=comm.aliases,
        scratch_shapes=comm.scratch, name="gather_weights")(*comm.inputs)


def _gather_comm(fulls, cw=None):
    n_big = len(fulls)
    n_piece = n_big + (0 if cw is None else 1)

    def pieces(in_refs, o_refs, js, c):
        def piece(p, chip, half):
            if p == n_big:
                return o_refs[p].at[half, chip]
            _, rows, cols, kind = fulls[p]
            return _window(o_refs[p], rows, cols, kind, chip, half)

        def mine(p):
            return in_refs[p].at[c] if p == n_big else piece(p, js, c)

        return piece, mine

    def local_copies(in_refs, piece, js, loc_sem):
        if cw is None:
            return []
        return [pltpu.make_async_copy(in_refs[n_big].at[half], piece(n_big, js, half), loc_sem.at[half]) for half in range(2)]

    def ici_copy(piece, mine, js, x, y, c, r, p, send_sem, recv_sem):
        return pltpu.make_async_remote_copy(mine(p), piece(p, js, c), send_sem.at[r * n_piece + p], recv_sem.at[r * n_piece + p],
                                            _rel_dev(x, y, c, r), MESH)

    def start(in_refs, o_refs, sems):
        send_sem, recv_sem, _, _, loc_sem = sems
        x, y, c, j = _mesh_pos()

        def run(js):
            piece, mine = pieces(in_refs, o_refs, js, c)
            for cp in local_copies(in_refs, piece, js, loc_sem):
                cp.start()
            for r in range(3):
                for p in range(n_piece):
                    ici_copy(piece, mine, js, x, y, c, r, p, send_sem, recv_sem).start()

        _for_my_chip(j, run)

    def finish(in_refs, o_refs, sems):
        send_sem, recv_sem, fsend_sem, frecv_sem, loc_sem = sems
        x, y, c, j = _mesh_pos()

        def run(js):
            piece, mine = pieces(in_refs, o_refs, js, c)
            fwds = []
            for r in range(3):
                ks = js ^ _REL_BITS[r]
                for p in range(n_piece):
                    got = piece(p, ks, c)
                    pltpu.make_async_remote_copy(got, got, send_sem.at[r * n_piece + p], recv_sem.at[r * n_piece + p],
                                                 _rel_dev(x, y, c, r), MESH).wait_recv()
                    cp = pltpu.make_async_remote_copy(got, got, fsend_sem.at[r * n_piece + p], frecv_sem.at[r * n_piece + p],
                                                      (x, y, 1 - c), MESH)
                    cp.start()
                    fwds.append(cp)
            for r in range(3):
                ks = js ^ _REL_BITS[r]
                for p in range(n_piece):
                    got = piece(p, ks, 1 - c)
                    pltpu.make_async_remote_copy(got, got, fsend_sem.at[r * n_piece + p], frecv_sem.at[r * n_piece + p],
                                                 (x, y, 1 - c), MESH).wait_recv()
            for r in range(3):
                for p in range(n_piece):
                    ici_copy(piece, mine, js, x, y, c, r, p, send_sem, recv_sem).wait_send()
            for cp in fwds:
                cp.wait_send()
            for cp in local_copies(in_refs, piece, js, loc_sem):
                cp.wait()

        _for_my_chip(j, run)

    out_shape = [jax.ShapeDtypeStruct(a.shape, BF16) for a, _, _, _ in fulls]
    ins = [a for a, _, _, _ in fulls]
    if cw is not None:
        out_shape.append(jax.ShapeDtypeStruct((DEPTH, N_CHIP, 3, D_MODEL // N_CHIP), F32))
        ins.append(cw)
    scratch = [pltpu.SemaphoreType.DMA((3 * n_piece,))] * 4 + [pltpu.SemaphoreType.DMA((2,))]
    return _Hosted(ins, out_shape, {p: p for p in range(n_big)}, scratch, start, finish)


def _sibling_exchange(gb):
    n = len(gb)

    def body(*refs):
        g_refs, o_refs = refs[:n], refs[n:2 * n]
        send_sem, recv_sem = refs[2 * n:]
        x, y, c, _ = _mesh_pos()
        cps = []
        for t in range(n):
            _, rows, cols, kind = gb[t]
            for chip in range(N_CHIP):
                out_w = _window(g_refs[t], rows, cols, kind, chip, 1 - c)
                dst_w = _window(o_refs[t], rows, cols, kind, chip, 1 - c)
                cp = pltpu.make_async_remote_copy(out_w, dst_w, send_sem.at[N_CHIP * t + chip], recv_sem.at[N_CHIP * t + chip],
                                                  (x, y, 1 - c), MESH)
                cp.start()
                cps.append(cp)
        for t in range(n):
            _, rows, cols, kind = gb[t]
            for chip in range(N_CHIP):
                w = _window(o_refs[t], rows, cols, kind, chip, c)
                pltpu.make_async_remote_copy(w, w, send_sem.at[N_CHIP * t + chip], recv_sem.at[N_CHIP * t + chip],
                                             (x, y, 1 - c), MESH).wait_recv()
        for cp in cps:
            cp.wait_send()

    return pl.pallas_call(
        body, out_shape=[jax.ShapeDtypeStruct(a.shape, BF16) for a, _, _, _ in gb], in_specs=[ANY] * n, out_specs=[ANY] * n,
        scratch_shapes=[pltpu.SemaphoreType.DMA((N_CHIP * n,))] * 2, name="grad_sibling_exchange")(*[a for a, _, _, _ in gb])


def _half_add(c_arr, g, sib, rows, cols, kind):
    sr, sc = _shard_dims(rows, cols, kind)
    hr = sr // 2

    def kern(c_ref, g_ref, s_ref, ob_ref, of_ref):
        v = g_ref[...] + s_ref[...].astype(F32)
        of_ref[...] = v
        ob_ref[...] = v.astype(BF16)

    if kind == "col":
        imap = lambda j, c_ref: (c_ref[0], j)
    else:
        imap = lambda j, c_ref: (2 * j + c_ref[0], 0)
    gs = pltpu.PrefetchScalarGridSpec(
        num_scalar_prefetch=1, grid=(N_CHIP,),
        in_specs=[pl.BlockSpec((hr, sc), imap), pl.BlockSpec((hr, sc), imap)],
        out_specs=[pl.BlockSpec((None, hr, sc), lambda j, c_ref: (j, 0, 0))] * 2)
    return pl.pallas_call(
        kern, out_shape=(jax.ShapeDtypeStruct((N_CHIP, hr, sc), BF16), jax.ShapeDtypeStruct((N_CHIP, hr, sc), F32)),
        grid_spec=gs, name="grad_half_add", compiler_params=_params("arbitrary"))(c_arr, g, sib)


def _chip_exchange(sbs):
    comm = _chip_exchange_comm(sbs)
    n = len(sbs)

    def body(*refs):
        comm.start(refs[:n], refs[n:2 * n], refs[2 * n:])
        comm.finish(refs[:n], refs[n:2 * n], refs[2 * n:])

    return pl.pallas_call(
        body, out_shape=comm.out_shape, in_specs=[ANY] * n, out_specs=[ANY] * n, scratch_shapes=comm.scratch,
        name="grad_chip_exchange")(*sbs)


def _chip_exchange_comm(sbs):
    n = len(sbs)

    def copies(s_refs, o_refs, sems):
        send_sem, recv_sem = sems
        x, y, c, j = _mesh_pos()
        return [pltpu.make_async_remote_copy(s_refs[t].at[j ^ _REL_BITS[r]], o_refs[t].at[r], send_sem.at[n * r + t],
                                             recv_sem.at[n * r + t], _rel_dev(x, y, c, r), MESH)
                for r in range(3) for t in range(n)]

    def start(s_refs, o_refs, sems):
        for cp in copies(s_refs, o_refs, sems):
            cp.start()

    def finish(s_refs, o_refs, sems):
        for cp in copies(s_refs, o_refs, sems):
            cp.wait()

    return _Hosted(sbs, [jax.ShapeDtypeStruct((3,) + a.shape[1:], BF16) for a in sbs], {},
                   [pltpu.SemaphoreType.DMA((3 * n,))] * 2, start, finish)


def _owner_sum(jc_arr, sf, rb, l, into=None):
    _, hr, sc = sf.shape

    def kern(jc_ref, s_ref, r0_ref, r1_ref, r2_ref, *rest):
        o_ref = rest[-1]
        o_ref[...] = ((s_ref[...] + r0_ref[...].astype(F32)) + r1_ref[...].astype(F32)) + r2_ref[...].astype(F32)

    in_specs = [pl.BlockSpec((None, hr, sc), lambda i, jc_ref: (jc_ref[0], 0, 0))]
    in_specs += [pl.BlockSpec((None, hr, sc), lambda i, jc_ref, r=r: (r, 0, 0)) for r in range(3)]
    args = [jc_arr, sf, rb, rb, rb]
    aliases = {}
    if into is not None:
        in_specs.append(ANY)
        args.append(into)
        aliases = {len(args) - 1: 0}
    gs = pltpu.PrefetchScalarGridSpec(
        num_scalar_prefetch=1, grid=(1,), in_specs=in_specs,
        out_specs=pl.BlockSpec((None, hr, sc), lambda i, jc_ref: (l, jc_ref[1], 0)))
    return pl.pallas_call(kern, out_shape=jax.ShapeDtypeStruct((DEPTH, 2 * hr, sc), F32), grid_spec=gs,
                          input_output_aliases=aliases, name="grad_owner_sum", compiler_params=_params("arbitrary"))(*args)


def _sibling_assemble(grads):
    n = len(grads)

    def body(*refs):
        o_refs = refs[n:2 * n]
        send_sem, recv_sem = refs[2 * n:]
        x, y, c, _ = _mesh_pos()
        cps = []
        for q in range(n):
            hr = grads[q].shape[1] // 2
            for l in range(DEPTH):
                mine = o_refs[q].at[l, pl.ds(c * hr, hr), :]
                cp = pltpu.make_async_remote_copy(mine, mine, send_sem.at[DEPTH * q + l], recv_sem.at[DEPTH * q + l],
                                                  (x, y, 1 - c), MESH)
                cp.start()
                cps.append(cp)
        for q in range(n):
            hr = grads[q].shape[1] // 2
            for l in range(DEPTH):
                theirs = o_refs[q].at[l, pl.ds((1 - c) * hr, hr), :]
                pltpu.make_async_remote_copy(theirs, theirs, send_sem.at[DEPTH * q + l], recv_sem.at[DEPTH * q + l],
                                             (x, y, 1 - c), MESH).wait_recv()
        for cp in cps:
            cp.wait_send()

    return pl.pallas_call(
        body, out_shape=[jax.ShapeDtypeStruct(g.shape, F32) for g in grads], in_specs=[ANY] * n, out_specs=[ANY] * n,
        input_output_aliases={q: q for q in range(n)},
        scratch_shapes=[pltpu.SemaphoreType.DMA((DEPTH * n,))] * 2, name="grad_sibling_assemble")(*grads)


def _adamw_math(w, g, m, v):
    m = ADAM_B1 * m + (1.0 - ADAM_B1) * g
    v = ADAM_B2 * v + (1.0 - ADAM_B2) * jnp.square(g)
    m_hat = m / (1.0 - ADAM_B1 ** ADAM_STEP)
    v_hat = v / (1.0 - ADAM_B2 ** ADAM_STEP)
    delta = -ADAM_LR * (m_hat / (jnp.sqrt(v_hat) + ADAM_EPS) + ADAM_WD * w)
    return delta, m, v


def _adamw(w, g, m, v):
    shape = w.shape
    C = shape[-1]
    R = int(np.prod(shape[:-1]))
    tr = min(256, R)
    args = [a.reshape(R, C) for a in (w, g, m, v)]

    def kern(w_ref, g_ref, m_ref, v_ref, d_ref, nm_ref, nv_ref):
        d, nm, nv = _adamw_math(w_ref[...], g_ref[...], m_ref[...], v_ref[...])
        d_ref[...] = d
        nm_ref[...] = nm
        nv_ref[...] = nv

    spec = pl.BlockSpec((tr, C), lambda i: (i, 0))
    outs = pl.pallas_call(
        kern, out_shape=[jax.ShapeDtypeStruct((R, C), F32)] * 3, grid=(R // tr,), in_specs=[spec] * 4, out_specs=[spec] * 3,
        name="adamw", compiler_params=_params("parallel"))(*args)
    return [o.reshape(shape) for o in outs]


_ROW_G_MIX, _ROW_B_GATES, _ROW_SINKS, _ROW_CONV, _ROW_G_MLP, _ROW_G_FINAL, _ROW_LOSS = 0, 2, 6, 8, 16, 18, 19


def _small_step(parts, params, moms, vels):
    names = ["g_mix", "b_gates", "sinks", "conv_w", "conv_b", "g_mlp", "g_final"]
    D = D_MODEL
    QW = D // N_CHIP
    n_dev = 8

    def body(*refs):
        it = iter(refs)
        dgmix = [next(it) for _ in range(DEPTH)]
        dbg = [next(it) for _ in range(DEPTH)]
        dsk = [next(it) for _ in range(DEPTH)]
        dwb = [next(it) for _ in range(DEPTH)]
        dgmlp = [next(it) for _ in range(DEPTH)]
        lst = next(it)
        p_refs = {n: next(it) for n in names}
        m_refs = {n: next(it) for n in names}
        v_refs = {n: next(it) for n in names}
        loss_ref = next(it)
        outs = {n: [next(it) for _ in range(4)] for n in names}
        pack_ref, all_ref, send_sem, recv_sem = next(it), next(it), next(it), next(it)

        x, y, c, j = _mesh_pos()
        me = 4 * x + 2 * y + c
        pack_ref[...] = jnp.zeros_like(pack_ref)
        for l in range(DEPTH):
            pack_ref[_ROW_G_MIX + l:_ROW_G_MIX + l + 1, :] = dgmix[l][...]
            pack_ref[_ROW_B_GATES + 2 * l:_ROW_B_GATES + 2 * l + 1, :] = dbg[l][:, 0:D]
            pack_ref[_ROW_B_GATES + 2 * l + 1:_ROW_B_GATES + 2 * l + 2, :] = dbg[l][:, D:2 * D]
            pack_ref[_ROW_SINKS + l:_ROW_SINKS + l + 1, 0:128] = dsk[l][0:1, :]
            pack_ref[_ROW_CONV + 4 * l:_ROW_CONV + 4 * l + 4, :] = dwb[l][0:4, :]
            pack_ref[_ROW_G_MLP + l:_ROW_G_MLP + l + 1, :] = dgmlp[l][...]
        pack_ref[_ROW_G_FINAL:_ROW_G_FINAL + 1, :] = lst[0:1, :]
        pack_ref[_ROW_LOSS:_ROW_LOSS + 1, :] = lst[1:2, :]

        all_ref[me] = pack_ref[...]
        cps = []
        for k in range(1, n_dev):
            dx_, dy_, dc_ = (k >> 2) & 1, (k >> 1) & 1, k & 1
            peer = (x ^ dx_, y ^ dy_, c ^ dc_)
            cp = pltpu.make_async_remote_copy(pack_ref, all_ref.at[me], send_sem.at[k - 1], recv_sem.at[k - 1], peer, MESH)
            cp.start()
            cps.append(cp)
        for cp in cps:
            cp.wait()

        tot = all_ref[0]
        for d in range(1, n_dev):
            tot = tot + all_ref[d]
        pack_ref[...] = tot

        loss_ref[...] = pack_ref[_ROW_LOSS:_ROW_LOSS + 1, 0:1]

        def finish(name, idx, g):
            w, m, v = p_refs[name][idx], m_refs[name][idx], v_refs[name][idx]
            d, nm, nv = _adamw_math(w, g, m, v)
            for ref, val in zip(outs[name], (g, d, nm, nv)):
                ref[idx] = val

        for l in range(DEPTH):
            finish("g_mix", (slice(l, l + 1), slice(None)), pack_ref[_ROW_G_MIX + l:_ROW_G_MIX + l + 1, :])
            finish("g_mlp", (slice(l, l + 1), slice(None)), pack_ref[_ROW_G_MLP + l:_ROW_G_MLP + l + 1, :])
            finish("conv_b", (slice(l, l + 1), slice(None)), pack_ref[_ROW_CONV + 4 * l + 3:_ROW_CONV + 4 * l + 4, :])
            finish("sinks", (slice(l, l + 1), slice(None)), pack_ref[_ROW_SINKS + l:_ROW_SINKS + l + 1, 0:N_Q_HEADS])
            for hf in range(2):
                finish("b_gates", (slice(l, l + 1), slice(hf * D, (hf + 1) * D)),
                       pack_ref[_ROW_B_GATES + 2 * l + hf:_ROW_B_GATES + 2 * l + hf + 1, :])
        finish("g_final", (slice(0, 1), slice(None)), pack_ref[_ROW_G_FINAL:_ROW_G_FINAL + 1, :])

        def conv_w_chip(js):
            for l in range(DEPTH):
                for k in range(3):
                    row = _ROW_CONV + 4 * l + k
                    finish("conv_w", (l, slice(k, k + 1), slice(None)), pack_ref[row:row + 1, js * QW:(js + 1) * QW])

        _for_my_chip(j, conv_w_chip)

    vm = pl.BlockSpec(memory_space=pltpu.VMEM)
    ins = (parts["g_mix"] + parts["b_gates"] + parts["sinks"] + parts["conv"] + parts["g_mlp"] + [parts["loss"]]
           + [params[n] for n in names] + [moms[n] for n in names] + [vels[n] for n in names])
    out_shape = [jax.ShapeDtypeStruct((1, 1), F32)]
    for n in names:
        out_shape += [jax.ShapeDtypeStruct(params[n].shape, F32)] * 4
    res = pl.pallas_call(
        body, out_shape=out_shape, in_specs=[vm] * len(ins), out_specs=[vm] * len(out_shape),
        scratch_shapes=[pltpu.VMEM((SMALL_ROWS, D), F32), pltpu.VMEM((n_dev, SMALL_ROWS, D), F32),
                        pltpu.SemaphoreType.DMA((n_dev - 1,)), pltpu.SemaphoreType.DMA((n_dev - 1,))],
        name="small_allreduce_adamw")(*ins)
    loss = res[0]
    out = {n: res[1 + 4 * i:5 + 4 * i] for i, n in enumerate(names)}
    return loss, out


def kernel(x, g_mix, w_in, b_gates, sinks, w_attn_out, conv_w, conv_b, w_conv_out, w_o, g_mlp, w_up, w_down, g_final, loss_target, m_g_mix, m_w_in, m_b_gates, m_sinks, m_w_attn_out, m_conv_w, m_conv_b, m_w_conv_out, m_w_o, m_g_mlp, m_w_up, m_w_down, m_g_final, v_g_mix, v_w_in, v_b_gates, v_sinks, v_w_attn_out, v_conv_w, v_conv_b, v_w_conv_out, v_w_o, v_g_mlp, v_w_up, v_w_down, v_g_final):
    B, S, D = x.shape
    T = B * S
    big_w = dict(w_in=w_in, w_attn_out=w_attn_out, w_conv_out=w_conv_out, w_o=w_o, w_up=w_up, w_down=w_down)
    big_m = dict(w_in=m_w_in, w_attn_out=m_w_attn_out, w_conv_out=m_w_conv_out, w_o=m_w_o, w_up=m_w_up, w_down=m_w_down)
    big_v = dict(w_in=v_w_in, w_attn_out=v_w_attn_out, w_conv_out=v_w_conv_out, w_o=v_w_o, w_up=v_w_up, w_down=v_w_down)

    c_arr = lax.axis_index("c").astype(jnp.int32).reshape(1)
    j_arr = (2 * lax.axis_index("x") + lax.axis_index("y")).astype(jnp.int32).reshape(1)
    jc_arr = jnp.concatenate([j_arr, c_arr])
    order = [(n, l) for n, _, _, _ in BIG for l in range(DEPTH)]
    dims = {n: (r, c_, k) for n, r, c_, k in BIG}

    full = {(n, l): _cast_into_full(j_arr, big_w[n], l, *dims[n]) for n, l in order}
    mixers = ("w_attn_out", "w_conv_out", "w_o")

    def gather_of(keys):
        return _gather_comm([(full[k],) + dims[k[0]] for k in keys])

    carried = {("proj", 0): [(n, 0) for n in mixers], ("attn", 0): [("w_up", 0), ("w_down", 0)],
               ("mlp", 0): [("w_in", 1)] + [(n, 1) for n in mixers], ("attn", 1): [("w_up", 1), ("w_down", 1)]}

    def carry(fn, where, *args):
        keys = carried.get(where)
        if keys is None:
            return fn(*args)
        res, got = fn(*args, comm=gather_of(keys))
        full.update(zip(keys, got))
        return res

    first = _gather([(full["w_in", 0],) + dims["w_in"]], conv_w)
    full["w_in", 0] = first[0]
    conv_w_full = jnp.transpose(first[1], (0, 2, 1, 3)).reshape(DEPTH, 3, D)
    attn_bias = _attn_bias_table()

    xs = [x.reshape(T, D)]
    saved = []
    for l in range(DEPTH):
        ht, pqkv, pconv, pgate = carry(_norm_proj, ("proj", l), xs[-1], g_mix[l:l + 1], full["w_in", l])
        att, att_t, lse = carry(_attn_fwd, ("attn", l), pqkv, sinks[l], attn_bias, S)
        cv, cv_t = _conv_fwd(pconv, conv_w_full[l], conv_b[l:l + 1], S)
        x1, ya, yc, mg_t = _mix_fwd(xs[-1], att, cv, pgate, b_gates[l:l + 1], full["w_attn_out", l], full["w_conv_out", l],
                                    full["w_o", l])
        x2, a = carry(_mlp_fwd, ("mlp", l), x1, g_mlp[l:l + 1], full["w_up", l], full["w_down", l])
        saved.append(dict(ht=ht, pqkv=pqkv, pconv=pconv, pgate=pgate, att=att, att_t=att_t, lse=lse, cv_t=cv_t, x1=x1, ya=ya,
                          yc=yc, mg_t=mg_t, a=a))
        xs.append(x2)

    loss_stats, dx = _loss_bwd(xs[-1], g_final.reshape(1, D), loss_target.reshape(T, D))

    parts = dict(g_mix=[None] * DEPTH, b_gates=[None] * DEPTH, sinks=[None] * DEPTH, conv=[None] * DEPTH,
                 g_mlp=[None] * DEPTH, loss=loss_stats)
    gf, gb = {}, {}
    pre, got = {}, {}

    def pre_reduce(keys):
        sib = _sibling_exchange([(gb[k],) + dims[k[0]] for k in keys])
        for k, s in zip(keys, sib):
            pre[k] = _half_add(c_arr, gf[k], s, *dims[k[0]])
        return keys

    def exchange_in(fn, keys, *args):
        res, arrived = fn(*args, comm=_chip_exchange_comm([pre[k][0] for k in keys]))
        got.update(zip(keys, arrived))
        return res

    pending = None
    for l in reversed(range(DEPTH)):
        W = {n: full[(n, l)] for n in big_w}
        sv = saved[l]
        mlp_args = (dx, sv["x1"], sv["a"], g_mlp[l:l + 1], W["w_up"], W["w_down"])
        dx1, da, u_t, h2_t, parts["g_mlp"][l] = exchange_in(_mlp_bwd, pending, *mlp_args) if pending else _mlp_bwd(*mlp_args)
        gf["w_up", l], gb["w_up", l] = _dw(h2_t, da)
        gf["w_down", l], gb["w_down", l] = _dw(u_t, dx)
        datt, dcv, dya, dyc, dgate, parts["b_gates"][l] = _mix_bwd(
            dx1, sv["ya"], sv["yc"], sv["pgate"], b_gates[l:l + 1], W["w_attn_out"], W["w_conv_out"], W["w_o"])
        gf["w_o", l], gb["w_o", l] = _dw(sv["mg_t"], dx1)
        gf["w_attn_out", l], gb["w_attn_out", l] = _dw(sv["att_t"], dya)
        gf["w_conv_out", l], gb["w_conv_out", l] = _dw(sv["cv_t"], dyc)
        dconv, parts["conv"][l] = _conv_bwd(dcv, sv["pconv"], conv_w_full[l], conv_b[l:l + 1], S)
        attn_args = (sv["pqkv"], datt, sv["lse"], sinks[l], attn_bias, S)
        last = l == 0
        if last:
            keys = pre_reduce([(n, l) for n in mixers + ("w_up", "w_down")])
            dq, dkv, parts["sinks"][l] = exchange_in(_attn_bwd, keys, *attn_args)
        else:
            dq, dkv, parts["sinks"][l] = _attn_bwd(*attn_args)
        pieces = [(dq, D), (dkv, QKV_W - D), (dconv, CONV_W), (dgate, GATE_W)]
        gf["w_in", l], gb["w_in", l] = _dw_pieces(sv["ht"], pieces)
        in_args = (pieces, W["w_in"], xs[l], dx1, g_mix[l:l + 1])
        if last:
            dx, parts["g_mix"][l] = exchange_in(_inproj_bwd, pre_reduce([("w_in", l)]), *in_args)
        else:
            dx, parts["g_mix"][l] = _inproj_bwd(*in_args)
            pending = pre_reduce([(n, l) for n, _, _, _ in BIG])

    mine = {}
    for n, l in order:
        mine[n] = _owner_sum(jc_arr, pre[n, l][1], got[n, l], l, mine.get(n))
    grads = dict(zip(mine, _sibling_assemble(list(mine.values()))))

    res = {}
    for n in big_w:
        d, nm, nv = _adamw(big_w[n], grads[n], big_m[n], big_v[n])
        res[n] = (grads[n], d, nm, nv)

    small_p = dict(g_mix=g_mix, b_gates=b_gates, sinks=sinks, conv_w=conv_w, conv_b=conv_b, g_mlp=g_mlp, g_final=g_final.reshape(1, D))
    small_m = dict(g_mix=m_g_mix, b_gates=m_b_gates, sinks=m_sinks, conv_w=m_conv_w, conv_b=m_conv_b, g_mlp=m_g_mlp,
                   g_final=m_g_final.reshape(1, D))
    small_v = dict(g_mix=v_g_mix, b_gates=v_b_gates, sinks=v_sinks, conv_w=v_conv_w, conv_b=v_conv_b, g_mlp=v_g_mlp,
                   g_final=v_g_final.reshape(1, D))
    loss, small = _small_step(parts, small_p, small_m, small_v)
    for n, vals in small.items():
        res[n] = tuple(v.reshape(D) for v in vals) if n == "g_final" else tuple(vals)

    weights = ["g_mix", "w_in", "b_gates", "sinks", "w_attn_out", "conv_w", "conv_b", "w_conv_out", "w_o", "g_mlp", "w_up",
               "w_down", "g_final"]
    out = [loss.reshape(()), dx.reshape(B, S, D)]
    for k in range(4):
        out += [res[n][k] for n in weights]
    return tuple(out)
```

```python
import functools

import numpy as np
import jax
import jax.numpy as jnp
from jax import lax
from jax.experimental import pallas as pl
from jax.experimental.pallas import tpu as pltpu

F32 = jnp.float32
BF16 = jnp.bfloat16

D_MODEL = 1024
HEAD_DIM = 64
N_Q_HEADS = 16
N_KV_HEADS = 4
GQA_GROUP = 4
WINDOW = 128
D_FF = 4096
DEPTH = 2
RMS_EPS = 1e-6
NEG_INF = -1e30
ATTN_SCALE = HEAD_DIM ** -0.5
QKV_W = 1536
CONV_W = 3072
GATE_W = 2048
IN_COLS = QKV_W + CONV_W + GATE_W
COL_TILE = 512
N_CHIP = 4
ADAM_LR = 0.001
ADAM_B1 = 0.9
ADAM_B2 = 0.999
ADAM_EPS = 1e-08
ADAM_WD = 0.01
ADAM_STEP = 10
V7X_VMEM_BYTES = 64 * 2 ** 20
VMEM_LIMIT = V7X_VMEM_BYTES - 8 * 2 ** 20
MESH = pl.DeviceIdType.MESH
ANY = pl.BlockSpec(memory_space=pl.ANY)
SMALL_ROWS = 24

_SLOPES = [float(v) for v in np.power(np.float32(2.0), -8.0 * np.arange(1, N_Q_HEADS + 1, dtype=np.float32) / N_Q_HEADS)]


def _params(*sem):
    return pltpu.CompilerParams(dimension_semantics=sem, vmem_limit_bytes=VMEM_LIMIT)


class _Hosted:
    def __init__(self, inputs, out_shape, aliases, scratch, start, finish):
        self.inputs, self.out_shape, self.aliases, self.scratch = list(inputs), list(out_shape), dict(aliases), list(scratch)
        self.start, self.finish = start, finish


def _hosted_call(comm, kern, *, out_shape, grid, in_specs, out_specs, args, name, sem, scratch_shapes=()):
    single = not isinstance(out_shape, (tuple, list))
    outs = [out_shape] if single else list(out_shape)
    ospecs = [out_specs] if single else list(out_specs)
    if comm is None:
        res = pl.pallas_call(kern, out_shape=outs, grid=grid, in_specs=list(in_specs), out_specs=ospecs,
                             scratch_shapes=list(scratch_shapes), name=name, compiler_params=_params(*sem))(*args)
        return res[0] if single else res
    n_in, n_out, n_scr = len(args), len(outs), len(scratch_shapes)
    ci, co, cs = len(comm.inputs), len(comm.out_shape), len(comm.scratch)

    def body(*refs):
        cuts = np.cumsum([0, n_in, ci, n_out, co, n_scr, cs])
        a, b, c, d, e, f = [refs[lo:hi] for lo, hi in zip(cuts[:-1], cuts[1:])]
        ids = [pl.program_id(k) for k in range(len(grid))]
        first = functools.reduce(jnp.logical_and, [i == 0 for i in ids])
        last = functools.reduce(jnp.logical_and, [i == n - 1 for i, n in zip(ids, grid)])
        pl.when(first)(lambda: comm.start(b, d, f))
        kern(*a, *c, *e)
        pl.when(last)(lambda: comm.finish(b, d, f))

    res = pl.pallas_call(
        body, out_shape=outs + comm.out_shape, grid=grid, in_specs=list(in_specs) + [ANY] * ci, out_specs=ospecs + [ANY] * co,
        scratch_shapes=list(scratch_shapes) + comm.scratch,
        input_output_aliases={n_in + i: n_out + o for i, o in comm.aliases.items()},
        name=name + "_carrier", compiler_params=_params(*(["arbitrary"] * len(grid))))(*args, *comm.inputs)
    main = res[:n_out]
    return (main[0] if single else main), res[n_out:]


def _nt(a, b):
    return lax.dot_general(a, b, (((1,), (1,)), ((), ())), preferred_element_type=F32)


def _tn(a, b):
    return lax.dot_general(a, b, (((0,), (0,)), ((), ())), preferred_element_type=F32)


def _nn(a, b):
    return jnp.dot(a, b, preferred_element_type=F32)


def _rms_stats(xf):
    r = lax.rsqrt(jnp.mean(xf * xf, axis=-1, keepdims=True) + RMS_EPS)
    return r, xf * r


def _rms_bwd(dh, xh, r, g):
    dxh = dh * g
    dx = r * (dxh - xh * jnp.mean(dxh * xh, axis=-1, keepdims=True))
    dg = jnp.sum(dh * xh, axis=0, keepdims=True)
    return dx, dg


def _resident(shape):
    return pl.BlockSpec(shape, lambda *_: (0,) * len(shape), pipeline_mode=pl.Buffered(1))


def _norm_proj(x, g, w, comm=None):
    T, D = x.shape
    tm = min(256, T)
    widths = (QKV_W, CONV_W, GATE_W)

    def kern(x_ref, g_ref, w_ref, ht_ref, *o_refs):
        _, xh = _rms_stats(x_ref[...])
        h = (xh * g_ref[...]).astype(BF16)
        ht_ref[...] = h.T
        off = 0
        for o_ref, wd in zip(o_refs, widths):
            o_ref[...] = _nn(h, w_ref[:, off:off + wd]).astype(BF16)
            off += wd

    row = lambda wd: pl.BlockSpec((tm, wd), lambda i: (i, 0))
    return _hosted_call(
        comm, kern,
        out_shape=[jax.ShapeDtypeStruct((D, T), BF16)] + [jax.ShapeDtypeStruct((T, wd), BF16) for wd in widths],
        grid=(T // tm,), in_specs=[row(D), pl.BlockSpec((1, D), lambda i: (0, 0)), _resident((D, IN_COLS))],
        out_specs=[pl.BlockSpec((D, tm), lambda i: (0, i))] + [row(wd) for wd in widths],
        name="norm_proj", sem=("parallel",), args=(x, g, w))


GW = GQA_GROUP * WINDOW
BAND = 2 * WINDOW
KV_W = N_KV_HEADS * HEAD_DIM


def _attn_bias_table():
    jj = np.arange(BAND)[:, None]
    col = np.arange(GW)[None, :]
    dist = WINDOW + (col % WINDOW) - jj
    valid = (dist >= 0) & (dist < WINDOW)
    slopes = np.asarray(_SLOPES, np.float32).reshape(N_KV_HEADS, GQA_GROUP)
    tab = np.empty((2, N_KV_HEADS, BAND, GW), np.float32)
    for hk in range(N_KV_HEADS):
        bias = -slopes[hk][col // WINDOW] * dist.astype(np.float32)
        tab[0, hk] = np.where(valid, bias, np.float32(NEG_INF))
        tab[1, hk] = np.where(valid & (jj >= WINDOW), bias, np.float32(NEG_INF))
    return jnp.asarray(tab)


def _stack_heads(ref, hk):
    return jnp.concatenate(
        [ref[:, HEAD_DIM * (GQA_GROUP * hk + g): HEAD_DIM * (GQA_GROUP * hk + g + 1)] for g in range(GQA_GROUP)], axis=0)


def _kv_band(cur_ref, prev_ref, hk):
    k0 = N_Q_HEADS * HEAD_DIM
    sl = slice(HEAD_DIM * hk, HEAD_DIM * (hk + 1))
    ksl, vsl = slice(k0 + sl.start, k0 + sl.stop), slice(k0 + KV_W + sl.start, k0 + KV_W + sl.stop)
    k_band = jnp.concatenate([prev_ref[:, sl], cur_ref[:, ksl]], axis=0)
    v_band = jnp.concatenate([prev_ref[:, KV_W + sl.start:KV_W + sl.stop], cur_ref[:, vsl]], axis=0)
    return k_band, v_band


def _lane_row(vals):
    return jnp.concatenate([jnp.full((1, WINDOW), v, F32) for v in vals], axis=1)


def _attn_fwd(pqkv, sinks, bias, seq, comm=None):
    T = pqkv.shape[0]
    nblk = seq // WINDOW

    def kern(sink_ref, cur_ref, prev_ref, bias_ref, o_ref, ot_ref, lse_ref):
        i = pl.program_id(0)
        first_i = ((i % nblk) == 0).astype(jnp.int32)
        bands = [_kv_band(cur_ref, prev_ref, hk) for hk in range(N_KV_HEADS)]
        sts = [_nt(bands[hk][0], _stack_heads(cur_ref, hk) * ATTN_SCALE) + bias_ref[first_i, hk] for hk in range(N_KV_HEADS)]
        ps, scales = [], []
        for hk in range(N_KV_HEADS):
            heads = [GQA_GROUP * hk + g for g in range(GQA_GROUP)]
            sink = _lane_row([sink_ref[h] for h in heads])
            m = jnp.maximum(jnp.max(sts[hk], axis=0, keepdims=True), sink)
            p = jnp.exp(sts[hk] - m)
            den = jnp.sum(p, axis=0, keepdims=True) + jnp.exp(sink - m)
            lse = m + jnp.log(den)
            for g, h in enumerate(heads):
                lse_ref[h:h + 1, :] = lse[:, WINDOW * g:WINDOW * (g + 1)]
            ps.append(p.astype(BF16))
            scales.append(1.0 / den)
        for hk in range(N_KV_HEADS):
            ot = _tn(bands[hk][1], ps[hk]) * scales[hk]
            for g in range(GQA_GROUP):
                h = GQA_GROUP * hk + g
                ot_ref[HEAD_DIM * h:HEAD_DIM * (h + 1), :] = ot[:, WINDOW * g:WINDOW * (g + 1)].astype(BF16)
        o_ref[...] = ot_ref[...].T

    return _hosted_call(
        comm, kern,
        out_shape=(jax.ShapeDtypeStruct((T, D_MODEL), BF16), jax.ShapeDtypeStruct((D_MODEL, T), BF16),
                   jax.ShapeDtypeStruct((N_Q_HEADS, T), F32)),
        grid=(T // WINDOW,),
        in_specs=[pl.BlockSpec(memory_space=pltpu.SMEM),
                  pl.BlockSpec((WINDOW, QKV_W), lambda i: (i, 0)),
                  pl.BlockSpec((WINDOW, 2 * KV_W), lambda i: (jnp.maximum(i - 1, 0), 2)),
                  _resident(bias.shape)],
        out_specs=(pl.BlockSpec((WINDOW, D_MODEL), lambda i: (i, 0)), pl.BlockSpec((D_MODEL, WINDOW), lambda i: (0, i)),
                   pl.BlockSpec((N_Q_HEADS, WINDOW), lambda i: (0, i))),
        name="attn_fwd", sem=("parallel",), args=(sinks, pqkv, pqkv, bias))


def _pick_row(a, row):
    rid = lax.broadcasted_iota(jnp.int32, a.shape, 0)
    return jnp.sum(jnp.where(rid == row, a, 0.0), axis=0, keepdims=True)


def _conv_taps(yc, halo_yc, first_i):
    keep = (1 - first_i).astype(F32)
    p1 = _pick_row(halo_yc, 15) * keep
    p2 = _pick_row(halo_yc, 14) * keep
    rowid = lax.broadcasted_iota(jnp.int32, yc.shape, 0)
    s1 = jnp.where(rowid == 0, p1, pltpu.roll(yc, 1, 0))
    s2 = jnp.where(rowid == 0, p2, jnp.where(rowid == 1, p1, pltpu.roll(yc, 2, 0)))
    return s1, s2


def _conv_fwd(pconv, conv_w, conv_b, seq):
    T = pconv.shape[0]
    tm = min(256, seq)
    per_seq = seq // tm
    D = D_MODEL

    def kern(cur_ref, halo_ref, w_ref, b_ref, o_ref, ot_ref):
        i = pl.program_id(0)
        first_i = ((i % per_seq) == 0).astype(jnp.int32)
        cb = cur_ref[:, 0:D].astype(F32)
        yc = cur_ref[:, D:2 * D].astype(F32) * cur_ref[:, 2 * D:3 * D].astype(F32)
        halo_yc = halo_ref[:, D:2 * D].astype(F32) * halo_ref[:, 2 * D:3 * D].astype(F32)
        s1, s2 = _conv_taps(yc, halo_yc, first_i)
        z = w_ref[0:1, :] * s2 + w_ref[1:2, :] * s1 + w_ref[2:3, :] * yc
        cv = (cb * (z + b_ref[...])).astype(BF16)
        o_ref[...] = cv
        ot_ref[...] = cv.T

    return pl.pallas_call(
        kern, out_shape=(jax.ShapeDtypeStruct((T, D), BF16), jax.ShapeDtypeStruct((D, T), BF16)), grid=(T // tm,),
        in_specs=[pl.BlockSpec((tm, CONV_W), lambda i: (i, 0)),
                  pl.BlockSpec((16, CONV_W), lambda i: (jnp.maximum(i * (tm // 16) - 1, 0), 0)),
                  pl.BlockSpec((3, D), lambda i: (0, 0)), pl.BlockSpec((1, D), lambda i: (0, 0))],
        out_specs=(pl.BlockSpec((tm, D), lambda i: (i, 0)), pl.BlockSpec((D, tm), lambda i: (0, i))),
        name="conv_fwd", compiler_params=_params("parallel"))(pconv, pconv, conv_w, conv_b)


def _mix_fwd(x, att, cv, pgate, b_gates, wao, wco, wo):
    T, D = x.shape
    tm = min(512, T)

    def kern(x_ref, att_ref, cv_ref, pg_ref, bg_ref, wao_ref, wco_ref, wo_ref, x1_ref, ya_ref, yc_ref, mgt_ref):
        ya = _nn(att_ref[...], wao_ref[...])
        yc = _nn(cv_ref[...], wco_ref[...])
        sa = jax.nn.sigmoid(pg_ref[:, 0:D].astype(F32) + bg_ref[:, 0:D])
        sc = jax.nn.sigmoid(pg_ref[:, D:2 * D].astype(F32) + bg_ref[:, D:2 * D])
        mg = (sa * ya + sc * yc).astype(BF16)
        ya_ref[...] = ya.astype(BF16)
        yc_ref[...] = yc.astype(BF16)
        mgt_ref[...] = mg.T
        x1_ref[...] = x_ref[...] + _nn(mg, wo_ref[...])

    row = lambda w: pl.BlockSpec((tm, w), lambda i: (i, 0))
    full = lambda a, b: pl.BlockSpec((a, b), lambda i: (0, 0))
    bf = jax.ShapeDtypeStruct((T, D), BF16)
    return pl.pallas_call(
        kern, out_shape=(jax.ShapeDtypeStruct((T, D), F32), bf, bf, jax.ShapeDtypeStruct((D, T), BF16)), grid=(T // tm,),
        in_specs=[row(D), row(D), row(D), row(GATE_W), full(1, GATE_W)] + [_resident((D, D))] * 3,
        out_specs=(row(D), row(D), row(D), pl.BlockSpec((D, tm), lambda i: (0, i))),
        name="mix_fwd", compiler_params=_params("parallel"))(x, att, cv, pgate, b_gates, wao, wco, wo)


def _mlp_fwd(x1, g, wup, wdn, comm=None):
    T, D = x1.shape
    tm = min(1024, T)
    nj = D_FF // D

    def kern(x_ref, g_ref, wup_ref, wdn_ref, x2_ref, a_ref, h_scr, acc_scr):
        j = pl.program_id(1)

        @pl.when(j == 0)
        def _():
            xf = x_ref[...]
            _, xh = _rms_stats(xf)
            h_scr[...] = (xh * g_ref[...]).astype(BF16)
            acc_scr[...] = xf

        a = _nn(h_scr[...], wup_ref[...])
        a_ref[...] = a.astype(BF16)
        u = jnp.square(jnp.maximum(a, 0.0)).astype(BF16)
        acc_scr[...] += _nn(u, wdn_ref[...])

        @pl.when(j == nj - 1)
        def _():
            x2_ref[...] = acc_scr[...]

    return _hosted_call(
        comm, kern, out_shape=(jax.ShapeDtypeStruct((T, D), F32), jax.ShapeDtypeStruct((T, D_FF), BF16)), grid=(T // tm, nj),
        in_specs=[pl.BlockSpec((tm, D), lambda i, j: (i, 0)), pl.BlockSpec((1, D), lambda i, j: (0, 0)),
                  pl.BlockSpec((D, D), lambda i, j: (0, j)), pl.BlockSpec((D, D), lambda i, j: (j, 0))],
        out_specs=(pl.BlockSpec((tm, D), lambda i, j: (i, 0)), pl.BlockSpec((tm, D), lambda i, j: (i, j))),
        scratch_shapes=[pltpu.VMEM((tm, D), BF16), pltpu.VMEM((tm, D), F32)],
        name="mlp_fwd", sem=("parallel", "arbitrary"), args=(x1, g, wup, wdn))


def _loss_bwd(x, g, tgt):
    T, D = x.shape
    tm = min(512, T)

    def kern(x_ref, g_ref, t_ref, st_ref, dx_ref):
        i = pl.program_id(0)

        @pl.when(i == 0)
        def _():
            st_ref[...] = jnp.zeros_like(st_ref)

        gg = g_ref[...]
        r, xh = _rms_stats(x_ref[...])
        e = xh * gg - t_ref[...]
        part = 0.5 * jnp.sum(jnp.mean(e * e, axis=-1, keepdims=True), axis=0, keepdims=True)
        dx, dg = _rms_bwd(e * (1.0 / D), xh, r, gg)
        dx_ref[...] = dx
        st_ref[0:1, :] += dg
        st_ref[1:2, 0:1] += part

    return pl.pallas_call(
        kern, out_shape=(jax.ShapeDtypeStruct((8, D), F32), jax.ShapeDtypeStruct((T, D), F32)), grid=(T // tm,),
        in_specs=[pl.BlockSpec((tm, D), lambda i: (i, 0)), pl.BlockSpec((1, D), lambda i: (0, 0)),
                  pl.BlockSpec((tm, D), lambda i: (i, 0))],
        out_specs=(pl.BlockSpec((8, D), lambda i: (0, 0)), pl.BlockSpec((tm, D), lambda i: (i, 0))),
        name="loss_bwd", compiler_params=_params("arbitrary"))(x, g, tgt)


def _mlp_bwd(dx2, x1, a, g, wup, wdn, comm=None):
    T, D = x1.shape
    tm = min(512, T)
    nj = D_FF // D

    def kern(dx2_ref, x1_ref, a_ref, g_ref, wup_ref, wdn_ref, dx1_ref, da_ref, ut_ref, h2t_ref, dg_ref, dyb_scr, acc_scr):
        i, j = pl.program_id(0), pl.program_id(1)

        @pl.when((i == 0) & (j == 0))
        def _():
            dg_ref[...] = jnp.zeros_like(dg_ref)

        @pl.when(j == 0)
        def _():
            dyb_scr[...] = dx2_ref[...].astype(BF16)
            acc_scr[...] = jnp.zeros_like(acc_scr)

        du = _nt(dyb_scr[...], wdn_ref[...])
        relu = jnp.maximum(a_ref[...].astype(F32), 0.0)
        da = (du * (2.0 * relu)).astype(BF16)
        da_ref[...] = da
        ut_ref[...] = jnp.square(relu).astype(BF16).T
        acc_scr[...] += _nt(da, wup_ref[...])

        @pl.when(j == nj - 1)
        def _():
            gg = g_ref[...]
            r, xh = _rms_stats(x1_ref[...])
            h2t_ref[...] = (xh * gg).astype(BF16).T
            dx, dg = _rms_bwd(acc_scr[...], xh, r, gg)
            dx1_ref[...] = dx2_ref[...] + dx
            dg_ref[...] += dg

    return _hosted_call(
        comm, kern,
        out_shape=(jax.ShapeDtypeStruct((T, D), F32), jax.ShapeDtypeStruct((T, D_FF), BF16),
                   jax.ShapeDtypeStruct((D_FF, T), BF16), jax.ShapeDtypeStruct((D, T), BF16), jax.ShapeDtypeStruct((1, D), F32)),
        grid=(T // tm, nj),
        in_specs=[pl.BlockSpec((tm, D), lambda i, j: (i, 0)), pl.BlockSpec((tm, D), lambda i, j: (i, 0)),
                  pl.BlockSpec((tm, D), lambda i, j: (i, j)), pl.BlockSpec((1, D), lambda i, j: (0, 0)),
                  pl.BlockSpec((D, D), lambda i, j: (0, j)), pl.BlockSpec((D, D), lambda i, j: (j, 0))],
        out_specs=(pl.BlockSpec((tm, D), lambda i, j: (i, 0)), pl.BlockSpec((tm, D), lambda i, j: (i, j)),
                   pl.BlockSpec((D, tm), lambda i, j: (j, i)), pl.BlockSpec((D, tm), lambda i, j: (0, i)),
                   pl.BlockSpec((1, D), lambda i, j: (0, 0))),
        scratch_shapes=[pltpu.VMEM((tm, D), BF16), pltpu.VMEM((tm, D), F32)],
        name="mlp_bwd", sem=("arbitrary", "arbitrary"), args=(dx2, x1, a, g, wup, wdn))


def _mix_bwd(dx1, ya, yc, pgate, b_gates, wao, wco, wo):
    T, D = dx1.shape
    tm = min(512, T)

    def kern(dx_ref, ya_ref, yc_ref, pg_ref, bg_ref, wao_ref, wco_ref, wo_ref,
             datt_ref, dcv_ref, dya_ref, dyc_ref, dgt_ref, dbg_ref):
        @pl.when(pl.program_id(0) == 0)
        def _():
            dbg_ref[...] = jnp.zeros_like(dbg_ref)

        dm = _nt(dx_ref[...].astype(BF16), wo_ref[...])
        sa = jax.nn.sigmoid(pg_ref[:, 0:D].astype(F32) + bg_ref[:, 0:D])
        sc = jax.nn.sigmoid(pg_ref[:, D:2 * D].astype(F32) + bg_ref[:, D:2 * D])
        dya = (dm * sa).astype(BF16)
        dyc = (dm * sc).astype(BF16)
        dga = dm * ya_ref[...].astype(F32) * (sa * (1.0 - sa))
        dgc = dm * yc_ref[...].astype(F32) * (sc * (1.0 - sc))
        dya_ref[...] = dya
        dyc_ref[...] = dyc
        dgt_ref[:, 0:D] = dga.astype(BF16)
        dgt_ref[:, D:2 * D] = dgc.astype(BF16)
        dbg_ref[:, 0:D] += jnp.sum(dga, axis=0, keepdims=True)
        dbg_ref[:, D:2 * D] += jnp.sum(dgc, axis=0, keepdims=True)
        datt_ref[...] = _nt(dya, wao_ref[...]).astype(BF16)
        dcv_ref[...] = _nt(dyc, wco_ref[...]).astype(BF16)

    row = lambda w: pl.BlockSpec((tm, w), lambda i: (i, 0))
    full = lambda a, b: pl.BlockSpec((a, b), lambda i: (0, 0))
    bf = jax.ShapeDtypeStruct((T, D), BF16)
    return pl.pallas_call(
        kern,
        out_shape=(bf, bf, bf, bf, jax.ShapeDtypeStruct((T, GATE_W), BF16), jax.ShapeDtypeStruct((1, GATE_W), F32)),
        grid=(T // tm,),
        in_specs=[row(D), row(D), row(D), row(GATE_W), full(1, GATE_W)] + [_resident((D, D))] * 3,
        out_specs=(row(D), row(D), row(D), row(D), row(GATE_W), full(1, GATE_W)),
        name="mix_bwd", compiler_params=_params("arbitrary"))(dx1, ya, yc, pgate, b_gates, wao, wco, wo)


def _conv_bwd(dcv, pconv, conv_w, conv_b, seq):
    T = pconv.shape[0]
    tm = min(256, seq)
    per_seq = seq // tm
    D = D_MODEL
    nb16 = T // 16

    def kern(dcv_ref, dcvn_ref, cur_ref, prev_ref, next_ref, w_ref, b_ref, o_ref, dwb_ref):
        i = pl.program_id(0)

        @pl.when(i == 0)
        def _():
            dwb_ref[...] = jnp.zeros_like(dwb_ref)

        first_i = ((i % per_seq) == 0).astype(jnp.int32)
        keep_next = 1.0 - (((i + 1) % per_seq) == 0).astype(F32)
        cb = cur_ref[:, 0:D].astype(F32)
        cc = cur_ref[:, D:2 * D].astype(F32)
        cu = cur_ref[:, 2 * D:3 * D].astype(F32)
        yc = cc * cu
        halo_yc = prev_ref[:, D:2 * D].astype(F32) * prev_ref[:, 2 * D:3 * D].astype(F32)
        s1, s2 = _conv_taps(yc, halo_yc, first_i)
        w0, w1, w2 = w_ref[0:1, :], w_ref[1:2, :], w_ref[2:3, :]
        z = w0 * s2 + w1 * s1 + w2 * yc
        dcv = dcv_ref[...].astype(F32)
        dz = dcv * cb
        dzn = dcvn_ref[...].astype(F32) * next_ref[:, 0:D].astype(F32) * keep_next
        n1, n2 = _pick_row(dzn, 0), _pick_row(dzn, 1)
        rowid = lax.broadcasted_iota(jnp.int32, dz.shape, 0)
        u1 = jnp.where(rowid == tm - 1, n1, pltpu.roll(dz, tm - 1, 0))
        u2 = jnp.where(rowid == tm - 1, n2, jnp.where(rowid == tm - 2, n1, pltpu.roll(dz, tm - 2, 0)))
        dyc = w2 * dz + w1 * u1 + w0 * u2
        o_ref[:, 0:D] = (dcv * (z + b_ref[...])).astype(BF16)
        o_ref[:, D:2 * D] = (dyc * cu).astype(BF16)
        o_ref[:, 2 * D:3 * D] = (dyc * cc).astype(BF16)
        dwb_ref[0:1, :] += jnp.sum(dz * s2, axis=0, keepdims=True)
        dwb_ref[1:2, :] += jnp.sum(dz * s1, axis=0, keepdims=True)
        dwb_ref[2:3, :] += jnp.sum(dz * yc, axis=0, keepdims=True)
        dwb_ref[3:4, :] += jnp.sum(dz, axis=0, keepdims=True)

    prev_map = lambda i: (jnp.maximum(i * (tm // 16) - 1, 0), 0)
    next_map = lambda i: (jnp.minimum((i + 1) * (tm // 16), nb16 - 1), 0)
    return pl.pallas_call(
        kern, out_shape=(jax.ShapeDtypeStruct((T, CONV_W), BF16), jax.ShapeDtypeStruct((8, D), F32)), grid=(T // tm,),
        in_specs=[pl.BlockSpec((tm, D), lambda i: (i, 0)), pl.BlockSpec((16, D), next_map),
                  pl.BlockSpec((tm, CONV_W), lambda i: (i, 0)), pl.BlockSpec((16, CONV_W), prev_map),
                  pl.BlockSpec((16, CONV_W), next_map),
                  pl.BlockSpec((3, D), lambda i: (0, 0)), pl.BlockSpec((1, D), lambda i: (0, 0))],
        out_specs=(pl.BlockSpec((tm, CONV_W), lambda i: (i, 0)), pl.BlockSpec((8, D), lambda i: (0, 0))),
        name="conv_bwd", compiler_params=_params("arbitrary"))(dcv, dcv, pconv, pconv, pconv, conv_w, conv_b)


def _attn_bwd(pqkv, datt, lse, sinks, bias, seq, comm=None):
    T = pqkv.shape[0]
    nblk = seq // WINDOW
    nseq = T // seq
    KVW = KV_W

    def kern(sink_ref, cur_ref, prev_ref, do_ref, lse_ref, bias_ref, dq_ref, dkv_ref, ds_ref, kc_scr, vc_scr, dqt_scr):
        b, st = pl.program_id(0), pl.program_id(1)

        @pl.when((b == 0) & (st == 0))
        def _():
            ds_ref[...] = jnp.zeros_like(ds_ref)

        @pl.when(st == 0)
        def _():
            kc_scr[...] = jnp.zeros_like(kc_scr)
            vc_scr[...] = jnp.zeros_like(vc_scr)

        @pl.when(st < nblk)
        def _():
            first_i = (st == 0).astype(jnp.int32)
            groups = range(N_KV_HEADS)
            bands = [_kv_band(cur_ref, prev_ref, hk) for hk in groups]
            qs = [_stack_heads(cur_ref, hk) for hk in groups]
            dos = [_stack_heads(do_ref, hk) for hk in groups]
            sts = [_nt(bands[hk][0], qs[hk] * ATTN_SCALE) + bias_ref[first_i, hk] for hk in groups]
            dps = [_nt(bands[hk][1], dos[hk]) for hk in groups]
            pbs, dsss = [], []
            for hk in groups:
                heads = [GQA_GROUP * hk + g for g in range(GQA_GROUP)]
                sink = _lane_row([sink_ref[h] for h in heads])
                lse_g = jnp.concatenate([lse_ref[h:h + 1, :] for h in heads], axis=1)
                p = jnp.exp(sts[hk] - lse_g)
                d_row = jnp.sum(p * dps[hk], axis=0, keepdims=True)
                dsss.append((p * (dps[hk] - d_row) * ATTN_SCALE).astype(BF16))
                pbs.append(p.astype(BF16))
                psd = jnp.exp(sink - lse_g) * d_row
                for g, h in enumerate(heads):
                    ds_ref[0:1, h:h + 1] -= jnp.sum(psd[:, WINDOW * g:WINDOW * (g + 1)], axis=1, keepdims=True)
            for hk in groups:
                dqt = _tn(bands[hk][0], dsss[hk])
                dk_b = _nn(dsss[hk], qs[hk])
                dv_b = _nn(pbs[hk], dos[hk])
                for g in range(GQA_GROUP):
                    h = GQA_GROUP * hk + g
                    dqt_scr[HEAD_DIM * h:HEAD_DIM * (h + 1), :] = dqt[:, WINDOW * g:WINDOW * (g + 1)].astype(BF16)
                ksl = slice(HEAD_DIM * hk, HEAD_DIM * (hk + 1))
                vsl = slice(KVW + HEAD_DIM * hk, KVW + HEAD_DIM * (hk + 1))
                dkv_ref[:, ksl] = (kc_scr[:, ksl] + dk_b[0:WINDOW]).astype(BF16)
                dkv_ref[:, vsl] = (vc_scr[:, ksl] + dv_b[0:WINDOW]).astype(BF16)
                kc_scr[:, ksl] = dk_b[WINDOW:2 * WINDOW]
                vc_scr[:, ksl] = dv_b[WINDOW:2 * WINDOW]
            dq_ref[...] = dqt_scr[...].T

        @pl.when(st == nblk)
        def _():
            dkv_ref[:, 0:KVW] = kc_scr[...].astype(BF16)
            dkv_ref[:, KVW:2 * KVW] = vc_scr[...].astype(BF16)

    cur_map = lambda b, s: (b * nblk + jnp.minimum(s, nblk - 1), 0)
    prev_row = lambda b, s: b * nblk + jnp.clip(s - 1, 0, nblk - 1)
    return _hosted_call(
        comm, kern,
        out_shape=(jax.ShapeDtypeStruct((T, D_MODEL), BF16), jax.ShapeDtypeStruct((T, 2 * KVW), BF16),
                   jax.ShapeDtypeStruct((8, 128), F32)),
        grid=(nseq, nblk + 1),
        in_specs=[pl.BlockSpec(memory_space=pltpu.SMEM),
                  pl.BlockSpec((WINDOW, QKV_W), cur_map),
                  pl.BlockSpec((WINDOW, 2 * KVW), lambda b, s: (prev_row(b, s), 2)),
                  pl.BlockSpec((WINDOW, D_MODEL), cur_map),
                  pl.BlockSpec((N_Q_HEADS, WINDOW), lambda b, s: (0, b * nblk + jnp.minimum(s, nblk - 1))),
                  _resident(bias.shape)],
        out_specs=(pl.BlockSpec((WINDOW, D_MODEL), cur_map),
                   pl.BlockSpec((WINDOW, 2 * KVW), lambda b, s: (prev_row(b, s), 0)),
                   pl.BlockSpec((8, 128), lambda b, s: (0, 0))),
        scratch_shapes=[pltpu.VMEM((WINDOW, KVW), F32), pltpu.VMEM((WINDOW, KVW), F32), pltpu.VMEM((D_MODEL, WINDOW), BF16)],
        name="attn_bwd", sem=("arbitrary", "arbitrary"), args=(sinks, pqkv, pqkv, datt, lse, bias))


def _piece_tiles(pieces):
    out, start = [], 0
    for arr, width in pieces:
        out.append((arr, start, width // COL_TILE))
        start += width // COL_TILE
    return out, start


def _inproj_bwd(pieces, w_in, x, dx_in, g, comm=None):
    T, D = x.shape
    tm = min(256, T)

    def kern(*refs):
        p_refs = refs[:len(pieces)]
        w_ref, x_ref, dxin_ref, g_ref, dx_ref, dg_ref = refs[len(pieces):]

        @pl.when(pl.program_id(0) == 0)
        def _():
            dg_ref[...] = jnp.zeros_like(dg_ref)

        dh, off = None, 0
        for p_ref, (_, width) in zip(p_refs, pieces):
            part = _nt(p_ref[...], w_ref[:, off:off + width])
            dh = part if dh is None else dh + part
            off += width
        gg = g_ref[...]
        r, xh = _rms_stats(x_ref[...])
        dx, dg = _rms_bwd(dh, xh, r, gg)
        dx_ref[...] = dxin_ref[...] + dx
        dg_ref[...] += dg

    row = lambda wd: pl.BlockSpec((tm, wd), lambda i: (i, 0))
    return _hosted_call(
        comm, kern, out_shape=(jax.ShapeDtypeStruct((T, D), F32), jax.ShapeDtypeStruct((1, D), F32)), grid=(T // tm,),
        in_specs=[row(wd) for _, wd in pieces] + [_resident((D, IN_COLS)), row(D), row(D), pl.BlockSpec((1, D), lambda i: (0, 0))],
        out_specs=(row(D), pl.BlockSpec((1, D), lambda i: (0, 0))),
        name="inproj_bwd", sem=("arbitrary",), args=(*[a for a, _ in pieces], w_in, x, dx_in, g))


def _dw_pieces(lhs_t, pieces):
    K, T = lhs_t.shape
    tk = min(2048, T)
    nt = T // tk
    tiles, nj = _piece_tiles(pieces)

    def kern(*refs):
        lhs_ref = refs[0]
        p_refs = refs[1:1 + len(tiles)]
        o_ref, ob_ref = refs[1 + len(tiles):]
        j, t = pl.program_id(0), pl.program_id(1)

        @pl.when(t == 0)
        def _():
            o_ref[...] = jnp.zeros_like(o_ref)

        for p_ref, (_, start, n) in zip(p_refs, tiles):
            @pl.when((j >= start) & (j < start + n))
            def _(p_ref=p_ref):
                o_ref[...] += _nn(lhs_ref[...], p_ref[...])

        @pl.when(t == nt - 1)
        def _():
            ob_ref[...] = o_ref[...].astype(BF16)

    def p_map(start, n):
        return lambda j, t: (jnp.where((j >= start) & (j < start + n), t, 0), jnp.clip(j - start, 0, n - 1))

    N = nj * COL_TILE
    return pl.pallas_call(
        kern, out_shape=(jax.ShapeDtypeStruct((K, N), F32), jax.ShapeDtypeStruct((K, N), BF16)), grid=(nj, nt),
        in_specs=[pl.BlockSpec((K, tk), lambda j, t: (0, t))] + [pl.BlockSpec((tk, COL_TILE), p_map(s, n)) for _, s, n in tiles],
        out_specs=(pl.BlockSpec((K, COL_TILE), lambda j, t: (0, j)), pl.BlockSpec((K, COL_TILE), lambda j, t: (0, j))),
        name="dw_pieces", compiler_params=_params("arbitrary", "arbitrary"))(lhs_t, *[a for a, _, _ in tiles])


def _dw(lhs_t, rhs):
    K, T = lhs_t.shape
    N = rhs.shape[1]
    tk = min(2048, T)
    nt = T // tk
    W = D_MODEL

    def kern(lhs_ref, rhs_ref, o_ref, ob_ref):
        t = pl.program_id(2)

        @pl.when(t == 0)
        def _():
            o_ref[...] = jnp.zeros_like(o_ref)

        o_ref[...] += _nn(lhs_ref[...], rhs_ref[...].astype(BF16))

        @pl.when(t == nt - 1)
        def _():
            ob_ref[...] = o_ref[...].astype(BF16)

    omap = lambda i, j, t: (i, j)
    return pl.pallas_call(
        kern, out_shape=(jax.ShapeDtypeStruct((K, N), F32), jax.ShapeDtypeStruct((K, N), BF16)), grid=(K // W, N // W, nt),
        in_specs=[pl.BlockSpec((W, tk), lambda i, j, t: (i, t)), pl.BlockSpec((tk, W), lambda i, j, t: (t, j))],
        out_specs=(pl.BlockSpec((W, W), omap), pl.BlockSpec((W, W), omap)),
        name="dw", compiler_params=_params("arbitrary", "arbitrary", "arbitrary"))(lhs_t, rhs)


BIG = (("w_in", D_MODEL, IN_COLS, "col"), ("w_attn_out", D_MODEL, D_MODEL, "row"), ("w_conv_out", D_MODEL, D_MODEL, "row"),
       ("w_o", D_MODEL, D_MODEL, "row"), ("w_up", D_MODEL, D_FF, "col"), ("w_down", D_FF, D_MODEL, "row"))


def _shard_dims(rows, cols, kind):
    return (rows, cols // N_CHIP) if kind == "col" else (rows // N_CHIP, cols)


def _window(ref, rows, cols, kind, chip, half):
    sr, sc = _shard_dims(rows, cols, kind)
    hr = sr // 2
    if kind == "col":
        return ref.at[pl.ds(half * hr, hr), pl.ds(chip * sc, sc)]
    return ref.at[pl.ds(chip * sr + half * hr, hr), :]


def _mesh_pos():
    x, y, c = lax.axis_index("x"), lax.axis_index("y"), lax.axis_index("c")
    return x, y, c, 2 * x + y


_REL_BITS = (2, 1, 3)


def _rel_dev(x, y, c, r):
    return ((1 - x, y, c), (x, 1 - y, c), (1 - x, 1 - y, c))[r]


def _for_my_chip(j, fn):
    for js in range(N_CHIP):
        pl.when(j == js)(functools.partial(fn, js))


def _cast_into_full(j_arr, shard, l, rows, cols, kind):
    sr, sc = _shard_dims(rows, cols, kind)
    tr = min(256, sr)

    def kern(j_ref, s_ref, o_ref):
        o_ref[...] = s_ref[...].astype(BF16)

    if kind == "col":
        omap = lambda i, j_ref: (i, j_ref[0])
    else:
        omap = lambda i, j_ref: (j_ref[0] * (sr // tr) + i, 0)
    gs = pltpu.PrefetchScalarGridSpec(
        num_scalar_prefetch=1, grid=(sr // tr,),
        in_specs=[pl.BlockSpec((None, tr, sc), lambda i, j_ref: (l, i, 0))], out_specs=pl.BlockSpec((tr, sc), omap))
    return pl.pallas_call(kern, out_shape=jax.ShapeDtypeStruct((rows, cols), BF16), grid_spec=gs, name="cast_into_full",
                          compiler_params=_params("arbitrary"))(j_arr, shard)


def _gather(fulls, cw=None):
    comm = _gather_comm(fulls, cw)
    n = len(comm.inputs)

    def body(*refs):
        comm.start(refs[:n], refs[n:2 * n], refs[2 * n:])
        comm.finish(refs[:n], refs[n:2 * n], refs[2 * n:])

    return pl.pallas_call(
        body, out_shape=comm.out_shape, in_specs=[ANY] * n, out_specs=[ANY] * n, input_output_aliases=comm.aliases,
        scratch_shapes=comm.scratch, name="gather_weights")(*comm.inputs)


def _gather_comm(fulls, cw=None):
    n_big = len(fulls)
    n_piece = n_big + (0 if cw is None else 1)

    def pieces(in_refs, o_refs, js, c):
        def piece(p, chip, half):
            if p == n_big:
                return o_refs[p].at[half, chip]
            _, rows, cols, kind = fulls[p]
            return _window(o_refs[p], rows, cols, kind, chip, half)

        def mine(p):
            return in_refs[p].at[c] if p == n_big else piece(p, js, c)

        return piece, mine

    def local_copies(in_refs, piece, js, loc_sem):
        if cw is None:
            return []
        return [pltpu.make_async_copy(in_refs[n_big].at[half], piece(n_big, js, half), loc_sem.at[half]) for half in range(2)]

    def ici_copy(piece, mine, js, x, y, c, r, p, send_sem, recv_sem):
        return pltpu.make_async_remote_copy(mine(p), piece(p, js, c), send_sem.at[r * n_piece + p], recv_sem.at[r * n_piece + p],
                                            _rel_dev(x, y, c, r), MESH)

    def start(in_refs, o_refs, sems):
        send_sem, recv_sem, _, _, loc_sem = sems
        x, y, c, j = _mesh_pos()

        def run(js):
            piece, mine = pieces(in_refs, o_refs, js, c)
            for cp in local_copies(in_refs, piece, js, loc_sem):
                cp.start()
            for r in range(3):
                for p in range(n_piece):
                    ici_copy(piece, mine, js, x, y, c, r, p, send_sem, recv_sem).start()

        _for_my_chip(j, run)

    def finish(in_refs, o_refs, sems):
        send_sem, recv_sem, fsend_sem, frecv_sem, loc_sem = sems
        x, y, c, j = _mesh_pos()

        def run(js):
            piece, mine = pieces(in_refs, o_refs, js, c)
            fwds = []
            for r in range(3):
                ks = js ^ _REL_BITS[r]
                for p in range(n_piece):
                    got = piece(p, ks, c)
                    pltpu.make_async_remote_copy(got, got, send_sem.at[r * n_piece + p], recv_sem.at[r * n_piece + p],
                                                 _rel_dev(x, y, c, r), MESH).wait_recv()
                    cp = pltpu.make_async_remote_copy(got, got, fsend_sem.at[r * n_piece + p], frecv_sem.at[r * n_piece + p],
                                                      (x, y, 1 - c), MESH)
                    cp.start()
                    fwds.append(cp)
            for r in range(3):
                ks = js ^ _REL_BITS[r]
                for p in range(n_piece):
                    got = piece(p, ks, 1 - c)
                    pltpu.make_async_remote_copy(got, got, fsend_sem.at[r * n_piece + p], frecv_sem.at[r * n_piece + p],
                                                 (x, y, 1 - c), MESH).wait_recv()
            for r in range(3):
                for p in range(n_piece):
                    ici_copy(piece, mine, js, x, y, c, r, p, send_sem, recv_sem).wait_send()
            for cp in fwds:
                cp.wait_send()
            for cp in local_copies(in_refs, piece, js, loc_sem):
                cp.wait()

        _for_my_chip(j, run)

    out_shape = [jax.ShapeDtypeStruct(a.shape, BF16) for a, _, _, _ in fulls]
    ins = [a for a, _, _, _ in fulls]
    if cw is not None:
        out_shape.append(jax.ShapeDtypeStruct((DEPTH, N_CHIP, 3, D_MODEL // N_CHIP), F32))
        ins.append(cw)
    scratch = [pltpu.SemaphoreType.DMA((3 * n_piece,))] * 4 + [pltpu.SemaphoreType.DMA((2,))]
    return _Hosted(ins, out_shape, {p: p for p in range(n_big)}, scratch, start, finish)


def _sibling_exchange(gb):
    n = len(gb)

    def body(*refs):
        g_refs, o_refs = refs[:n], refs[n:2 * n]
        send_sem, recv_sem = refs[2 * n:]
        x, y, c, _ = _mesh_pos()
        cps = []
        for t in range(n):
            _, rows, cols, kind = gb[t]
            for chip in range(N_CHIP):
                out_w = _window(g_refs[t], rows, cols, kind, chip, 1 - c)
                dst_w = _window(o_refs[t], rows, cols, kind, chip, 1 - c)
                cp = pltpu.make_async_remote_copy(out_w, dst_w, send_sem.at[N_CHIP * t + chip], recv_sem.at[N_CHIP * t + chip],
                                                  (x, y, 1 - c), MESH)
                cp.start()
                cps.append(cp)
        for t in range(n):
            _, rows, cols, kind = gb[t]
            for chip in range(N_CHIP):
                w = _window(o_refs[t], rows, cols, kind, chip, c)
                pltpu.make_async_remote_copy(w, w, send_sem.at[N_CHIP * t + chip], recv_sem.at[N_CHIP * t + chip],
                                             (x, y, 1 - c), MESH).wait_recv()
        for cp in cps:
            cp.wait_send()

    return pl.pallas_call(
        body, out_shape=[jax.ShapeDtypeStruct(a.shape, BF16) for a, _, _, _ in gb], in_specs=[ANY] * n, out_specs=[ANY] * n,
        scratch_shapes=[pltpu.SemaphoreType.DMA((N_CHIP * n,))] * 2, name="grad_sibling_exchange")(*[a for a, _, _, _ in gb])


def _half_add(c_arr, g, sib, rows, cols, kind):
    sr, sc = _shard_dims(rows, cols, kind)
    hr = sr // 2

    def kern(c_ref, g_ref, s_ref, ob_ref, of_ref):
        v = g_ref[...] + s_ref[...].astype(F32)
        of_ref[...] = v
        ob_ref[...] = v.astype(BF16)

    if kind == "col":
        imap = lambda j, c_ref: (c_ref[0], j)
    else:
        imap = lambda j, c_ref: (2 * j + c_ref[0], 0)
    gs = pltpu.PrefetchScalarGridSpec(
        num_scalar_prefetch=1, grid=(N_CHIP,),
        in_specs=[pl.BlockSpec((hr, sc), imap), pl.BlockSpec((hr, sc), imap)],
        out_specs=[pl.BlockSpec((None, hr, sc), lambda j, c_ref: (j, 0, 0))] * 2)
    return pl.pallas_call(
        kern, out_shape=(jax.ShapeDtypeStruct((N_CHIP, hr, sc), BF16), jax.ShapeDtypeStruct((N_CHIP, hr, sc), F32)),
        grid_spec=gs, name="grad_half_add", compiler_params=_params("arbitrary"))(c_arr, g, sib)


def _chip_exchange(sbs):
    comm = _chip_exchange_comm(sbs)
    n = len(sbs)

    def body(*refs):
        comm.start(refs[:n], refs[n:2 * n], refs[2 * n:])
        comm.finish(refs[:n], refs[n:2 * n], refs[2 * n:])

    return pl.pallas_call(
        body, out_shape=comm.out_shape, in_specs=[ANY] * n, out_specs=[ANY] * n, scratch_shapes=comm.scratch,
        name="grad_chip_exchange")(*sbs)


def _chip_exchange_comm(sbs):
    n = len(sbs)

    def copies(s_refs, o_refs, sems):
        send_sem, recv_sem = sems
        x, y, c, j = _mesh_pos()
        return [pltpu.make_async_remote_copy(s_refs[t].at[j ^ _REL_BITS[r]], o_refs[t].at[r], send_sem.at[n * r + t],
                                             recv_sem.at[n * r + t], _rel_dev(x, y, c, r), MESH)
                for r in range(3) for t in range(n)]

    def start(s_refs, o_refs, sems):
        for cp in copies(s_refs, o_refs, sems):
            cp.start()

    def finish(s_refs, o_refs, sems):
        for cp in copies(s_refs, o_refs, sems):
            cp.wait()

    return _Hosted(sbs, [jax.ShapeDtypeStruct((3,) + a.shape[1:], BF16) for a in sbs], {},
                   [pltpu.SemaphoreType.DMA((3 * n,))] * 2, start, finish)


def _owner_sum(jc_arr, sf, rb, l, into=None):
    _, hr, sc = sf.shape

    def kern(jc_ref, s_ref, r0_ref, r1_ref, r2_ref, *rest):
        o_ref = rest[-1]
        o_ref[...] = ((s_ref[...] + r0_ref[...].astype(F32)) + r1_ref[...].astype(F32)) + r2_ref[...].astype(F32)

    in_specs = [pl.BlockSpec((None, hr, sc), lambda i, jc_ref: (jc_ref[0], 0, 0))]
    in_specs += [pl.BlockSpec((None, hr, sc), lambda i, jc_ref, r=r: (r, 0, 0)) for r in range(3)]
    args = [jc_arr, sf, rb, rb, rb]
    aliases = {}
    if into is not None:
        in_specs.append(ANY)
        args.append(into)
        aliases = {len(args) - 1: 0}
    gs = pltpu.PrefetchScalarGridSpec(
        num_scalar_prefetch=1, grid=(1,), in_specs=in_specs,
        out_specs=pl.BlockSpec((None, hr, sc), lambda i, jc_ref: (l, jc_ref[1], 0)))
    return pl.pallas_call(kern, out_shape=jax.ShapeDtypeStruct((DEPTH, 2 * hr, sc), F32), grid_spec=gs,
                          input_output_aliases=aliases, name="grad_owner_sum", compiler_params=_params("arbitrary"))(*args)


def _sibling_assemble(grads):
    n = len(grads)

    def body(*refs):
        o_refs = refs[n:2 * n]
        send_sem, recv_sem = refs[2 * n:]
        x, y, c, _ = _mesh_pos()
        cps = []
        for q in range(n):
            hr = grads[q].shape[1] // 2
            for l in range(DEPTH):
                mine = o_refs[q].at[l, pl.ds(c * hr, hr), :]
                cp = pltpu.make_async_remote_copy(mine, mine, send_sem.at[DEPTH * q + l], recv_sem.at[DEPTH * q + l],
                                                  (x, y, 1 - c), MESH)
                cp.start()
                cps.append(cp)
        for q in range(n):
            hr = grads[q].shape[1] // 2
            for l in range(DEPTH):
                theirs = o_refs[q].at[l, pl.ds((1 - c) * hr, hr), :]
                pltpu.make_async_remote_copy(theirs, theirs, send_sem.at[DEPTH * q + l], recv_sem.at[DEPTH * q + l],
                                             (x, y, 1 - c), MESH).wait_recv()
        for cp in cps:
            cp.wait_send()

    return pl.pallas_call(
        body, out_shape=[jax.ShapeDtypeStruct(g.shape, F32) for g in grads], in_specs=[ANY] * n, out_specs=[ANY] * n,
        input_output_aliases={q: q for q in range(n)},
        scratch_shapes=[pltpu.SemaphoreType.DMA((DEPTH * n,))] * 2, name="grad_sibling_assemble")(*grads)


def _adamw_math(w, g, m, v):
    m = ADAM_B1 * m + (1.0 - ADAM_B1) * g
    v = ADAM_B2 * v + (1.0 - ADAM_B2) * jnp.square(g)
    m_hat = m / (1.0 - ADAM_B1 ** ADAM_STEP)
    v_hat = v / (1.0 - ADAM_B2 ** ADAM_STEP)
    delta = -ADAM_LR * (m_hat / (jnp.sqrt(v_hat) + ADAM_EPS) + ADAM_WD * w)
    return delta, m, v


def _adamw(w, g, m, v):
    shape = w.shape
    C = shape[-1]
    R = int(np.prod(shape[:-1]))
    tr = min(256, R)
    args = [a.reshape(R, C) for a in (w, g, m, v)]

    def kern(w_ref, g_ref, m_ref, v_ref, d_ref, nm_ref, nv_ref):
        d, nm, nv = _adamw_math(w_ref[...], g_ref[...], m_ref[...], v_ref[...])
        d_ref[...] = d
        nm_ref[...] = nm
        nv_ref[...] = nv

    spec = pl.BlockSpec((tr, C), lambda i: (i, 0))
    outs = pl.pallas_call(
        kern, out_shape=[jax.ShapeDtypeStruct((R, C), F32)] * 3, grid=(R // tr,), in_specs=[spec] * 4, out_specs=[spec] * 3,
        name="adamw", compiler_params=_params("parallel"))(*args)
    return [o.reshape(shape) for o in outs]


_ROW_G_MIX, _ROW_B_GATES, _ROW_SINKS, _ROW_CONV, _ROW_G_MLP, _ROW_G_FINAL, _ROW_LOSS = 0, 2, 6, 8, 16, 18, 19


def _small_step(parts, params, moms, vels):
    names = ["g_mix", "b_gates", "sinks", "conv_w", "conv_b", "g_mlp", "g_final"]
    D = D_MODEL
    QW = D // N_CHIP
    n_dev = 8

    def body(*refs):
        it = iter(refs)
        dgmix = [next(it) for _ in range(DEPTH)]
        dbg = [next(it) for _ in range(DEPTH)]
        dsk = [next(it) for _ in range(DEPTH)]
        dwb = [next(it) for _ in range(DEPTH)]
        dgmlp = [next(it) for _ in range(DEPTH)]
        lst = next(it)
        p_refs = {n: next(it) for n in names}
        m_refs = {n: next(it) for n in names}
        v_refs = {n: next(it) for n in names}
        loss_ref = next(it)
        outs = {n: [next(it) for _ in range(4)] for n in names}
        pack_ref, all_ref, send_sem, recv_sem = next(it), next(it), next(it), next(it)

        x, y, c, j = _mesh_pos()
        me = 4 * x + 2 * y + c
        pack_ref[...] = jnp.zeros_like(pack_ref)
        for l in range(DEPTH):
            pack_ref[_ROW_G_MIX + l:_ROW_G_MIX + l + 1, :] = dgmix[l][...]
            pack_ref[_ROW_B_GATES + 2 * l:_ROW_B_GATES + 2 * l + 1, :] = dbg[l][:, 0:D]
            pack_ref[_ROW_B_GATES + 2 * l + 1:_ROW_B_GATES + 2 * l + 2, :] = dbg[l][:, D:2 * D]
            pack_ref[_ROW_SINKS + l:_ROW_SINKS + l + 1, 0:128] = dsk[l][0:1, :]
            pack_ref[_ROW_CONV + 4 * l:_ROW_CONV + 4 * l + 4, :] = dwb[l][0:4, :]
            pack_ref[_ROW_G_MLP + l:_ROW_G_MLP + l + 1, :] = dgmlp[l][...]
        pack_ref[_ROW_G_FINAL:_ROW_G_FINAL + 1, :] = lst[0:1, :]
        pack_ref[_ROW_LOSS:_ROW_LOSS + 1, :] = lst[1:2, :]

        all_ref[me] = pack_ref[...]
        cps = []
        for k in range(1, n_dev):
            dx_, dy_, dc_ = (k >> 2) & 1, (k >> 1) & 1, k & 1
            peer = (x ^ dx_, y ^ dy_, c ^ dc_)
            cp = pltpu.make_async_remote_copy(pack_ref, all_ref.at[me], send_sem.at[k - 1], recv_sem.at[k - 1], peer, MESH)
            cp.start()
            cps.append(cp)
        for cp in cps:
            cp.wait()

        tot = all_ref[0]
        for d in range(1, n_dev):
            tot = tot + all_ref[d]
        pack_ref[...] = tot

        loss_ref[...] = pack_ref[_ROW_LOSS:_ROW_LOSS + 1, 0:1]

        def finish(name, idx, g):
            w, m, v = p_refs[name][idx], m_refs[name][idx], v_refs[name][idx]
            d, nm, nv = _adamw_math(w, g, m, v)
            for ref, val in zip(outs[name], (g, d, nm, nv)):
                ref[idx] = val

        for l in range(DEPTH):
            finish("g_mix", (slice(l, l + 1), slice(None)), pack_ref[_ROW_G_MIX + l:_ROW_G_MIX + l + 1, :])
            finish("g_mlp", (slice(l, l + 1), slice(None)), pack_ref[_ROW_G_MLP + l:_ROW_G_MLP + l + 1, :])
            finish("conv_b", (slice(l, l + 1), slice(None)), pack_ref[_ROW_CONV + 4 * l + 3:_ROW_CONV + 4 * l + 4, :])
            finish("sinks", (slice(l, l + 1), slice(None)), pack_ref[_ROW_SINKS + l:_ROW_SINKS + l + 1, 0:N_Q_HEADS])
            for hf in range(2):
                finish("b_gates", (slice(l, l + 1), slice(hf * D, (hf + 1) * D)),
                       pack_ref[_ROW_B_GATES + 2 * l + hf:_ROW_B_GATES + 2 * l + hf + 1, :])
        finish("g_final", (slice(0, 1), slice(None)), pack_ref[_ROW_G_FINAL:_ROW_G_FINAL + 1, :])

        def conv_w_chip(js):
            for l in range(DEPTH):
                for k in range(3):
                    row = _ROW_CONV + 4 * l + k
                    finish("conv_w", (l, slice(k, k + 1), slice(None)), pack_ref[row:row + 1, js * QW:(js + 1) * QW])

        _for_my_chip(j, conv_w_chip)

    vm = pl.BlockSpec(memory_space=pltpu.VMEM)
    ins = (parts["g_mix"] + parts["b_gates"] + parts["sinks"] + parts["conv"] + parts["g_mlp"] + [parts["loss"]]
           + [params[n] for n in names] + [moms[n] for n in names] + [vels[n] for n in names])
    out_shape = [jax.ShapeDtypeStruct((1, 1), F32)]
    for n in names:
        out_shape += [jax.ShapeDtypeStruct(params[n].shape, F32)] * 4
    res = pl.pallas_call(
        body, out_shape=out_shape, in_specs=[vm] * len(ins), out_specs=[vm] * len(out_shape),
        scratch_shapes=[pltpu.VMEM((SMALL_ROWS, D), F32), pltpu.VMEM((n_dev, SMALL_ROWS, D), F32),
                        pltpu.SemaphoreType.DMA((n_dev - 1,)), pltpu.SemaphoreType.DMA((n_dev - 1,))],
        name="small_allreduce_adamw")(*ins)
    loss = res[0]
    out = {n: res[1 + 4 * i:5 + 4 * i] for i, n in enumerate(names)}
    return loss, out


def kernel(x, g_mix, w_in, b_gates, sinks, w_attn_out, conv_w, conv_b, w_conv_out, w_o, g_mlp, w_up, w_down, g_final, loss_target, m_g_mix, m_w_in, m_b_gates, m_sinks, m_w_attn_out, m_conv_w, m_conv_b, m_w_conv_out, m_w_o, m_g_mlp, m_w_up, m_w_down, m_g_final, v_g_mix, v_w_in, v_b_gates, v_sinks, v_w_attn_out, v_conv_w, v_conv_b, v_w_conv_out, v_w_o, v_g_mlp, v_w_up, v_w_down, v_g_final):
    B, S, D = x.shape
    T = B * S
    big_w = dict(w_in=w_in, w_attn_out=w_attn_out, w_conv_out=w_conv_out, w_o=w_o, w_up=w_up, w_down=w_down)
    big_m = dict(w_in=m_w_in, w_attn_out=m_w_attn_out, w_conv_out=m_w_conv_out, w_o=m_w_o, w_up=m_w_up, w_down=m_w_down)
    big_v = dict(w_in=v_w_in, w_attn_out=v_w_attn_out, w_conv_out=v_w_conv_out, w_o=v_w_o, w_up=v_w_up, w_down=v_w_down)

    c_arr = lax.axis_index("c").astype(jnp.int32).reshape(1)
    j_arr = (2 * lax.axis_index("x") + lax.axis_index("y")).astype(jnp.int32).reshape(1)
    jc_arr = jnp.concatenate([j_arr, c_arr])
    order = [(n, l) for n, _, _, _ in BIG for l in range(DEPTH)]
    dims = {n: (r, c_, k) for n, r, c_, k in BIG}

    full = {(n, l): _cast_into_full(j_arr, big_w[n], l, *dims[n]) for n, l in order}
    mixers = ("w_attn_out", "w_conv_out", "w_o")

    def gather_of(keys):
        return _gather_comm([(full[k],) + dims[k[0]] for k in keys])

    carried = {("proj", 0): [(n, 0) for n in mixers], ("attn", 0): [("w_up", 0), ("w_down", 0)],
               ("mlp", 0): [("w_in", 1)] + [(n, 1) for n in mixers], ("attn", 1): [("w_up", 1), ("w_down", 1)]}

    def carry(fn, where, *args):
        keys = carried.get(where)
        if keys is None:
            return fn(*args)
        res, got = fn(*args, comm=gather_of(keys))
        full.update(zip(keys, got))
        return res

    first = _gather([(full["w_in", 0],) + dims["w_in"]], conv_w)
    full["w_in", 0] = first[0]
    conv_w_full = jnp.transpose(first[1], (0, 2, 1, 3)).reshape(DEPTH, 3, D)
    attn_bias = _attn_bias_table()

    xs = [x.reshape(T, D)]
    saved = []
    for l in range(DEPTH):
        ht, pqkv, pconv, pgate = carry(_norm_proj, ("proj", l), xs[-1], g_mix[l:l + 1], full["w_in", l])
        att, att_t, lse = carry(_attn_fwd, ("attn", l), pqkv, sinks[l], attn_bias, S)
        cv, cv_t = _conv_fwd(pconv, conv_w_full[l], conv_b[l:l + 1], S)
        x1, ya, yc, mg_t = _mix_fwd(xs[-1], att, cv, pgate, b_gates[l:l + 1], full["w_attn_out", l], full["w_conv_out", l],
                                    full["w_o", l])
        x2, a = carry(_mlp_fwd, ("mlp", l), x1, g_mlp[l:l + 1], full["w_up", l], full["w_down", l])
        saved.append(dict(ht=ht, pqkv=pqkv, pconv=pconv, pgate=pgate, att=att, att_t=att_t, lse=lse, cv_t=cv_t, x1=x1, ya=ya,
                          yc=yc, mg_t=mg_t, a=a))
        xs.append(x2)

    loss_stats, dx = _loss_bwd(xs[-1], g_final.reshape(1, D), loss_target.reshape(T, D))

    parts = dict(g_mix=[None] * DEPTH, b_gates=[None] * DEPTH, sinks=[None] * DEPTH, conv=[None] * DEPTH,
                 g_mlp=[None] * DEPTH, loss=loss_stats)
    gf, gb = {}, {}
    pre, got = {}, {}

    def pre_reduce(keys):
        sib = _sibling_exchange([(gb[k],) + dims[k[0]] for k in keys])
        for k, s in zip(keys, sib):
            pre[k] = _half_add(c_arr, gf[k], s, *dims[k[0]])
        return keys

    def exchange_in(fn, keys, *args):
        res, arrived = fn(*args, comm=_chip_exchange_comm([pre[k][0] for k in keys]))
        got.update(zip(keys, arrived))
        return res

    pending = None
    for l in reversed(range(DEPTH)):
        W = {n: full[(n, l)] for n in big_w}
        sv = saved[l]
        mlp_args = (dx, sv["x1"], sv["a"], g_mlp[l:l + 1], W["w_up"], W["w_down"])
        dx1, da, u_t, h2_t, parts["g_mlp"][l] = exchange_in(_mlp_bwd, pending, *mlp_args) if pending else _mlp_bwd(*mlp_args)
        gf["w_up", l], gb["w_up", l] = _dw(h2_t, da)
        gf["w_down", l], gb["w_down", l] = _dw(u_t, dx)
        datt, dcv, dya, dyc, dgate, parts["b_gates"][l] = _mix_bwd(
            dx1, sv["ya"], sv["yc"], sv["pgate"], b_gates[l:l + 1], W["w_attn_out"], W["w_conv_out"], W["w_o"])
        gf["w_o", l], gb["w_o", l] = _dw(sv["mg_t"], dx1)
        gf["w_attn_out", l], gb["w_attn_out", l] = _dw(sv["att_t"], dya)
        gf["w_conv_out", l], gb["w_conv_out", l] = _dw(sv["cv_t"], dyc)
        dconv, parts["conv"][l] = _conv_bwd(dcv, sv["pconv"], conv_w_full[l], conv_b[l:l + 1], S)
        attn_args = (sv["pqkv"], datt, sv["lse"], sinks[l], attn_bias, S)
        last = l == 0
        if last:
            keys = pre_reduce([(n, l) for n in mixers + ("w_up", "w_down")])
            dq, dkv, parts["sinks"][l] = exchange_in(_attn_bwd, keys, *attn_args)
        else:
            dq, dkv, parts["sinks"][l] = _attn_bwd(*attn_args)
        pieces = [(dq, D), (dkv, QKV_W - D), (dconv, CONV_W), (dgate, GATE_W)]
        gf["w_in", l], gb["w_in", l] = _dw_pieces(sv["ht"], pieces)
        in_args = (pieces, W["w_in"], xs[l], dx1, g_mix[l:l + 1])
        if last:
            dx, parts["g_mix"][l] = exchange_in(_inproj_bwd, pre_reduce([("w_in", l)]), *in_args)
        else:
            dx, parts["g_mix"][l] = _inproj_bwd(*in_args)
            pending = pre_reduce([(n, l) for n, _, _, _ in BIG])

    mine = {}
    for n, l in order:
        mine[n] = _owner_sum(jc_arr, pre[n, l][1], got[n, l], l, mine.get(n))
    grads = dict(zip(mine, _sibling_assemble(list(mine.values()))))

    res = {}
    for n in big_w:
        d, nm, nv = _adamw(big_w[n], grads[n], big_m[n], big_v[n])
        res[n] = (grads[n], d, nm, nv)

    small_p = dict(g_mix=g_mix, b_gates=b_gates, sinks=sinks, conv_w=conv_w, conv_b=conv_b, g_mlp=g_mlp, g_final=g_final.reshape(1, D))
    small_m = dict(g_mix=m_g_mix, b_gates=m_b_gates, sinks=m_sinks, conv_w=m_conv_w, conv_b=m_conv_b, g_mlp=m_g_mlp,
                   g_final=m_g_final.reshape(1, D))
    small_v = dict(g_mix=v_g_mix, b_gates=v_b_gates, sinks=v_sinks, conv_w=v_conv_w, conv_b=v_conv_b, g_mlp=v_g_mlp,
                   g_final=v_g_final.reshape(1, D))
    loss, small = _small_step(parts, small_p, small_m, small_v)
    for n, vals in small.items():
        res[n] = tuple(v.reshape(D) for v in vals) if n == "g_final" else tuple(vals)

    weights = ["g_mix", "w_in", "b_gates", "sinks", "w_attn_out", "conv_w", "conv_b", "w_conv_out", "w_o", "g_mlp", "w_up",
               "w_down", "g_final"]
    out = [loss.reshape(()), dx.reshape(B, S, D)]
    for k in range(4):
        out += [res[n][k] for n in weights]
    return tuple(out)
```

```python
import functools

import numpy as np
import jax
import jax.numpy as jnp
from jax import lax
from jax.experimental import pallas as pl
from jax.experimental.pallas import tpu as pltpu

F32 = jnp.float32
BF16 = jnp.bfloat16

D_MODEL = 1024
HEAD_DIM = 64
N_Q_HEADS = 16
N_KV_HEADS = 4
GQA_GROUP = 4
WINDOW = 128
D_FF = 4096
DEPTH = 2
RMS_EPS = 1e-6
NEG_INF = -1e30
ATTN_SCALE = HEAD_DIM ** -0.5
QKV_W = 1536
CONV_W = 3072
GATE_W = 2048
IN_COLS = QKV_W + CONV_W + GATE_W
COL_TILE = 512
N_CHIP = 4
ADAM_LR = 0.001
ADAM_B1 = 0.9
ADAM_B2 = 0.999
ADAM_EPS = 1e-08
ADAM_WD = 0.01
ADAM_STEP = 10
V7X_VMEM_BYTES = 64 * 2 ** 20
VMEM_LIMIT = V7X_VMEM_BYTES - 8 * 2 ** 20
MESH = pl.DeviceIdType.MESH
ANY = pl.BlockSpec(memory_space=pl.ANY)
SMALL_ROWS = 24

_SLOPES = [float(v) for v in np.power(np.float32(2.0), -8.0 * np.arange(1, N_Q_HEADS + 1, dtype=np.float32) / N_Q_HEADS)]


def _params(*sem):
    return pltpu.CompilerParams(dimension_semantics=sem, vmem_limit_bytes=VMEM_LIMIT)


class _Hosted:
    def __init__(self, inputs, out_shape, aliases, scratch, start, finish):
        self.inputs, self.out_shape, self.aliases, self.scratch = list(inputs), list(out_shape), dict(aliases), list(scratch)
        self.start, self.finish = start, finish


def _hosted_call(comm, kern, *, out_shape, grid, in_specs, out_specs, args, name, sem, scratch_shapes=()):
    single = not isinstance(out_shape, (tuple, list))
    outs = [out_shape] if single else list(out_shape)
    ospecs = [out_specs] if single else list(out_specs)
    if comm is None:
        res = pl.pallas_call(kern, out_shape=outs, grid=grid, in_specs=list(in_specs), out_specs=ospecs,
                             scratch_shapes=list(scratch_shapes), name=name, compiler_params=_params(*sem))(*args)
        return res[0] if single else res
    n_in, n_out, n_scr = len(args), len(outs), len(scratch_shapes)
    ci, co, cs = len(comm.inputs), len(comm.out_shape), len(comm.scratch)

    def body(*refs):
        cuts = np.cumsum([0, n_in, ci, n_out, co, n_scr, cs])
        a, b, c, d, e, f = [refs[lo:hi] for lo, hi in zip(cuts[:-1], cuts[1:])]
        ids = [pl.program_id(k) for k in range(len(grid))]
        first = functools.reduce(jnp.logical_and, [i == 0 for i in ids])
        last = functools.reduce(jnp.logical_and, [i == n - 1 for i, n in zip(ids, grid)])
        pl.when(first)(lambda: comm.start(b, d, f))
        kern(*a, *c, *e)
        pl.when(last)(lambda: comm.finish(b, d, f))

    res = pl.pallas_call(
        body, out_shape=outs + comm.out_shape, grid=grid, in_specs=list(in_specs) + [ANY] * ci, out_specs=ospecs + [ANY] * co,
        scratch_shapes=list(scratch_shapes) + comm.scratch,
        input_output_aliases={n_in + i: n_out + o for i, o in comm.aliases.items()},
        name=name + "_carrier", compiler_params=_params(*(["arbitrary"] * len(grid))))(*args, *comm.inputs)
    main = res[:n_out]
    return (main[0] if single else main), res[n_out:]


def _nt(a, b):
    return lax.dot_general(a, b, (((1,), (1,)), ((), ())), preferred_element_type=F32)


def _tn(a, b):
    return lax.dot_general(a, b, (((0,), (0,)), ((), ())), preferred_element_type=F32)


def _nn(a, b):
    return jnp.dot(a, b, preferred_element_type=F32)


def _rms_stats(xf):
    r = lax.rsqrt(jnp.mean(xf * xf, axis=-1, keepdims=True) + RMS_EPS)
    return r, xf * r


def _rms_bwd(dh, xh, r, g):
    dxh = dh * g
    dx = r * (dxh - xh * jnp.mean(dxh * xh, axis=-1, keepdims=True))
    dg = jnp.sum(dh * xh, axis=0, keepdims=True)
    return dx, dg


def _dw_chunk(T):
    return min(2048, T)


def _row_halves(tm):
    return (slice(0, tm // 2), slice(tm // 2, tm))


def _resident(shape):
    return pl.BlockSpec(shape, lambda *_: (0,) * len(shape), pipeline_mode=pl.Buffered(1))


def _norm_proj(x, g, w, comm=None):
    T, D = x.shape
    tm = min(256, T)
    tk = _dw_chunk(T)
    per = tk // tm
    widths = (QKV_W, CONV_W, GATE_W)

    def kern(x_ref, g_ref, w_ref, ht_ref, *o_refs):
        _, xh = _rms_stats(x_ref[...])
        h = (xh * g_ref[...]).astype(BF16)
        ht_ref[...] = h.T
        off = 0
        for o_ref, wd in zip(o_refs, widths):
            o_ref[...] = _nn(h, w_ref[:, off:off + wd]).astype(BF16)
            off += wd

    row = lambda wd: pl.BlockSpec((tm, wd), lambda i: (i, 0))
    return _hosted_call(
        comm, kern,
        out_shape=[jax.ShapeDtypeStruct((T // tk, D, tk), BF16)] + [jax.ShapeDtypeStruct((T, wd), BF16) for wd in widths],
        grid=(T // tm,), in_specs=[row(D), pl.BlockSpec((1, D), lambda i: (0, 0)), _resident((D, IN_COLS))],
        out_specs=[pl.BlockSpec((None, D, tm), lambda i: (i // per, 0, i % per))] + [row(wd) for wd in widths],
        name="norm_proj", sem=("parallel",), args=(x, g, w))


GW = GQA_GROUP * WINDOW
BAND = 2 * WINDOW
KV_W = N_KV_HEADS * HEAD_DIM


def _attn_bias_table():
    jj = np.arange(BAND)[:, None]
    col = np.arange(GW)[None, :]
    dist = WINDOW + (col % WINDOW) - jj
    valid = (dist >= 0) & (dist < WINDOW)
    slopes = np.asarray(_SLOPES, np.float32).reshape(N_KV_HEADS, GQA_GROUP)
    tab = np.empty((2, N_KV_HEADS, BAND, GW), np.float32)
    for hk in range(N_KV_HEADS):
        bias = -slopes[hk][col // WINDOW] * dist.astype(np.float32)
        tab[0, hk] = np.where(valid, bias, np.float32(NEG_INF))
        tab[1, hk] = np.where(valid & (jj >= WINDOW), bias, np.float32(NEG_INF))
    return jnp.asarray(tab)


def _stack_heads(ref, hk):
    return jnp.concatenate(
        [ref[:, HEAD_DIM * (GQA_GROUP * hk + g): HEAD_DIM * (GQA_GROUP * hk + g + 1)] for g in range(GQA_GROUP)], axis=0)


def _kv_band(cur_ref, prev_ref, hk):
    k0 = N_Q_HEADS * HEAD_DIM
    sl = slice(HEAD_DIM * hk, HEAD_DIM * (hk + 1))
    ksl, vsl = slice(k0 + sl.start, k0 + sl.stop), slice(k0 + KV_W + sl.start, k0 + KV_W + sl.stop)
    k_band = jnp.concatenate([prev_ref[:, sl], cur_ref[:, ksl]], axis=0)
    v_band = jnp.concatenate([prev_ref[:, KV_W + sl.start:KV_W + sl.stop], cur_ref[:, vsl]], axis=0)
    return k_band, v_band


def _lane_row(vals):
    return jnp.concatenate([jnp.full((1, WINDOW), v, F32) for v in vals], axis=1)


def _attn_fwd(pqkv, sinks, bias, seq, comm=None):
    T = pqkv.shape[0]
    nblk = seq // WINDOW

    def kern(sink_ref, cur_ref, prev_ref, bias_ref, o_ref, ot_ref, lse_ref):
        i = pl.program_id(0)
        first_i = ((i % nblk) == 0).astype(jnp.int32)
        bands = [_kv_band(cur_ref, prev_ref, hk) for hk in range(N_KV_HEADS)]
        sts = [_nt(bands[hk][0], _stack_heads(cur_ref, hk) * ATTN_SCALE) + bias_ref[first_i, hk] for hk in range(N_KV_HEADS)]
        ps, scales = [], []
        for hk in range(N_KV_HEADS):
            heads = [GQA_GROUP * hk + g for g in range(GQA_GROUP)]
            sink = _lane_row([sink_ref[h] for h in heads])
            m = jnp.maximum(jnp.max(sts[hk], axis=0, keepdims=True), sink)
            p = jnp.exp(sts[hk] - m)
            den = jnp.sum(p, axis=0, keepdims=True) + jnp.exp(sink - m)
            lse = m + jnp.log(den)
            for g, h in enumerate(heads):
                lse_ref[h:h + 1, :] = lse[:, WINDOW * g:WINDOW * (g + 1)]
            ps.append(p.astype(BF16))
            scales.append(1.0 / den)
        for hk in range(N_KV_HEADS):
            ot = _tn(bands[hk][1], ps[hk]) * scales[hk]
            for g in range(GQA_GROUP):
                h = GQA_GROUP * hk + g
                ot_ref[HEAD_DIM * h:HEAD_DIM * (h + 1), :] = ot[:, WINDOW * g:WINDOW * (g + 1)].astype(BF16)
        o_ref[...] = ot_ref[...].T

    return _hosted_call(
        comm, kern,
        out_shape=(jax.ShapeDtypeStruct((T, D_MODEL), BF16), jax.ShapeDtypeStruct((D_MODEL, T), BF16),
                   jax.ShapeDtypeStruct((N_Q_HEADS, T), F32)),
        grid=(T // WINDOW,),
        in_specs=[pl.BlockSpec(memory_space=pltpu.SMEM),
                  pl.BlockSpec((WINDOW, QKV_W), lambda i: (i, 0)),
                  pl.BlockSpec((WINDOW, 2 * KV_W), lambda i: (jnp.maximum(i - 1, 0), 2)),
                  _resident(bias.shape)],
        out_specs=(pl.BlockSpec((WINDOW, D_MODEL), lambda i: (i, 0)), pl.BlockSpec((D_MODEL, WINDOW), lambda i: (0, i)),
                   pl.BlockSpec((N_Q_HEADS, WINDOW), lambda i: (0, i))),
        name="attn_fwd", sem=("parallel",), args=(sinks, pqkv, pqkv, bias))


def _pick_row(a, row):
    rid = lax.broadcasted_iota(jnp.int32, a.shape, 0)
    return jnp.sum(jnp.where(rid == row, a, 0.0), axis=0, keepdims=True)


def _conv_taps(yc, halo_yc, first_i):
    keep = (1 - first_i).astype(F32)
    p1 = _pick_row(halo_yc, 15) * keep
    p2 = _pick_row(halo_yc, 14) * keep
    rowid = lax.broadcasted_iota(jnp.int32, yc.shape, 0)
    s1 = jnp.where(rowid == 0, p1, pltpu.roll(yc, 1, 0))
    s2 = jnp.where(rowid == 0, p2, jnp.where(rowid == 1, p1, pltpu.roll(yc, 2, 0)))
    return s1, s2


def _conv_fwd(pconv, conv_w, conv_b, seq):
    T = pconv.shape[0]
    tm = min(256, seq)
    per_seq = seq // tm
    D = D_MODEL

    def kern(cur_ref, halo_ref, w_ref, b_ref, o_ref, ot_ref):
        i = pl.program_id(0)
        first_i = ((i % per_seq) == 0).astype(jnp.int32)
        cb = cur_ref[:, 0:D].astype(F32)
        yc = cur_ref[:, D:2 * D].astype(F32) * cur_ref[:, 2 * D:3 * D].astype(F32)
        halo_yc = halo_ref[:, D:2 * D].astype(F32) * halo_ref[:, 2 * D:3 * D].astype(F32)
        s1, s2 = _conv_taps(yc, halo_yc, first_i)
        z = w_ref[0:1, :] * s2 + w_ref[1:2, :] * s1 + w_ref[2:3, :] * yc
        cv = (cb * (z + b_ref[...])).astype(BF16)
        o_ref[...] = cv
        ot_ref[...] = cv.T

    return pl.pallas_call(
        kern, out_shape=(jax.ShapeDtypeStruct((T, D), BF16), jax.ShapeDtypeStruct((D, T), BF16)), grid=(T // tm,),
        in_specs=[pl.BlockSpec((tm, CONV_W), lambda i: (i, 0)),
                  pl.BlockSpec((16, CONV_W), lambda i: (jnp.maximum(i * (tm // 16) - 1, 0), 0)),
                  pl.BlockSpec((3, D), lambda i: (0, 0)), pl.BlockSpec((1, D), lambda i: (0, 0))],
        out_specs=(pl.BlockSpec((tm, D), lambda i: (i, 0)), pl.BlockSpec((D, tm), lambda i: (0, i))),
        name="conv_fwd", compiler_params=_params("parallel"))(pconv, pconv, conv_w, conv_b)


def _mix_fwd(x, att, cv, pgate, b_gates, wao, wco, wo):
    T, D = x.shape
    tm = min(512, T)

    halves = _row_halves(tm)

    def kern(x_ref, att_ref, cv_ref, pg_ref, bg_ref, wao_ref, wco_ref, wo_ref, x1_ref, ya_ref, yc_ref, mgt_ref):
        yas = [_nn(att_ref[rows, :], wao_ref[...]) for rows in halves]
        ycs = [_nn(cv_ref[rows, :], wco_ref[...]) for rows in halves]
        mgs = []
        for rows, ya, yc in zip(halves, yas, ycs):
            sa = jax.nn.sigmoid(pg_ref[rows, 0:D].astype(F32) + bg_ref[:, 0:D])
            sc = jax.nn.sigmoid(pg_ref[rows, D:2 * D].astype(F32) + bg_ref[:, D:2 * D])
            mg = (sa * ya + sc * yc).astype(BF16)
            ya_ref[rows, :] = ya.astype(BF16)
            yc_ref[rows, :] = yc.astype(BF16)
            mgt_ref[:, rows] = mg.T
            mgs.append(mg)
        for rows, mg in zip(halves, mgs):
            x1_ref[rows, :] = x_ref[rows, :] + _nn(mg, wo_ref[...])

    row = lambda w: pl.BlockSpec((tm, w), lambda i: (i, 0))
    full = lambda a, b: pl.BlockSpec((a, b), lambda i: (0, 0))
    bf = jax.ShapeDtypeStruct((T, D), BF16)
    return pl.pallas_call(
        kern, out_shape=(jax.ShapeDtypeStruct((T, D), F32), bf, bf, jax.ShapeDtypeStruct((D, T), BF16)), grid=(T // tm,),
        in_specs=[row(D), row(D), row(D), row(GATE_W), full(1, GATE_W)] + [_resident((D, D))] * 3,
        out_specs=(row(D), row(D), row(D), pl.BlockSpec((D, tm), lambda i: (0, i))),
        name="mix_fwd", compiler_params=_params("parallel"))(x, att, cv, pgate, b_gates, wao, wco, wo)


def _mlp_fwd(x1, g, wup, wdn, comm=None):
    T, D = x1.shape
    tm = min(1024, T)
    nj = D_FF // D
    halves = _row_halves(tm)

    def kern(x_ref, g_ref, wup_ref, wdn_ref, x2_ref, a_ref, h_scr, acc_scr):
        j = pl.program_id(1)

        @pl.when(j == 0)
        def _():
            xf = x_ref[...]
            _, xh = _rms_stats(xf)
            h_scr[...] = (xh * g_ref[...]).astype(BF16)
            acc_scr[...] = xf

        pre = [_nn(h_scr[rows, :], wup_ref[...]) for rows in halves]
        us = []
        for rows, a in zip(halves, pre):
            a_ref[rows, :] = a.astype(BF16)
            us.append(jnp.square(jnp.maximum(a, 0.0)).astype(BF16))
        for rows, u in zip(halves, us):
            acc_scr[rows, :] += _nn(u, wdn_ref[...])

        @pl.when(j == nj - 1)
        def _():
            x2_ref[...] = acc_scr[...]

    return _hosted_call(
        comm, kern, out_shape=(jax.ShapeDtypeStruct((T, D), F32), jax.ShapeDtypeStruct((T, D_FF), BF16)), grid=(T // tm, nj),
        in_specs=[pl.BlockSpec((tm, D), lambda i, j: (i, 0)), pl.BlockSpec((1, D), lambda i, j: (0, 0)),
                  pl.BlockSpec((D, D), lambda i, j: (0, j)), pl.BlockSpec((D, D), lambda i, j: (j, 0))],
        out_specs=(pl.BlockSpec((tm, D), lambda i, j: (i, 0)), pl.BlockSpec((tm, D), lambda i, j: (i, j))),
        scratch_shapes=[pltpu.VMEM((tm, D), BF16), pltpu.VMEM((tm, D), F32)],
        name="mlp_fwd", sem=("parallel", "arbitrary"), args=(x1, g, wup, wdn))


def _loss_bwd(x, g, tgt):
    T, D = x.shape
    tm = min(512, T)

    def kern(x_ref, g_ref, t_ref, st_ref, dx_ref):
        i = pl.program_id(0)

        @pl.when(i == 0)
        def _():
            st_ref[...] = jnp.zeros_like(st_ref)

        gg = g_ref[...]
        r, xh = _rms_stats(x_ref[...])
        e = xh * gg - t_ref[...]
        part = 0.5 * jnp.sum(jnp.mean(e * e, axis=-1, keepdims=True), axis=0, keepdims=True)
        dx, dg = _rms_bwd(e * (1.0 / D), xh, r, gg)
        dx_ref[...] = dx
        st_ref[0:1, :] += dg
        st_ref[1:2, 0:1] += part

    return pl.pallas_call(
        kern, out_shape=(jax.ShapeDtypeStruct((8, D), F32), jax.ShapeDtypeStruct((T, D), F32)), grid=(T // tm,),
        in_specs=[pl.BlockSpec((tm, D), lambda i: (i, 0)), pl.BlockSpec((1, D), lambda i: (0, 0)),
                  pl.BlockSpec((tm, D), lambda i: (i, 0))],
        out_specs=(pl.BlockSpec((8, D), lambda i: (0, 0)), pl.BlockSpec((tm, D), lambda i: (i, 0))),
        name="loss_bwd", compiler_params=_params("arbitrary"))(x, g, tgt)


def _mlp_bwd(dx2, x1, a, g, wup, wdn, comm=None):
    T, D = x1.shape
    tm = min(512, T)
    nj = D_FF // D

    halves = _row_halves(tm)

    def kern(dx2_ref, x1_ref, a_ref, g_ref, wup_ref, wdn_ref, dx1_ref, da_ref, ut_ref, h2t_ref, dg_ref, dyb_scr, acc_scr):
        i, j = pl.program_id(0), pl.program_id(1)

        @pl.when((i == 0) & (j == 0))
        def _():
            dg_ref[...] = jnp.zeros_like(dg_ref)

        @pl.when(j == 0)
        def _():
            dyb_scr[...] = dx2_ref[...].astype(BF16)
            acc_scr[...] = jnp.zeros_like(acc_scr)

        dus = [_nt(dyb_scr[rows, :], wdn_ref[...]) for rows in halves]
        das = []
        for rows, du in zip(halves, dus):
            relu = jnp.maximum(a_ref[rows, :].astype(F32), 0.0)
            da = (du * (2.0 * relu)).astype(BF16)
            da_ref[rows, :] = da
            ut_ref[:, rows] = jnp.square(relu).astype(BF16).T
            das.append(da)
        for rows, da in zip(halves, das):
            acc_scr[rows, :] += _nt(da, wup_ref[...])

        @pl.when(j == nj - 1)
        def _():
            gg = g_ref[...]
            r, xh = _rms_stats(x1_ref[...])
            h2t_ref[...] = (xh * gg).astype(BF16).T
            dx, dg = _rms_bwd(acc_scr[...], xh, r, gg)
            dx1_ref[...] = dx2_ref[...] + dx
            dg_ref[...] += dg

    return _hosted_call(
        comm, kern,
        out_shape=(jax.ShapeDtypeStruct((T, D), F32), jax.ShapeDtypeStruct((T, D_FF), BF16),
                   jax.ShapeDtypeStruct((D_FF, T), BF16), jax.ShapeDtypeStruct((D, T), BF16), jax.ShapeDtypeStruct((1, D), F32)),
        grid=(T // tm, nj),
        in_specs=[pl.BlockSpec((tm, D), lambda i, j: (i, 0)), pl.BlockSpec((tm, D), lambda i, j: (i, 0)),
                  pl.BlockSpec((tm, D), lambda i, j: (i, j)), pl.BlockSpec((1, D), lambda i, j: (0, 0)),
                  pl.BlockSpec((D, D), lambda i, j: (0, j)), pl.BlockSpec((D, D), lambda i, j: (j, 0))],
        out_specs=(pl.BlockSpec((tm, D), lambda i, j: (i, 0)), pl.BlockSpec((tm, D), lambda i, j: (i, j)),
                   pl.BlockSpec((D, tm), lambda i, j: (j, i)), pl.BlockSpec((D, tm), lambda i, j: (0, i)),
                   pl.BlockSpec((1, D), lambda i, j: (0, 0))),
        scratch_shapes=[pltpu.VMEM((tm, D), BF16), pltpu.VMEM((tm, D), F32)],
        name="mlp_bwd", sem=("arbitrary", "arbitrary"), args=(dx2, x1, a, g, wup, wdn))


def _mix_bwd(dx1, ya, yc, pgate, b_gates, wao, wco, wo):
    T, D = dx1.shape
    tm = min(512, T)
    halves = _row_halves(tm)

    def kern(dx_ref, ya_ref, yc_ref, pg_ref, bg_ref, wao_ref, wco_ref, wo_ref,
             datt_ref, dcv_ref, dya_ref, dyc_ref, dgt_ref, dbg_ref):
        @pl.when(pl.program_id(0) == 0)
        def _():
            dbg_ref[...] = jnp.zeros_like(dbg_ref)

        dms = [_nt(dx_ref[rows, :].astype(BF16), wo_ref[...]) for rows in halves]
        dys = []
        for rows, dm in zip(halves, dms):
            sa = jax.nn.sigmoid(pg_ref[rows, 0:D].astype(F32) + bg_ref[:, 0:D])
            sc = jax.nn.sigmoid(pg_ref[rows, D:2 * D].astype(F32) + bg_ref[:, D:2 * D])
            dya = (dm * sa).astype(BF16)
            dyc = (dm * sc).astype(BF16)
            dga = dm * ya_ref[rows, :].astype(F32) * (sa * (1.0 - sa))
            dgc = dm * yc_ref[rows, :].astype(F32) * (sc * (1.0 - sc))
            dya_ref[rows, :] = dya
            dyc_ref[rows, :] = dyc
            dgt_ref[rows, 0:D] = dga.astype(BF16)
            dgt_ref[rows, D:2 * D] = dgc.astype(BF16)
            dbg_ref[:, 0:D] += jnp.sum(dga, axis=0, keepdims=True)
            dbg_ref[:, D:2 * D] += jnp.sum(dgc, axis=0, keepdims=True)
            dys.append((dya, dyc))
        for rows, (dya, dyc) in zip(halves, dys):
            datt_ref[rows, :] = _nt(dya, wao_ref[...]).astype(BF16)
            dcv_ref[rows, :] = _nt(dyc, wco_ref[...]).astype(BF16)

    row = lambda w: pl.BlockSpec((tm, w), lambda i: (i, 0))
    full = lambda a, b: pl.BlockSpec((a, b), lambda i: (0, 0))
    bf = jax.ShapeDtypeStruct((T, D), BF16)
    return pl.pallas_call(
        kern,
        out_shape=(bf, bf, bf, bf, jax.ShapeDtypeStruct((T, GATE_W), BF16), jax.ShapeDtypeStruct((1, GATE_W), F32)),
        grid=(T // tm,),
        in_specs=[row(D), row(D), row(D), row(GATE_W), full(1, GATE_W)] + [_resident((D, D))] * 3,
        out_specs=(row(D), row(D), row(D), row(D), row(GATE_W), full(1, GATE_W)),
        name="mix_bwd", compiler_params=_params("arbitrary"))(dx1, ya, yc, pgate, b_gates, wao, wco, wo)


def _conv_bwd(dcv, pconv, conv_w, conv_b, seq):
    T = pconv.shape[0]
    tm = min(256, seq)
    per_seq = seq // tm
    D = D_MODEL
    nb16 = T // 16

    def kern(dcv_ref, dcvn_ref, cur_ref, prev_ref, next_ref, w_ref, b_ref, o_ref, dwb_ref):
        i = pl.program_id(0)

        @pl.when(i == 0)
        def _():
            dwb_ref[...] = jnp.zeros_like(dwb_ref)

        first_i = ((i % per_seq) == 0).astype(jnp.int32)
        keep_next = 1.0 - (((i + 1) % per_seq) == 0).astype(F32)
        cb = cur_ref[:, 0:D].astype(F32)
        cc = cur_ref[:, D:2 * D].astype(F32)
        cu = cur_ref[:, 2 * D:3 * D].astype(F32)
        yc = cc * cu
        halo_yc = prev_ref[:, D:2 * D].astype(F32) * prev_ref[:, 2 * D:3 * D].astype(F32)
        s1, s2 = _conv_taps(yc, halo_yc, first_i)
        w0, w1, w2 = w_ref[0:1, :], w_ref[1:2, :], w_ref[2:3, :]
        z = w0 * s2 + w1 * s1 + w2 * yc
        dcv = dcv_ref[...].astype(F32)
        dz = dcv * cb
        dzn = dcvn_ref[...].astype(F32) * next_ref[:, 0:D].astype(F32) * keep_next
        n1, n2 = _pick_row(dzn, 0), _pick_row(dzn, 1)
        rowid = lax.broadcasted_iota(jnp.int32, dz.shape, 0)
        u1 = jnp.where(rowid == tm - 1, n1, pltpu.roll(dz, tm - 1, 0))
        u2 = jnp.where(rowid == tm - 1, n2, jnp.where(rowid == tm - 2, n1, pltpu.roll(dz, tm - 2, 0)))
        dyc = w2 * dz + w1 * u1 + w0 * u2
        o_ref[:, 0:D] = (dcv * (z + b_ref[...])).astype(BF16)
        o_ref[:, D:2 * D] = (dyc * cu).astype(BF16)
        o_ref[:, 2 * D:3 * D] = (dyc * cc).astype(BF16)
        dwb_ref[0:1, :] += jnp.sum(dz * s2, axis=0, keepdims=True)
        dwb_ref[1:2, :] += jnp.sum(dz * s1, axis=0, keepdims=True)
        dwb_ref[2:3, :] += jnp.sum(dz * yc, axis=0, keepdims=True)
        dwb_ref[3:4, :] += jnp.sum(dz, axis=0, keepdims=True)

    prev_map = lambda i: (jnp.maximum(i * (tm // 16) - 1, 0), 0)
    next_map = lambda i: (jnp.minimum((i + 1) * (tm // 16), nb16 - 1), 0)
    return pl.pallas_call(
        kern, out_shape=(jax.ShapeDtypeStruct((T, CONV_W), BF16), jax.ShapeDtypeStruct((8, D), F32)), grid=(T // tm,),
        in_specs=[pl.BlockSpec((tm, D), lambda i: (i, 0)), pl.BlockSpec((16, D), next_map),
                  pl.BlockSpec((tm, CONV_W), lambda i: (i, 0)), pl.BlockSpec((16, CONV_W), prev_map),
                  pl.BlockSpec((16, CONV_W), next_map),
                  pl.BlockSpec((3, D), lambda i: (0, 0)), pl.BlockSpec((1, D), lambda i: (0, 0))],
        out_specs=(pl.BlockSpec((tm, CONV_W), lambda i: (i, 0)), pl.BlockSpec((8, D), lambda i: (0, 0))),
        name="conv_bwd", compiler_params=_params("arbitrary"))(dcv, dcv, pconv, pconv, pconv, conv_w, conv_b)


def _attn_bwd(pqkv, datt, lse, sinks, bias, seq, comm=None):
    T = pqkv.shape[0]
    nblk = seq // WINDOW
    nseq = T // seq
    KVW = KV_W

    def kern(sink_ref, cur_ref, prev_ref, do_ref, lse_ref, bias_ref, dq_ref, dkv_ref, ds_ref, kc_scr, vc_scr, dqt_scr):
        b, st = pl.program_id(0), pl.program_id(1)

        @pl.when((b == 0) & (st == 0))
        def _():
            ds_ref[...] = jnp.zeros_like(ds_ref)

        @pl.when(st == 0)
        def _():
            kc_scr[...] = jnp.zeros_like(kc_scr)
            vc_scr[...] = jnp.zeros_like(vc_scr)

        @pl.when(st < nblk)
        def _():
            first_i = (st == 0).astype(jnp.int32)
            groups = range(N_KV_HEADS)
            bands = [_kv_band(cur_ref, prev_ref, hk) for hk in groups]
            qs = [_stack_heads(cur_ref, hk) for hk in groups]
            dos = [_stack_heads(do_ref, hk) for hk in groups]
            sts = [_nt(bands[hk][0], qs[hk] * ATTN_SCALE) + bias_ref[first_i, hk] for hk in groups]
            dps = [_nt(bands[hk][1], dos[hk]) for hk in groups]
            pbs, dsss = [], []
            for hk in groups:
                heads = [GQA_GROUP * hk + g for g in range(GQA_GROUP)]
                sink = _lane_row([sink_ref[h] for h in heads])
                lse_g = jnp.concatenate([lse_ref[h:h + 1, :] for h in heads], axis=1)
                p = jnp.exp(sts[hk] - lse_g)
                d_row = jnp.sum(p * dps[hk], axis=0, keepdims=True)
                dsss.append((p * (dps[hk] - d_row) * ATTN_SCALE).astype(BF16))
                pbs.append(p.astype(BF16))
                psd = jnp.exp(sink - lse_g) * d_row
                for g, h in enumerate(heads):
                    ds_ref[0:1, h:h + 1] -= jnp.sum(psd[:, WINDOW * g:WINDOW * (g + 1)], axis=1, keepdims=True)
            for hk in groups:
                dqt = _tn(bands[hk][0], dsss[hk])
                dk_b = _nn(dsss[hk], qs[hk])
                dv_b = _nn(pbs[hk], dos[hk])
                for g in range(GQA_GROUP):
                    h = GQA_GROUP * hk + g
                    dqt_scr[HEAD_DIM * h:HEAD_DIM * (h + 1), :] = dqt[:, WINDOW * g:WINDOW * (g + 1)].astype(BF16)
                ksl = slice(HEAD_DIM * hk, HEAD_DIM * (hk + 1))
                vsl = slice(KVW + HEAD_DIM * hk, KVW + HEAD_DIM * (hk + 1))
                dkv_ref[:, ksl] = (kc_scr[:, ksl] + dk_b[0:WINDOW]).astype(BF16)
                dkv_ref[:, vsl] = (vc_scr[:, ksl] + dv_b[0:WINDOW]).astype(BF16)
                kc_scr[:, ksl] = dk_b[WINDOW:2 * WINDOW]
                vc_scr[:, ksl] = dv_b[WINDOW:2 * WINDOW]
            dq_ref[...] = dqt_scr[...].T

        @pl.when(st == nblk)
        def _():
            dkv_ref[:, 0:KVW] = kc_scr[...].astype(BF16)
            dkv_ref[:, KVW:2 * KVW] = vc_scr[...].astype(BF16)

    cur_map = lambda b, s: (b * nblk + jnp.minimum(s, nblk - 1), 0)
    prev_row = lambda b, s: b * nblk + jnp.clip(s - 1, 0, nblk - 1)
    return _hosted_call(
        comm, kern,
        out_shape=(jax.ShapeDtypeStruct((T, D_MODEL), BF16), jax.ShapeDtypeStruct((T, 2 * KVW), BF16),
                   jax.ShapeDtypeStruct((8, 128), F32)),
        grid=(nseq, nblk + 1),
        in_specs=[pl.BlockSpec(memory_space=pltpu.SMEM),
                  pl.BlockSpec((WINDOW, QKV_W), cur_map),
                  pl.BlockSpec((WINDOW, 2 * KVW), lambda b, s: (prev_row(b, s), 2)),
                  pl.BlockSpec((WINDOW, D_MODEL), cur_map),
                  pl.BlockSpec((N_Q_HEADS, WINDOW), lambda b, s: (0, b * nblk + jnp.minimum(s, nblk - 1))),
                  _resident(bias.shape)],
        out_specs=(pl.BlockSpec((WINDOW, D_MODEL), cur_map),
                   pl.BlockSpec((WINDOW, 2 * KVW), lambda b, s: (prev_row(b, s), 0)),
                   pl.BlockSpec((8, 128), lambda b, s: (0, 0))),
        scratch_shapes=[pltpu.VMEM((WINDOW, KVW), F32), pltpu.VMEM((WINDOW, KVW), F32), pltpu.VMEM((D_MODEL, WINDOW), BF16)],
        name="attn_bwd", sem=("arbitrary", "arbitrary"), args=(sinks, pqkv, pqkv, datt, lse, bias))


def _piece_tiles(pieces):
    out, start = [], 0
    for arr, width in pieces:
        out.append((arr, start, width // COL_TILE))
        start += width // COL_TILE
    return out, start


def _inproj_bwd(pieces, w_in, x, dx_in, g, comm=None):
    T, D = x.shape
    tm = min(256, T)

    def kern(*refs):
        p_refs = refs[:len(pieces)]
        w_ref, x_ref, dxin_ref, g_ref, dx_ref, dg_ref = refs[len(pieces):]

        @pl.when(pl.program_id(0) == 0)
        def _():
            dg_ref[...] = jnp.zeros_like(dg_ref)

        dh, off = None, 0
        for p_ref, (_, width) in zip(p_refs, pieces):
            part = _nt(p_ref[...], w_ref[:, off:off + width])
            dh = part if dh is None else dh + part
            off += width
        gg = g_ref[...]
        r, xh = _rms_stats(x_ref[...])
        dx, dg = _rms_bwd(dh, xh, r, gg)
        dx_ref[...] = dxin_ref[...] + dx
        dg_ref[...] += dg

    row = lambda wd: pl.BlockSpec((tm, wd), lambda i: (i, 0))
    return _hosted_call(
        comm, kern, out_shape=(jax.ShapeDtypeStruct((T, D), F32), jax.ShapeDtypeStruct((1, D), F32)), grid=(T // tm,),
        in_specs=[row(wd) for _, wd in pieces] + [_resident((D, IN_COLS)), row(D), row(D), pl.BlockSpec((1, D), lambda i: (0, 0))],
        out_specs=(row(D), pl.BlockSpec((1, D), lambda i: (0, 0))),
        name="inproj_bwd", sem=("arbitrary",), args=(*[a for a, _ in pieces], w_in, x, dx_in, g))


def _dw_pieces(lhs_t, pieces):
    nt, K, tk = lhs_t.shape
    tiles, nj = _piece_tiles(pieces)

    def kern(*refs):
        lhs_ref = refs[0]
        p_refs = refs[1:1 + len(tiles)]
        o_ref, ob_ref = refs[1 + len(tiles):]
        j, t = pl.program_id(0), pl.program_id(1)

        @pl.when(t == 0)
        def _():
            o_ref[...] = jnp.zeros_like(o_ref)

        for p_ref, (_, start, n) in zip(p_refs, tiles):
            @pl.when((j >= start) & (j < start + n))
            def _(p_ref=p_ref):
                o_ref[...] += _nn(lhs_ref[t], p_ref[...])

        @pl.when(t == nt - 1)
        def _():
            ob_ref[...] = o_ref[...].astype(BF16)

    def p_map(start, n):
        return lambda j, t: (jnp.where((j >= start) & (j < start + n), t, 0), jnp.clip(j - start, 0, n - 1))

    N = nj * COL_TILE
    return pl.pallas_call(
        kern, out_shape=(jax.ShapeDtypeStruct((K, N), F32), jax.ShapeDtypeStruct((K, N), BF16)), grid=(nj, nt),
        in_specs=[_resident((nt, K, tk))] + [pl.BlockSpec((tk, COL_TILE), p_map(s, n)) for _, s, n in tiles],
        out_specs=(pl.BlockSpec((K, COL_TILE), lambda j, t: (0, j)), pl.BlockSpec((K, COL_TILE), lambda j, t: (0, j))),
        name="dw_pieces", compiler_params=_params("arbitrary", "arbitrary"))(lhs_t, *[a for a, _, _ in tiles])


def _dw(lhs_t, rhs):
    K, T = lhs_t.shape
    N = rhs.shape[1]
    tk = min(2048, T)
    nt = T // tk
    W = D_MODEL

    def kern(lhs_ref, rhs_ref, o_ref, ob_ref):
        t = pl.program_id(2)

        @pl.when(t == 0)
        def _():
            o_ref[...] = jnp.zeros_like(o_ref)

        o_ref[...] += _nn(lhs_ref[...], rhs_ref[...].astype(BF16))

        @pl.when(t == nt - 1)
        def _():
            ob_ref[...] = o_ref[...].astype(BF16)

    omap = lambda i, j, t: (i, j)
    return pl.pallas_call(
        kern, out_shape=(jax.ShapeDtypeStruct((K, N), F32), jax.ShapeDtypeStruct((K, N), BF16)), grid=(K // W, N // W, nt),
        in_specs=[pl.BlockSpec((W, tk), lambda i, j, t: (i, t)), pl.BlockSpec((tk, W), lambda i, j, t: (t, j))],
        out_specs=(pl.BlockSpec((W, W), omap), pl.BlockSpec((W, W), omap)),
        name="dw", compiler_params=_params("arbitrary", "arbitrary", "arbitrary"))(lhs_t, rhs)


BIG = (("w_in", D_MODEL, IN_COLS, "col"), ("w_attn_out", D_MODEL, D_MODEL, "row"), ("w_conv_out", D_MODEL, D_MODEL, "row"),
       ("w_o", D_MODEL, D_MODEL, "row"), ("w_up", D_MODEL, D_FF, "col"), ("w_down", D_FF, D_MODEL, "row"))


def _shard_dims(rows, cols, kind):
    return (rows, cols // N_CHIP) if kind == "col" else (rows // N_CHIP, cols)


def _window(ref, rows, cols, kind, chip, half):
    sr, sc = _shard_dims(rows, cols, kind)
    hr = sr // 2
    if kind == "col":
        return ref.at[pl.ds(half * hr, hr), pl.ds(chip * sc, sc)]
    return ref.at[pl.ds(chip * sr + half * hr, hr), :]


def _mesh_pos():
    x, y, c = lax.axis_index("x"), lax.axis_index("y"), lax.axis_index("c")
    return x, y, c, 2 * x + y


_REL_BITS = (2, 1, 3)


def _rel_dev(x, y, c, r):
    return ((1 - x, y, c), (x, 1 - y, c), (1 - x, 1 - y, c))[r]


def _for_my_chip(j, fn):
    for js in range(N_CHIP):
        pl.when(j == js)(functools.partial(fn, js))


def _cast_into_full(j_arr, shard, l, rows, cols, kind):
    sr, sc = _shard_dims(rows, cols, kind)
    tr = min(256, sr)

    def kern(j_ref, s_ref, o_ref):
        o_ref[...] = s_ref[...].astype(BF16)

    if kind == "col":
        omap = lambda i, j_ref: (i, j_ref[0])
    else:
        omap = lambda i, j_ref: (j_ref[0] * (sr // tr) + i, 0)
    gs = pltpu.PrefetchScalarGridSpec(
        num_scalar_prefetch=1, grid=(sr // tr,),
        in_specs=[pl.BlockSpec((None, tr, sc), lambda i, j_ref: (l, i, 0))], out_specs=pl.BlockSpec((tr, sc), omap))
    return pl.pallas_call(kern, out_shape=jax.ShapeDtypeStruct((rows, cols), BF16), grid_spec=gs, name="cast_into_full",
                          compiler_params=_params("arbitrary"))(j_arr, shard)


def _gather(fulls, cw=None):
    comm = _gather_comm(fulls, cw)
    n = len(comm.inputs)

    def body(*refs):
        comm.start(refs[:n], refs[n:2 * n], refs[2 * n:])
        comm.finish(refs[:n], refs[n:2 * n], refs[2 * n:])

    return pl.pallas_call(
        body, out_shape=comm.out_shape, in_specs=[ANY] * n, out_specs=[ANY] * n, input_output_aliases=comm.aliases,
        scratch_shapes=comm.scratch, name="gather_weights")(*comm.inputs)


def _gather_comm(fulls, cw=None):
    n_big = len(fulls)
    n_piece = n_big + (0 if cw is None else 1)

    def pieces(in_refs, o_refs, js, c):
        def piece(p, chip, half):
            if p == n_big:
                return o_refs[p].at[half, chip]
            _, rows, cols, kind = fulls[p]
            return _window(o_refs[p], rows, cols, kind, chip, half)

        def mine(p):
            return in_refs[p].at[c] if p == n_big else piece(p, js, c)

        return piece, mine

    def local_copies(in_refs, piece, js, loc_sem):
        if cw is None:
            return []
        return [pltpu.make_async_copy(in_refs[n_big].at[half], piece(n_big, js, half), loc_sem.at[half]) for half in range(2)]

    def ici_copy(piece, mine, js, x, y, c, r, p, send_sem, recv_sem):
        return pltpu.make_async_remote_copy(mine(p), piece(p, js, c), send_sem.at[r * n_piece + p], recv_sem.at[r * n_piece + p],
                                            _rel_dev(x, y, c, r), MESH)

    def start(in_refs, o_refs, sems):
        send_sem, recv_sem, _, _, loc_sem = sems
        x, y, c, j = _mesh_pos()

        def run(js):
            piece, mine = pieces(in_refs, o_refs, js, c)
            for cp in local_copies(in_refs, piece, js, loc_sem):
                cp.start()
            for r in range(3):
                for p in range(n_piece):
                    ici_copy(piece, mine, js, x, y, c, r, p, send_sem, recv_sem).start()

        _for_my_chip(j, run)

    def finish(in_refs, o_refs, sems):
        send_sem, recv_sem, fsend_sem, frecv_sem, loc_sem = sems
        x, y, c, j = _mesh_pos()

        def run(js):
            piece, mine = pieces(in_refs, o_refs, js, c)
            fwds = []
            for r in range(3):
                ks = js ^ _REL_BITS[r]
                for p in range(n_piece):
                    got = piece(p, ks, c)
                    pltpu.make_async_remote_copy(got, got, send_sem.at[r * n_piece + p], recv_sem.at[r * n_piece + p],
                                                 _rel_dev(x, y, c, r), MESH).wait_recv()
                    cp = pltpu.make_async_remote_copy(got, got, fsend_sem.at[r * n_piece + p], frecv_sem.at[r * n_piece + p],
                                                      (x, y, 1 - c), MESH)
                    cp.start()
                    fwds.append(cp)
            for r in range(3):
                ks = js ^ _REL_BITS[r]
                for p in range(n_piece):
                    got = piece(p, ks, 1 - c)
                    pltpu.make_async_remote_copy(got, got, fsend_sem.at[r * n_piece + p], frecv_sem.at[r * n_piece + p],
                                                 (x, y, 1 - c), MESH).wait_recv()
            for r in range(3):
                for p in range(n_piece):
                    ici_copy(piece, mine, js, x, y, c, r, p, send_sem, recv_sem).wait_send()
            for cp in fwds:
                cp.wait_send()
            for cp in local_copies(in_refs, piece, js, loc_sem):
                cp.wait()

        _for_my_chip(j, run)

    out_shape = [jax.ShapeDtypeStruct(a.shape, BF16) for a, _, _, _ in fulls]
    ins = [a for a, _, _, _ in fulls]
    if cw is not None:
        out_shape.append(jax.ShapeDtypeStruct((DEPTH, N_CHIP, 3, D_MODEL // N_CHIP), F32))
        ins.append(cw)
    scratch = [pltpu.SemaphoreType.DMA((3 * n_piece,))] * 4 + [pltpu.SemaphoreType.DMA((2,))]
    return _Hosted(ins, out_shape, {p: p for p in range(n_big)}, scratch, start, finish)


def _sibling_exchange(gb):
    n = len(gb)

    def body(*refs):
        g_refs, o_refs = refs[:n], refs[n:2 * n]
        send_sem, recv_sem = refs[2 * n:]
        x, y, c, _ = _mesh_pos()
        cps = []
        for t in range(n):
            _, rows, cols, kind = gb[t]
            for chip in range(N_CHIP):
                out_w = _window(g_refs[t], rows, cols, kind, chip, 1 - c)
                dst_w = _window(o_refs[t], rows, cols, kind, chip, 1 - c)
                cp = pltpu.make_async_remote_copy(out_w, dst_w, send_sem.at[N_CHIP * t + chip], recv_sem.at[N_CHIP * t + chip],
                                                  (x, y, 1 - c), MESH)
                cp.start()
                cps.append(cp)
        for t in range(n):
            _, rows, cols, kind = gb[t]
            for chip in range(N_CHIP):
                w = _window(o_refs[t], rows, cols, kind, chip, c)
                pltpu.make_async_remote_copy(w, w, send_sem.at[N_CHIP * t + chip], recv_sem.at[N_CHIP * t + chip],
                                             (x, y, 1 - c), MESH).wait_recv()
        for cp in cps:
            cp.wait_send()

    return pl.pallas_call(
        body, out_shape=[jax.ShapeDtypeStruct(a.shape, BF16) for a, _, _, _ in gb], in_specs=[ANY] * n, out_specs=[ANY] * n,
        scratch_shapes=[pltpu.SemaphoreType.DMA((N_CHIP * n,))] * 2, name="grad_sibling_exchange")(*[a for a, _, _, _ in gb])


def _half_add(c_arr, g, sib, rows, cols, kind):
    sr, sc = _shard_dims(rows, cols, kind)
    hr = sr // 2

    def kern(c_ref, g_ref, s_ref, ob_ref, of_ref):
        v = g_ref[...] + s_ref[...].astype(F32)
        of_ref[...] = v
        ob_ref[...] = v.astype(BF16)

    if kind == "col":
        imap = lambda j, c_ref: (c_ref[0], j)
    else:
        imap = lambda j, c_ref: (2 * j + c_ref[0], 0)
    gs = pltpu.PrefetchScalarGridSpec(
        num_scalar_prefetch=1, grid=(N_CHIP,),
        in_specs=[pl.BlockSpec((hr, sc), imap), pl.BlockSpec((hr, sc), imap)],
        out_specs=[pl.BlockSpec((None, hr, sc), lambda j, c_ref: (j, 0, 0))] * 2)
    return pl.pallas_call(
        kern, out_shape=(jax.ShapeDtypeStruct((N_CHIP, hr, sc), BF16), jax.ShapeDtypeStruct((N_CHIP, hr, sc), F32)),
        grid_spec=gs, name="grad_half_add", compiler_params=_params("arbitrary"))(c_arr, g, sib)


def _chip_exchange(sbs):
    comm = _chip_exchange_comm(sbs)
    n = len(sbs)

    def body(*refs):
        comm.start(refs[:n], refs[n:2 * n], refs[2 * n:])
        comm.finish(refs[:n], refs[n:2 * n], refs[2 * n:])

    return pl.pallas_call(
        body, out_shape=comm.out_shape, in_specs=[ANY] * n, out_specs=[ANY] * n, scratch_shapes=comm.scratch,
        name="grad_chip_exchange")(*sbs)


def _chip_exchange_comm(sbs):
    n = len(sbs)

    def copies(s_refs, o_refs, sems):
        send_sem, recv_sem = sems
        x, y, c, j = _mesh_pos()
        return [pltpu.make_async_remote_copy(s_refs[t].at[j ^ _REL_BITS[r]], o_refs[t].at[r], send_sem.at[n * r + t],
                                             recv_sem.at[n * r + t], _rel_dev(x, y, c, r), MESH)
                for r in range(3) for t in range(n)]

    def start(s_refs, o_refs, sems):
        for cp in copies(s_refs, o_refs, sems):
            cp.start()

    def finish(s_refs, o_refs, sems):
        for cp in copies(s_refs, o_refs, sems):
            cp.wait()

    return _Hosted(sbs, [jax.ShapeDtypeStruct((3,) + a.shape[1:], BF16) for a in sbs], {},
                   [pltpu.SemaphoreType.DMA((3 * n,))] * 2, start, finish)


def _owner_sum(jc_arr, sf, rb, l, into=None):
    _, hr, sc = sf.shape

    def kern(jc_ref, s_ref, r0_ref, r1_ref, r2_ref, *rest):
        o_ref = rest[-1]
        o_ref[...] = ((s_ref[...] + r0_ref[...].astype(F32)) + r1_ref[...].astype(F32)) + r2_ref[...].astype(F32)

    in_specs = [pl.BlockSpec((None, hr, sc), lambda i, jc_ref: (jc_ref[0], 0, 0))]
    in_specs += [pl.BlockSpec((None, hr, sc), lambda i, jc_ref, r=r: (r, 0, 0)) for r in range(3)]
    args = [jc_arr, sf, rb, rb, rb]
    aliases = {}
    if into is not None:
        in_specs.append(ANY)
        args.append(into)
        aliases = {len(args) - 1: 0}
    gs = pltpu.PrefetchScalarGridSpec(
        num_scalar_prefetch=1, grid=(1,), in_specs=in_specs,
        out_specs=pl.BlockSpec((None, hr, sc), lambda i, jc_ref: (l, jc_ref[1], 0)))
    return pl.pallas_call(kern, out_shape=jax.ShapeDtypeStruct((DEPTH, 2 * hr, sc), F32), grid_spec=gs,
                          input_output_aliases=aliases, name="grad_owner_sum", compiler_params=_params("arbitrary"))(*args)


def _sibling_assemble(grads):
    n = len(grads)

    def body(*refs):
        o_refs = refs[n:2 * n]
        send_sem, recv_sem = refs[2 * n:]
        x, y, c, _ = _mesh_pos()
        cps = []
        for q in range(n):
            hr = grads[q].shape[1] // 2
            for l in range(DEPTH):
                mine = o_refs[q].at[l, pl.ds(c * hr, hr), :]
                cp = pltpu.make_async_remote_copy(mine, mine, send_sem.at[DEPTH * q + l], recv_sem.at[DEPTH * q + l],
                                                  (x, y, 1 - c), MESH)
                cp.start()
                cps.append(cp)
        for q in range(n):
            hr = grads[q].shape[1] // 2
            for l in range(DEPTH):
                theirs = o_refs[q].at[l, pl.ds((1 - c) * hr, hr), :]
                pltpu.make_async_remote_copy(theirs, theirs, send_sem.at[DEPTH * q + l], recv_sem.at[DEPTH * q + l],
                                             (x, y, 1 - c), MESH).wait_recv()
        for cp in cps:
            cp.wait_send()

    return pl.pallas_call(
        body, out_shape=[jax.ShapeDtypeStruct(g.shape, F32) for g in grads], in_specs=[ANY] * n, out_specs=[ANY] * n,
        input_output_aliases={q: q for q in range(n)},
        scratch_shapes=[pltpu.SemaphoreType.DMA((DEPTH * n,))] * 2, name="grad_sibling_assemble")(*grads)


def _adamw_math(w, g, m, v):
    m = ADAM_B1 * m + (1.0 - ADAM_B1) * g
    v = ADAM_B2 * v + (1.0 - ADAM_B2) * jnp.square(g)
    m_hat = m / (1.0 - ADAM_B1 ** ADAM_STEP)
    v_hat = v / (1.0 - ADAM_B2 ** ADAM_STEP)
    delta = -ADAM_LR * (m_hat / (jnp.sqrt(v_hat) + ADAM_EPS) + ADAM_WD * w)
    return delta, m, v


def _adamw(w, g, m, v):
    shape = w.shape
    C = shape[-1]
    R = int(np.prod(shape[:-1]))
    tr = min(256, R)
    args = [a.reshape(R, C) for a in (w, g, m, v)]

    def kern(w_ref, g_ref, m_ref, v_ref, go_ref, d_ref, nm_ref, nv_ref):
        g_val = g_ref[...]
        d, nm, nv = _adamw_math(w_ref[...], g_val, m_ref[...], v_ref[...])
        go_ref[...] = g_val
        d_ref[...] = d
        nm_ref[...] = nm
        nv_ref[...] = nv

    spec = pl.BlockSpec((tr, C), lambda i: (i, 0))
    outs = pl.pallas_call(
        kern, out_shape=[jax.ShapeDtypeStruct((R, C), F32)] * 4, grid=(R // tr,), in_specs=[spec] * 4, out_specs=[spec] * 4,
        name="adamw", compiler_params=_params("parallel"))(*args)
    return [o.reshape(shape) for o in outs]


_ROW_G_MIX, _ROW_B_GATES, _ROW_SINKS, _ROW_CONV, _ROW_G_MLP, _ROW_G_FINAL, _ROW_LOSS = 0, 2, 6, 8, 16, 18, 19


def _small_step(parts, params, moms, vels):
    names = ["g_mix", "b_gates", "sinks", "conv_w", "conv_b", "g_mlp", "g_final"]
    D = D_MODEL
    QW = D // N_CHIP
    n_dev = 8

    def body(*refs):
        it = iter(refs)
        dgmix = [next(it) for _ in range(DEPTH)]
        dbg = [next(it) for _ in range(DEPTH)]
        dsk = [next(it) for _ in range(DEPTH)]
        dwb = [next(it) for _ in range(DEPTH)]
        dgmlp = [next(it) for _ in range(DEPTH)]
        lst = next(it)
        p_refs = {n: next(it) for n in names}
        m_refs = {n: next(it) for n in names}
        v_refs = {n: next(it) for n in names}
        loss_ref = next(it)
        outs = {n: [next(it) for _ in range(4)] for n in names}
        pack_ref, all_ref, send_sem, recv_sem = next(it), next(it), next(it), next(it)

        x, y, c, j = _mesh_pos()
        me = 4 * x + 2 * y + c
        pack_ref[...] = jnp.zeros_like(pack_ref)
        for l in range(DEPTH):
            pack_ref[_ROW_G_MIX + l:_ROW_G_MIX + l + 1, :] = dgmix[l][...]
            pack_ref[_ROW_B_GATES + 2 * l:_ROW_B_GATES + 2 * l + 1, :] = dbg[l][:, 0:D]
            pack_ref[_ROW_B_GATES + 2 * l + 1:_ROW_B_GATES + 2 * l + 2, :] = dbg[l][:, D:2 * D]
            pack_ref[_ROW_SINKS + l:_ROW_SINKS + l + 1, 0:128] = dsk[l][0:1, :]
            pack_ref[_ROW_CONV + 4 * l:_ROW_CONV + 4 * l + 4, :] = dwb[l][0:4, :]
            pack_ref[_ROW_G_MLP + l:_ROW_G_MLP + l + 1, :] = dgmlp[l][...]
        pack_ref[_ROW_G_FINAL:_ROW_G_FINAL + 1, :] = lst[0:1, :]
        pack_ref[_ROW_LOSS:_ROW_LOSS + 1, :] = lst[1:2, :]

        all_ref[me] = pack_ref[...]
        cps = []
        for k in range(1, n_dev):
            dx_, dy_, dc_ = (k >> 2) & 1, (k >> 1) & 1, k & 1
            peer = (x ^ dx_, y ^ dy_, c ^ dc_)
            cp = pltpu.make_async_remote_copy(pack_ref, all_ref.at[me], send_sem.at[k - 1], recv_sem.at[k - 1], peer, MESH)
            cp.start()
            cps.append(cp)
        for cp in cps:
            cp.wait()

        tot = all_ref[0]
        for d in range(1, n_dev):
            tot = tot + all_ref[d]
        pack_ref[...] = tot

        loss_ref[...] = pack_ref[_ROW_LOSS:_ROW_LOSS + 1, 0:1]

        def finish(name, idx, g):
            w, m, v = p_refs[name][idx], m_refs[name][idx], v_refs[name][idx]
            d, nm, nv = _adamw_math(w, g, m, v)
            for ref, val in zip(outs[name], (g, d, nm, nv)):
                ref[idx] = val

        for l in range(DEPTH):
            finish("g_mix", (slice(l, l + 1), slice(None)), pack_ref[_ROW_G_MIX + l:_ROW_G_MIX + l + 1, :])
            finish("g_mlp", (slice(l, l + 1), slice(None)), pack_ref[_ROW_G_MLP + l:_ROW_G_MLP + l + 1, :])
            finish("conv_b", (slice(l, l + 1), slice(None)), pack_ref[_ROW_CONV + 4 * l + 3:_ROW_CONV + 4 * l + 4, :])
            finish("sinks", (slice(l, l + 1), slice(None)), pack_ref[_ROW_SINKS + l:_ROW_SINKS + l + 1, 0:N_Q_HEADS])
            for hf in range(2):
                finish("b_gates", (slice(l, l + 1), slice(hf * D, (hf + 1) * D)),
                       pack_ref[_ROW_B_GATES + 2 * l + hf:_ROW_B_GATES + 2 * l + hf + 1, :])
        finish("g_final", (slice(0, 1), slice(None)), pack_ref[_ROW_G_FINAL:_ROW_G_FINAL + 1, :])

        def conv_w_chip(js):
            for l in range(DEPTH):
                for k in range(3):
                    row = _ROW_CONV + 4 * l + k
                    finish("conv_w", (l, slice(k, k + 1), slice(None)), pack_ref[row:row + 1, js * QW:(js + 1) * QW])

        _for_my_chip(j, conv_w_chip)

    vm = pl.BlockSpec(memory_space=pltpu.VMEM)
    ins = (parts["g_mix"] + parts["b_gates"] + parts["sinks"] + parts["conv"] + parts["g_mlp"] + [parts["loss"]]
           + [params[n] for n in names] + [moms[n] for n in names] + [vels[n] for n in names])
    out_shape = [jax.ShapeDtypeStruct((1, 1), F32)]
    for n in names:
        out_shape += [jax.ShapeDtypeStruct(params[n].shape, F32)] * 4
    res = pl.pallas_call(
        body, out_shape=out_shape, in_specs=[vm] * len(ins), out_specs=[vm] * len(out_shape),
        scratch_shapes=[pltpu.VMEM((SMALL_ROWS, D), F32), pltpu.VMEM((n_dev, SMALL_ROWS, D), F32),
                        pltpu.SemaphoreType.DMA((n_dev - 1,)), pltpu.SemaphoreType.DMA((n_dev - 1,))],
        name="small_allreduce_adamw")(*ins)
    loss = res[0]
    out = {n: res[1 + 4 * i:5 + 4 * i] for i, n in enumerate(names)}
    return loss, out


def kernel(x, g_mix, w_in, b_gates, sinks, w_attn_out, conv_w, conv_b, w_conv_out, w_o, g_mlp, w_up, w_down, g_final, loss_target, m_g_mix, m_w_in, m_b_gates, m_sinks, m_w_attn_out, m_conv_w, m_conv_b, m_w_conv_out, m_w_o, m_g_mlp, m_w_up, m_w_down, m_g_final, v_g_mix, v_w_in, v_b_gates, v_sinks, v_w_attn_out, v_conv_w, v_conv_b, v_w_conv_out, v_w_o, v_g_mlp, v_w_up, v_w_down, v_g_final):
    B, S, D = x.shape
    T = B * S
    big_w = dict(w_in=w_in, w_attn_out=w_attn_out, w_conv_out=w_conv_out, w_o=w_o, w_up=w_up, w_down=w_down)
    big_m = dict(w_in=m_w_in, w_attn_out=m_w_attn_out, w_conv_out=m_w_conv_out, w_o=m_w_o, w_up=m_w_up, w_down=m_w_down)
    big_v = dict(w_in=v_w_in, w_attn_out=v_w_attn_out, w_conv_out=v_w_conv_out, w_o=v_w_o, w_up=v_w_up, w_down=v_w_down)

    c_arr = lax.axis_index("c").astype(jnp.int32).reshape(1)
    j_arr = (2 * lax.axis_index("x") + lax.axis_index("y")).astype(jnp.int32).reshape(1)
    jc_arr = jnp.concatenate([j_arr, c_arr])
    order = [(n, l) for n, _, _, _ in BIG for l in range(DEPTH)]
    dims = {n: (r, c_, k) for n, r, c_, k in BIG}

    full = {(n, l): _cast_into_full(j_arr, big_w[n], l, *dims[n]) for n, l in order}
    mixers = ("w_attn_out", "w_conv_out", "w_o")

    def gather_of(keys):
        return _gather_comm([(full[k],) + dims[k[0]] for k in keys])

    carried = {("proj", 0): [(n, 0) for n in mixers], ("attn", 0): [("w_up", 0), ("w_down", 0)],
               ("mlp", 0): [("w_in", 1)] + [(n, 1) for n in mixers], ("attn", 1): [("w_up", 1), ("w_down", 1)]}

    def carry(fn, where, *args):
        keys = carried.get(where)
        if keys is None:
            return fn(*args)
        res, got = fn(*args, comm=gather_of(keys))
        full.update(zip(keys, got))
        return res

    first = _gather([(full["w_in", 0],) + dims["w_in"]], conv_w)
    full["w_in", 0] = first[0]
    conv_w_full = jnp.transpose(first[1], (0, 2, 1, 3)).reshape(DEPTH, 3, D)
    attn_bias = _attn_bias_table()

    xs = [x.reshape(T, D)]
    saved = []
    for l in range(DEPTH):
        ht, pqkv, pconv, pgate = carry(_norm_proj, ("proj", l), xs[-1], g_mix[l:l + 1], full["w_in", l])
        att, att_t, lse = carry(_attn_fwd, ("attn", l), pqkv, sinks[l], attn_bias, S)
        cv, cv_t = _conv_fwd(pconv, conv_w_full[l], conv_b[l:l + 1], S)
        x1, ya, yc, mg_t = _mix_fwd(xs[-1], att, cv, pgate, b_gates[l:l + 1], full["w_attn_out", l], full["w_conv_out", l],
                                    full["w_o", l])
        x2, a = carry(_mlp_fwd, ("mlp", l), x1, g_mlp[l:l + 1], full["w_up", l], full["w_down", l])
        saved.append(dict(ht=ht, pqkv=pqkv, pconv=pconv, pgate=pgate, att=att, att_t=att_t, lse=lse, cv_t=cv_t, x1=x1, ya=ya,
                          yc=yc, mg_t=mg_t, a=a))
        xs.append(x2)

    loss_stats, dx = _loss_bwd(xs[-1], g_final.reshape(1, D), loss_target.reshape(T, D))

    parts = dict(g_mix=[None] * DEPTH, b_gates=[None] * DEPTH, sinks=[None] * DEPTH, conv=[None] * DEPTH,
                 g_mlp=[None] * DEPTH, loss=loss_stats)
    gf, gb = {}, {}
    pre, got = {}, {}

    def pre_reduce(keys):
        sib = _sibling_exchange([(gb[k],) + dims[k[0]] for k in keys])
        for k, s in zip(keys, sib):
            pre[k] = _half_add(c_arr, gf[k], s, *dims[k[0]])
        return keys

    def exchange_in(fn, keys, *args):
        res, arrived = fn(*args, comm=_chip_exchange_comm([pre[k][0] for k in keys]))
        got.update(zip(keys, arrived))
        return res

    pending = None
    for l in reversed(range(DEPTH)):
        W = {n: full[(n, l)] for n in big_w}
        sv = saved[l]
        mlp_args = (dx, sv["x1"], sv["a"], g_mlp[l:l + 1], W["w_up"], W["w_down"])
        dx1, da, u_t, h2_t, parts["g_mlp"][l] = exchange_in(_mlp_bwd, pending, *mlp_args) if pending else _mlp_bwd(*mlp_args)
        gf["w_up", l], gb["w_up", l] = _dw(h2_t, da)
        gf["w_down", l], gb["w_down", l] = _dw(u_t, dx)
        datt, dcv, dya, dyc, dgate, parts["b_gates"][l] = _mix_bwd(
            dx1, sv["ya"], sv["yc"], sv["pgate"], b_gates[l:l + 1], W["w_attn_out"], W["w_conv_out"], W["w_o"])
        gf["w_o", l], gb["w_o", l] = _dw(sv["mg_t"], dx1)
        gf["w_attn_out", l], gb["w_attn_out", l] = _dw(sv["att_t"], dya)
        gf["w_conv_out", l], gb["w_conv_out", l] = _dw(sv["cv_t"], dyc)
        dconv, parts["conv"][l] = _conv_bwd(dcv, sv["pconv"], conv_w_full[l], conv_b[l:l + 1], S)
        attn_args = (sv["pqkv"], datt, sv["lse"], sinks[l], attn_bias, S)
        last = l == 0
        if last:
            keys = pre_reduce([(n, l) for n in mixers + ("w_up", "w_down")])
            dq, dkv, parts["sinks"][l] = exchange_in(_attn_bwd, keys, *attn_args)
        else:
            dq, dkv, parts["sinks"][l] = _attn_bwd(*attn_args)
        pieces = [(dq, D), (dkv, QKV_W - D), (dconv, CONV_W), (dgate, GATE_W)]
        gf["w_in", l], gb["w_in", l] = _dw_pieces(sv["ht"], pieces)
        in_args = (pieces, W["w_in"], xs[l], dx1, g_mix[l:l + 1])
        if last:
            dx, parts["g_mix"][l] = exchange_in(_inproj_bwd, pre_reduce([("w_in", l)]), *in_args)
        else:
            dx, parts["g_mix"][l] = _inproj_bwd(*in_args)
            pending = pre_reduce([(n, l) for n, _, _, _ in BIG])

    mine = {}
    for n, l in order:
        mine[n] = _owner_sum(jc_arr, pre[n, l][1], got[n, l], l, mine.get(n))
    grads = dict(zip(mine, _sibling_assemble(list(mine.values()))))

    res = {}
    for n in big_w:
        res[n] = tuple(_adamw(big_w[n], grads[n], big_m[n], big_v[n]))

    small_p = dict(g_mix=g_mix, b_gates=b_gates, sinks=sinks, conv_w=conv_w, conv_b=conv_b, g_mlp=g_mlp, g_final=g_final.reshape(1, D))
    small_m = dict(g_mix=m_g_mix, b_gates=m_b_gates, sinks=m_sinks, conv_w=m_conv_w, conv_b=m_conv_b, g_mlp=m_g_mlp,
                   g_final=m_g_final.reshape(1, D))
    small_v = dict(g_mix=v_g_mix, b_gates=v_b_gates, sinks=v_sinks, conv_w=v_conv_w, conv_b=v_conv_b, g_mlp=v_g_mlp,
                   g_final=v_g_final.reshape(1, D))
    loss, small = _small_step(parts, small_p, small_m, small_v)
    for n, vals in small.items():
        res[n] = tuple(v.reshape(D) for v in vals) if n == "g_final" else tuple(vals)

    weights = ["g_mix", "w_in", "b_gates", "sinks", "w_attn_out", "conv_w", "conv_b", "w_conv_out", "w_o", "g_mlp", "w_up",
               "w_down", "g_final"]
    out = [loss.reshape(()), dx.reshape(B, S, D)]
    for k in range(4):
        out += [res[n][k] for n in weights]
    return tuple(out)
```

```python
import functools

import numpy as np
import jax
import jax.numpy as jnp
from jax import lax
from jax.experimental import pallas as pl
from jax.experimental.pallas import tpu as pltpu

F32 = jnp.float32
BF16 = jnp.bfloat16

D_MODEL = 1024
HEAD_DIM = 64
N_Q_HEADS = 16
N_KV_HEADS = 4
GQA_GROUP = 4
WINDOW = 128
D_FF = 4096
DEPTH = 2
RMS_EPS = 1e-6
NEG_INF = -1e30
ATTN_SCALE = HEAD_DIM ** -0.5
QKV_W = 1536
CONV_W = 3072
GATE_W = 2048
IN_COLS = QKV_W + CONV_W + GATE_W
COL_TILE = 512
N_CHIP = 4
ADAM_LR = 0.001
ADAM_B1 = 0.9
ADAM_B2 = 0.999
ADAM_EPS = 1e-08
ADAM_WD = 0.01
ADAM_STEP = 10
V7X_VMEM_BYTES = 64 * 2 ** 20
VMEM_LIMIT = V7X_VMEM_BYTES - 8 * 2 ** 20
MESH = pl.DeviceIdType.MESH
ANY = pl.BlockSpec(memory_space=pl.ANY)
SMALL_ROWS = 24

_SLOPES = [float(v) for v in np.power(np.float32(2.0), -8.0 * np.arange(1, N_Q_HEADS + 1, dtype=np.float32) / N_Q_HEADS)]


def _params(*sem):
    return pltpu.CompilerParams(dimension_semantics=sem, vmem_limit_bytes=VMEM_LIMIT)


class _Hosted:
    def __init__(self, inputs, out_shape, aliases, scratch, start, finish):
        self.inputs, self.out_shape, self.aliases, self.scratch = list(inputs), list(out_shape), dict(aliases), list(scratch)
        self.start, self.finish = start, finish


def _hosted_call(comm, kern, *, out_shape, grid, in_specs, out_specs, args, name, sem, scratch_shapes=()):
    single = not isinstance(out_shape, (tuple, list))
    outs = [out_shape] if single else list(out_shape)
    ospecs = [out_specs] if single else list(out_specs)
    if comm is None:
        res = pl.pallas_call(kern, out_shape=outs, grid=grid, in_specs=list(in_specs), out_specs=ospecs,
                             scratch_shapes=list(scratch_shapes), name=name, compiler_params=_params(*sem))(*args)
        return res[0] if single else res
    n_in, n_out, n_scr = len(args), len(outs), len(scratch_shapes)
    ci, co, cs = len(comm.inputs), len(comm.out_shape), len(comm.scratch)

    def body(*refs):
        cuts = np.cumsum([0, n_in, ci, n_out, co, n_scr, cs])
        a, b, c, d, e, f = [refs[lo:hi] for lo, hi in zip(cuts[:-1], cuts[1:])]
        ids = [pl.program_id(k) for k in range(len(grid))]
        first = functools.reduce(jnp.logical_and, [i == 0 for i in ids])
        last = functools.reduce(jnp.logical_and, [i == n - 1 for i, n in zip(ids, grid)])
        pl.when(first)(lambda: comm.start(b, d, f))
        kern(*a, *c, *e)
        pl.when(last)(lambda: comm.finish(b, d, f))

    res = pl.pallas_call(
        body, out_shape=outs + comm.out_shape, grid=grid, in_specs=list(in_specs) + [ANY] * ci, out_specs=ospecs + [ANY] * co,
        scratch_shapes=list(scratch_shapes) + comm.scratch,
        input_output_aliases={n_in + i: n_out + o for i, o in comm.aliases.items()},
        name=name + "_carrier", compiler_params=_params(*(["arbitrary"] * len(grid))))(*args, *comm.inputs)
    main = res[:n_out]
    return (main[0] if single else main), res[n_out:]


def _nt(a, b):
    return lax.dot_general(a, b, (((1,), (1,)), ((), ())), preferred_element_type=F32)


def _tn(a, b):
    return lax.dot_general(a, b, (((0,), (0,)), ((), ())), preferred_element_type=F32)


def _nn(a, b):
    return jnp.dot(a, b, preferred_element_type=F32)


def _rms_stats(xf):
    r = lax.rsqrt(jnp.mean(xf * xf, axis=-1, keepdims=True) + RMS_EPS)
    return r, xf * r


def _rms_bwd(dh, xh, r, g):
    dxh = dh * g
    dx = r * (dxh - xh * jnp.mean(dxh * xh, axis=-1, keepdims=True))
    dg = jnp.sum(dh * xh, axis=0, keepdims=True)
    return dx, dg


def _dw_chunk(T):
    return min(2048, T)


def _row_halves(tm):
    return (slice(0, tm // 2), slice(tm // 2, tm))


def _resident(shape):
    return pl.BlockSpec(shape, lambda *_: (0,) * len(shape), pipeline_mode=pl.Buffered(1))


def _norm_proj(x, g, w, comm=None):
    T, D = x.shape
    tm = min(256, T)
    tk = _dw_chunk(T)
    per = tk // tm
    widths = (QKV_W, CONV_W, GATE_W)

    def kern(x_ref, g_ref, w_ref, ht_ref, *o_refs):
        _, xh = _rms_stats(x_ref[...])
        h = (xh * g_ref[...]).astype(BF16)
        ht_ref[...] = h.T
        off = 0
        for o_ref, wd in zip(o_refs, widths):
            o_ref[...] = _nn(h, w_ref[:, off:off + wd]).astype(BF16)
            off += wd

    row = lambda wd: pl.BlockSpec((tm, wd), lambda i: (i, 0))
    return _hosted_call(
        comm, kern,
        out_shape=[jax.ShapeDtypeStruct((T // tk, D, tk), BF16)] + [jax.ShapeDtypeStruct((T, wd), BF16) for wd in widths],
        grid=(T // tm,), in_specs=[row(D), pl.BlockSpec((1, D), lambda i: (0, 0)), _resident((D, IN_COLS))],
        out_specs=[pl.BlockSpec((None, D, tm), lambda i: (i // per, 0, i % per))] + [row(wd) for wd in widths],
        name="norm_proj", sem=("parallel",), args=(x, g, w))


GW = GQA_GROUP * WINDOW
BAND = 2 * WINDOW
KV_W = N_KV_HEADS * HEAD_DIM


def _attn_bias_table():
    jj = np.arange(BAND)[:, None]
    col = np.arange(GW)[None, :]
    dist = WINDOW + (col % WINDOW) - jj
    valid = (dist >= 0) & (dist < WINDOW)
    slopes = np.asarray(_SLOPES, np.float32).reshape(N_KV_HEADS, GQA_GROUP)
    tab = np.empty((2, N_KV_HEADS, BAND, GW), np.float32)
    for hk in range(N_KV_HEADS):
        bias = -slopes[hk][col // WINDOW] * dist.astype(np.float32)
        tab[0, hk] = np.where(valid, bias, np.float32(NEG_INF))
        tab[1, hk] = np.where(valid & (jj >= WINDOW), bias, np.float32(NEG_INF))
    return jnp.asarray(tab)


def _stack_heads(ref, hk):
    return jnp.concatenate(
        [ref[:, HEAD_DIM * (GQA_GROUP * hk + g): HEAD_DIM * (GQA_GROUP * hk + g + 1)] for g in range(GQA_GROUP)], axis=0)


def _kv_band(cur_ref, prev_ref, hk):
    k0 = N_Q_HEADS * HEAD_DIM
    sl = slice(HEAD_DIM * hk, HEAD_DIM * (hk + 1))
    ksl, vsl = slice(k0 + sl.start, k0 + sl.stop), slice(k0 + KV_W + sl.start, k0 + KV_W + sl.stop)
    k_band = jnp.concatenate([prev_ref[:, sl], cur_ref[:, ksl]], axis=0)
    v_band = jnp.concatenate([prev_ref[:, KV_W + sl.start:KV_W + sl.stop], cur_ref[:, vsl]], axis=0)
    return k_band, v_band


def _lane_row(vals):
    return jnp.concatenate([jnp.full((1, WINDOW), v, F32) for v in vals], axis=1)


def _attn_fwd(pqkv, sinks, bias, seq, comm=None):
    T = pqkv.shape[0]
    nblk = seq // WINDOW

    def kern(sink_ref, cur_ref, prev_ref, bias_ref, o_ref, ot_ref, lse_ref):
        i = pl.program_id(0)
        first_i = ((i % nblk) == 0).astype(jnp.int32)
        bands = [_kv_band(cur_ref, prev_ref, hk) for hk in range(N_KV_HEADS)]
        sts = [_nt(bands[hk][0], _stack_heads(cur_ref, hk) * ATTN_SCALE) + bias_ref[first_i, hk] for hk in range(N_KV_HEADS)]
        ps, scales = [], []
        for hk in range(N_KV_HEADS):
            heads = [GQA_GROUP * hk + g for g in range(GQA_GROUP)]
            sink = _lane_row([sink_ref[h] for h in heads])
            m = jnp.maximum(jnp.max(sts[hk], axis=0, keepdims=True), sink)
            p = jnp.exp(sts[hk] - m)
            den = jnp.sum(p, axis=0, keepdims=True) + jnp.exp(sink - m)
            lse = m + jnp.log(den)
            for g, h in enumerate(heads):
                lse_ref[h:h + 1, :] = lse[:, WINDOW * g:WINDOW * (g + 1)]
            ps.append(p.astype(BF16))
            scales.append(1.0 / den)
        for hk in range(N_KV_HEADS):
            ot = _tn(bands[hk][1], ps[hk]) * scales[hk]
            for g in range(GQA_GROUP):
                h = GQA_GROUP * hk + g
                ot_ref[HEAD_DIM * h:HEAD_DIM * (h + 1), :] = ot[:, WINDOW * g:WINDOW * (g + 1)].astype(BF16)
        o_ref[...] = ot_ref[...].T

    return _hosted_call(
        comm, kern,
        out_shape=(jax.ShapeDtypeStruct((T, D_MODEL), BF16), jax.ShapeDtypeStruct((D_MODEL, T), BF16),
                   jax.ShapeDtypeStruct((N_Q_HEADS, T), F32)),
        grid=(T // WINDOW,),
        in_specs=[pl.BlockSpec(memory_space=pltpu.SMEM),
                  pl.BlockSpec((WINDOW, QKV_W), lambda i: (i, 0)),
                  pl.BlockSpec((WINDOW, 2 * KV_W), lambda i: (jnp.maximum(i - 1, 0), 2)),
                  _resident(bias.shape)],
        out_specs=(pl.BlockSpec((WINDOW, D_MODEL), lambda i: (i, 0)), pl.BlockSpec((D_MODEL, WINDOW), lambda i: (0, i)),
                   pl.BlockSpec((N_Q_HEADS, WINDOW), lambda i: (0, i))),
        name="attn_fwd", sem=("parallel",), args=(sinks, pqkv, pqkv, bias))


def _pick_row(a, row):
    rid = lax.broadcasted_iota(jnp.int32, a.shape, 0)
    return jnp.sum(jnp.where(rid == row, a, 0.0), axis=0, keepdims=True)


def _conv_taps(yc, halo_yc, first_i):
    keep = (1 - first_i).astype(F32)
    p1 = _pick_row(halo_yc, 15) * keep
    p2 = _pick_row(halo_yc, 14) * keep
    rowid = lax.broadcasted_iota(jnp.int32, yc.shape, 0)
    s1 = jnp.where(rowid == 0, p1, pltpu.roll(yc, 1, 0))
    s2 = jnp.where(rowid == 0, p2, jnp.where(rowid == 1, p1, pltpu.roll(yc, 2, 0)))
    return s1, s2


def _conv_fwd(pconv, conv_w, conv_b, seq):
    T = pconv.shape[0]
    tm = min(256, seq)
    per_seq = seq // tm
    D = D_MODEL

    def kern(cur_ref, halo_ref, w_ref, b_ref, o_ref, ot_ref):
        i = pl.program_id(0)
        first_i = ((i % per_seq) == 0).astype(jnp.int32)
        cb = cur_ref[:, 0:D].astype(F32)
        yc = cur_ref[:, D:2 * D].astype(F32) * cur_ref[:, 2 * D:3 * D].astype(F32)
        halo_yc = halo_ref[:, D:2 * D].astype(F32) * halo_ref[:, 2 * D:3 * D].astype(F32)
        s1, s2 = _conv_taps(yc, halo_yc, first_i)
        z = w_ref[0:1, :] * s2 + w_ref[1:2, :] * s1 + w_ref[2:3, :] * yc
        cv = (cb * (z + b_ref[...])).astype(BF16)
        o_ref[...] = cv
        ot_ref[...] = cv.T

    return pl.pallas_call(
        kern, out_shape=(jax.ShapeDtypeStruct((T, D), BF16), jax.ShapeDtypeStruct((D, T), BF16)), grid=(T // tm,),
        in_specs=[pl.BlockSpec((tm, CONV_W), lambda i: (i, 0)),
                  pl.BlockSpec((16, CONV_W), lambda i: (jnp.maximum(i * (tm // 16) - 1, 0), 0)),
                  pl.BlockSpec((3, D), lambda i: (0, 0)), pl.BlockSpec((1, D), lambda i: (0, 0))],
        out_specs=(pl.BlockSpec((tm, D), lambda i: (i, 0)), pl.BlockSpec((D, tm), lambda i: (0, i))),
        name="conv_fwd", compiler_params=_params("parallel"))(pconv, pconv, conv_w, conv_b)


def _mix_fwd(x, att, cv, pgate, b_gates, wao, wco, wo):
    T, D = x.shape
    tm = min(512, T)

    halves = _row_halves(tm)

    def kern(x_ref, att_ref, cv_ref, pg_ref, bg_ref, wao_ref, wco_ref, wo_ref, x1_ref, ya_ref, yc_ref, mgt_ref):
        yas = [_nn(att_ref[rows, :], wao_ref[...]) for rows in halves]
        ycs = [_nn(cv_ref[rows, :], wco_ref[...]) for rows in halves]
        mgs = []
        for rows, ya, yc in zip(halves, yas, ycs):
            sa = jax.nn.sigmoid(pg_ref[rows, 0:D].astype(F32) + bg_ref[:, 0:D])
            sc = jax.nn.sigmoid(pg_ref[rows, D:2 * D].astype(F32) + bg_ref[:, D:2 * D])
            mg = (sa * ya + sc * yc).astype(BF16)
            ya_ref[rows, :] = ya.astype(BF16)
            yc_ref[rows, :] = yc.astype(BF16)
            mgt_ref[:, rows] = mg.T
            mgs.append(mg)
        for rows, mg in zip(halves, mgs):
            x1_ref[rows, :] = x_ref[rows, :] + _nn(mg, wo_ref[...])

    row = lambda w: pl.BlockSpec((tm, w), lambda i: (i, 0))
    full = lambda a, b: pl.BlockSpec((a, b), lambda i: (0, 0))
    bf = jax.ShapeDtypeStruct((T, D), BF16)
    return pl.pallas_call(
        kern, out_shape=(jax.ShapeDtypeStruct((T, D), F32), bf, bf, jax.ShapeDtypeStruct((D, T), BF16)), grid=(T // tm,),
        in_specs=[row(D), row(D), row(D), row(GATE_W), full(1, GATE_W)] + [_resident((D, D))] * 3,
        out_specs=(row(D), row(D), row(D), pl.BlockSpec((D, tm), lambda i: (0, i))),
        name="mix_fwd", compiler_params=_params("parallel"))(x, att, cv, pgate, b_gates, wao, wco, wo)


def _mlp_fwd(x1, g, wup, wdn, comm=None):
    T, D = x1.shape
    tm = min(1024, T)
    nj = D_FF // D

    def kern(x_ref, g_ref, wup_ref, wdn_ref, x2_ref, a_ref, h_scr, acc_scr):
        j = pl.program_id(1)

        @pl.when(j == 0)
        def _():
            xf = x_ref[...]
            _, xh = _rms_stats(xf)
            h_scr[...] = (xh * g_ref[...]).astype(BF16)
            acc_scr[...] = xf

        a = _nn(h_scr[...], wup_ref[j])
        a_ref[...] = a.astype(BF16)
        u = jnp.square(jnp.maximum(a, 0.0)).astype(BF16)
        acc_scr[...] += _nn(u, wdn_ref[j])

        @pl.when(j == nj - 1)
        def _():
            x2_ref[...] = acc_scr[...]

    return _hosted_call(
        comm, kern, out_shape=(jax.ShapeDtypeStruct((T, D), F32), jax.ShapeDtypeStruct((T, D_FF), BF16)), grid=(T // tm, nj),
        in_specs=[pl.BlockSpec((tm, D), lambda i, j: (i, 0)), pl.BlockSpec((1, D), lambda i, j: (0, 0)),
                  _resident((nj, D, D)), _resident((nj, D, D))],
        out_specs=(pl.BlockSpec((tm, D), lambda i, j: (i, 0)), pl.BlockSpec((tm, D), lambda i, j: (i, j))),
        scratch_shapes=[pltpu.VMEM((tm, D), BF16), pltpu.VMEM((tm, D), F32)],
        name="mlp_fwd", sem=("parallel", "arbitrary"), args=(x1, g, wup, wdn))


def _loss_bwd(x, g, tgt):
    T, D = x.shape
    tm = min(512, T)

    def kern(x_ref, g_ref, t_ref, st_ref, dx_ref):
        i = pl.program_id(0)

        @pl.when(i == 0)
        def _():
            st_ref[...] = jnp.zeros_like(st_ref)

        gg = g_ref[...]
        r, xh = _rms_stats(x_ref[...])
        e = xh * gg - t_ref[...]
        part = 0.5 * jnp.sum(jnp.mean(e * e, axis=-1, keepdims=True), axis=0, keepdims=True)
        dx, dg = _rms_bwd(e * (1.0 / D), xh, r, gg)
        dx_ref[...] = dx
        st_ref[0:1, :] += dg
        st_ref[1:2, 0:1] += part

    return pl.pallas_call(
        kern, out_shape=(jax.ShapeDtypeStruct((8, D), F32), jax.ShapeDtypeStruct((T, D), F32)), grid=(T // tm,),
        in_specs=[pl.BlockSpec((tm, D), lambda i: (i, 0)), pl.BlockSpec((1, D), lambda i: (0, 0)),
                  pl.BlockSpec((tm, D), lambda i: (i, 0))],
        out_specs=(pl.BlockSpec((8, D), lambda i: (0, 0)), pl.BlockSpec((tm, D), lambda i: (i, 0))),
        name="loss_bwd", compiler_params=_params("arbitrary"))(x, g, tgt)


def _mlp_bwd(dx2, x1, a, g, wup, wdn, comm=None):
    T, D = x1.shape
    tm = min(512, T)
    nj = D_FF // D
    tk = _dw_chunk(T)
    per = tk // tm

    def kern(dx2_ref, x1_ref, a_ref, g_ref, wup_ref, wdn_ref, dx1_ref, da_ref, ut_ref, h2t_ref, dyb_ref, dg_ref, acc_scr):
        i, j = pl.program_id(0), pl.program_id(1)

        @pl.when((i == 0) & (j == 0))
        def _():
            dg_ref[...] = jnp.zeros_like(dg_ref)

        @pl.when(j == 0)
        def _():
            dyb_ref[...] = dx2_ref[...].astype(BF16)
            acc_scr[...] = jnp.zeros_like(acc_scr)

        du = _nt(dyb_ref[...], wdn_ref[j])
        relu = jnp.maximum(a_ref[...].astype(F32), 0.0)
        da = (du * (2.0 * relu)).astype(BF16)
        da_ref[...] = da
        ut_ref[...] = jnp.square(relu).astype(BF16).T
        acc_scr[...] += _nt(da, wup_ref[j])

        @pl.when(j == nj - 1)
        def _():
            gg = g_ref[...]
            r, xh = _rms_stats(x1_ref[...])
            h2t_ref[...] = (xh * gg).astype(BF16).T
            dx, dg = _rms_bwd(acc_scr[...], xh, r, gg)
            dx1_ref[...] = dx2_ref[...] + dx
            dg_ref[...] += dg

    return _hosted_call(
        comm, kern,
        out_shape=(jax.ShapeDtypeStruct((T, D), F32), jax.ShapeDtypeStruct((T, D_FF), BF16),
                   jax.ShapeDtypeStruct((D_FF, T), BF16), jax.ShapeDtypeStruct((T // tk, D, tk), BF16),
                   jax.ShapeDtypeStruct((T, D), BF16), jax.ShapeDtypeStruct((1, D), F32)),
        grid=(T // tm, nj),
        in_specs=[pl.BlockSpec((tm, D), lambda i, j: (i, 0)), pl.BlockSpec((tm, D), lambda i, j: (i, 0)),
                  pl.BlockSpec((tm, D), lambda i, j: (i, j)), pl.BlockSpec((1, D), lambda i, j: (0, 0)),
                  _resident((nj, D, D)), _resident((nj, D, D))],
        out_specs=(pl.BlockSpec((tm, D), lambda i, j: (i, 0)), pl.BlockSpec((tm, D), lambda i, j: (i, j)),
                   pl.BlockSpec((D, tm), lambda i, j: (j, i)), pl.BlockSpec((None, D, tm), lambda i, j: (i // per, 0, i % per)),
                   pl.BlockSpec((tm, D), lambda i, j: (i, 0)), pl.BlockSpec((1, D), lambda i, j: (0, 0))),
        scratch_shapes=[pltpu.VMEM((tm, D), F32)],
        name="mlp_bwd", sem=("arbitrary", "arbitrary"), args=(dx2, x1, a, g, wup, wdn))


def _mix_bwd(dx1, ya, yc, pgate, b_gates, wao, wco, wo):
    T, D = dx1.shape
    tm = min(512, T)
    halves = _row_halves(tm)

    def kern(dx_ref, ya_ref, yc_ref, pg_ref, bg_ref, wao_ref, wco_ref, wo_ref,
             datt_ref, dcv_ref, dya_ref, dyc_ref, dgt_ref, dbg_ref):
        @pl.when(pl.program_id(0) == 0)
        def _():
            dbg_ref[...] = jnp.zeros_like(dbg_ref)

        dms = [_nt(dx_ref[rows, :].astype(BF16), wo_ref[...]) for rows in halves]
        dys = []
        for rows, dm in zip(halves, dms):
            sa = jax.nn.sigmoid(pg_ref[rows, 0:D].astype(F32) + bg_ref[:, 0:D])
            sc = jax.nn.sigmoid(pg_ref[rows, D:2 * D].astype(F32) + bg_ref[:, D:2 * D])
            dya = (dm * sa).astype(BF16)
            dyc = (dm * sc).astype(BF16)
            dga = dm * ya_ref[rows, :].astype(F32) * (sa * (1.0 - sa))
            dgc = dm * yc_ref[rows, :].astype(F32) * (sc * (1.0 - sc))
            dya_ref[rows, :] = dya
            dyc_ref[rows, :] = dyc
            dgt_ref[rows, 0:D] = dga.astype(BF16)
            dgt_ref[rows, D:2 * D] = dgc.astype(BF16)
            dbg_ref[:, 0:D] += jnp.sum(dga, axis=0, keepdims=True)
            dbg_ref[:, D:2 * D] += jnp.sum(dgc, axis=0, keepdims=True)
            dys.append((dya, dyc))
        for rows, (dya, dyc) in zip(halves, dys):
            datt_ref[rows, :] = _nt(dya, wao_ref[...]).astype(BF16)
            dcv_ref[rows, :] = _nt(dyc, wco_ref[...]).astype(BF16)

    row = lambda w: pl.BlockSpec((tm, w), lambda i: (i, 0))
    full = lambda a, b: pl.BlockSpec((a, b), lambda i: (0, 0))
    bf = jax.ShapeDtypeStruct((T, D), BF16)
    return pl.pallas_call(
        kern,
        out_shape=(bf, bf, bf, bf, jax.ShapeDtypeStruct((T, GATE_W), BF16), jax.ShapeDtypeStruct((1, GATE_W), F32)),
        grid=(T // tm,),
        in_specs=[row(D), row(D), row(D), row(GATE_W), full(1, GATE_W)] + [_resident((D, D))] * 3,
        out_specs=(row(D), row(D), row(D), row(D), row(GATE_W), full(1, GATE_W)),
        name="mix_bwd", compiler_params=_params("arbitrary"))(dx1, ya, yc, pgate, b_gates, wao, wco, wo)


def _conv_bwd(dcv, pconv, conv_w, conv_b, seq):
    T = pconv.shape[0]
    tm = min(256, seq)
    per_seq = seq // tm
    D = D_MODEL
    nb16 = T // 16

    def kern(dcv_ref, dcvn_ref, cur_ref, prev_ref, next_ref, w_ref, b_ref, o_ref, dwb_ref):
        i = pl.program_id(0)

        @pl.when(i == 0)
        def _():
            dwb_ref[...] = jnp.zeros_like(dwb_ref)

        first_i = ((i % per_seq) == 0).astype(jnp.int32)
        keep_next = 1.0 - (((i + 1) % per_seq) == 0).astype(F32)
        cb = cur_ref[:, 0:D].astype(F32)
        cc = cur_ref[:, D:2 * D].astype(F32)
        cu = cur_ref[:, 2 * D:3 * D].astype(F32)
        yc = cc * cu
        halo_yc = prev_ref[:, D:2 * D].astype(F32) * prev_ref[:, 2 * D:3 * D].astype(F32)
        s1, s2 = _conv_taps(yc, halo_yc, first_i)
        w0, w1, w2 = w_ref[0:1, :], w_ref[1:2, :], w_ref[2:3, :]
        z = w0 * s2 + w1 * s1 + w2 * yc
        dcv = dcv_ref[...].astype(F32)
        dz = dcv * cb
        dzn = dcvn_ref[...].astype(F32) * next_ref[:, 0:D].astype(F32) * keep_next
        n1, n2 = _pick_row(dzn, 0), _pick_row(dzn, 1)
        rowid = lax.broadcasted_iota(jnp.int32, dz.shape, 0)
        u1 = jnp.where(rowid == tm - 1, n1, pltpu.roll(dz, tm - 1, 0))
        u2 = jnp.where(rowid == tm - 1, n2, jnp.where(rowid == tm - 2, n1, pltpu.roll(dz, tm - 2, 0)))
        dyc = w2 * dz + w1 * u1 + w0 * u2
        o_ref[:, 0:D] = (dcv * (z + b_ref[...])).astype(BF16)
        o_ref[:, D:2 * D] = (dyc * cu).astype(BF16)
        o_ref[:, 2 * D:3 * D] = (dyc * cc).astype(BF16)
        dwb_ref[0:1, :] += jnp.sum(dz * s2, axis=0, keepdims=True)
        dwb_ref[1:2, :] += jnp.sum(dz * s1, axis=0, keepdims=True)
        dwb_ref[2:3, :] += jnp.sum(dz * yc, axis=0, keepdims=True)
        dwb_ref[3:4, :] += jnp.sum(dz, axis=0, keepdims=True)

    prev_map = lambda i: (jnp.maximum(i * (tm // 16) - 1, 0), 0)
    next_map = lambda i: (jnp.minimum((i + 1) * (tm // 16), nb16 - 1), 0)
    return pl.pallas_call(
        kern, out_shape=(jax.ShapeDtypeStruct((T, CONV_W), BF16), jax.ShapeDtypeStruct((8, D), F32)), grid=(T // tm,),
        in_specs=[pl.BlockSpec((tm, D), lambda i: (i, 0)), pl.BlockSpec((16, D), next_map),
                  pl.BlockSpec((tm, CONV_W), lambda i: (i, 0)), pl.BlockSpec((16, CONV_W), prev_map),
                  pl.BlockSpec((16, CONV_W), next_map),
                  pl.BlockSpec((3, D), lambda i: (0, 0)), pl.BlockSpec((1, D), lambda i: (0, 0))],
        out_specs=(pl.BlockSpec((tm, CONV_W), lambda i: (i, 0)), pl.BlockSpec((8, D), lambda i: (0, 0))),
        name="conv_bwd", compiler_params=_params("arbitrary"))(dcv, dcv, pconv, pconv, pconv, conv_w, conv_b)


def _attn_bwd(pqkv, datt, lse, sinks, bias, seq, comm=None):
    T = pqkv.shape[0]
    nblk = seq // WINDOW
    nseq = T // seq
    KVW = KV_W

    def kern(sink_ref, cur_ref, prev_ref, do_ref, lse_ref, bias_ref, dq_ref, dkv_ref, ds_ref, kc_scr, vc_scr, dqt_scr):
        b, st = pl.program_id(0), pl.program_id(1)

        @pl.when((b == 0) & (st == 0))
        def _():
            ds_ref[...] = jnp.zeros_like(ds_ref)

        @pl.when(st == 0)
        def _():
            kc_scr[...] = jnp.zeros_like(kc_scr)
            vc_scr[...] = jnp.zeros_like(vc_scr)

        @pl.when(st < nblk)
        def _():
            first_i = (st == 0).astype(jnp.int32)
            groups = range(N_KV_HEADS)
            bands = [_kv_band(cur_ref, prev_ref, hk) for hk in groups]
            qs = [_stack_heads(cur_ref, hk) for hk in groups]
            dos = [_stack_heads(do_ref, hk) for hk in groups]
            sts = [_nt(bands[hk][0], qs[hk] * ATTN_SCALE) + bias_ref[first_i, hk] for hk in groups]
            dps = [_nt(bands[hk][1], dos[hk]) for hk in groups]
            pbs, dsss = [], []
            for hk in groups:
                heads = [GQA_GROUP * hk + g for g in range(GQA_GROUP)]
                sink = _lane_row([sink_ref[h] for h in heads])
                lse_g = jnp.concatenate([lse_ref[h:h + 1, :] for h in heads], axis=1)
                p = jnp.exp(sts[hk] - lse_g)
                d_row = jnp.sum(p * dps[hk], axis=0, keepdims=True)
                dsss.append((p * (dps[hk] - d_row) * ATTN_SCALE).astype(BF16))
                pbs.append(p.astype(BF16))
                psd = jnp.exp(sink - lse_g) * d_row
                for g, h in enumerate(heads):
                    ds_ref[0:1, h:h + 1] -= jnp.sum(psd[:, WINDOW * g:WINDOW * (g + 1)], axis=1, keepdims=True)
            for hk in groups:
                dqt = _tn(bands[hk][0], dsss[hk])
                dk_b = _nn(dsss[hk], qs[hk])
                dv_b = _nn(pbs[hk], dos[hk])
                for g in range(GQA_GROUP):
                    h = GQA_GROUP * hk + g
                    dqt_scr[HEAD_DIM * h:HEAD_DIM * (h + 1), :] = dqt[:, WINDOW * g:WINDOW * (g + 1)].astype(BF16)
                ksl = slice(HEAD_DIM * hk, HEAD_DIM * (hk + 1))
                vsl = slice(KVW + HEAD_DIM * hk, KVW + HEAD_DIM * (hk + 1))
                dkv_ref[:, ksl] = (kc_scr[:, ksl] + dk_b[0:WINDOW]).astype(BF16)
                dkv_ref[:, vsl] = (vc_scr[:, ksl] + dv_b[0:WINDOW]).astype(BF16)
                kc_scr[:, ksl] = dk_b[WINDOW:2 * WINDOW]
                vc_scr[:, ksl] = dv_b[WINDOW:2 * WINDOW]
            dq_ref[...] = dqt_scr[...].T

        @pl.when(st == nblk)
        def _():
            dkv_ref[:, 0:KVW] = kc_scr[...].astype(BF16)
            dkv_ref[:, KVW:2 * KVW] = vc_scr[...].astype(BF16)

    cur_map = lambda b, s: (b * nblk + jnp.minimum(s, nblk - 1), 0)
    prev_row = lambda b, s: b * nblk + jnp.clip(s - 1, 0, nblk - 1)
    return _hosted_call(
        comm, kern,
        out_shape=(jax.ShapeDtypeStruct((T, D_MODEL), BF16), jax.ShapeDtypeStruct((T, 2 * KVW), BF16),
                   jax.ShapeDtypeStruct((8, 128), F32)),
        grid=(nseq, nblk + 1),
        in_specs=[pl.BlockSpec(memory_space=pltpu.SMEM),
                  pl.BlockSpec((WINDOW, QKV_W), cur_map),
                  pl.BlockSpec((WINDOW, 2 * KVW), lambda b, s: (prev_row(b, s), 2)),
                  pl.BlockSpec((WINDOW, D_MODEL), cur_map),
                  pl.BlockSpec((N_Q_HEADS, WINDOW), lambda b, s: (0, b * nblk + jnp.minimum(s, nblk - 1))),
                  _resident(bias.shape)],
        out_specs=(pl.BlockSpec((WINDOW, D_MODEL), cur_map),
                   pl.BlockSpec((WINDOW, 2 * KVW), lambda b, s: (prev_row(b, s), 0)),
                   pl.BlockSpec((8, 128), lambda b, s: (0, 0))),
        scratch_shapes=[pltpu.VMEM((WINDOW, KVW), F32), pltpu.VMEM((WINDOW, KVW), F32), pltpu.VMEM((D_MODEL, WINDOW), BF16)],
        name="attn_bwd", sem=("arbitrary", "arbitrary"), args=(sinks, pqkv, pqkv, datt, lse, bias))


def _piece_tiles(pieces):
    out, start = [], 0
    for arr, width in pieces:
        out.append((arr, start, width // COL_TILE))
        start += width // COL_TILE
    return out, start


def _inproj_bwd(pieces, w_in, x, dx_in, g, comm=None):
    T, D = x.shape
    tm = min(256, T)

    def kern(*refs):
        p_refs = refs[:len(pieces)]
        w_ref, x_ref, dxin_ref, g_ref, dx_ref, dg_ref = refs[len(pieces):]

        @pl.when(pl.program_id(0) == 0)
        def _():
            dg_ref[...] = jnp.zeros_like(dg_ref)

        dh, off = None, 0
        for p_ref, (_, width) in zip(p_refs, pieces):
            part = _nt(p_ref[...], w_ref[:, off:off + width])
            dh = part if dh is None else dh + part
            off += width
        gg = g_ref[...]
        r, xh = _rms_stats(x_ref[...])
        dx, dg = _rms_bwd(dh, xh, r, gg)
        dx_ref[...] = dxin_ref[...] + dx
        dg_ref[...] += dg

    row = lambda wd: pl.BlockSpec((tm, wd), lambda i: (i, 0))
    return _hosted_call(
        comm, kern, out_shape=(jax.ShapeDtypeStruct((T, D), F32), jax.ShapeDtypeStruct((1, D), F32)), grid=(T // tm,),
        in_specs=[row(wd) for _, wd in pieces] + [_resident((D, IN_COLS)), row(D), row(D), pl.BlockSpec((1, D), lambda i: (0, 0))],
        out_specs=(row(D), pl.BlockSpec((1, D), lambda i: (0, 0))),
        name="inproj_bwd", sem=("arbitrary",), args=(*[a for a, _ in pieces], w_in, x, dx_in, g))


def _dw_pieces(lhs_t, pieces):
    nt, K, tk = lhs_t.shape
    tiles, nj = _piece_tiles(pieces)

    def kern(*refs):
        lhs_ref = refs[0]
        p_refs = refs[1:1 + len(tiles)]
        o_ref, ob_ref = refs[1 + len(tiles):]
        j, t = pl.program_id(0), pl.program_id(1)

        @pl.when(t == 0)
        def _():
            o_ref[...] = jnp.zeros_like(o_ref)

        for p_ref, (_, start, n) in zip(p_refs, tiles):
            @pl.when((j >= start) & (j < start + n))
            def _(p_ref=p_ref):
                o_ref[...] += _nn(lhs_ref[t], p_ref[...])

        @pl.when(t == nt - 1)
        def _():
            ob_ref[...] = o_ref[...].astype(BF16)

    def p_map(start, n):
        return lambda j, t: (jnp.where((j >= start) & (j < start + n), t, 0), jnp.clip(j - start, 0, n - 1))

    N = nj * COL_TILE
    return pl.pallas_call(
        kern, out_shape=(jax.ShapeDtypeStruct((K, N), F32), jax.ShapeDtypeStruct((K, N), BF16)), grid=(nj, nt),
        in_specs=[_resident((nt, K, tk))] + [pl.BlockSpec((tk, COL_TILE), p_map(s, n)) for _, s, n in tiles],
        out_specs=(pl.BlockSpec((K, COL_TILE), lambda j, t: (0, j)), pl.BlockSpec((K, COL_TILE), lambda j, t: (0, j))),
        name="dw_pieces", compiler_params=_params("arbitrary", "arbitrary"))(lhs_t, *[a for a, _, _ in tiles])


def _dw(lhs_t, rhs):
    lhs_res, rhs_res = lhs_t.ndim == 3, rhs.ndim == 3
    W = D_MODEL
    if lhs_res:
        nt, K, tk = lhs_t.shape
    else:
        K, tk = lhs_t.shape[0], _dw_chunk(lhs_t.shape[1])
        nt = lhs_t.shape[1] // tk
    N = rhs.shape[-1]

    def kern(lhs_ref, rhs_ref, o_ref, ob_ref):
        t = pl.program_id(2)

        @pl.when(t == 0)
        def _():
            o_ref[...] = jnp.zeros_like(o_ref)

        a = lhs_ref[t] if lhs_res else lhs_ref[...]
        b = rhs_ref[t] if rhs_res else rhs_ref[...]
        o_ref[...] += _nn(a, b.astype(BF16))

        @pl.when(t == nt - 1)
        def _():
            ob_ref[...] = o_ref[...].astype(BF16)

    omap = lambda i, j, t: (i, j)
    lspec = _resident((nt, W, tk)) if lhs_res else pl.BlockSpec((W, tk), lambda i, j, t: (i, t))
    rspec = _resident((nt, tk, W)) if rhs_res else pl.BlockSpec((tk, W), lambda i, j, t: (t, j))
    return pl.pallas_call(
        kern, out_shape=(jax.ShapeDtypeStruct((K, N), F32), jax.ShapeDtypeStruct((K, N), BF16)), grid=(K // W, N // W, nt),
        in_specs=[lspec, rspec], out_specs=(pl.BlockSpec((W, W), omap), pl.BlockSpec((W, W), omap)),
        name="dw", compiler_params=_params("arbitrary", "arbitrary", "arbitrary"))(lhs_t, rhs)


BIG = (("w_in", D_MODEL, IN_COLS, "col"), ("w_attn_out", D_MODEL, D_MODEL, "row"), ("w_conv_out", D_MODEL, D_MODEL, "row"),
       ("w_o", D_MODEL, D_MODEL, "row"), ("w_up", D_MODEL, D_FF, "col"), ("w_down", D_FF, D_MODEL, "row"))


def _shard_dims(rows, cols, kind):
    return (rows, cols // N_CHIP) if kind in ("col", "chip") else (rows // N_CHIP, cols)


def _window(ref, rows, cols, kind, chip, half):
    sr, sc = _shard_dims(rows, cols, kind)
    hr = sr // 2
    if kind == "col":
        return ref.at[pl.ds(half * hr, hr), pl.ds(chip * sc, sc)]
    if kind == "chip":
        return ref.at[chip, pl.ds(half * hr, hr), :]
    return ref.at[pl.ds(chip * sr + half * hr, hr), :]


def _mesh_pos():
    x, y, c = lax.axis_index("x"), lax.axis_index("y"), lax.axis_index("c")
    return x, y, c, 2 * x + y


_REL_BITS = (2, 1, 3)


def _rel_dev(x, y, c, r):
    return ((1 - x, y, c), (x, 1 - y, c), (1 - x, 1 - y, c))[r]


def _for_my_chip(j, fn):
    for js in range(N_CHIP):
        pl.when(j == js)(functools.partial(fn, js))


def _cast_into_full(j_arr, shard, l, rows, cols, kind):
    sr, sc = _shard_dims(rows, cols, kind)
    tr = min(256, sr)

    def kern(j_ref, s_ref, o_ref):
        o_ref[...] = s_ref[...].astype(BF16)

    shape, block = (rows, cols), (tr, sc)
    if kind == "col":
        omap = lambda i, j_ref: (i, j_ref[0])
    elif kind == "chip":
        shape, block = (N_CHIP, rows, sc), (None, tr, sc)
        omap = lambda i, j_ref: (j_ref[0], i, 0)
    else:
        omap = lambda i, j_ref: (j_ref[0] * (sr // tr) + i, 0)
    gs = pltpu.PrefetchScalarGridSpec(
        num_scalar_prefetch=1, grid=(sr // tr,),
        in_specs=[pl.BlockSpec((None, tr, sc), lambda i, j_ref: (l, i, 0))], out_specs=pl.BlockSpec(block, omap))
    return pl.pallas_call(kern, out_shape=jax.ShapeDtypeStruct(shape, BF16), grid_spec=gs, name="cast_into_full",
                          compiler_params=_params("arbitrary"))(j_arr, shard)


def _gather(fulls, cw=None):
    comm = _gather_comm(fulls, cw)
    n = len(comm.inputs)

    def body(*refs):
        comm.start(refs[:n], refs[n:2 * n], refs[2 * n:])
        comm.finish(refs[:n], refs[n:2 * n], refs[2 * n:])

    return pl.pallas_call(
        body, out_shape=comm.out_shape, in_specs=[ANY] * n, out_specs=[ANY] * n, input_output_aliases=comm.aliases,
        scratch_shapes=comm.scratch, name="gather_weights")(*comm.inputs)


def _gather_comm(fulls, cw=None):
    n_big = len(fulls)
    n_piece = n_big + (0 if cw is None else 1)

    def pieces(in_refs, o_refs, js, c):
        def piece(p, chip, half):
            if p == n_big:
                return o_refs[p].at[half, chip]
            _, rows, cols, kind = fulls[p]
            return _window(o_refs[p], rows, cols, kind, chip, half)

        def mine(p):
            return in_refs[p].at[c] if p == n_big else piece(p, js, c)

        return piece, mine

    def local_copies(in_refs, piece, js, loc_sem):
        if cw is None:
            return []
        return [pltpu.make_async_copy(in_refs[n_big].at[half], piece(n_big, js, half), loc_sem.at[half]) for half in range(2)]

    def ici_copy(piece, mine, js, x, y, c, r, p, send_sem, recv_sem):
        return pltpu.make_async_remote_copy(mine(p), piece(p, js, c), send_sem.at[r * n_piece + p], recv_sem.at[r * n_piece + p],
                                            _rel_dev(x, y, c, r), MESH)

    def start(in_refs, o_refs, sems):
        send_sem, recv_sem, _, _, loc_sem = sems
        x, y, c, j = _mesh_pos()

        def run(js):
            piece, mine = pieces(in_refs, o_refs, js, c)
            for cp in local_copies(in_refs, piece, js, loc_sem):
                cp.start()
            for r in range(3):
                for p in range(n_piece):
                    ici_copy(piece, mine, js, x, y, c, r, p, send_sem, recv_sem).start()

        _for_my_chip(j, run)

    def finish(in_refs, o_refs, sems):
        send_sem, recv_sem, fsend_sem, frecv_sem, loc_sem = sems
        x, y, c, j = _mesh_pos()

        def run(js):
            piece, mine = pieces(in_refs, o_refs, js, c)
            fwds = []
            for r in range(3):
                ks = js ^ _REL_BITS[r]
                for p in range(n_piece):
                    got = piece(p, ks, c)
                    pltpu.make_async_remote_copy(got, got, send_sem.at[r * n_piece + p], recv_sem.at[r * n_piece + p],
                                                 _rel_dev(x, y, c, r), MESH).wait_recv()
                    cp = pltpu.make_async_remote_copy(got, got, fsend_sem.at[r * n_piece + p], frecv_sem.at[r * n_piece + p],
                                                      (x, y, 1 - c), MESH)
                    cp.start()
                    fwds.append(cp)
            for r in range(3):
                ks = js ^ _REL_BITS[r]
                for p in range(n_piece):
                    got = piece(p, ks, 1 - c)
                    pltpu.make_async_remote_copy(got, got, fsend_sem.at[r * n_piece + p], frecv_sem.at[r * n_piece + p],
                                                 (x, y, 1 - c), MESH).wait_recv()
            for r in range(3):
                for p in range(n_piece):
                    ici_copy(piece, mine, js, x, y, c, r, p, send_sem, recv_sem).wait_send()
            for cp in fwds:
                cp.wait_send()
            for cp in local_copies(in_refs, piece, js, loc_sem):
                cp.wait()

        _for_my_chip(j, run)

    out_shape = [jax.ShapeDtypeStruct(a.shape, BF16) for a, _, _, _ in fulls]
    ins = [a for a, _, _, _ in fulls]
    if cw is not None:
        out_shape.append(jax.ShapeDtypeStruct((DEPTH, N_CHIP, 3, D_MODEL // N_CHIP), F32))
        ins.append(cw)
    scratch = [pltpu.SemaphoreType.DMA((3 * n_piece,))] * 4 + [pltpu.SemaphoreType.DMA((2,))]
    return _Hosted(ins, out_shape, {p: p for p in range(n_big)}, scratch, start, finish)


def _sibling_exchange(gb):
    n = len(gb)

    def body(*refs):
        g_refs, o_refs = refs[:n], refs[n:2 * n]
        send_sem, recv_sem = refs[2 * n:]
        x, y, c, _ = _mesh_pos()
        cps = []
        for t in range(n):
            _, rows, cols, kind = gb[t]
            for chip in range(N_CHIP):
                out_w = _window(g_refs[t], rows, cols, kind, chip, 1 - c)
                dst_w = _window(o_refs[t], rows, cols, kind, chip, 1 - c)
                cp = pltpu.make_async_remote_copy(out_w, dst_w, send_sem.at[N_CHIP * t + chip], recv_sem.at[N_CHIP * t + chip],
                                                  (x, y, 1 - c), MESH)
                cp.start()
                cps.append(cp)
        for t in range(n):
            _, rows, cols, kind = gb[t]
            for chip in range(N_CHIP):
                w = _window(o_refs[t], rows, cols, kind, chip, c)
                pltpu.make_async_remote_copy(w, w, send_sem.at[N_CHIP * t + chip], recv_sem.at[N_CHIP * t + chip],
                                             (x, y, 1 - c), MESH).wait_recv()
        for cp in cps:
            cp.wait_send()

    return pl.pallas_call(
        body, out_shape=[jax.ShapeDtypeStruct(a.shape, BF16) for a, _, _, _ in gb], in_specs=[ANY] * n, out_specs=[ANY] * n,
        scratch_shapes=[pltpu.SemaphoreType.DMA((N_CHIP * n,))] * 2, name="grad_sibling_exchange")(*[a for a, _, _, _ in gb])


def _half_add(c_arr, g, sib, rows, cols, kind):
    sr, sc = _shard_dims(rows, cols, kind)
    hr = sr // 2

    def kern(c_ref, g_ref, s_ref, ob_ref, of_ref):
        v = g_ref[...] + s_ref[...].astype(F32)
        of_ref[...] = v
        ob_ref[...] = v.astype(BF16)

    if kind == "col":
        imap = lambda j, c_ref: (c_ref[0], j)
    else:
        imap = lambda j, c_ref: (2 * j + c_ref[0], 0)
    gs = pltpu.PrefetchScalarGridSpec(
        num_scalar_prefetch=1, grid=(N_CHIP,),
        in_specs=[pl.BlockSpec((hr, sc), imap), pl.BlockSpec((hr, sc), imap)],
        out_specs=[pl.BlockSpec((None, hr, sc), lambda j, c_ref: (j, 0, 0))] * 2)
    return pl.pallas_call(
        kern, out_shape=(jax.ShapeDtypeStruct((N_CHIP, hr, sc), BF16), jax.ShapeDtypeStruct((N_CHIP, hr, sc), F32)),
        grid_spec=gs, name="grad_half_add", compiler_params=_params("arbitrary"))(c_arr, g, sib)


def _chip_exchange(sbs):
    comm = _chip_exchange_comm(sbs)
    n = len(sbs)

    def body(*refs):
        comm.start(refs[:n], refs[n:2 * n], refs[2 * n:])
        comm.finish(refs[:n], refs[n:2 * n], refs[2 * n:])

    return pl.pallas_call(
        body, out_shape=comm.out_shape, in_specs=[ANY] * n, out_specs=[ANY] * n, scratch_shapes=comm.scratch,
        name="grad_chip_exchange")(*sbs)


def _chip_exchange_comm(sbs):
    n = len(sbs)

    def copies(s_refs, o_refs, sems):
        send_sem, recv_sem = sems
        x, y, c, j = _mesh_pos()
        return [pltpu.make_async_remote_copy(s_refs[t].at[j ^ _REL_BITS[r]], o_refs[t].at[r], send_sem.at[n * r + t],
                                             recv_sem.at[n * r + t], _rel_dev(x, y, c, r), MESH)
                for r in range(3) for t in range(n)]

    def start(s_refs, o_refs, sems):
        for cp in copies(s_refs, o_refs, sems):
            cp.start()

    def finish(s_refs, o_refs, sems):
        for cp in copies(s_refs, o_refs, sems):
            cp.wait()

    return _Hosted(sbs, [jax.ShapeDtypeStruct((3,) + a.shape[1:], BF16) for a in sbs], {},
                   [pltpu.SemaphoreType.DMA((3 * n,))] * 2, start, finish)


def _owner_sum(jc_arr, sf, rb, l, into=None):
    _, hr, sc = sf.shape

    def kern(jc_ref, s_ref, r0_ref, r1_ref, r2_ref, *rest):
        o_ref = rest[-1]
        o_ref[...] = ((s_ref[...] + r0_ref[...].astype(F32)) + r1_ref[...].astype(F32)) + r2_ref[...].astype(F32)

    in_specs = [pl.BlockSpec((None, hr, sc), lambda i, jc_ref: (jc_ref[0], 0, 0))]
    in_specs += [pl.BlockSpec((None, hr, sc), lambda i, jc_ref, r=r: (r, 0, 0)) for r in range(3)]
    args = [jc_arr, sf, rb, rb, rb]
    aliases = {}
    if into is not None:
        in_specs.append(ANY)
        args.append(into)
        aliases = {len(args) - 1: 0}
    gs = pltpu.PrefetchScalarGridSpec(
        num_scalar_prefetch=1, grid=(1,), in_specs=in_specs,
        out_specs=pl.BlockSpec((None, hr, sc), lambda i, jc_ref: (l, jc_ref[1], 0)))
    return pl.pallas_call(kern, out_shape=jax.ShapeDtypeStruct((DEPTH, 2 * hr, sc), F32), grid_spec=gs,
                          input_output_aliases=aliases, name="grad_owner_sum", compiler_params=_params("arbitrary"))(*args)


def _sibling_assemble(grads):
    n = len(grads)

    def body(*refs):
        o_refs = refs[n:2 * n]
        send_sem, recv_sem = refs[2 * n:]
        x, y, c, _ = _mesh_pos()
        cps = []
        for q in range(n):
            hr = grads[q].shape[1] // 2
            for l in range(DEPTH):
                mine = o_refs[q].at[l, pl.ds(c * hr, hr), :]
                cp = pltpu.make_async_remote_copy(mine, mine, send_sem.at[DEPTH * q + l], recv_sem.at[DEPTH * q + l],
                                                  (x, y, 1 - c), MESH)
                cp.start()
                cps.append(cp)
        for q in range(n):
            hr = grads[q].shape[1] // 2
            for l in range(DEPTH):
                theirs = o_refs[q].at[l, pl.ds((1 - c) * hr, hr), :]
                pltpu.make_async_remote_copy(theirs, theirs, send_sem.at[DEPTH * q + l], recv_sem.at[DEPTH * q + l],
                                             (x, y, 1 - c), MESH).wait_recv()
        for cp in cps:
            cp.wait_send()

    return pl.pallas_call(
        body, out_shape=[jax.ShapeDtypeStruct(g.shape, F32) for g in grads], in_specs=[ANY] * n, out_specs=[ANY] * n,
        input_output_aliases={q: q for q in range(n)},
        scratch_shapes=[pltpu.SemaphoreType.DMA((DEPTH * n,))] * 2, name="grad_sibling_assemble")(*grads)


def _adamw_math(w, g, m, v):
    m = ADAM_B1 * m + (1.0 - ADAM_B1) * g
    v = ADAM_B2 * v + (1.0 - ADAM_B2) * jnp.square(g)
    m_hat = m / (1.0 - ADAM_B1 ** ADAM_STEP)
    v_hat = v / (1.0 - ADAM_B2 ** ADAM_STEP)
    delta = -ADAM_LR * (m_hat / (jnp.sqrt(v_hat) + ADAM_EPS) + ADAM_WD * w)
    return delta, m, v


def _adamw(w, g, m, v):
    shape = w.shape
    C = shape[-1]
    R = int(np.prod(shape[:-1]))
    tr = min(256, R)
    args = [a.reshape(R, C) for a in (w, g, m, v)]

    def kern(w_ref, g_ref, m_ref, v_ref, go_ref, d_ref, nm_ref, nv_ref):
        g_val = g_ref[...]
        d, nm, nv = _adamw_math(w_ref[...], g_val, m_ref[...], v_ref[...])
        go_ref[...] = g_val
        d_ref[...] = d
        nm_ref[...] = nm
        nv_ref[...] = nv

    spec = pl.BlockSpec((tr, C), lambda i: (i, 0))
    outs = pl.pallas_call(
        kern, out_shape=[jax.ShapeDtypeStruct((R, C), F32)] * 4, grid=(R // tr,), in_specs=[spec] * 4, out_specs=[spec] * 4,
        name="adamw", compiler_params=_params("parallel"))(*args)
    return [o.reshape(shape) for o in outs]


_ROW_G_MIX, _ROW_B_GATES, _ROW_SINKS, _ROW_CONV, _ROW_G_MLP, _ROW_G_FINAL, _ROW_LOSS = 0, 2, 6, 8, 16, 18, 19


def _small_step(parts, params, moms, vels):
    names = ["g_mix", "b_gates", "sinks", "conv_w", "conv_b", "g_mlp", "g_final"]
    D = D_MODEL
    QW = D // N_CHIP
    n_dev = 8

    def body(*refs):
        it = iter(refs)
        dgmix = [next(it) for _ in range(DEPTH)]
        dbg = [next(it) for _ in range(DEPTH)]
        dsk = [next(it) for _ in range(DEPTH)]
        dwb = [next(it) for _ in range(DEPTH)]
        dgmlp = [next(it) for _ in range(DEPTH)]
        lst = next(it)
        p_refs = {n: next(it) for n in names}
        m_refs = {n: next(it) for n in names}
        v_refs = {n: next(it) for n in names}
        loss_ref = next(it)
        outs = {n: [next(it) for _ in range(4)] for n in names}
        pack_ref, all_ref, send_sem, recv_sem = next(it), next(it), next(it), next(it)

        x, y, c, j = _mesh_pos()
        me = 4 * x + 2 * y + c
        pack_ref[...] = jnp.zeros_like(pack_ref)
        for l in range(DEPTH):
            pack_ref[_ROW_G_MIX + l:_ROW_G_MIX + l + 1, :] = dgmix[l][...]
            pack_ref[_ROW_B_GATES + 2 * l:_ROW_B_GATES + 2 * l + 1, :] = dbg[l][:, 0:D]
            pack_ref[_ROW_B_GATES + 2 * l + 1:_ROW_B_GATES + 2 * l + 2, :] = dbg[l][:, D:2 * D]
            pack_ref[_ROW_SINKS + l:_ROW_SINKS + l + 1, 0:128] = dsk[l][0:1, :]
            pack_ref[_ROW_CONV + 4 * l:_ROW_CONV + 4 * l + 4, :] = dwb[l][0:4, :]
            pack_ref[_ROW_G_MLP + l:_ROW_G_MLP + l + 1, :] = dgmlp[l][...]
        pack_ref[_ROW_G_FINAL:_ROW_G_FINAL + 1, :] = lst[0:1, :]
        pack_ref[_ROW_LOSS:_ROW_LOSS + 1, :] = lst[1:2, :]

        all_ref[me] = pack_ref[...]
        cps = []
        for k in range(1, n_dev):
            dx_, dy_, dc_ = (k >> 2) & 1, (k >> 1) & 1, k & 1
            peer = (x ^ dx_, y ^ dy_, c ^ dc_)
            cp = pltpu.make_async_remote_copy(pack_ref, all_ref.at[me], send_sem.at[k - 1], recv_sem.at[k - 1], peer, MESH)
            cp.start()
            cps.append(cp)
        for cp in cps:
            cp.wait()

        tot = all_ref[0]
        for d in range(1, n_dev):
            tot = tot + all_ref[d]
        pack_ref[...] = tot

        loss_ref[...] = pack_ref[_ROW_LOSS:_ROW_LOSS + 1, 0:1]

        def finish(name, idx, g):
            w, m, v = p_refs[name][idx], m_refs[name][idx], v_refs[name][idx]
            d, nm, nv = _adamw_math(w, g, m, v)
            for ref, val in zip(outs[name], (g, d, nm, nv)):
                ref[idx] = val

        for l in range(DEPTH):
            finish("g_mix", (slice(l, l + 1), slice(None)), pack_ref[_ROW_G_MIX + l:_ROW_G_MIX + l + 1, :])
            finish("g_mlp", (slice(l, l + 1), slice(None)), pack_ref[_ROW_G_MLP + l:_ROW_G_MLP + l + 1, :])
            finish("conv_b", (slice(l, l + 1), slice(None)), pack_ref[_ROW_CONV + 4 * l + 3:_ROW_CONV + 4 * l + 4, :])
            finish("sinks", (slice(l, l + 1), slice(None)), pack_ref[_ROW_SINKS + l:_ROW_SINKS + l + 1, 0:N_Q_HEADS])
            for hf in range(2):
                finish("b_gates", (slice(l, l + 1), slice(hf * D, (hf + 1) * D)),
                       pack_ref[_ROW_B_GATES + 2 * l + hf:_ROW_B_GATES + 2 * l + hf + 1, :])
        finish("g_final", (slice(0, 1), slice(None)), pack_ref[_ROW_G_FINAL:_ROW_G_FINAL + 1, :])

        def conv_w_chip(js):
            for l in range(DEPTH):
                for k in range(3):
                    row = _ROW_CONV + 4 * l + k
                    finish("conv_w", (l, slice(k, k + 1), slice(None)), pack_ref[row:row + 1, js * QW:(js + 1) * QW])

        _for_my_chip(j, conv_w_chip)

    vm = pl.BlockSpec(memory_space=pltpu.VMEM)
    ins = (parts["g_mix"] + parts["b_gates"] + parts["sinks"] + parts["conv"] + parts["g_mlp"] + [parts["loss"]]
           + [params[n] for n in names] + [moms[n] for n in names] + [vels[n] for n in names])
    out_shape = [jax.ShapeDtypeStruct((1, 1), F32)]
    for n in names:
        out_shape += [jax.ShapeDtypeStruct(params[n].shape, F32)] * 4
    res = pl.pallas_call(
        body, out_shape=out_shape, in_specs=[vm] * len(ins), out_specs=[vm] * len(out_shape),
        scratch_shapes=[pltpu.VMEM((SMALL_ROWS, D), F32), pltpu.VMEM((n_dev, SMALL_ROWS, D), F32),
                        pltpu.SemaphoreType.DMA((n_dev - 1,)), pltpu.SemaphoreType.DMA((n_dev - 1,))],
        name="small_allreduce_adamw")(*ins)
    loss = res[0]
    out = {n: res[1 + 4 * i:5 + 4 * i] for i, n in enumerate(names)}
    return loss, out


def kernel(x, g_mix, w_in, b_gates, sinks, w_attn_out, conv_w, conv_b, w_conv_out, w_o, g_mlp, w_up, w_down, g_final, loss_target, m_g_mix, m_w_in, m_b_gates, m_sinks, m_w_attn_out, m_conv_w, m_conv_b, m_w_conv_out, m_w_o, m_g_mlp, m_w_up, m_w_down, m_g_final, v_g_mix, v_w_in, v_b_gates, v_sinks, v_w_attn_out, v_conv_w, v_conv_b, v_w_conv_out, v_w_o, v_g_mlp, v_w_up, v_w_down, v_g_final):
    B, S, D = x.shape
    T = B * S
    big_w = dict(w_in=w_in, w_attn_out=w_attn_out, w_conv_out=w_conv_out, w_o=w_o, w_up=w_up, w_down=w_down)
    big_m = dict(w_in=m_w_in, w_attn_out=m_w_attn_out, w_conv_out=m_w_conv_out, w_o=m_w_o, w_up=m_w_up, w_down=m_w_down)
    big_v = dict(w_in=v_w_in, w_attn_out=v_w_attn_out, w_conv_out=v_w_conv_out, w_o=v_w_o, w_up=v_w_up, w_down=v_w_down)

    c_arr = lax.axis_index("c").astype(jnp.int32).reshape(1)
    j_arr = (2 * lax.axis_index("x") + lax.axis_index("y")).astype(jnp.int32).reshape(1)
    jc_arr = jnp.concatenate([j_arr, c_arr])
    order = [(n, l) for n, _, _, _ in BIG for l in range(DEPTH)]
    dims = {n: (r, c_, k) for n, r, c_, k in BIG}

    wdims = dict(dims, w_up=(D, D_FF, "chip"))
    full = {(n, l): _cast_into_full(j_arr, big_w[n], l, *wdims[n]) for n, l in order}
    mixers = ("w_attn_out", "w_conv_out", "w_o")

    def gather_of(keys):
        return _gather_comm([(full[k],) + wdims[k[0]] for k in keys])

    carried = {("proj", 0): [(n, 0) for n in mixers], ("attn", 0): [("w_up", 0), ("w_down", 0)],
               ("mlp", 0): [("w_in", 1)] + [(n, 1) for n in mixers], ("attn", 1): [("w_up", 1), ("w_down", 1)]}

    def carry(fn, where, *args):
        keys = carried.get(where)
        if keys is None:
            return fn(*args)
        res, got = fn(*args, comm=gather_of(keys))
        full.update(zip(keys, got))
        return res

    first = _gather([(full["w_in", 0],) + dims["w_in"]], conv_w)
    full["w_in", 0] = first[0]
    conv_w_full = jnp.transpose(first[1], (0, 2, 1, 3)).reshape(DEPTH, 3, D)
    attn_bias = _attn_bias_table()

    xs = [x.reshape(T, D)]
    saved = []
    for l in range(DEPTH):
        ht, pqkv, pconv, pgate = carry(_norm_proj, ("proj", l), xs[-1], g_mix[l:l + 1], full["w_in", l])
        att, att_t, lse = carry(_attn_fwd, ("attn", l), pqkv, sinks[l], attn_bias, S)
        cv, cv_t = _conv_fwd(pconv, conv_w_full[l], conv_b[l:l + 1], S)
        x1, ya, yc, mg_t = _mix_fwd(xs[-1], att, cv, pgate, b_gates[l:l + 1], full["w_attn_out", l], full["w_conv_out", l],
                                    full["w_o", l])
        x2, a = carry(_mlp_fwd, ("mlp", l), x1, g_mlp[l:l + 1], full["w_up", l], full["w_down", l].reshape(N_CHIP, D, D))
        saved.append(dict(ht=ht, pqkv=pqkv, pconv=pconv, pgate=pgate, att=att, att_t=att_t, lse=lse, cv_t=cv_t, x1=x1, ya=ya,
                          yc=yc, mg_t=mg_t, a=a))
        xs.append(x2)

    loss_stats, dx = _loss_bwd(xs[-1], g_final.reshape(1, D), loss_target.reshape(T, D))

    parts = dict(g_mix=[None] * DEPTH, b_gates=[None] * DEPTH, sinks=[None] * DEPTH, conv=[None] * DEPTH,
                 g_mlp=[None] * DEPTH, loss=loss_stats)
    gf, gb = {}, {}
    pre, got = {}, {}

    def pre_reduce(keys):
        sib = _sibling_exchange([(gb[k],) + dims[k[0]] for k in keys])
        for k, s in zip(keys, sib):
            pre[k] = _half_add(c_arr, gf[k], s, *dims[k[0]])
        return keys

    def exchange_in(fn, keys, *args):
        res, arrived = fn(*args, comm=_chip_exchange_comm([pre[k][0] for k in keys]))
        got.update(zip(keys, arrived))
        return res

    pending = None
    for l in reversed(range(DEPTH)):
        W = {n: full[(n, l)] for n in big_w}
        sv = saved[l]
        mlp_args = (dx, sv["x1"], sv["a"], g_mlp[l:l + 1], W["w_up"], W["w_down"].reshape(N_CHIP, D, D))
        dx1, da, u_t, h2_t, dyb, parts["g_mlp"][l] = (exchange_in(_mlp_bwd, pending, *mlp_args) if pending
                                                      else _mlp_bwd(*mlp_args))
        gf["w_up", l], gb["w_up", l] = _dw(h2_t, da)
        gf["w_down", l], gb["w_down", l] = _dw(u_t, dyb.reshape(-1, _dw_chunk(T), D))
        datt, dcv, dya, dyc, dgate, parts["b_gates"][l] = _mix_bwd(
            dx1, sv["ya"], sv["yc"], sv["pgate"], b_gates[l:l + 1], W["w_attn_out"], W["w_conv_out"], W["w_o"])
        gf["w_o", l], gb["w_o", l] = _dw(sv["mg_t"], dx1)
        gf["w_attn_out", l], gb["w_attn_out", l] = _dw(sv["att_t"], dya)
        gf["w_conv_out", l], gb["w_conv_out", l] = _dw(sv["cv_t"], dyc)
        dconv, parts["conv"][l] = _conv_bwd(dcv, sv["pconv"], conv_w_full[l], conv_b[l:l + 1], S)
        attn_args = (sv["pqkv"], datt, sv["lse"], sinks[l], attn_bias, S)
        last = l == 0
        if last:
            keys = pre_reduce([(n, l) for n in mixers + ("w_up", "w_down")])
            dq, dkv, parts["sinks"][l] = exchange_in(_attn_bwd, keys, *attn_args)
        else:
            dq, dkv, parts["sinks"][l] = _attn_bwd(*attn_args)
        pieces = [(dq, D), (dkv, QKV_W - D), (dconv, CONV_W), (dgate, GATE_W)]
        gf["w_in", l], gb["w_in", l] = _dw_pieces(sv["ht"], pieces)
        in_args = (pieces, W["w_in"], xs[l], dx1, g_mix[l:l + 1])
        if last:
            dx, parts["g_mix"][l] = exchange_in(_inproj_bwd, pre_reduce([("w_in", l)]), *in_args)
        else:
            dx, parts["g_mix"][l] = _inproj_bwd(*in_args)
            pending = pre_reduce([(n, l) for n, _, _, _ in BIG])

    mine = {}
    for n, l in order:
        mine[n] = _owner_sum(jc_arr, pre[n, l][1], got[n, l], l, mine.get(n))
    grads = dict(zip(mine, _sibling_assemble(list(mine.values()))))

    res = {}
    for n in big_w:
        res[n] = tuple(_adamw(big_w[n], grads[n], big_m[n], big_v[n]))

    small_p = dict(g_mix=g_mix, b_gates=b_gates, sinks=sinks, conv_w=conv_w, conv_b=conv_b, g_mlp=g_mlp, g_final=g_final.reshape(1, D))
    small_m = dict(g_mix=m_g_mix, b_gates=m_b_gates, sinks=m_sinks, conv_w=m_conv_w, conv_b=m_conv_b, g_mlp=m_g_mlp,
                   g_final=m_g_final.reshape(1, D))
    small_v = dict(g_mix=v_g_mix, b_gates=v_b_gates, sinks=v_sinks, conv_w=v_conv_w, conv_b=v_conv_b, g_mlp=v_g_mlp,
                   g_final=v_g_final.reshape(1, D))
    loss, small = _small_step(parts, small_p, small_m, small_v)
    for n, vals in small.items():
        res[n] = tuple(v.reshape(D) for v in vals) if n == "g_final" else tuple(vals)

    weights = ["g_mix", "w_in", "b_gates", "sinks", "w_attn_out", "conv_w", "conv_b", "w_conv_out", "w_o", "g_mlp", "w_up",
               "w_down", "g_final"]
    out = [loss.reshape(()), dx.reshape(B, S, D)]
    for k in range(4):
        out += [res[n][k] for n in weights]
    return tuple(out)
```

```python
import functools

import numpy as np
import jax
import jax.numpy as jnp
from jax import lax
from jax.experimental import pallas as pl
from jax.experimental.pallas import tpu as pltpu

F32 = jnp.float32
BF16 = jnp.bfloat16

D_MODEL = 1024
HEAD_DIM = 64
N_Q_HEADS = 16
N_KV_HEADS = 4
GQA_GROUP = 4
WINDOW = 128
D_FF = 4096
DEPTH = 2
RMS_EPS = 1e-6
NEG_INF = -1e30
ATTN_SCALE = HEAD_DIM ** -0.5
QKV_W = 1536
CONV_W = 3072
GATE_W = 2048
IN_COLS = QKV_W + CONV_W + GATE_W
COL_TILE = 512
N_CHIP = 4
ADAM_LR = 0.001
ADAM_B1 = 0.9
ADAM_B2 = 0.999
ADAM_EPS = 1e-08
ADAM_WD = 0.01
ADAM_STEP = 10
V7X_VMEM_BYTES = 64 * 2 ** 20
VMEM_LIMIT = V7X_VMEM_BYTES - 8 * 2 ** 20
MESH = pl.DeviceIdType.MESH
ANY = pl.BlockSpec(memory_space=pl.ANY)
SMALL_ROWS = 24

_SLOPES = [float(v) for v in np.power(np.float32(2.0), -8.0 * np.arange(1, N_Q_HEADS + 1, dtype=np.float32) / N_Q_HEADS)]


def _params(*sem):
    return pltpu.CompilerParams(dimension_semantics=sem, vmem_limit_bytes=VMEM_LIMIT)


class _Hosted:
    def __init__(self, inputs, out_shape, aliases, scratch, start, finish):
        self.inputs, self.out_shape, self.aliases, self.scratch = list(inputs), list(out_shape), dict(aliases), list(scratch)
        self.start, self.finish = start, finish


def _hosted_call(comm, kern, *, out_shape, grid, in_specs, out_specs, args, name, sem, scratch_shapes=()):
    single = not isinstance(out_shape, (tuple, list))
    outs = [out_shape] if single else list(out_shape)
    ospecs = [out_specs] if single else list(out_specs)
    if comm is None:
        res = pl.pallas_call(kern, out_shape=outs, grid=grid, in_specs=list(in_specs), out_specs=ospecs,
                             scratch_shapes=list(scratch_shapes), name=name, compiler_params=_params(*sem))(*args)
        return res[0] if single else res
    n_in, n_out, n_scr = len(args), len(outs), len(scratch_shapes)
    ci, co, cs = len(comm.inputs), len(comm.out_shape), len(comm.scratch)

    def body(*refs):
        cuts = np.cumsum([0, n_in, ci, n_out, co, n_scr, cs])
        a, b, c, d, e, f = [refs[lo:hi] for lo, hi in zip(cuts[:-1], cuts[1:])]
        ids = [pl.program_id(k) for k in range(len(grid))]
        first = functools.reduce(jnp.logical_and, [i == 0 for i in ids])
        last = functools.reduce(jnp.logical_and, [i == n - 1 for i, n in zip(ids, grid)])
        pl.when(first)(lambda: comm.start(b, d, f))
        kern(*a, *c, *e)
        pl.when(last)(lambda: comm.finish(b, d, f))

    res = pl.pallas_call(
        body, out_shape=outs + comm.out_shape, grid=grid, in_specs=list(in_specs) + [ANY] * ci, out_specs=ospecs + [ANY] * co,
        scratch_shapes=list(scratch_shapes) + comm.scratch,
        input_output_aliases={n_in + i: n_out + o for i, o in comm.aliases.items()},
        name=name + "_carrier", compiler_params=_params(*(["arbitrary"] * len(grid))))(*args, *comm.inputs)
    main = res[:n_out]
    return (main[0] if single else main), res[n_out:]


def _nt(a, b):
    return lax.dot_general(a, b, (((1,), (1,)), ((), ())), preferred_element_type=F32)


def _tn(a, b):
    return lax.dot_general(a, b, (((0,), (0,)), ((), ())), preferred_element_type=F32)


def _nn(a, b):
    return jnp.dot(a, b, preferred_element_type=F32)


def _rms_stats(xf):
    r = lax.rsqrt(jnp.mean(xf * xf, axis=-1, keepdims=True) + RMS_EPS)
    return r, xf * r


def _rms_bwd(dh, xh, r, g):
    dxh = dh * g
    dx = r * (dxh - xh * jnp.mean(dxh * xh, axis=-1, keepdims=True))
    dg = jnp.sum(dh * xh, axis=0, keepdims=True)
    return dx, dg


def _dw_chunk(T):
    return min(2048, T)


def _row_halves(tm):
    return (slice(0, tm // 2), slice(tm // 2, tm))


def _resident(shape):
    return pl.BlockSpec(shape, lambda *_: (0,) * len(shape), pipeline_mode=pl.Buffered(1))


def _norm_proj(x, g, w, comm=None):
    T, D = x.shape
    tm = min(512, T)
    tk = _dw_chunk(T)
    per = tk // tm
    widths = (QKV_W, CONV_W, GATE_W)

    def kern(x_ref, g_ref, w_ref, ht_ref, *o_refs):
        _, xh = _rms_stats(x_ref[...])
        h = (xh * g_ref[...]).astype(BF16)
        ht_ref[...] = h.T
        off = 0
        for o_ref, wd in zip(o_refs, widths):
            o_ref[...] = _nn(h, w_ref[:, off:off + wd]).astype(BF16)
            off += wd

    row = lambda wd: pl.BlockSpec((tm, wd), lambda i: (i, 0))
    return _hosted_call(
        comm, kern,
        out_shape=[jax.ShapeDtypeStruct((T // tk, D, tk), BF16)] + [jax.ShapeDtypeStruct((T, wd), BF16) for wd in widths],
        grid=(T // tm,), in_specs=[row(D), pl.BlockSpec((1, D), lambda i: (0, 0)), _resident((D, IN_COLS))],
        out_specs=[pl.BlockSpec((None, D, tm), lambda i: (i // per, 0, i % per))] + [row(wd) for wd in widths],
        name="norm_proj", sem=("parallel",), args=(x, g, w))


GW = GQA_GROUP * WINDOW
BAND = 2 * WINDOW
KV_W = N_KV_HEADS * HEAD_DIM


def _attn_bias_table():
    jj = np.arange(BAND)[:, None]
    col = np.arange(GW)[None, :]
    dist = WINDOW + (col % WINDOW) - jj
    valid = (dist >= 0) & (dist < WINDOW)
    slopes = np.asarray(_SLOPES, np.float32).reshape(N_KV_HEADS, GQA_GROUP)
    tab = np.empty((2, N_KV_HEADS, BAND, GW), np.float32)
    for hk in range(N_KV_HEADS):
        bias = -slopes[hk][col // WINDOW] * dist.astype(np.float32)
        tab[0, hk] = np.where(valid, bias, np.float32(NEG_INF))
        tab[1, hk] = np.where(valid & (jj >= WINDOW), bias, np.float32(NEG_INF))
    return jnp.asarray(tab)


def _stack_heads(ref, hk):
    return jnp.concatenate(
        [ref[:, HEAD_DIM * (GQA_GROUP * hk + g): HEAD_DIM * (GQA_GROUP * hk + g + 1)] for g in range(GQA_GROUP)], axis=0)


def _kv_band(cur_ref, prev_ref, hk):
    k0 = N_Q_HEADS * HEAD_DIM
    sl = slice(HEAD_DIM * hk, HEAD_DIM * (hk + 1))
    ksl, vsl = slice(k0 + sl.start, k0 + sl.stop), slice(k0 + KV_W + sl.start, k0 + KV_W + sl.stop)
    k_band = jnp.concatenate([prev_ref[:, sl], cur_ref[:, ksl]], axis=0)
    v_band = jnp.concatenate([prev_ref[:, KV_W + sl.start:KV_W + sl.stop], cur_ref[:, vsl]], axis=0)
    return k_band, v_band


def _lane_row(vals):
    return jnp.concatenate([jnp.full((1, WINDOW), v, F32) for v in vals], axis=1)


def _attn_fwd(pqkv, sinks, bias, seq, comm=None):
    T = pqkv.shape[0]
    nblk = seq // WINDOW

    def kern(sink_ref, cur_ref, prev_ref, bias_ref, o_ref, ot_ref, lse_ref):
        i = pl.program_id(0)
        first_i = ((i % nblk) == 0).astype(jnp.int32)
        bands = [_kv_band(cur_ref, prev_ref, hk) for hk in range(N_KV_HEADS)]
        sts = [_nt(bands[hk][0], _stack_heads(cur_ref, hk) * ATTN_SCALE) + bias_ref[first_i, hk] for hk in range(N_KV_HEADS)]
        ps, scales = [], []
        for hk in range(N_KV_HEADS):
            heads = [GQA_GROUP * hk + g for g in range(GQA_GROUP)]
            sink = _lane_row([sink_ref[h] for h in heads])
            m = jnp.maximum(jnp.max(sts[hk], axis=0, keepdims=True), sink)
            p = jnp.exp(sts[hk] - m)
            den = jnp.sum(p, axis=0, keepdims=True) + jnp.exp(sink - m)
            lse = m + jnp.log(den)
            for g, h in enumerate(heads):
                lse_ref[h:h + 1, :] = lse[:, WINDOW * g:WINDOW * (g + 1)]
            ps.append(p.astype(BF16))
            scales.append(1.0 / den)
        for hk in range(N_KV_HEADS):
            ot = _tn(bands[hk][1], ps[hk]) * scales[hk]
            for g in range(GQA_GROUP):
                h = GQA_GROUP * hk + g
                ot_ref[HEAD_DIM * h:HEAD_DIM * (h + 1), :] = ot[:, WINDOW * g:WINDOW * (g + 1)].astype(BF16)
        o_ref[...] = ot_ref[...].T

    return _hosted_call(
        comm, kern,
        out_shape=(jax.ShapeDtypeStruct((T, D_MODEL), BF16), jax.ShapeDtypeStruct((D_MODEL, T), BF16),
                   jax.ShapeDtypeStruct((N_Q_HEADS, T), F32)),
        grid=(T // WINDOW,),
        in_specs=[pl.BlockSpec(memory_space=pltpu.SMEM),
                  pl.BlockSpec((WINDOW, QKV_W), lambda i: (i, 0)),
                  pl.BlockSpec((WINDOW, 2 * KV_W), lambda i: (jnp.maximum(i - 1, 0), 2)),
                  _resident(bias.shape)],
        out_specs=(pl.BlockSpec((WINDOW, D_MODEL), lambda i: (i, 0)), pl.BlockSpec((D_MODEL, WINDOW), lambda i: (0, i)),
                   pl.BlockSpec((N_Q_HEADS, WINDOW), lambda i: (0, i))),
        name="attn_fwd", sem=("parallel",), args=(sinks, pqkv, pqkv, bias))


def _pick_row(a, row):
    rid = lax.broadcasted_iota(jnp.int32, a.shape, 0)
    return jnp.sum(jnp.where(rid == row, a, 0.0), axis=0, keepdims=True)


def _conv_taps(yc, halo_yc, first_i):
    keep = (1 - first_i).astype(F32)
    p1 = _pick_row(halo_yc, 15) * keep
    p2 = _pick_row(halo_yc, 14) * keep
    rowid = lax.broadcasted_iota(jnp.int32, yc.shape, 0)
    s1 = jnp.where(rowid == 0, p1, pltpu.roll(yc, 1, 0))
    s2 = jnp.where(rowid == 0, p2, jnp.where(rowid == 1, p1, pltpu.roll(yc, 2, 0)))
    return s1, s2


def _conv_fwd(pconv, conv_w, conv_b, seq):
    T = pconv.shape[0]
    tm = min(512, seq)
    per_seq = seq // tm
    D = D_MODEL

    def kern(cur_ref, halo_ref, w_ref, b_ref, o_ref, ot_ref):
        i = pl.program_id(0)
        first_i = ((i % per_seq) == 0).astype(jnp.int32)
        cb = cur_ref[:, 0:D].astype(F32)
        yc = cur_ref[:, D:2 * D].astype(F32) * cur_ref[:, 2 * D:3 * D].astype(F32)
        halo_yc = halo_ref[:, D:2 * D].astype(F32) * halo_ref[:, 2 * D:3 * D].astype(F32)
        s1, s2 = _conv_taps(yc, halo_yc, first_i)
        z = w_ref[0:1, :] * s2 + w_ref[1:2, :] * s1 + w_ref[2:3, :] * yc
        cv = (cb * (z + b_ref[...])).astype(BF16)
        o_ref[...] = cv
        ot_ref[...] = cv.T

    return pl.pallas_call(
        kern, out_shape=(jax.ShapeDtypeStruct((T, D), BF16), jax.ShapeDtypeStruct((D, T), BF16)), grid=(T // tm,),
        in_specs=[pl.BlockSpec((tm, CONV_W), lambda i: (i, 0)),
                  pl.BlockSpec((16, CONV_W), lambda i: (jnp.maximum(i * (tm // 16) - 1, 0), 0)),
                  pl.BlockSpec((3, D), lambda i: (0, 0)), pl.BlockSpec((1, D), lambda i: (0, 0))],
        out_specs=(pl.BlockSpec((tm, D), lambda i: (i, 0)), pl.BlockSpec((D, tm), lambda i: (0, i))),
        name="conv_fwd", compiler_params=_params("parallel"))(pconv, pconv, conv_w, conv_b)


def _mix_fwd(x, att, cv, pgate, b_gates, wao, wco, wo):
    T, D = x.shape
    tm = min(512, T)

    halves = _row_halves(tm)

    def kern(x_ref, att_ref, cv_ref, pg_ref, bg_ref, wao_ref, wco_ref, wo_ref, x1_ref, ya_ref, yc_ref, mgt_ref):
        yas = [_nn(att_ref[rows, :], wao_ref[...]) for rows in halves]
        ycs = [_nn(cv_ref[rows, :], wco_ref[...]) for rows in halves]
        mgs = []
        for rows, ya, yc in zip(halves, yas, ycs):
            sa = jax.nn.sigmoid(pg_ref[rows, 0:D].astype(F32) + bg_ref[:, 0:D])
            sc = jax.nn.sigmoid(pg_ref[rows, D:2 * D].astype(F32) + bg_ref[:, D:2 * D])
            mg = (sa * ya + sc * yc).astype(BF16)
            ya_ref[rows, :] = ya.astype(BF16)
            yc_ref[rows, :] = yc.astype(BF16)
            mgt_ref[:, rows] = mg.T
            mgs.append(mg)
        for rows, mg in zip(halves, mgs):
            x1_ref[rows, :] = x_ref[rows, :] + _nn(mg, wo_ref[...])

    row = lambda w: pl.BlockSpec((tm, w), lambda i: (i, 0))
    full = lambda a, b: pl.BlockSpec((a, b), lambda i: (0, 0))
    bf = jax.ShapeDtypeStruct((T, D), BF16)
    return pl.pallas_call(
        kern, out_shape=(jax.ShapeDtypeStruct((T, D), F32), bf, bf, jax.ShapeDtypeStruct((D, T), BF16)), grid=(T // tm,),
        in_specs=[row(D), row(D), row(D), row(GATE_W), full(1, GATE_W)] + [_resident((D, D))] * 3,
        out_specs=(row(D), row(D), row(D), pl.BlockSpec((D, tm), lambda i: (0, i))),
        name="mix_fwd", compiler_params=_params("parallel"))(x, att, cv, pgate, b_gates, wao, wco, wo)


def _mlp_fwd(x1, g, wup, wdn, comm=None):
    T, D = x1.shape
    tm = min(1024, T)
    nj = D_FF // D

    def kern(x_ref, g_ref, wup_ref, wdn_ref, x2_ref, a_ref, h_scr, acc_scr):
        j = pl.program_id(1)

        @pl.when(j == 0)
        def _():
            xf = x_ref[...]
            _, xh = _rms_stats(xf)
            h_scr[...] = (xh * g_ref[...]).astype(BF16)
            acc_scr[...] = xf

        a = _nn(h_scr[...], wup_ref[j])
        a_ref[...] = a.astype(BF16)
        u = jnp.square(jnp.maximum(a, 0.0)).astype(BF16)
        acc_scr[...] += _nn(u, wdn_ref[j])

        @pl.when(j == nj - 1)
        def _():
            x2_ref[...] = acc_scr[...]

    return _hosted_call(
        comm, kern, out_shape=(jax.ShapeDtypeStruct((T, D), F32), jax.ShapeDtypeStruct((T, D_FF), BF16)), grid=(T // tm, nj),
        in_specs=[pl.BlockSpec((tm, D), lambda i, j: (i, 0)), pl.BlockSpec((1, D), lambda i, j: (0, 0)),
                  _resident((nj, D, D)), _resident((nj, D, D))],
        out_specs=(pl.BlockSpec((tm, D), lambda i, j: (i, 0)), pl.BlockSpec((tm, D), lambda i, j: (i, j))),
        scratch_shapes=[pltpu.VMEM((tm, D), BF16), pltpu.VMEM((tm, D), F32)],
        name="mlp_fwd", sem=("parallel", "arbitrary"), args=(x1, g, wup, wdn))


def _loss_bwd(x, g, tgt):
    T, D = x.shape
    tm = min(512, T)

    def kern(x_ref, g_ref, t_ref, st_ref, dx_ref):
        i = pl.program_id(0)

        @pl.when(i == 0)
        def _():
            st_ref[...] = jnp.zeros_like(st_ref)

        gg = g_ref[...]
        r, xh = _rms_stats(x_ref[...])
        e = xh * gg - t_ref[...]
        part = 0.5 * jnp.sum(jnp.mean(e * e, axis=-1, keepdims=True), axis=0, keepdims=True)
        dx, dg = _rms_bwd(e * (1.0 / D), xh, r, gg)
        dx_ref[...] = dx
        st_ref[0:1, :] += dg
        st_ref[1:2, 0:1] += part

    return pl.pallas_call(
        kern, out_shape=(jax.ShapeDtypeStruct((8, D), F32), jax.ShapeDtypeStruct((T, D), F32)), grid=(T // tm,),
        in_specs=[pl.BlockSpec((tm, D), lambda i: (i, 0)), pl.BlockSpec((1, D), lambda i: (0, 0)),
                  pl.BlockSpec((tm, D), lambda i: (i, 0))],
        out_specs=(pl.BlockSpec((8, D), lambda i: (0, 0)), pl.BlockSpec((tm, D), lambda i: (i, 0))),
        name="loss_bwd", compiler_params=_params("arbitrary"))(x, g, tgt)


def _mlp_bwd(dx2, x1, a, g, wup, wdn, comm=None):
    T, D = x1.shape
    tm = min(512, T)
    nj = D_FF // D
    tk = _dw_chunk(T)
    per = tk // tm

    def kern(dx2_ref, x1_ref, a_ref, g_ref, wup_ref, wdn_ref, dx1_ref, da_ref, ut_ref, h2t_ref, dyb_ref, dg_ref, acc_scr):
        i, j = pl.program_id(0), pl.program_id(1)

        @pl.when((i == 0) & (j == 0))
        def _():
            dg_ref[...] = jnp.zeros_like(dg_ref)

        @pl.when(j == 0)
        def _():
            dyb_ref[...] = dx2_ref[...].astype(BF16)
            acc_scr[...] = jnp.zeros_like(acc_scr)

        du = _nt(dyb_ref[...], wdn_ref[j])
        relu = jnp.maximum(a_ref[...].astype(F32), 0.0)
        da = (du * (2.0 * relu)).astype(BF16)
        da_ref[...] = da
        ut_ref[...] = jnp.square(relu).astype(BF16).T
        acc_scr[...] += _nt(da, wup_ref[j])

        @pl.when(j == nj - 1)
        def _():
            gg = g_ref[...]
            r, xh = _rms_stats(x1_ref[...])
            h2t_ref[...] = (xh * gg).astype(BF16).T
            dx, dg = _rms_bwd(acc_scr[...], xh, r, gg)
            dx1_ref[...] = dx2_ref[...] + dx
            dg_ref[...] += dg

    return _hosted_call(
        comm, kern,
        out_shape=(jax.ShapeDtypeStruct((T, D), F32), jax.ShapeDtypeStruct((T, D_FF), BF16),
                   jax.ShapeDtypeStruct((D_FF, T), BF16), jax.ShapeDtypeStruct((T // tk, D, tk), BF16),
                   jax.ShapeDtypeStruct((T, D), BF16), jax.ShapeDtypeStruct((1, D), F32)),
        grid=(T // tm, nj),
        in_specs=[pl.BlockSpec((tm, D), lambda i, j: (i, 0)), pl.BlockSpec((tm, D), lambda i, j: (i, 0)),
                  pl.BlockSpec((tm, D), lambda i, j: (i, j)), pl.BlockSpec((1, D), lambda i, j: (0, 0)),
                  _resident((nj, D, D)), _resident((nj, D, D))],
        out_specs=(pl.BlockSpec((tm, D), lambda i, j: (i, 0)), pl.BlockSpec((tm, D), lambda i, j: (i, j)),
                   pl.BlockSpec((D, tm), lambda i, j: (j, i)), pl.BlockSpec((None, D, tm), lambda i, j: (i // per, 0, i % per)),
                   pl.BlockSpec((tm, D), lambda i, j: (i, 0)), pl.BlockSpec((1, D), lambda i, j: (0, 0))),
        scratch_shapes=[pltpu.VMEM((tm, D), F32)],
        name="mlp_bwd", sem=("arbitrary", "arbitrary"), args=(dx2, x1, a, g, wup, wdn))


def _mix_bwd(dx1, ya, yc, pgate, b_gates, wao, wco, wo):
    T, D = dx1.shape
    tm = min(512, T)
    halves = _row_halves(tm)

    def kern(dx_ref, ya_ref, yc_ref, pg_ref, bg_ref, wao_ref, wco_ref, wo_ref,
             datt_ref, dcv_ref, dya_ref, dyc_ref, dgt_ref, dbg_ref):
        @pl.when(pl.program_id(0) == 0)
        def _():
            dbg_ref[...] = jnp.zeros_like(dbg_ref)

        dms = [_nt(dx_ref[rows, :].astype(BF16), wo_ref[...]) for rows in halves]
        dys = []
        for rows, dm in zip(halves, dms):
            sa = jax.nn.sigmoid(pg_ref[rows, 0:D].astype(F32) + bg_ref[:, 0:D])
            sc = jax.nn.sigmoid(pg_ref[rows, D:2 * D].astype(F32) + bg_ref[:, D:2 * D])
            dya = (dm * sa).astype(BF16)
            dyc = (dm * sc).astype(BF16)
            dga = dm * ya_ref[rows, :].astype(F32) * (sa * (1.0 - sa))
            dgc = dm * yc_ref[rows, :].astype(F32) * (sc * (1.0 - sc))
            dya_ref[rows, :] = dya
            dyc_ref[rows, :] = dyc
            dgt_ref[rows, 0:D] = dga.astype(BF16)
            dgt_ref[rows, D:2 * D] = dgc.astype(BF16)
            dbg_ref[:, 0:D] += jnp.sum(dga, axis=0, keepdims=True)
            dbg_ref[:, D:2 * D] += jnp.sum(dgc, axis=0, keepdims=True)
            dys.append((dya, dyc))
        for rows, (dya, dyc) in zip(halves, dys):
            datt_ref[rows, :] = _nt(dya, wao_ref[...]).astype(BF16)
            dcv_ref[rows, :] = _nt(dyc, wco_ref[...]).astype(BF16)

    row = lambda w: pl.BlockSpec((tm, w), lambda i: (i, 0))
    full = lambda a, b: pl.BlockSpec((a, b), lambda i: (0, 0))
    bf = jax.ShapeDtypeStruct((T, D), BF16)
    return pl.pallas_call(
        kern,
        out_shape=(bf, bf, bf, bf, jax.ShapeDtypeStruct((T, GATE_W), BF16), jax.ShapeDtypeStruct((1, GATE_W), F32)),
        grid=(T // tm,),
        in_specs=[row(D), row(D), row(D), row(GATE_W), full(1, GATE_W)] + [_resident((D, D))] * 3,
        out_specs=(row(D), row(D), row(D), row(D), row(GATE_W), full(1, GATE_W)),
        name="mix_bwd", compiler_params=_params("arbitrary"))(dx1, ya, yc, pgate, b_gates, wao, wco, wo)


def _conv_bwd(dcv, pconv, conv_w, conv_b, seq, comm=None):
    T = pconv.shape[0]
    tm = min(512, seq)
    per_seq = seq // tm
    D = D_MODEL
    nb16 = T // 16

    def kern(dcv_ref, dcvn_ref, cur_ref, prev_ref, next_ref, w_ref, b_ref, o_ref, dwb_ref):
        i = pl.program_id(0)

        @pl.when(i == 0)
        def _():
            dwb_ref[...] = jnp.zeros_like(dwb_ref)

        first_i = ((i % per_seq) == 0).astype(jnp.int32)
        keep_next = 1.0 - (((i + 1) % per_seq) == 0).astype(F32)
        cb = cur_ref[:, 0:D].astype(F32)
        cc = cur_ref[:, D:2 * D].astype(F32)
        cu = cur_ref[:, 2 * D:3 * D].astype(F32)
        yc = cc * cu
        halo_yc = prev_ref[:, D:2 * D].astype(F32) * prev_ref[:, 2 * D:3 * D].astype(F32)
        s1, s2 = _conv_taps(yc, halo_yc, first_i)
        w0, w1, w2 = w_ref[0:1, :], w_ref[1:2, :], w_ref[2:3, :]
        z = w0 * s2 + w1 * s1 + w2 * yc
        dcv = dcv_ref[...].astype(F32)
        dz = dcv * cb
        dzn = dcvn_ref[...].astype(F32) * next_ref[:, 0:D].astype(F32) * keep_next
        n1, n2 = _pick_row(dzn, 0), _pick_row(dzn, 1)
        rowid = lax.broadcasted_iota(jnp.int32, dz.shape, 0)
        u1 = jnp.where(rowid == tm - 1, n1, pltpu.roll(dz, tm - 1, 0))
        u2 = jnp.where(rowid == tm - 1, n2, jnp.where(rowid == tm - 2, n1, pltpu.roll(dz, tm - 2, 0)))
        dyc = w2 * dz + w1 * u1 + w0 * u2
        o_ref[:, 0:D] = (dcv * (z + b_ref[...])).astype(BF16)
        o_ref[:, D:2 * D] = (dyc * cu).astype(BF16)
        o_ref[:, 2 * D:3 * D] = (dyc * cc).astype(BF16)
        dwb_ref[0:1, :] += jnp.sum(dz * s2, axis=0, keepdims=True)
        dwb_ref[1:2, :] += jnp.sum(dz * s1, axis=0, keepdims=True)
        dwb_ref[2:3, :] += jnp.sum(dz * yc, axis=0, keepdims=True)
        dwb_ref[3:4, :] += jnp.sum(dz, axis=0, keepdims=True)

    prev_map = lambda i: (jnp.maximum(i * (tm // 16) - 1, 0), 0)
    next_map = lambda i: (jnp.minimum((i + 1) * (tm // 16), nb16 - 1), 0)
    return _hosted_call(
        comm, kern, out_shape=(jax.ShapeDtypeStruct((T, CONV_W), BF16), jax.ShapeDtypeStruct((8, D), F32)), grid=(T // tm,),
        in_specs=[pl.BlockSpec((tm, D), lambda i: (i, 0)), pl.BlockSpec((16, D), next_map),
                  pl.BlockSpec((tm, CONV_W), lambda i: (i, 0)), pl.BlockSpec((16, CONV_W), prev_map),
                  pl.BlockSpec((16, CONV_W), next_map),
                  pl.BlockSpec((3, D), lambda i: (0, 0)), pl.BlockSpec((1, D), lambda i: (0, 0))],
        out_specs=(pl.BlockSpec((tm, CONV_W), lambda i: (i, 0)), pl.BlockSpec((8, D), lambda i: (0, 0))),
        name="conv_bwd", sem=("arbitrary",), args=(dcv, dcv, pconv, pconv, pconv, conv_w, conv_b))


def _attn_bwd(pqkv, datt, lse, sinks, bias, seq, comm=None):
    T = pqkv.shape[0]
    nblk = seq // WINDOW
    nseq = T // seq
    KVW = KV_W

    def kern(sink_ref, cur_ref, prev_ref, do_ref, lse_ref, bias_ref, dq_ref, dkv_ref, ds_ref, kc_scr, vc_scr, dqt_scr):
        b, st = pl.program_id(0), pl.program_id(1)

        @pl.when((b == 0) & (st == 0))
        def _():
            ds_ref[...] = jnp.zeros_like(ds_ref)

        @pl.when(st == 0)
        def _():
            kc_scr[...] = jnp.zeros_like(kc_scr)
            vc_scr[...] = jnp.zeros_like(vc_scr)

        @pl.when(st < nblk)
        def _():
            first_i = (st == 0).astype(jnp.int32)
            groups = range(N_KV_HEADS)
            bands = [_kv_band(cur_ref, prev_ref, hk) for hk in groups]
            qs = [_stack_heads(cur_ref, hk) for hk in groups]
            dos = [_stack_heads(do_ref, hk) for hk in groups]
            sts = [_nt(bands[hk][0], qs[hk] * ATTN_SCALE) + bias_ref[first_i, hk] for hk in groups]
            dps = [_nt(bands[hk][1], dos[hk]) for hk in groups]
            pbs, dsss = [], []
            for hk in groups:
                heads = [GQA_GROUP * hk + g for g in range(GQA_GROUP)]
                sink = _lane_row([sink_ref[h] for h in heads])
                lse_g = jnp.concatenate([lse_ref[h:h + 1, :] for h in heads], axis=1)
                p = jnp.exp(sts[hk] - lse_g)
                d_row = jnp.sum(p * dps[hk], axis=0, keepdims=True)
                dsss.append((p * (dps[hk] - d_row) * ATTN_SCALE).astype(BF16))
                pbs.append(p.astype(BF16))
                psd = jnp.exp(sink - lse_g) * d_row
                for g, h in enumerate(heads):
                    ds_ref[0:1, h:h + 1] -= jnp.sum(psd[:, WINDOW * g:WINDOW * (g + 1)], axis=1, keepdims=True)
            for hk in groups:
                dqt = _tn(bands[hk][0], dsss[hk])
                dk_b = _nn(dsss[hk], qs[hk])
                dv_b = _nn(pbs[hk], dos[hk])
                for g in range(GQA_GROUP):
                    h = GQA_GROUP * hk + g
                    dqt_scr[HEAD_DIM * h:HEAD_DIM * (h + 1), :] = dqt[:, WINDOW * g:WINDOW * (g + 1)].astype(BF16)
                ksl = slice(HEAD_DIM * hk, HEAD_DIM * (hk + 1))
                vsl = slice(KVW + HEAD_DIM * hk, KVW + HEAD_DIM * (hk + 1))
                dkv_ref[:, ksl] = (kc_scr[:, ksl] + dk_b[0:WINDOW]).astype(BF16)
                dkv_ref[:, vsl] = (vc_scr[:, ksl] + dv_b[0:WINDOW]).astype(BF16)
                kc_scr[:, ksl] = dk_b[WINDOW:2 * WINDOW]
                vc_scr[:, ksl] = dv_b[WINDOW:2 * WINDOW]
            dq_ref[...] = dqt_scr[...].T

        @pl.when(st == nblk)
        def _():
            dkv_ref[:, 0:KVW] = kc_scr[...].astype(BF16)
            dkv_ref[:, KVW:2 * KVW] = vc_scr[...].astype(BF16)

    cur_map = lambda b, s: (b * nblk + jnp.minimum(s, nblk - 1), 0)
    prev_row = lambda b, s: b * nblk + jnp.clip(s - 1, 0, nblk - 1)
    return _hosted_call(
        comm, kern,
        out_shape=(jax.ShapeDtypeStruct((T, D_MODEL), BF16), jax.ShapeDtypeStruct((T, 2 * KVW), BF16),
                   jax.ShapeDtypeStruct((8, 128), F32)),
        grid=(nseq, nblk + 1),
        in_specs=[pl.BlockSpec(memory_space=pltpu.SMEM),
                  pl.BlockSpec((WINDOW, QKV_W), cur_map),
                  pl.BlockSpec((WINDOW, 2 * KVW), lambda b, s: (prev_row(b, s), 2)),
                  pl.BlockSpec((WINDOW, D_MODEL), cur_map),
                  pl.BlockSpec((N_Q_HEADS, WINDOW), lambda b, s: (0, b * nblk + jnp.minimum(s, nblk - 1))),
                  _resident(bias.shape)],
        out_specs=(pl.BlockSpec((WINDOW, D_MODEL), cur_map),
                   pl.BlockSpec((WINDOW, 2 * KVW), lambda b, s: (prev_row(b, s), 0)),
                   pl.BlockSpec((8, 128), lambda b, s: (0, 0))),
        scratch_shapes=[pltpu.VMEM((WINDOW, KVW), F32), pltpu.VMEM((WINDOW, KVW), F32), pltpu.VMEM((D_MODEL, WINDOW), BF16)],
        name="attn_bwd", sem=("arbitrary", "arbitrary"), args=(sinks, pqkv, pqkv, datt, lse, bias))


def _piece_tiles(pieces):
    out, start = [], 0
    for arr, width in pieces:
        out.append((arr, start, width // COL_TILE))
        start += width // COL_TILE
    return out, start


def _inproj_bwd(pieces, w_in, x, dx_in, g, comm=None):
    T, D = x.shape
    tm = min(512, T)

    def kern(*refs):
        p_refs = refs[:len(pieces)]
        w_ref, x_ref, dxin_ref, g_ref, dx_ref, dg_ref = refs[len(pieces):]

        @pl.when(pl.program_id(0) == 0)
        def _():
            dg_ref[...] = jnp.zeros_like(dg_ref)

        dh, off = None, 0
        for p_ref, (_, width) in zip(p_refs, pieces):
            part = _nt(p_ref[...], w_ref[:, off:off + width])
            dh = part if dh is None else dh + part
            off += width
        gg = g_ref[...]
        r, xh = _rms_stats(x_ref[...])
        dx, dg = _rms_bwd(dh, xh, r, gg)
        dx_ref[...] = dxin_ref[...] + dx
        dg_ref[...] += dg

    row = lambda wd: pl.BlockSpec((tm, wd), lambda i: (i, 0))
    return _hosted_call(
        comm, kern, out_shape=(jax.ShapeDtypeStruct((T, D), F32), jax.ShapeDtypeStruct((1, D), F32)), grid=(T // tm,),
        in_specs=[row(wd) for _, wd in pieces] + [_resident((D, IN_COLS)), row(D), row(D), pl.BlockSpec((1, D), lambda i: (0, 0))],
        out_specs=(row(D), pl.BlockSpec((1, D), lambda i: (0, 0))),
        name="inproj_bwd", sem=("arbitrary",), args=(*[a for a, _ in pieces], w_in, x, dx_in, g))


def _dw_pieces(lhs_t, pieces):
    nt, K, tk = lhs_t.shape
    tiles, nj = _piece_tiles(pieces)

    def kern(*refs):
        lhs_ref = refs[0]
        p_refs = refs[1:1 + len(tiles)]
        o_ref, ob_ref = refs[1 + len(tiles):]
        j, t = pl.program_id(0), pl.program_id(1)

        @pl.when(t == 0)
        def _():
            o_ref[...] = jnp.zeros_like(o_ref)

        for p_ref, (_, start, n) in zip(p_refs, tiles):
            @pl.when((j >= start) & (j < start + n))
            def _(p_ref=p_ref):
                o_ref[...] += _nn(lhs_ref[t], p_ref[...])

        @pl.when(t == nt - 1)
        def _():
            ob_ref[...] = o_ref[...].astype(BF16)

    def p_map(start, n):
        return lambda j, t: (jnp.where((j >= start) & (j < start + n), t, 0), jnp.clip(j - start, 0, n - 1))

    N = nj * COL_TILE
    return pl.pallas_call(
        kern, out_shape=(jax.ShapeDtypeStruct((K, N), F32), jax.ShapeDtypeStruct((K, N), BF16)), grid=(nj, nt),
        in_specs=[_resident((nt, K, tk))] + [pl.BlockSpec((tk, COL_TILE), p_map(s, n)) for _, s, n in tiles],
        out_specs=(pl.BlockSpec((K, COL_TILE), lambda j, t: (0, j)), pl.BlockSpec((K, COL_TILE), lambda j, t: (0, j))),
        name="dw_pieces", compiler_params=_params("arbitrary", "arbitrary"))(lhs_t, *[a for a, _, _ in tiles])


def _dw(lhs_t, rhs):
    lhs_res, rhs_res = lhs_t.ndim == 3, rhs.ndim == 3
    W = D_MODEL
    if lhs_res:
        nt, K, tk = lhs_t.shape
    else:
        K, tk = lhs_t.shape[0], _dw_chunk(lhs_t.shape[1])
        nt = lhs_t.shape[1] // tk
    N = rhs.shape[-1]

    def kern(lhs_ref, rhs_ref, o_ref, ob_ref):
        t = pl.program_id(2)

        @pl.when(t == 0)
        def _():
            o_ref[...] = jnp.zeros_like(o_ref)

        a = lhs_ref[t] if lhs_res else lhs_ref[...]
        b = rhs_ref[t] if rhs_res else rhs_ref[...]
        o_ref[...] += _nn(a, b.astype(BF16))

        @pl.when(t == nt - 1)
        def _():
            ob_ref[...] = o_ref[...].astype(BF16)

    omap = lambda i, j, t: (i, j)
    lspec = _resident((nt, W, tk)) if lhs_res else pl.BlockSpec((W, tk), lambda i, j, t: (i, t))
    rspec = _resident((nt, tk, W)) if rhs_res else pl.BlockSpec((tk, W), lambda i, j, t: (t, j))
    return pl.pallas_call(
        kern, out_shape=(jax.ShapeDtypeStruct((K, N), F32), jax.ShapeDtypeStruct((K, N), BF16)), grid=(K // W, N // W, nt),
        in_specs=[lspec, rspec], out_specs=(pl.BlockSpec((W, W), omap), pl.BlockSpec((W, W), omap)),
        name="dw", compiler_params=_params("arbitrary", "arbitrary", "arbitrary"))(lhs_t, rhs)


BIG = (("w_in", D_MODEL, IN_COLS, "col"), ("w_attn_out", D_MODEL, D_MODEL, "row"), ("w_conv_out", D_MODEL, D_MODEL, "row"),
       ("w_o", D_MODEL, D_MODEL, "row"), ("w_up", D_MODEL, D_FF, "col"), ("w_down", D_FF, D_MODEL, "row"))


def _shard_dims(rows, cols, kind):
    return (rows, cols // N_CHIP) if kind in ("col", "chip") else (rows // N_CHIP, cols)


def _window(ref, rows, cols, kind, chip, half):
    sr, sc = _shard_dims(rows, cols, kind)
    hr = sr // 2
    if kind == "col":
        return ref.at[pl.ds(half * hr, hr), pl.ds(chip * sc, sc)]
    if kind == "chip":
        return ref.at[chip, pl.ds(half * hr, hr), :]
    return ref.at[pl.ds(chip * sr + half * hr, hr), :]


def _mesh_pos():
    x, y, c = lax.axis_index("x"), lax.axis_index("y"), lax.axis_index("c")
    return x, y, c, 2 * x + y


_REL_BITS = (2, 1, 3)


def _rel_dev(x, y, c, r):
    return ((1 - x, y, c), (x, 1 - y, c), (1 - x, 1 - y, c))[r]


def _for_my_chip(j, fn):
    for js in range(N_CHIP):
        pl.when(j == js)(functools.partial(fn, js))


def _cast_into_full(j_arr, shard, l, rows, cols, kind):
    sr, sc = _shard_dims(rows, cols, kind)
    tr = min(256, sr)

    def kern(j_ref, s_ref, o_ref):
        o_ref[...] = s_ref[...].astype(BF16)

    shape, block = (rows, cols), (tr, sc)
    if kind == "col":
        omap = lambda i, j_ref: (i, j_ref[0])
    elif kind == "chip":
        shape, block = (N_CHIP, rows, sc), (None, tr, sc)
        omap = lambda i, j_ref: (j_ref[0], i, 0)
    else:
        omap = lambda i, j_ref: (j_ref[0] * (sr // tr) + i, 0)
    gs = pltpu.PrefetchScalarGridSpec(
        num_scalar_prefetch=1, grid=(sr // tr,),
        in_specs=[pl.BlockSpec((None, tr, sc), lambda i, j_ref: (l, i, 0))], out_specs=pl.BlockSpec(block, omap))
    return pl.pallas_call(kern, out_shape=jax.ShapeDtypeStruct(shape, BF16), grid_spec=gs, name="cast_into_full",
                          compiler_params=_params("arbitrary"))(j_arr, shard)


def _gather_comm(fulls, cw=None):
    n_big = len(fulls)
    n_piece = n_big + (0 if cw is None else 1)

    def pieces(in_refs, o_refs, js, c):
        def piece(p, chip, half):
            if p == n_big:
                return o_refs[p].at[half, chip]
            _, rows, cols, kind = fulls[p]
            return _window(o_refs[p], rows, cols, kind, chip, half)

        def mine(p):
            return in_refs[p].at[c] if p == n_big else piece(p, js, c)

        return piece, mine

    def local_copies(in_refs, piece, js, loc_sem):
        if cw is None:
            return []
        return [pltpu.make_async_copy(in_refs[n_big].at[half], piece(n_big, js, half), loc_sem.at[half]) for half in range(2)]

    def ici_copy(piece, mine, js, x, y, c, r, p, send_sem, recv_sem):
        return pltpu.make_async_remote_copy(mine(p), piece(p, js, c), send_sem.at[r * n_piece + p], recv_sem.at[r * n_piece + p],
                                            _rel_dev(x, y, c, r), MESH)

    def start(in_refs, o_refs, sems):
        send_sem, recv_sem, _, _, loc_sem = sems
        x, y, c, j = _mesh_pos()

        def run(js):
            piece, mine = pieces(in_refs, o_refs, js, c)
            for cp in local_copies(in_refs, piece, js, loc_sem):
                cp.start()
            for r in range(3):
                for p in range(n_piece):
                    ici_copy(piece, mine, js, x, y, c, r, p, send_sem, recv_sem).start()

        _for_my_chip(j, run)

    def finish(in_refs, o_refs, sems):
        send_sem, recv_sem, fsend_sem, frecv_sem, loc_sem = sems
        x, y, c, j = _mesh_pos()

        def run(js):
            piece, mine = pieces(in_refs, o_refs, js, c)
            fwds = []
            for r in range(3):
                ks = js ^ _REL_BITS[r]
                for p in range(n_piece):
                    got = piece(p, ks, c)
                    pltpu.make_async_remote_copy(got, got, send_sem.at[r * n_piece + p], recv_sem.at[r * n_piece + p],
                                                 _rel_dev(x, y, c, r), MESH).wait_recv()
                    cp = pltpu.make_async_remote_copy(got, got, fsend_sem.at[r * n_piece + p], frecv_sem.at[r * n_piece + p],
                                                      (x, y, 1 - c), MESH)
                    cp.start()
                    fwds.append(cp)
            for r in range(3):
                ks = js ^ _REL_BITS[r]
                for p in range(n_piece):
                    got = piece(p, ks, 1 - c)
                    pltpu.make_async_remote_copy(got, got, fsend_sem.at[r * n_piece + p], frecv_sem.at[r * n_piece + p],
                                                 (x, y, 1 - c), MESH).wait_recv()
            for r in range(3):
                for p in range(n_piece):
                    ici_copy(piece, mine, js, x, y, c, r, p, send_sem, recv_sem).wait_send()
            for cp in fwds:
                cp.wait_send()
            for cp in local_copies(in_refs, piece, js, loc_sem):
                cp.wait()

        _for_my_chip(j, run)

    out_shape = [jax.ShapeDtypeStruct(a.shape, BF16) for a, _, _, _ in fulls]
    ins = [a for a, _, _, _ in fulls]
    if cw is not None:
        out_shape.append(jax.ShapeDtypeStruct((DEPTH, N_CHIP, 3, D_MODEL // N_CHIP), F32))
        ins.append(cw)
    scratch = [pltpu.SemaphoreType.DMA((3 * n_piece,))] * 4 + [pltpu.SemaphoreType.DMA((2,))]
    return _Hosted(ins, out_shape, {p: p for p in range(n_big)}, scratch, start, finish)


def _compose(*comms):
    comms = [cm for cm in comms if cm is not None]
    if len(comms) <= 1:
        return comms[0] if comms else None
    ins, outs, aliases, scratch, cuts = [], [], {}, [], []
    for cm in comms:
        cuts.append((len(ins), len(outs), len(scratch)))
        aliases.update({len(ins) + i: len(outs) + o for i, o in cm.aliases.items()})
        ins, outs, scratch = ins + cm.inputs, outs + cm.out_shape, scratch + cm.scratch

    def parts(a, b, s):
        for cm, (i0, o0, s0) in zip(comms, cuts):
            yield cm, a[i0:i0 + len(cm.inputs)], b[o0:o0 + len(cm.out_shape)], s[s0:s0 + len(cm.scratch)]

    def start(a, b, s):
        for cm, pa, pb, ps in parts(a, b, s):
            cm.start(pa, pb, ps)

    def finish(a, b, s):
        for cm, pa, pb, ps in parts(a, b, s):
            cm.finish(pa, pb, ps)

    return _Hosted(ins, outs, aliases, scratch, start, finish)


def _run_comm(comm, name):
    n_in, n_out = len(comm.inputs), len(comm.out_shape)

    def body(*refs):
        comm.start(refs[:n_in], refs[n_in:n_in + n_out], refs[n_in + n_out:])
        comm.finish(refs[:n_in], refs[n_in:n_in + n_out], refs[n_in + n_out:])

    return pl.pallas_call(body, out_shape=comm.out_shape, in_specs=[ANY] * n_in, out_specs=[ANY] * n_out,
                          input_output_aliases=comm.aliases, scratch_shapes=comm.scratch, name=name)(*comm.inputs)


def _sibling_exchange_comm(gb):
    n = len(gb)

    def copies(g_refs, o_refs, sems, c):
        send_sem, recv_sem = sems
        x, y, _, _ = _mesh_pos()
        return [pltpu.make_async_remote_copy(_window(g_refs[t], rows, cols, kind, chip, 1 - c),
                                             _window(o_refs[t], rows, cols, kind, chip, 1 - c),
                                             send_sem.at[N_CHIP * t + chip], recv_sem.at[N_CHIP * t + chip], (x, y, 1 - c), MESH)
                for t, (_, rows, cols, kind) in enumerate(gb) for chip in range(N_CHIP)]

    def start(g_refs, o_refs, sems):
        for cp in copies(g_refs, o_refs, sems, lax.axis_index("c")):
            cp.start()

    def finish(g_refs, o_refs, sems):
        c = lax.axis_index("c")
        for cp in copies(g_refs, o_refs, sems, 1 - c):
            cp.wait_recv()
        for cp in copies(g_refs, o_refs, sems, c):
            cp.wait_send()

    return _Hosted([a for a, _, _, _ in gb], [jax.ShapeDtypeStruct(a.shape, BF16) for a, _, _, _ in gb], {},
                   [pltpu.SemaphoreType.DMA((N_CHIP * n,))] * 2, start, finish)


def _half_add(jc_arr, g, sib, rows, cols, kind):
    sr, sc = _shard_dims(rows, cols, kind)
    hr = sr // 2

    def kern(jc_ref, g_ref, s_ref, ob_ref, of_ref):
        v = g_ref[...] + s_ref[...].astype(F32)
        ob_ref[...] = v.astype(BF16)

        @pl.when(pl.program_id(0) == jc_ref[0])
        def _():
            of_ref[...] = v

    if kind == "col":
        imap = lambda j, jc_ref: (jc_ref[1], j)
    else:
        imap = lambda j, jc_ref: (2 * j + jc_ref[1], 0)
    gs = pltpu.PrefetchScalarGridSpec(
        num_scalar_prefetch=1, grid=(N_CHIP,),
        in_specs=[pl.BlockSpec((hr, sc), imap), pl.BlockSpec((hr, sc), imap)],
        out_specs=[pl.BlockSpec((None, hr, sc), lambda j, jc_ref: (j, 0, 0)), pl.BlockSpec((hr, sc), lambda j, jc_ref: (0, 0))])
    return pl.pallas_call(
        kern, out_shape=(jax.ShapeDtypeStruct((N_CHIP, hr, sc), BF16), jax.ShapeDtypeStruct((hr, sc), F32)),
        grid_spec=gs, name="grad_half_add", compiler_params=_params("arbitrary"))(jc_arr, g, sib)


def _chip_exchange_comm(sbs):
    n = len(sbs)

    def copies(s_refs, o_refs, sems):
        send_sem, recv_sem = sems
        x, y, c, j = _mesh_pos()
        return [pltpu.make_async_remote_copy(s_refs[t].at[j ^ _REL_BITS[r]], o_refs[t].at[r], send_sem.at[n * r + t],
                                             recv_sem.at[n * r + t], _rel_dev(x, y, c, r), MESH)
                for r in range(3) for t in range(n)]

    def start(s_refs, o_refs, sems):
        for cp in copies(s_refs, o_refs, sems):
            cp.start()

    def finish(s_refs, o_refs, sems):
        for cp in copies(s_refs, o_refs, sems):
            cp.wait()

    return _Hosted(sbs, [jax.ShapeDtypeStruct((3,) + a.shape[1:], BF16) for a in sbs], {},
                   [pltpu.SemaphoreType.DMA((3 * n,))] * 2, start, finish)


def _owner_sum(jc_arr, sf, rb, l, into=None):
    hr, sc = sf.shape

    def kern(jc_ref, s_ref, r0_ref, r1_ref, r2_ref, *rest):
        o_ref = rest[-1]
        o_ref[...] = ((s_ref[...] + r0_ref[...].astype(F32)) + r1_ref[...].astype(F32)) + r2_ref[...].astype(F32)

    in_specs = [pl.BlockSpec((hr, sc), lambda i, jc_ref: (0, 0))]
    in_specs += [pl.BlockSpec((None, hr, sc), lambda i, jc_ref, r=r: (r, 0, 0)) for r in range(3)]
    args = [jc_arr, sf, rb, rb, rb]
    aliases = {}
    if into is not None:
        in_specs.append(ANY)
        args.append(into)
        aliases = {len(args) - 1: 0}
    gs = pltpu.PrefetchScalarGridSpec(
        num_scalar_prefetch=1, grid=(1,), in_specs=in_specs,
        out_specs=pl.BlockSpec((None, hr, sc), lambda i, jc_ref: (l, jc_ref[1], 0)))
    return pl.pallas_call(kern, out_shape=jax.ShapeDtypeStruct((DEPTH, 2 * hr, sc), F32), grid_spec=gs,
                          input_output_aliases=aliases, name="grad_owner_sum", compiler_params=_params("arbitrary"))(*args)


def _sibling_assemble_comm(grads, layers):
    n = len(grads)
    todo = [(q, l) for q in range(n) for l in layers[q]]

    def copies(o_refs, sems, half):
        send_sem, recv_sem = sems
        x, y, c, _ = _mesh_pos()
        out = []
        for k, (q, l) in enumerate(todo):
            hr = grads[q].shape[1] // 2
            w = o_refs[q].at[l, pl.ds(half * hr, hr), :]
            out.append(pltpu.make_async_remote_copy(w, w, send_sem.at[k], recv_sem.at[k], (x, y, 1 - c), MESH))
        return out

    def start(_, o_refs, sems):
        for cp in copies(o_refs, sems, lax.axis_index("c")):
            cp.start()

    def finish(_, o_refs, sems):
        c = lax.axis_index("c")
        for cp in copies(o_refs, sems, 1 - c):
            cp.wait_recv()
        for cp in copies(o_refs, sems, c):
            cp.wait_send()

    return _Hosted(grads, [jax.ShapeDtypeStruct(g.shape, F32) for g in grads], {q: q for q in range(n)},
                   [pltpu.SemaphoreType.DMA((len(todo),))] * 2, start, finish)


def _adamw_math(w, g, m, v):
    m = ADAM_B1 * m + (1.0 - ADAM_B1) * g
    v = ADAM_B2 * v + (1.0 - ADAM_B2) * jnp.square(g)
    m_hat = m / (1.0 - ADAM_B1 ** ADAM_STEP)
    v_hat = v / (1.0 - ADAM_B2 ** ADAM_STEP)
    delta = -ADAM_LR * (m_hat / (jnp.sqrt(v_hat) + ADAM_EPS) + ADAM_WD * w)
    return delta, m, v


def _adamw(w, g, m, v):
    shape = w.shape
    C = shape[-1]
    R = int(np.prod(shape[:-1]))
    tr = min(256, R)
    args = [a.reshape(R, C) for a in (w, g, m, v)]

    def kern(w_ref, g_ref, m_ref, v_ref, go_ref, d_ref, nm_ref, nv_ref):
        g_val = g_ref[...]
        d, nm, nv = _adamw_math(w_ref[...], g_val, m_ref[...], v_ref[...])
        go_ref[...] = g_val
        d_ref[...] = d
        nm_ref[...] = nm
        nv_ref[...] = nv

    spec = pl.BlockSpec((tr, C), lambda i: (i, 0))
    outs = pl.pallas_call(
        kern, out_shape=[jax.ShapeDtypeStruct((R, C), F32)] * 4, grid=(R // tr,), in_specs=[spec] * 4, out_specs=[spec] * 4,
        name="adamw", compiler_params=_params("parallel"))(*args)
    return [o.reshape(shape) for o in outs]


_ROW_G_MIX, _ROW_B_GATES, _ROW_SINKS, _ROW_CONV, _ROW_G_MLP, _ROW_G_FINAL, _ROW_LOSS = 0, 2, 6, 8, 16, 18, 19


def _small_step(parts, params, moms, vels):
    names = ["g_mix", "b_gates", "sinks", "conv_w", "conv_b", "g_mlp", "g_final"]
    D = D_MODEL
    QW = D // N_CHIP
    n_dev = 8

    def body(*refs):
        it = iter(refs)
        dgmix = [next(it) for _ in range(DEPTH)]
        dbg = [next(it) for _ in range(DEPTH)]
        dsk = [next(it) for _ in range(DEPTH)]
        dwb = [next(it) for _ in range(DEPTH)]
        dgmlp = [next(it) for _ in range(DEPTH)]
        lst = next(it)
        p_refs = {n: next(it) for n in names}
        m_refs = {n: next(it) for n in names}
        v_refs = {n: next(it) for n in names}
        loss_ref = next(it)
        outs = {n: [next(it) for _ in range(4)] for n in names}
        pack_ref, all_ref, send_sem, recv_sem = next(it), next(it), next(it), next(it)

        x, y, c, j = _mesh_pos()
        me = 4 * x + 2 * y + c
        pack_ref[...] = jnp.zeros_like(pack_ref)
        for l in range(DEPTH):
            pack_ref[_ROW_G_MIX + l:_ROW_G_MIX + l + 1, :] = dgmix[l][...]
            pack_ref[_ROW_B_GATES + 2 * l:_ROW_B_GATES + 2 * l + 1, :] = dbg[l][:, 0:D]
            pack_ref[_ROW_B_GATES + 2 * l + 1:_ROW_B_GATES + 2 * l + 2, :] = dbg[l][:, D:2 * D]
            pack_ref[_ROW_SINKS + l:_ROW_SINKS + l + 1, 0:128] = dsk[l][0:1, :]
            pack_ref[_ROW_CONV + 4 * l:_ROW_CONV + 4 * l + 4, :] = dwb[l][0:4, :]
            pack_ref[_ROW_G_MLP + l:_ROW_G_MLP + l + 1, :] = dgmlp[l][...]
        pack_ref[_ROW_G_FINAL:_ROW_G_FINAL + 1, :] = lst[0:1, :]
        pack_ref[_ROW_LOSS:_ROW_LOSS + 1, :] = lst[1:2, :]

        all_ref[me] = pack_ref[...]
        cps = []
        for k in range(1, n_dev):
            dx_, dy_, dc_ = (k >> 2) & 1, (k >> 1) & 1, k & 1
            peer = (x ^ dx_, y ^ dy_, c ^ dc_)
            cp = pltpu.make_async_remote_copy(pack_ref, all_ref.at[me], send_sem.at[k - 1], recv_sem.at[k - 1], peer, MESH)
            cp.start()
            cps.append(cp)
        for cp in cps:
            cp.wait()

        tot = all_ref[0]
        for d in range(1, n_dev):
            tot = tot + all_ref[d]
        pack_ref[...] = tot

        loss_ref[...] = pack_ref[_ROW_LOSS:_ROW_LOSS + 1, 0:1]

        def finish(name, idx, g):
            w, m, v = p_refs[name][idx], m_refs[name][idx], v_refs[name][idx]
            d, nm, nv = _adamw_math(w, g, m, v)
            for ref, val in zip(outs[name], (g, d, nm, nv)):
                ref[idx] = val

        for l in range(DEPTH):
            finish("g_mix", (slice(l, l + 1), slice(None)), pack_ref[_ROW_G_MIX + l:_ROW_G_MIX + l + 1, :])
            finish("g_mlp", (slice(l, l + 1), slice(None)), pack_ref[_ROW_G_MLP + l:_ROW_G_MLP + l + 1, :])
            finish("conv_b", (slice(l, l + 1), slice(None)), pack_ref[_ROW_CONV + 4 * l + 3:_ROW_CONV + 4 * l + 4, :])
            finish("sinks", (slice(l, l + 1), slice(None)), pack_ref[_ROW_SINKS + l:_ROW_SINKS + l + 1, 0:N_Q_HEADS])
            for hf in range(2):
                finish("b_gates", (slice(l, l + 1), slice(hf * D, (hf + 1) * D)),
                       pack_ref[_ROW_B_GATES + 2 * l + hf:_ROW_B_GATES + 2 * l + hf + 1, :])
        finish("g_final", (slice(0, 1), slice(None)), pack_ref[_ROW_G_FINAL:_ROW_G_FINAL + 1, :])

        def conv_w_chip(js):
            for l in range(DEPTH):
                for k in range(3):
                    row = _ROW_CONV + 4 * l + k
                    finish("conv_w", (l, slice(k, k + 1), slice(None)), pack_ref[row:row + 1, js * QW:(js + 1) * QW])

        _for_my_chip(j, conv_w_chip)

    vm = pl.BlockSpec(memory_space=pltpu.VMEM)
    ins = (parts["g_mix"] + parts["b_gates"] + parts["sinks"] + parts["conv"] + parts["g_mlp"] + [parts["loss"]]
           + [params[n] for n in names] + [moms[n] for n in names] + [vels[n] for n in names])
    out_shape = [jax.ShapeDtypeStruct((1, 1), F32)]
    for n in names:
        out_shape += [jax.ShapeDtypeStruct(params[n].shape, F32)] * 4
    res = pl.pallas_call(
        body, out_shape=out_shape, in_specs=[vm] * len(ins), out_specs=[vm] * len(out_shape),
        scratch_shapes=[pltpu.VMEM((SMALL_ROWS, D), F32), pltpu.VMEM((n_dev, SMALL_ROWS, D), F32),
                        pltpu.SemaphoreType.DMA((n_dev - 1,)), pltpu.SemaphoreType.DMA((n_dev - 1,))],
        name="small_allreduce_adamw")(*ins)
    loss = res[0]
    out = {n: res[1 + 4 * i:5 + 4 * i] for i, n in enumerate(names)}
    return loss, out


def kernel(x, g_mix, w_in, b_gates, sinks, w_attn_out, conv_w, conv_b, w_conv_out, w_o, g_mlp, w_up, w_down, g_final, loss_target, m_g_mix, m_w_in, m_b_gates, m_sinks, m_w_attn_out, m_conv_w, m_conv_b, m_w_conv_out, m_w_o, m_g_mlp, m_w_up, m_w_down, m_g_final, v_g_mix, v_w_in, v_b_gates, v_sinks, v_w_attn_out, v_conv_w, v_conv_b, v_w_conv_out, v_w_o, v_g_mlp, v_w_up, v_w_down, v_g_final):
    B, S, D = x.shape
    T = B * S
    big_w = dict(w_in=w_in, w_attn_out=w_attn_out, w_conv_out=w_conv_out, w_o=w_o, w_up=w_up, w_down=w_down)
    big_m = dict(w_in=m_w_in, w_attn_out=m_w_attn_out, w_conv_out=m_w_conv_out, w_o=m_w_o, w_up=m_w_up, w_down=m_w_down)
    big_v = dict(w_in=v_w_in, w_attn_out=v_w_attn_out, w_conv_out=v_w_conv_out, w_o=v_w_o, w_up=v_w_up, w_down=v_w_down)

    c_arr = lax.axis_index("c").astype(jnp.int32).reshape(1)
    j_arr = (2 * lax.axis_index("x") + lax.axis_index("y")).astype(jnp.int32).reshape(1)
    jc_arr = jnp.concatenate([j_arr, c_arr])
    order = [(n, l) for n, _, _, _ in BIG for l in range(DEPTH)]
    dims = {n: (r, c_, k) for n, r, c_, k in BIG}

    wdims = dict(dims, w_up=(D, D_FF, "chip"))
    full = {(n, l): _cast_into_full(j_arr, big_w[n], l, *wdims[n]) for n, l in order}
    mixers = ("w_attn_out", "w_conv_out", "w_o")

    def gather_of(keys):
        return _gather_comm([(full[k],) + wdims[k[0]] for k in keys])

    carried = {("proj", 0): [(n, 0) for n in mixers], ("attn", 0): [("w_up", 0), ("w_down", 0)],
               ("mlp", 0): [("w_in", 1)] + [(n, 1) for n in mixers], ("attn", 1): [("w_up", 1), ("w_down", 1)]}

    def carry(fn, where, *args):
        keys = carried.get(where)
        if keys is None:
            return fn(*args)
        res, got = fn(*args, comm=gather_of(keys))
        full.update(zip(keys, got))
        return res

    first = _run_comm(_gather_comm([(full["w_in", 0],) + dims["w_in"]], conv_w), "gather_weights")
    full["w_in", 0] = first[0]
    conv_w_full = jnp.transpose(first[1], (0, 2, 1, 3)).reshape(DEPTH, 3, D)
    attn_bias = _attn_bias_table()

    xs = [x.reshape(T, D)]
    saved = []
    for l in range(DEPTH):
        ht, pqkv, pconv, pgate = carry(_norm_proj, ("proj", l), xs[-1], g_mix[l:l + 1], full["w_in", l])
        att, att_t, lse = carry(_attn_fwd, ("attn", l), pqkv, sinks[l], attn_bias, S)
        cv, cv_t = _conv_fwd(pconv, conv_w_full[l], conv_b[l:l + 1], S)
        x1, ya, yc, mg_t = _mix_fwd(xs[-1], att, cv, pgate, b_gates[l:l + 1], full["w_attn_out", l], full["w_conv_out", l],
                                    full["w_o", l])
        x2, a = carry(_mlp_fwd, ("mlp", l), x1, g_mlp[l:l + 1], full["w_up", l], full["w_down", l].reshape(N_CHIP, D, D))
        saved.append(dict(ht=ht, pqkv=pqkv, pconv=pconv, pgate=pgate, att=att, att_t=att_t, lse=lse, cv_t=cv_t, x1=x1, ya=ya,
                          yc=yc, mg_t=mg_t, a=a))
        xs.append(x2)

    loss_stats, dx = _loss_bwd(xs[-1], g_final.reshape(1, D), loss_target.reshape(T, D))

    parts = dict(g_mix=[None] * DEPTH, b_gates=[None] * DEPTH, sinks=[None] * DEPTH, conv=[None] * DEPTH,
                 g_mlp=[None] * DEPTH, loss=loss_stats)
    gf, gb, pre, got, mine = {}, {}, {}, {}, {}

    def sibling_exchange(keys):
        return _sibling_exchange_comm([(gb[k],) + dims[k[0]] for k in keys])

    def half_adds(keys, sib):
        for k, s in zip(keys, sib):
            pre[k] = _half_add(jc_arr, gf[k], s, *dims[k[0]])

    def chip_exchange(keys):
        return _chip_exchange_comm([pre[k][0] for k in keys])

    def owner_sums(keys):
        for n, l in keys:
            mine[n] = _owner_sum(jc_arr, pre[n, l][1], got[n, l], l, mine.get(n))

    def run(fn, comms, *args):
        if not comms:
            return fn(*args), []
        res, arrived = fn(*args, comm=_compose(*comms))
        outs, pos = [], 0
        for cm in comms:
            outs.append(arrived[pos:pos + len(cm.out_shape)])
            pos += len(cm.out_shape)
        return res, outs

    assert DEPTH == 2
    upper = [(n, 1) for n, _, _, _ in BIG]
    early = [(n, 0) for n in mixers + ("w_up", "w_down")]
    for l in reversed(range(DEPTH)):
        W = {n: full[(n, l)] for n in big_w}
        sv = saved[l]
        last = l == 0
        mlp_args = (dx, sv["x1"], sv["a"], g_mlp[l:l + 1], W["w_up"], W["w_down"].reshape(N_CHIP, D, D))
        (dx1, da, u_t, h2_t, dyb, parts["g_mlp"][l]), arrived = run(_mlp_bwd, [chip_exchange(upper)] if last else [], *mlp_args)
        if last:
            got.update(zip(upper, arrived[0]))
        gf["w_up", l], gb["w_up", l] = _dw(h2_t, da)
        gf["w_down", l], gb["w_down", l] = _dw(u_t, dyb.reshape(-1, _dw_chunk(T), D))
        datt, dcv, dya, dyc, dgate, parts["b_gates"][l] = _mix_bwd(
            dx1, sv["ya"], sv["yc"], sv["pgate"], b_gates[l:l + 1], W["w_attn_out"], W["w_conv_out"], W["w_o"])
        gf["w_o", l], gb["w_o", l] = _dw(sv["mg_t"], dx1)
        gf["w_attn_out", l], gb["w_attn_out", l] = _dw(sv["att_t"], dya)
        gf["w_conv_out", l], gb["w_conv_out", l] = _dw(sv["cv_t"], dyc)
        conv_args = (dcv, sv["pconv"], conv_w_full[l], conv_b[l:l + 1], S)
        (dconv, parts["conv"][l]), arrived = run(_conv_bwd, [sibling_exchange(early)] if last else [], *conv_args)
        if last:
            half_adds(early, arrived[0])
        attn_args = (sv["pqkv"], datt, sv["lse"], sinks[l], attn_bias, S)
        (dq, dkv, parts["sinks"][l]), arrived = run(_attn_bwd, [chip_exchange(early)] if last else [], *attn_args)
        if last:
            got.update(zip(early, arrived[0]))
            owner_sums(upper + early)
        pieces = [(dq, D), (dkv, QKV_W - D), (dconv, CONV_W), (dgate, GATE_W)]
        gf["w_in", l], gb["w_in", l] = _dw_pieces(sv["ht"], pieces)
        in_args = (pieces, W["w_in"], xs[l], dx1, g_mix[l:l + 1])
        if not last:
            (dx, parts["g_mix"][l]), arrived = run(_inproj_bwd, [sibling_exchange(upper)], *in_args)
            half_adds(upper, arrived[0])
        else:
            tail = [("w_in", 0)]
            half_adds(tail, _run_comm(sibling_exchange(tail), "grad_sibling_exchange"))
            done = _sibling_assemble_comm(list(mine.values()), [(1,) if n == "w_in" else (0, 1) for n in mine])
            (dx, parts["g_mix"][l]), arrived = run(_inproj_bwd, [chip_exchange(tail), done], *in_args)
            got.update(zip(tail, arrived[0]))
            mine = dict(zip(mine, arrived[1]))
            owner_sums(tail)
            mine["w_in"] = _run_comm(_sibling_assemble_comm([mine["w_in"]], [(0,)]), "grad_sibling_assemble")[0]
    grads = mine

    res = {}
    for n in big_w:
        res[n] = tuple(_adamw(big_w[n], grads[n], big_m[n], big_v[n]))

    small_p = dict(g_mix=g_mix, b_gates=b_gates, sinks=sinks, conv_w=conv_w, conv_b=conv_b, g_mlp=g_mlp, g_final=g_final.reshape(1, D))
    small_m = dict(g_mix=m_g_mix, b_gates=m_b_gates, sinks=m_sinks, conv_w=m_conv_w, conv_b=m_conv_b, g_mlp=m_g_mlp,
                   g_final=m_g_final.reshape(1, D))
    small_v = dict(g_mix=v_g_mix, b_gates=v_b_gates, sinks=v_sinks, conv_w=v_conv_w, conv_b=v_conv_b, g_mlp=v_g_mlp,
                   g_final=v_g_final.reshape(1, D))
    loss, small = _small_step(parts, small_p, small_m, small_v)
    for n, vals in small.items():
        res[n] = tuple(v.reshape(D) for v in vals) if n == "g_final" else tuple(vals)

    weights = ["g_mix", "w_in", "b_gates", "sinks", "w_attn_out", "conv_w", "conv_b", "w_conv_out", "w_o", "g_mlp", "w_up",
               "w_down", "g_final"]
    out = [loss.reshape(()), dx.reshape(B, S, D)]
    for k in range(4):
        out += [res[n][k] for n in weights]
    return tuple(out)
```

```python
import functools

import numpy as np
import jax
import jax.numpy as jnp
from jax import lax
from jax.experimental import pallas as pl
from jax.experimental.pallas import tpu as pltpu

F32 = jnp.float32
BF16 = jnp.bfloat16

D_MODEL = 1024
HEAD_DIM = 64
N_Q_HEADS = 16
N_KV_HEADS = 4
GQA_GROUP = 4
WINDOW = 128
D_FF = 4096
DEPTH = 2
RMS_EPS = 1e-6
NEG_INF = -1e30
ATTN_SCALE = HEAD_DIM ** -0.5
QKV_W = 1536
CONV_W = 3072
GATE_W = 2048
IN_COLS = QKV_W + CONV_W + GATE_W
COL_TILE = 512
N_CHIP = 4
ADAM_LR = 0.001
ADAM_B1 = 0.9
ADAM_B2 = 0.999
ADAM_EPS = 1e-08
ADAM_WD = 0.01
ADAM_STEP = 10
V7X_VMEM_BYTES = 64 * 2 ** 20
VMEM_LIMIT = V7X_VMEM_BYTES - 8 * 2 ** 20
MESH = pl.DeviceIdType.MESH
ANY = pl.BlockSpec(memory_space=pl.ANY)
SMALL_ROWS = 24

_SLOPES = [float(v) for v in np.power(np.float32(2.0), -8.0 * np.arange(1, N_Q_HEADS + 1, dtype=np.float32) / N_Q_HEADS)]


def _params(*sem):
    return pltpu.CompilerParams(dimension_semantics=sem, vmem_limit_bytes=VMEM_LIMIT)


class _Hosted:
    def __init__(self, inputs, out_shape, aliases, scratch, start, finish):
        self.inputs, self.out_shape, self.aliases, self.scratch = list(inputs), list(out_shape), dict(aliases), list(scratch)
        self.start, self.finish = start, finish


def _hosted_call(comm, kern, *, out_shape, grid, in_specs, out_specs, args, name, sem, scratch_shapes=()):
    single = not isinstance(out_shape, (tuple, list))
    outs = [out_shape] if single else list(out_shape)
    ospecs = [out_specs] if single else list(out_specs)
    if comm is None:
        res = pl.pallas_call(kern, out_shape=outs, grid=grid, in_specs=list(in_specs), out_specs=ospecs,
                             scratch_shapes=list(scratch_shapes), name=name, compiler_params=_params(*sem))(*args)
        return res[0] if single else res
    n_in, n_out, n_scr = len(args), len(outs), len(scratch_shapes)
    ci, co, cs = len(comm.inputs), len(comm.out_shape), len(comm.scratch)

    def body(*refs):
        cuts = np.cumsum([0, n_in, ci, n_out, co, n_scr, cs])
        a, b, c, d, e, f = [refs[lo:hi] for lo, hi in zip(cuts[:-1], cuts[1:])]
        ids = [pl.program_id(k) for k in range(len(grid))]
        first = functools.reduce(jnp.logical_and, [i == 0 for i in ids])
        last = functools.reduce(jnp.logical_and, [i == n - 1 for i, n in zip(ids, grid)])
        pl.when(first)(lambda: comm.start(b, d, f))
        kern(*a, *c, *e)
        pl.when(last)(lambda: comm.finish(b, d, f))

    res = pl.pallas_call(
        body, out_shape=outs + comm.out_shape, grid=grid, in_specs=list(in_specs) + [ANY] * ci, out_specs=ospecs + [ANY] * co,
        scratch_shapes=list(scratch_shapes) + comm.scratch,
        input_output_aliases={n_in + i: n_out + o for i, o in comm.aliases.items()},
        name=name + "_carrier", compiler_params=_params(*(["arbitrary"] * len(grid))))(*args, *comm.inputs)
    main = res[:n_out]
    return (main[0] if single else main), res[n_out:]


def _nt(a, b):
    return lax.dot_general(a, b, (((1,), (1,)), ((), ())), preferred_element_type=F32)


def _tn(a, b):
    return lax.dot_general(a, b, (((0,), (0,)), ((), ())), preferred_element_type=F32)


def _nn(a, b):
    return jnp.dot(a, b, preferred_element_type=F32)


def _rms_stats(xf):
    r = lax.rsqrt(jnp.mean(xf * xf, axis=-1, keepdims=True) + RMS_EPS)
    return r, xf * r


def _rms_bwd(dh, xh, r, g):
    dxh = dh * g
    dx = r * (dxh - xh * jnp.mean(dxh * xh, axis=-1, keepdims=True))
    dg = jnp.sum(dh * xh, axis=0, keepdims=True)
    return dx, dg


def _dw_chunk(T):
    return min(2048, T)


def _row_halves(tm):
    return (slice(0, tm // 2), slice(tm // 2, tm))


def _resident(shape):
    return pl.BlockSpec(shape, lambda *_: (0,) * len(shape), pipeline_mode=pl.Buffered(1))


def _norm_proj(x, g, w, comm=None):
    T, D = x.shape
    tm = min(512, T)
    tk = _dw_chunk(T)
    per = tk // tm
    widths = (QKV_W, CONV_W, GATE_W)

    def kern(x_ref, g_ref, w_ref, ht_ref, *o_refs):
        _, xh = _rms_stats(x_ref[...])
        h = (xh * g_ref[...]).astype(BF16)
        ht_ref[...] = h.T
        off = 0
        for o_ref, wd in zip(o_refs, widths):
            o_ref[...] = _nn(h, w_ref[:, off:off + wd]).astype(BF16)
            off += wd

    row = lambda wd: pl.BlockSpec((tm, wd), lambda i: (i, 0))
    return _hosted_call(
        comm, kern,
        out_shape=[jax.ShapeDtypeStruct((T // tk, D, tk), BF16)] + [jax.ShapeDtypeStruct((T, wd), BF16) for wd in widths],
        grid=(T // tm,), in_specs=[row(D), pl.BlockSpec((1, D), lambda i: (0, 0)), _resident((D, IN_COLS))],
        out_specs=[pl.BlockSpec((None, D, tm), lambda i: (i // per, 0, i % per))] + [row(wd) for wd in widths],
        name="norm_proj", sem=("parallel",), args=(x, g, w))


GW = GQA_GROUP * WINDOW
BAND = 2 * WINDOW
KV_W = N_KV_HEADS * HEAD_DIM


def _attn_bias_table():
    jj = np.arange(BAND)[:, None]
    col = np.arange(GW)[None, :]
    dist = WINDOW + (col % WINDOW) - jj
    valid = (dist >= 0) & (dist < WINDOW)
    slopes = np.asarray(_SLOPES, np.float32).reshape(N_KV_HEADS, GQA_GROUP)
    tab = np.empty((2, N_KV_HEADS, BAND, GW), np.float32)
    for hk in range(N_KV_HEADS):
        bias = -slopes[hk][col // WINDOW] * dist.astype(np.float32)
        tab[0, hk] = np.where(valid, bias, np.float32(NEG_INF))
        tab[1, hk] = np.where(valid & (jj >= WINDOW), bias, np.float32(NEG_INF))
    return jnp.asarray(tab)


def _stack_heads(ref, hk):
    return jnp.concatenate(
        [ref[:, HEAD_DIM * (GQA_GROUP * hk + g): HEAD_DIM * (GQA_GROUP * hk + g + 1)] for g in range(GQA_GROUP)], axis=0)


def _kv_band(cur_ref, prev_ref, hk):
    k0 = N_Q_HEADS * HEAD_DIM
    sl = slice(HEAD_DIM * hk, HEAD_DIM * (hk + 1))
    ksl, vsl = slice(k0 + sl.start, k0 + sl.stop), slice(k0 + KV_W + sl.start, k0 + KV_W + sl.stop)
    k_band = jnp.concatenate([prev_ref[:, sl], cur_ref[:, ksl]], axis=0)
    v_band = jnp.concatenate([prev_ref[:, KV_W + sl.start:KV_W + sl.stop], cur_ref[:, vsl]], axis=0)
    return k_band, v_band


def _lane_row(vals):
    return jnp.concatenate([jnp.full((1, WINDOW), v, F32) for v in vals], axis=1)


def _attn_fwd(pqkv, sinks, bias, seq, comm=None):
    T = pqkv.shape[0]
    nblk = seq // WINDOW

    def kern(sink_ref, cur_ref, prev_ref, bias_ref, o_ref, ot_ref, lse_ref):
        i = pl.program_id(0)
        first_i = ((i % nblk) == 0).astype(jnp.int32)
        bands = [_kv_band(cur_ref, prev_ref, hk) for hk in range(N_KV_HEADS)]
        sts = [_nt(bands[hk][0], _stack_heads(cur_ref, hk) * ATTN_SCALE) + bias_ref[first_i, hk] for hk in range(N_KV_HEADS)]
        ps, scales = [], []
        for hk in range(N_KV_HEADS):
            heads = [GQA_GROUP * hk + g for g in range(GQA_GROUP)]
            sink = _lane_row([sink_ref[h] for h in heads])
            m = jnp.maximum(jnp.max(sts[hk], axis=0, keepdims=True), sink)
            p = jnp.exp(sts[hk] - m)
            den = jnp.sum(p, axis=0, keepdims=True) + jnp.exp(sink - m)
            lse = m + jnp.log(den)
            for g, h in enumerate(heads):
                lse_ref[h:h + 1, :] = lse[:, WINDOW * g:WINDOW * (g + 1)]
            ps.append(p.astype(BF16))
            scales.append(1.0 / den)
        for hk in range(N_KV_HEADS):
            ot = _tn(bands[hk][1], ps[hk]) * scales[hk]
            for g in range(GQA_GROUP):
                h = GQA_GROUP * hk + g
                ot_ref[HEAD_DIM * h:HEAD_DIM * (h + 1), :] = ot[:, WINDOW * g:WINDOW * (g + 1)].astype(BF16)
        o_ref[...] = ot_ref[...].T

    return _hosted_call(
        comm, kern,
        out_shape=(jax.ShapeDtypeStruct((T, D_MODEL), BF16), jax.ShapeDtypeStruct((D_MODEL, T), BF16),
                   jax.ShapeDtypeStruct((N_Q_HEADS, T), F32)),
        grid=(T // WINDOW,),
        in_specs=[pl.BlockSpec(memory_space=pltpu.SMEM),
                  pl.BlockSpec((WINDOW, QKV_W), lambda i: (i, 0)),
                  pl.BlockSpec((WINDOW, 2 * KV_W), lambda i: (jnp.maximum(i - 1, 0), 2)),
                  _resident(bias.shape)],
        out_specs=(pl.BlockSpec((WINDOW, D_MODEL), lambda i: (i, 0)), pl.BlockSpec((D_MODEL, WINDOW), lambda i: (0, i)),
                   pl.BlockSpec((N_Q_HEADS, WINDOW), lambda i: (0, i))),
        name="attn_fwd", sem=("parallel",), args=(sinks, pqkv, pqkv, bias))


def _pick_row(a, row):
    rid = lax.broadcasted_iota(jnp.int32, a.shape, 0)
    return jnp.sum(jnp.where(rid == row, a, 0.0), axis=0, keepdims=True)


def _conv_taps(yc, halo_yc, first_i):
    keep = (1 - first_i).astype(F32)
    p1 = _pick_row(halo_yc, 15) * keep
    p2 = _pick_row(halo_yc, 14) * keep
    rowid = lax.broadcasted_iota(jnp.int32, yc.shape, 0)
    s1 = jnp.where(rowid == 0, p1, pltpu.roll(yc, 1, 0))
    s2 = jnp.where(rowid == 0, p2, jnp.where(rowid == 1, p1, pltpu.roll(yc, 2, 0)))
    return s1, s2


def _conv_fwd(pconv, conv_w, conv_b, seq):
    T = pconv.shape[0]
    tm = min(512, seq)
    per_seq = seq // tm
    D = D_MODEL

    def kern(cur_ref, halo_ref, w_ref, b_ref, o_ref, ot_ref):
        i = pl.program_id(0)
        first_i = ((i % per_seq) == 0).astype(jnp.int32)
        cb = cur_ref[:, 0:D].astype(F32)
        yc = cur_ref[:, D:2 * D].astype(F32) * cur_ref[:, 2 * D:3 * D].astype(F32)
        halo_yc = halo_ref[:, D:2 * D].astype(F32) * halo_ref[:, 2 * D:3 * D].astype(F32)
        s1, s2 = _conv_taps(yc, halo_yc, first_i)
        z = w_ref[0:1, :] * s2 + w_ref[1:2, :] * s1 + w_ref[2:3, :] * yc
        cv = (cb * (z + b_ref[...])).astype(BF16)
        o_ref[...] = cv
        ot_ref[...] = cv.T

    return pl.pallas_call(
        kern, out_shape=(jax.ShapeDtypeStruct((T, D), BF16), jax.ShapeDtypeStruct((D, T), BF16)), grid=(T // tm,),
        in_specs=[pl.BlockSpec((tm, CONV_W), lambda i: (i, 0)),
                  pl.BlockSpec((16, CONV_W), lambda i: (jnp.maximum(i * (tm // 16) - 1, 0), 0)),
                  pl.BlockSpec((3, D), lambda i: (0, 0)), pl.BlockSpec((1, D), lambda i: (0, 0))],
        out_specs=(pl.BlockSpec((tm, D), lambda i: (i, 0)), pl.BlockSpec((D, tm), lambda i: (0, i))),
        name="conv_fwd", compiler_params=_params("parallel"))(pconv, pconv, conv_w, conv_b)


def _mix_fwd(x, att, cv, pgate, b_gates, wao, wco, wo):
    T, D = x.shape
    tm = min(512, T)

    halves = _row_halves(tm)

    def kern(x_ref, att_ref, cv_ref, pg_ref, bg_ref, wao_ref, wco_ref, wo_ref, x1_ref, ya_ref, yc_ref, mgt_ref):
        yas = [_nn(att_ref[rows, :], wao_ref[...]) for rows in halves]
        ycs = [_nn(cv_ref[rows, :], wco_ref[...]) for rows in halves]
        mgs = []
        for rows, ya, yc in zip(halves, yas, ycs):
            sa = jax.nn.sigmoid(pg_ref[rows, 0:D].astype(F32) + bg_ref[:, 0:D])
            sc = jax.nn.sigmoid(pg_ref[rows, D:2 * D].astype(F32) + bg_ref[:, D:2 * D])
            mg = (sa * ya + sc * yc).astype(BF16)
            ya_ref[rows, :] = ya.astype(BF16)
            yc_ref[rows, :] = yc.astype(BF16)
            mgt_ref[:, rows] = mg.T
            mgs.append(mg)
        for rows, mg in zip(halves, mgs):
            x1_ref[rows, :] = x_ref[rows, :] + _nn(mg, wo_ref[...])

    row = lambda w: pl.BlockSpec((tm, w), lambda i: (i, 0))
    full = lambda a, b: pl.BlockSpec((a, b), lambda i: (0, 0))
    bf = jax.ShapeDtypeStruct((T, D), BF16)
    return pl.pallas_call(
        kern, out_shape=(jax.ShapeDtypeStruct((T, D), F32), bf, bf, jax.ShapeDtypeStruct((D, T), BF16)), grid=(T // tm,),
        in_specs=[row(D), row(D), row(D), row(GATE_W), full(1, GATE_W)] + [_resident((D, D))] * 3,
        out_specs=(row(D), row(D), row(D), pl.BlockSpec((D, tm), lambda i: (0, i))),
        name="mix_fwd", compiler_params=_params("parallel"))(x, att, cv, pgate, b_gates, wao, wco, wo)


def _mlp_fwd(x1, g, wup, wdn, comm=None):
    T, D = x1.shape
    tm = min(1024, T)
    nj = D_FF // D

    def kern(x_ref, g_ref, wup_ref, wdn_ref, x2_ref, a_ref, h_scr, acc_scr):
        j = pl.program_id(1)

        @pl.when(j == 0)
        def _():
            xf = x_ref[...]
            _, xh = _rms_stats(xf)
            h_scr[...] = (xh * g_ref[...]).astype(BF16)
            acc_scr[...] = xf

        a = _nn(h_scr[...], wup_ref[j])
        a_ref[...] = a.astype(BF16)
        u = jnp.square(jnp.maximum(a, 0.0)).astype(BF16)
        acc_scr[...] += _nn(u, wdn_ref[j])

        @pl.when(j == nj - 1)
        def _():
            x2_ref[...] = acc_scr[...]

    return _hosted_call(
        comm, kern, out_shape=(jax.ShapeDtypeStruct((T, D), F32), jax.ShapeDtypeStruct((T, D_FF), BF16)), grid=(T // tm, nj),
        in_specs=[pl.BlockSpec((tm, D), lambda i, j: (i, 0)), pl.BlockSpec((1, D), lambda i, j: (0, 0)),
                  _resident((nj, D, D)), _resident((nj, D, D))],
        out_specs=(pl.BlockSpec((tm, D), lambda i, j: (i, 0)), pl.BlockSpec((tm, D), lambda i, j: (i, j))),
        scratch_shapes=[pltpu.VMEM((tm, D), BF16), pltpu.VMEM((tm, D), F32)],
        name="mlp_fwd", sem=("parallel", "arbitrary"), args=(x1, g, wup, wdn))


def _loss_bwd(x, g, tgt):
    T, D = x.shape
    tm = min(512, T)

    def kern(x_ref, g_ref, t_ref, st_ref, dx_ref):
        i = pl.program_id(0)

        @pl.when(i == 0)
        def _():
            st_ref[...] = jnp.zeros_like(st_ref)

        gg = g_ref[...]
        r, xh = _rms_stats(x_ref[...])
        e = xh * gg - t_ref[...]
        part = 0.5 * jnp.sum(jnp.mean(e * e, axis=-1, keepdims=True), axis=0, keepdims=True)
        dx, dg = _rms_bwd(e * (1.0 / D), xh, r, gg)
        dx_ref[...] = dx
        st_ref[0:1, :] += dg
        st_ref[1:2, 0:1] += part

    return pl.pallas_call(
        kern, out_shape=(jax.ShapeDtypeStruct((8, D), F32), jax.ShapeDtypeStruct((T, D), F32)), grid=(T // tm,),
        in_specs=[pl.BlockSpec((tm, D), lambda i: (i, 0)), pl.BlockSpec((1, D), lambda i: (0, 0)),
                  pl.BlockSpec((tm, D), lambda i: (i, 0))],
        out_specs=(pl.BlockSpec((8, D), lambda i: (0, 0)), pl.BlockSpec((tm, D), lambda i: (i, 0))),
        name="loss_bwd", compiler_params=_params("arbitrary"))(x, g, tgt)


def _mlp_bwd(dx2, x1, a, g, wup, wdn, comm=None):
    T, D = x1.shape
    tm = min(512, T)
    nj = D_FF // D
    tk = _dw_chunk(T)
    per = tk // tm
    chunked = pl.BlockSpec((None, D, tm), lambda i, j: (i // per, 0, i % per))

    def kern(dx2_ref, x1_ref, a_ref, g_ref, wup_ref, wdn_ref, dx1_ref, da_ref, h2t_ref, dyt_ref, dg_ref, dyb_scr, acc_scr):
        i, j = pl.program_id(0), pl.program_id(1)

        @pl.when((i == 0) & (j == 0))
        def _():
            dg_ref[...] = jnp.zeros_like(dg_ref)

        @pl.when(j == 0)
        def _():
            dyb = dx2_ref[...].astype(BF16)
            dyb_scr[...] = dyb
            dyt_ref[...] = dyb.T
            acc_scr[...] = jnp.zeros_like(acc_scr)

        du = _nt(dyb_scr[...], wdn_ref[j])
        da = (du * (2.0 * jnp.maximum(a_ref[...].astype(F32), 0.0))).astype(BF16)
        da_ref[...] = da
        acc_scr[...] += _nt(da, wup_ref[j])

        @pl.when(j == nj - 1)
        def _():
            gg = g_ref[...]
            r, xh = _rms_stats(x1_ref[...])
            h2t_ref[...] = (xh * gg).astype(BF16).T
            dx, dg = _rms_bwd(acc_scr[...], xh, r, gg)
            dx1_ref[...] = dx2_ref[...] + dx
            dg_ref[...] += dg

    return _hosted_call(
        comm, kern,
        out_shape=(jax.ShapeDtypeStruct((T, D), F32), jax.ShapeDtypeStruct((T, D_FF), BF16),
                   jax.ShapeDtypeStruct((T // tk, D, tk), BF16), jax.ShapeDtypeStruct((T // tk, D, tk), BF16),
                   jax.ShapeDtypeStruct((1, D), F32)),
        grid=(T // tm, nj),
        in_specs=[pl.BlockSpec((tm, D), lambda i, j: (i, 0)), pl.BlockSpec((tm, D), lambda i, j: (i, 0)),
                  pl.BlockSpec((tm, D), lambda i, j: (i, j)), pl.BlockSpec((1, D), lambda i, j: (0, 0)),
                  _resident((nj, D, D)), _resident((nj, D, D))],
        out_specs=(pl.BlockSpec((tm, D), lambda i, j: (i, 0)), pl.BlockSpec((tm, D), lambda i, j: (i, j)),
                   chunked, chunked, pl.BlockSpec((1, D), lambda i, j: (0, 0))),
        scratch_shapes=[pltpu.VMEM((tm, D), BF16), pltpu.VMEM((tm, D), F32)],
        name="mlp_bwd", sem=("arbitrary", "arbitrary"), args=(dx2, x1, a, g, wup, wdn))


def _mix_bwd(dx1, ya, yc, pgate, b_gates, wao, wco, wo):
    T, D = dx1.shape
    tm = min(512, T)
    halves = _row_halves(tm)

    def kern(dx_ref, ya_ref, yc_ref, pg_ref, bg_ref, wao_ref, wco_ref, wo_ref,
             datt_ref, dcv_ref, dya_ref, dyc_ref, dgt_ref, dbg_ref):
        @pl.when(pl.program_id(0) == 0)
        def _():
            dbg_ref[...] = jnp.zeros_like(dbg_ref)

        dms = [_nt(dx_ref[rows, :].astype(BF16), wo_ref[...]) for rows in halves]
        dys = []
        for rows, dm in zip(halves, dms):
            sa = jax.nn.sigmoid(pg_ref[rows, 0:D].astype(F32) + bg_ref[:, 0:D])
            sc = jax.nn.sigmoid(pg_ref[rows, D:2 * D].astype(F32) + bg_ref[:, D:2 * D])
            dya = (dm * sa).astype(BF16)
            dyc = (dm * sc).astype(BF16)
            dga = dm * ya_ref[rows, :].astype(F32) * (sa * (1.0 - sa))
            dgc = dm * yc_ref[rows, :].astype(F32) * (sc * (1.0 - sc))
            dya_ref[rows, :] = dya
            dyc_ref[rows, :] = dyc
            dgt_ref[rows, 0:D] = dga.astype(BF16)
            dgt_ref[rows, D:2 * D] = dgc.astype(BF16)
            dbg_ref[:, 0:D] += jnp.sum(dga, axis=0, keepdims=True)
            dbg_ref[:, D:2 * D] += jnp.sum(dgc, axis=0, keepdims=True)
            dys.append((dya, dyc))
        for rows, (dya, dyc) in zip(halves, dys):
            datt_ref[rows, :] = _nt(dya, wao_ref[...]).astype(BF16)
            dcv_ref[rows, :] = _nt(dyc, wco_ref[...]).astype(BF16)

    row = lambda w: pl.BlockSpec((tm, w), lambda i: (i, 0))
    full = lambda a, b: pl.BlockSpec((a, b), lambda i: (0, 0))
    bf = jax.ShapeDtypeStruct((T, D), BF16)
    return pl.pallas_call(
        kern,
        out_shape=(bf, bf, bf, bf, jax.ShapeDtypeStruct((T, GATE_W), BF16), jax.ShapeDtypeStruct((1, GATE_W), F32)),
        grid=(T // tm,),
        in_specs=[row(D), row(D), row(D), row(GATE_W), full(1, GATE_W)] + [_resident((D, D))] * 3,
        out_specs=(row(D), row(D), row(D), row(D), row(GATE_W), full(1, GATE_W)),
        name="mix_bwd", compiler_params=_params("arbitrary"))(dx1, ya, yc, pgate, b_gates, wao, wco, wo)


def _conv_bwd(dcv, pconv, conv_w, conv_b, seq, comm=None):
    T = pconv.shape[0]
    tm = min(512, seq)
    per_seq = seq // tm
    D = D_MODEL
    nb16 = T // 16

    def kern(dcv_ref, dcvn_ref, cur_ref, prev_ref, next_ref, w_ref, b_ref, o_ref, dwb_ref):
        i = pl.program_id(0)

        @pl.when(i == 0)
        def _():
            dwb_ref[...] = jnp.zeros_like(dwb_ref)

        first_i = ((i % per_seq) == 0).astype(jnp.int32)
        keep_next = 1.0 - (((i + 1) % per_seq) == 0).astype(F32)
        cb = cur_ref[:, 0:D].astype(F32)
        cc = cur_ref[:, D:2 * D].astype(F32)
        cu = cur_ref[:, 2 * D:3 * D].astype(F32)
        yc = cc * cu
        halo_yc = prev_ref[:, D:2 * D].astype(F32) * prev_ref[:, 2 * D:3 * D].astype(F32)
        s1, s2 = _conv_taps(yc, halo_yc, first_i)
        w0, w1, w2 = w_ref[0:1, :], w_ref[1:2, :], w_ref[2:3, :]
        z = w0 * s2 + w1 * s1 + w2 * yc
        dcv = dcv_ref[...].astype(F32)
        dz = dcv * cb
        dzn = dcvn_ref[...].astype(F32) * next_ref[:, 0:D].astype(F32) * keep_next
        n1, n2 = _pick_row(dzn, 0), _pick_row(dzn, 1)
        rowid = lax.broadcasted_iota(jnp.int32, dz.shape, 0)
        u1 = jnp.where(rowid == tm - 1, n1, pltpu.roll(dz, tm - 1, 0))
        u2 = jnp.where(rowid == tm - 1, n2, jnp.where(rowid == tm - 2, n1, pltpu.roll(dz, tm - 2, 0)))
        dyc = w2 * dz + w1 * u1 + w0 * u2
        o_ref[:, 0:D] = (dcv * (z + b_ref[...])).astype(BF16)
        o_ref[:, D:2 * D] = (dyc * cu).astype(BF16)
        o_ref[:, 2 * D:3 * D] = (dyc * cc).astype(BF16)
        dwb_ref[0:1, :] += jnp.sum(dz * s2, axis=0, keepdims=True)
        dwb_ref[1:2, :] += jnp.sum(dz * s1, axis=0, keepdims=True)
        dwb_ref[2:3, :] += jnp.sum(dz * yc, axis=0, keepdims=True)
        dwb_ref[3:4, :] += jnp.sum(dz, axis=0, keepdims=True)

    prev_map = lambda i: (jnp.maximum(i * (tm // 16) - 1, 0), 0)
    next_map = lambda i: (jnp.minimum((i + 1) * (tm // 16), nb16 - 1), 0)
    return _hosted_call(
        comm, kern, out_shape=(jax.ShapeDtypeStruct((T, CONV_W), BF16), jax.ShapeDtypeStruct((8, D), F32)), grid=(T // tm,),
        in_specs=[pl.BlockSpec((tm, D), lambda i: (i, 0)), pl.BlockSpec((16, D), next_map),
                  pl.BlockSpec((tm, CONV_W), lambda i: (i, 0)), pl.BlockSpec((16, CONV_W), prev_map),
                  pl.BlockSpec((16, CONV_W), next_map),
                  pl.BlockSpec((3, D), lambda i: (0, 0)), pl.BlockSpec((1, D), lambda i: (0, 0))],
        out_specs=(pl.BlockSpec((tm, CONV_W), lambda i: (i, 0)), pl.BlockSpec((8, D), lambda i: (0, 0))),
        name="conv_bwd", sem=("arbitrary",), args=(dcv, dcv, pconv, pconv, pconv, conv_w, conv_b))


def _attn_bwd(pqkv, datt, lse, sinks, bias, seq, comm=None):
    T = pqkv.shape[0]
    nblk = seq // WINDOW
    nseq = T // seq
    KVW = KV_W

    def kern(sink_ref, cur_ref, prev_ref, do_ref, lse_ref, bias_ref, dq_ref, dkv_ref, ds_ref, kc_scr, vc_scr, dqt_scr, dkvt_scr):
        b, st = pl.program_id(0), pl.program_id(1)

        @pl.when((b == 0) & (st == 0))
        def _():
            ds_ref[...] = jnp.zeros_like(ds_ref)

        @pl.when(st == 0)
        def _():
            kc_scr[...] = jnp.zeros_like(kc_scr)
            vc_scr[...] = jnp.zeros_like(vc_scr)

        @pl.when(st < nblk)
        def _():
            first_i = (st == 0).astype(jnp.int32)
            groups = range(N_KV_HEADS)
            bands = [_kv_band(cur_ref, prev_ref, hk) for hk in groups]
            qt_all = cur_ref[:, 0:D_MODEL].T
            dot_all = do_ref[...].T
            side = lambda t_all, hk: jnp.concatenate(
                [t_all[HEAD_DIM * (GQA_GROUP * hk + g):HEAD_DIM * (GQA_GROUP * hk + g + 1), :] for g in range(GQA_GROUP)], axis=1)
            qts = [side(qt_all, hk) for hk in groups]
            dots = [side(dot_all, hk) for hk in groups]
            sts = [_nn(bands[hk][0], qts[hk] * ATTN_SCALE) + bias_ref[first_i, hk] for hk in groups]
            dps = [_nn(bands[hk][1], dots[hk]) for hk in groups]
            pbs, dsss = [], []
            for hk in groups:
                heads = [GQA_GROUP * hk + g for g in range(GQA_GROUP)]
                sink = _lane_row([sink_ref[h] for h in heads])
                lse_g = jnp.concatenate([lse_ref[h:h + 1, :] for h in heads], axis=1)
                p = jnp.exp(sts[hk] - lse_g)
                d_row = jnp.sum(p * dps[hk], axis=0, keepdims=True)
                dsss.append((p * (dps[hk] - d_row) * ATTN_SCALE).astype(BF16))
                pbs.append(p.astype(BF16))
                psd = jnp.exp(sink - lse_g) * d_row
                for g, h in enumerate(heads):
                    ds_ref[0:1, h:h + 1] -= jnp.sum(psd[:, WINDOW * g:WINDOW * (g + 1)], axis=1, keepdims=True)
            for hk in groups:
                dqt = _tn(bands[hk][0], dsss[hk])
                dkt = _nt(qts[hk], dsss[hk])
                dvt = _nt(dots[hk], pbs[hk])
                for g in range(GQA_GROUP):
                    h = GQA_GROUP * hk + g
                    dqt_scr[HEAD_DIM * h:HEAD_DIM * (h + 1), :] = dqt[:, WINDOW * g:WINDOW * (g + 1)].astype(BF16)
                ksl = slice(HEAD_DIM * hk, HEAD_DIM * (hk + 1))
                vsl = slice(KVW + HEAD_DIM * hk, KVW + HEAD_DIM * (hk + 1))
                dkvt_scr[ksl, :] = (kc_scr[ksl, :] + dkt[:, 0:WINDOW]).astype(BF16)
                dkvt_scr[vsl, :] = (vc_scr[ksl, :] + dvt[:, 0:WINDOW]).astype(BF16)
                kc_scr[ksl, :] = dkt[:, WINDOW:BAND]
                vc_scr[ksl, :] = dvt[:, WINDOW:BAND]
            dq_ref[...] = dqt_scr[...].T
            dkv_ref[...] = dkvt_scr[...].T

        @pl.when(st == nblk)
        def _():
            dkvt_scr[0:KVW, :] = kc_scr[...].astype(BF16)
            dkvt_scr[KVW:2 * KVW, :] = vc_scr[...].astype(BF16)
            dkv_ref[...] = dkvt_scr[...].T

    cur_map = lambda b, s: (b * nblk + jnp.minimum(s, nblk - 1), 0)
    prev_row = lambda b, s: b * nblk + jnp.clip(s - 1, 0, nblk - 1)
    return _hosted_call(
        comm, kern,
        out_shape=(jax.ShapeDtypeStruct((T, D_MODEL), BF16), jax.ShapeDtypeStruct((T, 2 * KVW), BF16),
                   jax.ShapeDtypeStruct((8, 128), F32)),
        grid=(nseq, nblk + 1),
        in_specs=[pl.BlockSpec(memory_space=pltpu.SMEM),
                  pl.BlockSpec((WINDOW, QKV_W), cur_map),
                  pl.BlockSpec((WINDOW, 2 * KVW), lambda b, s: (prev_row(b, s), 2)),
                  pl.BlockSpec((WINDOW, D_MODEL), cur_map),
                  pl.BlockSpec((N_Q_HEADS, WINDOW), lambda b, s: (0, b * nblk + jnp.minimum(s, nblk - 1))),
                  _resident(bias.shape)],
        out_specs=(pl.BlockSpec((WINDOW, D_MODEL), cur_map),
                   pl.BlockSpec((WINDOW, 2 * KVW), lambda b, s: (prev_row(b, s), 0)),
                   pl.BlockSpec((8, 128), lambda b, s: (0, 0))),
        scratch_shapes=[pltpu.VMEM((KVW, WINDOW), F32), pltpu.VMEM((KVW, WINDOW), F32), pltpu.VMEM((D_MODEL, WINDOW), BF16),
                        pltpu.VMEM((2 * KVW, WINDOW), BF16)],
        name="attn_bwd", sem=("arbitrary", "arbitrary"), args=(sinks, pqkv, pqkv, datt, lse, bias))


def _piece_tiles(pieces):
    out, start = [], 0
    for arr, width in pieces:
        out.append((arr, start, width // COL_TILE))
        start += width // COL_TILE
    return out, start


def _inproj_bwd(pieces, w_in, x, dx_in, g, comm=None):
    T, D = x.shape
    tm = min(512, T)

    def kern(*refs):
        p_refs = refs[:len(pieces)]
        w_ref, x_ref, dxin_ref, g_ref, dx_ref, dg_ref = refs[len(pieces):]

        @pl.when(pl.program_id(0) == 0)
        def _():
            dg_ref[...] = jnp.zeros_like(dg_ref)

        dh, off = None, 0
        for p_ref, (_, width) in zip(p_refs, pieces):
            part = _nt(p_ref[...], w_ref[:, off:off + width])
            dh = part if dh is None else dh + part
            off += width
        gg = g_ref[...]
        r, xh = _rms_stats(x_ref[...])
        dx, dg = _rms_bwd(dh, xh, r, gg)
        dx_ref[...] = dxin_ref[...] + dx
        dg_ref[...] += dg

    row = lambda wd: pl.BlockSpec((tm, wd), lambda i: (i, 0))
    return _hosted_call(
        comm, kern, out_shape=(jax.ShapeDtypeStruct((T, D), F32), jax.ShapeDtypeStruct((1, D), F32)), grid=(T // tm,),
        in_specs=[row(wd) for _, wd in pieces] + [_resident((D, IN_COLS)), row(D), row(D), pl.BlockSpec((1, D), lambda i: (0, 0))],
        out_specs=(row(D), pl.BlockSpec((1, D), lambda i: (0, 0))),
        name="inproj_bwd", sem=("arbitrary",), args=(*[a for a, _ in pieces], w_in, x, dx_in, g))


def _dw_pieces(lhs_t, pieces):
    nt, K, tk = lhs_t.shape
    tiles, nj = _piece_tiles(pieces)

    def kern(*refs):
        lhs_ref = refs[0]
        p_refs = refs[1:1 + len(tiles)]
        o_ref, ob_ref = refs[1 + len(tiles):]
        j, t = pl.program_id(0), pl.program_id(1)

        @pl.when(t == 0)
        def _():
            o_ref[...] = jnp.zeros_like(o_ref)

        for p_ref, (_, start, n) in zip(p_refs, tiles):
            @pl.when((j >= start) & (j < start + n))
            def _(p_ref=p_ref):
                o_ref[...] += _nn(lhs_ref[t], p_ref[...])

        @pl.when(t == nt - 1)
        def _():
            ob_ref[...] = o_ref[...].astype(BF16)

    def p_map(start, n):
        return lambda j, t: (jnp.where((j >= start) & (j < start + n), t, 0), jnp.clip(j - start, 0, n - 1))

    N = nj * COL_TILE
    return pl.pallas_call(
        kern, out_shape=(jax.ShapeDtypeStruct((K, N), F32), jax.ShapeDtypeStruct((K, N), BF16)), grid=(nj, nt),
        in_specs=[_resident((nt, K, tk))] + [pl.BlockSpec((tk, COL_TILE), p_map(s, n)) for _, s, n in tiles],
        out_specs=(pl.BlockSpec((K, COL_TILE), lambda j, t: (0, j)), pl.BlockSpec((K, COL_TILE), lambda j, t: (0, j))),
        name="dw_pieces", compiler_params=_params("arbitrary", "arbitrary"))(lhs_t, *[a for a, _, _ in tiles])


def _dw(lhs_t, rhs):
    lhs_res = lhs_t.ndim == 3
    W = D_MODEL
    if lhs_res:
        nt, K, tk = lhs_t.shape
    else:
        K, tk = lhs_t.shape[0], _dw_chunk(lhs_t.shape[1])
        nt = lhs_t.shape[1] // tk
    N = rhs.shape[-1]

    def kern(lhs_ref, rhs_ref, o_ref, ob_ref):
        t = pl.program_id(2)

        @pl.when(t == 0)
        def _():
            o_ref[...] = jnp.zeros_like(o_ref)

        o_ref[...] += _nn(lhs_ref[t] if lhs_res else lhs_ref[...], rhs_ref[...].astype(BF16))

        @pl.when(t == nt - 1)
        def _():
            ob_ref[...] = o_ref[...].astype(BF16)

    omap = lambda i, j, t: (i, j)
    lspec = _resident((nt, W, tk)) if lhs_res else pl.BlockSpec((W, tk), lambda i, j, t: (i, t))
    return pl.pallas_call(
        kern, out_shape=(jax.ShapeDtypeStruct((K, N), F32), jax.ShapeDtypeStruct((K, N), BF16)), grid=(K // W, N // W, nt),
        in_specs=[lspec, pl.BlockSpec((tk, W), lambda i, j, t: (t, j))],
        out_specs=(pl.BlockSpec((W, W), omap), pl.BlockSpec((W, W), omap)),
        name="dw", compiler_params=_params("arbitrary", "arbitrary", "arbitrary"))(lhs_t, rhs)


def _dw_relu2(dy_t, a):
    nt, D, tk = dy_t.shape
    F = a.shape[1]
    W = D_MODEL

    def kern(dyt_ref, a_ref, o_ref, ob_ref, acc_scr):
        t = pl.program_id(1)

        @pl.when(t == 0)
        def _():
            acc_scr[...] = jnp.zeros_like(acc_scr)

        u = jnp.square(jnp.maximum(a_ref[...].astype(F32), 0.0)).astype(BF16)
        acc_scr[...] += _nn(dyt_ref[t], u)

        @pl.when(t == nt - 1)
        def _():
            r = acc_scr[...].T
            o_ref[...] = r
            ob_ref[...] = r.astype(BF16)

    omap = lambda j, t: (j, 0)
    return pl.pallas_call(
        kern, out_shape=(jax.ShapeDtypeStruct((F, D), F32), jax.ShapeDtypeStruct((F, D), BF16)), grid=(F // W, nt),
        in_specs=[_resident((nt, D, tk)), pl.BlockSpec((tk, W), lambda j, t: (t, j))],
        out_specs=(pl.BlockSpec((W, D), omap), pl.BlockSpec((W, D), omap)),
        scratch_shapes=[pltpu.VMEM((D, W), F32)],
        name="dw_relu2", compiler_params=_params("arbitrary", "arbitrary"))(dy_t, a)


BIG = (("w_in", D_MODEL, IN_COLS, "col"), ("w_attn_out", D_MODEL, D_MODEL, "row"), ("w_conv_out", D_MODEL, D_MODEL, "row"),
       ("w_o", D_MODEL, D_MODEL, "row"), ("w_up", D_MODEL, D_FF, "col"), ("w_down", D_FF, D_MODEL, "row"))


def _shard_dims(rows, cols, kind):
    return (rows, cols // N_CHIP) if kind in ("col", "chip") else (rows // N_CHIP, cols)


def _window(ref, rows, cols, kind, chip, half):
    sr, sc = _shard_dims(rows, cols, kind)
    hr = sr // 2
    if kind == "col":
        return ref.at[pl.ds(half * hr, hr), pl.ds(chip * sc, sc)]
    if kind == "chip":
        return ref.at[chip, pl.ds(half * hr, hr), :]
    return ref.at[pl.ds(chip * sr + half * hr, hr), :]


def _mesh_pos():
    x, y, c = lax.axis_index("x"), lax.axis_index("y"), lax.axis_index("c")
    return x, y, c, 2 * x + y


_REL_BITS = (2, 1, 3)


def _rel_dev(x, y, c, r):
    return ((1 - x, y, c), (x, 1 - y, c), (1 - x, 1 - y, c))[r]


def _for_my_chip(j, fn):
    for js in range(N_CHIP):
        pl.when(j == js)(functools.partial(fn, js))


def _cast_into_full(j_arr, shard, l, rows, cols, kind):
    sr, sc = _shard_dims(rows, cols, kind)
    tr = min(256, sr)

    def kern(j_ref, s_ref, o_ref):
        o_ref[...] = s_ref[...].astype(BF16)

    shape, block = (rows, cols), (tr, sc)
    if kind == "col":
        omap = lambda i, j_ref: (i, j_ref[0])
    elif kind == "chip":
        shape, block = (N_CHIP, rows, sc), (None, tr, sc)
        omap = lambda i, j_ref: (j_ref[0], i, 0)
    else:
        omap = lambda i, j_ref: (j_ref[0] * (sr // tr) + i, 0)
    gs = pltpu.PrefetchScalarGridSpec(
        num_scalar_prefetch=1, grid=(sr // tr,),
        in_specs=[pl.BlockSpec((None, tr, sc), lambda i, j_ref: (l, i, 0))], out_specs=pl.BlockSpec(block, omap))
    return pl.pallas_call(kern, out_shape=jax.ShapeDtypeStruct(shape, BF16), grid_spec=gs, name="cast_into_full",
                          compiler_params=_params("arbitrary"))(j_arr, shard)


def _gather_comm(fulls, cw=None):
    n_big = len(fulls)
    n_piece = n_big + (0 if cw is None else 1)

    def pieces(in_refs, o_refs, js, c):
        def piece(p, chip, half):
            if p == n_big:
                return o_refs[p].at[half, chip]
            _, rows, cols, kind = fulls[p]
            return _window(o_refs[p], rows, cols, kind, chip, half)

        def mine(p):
            return in_refs[p].at[c] if p == n_big else piece(p, js, c)

        return piece, mine

    def local_copies(in_refs, piece, js, loc_sem):
        if cw is None:
            return []
        return [pltpu.make_async_copy(in_refs[n_big].at[half], piece(n_big, js, half), loc_sem.at[half]) for half in range(2)]

    def ici_copy(piece, mine, js, x, y, c, r, p, send_sem, recv_sem):
        return pltpu.make_async_remote_copy(mine(p), piece(p, js, c), send_sem.at[r * n_piece + p], recv_sem.at[r * n_piece + p],
                                            _rel_dev(x, y, c, r), MESH)

    def start(in_refs, o_refs, sems):
        send_sem, recv_sem, _, _, loc_sem = sems
        x, y, c, j = _mesh_pos()

        def run(js):
            piece, mine = pieces(in_refs, o_refs, js, c)
            for cp in local_copies(in_refs, piece, js, loc_sem):
                cp.start()
            for r in range(3):
                for p in range(n_piece):
                    ici_copy(piece, mine, js, x, y, c, r, p, send_sem, recv_sem).start()

        _for_my_chip(j, run)

    def finish(in_refs, o_refs, sems):
        send_sem, recv_sem, fsend_sem, frecv_sem, loc_sem = sems
        x, y, c, j = _mesh_pos()

        def run(js):
            piece, mine = pieces(in_refs, o_refs, js, c)
            fwds = []
            for r in range(3):
                ks = js ^ _REL_BITS[r]
                for p in range(n_piece):
                    got = piece(p, ks, c)
                    pltpu.make_async_remote_copy(got, got, send_sem.at[r * n_piece + p], recv_sem.at[r * n_piece + p],
                                                 _rel_dev(x, y, c, r), MESH).wait_recv()
                    cp = pltpu.make_async_remote_copy(got, got, fsend_sem.at[r * n_piece + p], frecv_sem.at[r * n_piece + p],
                                                      (x, y, 1 - c), MESH)
                    cp.start()
                    fwds.append(cp)
            for r in range(3):
                ks = js ^ _REL_BITS[r]
                for p in range(n_piece):
                    got = piece(p, ks, 1 - c)
                    pltpu.make_async_remote_copy(got, got, fsend_sem.at[r * n_piece + p], frecv_sem.at[r * n_piece + p],
                                                 (x, y, 1 - c), MESH).wait_recv()
            for r in range(3):
                for p in range(n_piece):
                    ici_copy(piece, mine, js, x, y, c, r, p, send_sem, recv_sem).wait_send()
            for cp in fwds:
                cp.wait_send()
            for cp in local_copies(in_refs, piece, js, loc_sem):
                cp.wait()

        _for_my_chip(j, run)

    out_shape = [jax.ShapeDtypeStruct(a.shape, BF16) for a, _, _, _ in fulls]
    ins = [a for a, _, _, _ in fulls]
    if cw is not None:
        out_shape.append(jax.ShapeDtypeStruct((DEPTH, N_CHIP, 3, D_MODEL // N_CHIP), F32))
        ins.append(cw)
    scratch = [pltpu.SemaphoreType.DMA((3 * n_piece,))] * 4 + [pltpu.SemaphoreType.DMA((2,))]
    return _Hosted(ins, out_shape, {p: p for p in range(n_big)}, scratch, start, finish)


def _compose(*comms):
    comms = [cm for cm in comms if cm is not None]
    if len(comms) <= 1:
        return comms[0] if comms else None
    ins, outs, aliases, scratch, cuts = [], [], {}, [], []
    for cm in comms:
        cuts.append((len(ins), len(outs), len(scratch)))
        aliases.update({len(ins) + i: len(outs) + o for i, o in cm.aliases.items()})
        ins, outs, scratch = ins + cm.inputs, outs + cm.out_shape, scratch + cm.scratch

    def parts(a, b, s):
        for cm, (i0, o0, s0) in zip(comms, cuts):
            yield cm, a[i0:i0 + len(cm.inputs)], b[o0:o0 + len(cm.out_shape)], s[s0:s0 + len(cm.scratch)]

    def start(a, b, s):
        for cm, pa, pb, ps in parts(a, b, s):
            cm.start(pa, pb, ps)

    def finish(a, b, s):
        for cm, pa, pb, ps in parts(a, b, s):
            cm.finish(pa, pb, ps)

    return _Hosted(ins, outs, aliases, scratch, start, finish)


def _run_comm(comm, name):
    n_in, n_out = len(comm.inputs), len(comm.out_shape)

    def body(*refs):
        comm.start(refs[:n_in], refs[n_in:n_in + n_out], refs[n_in + n_out:])
        comm.finish(refs[:n_in], refs[n_in:n_in + n_out], refs[n_in + n_out:])

    return pl.pallas_call(body, out_shape=comm.out_shape, in_specs=[ANY] * n_in, out_specs=[ANY] * n_out,
                          input_output_aliases=comm.aliases, scratch_shapes=comm.scratch, name=name)(*comm.inputs)


def _sibling_exchange_comm(gb):
    n = len(gb)

    def copies(g_refs, o_refs, sems, c):
        send_sem, recv_sem = sems
        x, y, _, _ = _mesh_pos()
        return [pltpu.make_async_remote_copy(_window(g_refs[t], rows, cols, kind, chip, 1 - c),
                                             _window(o_refs[t], rows, cols, kind, chip, 1 - c),
                                             send_sem.at[N_CHIP * t + chip], recv_sem.at[N_CHIP * t + chip], (x, y, 1 - c), MESH)
                for t, (_, rows, cols, kind) in enumerate(gb) for chip in range(N_CHIP)]

    def start(g_refs, o_refs, sems):
        for cp in copies(g_refs, o_refs, sems, lax.axis_index("c")):
            cp.start()

    def finish(g_refs, o_refs, sems):
        c = lax.axis_index("c")
        for cp in copies(g_refs, o_refs, sems, 1 - c):
            cp.wait_recv()
        for cp in copies(g_refs, o_refs, sems, c):
            cp.wait_send()

    return _Hosted([a for a, _, _, _ in gb], [jax.ShapeDtypeStruct(a.shape, BF16) for a, _, _, _ in gb], {},
                   [pltpu.SemaphoreType.DMA((N_CHIP * n,))] * 2, start, finish)


def _half_add(jc_arr, g, sib, rows, cols, kind):
    sr, sc = _shard_dims(rows, cols, kind)
    hr = sr // 2

    def kern(jc_ref, g_ref, s_ref, ob_ref, of_ref):
        v = g_ref[...] + s_ref[...].astype(F32)
        ob_ref[...] = v.astype(BF16)

        @pl.when(pl.program_id(0) == jc_ref[0])
        def _():
            of_ref[...] = v

    if kind == "col":
        imap = lambda j, jc_ref: (jc_ref[1], j)
    else:
        imap = lambda j, jc_ref: (2 * j + jc_ref[1], 0)
    gs = pltpu.PrefetchScalarGridSpec(
        num_scalar_prefetch=1, grid=(N_CHIP,),
        in_specs=[pl.BlockSpec((hr, sc), imap), pl.BlockSpec((hr, sc), imap)],
        out_specs=[pl.BlockSpec((None, hr, sc), lambda j, jc_ref: (j, 0, 0)), pl.BlockSpec((hr, sc), lambda j, jc_ref: (0, 0))])
    return pl.pallas_call(
        kern, out_shape=(jax.ShapeDtypeStruct((N_CHIP, hr, sc), BF16), jax.ShapeDtypeStruct((hr, sc), F32)),
        grid_spec=gs, name="grad_half_add", compiler_params=_params("arbitrary"))(jc_arr, g, sib)


def _chip_exchange_comm(sbs):
    n = len(sbs)

    def copies(s_refs, o_refs, sems):
        send_sem, recv_sem = sems
        x, y, c, j = _mesh_pos()
        return [pltpu.make_async_remote_copy(s_refs[t].at[j ^ _REL_BITS[r]], o_refs[t].at[r], send_sem.at[n * r + t],
                                             recv_sem.at[n * r + t], _rel_dev(x, y, c, r), MESH)
                for r in range(3) for t in range(n)]

    def start(s_refs, o_refs, sems):
        for cp in copies(s_refs, o_refs, sems):
            cp.start()

    def finish(s_refs, o_refs, sems):
        for cp in copies(s_refs, o_refs, sems):
            cp.wait()

    return _Hosted(sbs, [jax.ShapeDtypeStruct((3,) + a.shape[1:], BF16) for a in sbs], {},
                   [pltpu.SemaphoreType.DMA((3 * n,))] * 2, start, finish)


def _owner_sum(jc_arr, sf, rb, l, into=None):
    hr, sc = sf.shape

    def kern(jc_ref, s_ref, r0_ref, r1_ref, r2_ref, *rest):
        o_ref = rest[-1]
        o_ref[...] = ((s_ref[...] + r0_ref[...].astype(F32)) + r1_ref[...].astype(F32)) + r2_ref[...].astype(F32)

    in_specs = [pl.BlockSpec((hr, sc), lambda i, jc_ref: (0, 0))]
    in_specs += [pl.BlockSpec((None, hr, sc), lambda i, jc_ref, r=r: (r, 0, 0)) for r in range(3)]
    args = [jc_arr, sf, rb, rb, rb]
    aliases = {}
    if into is not None:
        in_specs.append(ANY)
        args.append(into)
        aliases = {len(args) - 1: 0}
    gs = pltpu.PrefetchScalarGridSpec(
        num_scalar_prefetch=1, grid=(1,), in_specs=in_specs,
        out_specs=pl.BlockSpec((None, hr, sc), lambda i, jc_ref: (l, jc_ref[1], 0)))
    return pl.pallas_call(kern, out_shape=jax.ShapeDtypeStruct((DEPTH, 2 * hr, sc), F32), grid_spec=gs,
                          input_output_aliases=aliases, name="grad_owner_sum", compiler_params=_params("arbitrary"))(*args)


def _sibling_assemble_comm(grads, layers):
    n = len(grads)
    todo = [(q, l) for q in range(n) for l in layers[q]]

    def copies(o_refs, sems, half):
        send_sem, recv_sem = sems
        x, y, c, _ = _mesh_pos()
        out = []
        for k, (q, l) in enumerate(todo):
            hr = grads[q].shape[1] // 2
            w = o_refs[q].at[l, pl.ds(half * hr, hr), :]
            out.append(pltpu.make_async_remote_copy(w, w, send_sem.at[k], recv_sem.at[k], (x, y, 1 - c), MESH))
        return out

    def start(_, o_refs, sems):
        for cp in copies(o_refs, sems, lax.axis_index("c")):
            cp.start()

    def finish(_, o_refs, sems):
        c = lax.axis_index("c")
        for cp in copies(o_refs, sems, 1 - c):
            cp.wait_recv()
        for cp in copies(o_refs, sems, c):
            cp.wait_send()

    return _Hosted(grads, [jax.ShapeDtypeStruct(g.shape, F32) for g in grads], {q: q for q in range(n)},
                   [pltpu.SemaphoreType.DMA((len(todo),))] * 2, start, finish)


def _adamw_math(w, g, m, v):
    m = ADAM_B1 * m + (1.0 - ADAM_B1) * g
    v = ADAM_B2 * v + (1.0 - ADAM_B2) * jnp.square(g)
    m_hat = m / (1.0 - ADAM_B1 ** ADAM_STEP)
    v_hat = v / (1.0 - ADAM_B2 ** ADAM_STEP)
    delta = -ADAM_LR * (m_hat / (jnp.sqrt(v_hat) + ADAM_EPS) + ADAM_WD * w)
    return delta, m, v


def _adamw(w, g, m, v):
    shape = w.shape
    C = shape[-1]
    R = int(np.prod(shape[:-1]))
    tr = min(256, R)
    args = [a.reshape(R, C) for a in (w, g, m, v)]

    def kern(w_ref, g_ref, m_ref, v_ref, go_ref, d_ref, nm_ref, nv_ref):
        g_val = g_ref[...]
        d, nm, nv = _adamw_math(w_ref[...], g_val, m_ref[...], v_ref[...])
        go_ref[...] = g_val
        d_ref[...] = d
        nm_ref[...] = nm
        nv_ref[...] = nv

    spec = pl.BlockSpec((tr, C), lambda i: (i, 0))
    outs = pl.pallas_call(
        kern, out_shape=[jax.ShapeDtypeStruct((R, C), F32)] * 4, grid=(R // tr,), in_specs=[spec] * 4, out_specs=[spec] * 4,
        name="adamw", compiler_params=_params("parallel"))(*args)
    return [o.reshape(shape) for o in outs]


_ROW_G_MIX, _ROW_B_GATES, _ROW_SINKS, _ROW_CONV, _ROW_G_MLP, _ROW_G_FINAL, _ROW_LOSS = 0, 2, 6, 8, 16, 18, 19


def _small_step(parts, params, moms, vels):
    names = ["g_mix", "b_gates", "sinks", "conv_w", "conv_b", "g_mlp", "g_final"]
    D = D_MODEL
    QW = D // N_CHIP
    n_dev = 8

    def body(*refs):
        it = iter(refs)
        dgmix = [next(it) for _ in range(DEPTH)]
        dbg = [next(it) for _ in range(DEPTH)]
        dsk = [next(it) for _ in range(DEPTH)]
        dwb = [next(it) for _ in range(DEPTH)]
        dgmlp = [next(it) for _ in range(DEPTH)]
        lst = next(it)
        p_refs = {n: next(it) for n in names}
        m_refs = {n: next(it) for n in names}
        v_refs = {n: next(it) for n in names}
        loss_ref = next(it)
        outs = {n: [next(it) for _ in range(4)] for n in names}
        pack_ref, all_ref, send_sem, recv_sem = next(it), next(it), next(it), next(it)

        x, y, c, j = _mesh_pos()
        me = 4 * x + 2 * y + c
        pack_ref[...] = jnp.zeros_like(pack_ref)
        for l in range(DEPTH):
            pack_ref[_ROW_G_MIX + l:_ROW_G_MIX + l + 1, :] = dgmix[l][...]
            pack_ref[_ROW_B_GATES + 2 * l:_ROW_B_GATES + 2 * l + 1, :] = dbg[l][:, 0:D]
            pack_ref[_ROW_B_GATES + 2 * l + 1:_ROW_B_GATES + 2 * l + 2, :] = dbg[l][:, D:2 * D]
            pack_ref[_ROW_SINKS + l:_ROW_SINKS + l + 1, 0:128] = dsk[l][0:1, :]
            pack_ref[_ROW_CONV + 4 * l:_ROW_CONV + 4 * l + 4, :] = dwb[l][0:4, :]
            pack_ref[_ROW_G_MLP + l:_ROW_G_MLP + l + 1, :] = dgmlp[l][...]
        pack_ref[_ROW_G_FINAL:_ROW_G_FINAL + 1, :] = lst[0:1, :]
        pack_ref[_ROW_LOSS:_ROW_LOSS + 1, :] = lst[1:2, :]

        all_ref[me] = pack_ref[...]
        cps = []
        for k in range(1, n_dev):
            dx_, dy_, dc_ = (k >> 2) & 1, (k >> 1) & 1, k & 1
            peer = (x ^ dx_, y ^ dy_, c ^ dc_)
            cp = pltpu.make_async_remote_copy(pack_ref, all_ref.at[me], send_sem.at[k - 1], recv_sem.at[k - 1], peer, MESH)
            cp.start()
            cps.append(cp)
        for cp in cps:
            cp.wait()

        tot = all_ref[0]
        for d in range(1, n_dev):
            tot = tot + all_ref[d]
        pack_ref[...] = tot

        loss_ref[...] = pack_ref[_ROW_LOSS:_ROW_LOSS + 1, 0:1]

        def finish(name, idx, g):
            w, m, v = p_refs[name][idx], m_refs[name][idx], v_refs[name][idx]
            d, nm, nv = _adamw_math(w, g, m, v)
            for ref, val in zip(outs[name], (g, d, nm, nv)):
                ref[idx] = val

        for l in range(DEPTH):
            finish("g_mix", (slice(l, l + 1), slice(None)), pack_ref[_ROW_G_MIX + l:_ROW_G_MIX + l + 1, :])
            finish("g_mlp", (slice(l, l + 1), slice(None)), pack_ref[_ROW_G_MLP + l:_ROW_G_MLP + l + 1, :])
            finish("conv_b", (slice(l, l + 1), slice(None)), pack_ref[_ROW_CONV + 4 * l + 3:_ROW_CONV + 4 * l + 4, :])
            finish("sinks", (slice(l, l + 1), slice(None)), pack_ref[_ROW_SINKS + l:_ROW_SINKS + l + 1, 0:N_Q_HEADS])
            for hf in range(2):
                finish("b_gates", (slice(l, l + 1), slice(hf * D, (hf + 1) * D)),
                       pack_ref[_ROW_B_GATES + 2 * l + hf:_ROW_B_GATES + 2 * l + hf + 1, :])
        finish("g_final", (slice(0, 1), slice(None)), pack_ref[_ROW_G_FINAL:_ROW_G_FINAL + 1, :])

        def conv_w_chip(js):
            for l in range(DEPTH):
                for k in range(3):
                    row = _ROW_CONV + 4 * l + k
                    finish("conv_w", (l, slice(k, k + 1), slice(None)), pack_ref[row:row + 1, js * QW:(js + 1) * QW])

        _for_my_chip(j, conv_w_chip)

    vm = pl.BlockSpec(memory_space=pltpu.VMEM)
    ins = (parts["g_mix"] + parts["b_gates"] + parts["sinks"] + parts["conv"] + parts["g_mlp"] + [parts["loss"]]
           + [params[n] for n in names] + [moms[n] for n in names] + [vels[n] for n in names])
    out_shape = [jax.ShapeDtypeStruct((1, 1), F32)]
    for n in names:
        out_shape += [jax.ShapeDtypeStruct(params[n].shape, F32)] * 4
    res = pl.pallas_call(
        body, out_shape=out_shape, in_specs=[vm] * len(ins), out_specs=[vm] * len(out_shape),
        scratch_shapes=[pltpu.VMEM((SMALL_ROWS, D), F32), pltpu.VMEM((n_dev, SMALL_ROWS, D), F32),
                        pltpu.SemaphoreType.DMA((n_dev - 1,)), pltpu.SemaphoreType.DMA((n_dev - 1,))],
        name="small_allreduce_adamw")(*ins)
    loss = res[0]
    out = {n: res[1 + 4 * i:5 + 4 * i] for i, n in enumerate(names)}
    return loss, out


def kernel(x, g_mix, w_in, b_gates, sinks, w_attn_out, conv_w, conv_b, w_conv_out, w_o, g_mlp, w_up, w_down, g_final, loss_target, m_g_mix, m_w_in, m_b_gates, m_sinks, m_w_attn_out, m_conv_w, m_conv_b, m_w_conv_out, m_w_o, m_g_mlp, m_w_up, m_w_down, m_g_final, v_g_mix, v_w_in, v_b_gates, v_sinks, v_w_attn_out, v_conv_w, v_conv_b, v_w_conv_out, v_w_o, v_g_mlp, v_w_up, v_w_down, v_g_final):
    B, S, D = x.shape
    T = B * S
    big_w = dict(w_in=w_in, w_attn_out=w_attn_out, w_conv_out=w_conv_out, w_o=w_o, w_up=w_up, w_down=w_down)
    big_m = dict(w_in=m_w_in, w_attn_out=m_w_attn_out, w_conv_out=m_w_conv_out, w_o=m_w_o, w_up=m_w_up, w_down=m_w_down)
    big_v = dict(w_in=v_w_in, w_attn_out=v_w_attn_out, w_conv_out=v_w_conv_out, w_o=v_w_o, w_up=v_w_up, w_down=v_w_down)

    c_arr = lax.axis_index("c").astype(jnp.int32).reshape(1)
    j_arr = (2 * lax.axis_index("x") + lax.axis_index("y")).astype(jnp.int32).reshape(1)
    jc_arr = jnp.concatenate([j_arr, c_arr])
    order = [(n, l) for n, _, _, _ in BIG for l in range(DEPTH)]
    dims = {n: (r, c_, k) for n, r, c_, k in BIG}

    wdims = dict(dims, w_up=(D, D_FF, "chip"))
    full = {(n, l): _cast_into_full(j_arr, big_w[n], l, *wdims[n]) for n, l in order}
    mixers = ("w_attn_out", "w_conv_out", "w_o")

    def gather_of(keys):
        return _gather_comm([(full[k],) + wdims[k[0]] for k in keys])

    carried = {("proj", 0): [(n, 0) for n in mixers], ("attn", 0): [("w_up", 0), ("w_down", 0)],
               ("mlp", 0): [("w_in", 1)] + [(n, 1) for n in mixers], ("attn", 1): [("w_up", 1), ("w_down", 1)]}

    def carry(fn, where, *args):
        keys = carried.get(where)
        if keys is None:
            return fn(*args)
        res, got = fn(*args, comm=gather_of(keys))
        full.update(zip(keys, got))
        return res

    first = _run_comm(_gather_comm([(full["w_in", 0],) + dims["w_in"]], conv_w), "gather_weights")
    full["w_in", 0] = first[0]
    conv_w_full = jnp.transpose(first[1], (0, 2, 1, 3)).reshape(DEPTH, 3, D)
    attn_bias = _attn_bias_table()

    xs = [x.reshape(T, D)]
    saved = []
    for l in range(DEPTH):
        ht, pqkv, pconv, pgate = carry(_norm_proj, ("proj", l), xs[-1], g_mix[l:l + 1], full["w_in", l])
        att, att_t, lse = carry(_attn_fwd, ("attn", l), pqkv, sinks[l], attn_bias, S)
        cv, cv_t = _conv_fwd(pconv, conv_w_full[l], conv_b[l:l + 1], S)
        x1, ya, yc, mg_t = _mix_fwd(xs[-1], att, cv, pgate, b_gates[l:l + 1], full["w_attn_out", l], full["w_conv_out", l],
                                    full["w_o", l])
        x2, a = carry(_mlp_fwd, ("mlp", l), x1, g_mlp[l:l + 1], full["w_up", l], full["w_down", l].reshape(N_CHIP, D, D))
        saved.append(dict(ht=ht, pqkv=pqkv, pconv=pconv, pgate=pgate, att=att, att_t=att_t, lse=lse, cv_t=cv_t, x1=x1, ya=ya,
                          yc=yc, mg_t=mg_t, a=a))
        xs.append(x2)

    loss_stats, dx = _loss_bwd(xs[-1], g_final.reshape(1, D), loss_target.reshape(T, D))

    parts = dict(g_mix=[None] * DEPTH, b_gates=[None] * DEPTH, sinks=[None] * DEPTH, conv=[None] * DEPTH,
                 g_mlp=[None] * DEPTH, loss=loss_stats)
    gf, gb, pre, got, mine = {}, {}, {}, {}, {}

    def sibling_exchange(keys):
        return _sibling_exchange_comm([(gb[k],) + dims[k[0]] for k in keys])

    def half_adds(keys, sib):
        for k, s in zip(keys, sib):
            pre[k] = _half_add(jc_arr, gf[k], s, *dims[k[0]])

    def chip_exchange(keys):
        return _chip_exchange_comm([pre[k][0] for k in keys])

    def owner_sums(keys):
        for n, l in keys:
            mine[n] = _owner_sum(jc_arr, pre[n, l][1], got[n, l], l, mine.get(n))

    def run(fn, comms, *args):
        if not comms:
            return fn(*args), []
        res, arrived = fn(*args, comm=_compose(*comms))
        outs, pos = [], 0
        for cm in comms:
            outs.append(arrived[pos:pos + len(cm.out_shape)])
            pos += len(cm.out_shape)
        return res, outs

    assert DEPTH == 2
    upper = [(n, 1) for n, _, _, _ in BIG]
    early = [(n, 0) for n in mixers + ("w_up", "w_down")]
    for l in reversed(range(DEPTH)):
        W = {n: full[(n, l)] for n in big_w}
        sv = saved[l]
        last = l == 0
        mlp_args = (dx, sv["x1"], sv["a"], g_mlp[l:l + 1], W["w_up"], W["w_down"].reshape(N_CHIP, D, D))
        (dx1, da, h2_t, dy_t, parts["g_mlp"][l]), arrived = run(_mlp_bwd, [chip_exchange(upper)] if last else [], *mlp_args)
        if last:
            got.update(zip(upper, arrived[0]))
        gf["w_up", l], gb["w_up", l] = _dw(h2_t, da)
        gf["w_down", l], gb["w_down", l] = _dw_relu2(dy_t, sv["a"])
        datt, dcv, dya, dyc, dgate, parts["b_gates"][l] = _mix_bwd(
            dx1, sv["ya"], sv["yc"], sv["pgate"], b_gates[l:l + 1], W["w_attn_out"], W["w_conv_out"], W["w_o"])
        gf["w_o", l], gb["w_o", l] = _dw(sv["mg_t"], dx1)
        gf["w_attn_out", l], gb["w_attn_out", l] = _dw(sv["att_t"], dya)
        gf["w_conv_out", l], gb["w_conv_out", l] = _dw(sv["cv_t"], dyc)
        conv_args = (dcv, sv["pconv"], conv_w_full[l], conv_b[l:l + 1], S)
        (dconv, parts["conv"][l]), arrived = run(_conv_bwd, [sibling_exchange(early)] if last else [], *conv_args)
        if last:
            half_adds(early, arrived[0])
        attn_args = (sv["pqkv"], datt, sv["lse"], sinks[l], attn_bias, S)
        (dq, dkv, parts["sinks"][l]), arrived = run(_attn_bwd, [chip_exchange(early)] if last else [], *attn_args)
        if last:
            got.update(zip(early, arrived[0]))
            owner_sums(upper + early)
        pieces = [(dq, D), (dkv, QKV_W - D), (dconv, CONV_W), (dgate, GATE_W)]
        gf["w_in", l], gb["w_in", l] = _dw_pieces(sv["ht"], pieces)
        in_args = (pieces, W["w_in"], xs[l], dx1, g_mix[l:l + 1])
        if not last:
            (dx, parts["g_mix"][l]), arrived = run(_inproj_bwd, [sibling_exchange(upper)], *in_args)
            half_adds(upper, arrived[0])
        else:
            tail = [("w_in", 0)]
            half_adds(tail, _run_comm(sibling_exchange(tail), "grad_sibling_exchange"))
            done = _sibling_assemble_comm(list(mine.values()), [(1,) if n == "w_in" else (0, 1) for n in mine])
            (dx, parts["g_mix"][l]), arrived = run(_inproj_bwd, [chip_exchange(tail), done], *in_args)
            got.update(zip(tail, arrived[0]))
            mine = dict(zip(mine, arrived[1]))
            owner_sums(tail)
            mine["w_in"] = _run_comm(_sibling_assemble_comm([mine["w_in"]], [(0,)]), "grad_sibling_assemble")[0]
    grads = mine

    res = {}
    for n in big_w:
        res[n] = tuple(_adamw(big_w[n], grads[n], big_m[n], big_v[n]))

    small_p = dict(g_mix=g_mix, b_gates=b_gates, sinks=sinks, conv_w=conv_w, conv_b=conv_b, g_mlp=g_mlp, g_final=g_final.reshape(1, D))
    small_m = dict(g_mix=m_g_mix, b_gates=m_b_gates, sinks=m_sinks, conv_w=m_conv_w, conv_b=m_conv_b, g_mlp=m_g_mlp,
                   g_final=m_g_final.reshape(1, D))
    small_v = dict(g_mix=v_g_mix, b_gates=v_b_gates, sinks=v_sinks, conv_w=v_conv_w, conv_b=v_conv_b, g_mlp=v_g_mlp,
                   g_final=v_g_final.reshape(1, D))
    loss, small = _small_step(parts, small_p, small_m, small_v)
    for n, vals in small.items():
        res[n] = tuple(v.reshape(D) for v in vals) if n == "g_final" else tuple(vals)

    weights = ["g_mix", "w_in", "b_gates", "sinks", "w_attn_out", "conv_w", "conv_b", "w_conv_out", "w_o", "g_mlp", "w_up",
               "w_down", "g_final"]
    out = [loss.reshape(()), dx.reshape(B, S, D)]
    for k in range(4):
        out += [res[n][k] for n in weights]
    return tuple(out)
```

```python
import functools

import numpy as np
import jax
import jax.numpy as jnp
from jax import lax
from jax.experimental import pallas as pl
from jax.experimental.pallas import tpu as pltpu

F32 = jnp.float32
BF16 = jnp.bfloat16

D_MODEL = 1024
HEAD_DIM = 64
N_Q_HEADS = 16
N_KV_HEADS = 4
GQA_GROUP = 4
WINDOW = 128
D_FF = 4096
DEPTH = 2
RMS_EPS = 1e-6
NEG_INF = -1e30
ATTN_SCALE = HEAD_DIM ** -0.5
QKV_W = 1536
CONV_W = 3072
GATE_W = 2048
IN_COLS = QKV_W + CONV_W + GATE_W
COL_TILE = 512
N_CHIP = 4
ADAM_LR = 0.001
ADAM_B1 = 0.9
ADAM_B2 = 0.999
ADAM_EPS = 1e-08
ADAM_WD = 0.01
ADAM_STEP = 10
V7X_VMEM_BYTES = 64 * 2 ** 20
VMEM_LIMIT = V7X_VMEM_BYTES - 8 * 2 ** 20
MESH = pl.DeviceIdType.MESH
ANY = pl.BlockSpec(memory_space=pl.ANY)
SMALL_ROWS = 24

_SLOPES = [float(v) for v in np.power(np.float32(2.0), -8.0 * np.arange(1, N_Q_HEADS + 1, dtype=np.float32) / N_Q_HEADS)]


def _params(*sem):
    return pltpu.CompilerParams(dimension_semantics=sem, vmem_limit_bytes=VMEM_LIMIT)


class _Hosted:
    def __init__(self, inputs, out_shape, aliases, scratch, start, finish):
        self.inputs, self.out_shape, self.aliases, self.scratch = list(inputs), list(out_shape), dict(aliases), list(scratch)
        self.start, self.finish = start, finish


def _hosted_call(comm, kern, *, out_shape, grid, in_specs, out_specs, args, name, sem, scratch_shapes=()):
    single = not isinstance(out_shape, (tuple, list))
    outs = [out_shape] if single else list(out_shape)
    ospecs = [out_specs] if single else list(out_specs)
    if comm is None:
        res = pl.pallas_call(kern, out_shape=outs, grid=grid, in_specs=list(in_specs), out_specs=ospecs,
                             scratch_shapes=list(scratch_shapes), name=name, compiler_params=_params(*sem))(*args)
        return res[0] if single else res
    n_in, n_out, n_scr = len(args), len(outs), len(scratch_shapes)
    ci, co, cs = len(comm.inputs), len(comm.out_shape), len(comm.scratch)

    def body(*refs):
        cuts = np.cumsum([0, n_in, ci, n_out, co, n_scr, cs])
        a, b, c, d, e, f = [refs[lo:hi] for lo, hi in zip(cuts[:-1], cuts[1:])]
        ids = [pl.program_id(k) for k in range(len(grid))]
        first = functools.reduce(jnp.logical_and, [i == 0 for i in ids])
        last = functools.reduce(jnp.logical_and, [i == n - 1 for i, n in zip(ids, grid)])
        pl.when(first)(lambda: comm.start(b, d, f))
        kern(*a, *c, *e)
        pl.when(last)(lambda: comm.finish(b, d, f))

    res = pl.pallas_call(
        body, out_shape=outs + comm.out_shape, grid=grid, in_specs=list(in_specs) + [ANY] * ci, out_specs=ospecs + [ANY] * co,
        scratch_shapes=list(scratch_shapes) + comm.scratch,
        input_output_aliases={n_in + i: n_out + o for i, o in comm.aliases.items()},
        name=name + "_carrier", compiler_params=_params(*(["arbitrary"] * len(grid))))(*args, *comm.inputs)
    main = res[:n_out]
    return (main[0] if single else main), res[n_out:]


def _nt(a, b):
    return lax.dot_general(a, b, (((1,), (1,)), ((), ())), preferred_element_type=F32)


def _tn(a, b):
    return lax.dot_general(a, b, (((0,), (0,)), ((), ())), preferred_element_type=F32)


def _nn(a, b):
    return jnp.dot(a, b, preferred_element_type=F32)


def _rms_stats(xf):
    r = lax.rsqrt(jnp.mean(xf * xf, axis=-1, keepdims=True) + RMS_EPS)
    return r, xf * r


def _rms_bwd(dh, xh, r, g):
    dxh = dh * g
    dx = r * (dxh - xh * jnp.mean(dxh * xh, axis=-1, keepdims=True))
    dg = jnp.sum(dh * xh, axis=0, keepdims=True)
    return dx, dg


def _dw_chunk(T):
    return min(2048, T)


def _row_halves(tm):
    return (slice(0, tm // 2), slice(tm // 2, tm))


def _resident(shape):
    return pl.BlockSpec(shape, lambda *_: (0,) * len(shape), pipeline_mode=pl.Buffered(1))


def _norm_proj(x, g, w, comm=None):
    T, D = x.shape
    tm = min(512, T)
    tk = _dw_chunk(T)
    per = tk // tm
    widths = (QKV_W, CONV_W, GATE_W)

    def kern(x_ref, g_ref, w_ref, ht_ref, *o_refs):
        _, xh = _rms_stats(x_ref[...])
        h = (xh * g_ref[...]).astype(BF16)
        ht_ref[...] = h.T
        off = 0
        for o_ref, wd in zip(o_refs, widths):
            o_ref[...] = _nn(h, w_ref[:, off:off + wd]).astype(BF16)
            off += wd

    row = lambda wd: pl.BlockSpec((tm, wd), lambda i: (i, 0))
    return _hosted_call(
        comm, kern,
        out_shape=[jax.ShapeDtypeStruct((T // tk, D, tk), BF16)] + [jax.ShapeDtypeStruct((T, wd), BF16) for wd in widths],
        grid=(T // tm,), in_specs=[row(D), pl.BlockSpec((1, D), lambda i: (0, 0)), _resident((D, IN_COLS))],
        out_specs=[pl.BlockSpec((None, D, tm), lambda i: (i // per, 0, i % per))] + [row(wd) for wd in widths],
        name="norm_proj", sem=("parallel",), args=(x, g, w))


GW = GQA_GROUP * WINDOW
BAND = 2 * WINDOW
KV_W = N_KV_HEADS * HEAD_DIM


def _attn_bias_table():
    jj = np.arange(BAND)[:, None]
    col = np.arange(GW)[None, :]
    dist = WINDOW + (col % WINDOW) - jj
    valid = (dist >= 0) & (dist < WINDOW)
    slopes = np.asarray(_SLOPES, np.float32).reshape(N_KV_HEADS, GQA_GROUP)
    tab = np.empty((2, N_KV_HEADS, BAND, GW), np.float32)
    for hk in range(N_KV_HEADS):
        bias = -slopes[hk][col // WINDOW] * dist.astype(np.float32)
        tab[0, hk] = np.where(valid, bias, np.float32(NEG_INF))
        tab[1, hk] = np.where(valid & (jj >= WINDOW), bias, np.float32(NEG_INF))
    return jnp.asarray(tab)


def _stack_heads(ref, hk):
    return jnp.concatenate(
        [ref[:, HEAD_DIM * (GQA_GROUP * hk + g): HEAD_DIM * (GQA_GROUP * hk + g + 1)] for g in range(GQA_GROUP)], axis=0)


def _kv_band(cur_ref, prev_ref, hk):
    k0 = N_Q_HEADS * HEAD_DIM
    sl = slice(HEAD_DIM * hk, HEAD_DIM * (hk + 1))
    ksl, vsl = slice(k0 + sl.start, k0 + sl.stop), slice(k0 + KV_W + sl.start, k0 + KV_W + sl.stop)
    k_band = jnp.concatenate([prev_ref[:, sl], cur_ref[:, ksl]], axis=0)
    v_band = jnp.concatenate([prev_ref[:, KV_W + sl.start:KV_W + sl.stop], cur_ref[:, vsl]], axis=0)
    return k_band, v_band


def _lane_row(vals):
    return jnp.concatenate([jnp.full((1, WINDOW), v, F32) for v in vals], axis=1)


def _attn_fwd(pqkv, sinks, bias, seq, comm=None):
    T = pqkv.shape[0]
    nblk = seq // WINDOW
    per = 2
    assert nblk % per == 0
    k0 = N_Q_HEADS * HEAD_DIM

    def kern(sink_ref, cur_ref, prev_ref, bias_ref, o_ref, ot_ref, lse_ref):
        i = pl.program_id(0)
        for sub in range(per):
            rows = slice(WINDOW * sub, WINDOW * (sub + 1))
            cur = cur_ref.at[rows, :]
            prev = prev_ref if sub == 0 else cur_ref.at[WINDOW * (sub - 1):WINDOW * sub, k0:k0 + 2 * KV_W]
            first_i = (((per * i) % nblk) == 0).astype(jnp.int32) if sub == 0 else 0
            bands = [_kv_band(cur, prev, hk) for hk in range(N_KV_HEADS)]
            sts = [_nt(bands[hk][0], _stack_heads(cur, hk) * ATTN_SCALE) + bias_ref[first_i, hk] for hk in range(N_KV_HEADS)]
            ps, scales = [], []
            for hk in range(N_KV_HEADS):
                heads = [GQA_GROUP * hk + g for g in range(GQA_GROUP)]
                sink = _lane_row([sink_ref[h] for h in heads])
                m = jnp.maximum(jnp.max(sts[hk], axis=0, keepdims=True), sink)
                p = jnp.exp(sts[hk] - m)
                den = jnp.sum(p, axis=0, keepdims=True) + jnp.exp(sink - m)
                lse = m + jnp.log(den)
                for g, h in enumerate(heads):
                    lse_ref[h:h + 1, rows] = lse[:, WINDOW * g:WINDOW * (g + 1)]
                ps.append(p.astype(BF16))
                scales.append(1.0 / den)
            for hk in range(N_KV_HEADS):
                ot = _tn(bands[hk][1], ps[hk]) * scales[hk]
                for g in range(GQA_GROUP):
                    h = GQA_GROUP * hk + g
                    ot_ref[HEAD_DIM * h:HEAD_DIM * (h + 1), rows] = ot[:, WINDOW * g:WINDOW * (g + 1)].astype(BF16)
        o_ref[...] = ot_ref[...].T

    tq = per * WINDOW
    return _hosted_call(
        comm, kern,
        out_shape=(jax.ShapeDtypeStruct((T, D_MODEL), BF16), jax.ShapeDtypeStruct((D_MODEL, T), BF16),
                   jax.ShapeDtypeStruct((N_Q_HEADS, T), F32)),
        grid=(T // tq,),
        in_specs=[pl.BlockSpec(memory_space=pltpu.SMEM),
                  pl.BlockSpec((tq, QKV_W), lambda i: (i, 0)),
                  pl.BlockSpec((WINDOW, 2 * KV_W), lambda i: (jnp.maximum(per * i - 1, 0), 2)),
                  _resident(bias.shape)],
        out_specs=(pl.BlockSpec((tq, D_MODEL), lambda i: (i, 0)), pl.BlockSpec((D_MODEL, tq), lambda i: (0, i)),
                   pl.BlockSpec((N_Q_HEADS, tq), lambda i: (0, i))),
        name="attn_fwd", sem=("parallel",), args=(sinks, pqkv, pqkv, bias))


def _pick_row(a, row):
    rid = lax.broadcasted_iota(jnp.int32, a.shape, 0)
    return jnp.sum(jnp.where(rid == row, a, 0.0), axis=0, keepdims=True)


def _conv_taps(yc, halo_yc, first_i):
    keep = (1 - first_i).astype(F32)
    p1 = _pick_row(halo_yc, 15) * keep
    p2 = _pick_row(halo_yc, 14) * keep
    rowid = lax.broadcasted_iota(jnp.int32, yc.shape, 0)
    s1 = jnp.where(rowid == 0, p1, pltpu.roll(yc, 1, 0))
    s2 = jnp.where(rowid == 0, p2, jnp.where(rowid == 1, p1, pltpu.roll(yc, 2, 0)))
    return s1, s2


def _conv_fwd(pconv, conv_w, conv_b, seq):
    T = pconv.shape[0]
    tm = min(512, seq)
    per_seq = seq // tm
    D = D_MODEL

    def kern(cur_ref, halo_ref, w_ref, b_ref, o_ref, ot_ref):
        i = pl.program_id(0)
        first_i = ((i % per_seq) == 0).astype(jnp.int32)
        cb = cur_ref[:, 0:D].astype(F32)
        yc = cur_ref[:, D:2 * D].astype(F32) * cur_ref[:, 2 * D:3 * D].astype(F32)
        halo_yc = halo_ref[:, D:2 * D].astype(F32) * halo_ref[:, 2 * D:3 * D].astype(F32)
        s1, s2 = _conv_taps(yc, halo_yc, first_i)
        z = w_ref[0:1, :] * s2 + w_ref[1:2, :] * s1 + w_ref[2:3, :] * yc
        cv = (cb * (z + b_ref[...])).astype(BF16)
        o_ref[...] = cv
        ot_ref[...] = cv.T

    return pl.pallas_call(
        kern, out_shape=(jax.ShapeDtypeStruct((T, D), BF16), jax.ShapeDtypeStruct((D, T), BF16)), grid=(T // tm,),
        in_specs=[pl.BlockSpec((tm, CONV_W), lambda i: (i, 0)),
                  pl.BlockSpec((16, CONV_W), lambda i: (jnp.maximum(i * (tm // 16) - 1, 0), 0)),
                  pl.BlockSpec((3, D), lambda i: (0, 0)), pl.BlockSpec((1, D), lambda i: (0, 0))],
        out_specs=(pl.BlockSpec((tm, D), lambda i: (i, 0)), pl.BlockSpec((D, tm), lambda i: (0, i))),
        name="conv_fwd", compiler_params=_params("parallel"))(pconv, pconv, conv_w, conv_b)


def _mix_fwd(x, att, cv, pgate, b_gates, wao, wco, wo):
    T, D = x.shape
    tm = min(512, T)

    halves = _row_halves(tm)

    def kern(x_ref, att_ref, cv_ref, pg_ref, bg_ref, wao_ref, wco_ref, wo_ref, x1_ref, ya_ref, yc_ref, mgt_ref):
        yas = [_nn(att_ref[rows, :], wao_ref[...]) for rows in halves]
        ycs = [_nn(cv_ref[rows, :], wco_ref[...]) for rows in halves]
        mgs = []
        for rows, ya, yc in zip(halves, yas, ycs):
            sa = jax.nn.sigmoid(pg_ref[rows, 0:D].astype(F32) + bg_ref[:, 0:D])
            sc = jax.nn.sigmoid(pg_ref[rows, D:2 * D].astype(F32) + bg_ref[:, D:2 * D])
            mg = (sa * ya + sc * yc).astype(BF16)
            ya_ref[rows, :] = ya.astype(BF16)
            yc_ref[rows, :] = yc.astype(BF16)
            mgt_ref[:, rows] = mg.T
            mgs.append(mg)
        for rows, mg in zip(halves, mgs):
            x1_ref[rows, :] = x_ref[rows, :] + _nn(mg, wo_ref[...])

    row = lambda w: pl.BlockSpec((tm, w), lambda i: (i, 0))
    full = lambda a, b: pl.BlockSpec((a, b), lambda i: (0, 0))
    bf = jax.ShapeDtypeStruct((T, D), BF16)
    return pl.pallas_call(
        kern, out_shape=(jax.ShapeDtypeStruct((T, D), F32), bf, bf, jax.ShapeDtypeStruct((D, T), BF16)), grid=(T // tm,),
        in_specs=[row(D), row(D), row(D), row(GATE_W), full(1, GATE_W)] + [_resident((D, D))] * 3,
        out_specs=(row(D), row(D), row(D), pl.BlockSpec((D, tm), lambda i: (0, i))),
        name="mix_fwd", compiler_params=_params("parallel"))(x, att, cv, pgate, b_gates, wao, wco, wo)


def _mlp_fwd(x1, g, wup, wdn, comm=None):
    T, D = x1.shape
    tm = min(1024, T)
    nj = D_FF // D

    def kern(x_ref, g_ref, wup_ref, wdn_ref, x2_ref, a_ref, h_scr, acc_scr):
        j = pl.program_id(1)

        @pl.when(j == 0)
        def _():
            xf = x_ref[...]
            _, xh = _rms_stats(xf)
            h_scr[...] = (xh * g_ref[...]).astype(BF16)
            acc_scr[...] = xf

        a = _nn(h_scr[...], wup_ref[j])
        a_ref[...] = a.astype(BF16)
        u = jnp.square(jnp.maximum(a, 0.0)).astype(BF16)
        acc_scr[...] += _nn(u, wdn_ref[j])

        @pl.when(j == nj - 1)
        def _():
            x2_ref[...] = acc_scr[...]

    return _hosted_call(
        comm, kern, out_shape=(jax.ShapeDtypeStruct((T, D), F32), jax.ShapeDtypeStruct((T, D_FF), BF16)), grid=(T // tm, nj),
        in_specs=[pl.BlockSpec((tm, D), lambda i, j: (i, 0)), pl.BlockSpec((1, D), lambda i, j: (0, 0)),
                  _resident((nj, D, D)), _resident((nj, D, D))],
        out_specs=(pl.BlockSpec((tm, D), lambda i, j: (i, 0)), pl.BlockSpec((tm, D), lambda i, j: (i, j))),
        scratch_shapes=[pltpu.VMEM((tm, D), BF16), pltpu.VMEM((tm, D), F32)],
        name="mlp_fwd", sem=("parallel", "arbitrary"), args=(x1, g, wup, wdn))


def _loss_bwd(x, g, tgt):
    T, D = x.shape
    tm = min(512, T)

    def kern(x_ref, g_ref, t_ref, st_ref, dx_ref):
        i = pl.program_id(0)

        @pl.when(i == 0)
        def _():
            st_ref[...] = jnp.zeros_like(st_ref)

        gg = g_ref[...]
        r, xh = _rms_stats(x_ref[...])
        e = xh * gg - t_ref[...]
        part = 0.5 * jnp.sum(jnp.mean(e * e, axis=-1, keepdims=True), axis=0, keepdims=True)
        dx, dg = _rms_bwd(e * (1.0 / D), xh, r, gg)
        dx_ref[...] = dx
        st_ref[0:1, :] += dg
        st_ref[1:2, 0:1] += part

    return pl.pallas_call(
        kern, out_shape=(jax.ShapeDtypeStruct((8, D), F32), jax.ShapeDtypeStruct((T, D), F32)), grid=(T // tm,),
        in_specs=[pl.BlockSpec((tm, D), lambda i: (i, 0)), pl.BlockSpec((1, D), lambda i: (0, 0)),
                  pl.BlockSpec((tm, D), lambda i: (i, 0))],
        out_specs=(pl.BlockSpec((8, D), lambda i: (0, 0)), pl.BlockSpec((tm, D), lambda i: (i, 0))),
        name="loss_bwd", compiler_params=_params("arbitrary"))(x, g, tgt)


def _mlp_bwd(dx2, x1, a, g, wup, wdn, comm=None):
    T, D = x1.shape
    tm = min(512, T)
    nj = D_FF // D
    tk = _dw_chunk(T)
    per = tk // tm

    def kern(dx2_ref, x1_ref, a_ref, g_ref, wup_ref, wdn_ref, dx1_ref, da_ref, ut_ref, h2t_ref, dyb_ref, dg_ref, acc_scr):
        i, j = pl.program_id(0), pl.program_id(1)

        @pl.when((i == 0) & (j == 0))
        def _():
            dg_ref[...] = jnp.zeros_like(dg_ref)

        @pl.when(j == 0)
        def _():
            dyb_ref[...] = dx2_ref[...].astype(BF16)
            acc_scr[...] = jnp.zeros_like(acc_scr)

        du = _nt(dyb_ref[...], wdn_ref[j])
        relu = jnp.maximum(a_ref[...].astype(F32), 0.0)
        da = (du * (2.0 * relu)).astype(BF16)
        da_ref[...] = da
        ut_ref[...] = jnp.square(relu).astype(BF16).T
        acc_scr[...] += _nt(da, wup_ref[j])

        @pl.when(j == nj - 1)
        def _():
            gg = g_ref[...]
            r, xh = _rms_stats(x1_ref[...])
            h2t_ref[...] = (xh * gg).astype(BF16).T
            dx, dg = _rms_bwd(acc_scr[...], xh, r, gg)
            dx1_ref[...] = dx2_ref[...] + dx
            dg_ref[...] += dg

    return _hosted_call(
        comm, kern,
        out_shape=(jax.ShapeDtypeStruct((T, D), F32), jax.ShapeDtypeStruct((T, D_FF), BF16),
                   jax.ShapeDtypeStruct((D_FF, T), BF16), jax.ShapeDtypeStruct((T // tk, D, tk), BF16),
                   jax.ShapeDtypeStruct((T, D), BF16), jax.ShapeDtypeStruct((1, D), F32)),
        grid=(T // tm, nj),
        in_specs=[pl.BlockSpec((tm, D), lambda i, j: (i, 0)), pl.BlockSpec((tm, D), lambda i, j: (i, 0)),
                  pl.BlockSpec((tm, D), lambda i, j: (i, j)), pl.BlockSpec((1, D), lambda i, j: (0, 0)),
                  _resident((nj, D, D)), _resident((nj, D, D))],
        out_specs=(pl.BlockSpec((tm, D), lambda i, j: (i, 0)), pl.BlockSpec((tm, D), lambda i, j: (i, j)),
                   pl.BlockSpec((D, tm), lambda i, j: (j, i)), pl.BlockSpec((None, D, tm), lambda i, j: (i // per, 0, i % per)),
                   pl.BlockSpec((tm, D), lambda i, j: (i, 0)), pl.BlockSpec((1, D), lambda i, j: (0, 0))),
        scratch_shapes=[pltpu.VMEM((tm, D), F32)],
        name="mlp_bwd", sem=("arbitrary", "arbitrary"), args=(dx2, x1, a, g, wup, wdn))


def _mix_bwd(dx1, ya, yc, pgate, b_gates, wao, wco, wo):
    T, D = dx1.shape
    tm = min(512, T)
    halves = _row_halves(tm)

    def kern(dx_ref, ya_ref, yc_ref, pg_ref, bg_ref, wao_ref, wco_ref, wo_ref,
             datt_ref, dcv_ref, dya_ref, dyc_ref, dgt_ref, dbg_ref):
        @pl.when(pl.program_id(0) == 0)
        def _():
            dbg_ref[...] = jnp.zeros_like(dbg_ref)

        dms = [_nt(dx_ref[rows, :].astype(BF16), wo_ref[...]) for rows in halves]
        dys = []
        for rows, dm in zip(halves, dms):
            sa = jax.nn.sigmoid(pg_ref[rows, 0:D].astype(F32) + bg_ref[:, 0:D])
            sc = jax.nn.sigmoid(pg_ref[rows, D:2 * D].astype(F32) + bg_ref[:, D:2 * D])
            dya = (dm * sa).astype(BF16)
            dyc = (dm * sc).astype(BF16)
            dga = dm * ya_ref[rows, :].astype(F32) * (sa * (1.0 - sa))
            dgc = dm * yc_ref[rows, :].astype(F32) * (sc * (1.0 - sc))
            dya_ref[rows, :] = dya
            dyc_ref[rows, :] = dyc
            dgt_ref[rows, 0:D] = dga.astype(BF16)
            dgt_ref[rows, D:2 * D] = dgc.astype(BF16)
            dbg_ref[:, 0:D] += jnp.sum(dga, axis=0, keepdims=True)
            dbg_ref[:, D:2 * D] += jnp.sum(dgc, axis=0, keepdims=True)
            dys.append((dya, dyc))
        for rows, (dya, dyc) in zip(halves, dys):
            datt_ref[rows, :] = _nt(dya, wao_ref[...]).astype(BF16)
            dcv_ref[rows, :] = _nt(dyc, wco_ref[...]).astype(BF16)

    row = lambda w: pl.BlockSpec((tm, w), lambda i: (i, 0))
    full = lambda a, b: pl.BlockSpec((a, b), lambda i: (0, 0))
    bf = jax.ShapeDtypeStruct((T, D), BF16)
    return pl.pallas_call(
        kern,
        out_shape=(bf, bf, bf, bf, jax.ShapeDtypeStruct((T, GATE_W), BF16), jax.ShapeDtypeStruct((1, GATE_W), F32)),
        grid=(T // tm,),
        in_specs=[row(D), row(D), row(D), row(GATE_W), full(1, GATE_W)] + [_resident((D, D))] * 3,
        out_specs=(row(D), row(D), row(D), row(D), row(GATE_W), full(1, GATE_W)),
        name="mix_bwd", compiler_params=_params("arbitrary"))(dx1, ya, yc, pgate, b_gates, wao, wco, wo)


def _conv_bwd(dcv, pconv, conv_w, conv_b, seq, comm=None):
    T = pconv.shape[0]
    tm = min(512, seq)
    per_seq = seq // tm
    D = D_MODEL
    nb16 = T // 16

    def kern(dcv_ref, dcvn_ref, cur_ref, prev_ref, next_ref, w_ref, b_ref, o_ref, dwb_ref):
        i = pl.program_id(0)

        @pl.when(i == 0)
        def _():
            dwb_ref[...] = jnp.zeros_like(dwb_ref)

        first_i = ((i % per_seq) == 0).astype(jnp.int32)
        keep_next = 1.0 - (((i + 1) % per_seq) == 0).astype(F32)
        cb = cur_ref[:, 0:D].astype(F32)
        cc = cur_ref[:, D:2 * D].astype(F32)
        cu = cur_ref[:, 2 * D:3 * D].astype(F32)
        yc = cc * cu
        halo_yc = prev_ref[:, D:2 * D].astype(F32) * prev_ref[:, 2 * D:3 * D].astype(F32)
        s1, s2 = _conv_taps(yc, halo_yc, first_i)
        w0, w1, w2 = w_ref[0:1, :], w_ref[1:2, :], w_ref[2:3, :]
        z = w0 * s2 + w1 * s1 + w2 * yc
        dcv = dcv_ref[...].astype(F32)
        dz = dcv * cb
        dzn = dcvn_ref[...].astype(F32) * next_ref[:, 0:D].astype(F32) * keep_next
        n1, n2 = _pick_row(dzn, 0), _pick_row(dzn, 1)
        rowid = lax.broadcasted_iota(jnp.int32, dz.shape, 0)
        u1 = jnp.where(rowid == tm - 1, n1, pltpu.roll(dz, tm - 1, 0))
        u2 = jnp.where(rowid == tm - 1, n2, jnp.where(rowid == tm - 2, n1, pltpu.roll(dz, tm - 2, 0)))
        dyc = w2 * dz + w1 * u1 + w0 * u2
        o_ref[:, 0:D] = (dcv * (z + b_ref[...])).astype(BF16)
        o_ref[:, D:2 * D] = (dyc * cu).astype(BF16)
        o_ref[:, 2 * D:3 * D] = (dyc * cc).astype(BF16)
        dwb_ref[0:1, :] += jnp.sum(dz * s2, axis=0, keepdims=True)
        dwb_ref[1:2, :] += jnp.sum(dz * s1, axis=0, keepdims=True)
        dwb_ref[2:3, :] += jnp.sum(dz * yc, axis=0, keepdims=True)
        dwb_ref[3:4, :] += jnp.sum(dz, axis=0, keepdims=True)

    prev_map = lambda i: (jnp.maximum(i * (tm // 16) - 1, 0), 0)
    next_map = lambda i: (jnp.minimum((i + 1) * (tm // 16), nb16 - 1), 0)
    return _hosted_call(
        comm, kern, out_shape=(jax.ShapeDtypeStruct((T, CONV_W), BF16), jax.ShapeDtypeStruct((8, D), F32)), grid=(T // tm,),
        in_specs=[pl.BlockSpec((tm, D), lambda i: (i, 0)), pl.BlockSpec((16, D), next_map),
                  pl.BlockSpec((tm, CONV_W), lambda i: (i, 0)), pl.BlockSpec((16, CONV_W), prev_map),
                  pl.BlockSpec((16, CONV_W), next_map),
                  pl.BlockSpec((3, D), lambda i: (0, 0)), pl.BlockSpec((1, D), lambda i: (0, 0))],
        out_specs=(pl.BlockSpec((tm, CONV_W), lambda i: (i, 0)), pl.BlockSpec((8, D), lambda i: (0, 0))),
        name="conv_bwd", sem=("arbitrary",), args=(dcv, dcv, pconv, pconv, pconv, conv_w, conv_b))


def _attn_bwd(pqkv, datt, lse, sinks, bias, seq, comm=None):
    T = pqkv.shape[0]
    nblk = seq // WINDOW
    nseq = T // seq
    KVW = KV_W

    def kern(sink_ref, cur_ref, prev_ref, do_ref, lse_ref, bias_ref, dq_ref, dkv_ref, ds_ref, kc_scr, vc_scr, dqt_scr):
        b, st = pl.program_id(0), pl.program_id(1)

        @pl.when((b == 0) & (st == 0))
        def _():
            ds_ref[...] = jnp.zeros_like(ds_ref)

        @pl.when(st == 0)
        def _():
            kc_scr[...] = jnp.zeros_like(kc_scr)
            vc_scr[...] = jnp.zeros_like(vc_scr)

        @pl.when(st < nblk)
        def _():
            first_i = (st == 0).astype(jnp.int32)
            groups = range(N_KV_HEADS)
            bands = [_kv_band(cur_ref, prev_ref, hk) for hk in groups]
            qs = [_stack_heads(cur_ref, hk) for hk in groups]
            dos = [_stack_heads(do_ref, hk) for hk in groups]
            sts = [_nt(bands[hk][0], qs[hk] * ATTN_SCALE) + bias_ref[first_i, hk] for hk in groups]
            dps = [_nt(bands[hk][1], dos[hk]) for hk in groups]
            pbs, dsss = [], []
            for hk in groups:
                heads = [GQA_GROUP * hk + g for g in range(GQA_GROUP)]
                sink = _lane_row([sink_ref[h] for h in heads])
                lse_g = jnp.concatenate([lse_ref[h:h + 1, :] for h in heads], axis=1)
                p = jnp.exp(sts[hk] - lse_g)
                d_row = jnp.sum(p * dps[hk], axis=0, keepdims=True)
                dsss.append((p * (dps[hk] - d_row) * ATTN_SCALE).astype(BF16))
                pbs.append(p.astype(BF16))
                psd = jnp.exp(sink - lse_g) * d_row
                for g, h in enumerate(heads):
                    ds_ref[0:1, h:h + 1] -= jnp.sum(psd[:, WINDOW * g:WINDOW * (g + 1)], axis=1, keepdims=True)
            for hk in groups:
                dqt = _tn(bands[hk][0], dsss[hk])
                dk_b = _nn(dsss[hk], qs[hk])
                dv_b = _nn(pbs[hk], dos[hk])
                for g in range(GQA_GROUP):
                    h = GQA_GROUP * hk + g
                    dqt_scr[HEAD_DIM * h:HEAD_DIM * (h + 1), :] = dqt[:, WINDOW * g:WINDOW * (g + 1)].astype(BF16)
                ksl = slice(HEAD_DIM * hk, HEAD_DIM * (hk + 1))
                vsl = slice(KVW + HEAD_DIM * hk, KVW + HEAD_DIM * (hk + 1))
                dkv_ref[:, ksl] = (kc_scr[:, ksl] + dk_b[0:WINDOW]).astype(BF16)
                dkv_ref[:, vsl] = (vc_scr[:, ksl] + dv_b[0:WINDOW]).astype(BF16)
                kc_scr[:, ksl] = dk_b[WINDOW:2 * WINDOW]
                vc_scr[:, ksl] = dv_b[WINDOW:2 * WINDOW]
            dq_ref[...] = dqt_scr[...].T

        @pl.when(st == nblk)
        def _():
            dkv_ref[:, 0:KVW] = kc_scr[...].astype(BF16)
            dkv_ref[:, KVW:2 * KVW] = vc_scr[...].astype(BF16)

    cur_map = lambda b, s: (b * nblk + jnp.minimum(s, nblk - 1), 0)
    prev_row = lambda b, s: b * nblk + jnp.clip(s - 1, 0, nblk - 1)
    return _hosted_call(
        comm, kern,
        out_shape=(jax.ShapeDtypeStruct((T, D_MODEL), BF16), jax.ShapeDtypeStruct((T, 2 * KVW), BF16),
                   jax.ShapeDtypeStruct((8, 128), F32)),
        grid=(nseq, nblk + 1),
        in_specs=[pl.BlockSpec(memory_space=pltpu.SMEM),
                  pl.BlockSpec((WINDOW, QKV_W), cur_map),
                  pl.BlockSpec((WINDOW, 2 * KVW), lambda b, s: (prev_row(b, s), 2)),
                  pl.BlockSpec((WINDOW, D_MODEL), cur_map),
                  pl.BlockSpec((N_Q_HEADS, WINDOW), lambda b, s: (0, b * nblk + jnp.minimum(s, nblk - 1))),
                  _resident(bias.shape)],
        out_specs=(pl.BlockSpec((WINDOW, D_MODEL), cur_map),
                   pl.BlockSpec((WINDOW, 2 * KVW), lambda b, s: (prev_row(b, s), 0)),
                   pl.BlockSpec((8, 128), lambda b, s: (0, 0))),
        scratch_shapes=[pltpu.VMEM((WINDOW, KVW), F32), pltpu.VMEM((WINDOW, KVW), F32), pltpu.VMEM((D_MODEL, WINDOW), BF16)],
        name="attn_bwd", sem=("arbitrary", "arbitrary"), args=(sinks, pqkv, pqkv, datt, lse, bias))


def _piece_tiles(pieces):
    out, start = [], 0
    for arr, width in pieces:
        out.append((arr, start, width // COL_TILE))
        start += width // COL_TILE
    return out, start


def _inproj_bwd(pieces, w_in, x, dx_in, g, comm=None):
    T, D = x.shape
    tm = min(512, T)

    def kern(*refs):
        p_refs = refs[:len(pieces)]
        w_ref, x_ref, dxin_ref, g_ref, dx_ref, dg_ref = refs[len(pieces):]

        @pl.when(pl.program_id(0) == 0)
        def _():
            dg_ref[...] = jnp.zeros_like(dg_ref)

        dh, off = None, 0
        for p_ref, (_, width) in zip(p_refs, pieces):
            part = _nt(p_ref[...], w_ref[:, off:off + width])
            dh = part if dh is None else dh + part
            off += width
        gg = g_ref[...]
        r, xh = _rms_stats(x_ref[...])
        dx, dg = _rms_bwd(dh, xh, r, gg)
        dx_ref[...] = dxin_ref[...] + dx
        dg_ref[...] += dg

    row = lambda wd: pl.BlockSpec((tm, wd), lambda i: (i, 0))
    return _hosted_call(
        comm, kern, out_shape=(jax.ShapeDtypeStruct((T, D), F32), jax.ShapeDtypeStruct((1, D), F32)), grid=(T // tm,),
        in_specs=[row(wd) for _, wd in pieces] + [_resident((D, IN_COLS)), row(D), row(D), pl.BlockSpec((1, D), lambda i: (0, 0))],
        out_specs=(row(D), pl.BlockSpec((1, D), lambda i: (0, 0))),
        name="inproj_bwd", sem=("arbitrary",), args=(*[a for a, _ in pieces], w_in, x, dx_in, g))


def _dw_pieces(lhs_t, pieces):
    nt, K, tk = lhs_t.shape
    tiles, nj = _piece_tiles(pieces)

    def kern(*refs):
        lhs_ref = refs[0]
        p_refs = refs[1:1 + len(tiles)]
        o_ref, ob_ref = refs[1 + len(tiles):]
        j, t = pl.program_id(0), pl.program_id(1)

        @pl.when(t == 0)
        def _():
            o_ref[...] = jnp.zeros_like(o_ref)

        for p_ref, (_, start, n) in zip(p_refs, tiles):
            @pl.when((j >= start) & (j < start + n))
            def _(p_ref=p_ref):
                o_ref[...] += _nn(lhs_ref[t], p_ref[...])

        @pl.when(t == nt - 1)
        def _():
            ob_ref[...] = o_ref[...].astype(BF16)

    def p_map(start, n):
        return lambda j, t: (jnp.where((j >= start) & (j < start + n), t, 0), jnp.clip(j - start, 0, n - 1))

    N = nj * COL_TILE
    return pl.pallas_call(
        kern, out_shape=(jax.ShapeDtypeStruct((K, N), F32), jax.ShapeDtypeStruct((K, N), BF16)), grid=(nj, nt),
        in_specs=[_resident((nt, K, tk))] + [pl.BlockSpec((tk, COL_TILE), p_map(s, n)) for _, s, n in tiles],
        out_specs=(pl.BlockSpec((K, COL_TILE), lambda j, t: (0, j)), pl.BlockSpec((K, COL_TILE), lambda j, t: (0, j))),
        name="dw_pieces", compiler_params=_params("arbitrary", "arbitrary"))(lhs_t, *[a for a, _, _ in tiles])


def _dw(lhs_t, rhs):
    lhs_res, rhs_res = lhs_t.ndim == 3, rhs.ndim == 3
    W = D_MODEL
    if lhs_res:
        nt, K, tk = lhs_t.shape
    else:
        K, tk = lhs_t.shape[0], _dw_chunk(lhs_t.shape[1])
        nt = lhs_t.shape[1] // tk
    N = rhs.shape[-1]

    def kern(lhs_ref, rhs_ref, o_ref, ob_ref):
        t = pl.program_id(2)

        @pl.when(t == 0)
        def _():
            o_ref[...] = jnp.zeros_like(o_ref)

        a = lhs_ref[t] if lhs_res else lhs_ref[...]
        b = rhs_ref[t] if rhs_res else rhs_ref[...]
        o_ref[...] += _nn(a, b.astype(BF16))

        @pl.when(t == nt - 1)
        def _():
            ob_ref[...] = o_ref[...].astype(BF16)

    omap = lambda i, j, t: (i, j)
    lspec = _resident((nt, W, tk)) if lhs_res else pl.BlockSpec((W, tk), lambda i, j, t: (i, t))
    rspec = _resident((nt, tk, W)) if rhs_res else pl.BlockSpec((tk, W), lambda i, j, t: (t, j))
    return pl.pallas_call(
        kern, out_shape=(jax.ShapeDtypeStruct((K, N), F32), jax.ShapeDtypeStruct((K, N), BF16)), grid=(K // W, N // W, nt),
        in_specs=[lspec, rspec], out_specs=(pl.BlockSpec((W, W), omap), pl.BlockSpec((W, W), omap)),
        name="dw", compiler_params=_params("arbitrary", "arbitrary", "arbitrary"))(lhs_t, rhs)


BIG = (("w_in", D_MODEL, IN_COLS, "col"), ("w_attn_out", D_MODEL, D_MODEL, "row"), ("w_conv_out", D_MODEL, D_MODEL, "row"),
       ("w_o", D_MODEL, D_MODEL, "row"), ("w_up", D_MODEL, D_FF, "col"), ("w_down", D_FF, D_MODEL, "row"))


def _shard_dims(rows, cols, kind):
    return (rows, cols // N_CHIP) if kind in ("col", "chip") else (rows // N_CHIP, cols)


def _window(ref, rows, cols, kind, chip, half):
    sr, sc = _shard_dims(rows, cols, kind)
    hr = sr // 2
    if kind == "col":
        return ref.at[pl.ds(half * hr, hr), pl.ds(chip * sc, sc)]
    if kind == "chip":
        return ref.at[chip, pl.ds(half * hr, hr), :]
    return ref.at[pl.ds(chip * sr + half * hr, hr), :]


def _mesh_pos():
    x, y, c = lax.axis_index("x"), lax.axis_index("y"), lax.axis_index("c")
    return x, y, c, 2 * x + y


_REL_BITS = (2, 1, 3)


def _rel_dev(x, y, c, r):
    return ((1 - x, y, c), (x, 1 - y, c), (1 - x, 1 - y, c))[r]


def _for_my_chip(j, fn):
    for js in range(N_CHIP):
        pl.when(j == js)(functools.partial(fn, js))


def _cast_into_full(j_arr, shard, l, rows, cols, kind):
    sr, sc = _shard_dims(rows, cols, kind)
    tr = min(256, sr)

    def kern(j_ref, s_ref, o_ref):
        o_ref[...] = s_ref[...].astype(BF16)

    shape, block = (rows, cols), (tr, sc)
    if kind == "col":
        omap = lambda i, j_ref: (i, j_ref[0])
    elif kind == "chip":
        shape, block = (N_CHIP, rows, sc), (None, tr, sc)
        omap = lambda i, j_ref: (j_ref[0], i, 0)
    else:
        omap = lambda i, j_ref: (j_ref[0] * (sr // tr) + i, 0)
    gs = pltpu.PrefetchScalarGridSpec(
        num_scalar_prefetch=1, grid=(sr // tr,),
        in_specs=[pl.BlockSpec((None, tr, sc), lambda i, j_ref: (l, i, 0))], out_specs=pl.BlockSpec(block, omap))
    return pl.pallas_call(kern, out_shape=jax.ShapeDtypeStruct(shape, BF16), grid_spec=gs, name="cast_into_full",
                          compiler_params=_params("arbitrary"))(j_arr, shard)


def _gather_comm(fulls, cw=None):
    n_big = len(fulls)
    n_piece = n_big + (0 if cw is None else 1)

    def pieces(in_refs, o_refs, js, c):
        def piece(p, chip, half):
            if p == n_big:
                return o_refs[p].at[half, chip]
            _, rows, cols, kind = fulls[p]
            return _window(o_refs[p], rows, cols, kind, chip, half)

        def mine(p):
            return in_refs[p].at[c] if p == n_big else piece(p, js, c)

        return piece, mine

    def local_copies(in_refs, piece, js, loc_sem):
        if cw is None:
            return []
        return [pltpu.make_async_copy(in_refs[n_big].at[half], piece(n_big, js, half), loc_sem.at[half]) for half in range(2)]

    def ici_copy(piece, mine, js, x, y, c, r, p, send_sem, recv_sem):
        return pltpu.make_async_remote_copy(mine(p), piece(p, js, c), send_sem.at[r * n_piece + p], recv_sem.at[r * n_piece + p],
                                            _rel_dev(x, y, c, r), MESH)

    def start(in_refs, o_refs, sems):
        send_sem, recv_sem, _, _, loc_sem = sems
        x, y, c, j = _mesh_pos()

        def run(js):
            piece, mine = pieces(in_refs, o_refs, js, c)
            for cp in local_copies(in_refs, piece, js, loc_sem):
                cp.start()
            for r in range(3):
                for p in range(n_piece):
                    ici_copy(piece, mine, js, x, y, c, r, p, send_sem, recv_sem).start()

        _for_my_chip(j, run)

    def finish(in_refs, o_refs, sems):
        send_sem, recv_sem, fsend_sem, frecv_sem, loc_sem = sems
        x, y, c, j = _mesh_pos()

        def run(js):
            piece, mine = pieces(in_refs, o_refs, js, c)
            fwds = []
            for r in range(3):
                ks = js ^ _REL_BITS[r]
                for p in range(n_piece):
                    got = piece(p, ks, c)
                    pltpu.make_async_remote_copy(got, got, send_sem.at[r * n_piece + p], recv_sem.at[r * n_piece + p],
                                                 _rel_dev(x, y, c, r), MESH).wait_recv()
                    cp = pltpu.make_async_remote_copy(got, got, fsend_sem.at[r * n_piece + p], frecv_sem.at[r * n_piece + p],
                                                      (x, y, 1 - c), MESH)
                    cp.start()
                    fwds.append(cp)
            for r in range(3):
                ks = js ^ _REL_BITS[r]
                for p in range(n_piece):
                    got = piece(p, ks, 1 - c)
                    pltpu.make_async_remote_copy(got, got, fsend_sem.at[r * n_piece + p], frecv_sem.at[r * n_piece + p],
                                                 (x, y, 1 - c), MESH).wait_recv()
            for r in range(3):
                for p in range(n_piece):
                    ici_copy(piece, mine, js, x, y, c, r, p, send_sem, recv_sem).wait_send()
            for cp in fwds:
                cp.wait_send()
            for cp in local_copies(in_refs, piece, js, loc_sem):
                cp.wait()

        _for_my_chip(j, run)

    out_shape = [jax.ShapeDtypeStruct(a.shape, BF16) for a, _, _, _ in fulls]
    ins = [a for a, _, _, _ in fulls]
    if cw is not None:
        out_shape.append(jax.ShapeDtypeStruct((DEPTH, N_CHIP, 3, D_MODEL // N_CHIP), F32))
        ins.append(cw)
    scratch = [pltpu.SemaphoreType.DMA((3 * n_piece,))] * 4 + [pltpu.SemaphoreType.DMA((2,))]
    return _Hosted(ins, out_shape, {p: p for p in range(n_big)}, scratch, start, finish)


def _compose(*comms):
    comms = [cm for cm in comms if cm is not None]
    if len(comms) <= 1:
        return comms[0] if comms else None
    ins, outs, aliases, scratch, cuts = [], [], {}, [], []
    for cm in comms:
        cuts.append((len(ins), len(outs), len(scratch)))
        aliases.update({len(ins) + i: len(outs) + o for i, o in cm.aliases.items()})
        ins, outs, scratch = ins + cm.inputs, outs + cm.out_shape, scratch + cm.scratch

    def parts(a, b, s):
        for cm, (i0, o0, s0) in zip(comms, cuts):
            yield cm, a[i0:i0 + len(cm.inputs)], b[o0:o0 + len(cm.out_shape)], s[s0:s0 + len(cm.scratch)]

    def start(a, b, s):
        for cm, pa, pb, ps in parts(a, b, s):
            cm.start(pa, pb, ps)

    def finish(a, b, s):
        for cm, pa, pb, ps in parts(a, b, s):
            cm.finish(pa, pb, ps)

    return _Hosted(ins, outs, aliases, scratch, start, finish)


def _run_comm(comm, name):
    n_in, n_out = len(comm.inputs), len(comm.out_shape)

    def body(*refs):
        comm.start(refs[:n_in], refs[n_in:n_in + n_out], refs[n_in + n_out:])
        comm.finish(refs[:n_in], refs[n_in:n_in + n_out], refs[n_in + n_out:])

    return pl.pallas_call(body, out_shape=comm.out_shape, in_specs=[ANY] * n_in, out_specs=[ANY] * n_out,
                          input_output_aliases=comm.aliases, scratch_shapes=comm.scratch, name=name)(*comm.inputs)


def _sibling_exchange_comm(gb):
    n = len(gb)

    def copies(g_refs, o_refs, sems, c):
        send_sem, recv_sem = sems
        x, y, _, _ = _mesh_pos()
        return [pltpu.make_async_remote_copy(_window(g_refs[t], rows, cols, kind, chip, 1 - c),
                                             _window(o_refs[t], rows, cols, kind, chip, 1 - c),
                                             send_sem.at[N_CHIP * t + chip], recv_sem.at[N_CHIP * t + chip], (x, y, 1 - c), MESH)
                for t, (_, rows, cols, kind) in enumerate(gb) for chip in range(N_CHIP)]

    def start(g_refs, o_refs, sems):
        for cp in copies(g_refs, o_refs, sems, lax.axis_index("c")):
            cp.start()

    def finish(g_refs, o_refs, sems):
        c = lax.axis_index("c")
        for cp in copies(g_refs, o_refs, sems, 1 - c):
            cp.wait_recv()
        for cp in copies(g_refs, o_refs, sems, c):
            cp.wait_send()

    return _Hosted([a for a, _, _, _ in gb], [jax.ShapeDtypeStruct(a.shape, BF16) for a, _, _, _ in gb], {},
                   [pltpu.SemaphoreType.DMA((N_CHIP * n,))] * 2, start, finish)


def _half_add(jc_arr, g, sib, rows, cols, kind):
    sr, sc = _shard_dims(rows, cols, kind)
    hr = sr // 2

    def kern(jc_ref, g_ref, s_ref, ob_ref, of_ref):
        v = g_ref[...] + s_ref[...].astype(F32)
        ob_ref[...] = v.astype(BF16)

        @pl.when(pl.program_id(0) == jc_ref[0])
        def _():
            of_ref[...] = v

    if kind == "col":
        imap = lambda j, jc_ref: (jc_ref[1], j)
    else:
        imap = lambda j, jc_ref: (2 * j + jc_ref[1], 0)
    gs = pltpu.PrefetchScalarGridSpec(
        num_scalar_prefetch=1, grid=(N_CHIP,),
        in_specs=[pl.BlockSpec((hr, sc), imap), pl.BlockSpec((hr, sc), imap)],
        out_specs=[pl.BlockSpec((None, hr, sc), lambda j, jc_ref: (j, 0, 0)), pl.BlockSpec((hr, sc), lambda j, jc_ref: (0, 0))])
    return pl.pallas_call(
        kern, out_shape=(jax.ShapeDtypeStruct((N_CHIP, hr, sc), BF16), jax.ShapeDtypeStruct((hr, sc), F32)),
        grid_spec=gs, name="grad_half_add", compiler_params=_params("arbitrary"))(jc_arr, g, sib)


def _chip_exchange_comm(sbs):
    n = len(sbs)

    def copies(s_refs, o_refs, sems):
        send_sem, recv_sem = sems
        x, y, c, j = _mesh_pos()
        return [pltpu.make_async_remote_copy(s_refs[t].at[j ^ _REL_BITS[r]], o_refs[t].at[r], send_sem.at[n * r + t],
                                             recv_sem.at[n * r + t], _rel_dev(x, y, c, r), MESH)
                for r in range(3) for t in range(n)]

    def start(s_refs, o_refs, sems):
        for cp in copies(s_refs, o_refs, sems):
            cp.start()

    def finish(s_refs, o_refs, sems):
        for cp in copies(s_refs, o_refs, sems):
            cp.wait()

    return _Hosted(sbs, [jax.ShapeDtypeStruct((3,) + a.shape[1:], BF16) for a in sbs], {},
                   [pltpu.SemaphoreType.DMA((3 * n,))] * 2, start, finish)


def _owner_sum(jc_arr, sf, rb, l, into=None):
    hr, sc = sf.shape

    def kern(jc_ref, s_ref, r0_ref, r1_ref, r2_ref, *rest):
        o_ref = rest[-1]
        o_ref[...] = ((s_ref[...] + r0_ref[...].astype(F32)) + r1_ref[...].astype(F32)) + r2_ref[...].astype(F32)

    in_specs = [pl.BlockSpec((hr, sc), lambda i, jc_ref: (0, 0))]
    in_specs += [pl.BlockSpec((None, hr, sc), lambda i, jc_ref, r=r: (r, 0, 0)) for r in range(3)]
    args = [jc_arr, sf, rb, rb, rb]
    aliases = {}
    if into is not None:
        in_specs.append(ANY)
        args.append(into)
        aliases = {len(args) - 1: 0}
    gs = pltpu.PrefetchScalarGridSpec(
        num_scalar_prefetch=1, grid=(1,), in_specs=in_specs,
        out_specs=pl.BlockSpec((None, hr, sc), lambda i, jc_ref: (l, jc_ref[1], 0)))
    return pl.pallas_call(kern, out_shape=jax.ShapeDtypeStruct((DEPTH, 2 * hr, sc), F32), grid_spec=gs,
                          input_output_aliases=aliases, name="grad_owner_sum", compiler_params=_params("arbitrary"))(*args)


def _sibling_assemble_comm(grads, layers):
    n = len(grads)
    todo = [(q, l) for q in range(n) for l in layers[q]]

    def copies(o_refs, sems, half):
        send_sem, recv_sem = sems
        x, y, c, _ = _mesh_pos()
        out = []
        for k, (q, l) in enumerate(todo):
            hr = grads[q].shape[1] // 2
            w = o_refs[q].at[l, pl.ds(half * hr, hr), :]
            out.append(pltpu.make_async_remote_copy(w, w, send_sem.at[k], recv_sem.at[k], (x, y, 1 - c), MESH))
        return out

    def start(_, o_refs, sems):
        for cp in copies(o_refs, sems, lax.axis_index("c")):
            cp.start()

    def finish(_, o_refs, sems):
        c = lax.axis_index("c")
        for cp in copies(o_refs, sems, 1 - c):
            cp.wait_recv()
        for cp in copies(o_refs, sems, c):
            cp.wait_send()

    return _Hosted(grads, [jax.ShapeDtypeStruct(g.shape, F32) for g in grads], {q: q for q in range(n)},
                   [pltpu.SemaphoreType.DMA((len(todo),))] * 2, start, finish)


def _adamw_math(w, g, m, v):
    m = ADAM_B1 * m + (1.0 - ADAM_B1) * g
    v = ADAM_B2 * v + (1.0 - ADAM_B2) * jnp.square(g)
    m_hat = m / (1.0 - ADAM_B1 ** ADAM_STEP)
    v_hat = v / (1.0 - ADAM_B2 ** ADAM_STEP)
    delta = -ADAM_LR * (m_hat / (jnp.sqrt(v_hat) + ADAM_EPS) + ADAM_WD * w)
    return delta, m, v


def _adamw(w, g, m, v):
    shape = w.shape
    C = shape[-1]
    R = int(np.prod(shape[:-1]))
    tr = min(256, R)
    args = [a.reshape(R, C) for a in (w, g, m, v)]

    def kern(w_ref, g_ref, m_ref, v_ref, go_ref, d_ref, nm_ref, nv_ref):
        g_val = g_ref[...]
        d, nm, nv = _adamw_math(w_ref[...], g_val, m_ref[...], v_ref[...])
        go_ref[...] = g_val
        d_ref[...] = d
        nm_ref[...] = nm
        nv_ref[...] = nv

    spec = pl.BlockSpec((tr, C), lambda i: (i, 0))
    outs = pl.pallas_call(
        kern, out_shape=[jax.ShapeDtypeStruct((R, C), F32)] * 4, grid=(R // tr,), in_specs=[spec] * 4, out_specs=[spec] * 4,
        name="adamw", compiler_params=_params("parallel"))(*args)
    return [o.reshape(shape) for o in outs]


_ROW_G_MIX, _ROW_B_GATES, _ROW_SINKS, _ROW_CONV, _ROW_G_MLP, _ROW_G_FINAL, _ROW_LOSS = 0, 2, 6, 8, 16, 18, 19


def _small_step(parts, params, moms, vels):
    names = ["g_mix", "b_gates", "sinks", "conv_w", "conv_b", "g_mlp", "g_final"]
    D = D_MODEL
    QW = D // N_CHIP
    n_dev = 8

    def body(*refs):
        it = iter(refs)
        dgmix = [next(it) for _ in range(DEPTH)]
        dbg = [next(it) for _ in range(DEPTH)]
        dsk = [next(it) for _ in range(DEPTH)]
        dwb = [next(it) for _ in range(DEPTH)]
        dgmlp = [next(it) for _ in range(DEPTH)]
        lst = next(it)
        p_refs = {n: next(it) for n in names}
        m_refs = {n: next(it) for n in names}
        v_refs = {n: next(it) for n in names}
        loss_ref = next(it)
        outs = {n: [next(it) for _ in range(4)] for n in names}
        pack_ref, all_ref, send_sem, recv_sem = next(it), next(it), next(it), next(it)

        x, y, c, j = _mesh_pos()
        me = 4 * x + 2 * y + c
        pack_ref[...] = jnp.zeros_like(pack_ref)
        for l in range(DEPTH):
            pack_ref[_ROW_G_MIX + l:_ROW_G_MIX + l + 1, :] = dgmix[l][...]
            pack_ref[_ROW_B_GATES + 2 * l:_ROW_B_GATES + 2 * l + 1, :] = dbg[l][:, 0:D]
            pack_ref[_ROW_B_GATES + 2 * l + 1:_ROW_B_GATES + 2 * l + 2, :] = dbg[l][:, D:2 * D]
            pack_ref[_ROW_SINKS + l:_ROW_SINKS + l + 1, 0:128] = dsk[l][0:1, :]
            pack_ref[_ROW_CONV + 4 * l:_ROW_CONV + 4 * l + 4, :] = dwb[l][0:4, :]
            pack_ref[_ROW_G_MLP + l:_ROW_G_MLP + l + 1, :] = dgmlp[l][...]
        pack_ref[_ROW_G_FINAL:_ROW_G_FINAL + 1, :] = lst[0:1, :]
        pack_ref[_ROW_LOSS:_ROW_LOSS + 1, :] = lst[1:2, :]

        all_ref[me] = pack_ref[...]
        cps = []
        for k in range(1, n_dev):
            dx_, dy_, dc_ = (k >> 2) & 1, (k >> 1) & 1, k & 1
            peer = (x ^ dx_, y ^ dy_, c ^ dc_)
            cp = pltpu.make_async_remote_copy(pack_ref, all_ref.at[me], send_sem.at[k - 1], recv_sem.at[k - 1], peer, MESH)
            cp.start()
            cps.append(cp)
        for cp in cps:
            cp.wait()

        tot = all_ref[0]
        for d in range(1, n_dev):
            tot = tot + all_ref[d]
        pack_ref[...] = tot

        loss_ref[...] = pack_ref[_ROW_LOSS:_ROW_LOSS + 1, 0:1]

        def finish(name, idx, g):
            w, m, v = p_refs[name][idx], m_refs[name][idx], v_refs[name][idx]
            d, nm, nv = _adamw_math(w, g, m, v)
            for ref, val in zip(outs[name], (g, d, nm, nv)):
                ref[idx] = val

        for l in range(DEPTH):
            finish("g_mix", (slice(l, l + 1), slice(None)), pack_ref[_ROW_G_MIX + l:_ROW_G_MIX + l + 1, :])
            finish("g_mlp", (slice(l, l + 1), slice(None)), pack_ref[_ROW_G_MLP + l:_ROW_G_MLP + l + 1, :])
            finish("conv_b", (slice(l, l + 1), slice(None)), pack_ref[_ROW_CONV + 4 * l + 3:_ROW_CONV + 4 * l + 4, :])
            finish("sinks", (slice(l, l + 1), slice(None)), pack_ref[_ROW_SINKS + l:_ROW_SINKS + l + 1, 0:N_Q_HEADS])
            for hf in range(2):
                finish("b_gates", (slice(l, l + 1), slice(hf * D, (hf + 1) * D)),
                       pack_ref[_ROW_B_GATES + 2 * l + hf:_ROW_B_GATES + 2 * l + hf + 1, :])
        finish("g_final", (slice(0, 1), slice(None)), pack_ref[_ROW_G_FINAL:_ROW_G_FINAL + 1, :])

        def conv_w_chip(js):
            for l in range(DEPTH):
                for k in range(3):
                    row = _ROW_CONV + 4 * l + k
                    finish("conv_w", (l, slice(k, k + 1), slice(None)), pack_ref[row:row + 1, js * QW:(js + 1) * QW])

        _for_my_chip(j, conv_w_chip)

    vm = pl.BlockSpec(memory_space=pltpu.VMEM)
    ins = (parts["g_mix"] + parts["b_gates"] + parts["sinks"] + parts["conv"] + parts["g_mlp"] + [parts["loss"]]
           + [params[n] for n in names] + [moms[n] for n in names] + [vels[n] for n in names])
    out_shape = [jax.ShapeDtypeStruct((1, 1), F32)]
    for n in names:
        out_shape += [jax.ShapeDtypeStruct(params[n].shape, F32)] * 4
    res = pl.pallas_call(
        body, out_shape=out_shape, in_specs=[vm] * len(ins), out_specs=[vm] * len(out_shape),
        scratch_shapes=[pltpu.VMEM((SMALL_ROWS, D), F32), pltpu.VMEM((n_dev, SMALL_ROWS, D), F32),
                        pltpu.SemaphoreType.DMA((n_dev - 1,)), pltpu.SemaphoreType.DMA((n_dev - 1,))],
        name="small_allreduce_adamw")(*ins)
    loss = res[0]
    out = {n: res[1 + 4 * i:5 + 4 * i] for i, n in enumerate(names)}
    return loss, out


def kernel(x, g_mix, w_in, b_gates, sinks, w_attn_out, conv_w, conv_b, w_conv_out, w_o, g_mlp, w_up, w_down, g_final, loss_target, m_g_mix, m_w_in, m_b_gates, m_sinks, m_w_attn_out, m_conv_w, m_conv_b, m_w_conv_out, m_w_o, m_g_mlp, m_w_up, m_w_down, m_g_final, v_g_mix, v_w_in, v_b_gates, v_sinks, v_w_attn_out, v_conv_w, v_conv_b, v_w_conv_out, v_w_o, v_g_mlp, v_w_up, v_w_down, v_g_final):
    B, S, D = x.shape
    T = B * S
    big_w = dict(w_in=w_in, w_attn_out=w_attn_out, w_conv_out=w_conv_out, w_o=w_o, w_up=w_up, w_down=w_down)
    big_m = dict(w_in=m_w_in, w_attn_out=m_w_attn_out, w_conv_out=m_w_conv_out, w_o=m_w_o, w_up=m_w_up, w_down=m_w_down)
    big_v = dict(w_in=v_w_in, w_attn_out=v_w_attn_out, w_conv_out=v_w_conv_out, w_o=v_w_o, w_up=v_w_up, w_down=v_w_down)

    c_arr = lax.axis_index("c").astype(jnp.int32).reshape(1)
    j_arr = (2 * lax.axis_index("x") + lax.axis_index("y")).astype(jnp.int32).reshape(1)
    jc_arr = jnp.concatenate([j_arr, c_arr])
    order = [(n, l) for n, _, _, _ in BIG for l in range(DEPTH)]
    dims = {n: (r, c_, k) for n, r, c_, k in BIG}

    wdims = dict(dims, w_up=(D, D_FF, "chip"))
    full = {(n, l): _cast_into_full(j_arr, big_w[n], l, *wdims[n]) for n, l in order}
    mixers = ("w_attn_out", "w_conv_out", "w_o")

    def gather_of(keys):
        return _gather_comm([(full[k],) + wdims[k[0]] for k in keys])

    carried = {("proj", 0): [(n, 0) for n in mixers], ("attn", 0): [("w_up", 0), ("w_down", 0)],
               ("mlp", 0): [("w_in", 1)] + [(n, 1) for n in mixers], ("attn", 1): [("w_up", 1), ("w_down", 1)]}

    def carry(fn, where, *args):
        keys = carried.get(where)
        if keys is None:
            return fn(*args)
        res, got = fn(*args, comm=gather_of(keys))
        full.update(zip(keys, got))
        return res

    first = _run_comm(_gather_comm([(full["w_in", 0],) + dims["w_in"]], conv_w), "gather_weights")
    full["w_in", 0] = first[0]
    conv_w_full = jnp.transpose(first[1], (0, 2, 1, 3)).reshape(DEPTH, 3, D)
    attn_bias = _attn_bias_table()

    xs = [x.reshape(T, D)]
    saved = []
    for l in range(DEPTH):
        ht, pqkv, pconv, pgate = carry(_norm_proj, ("proj", l), xs[-1], g_mix[l:l + 1], full["w_in", l])
        att, att_t, lse = carry(_attn_fwd, ("attn", l), pqkv, sinks[l], attn_bias, S)
        cv, cv_t = _conv_fwd(pconv, conv_w_full[l], conv_b[l:l + 1], S)
        x1, ya, yc, mg_t = _mix_fwd(xs[-1], att, cv, pgate, b_gates[l:l + 1], full["w_attn_out", l], full["w_conv_out", l],
                                    full["w_o", l])
        x2, a = carry(_mlp_fwd, ("mlp", l), x1, g_mlp[l:l + 1], full["w_up", l], full["w_down", l].reshape(N_CHIP, D, D))
        saved.append(dict(ht=ht, pqkv=pqkv, pconv=pconv, pgate=pgate, att=att, att_t=att_t, lse=lse, cv_t=cv_t, x1=x1, ya=ya,
                          yc=yc, mg_t=mg_t, a=a))
        xs.append(x2)

    loss_stats, dx = _loss_bwd(xs[-1], g_final.reshape(1, D), loss_target.reshape(T, D))

    parts = dict(g_mix=[None] * DEPTH, b_gates=[None] * DEPTH, sinks=[None] * DEPTH, conv=[None] * DEPTH,
                 g_mlp=[None] * DEPTH, loss=loss_stats)
    gf, gb, pre, got, mine = {}, {}, {}, {}, {}

    def sibling_exchange(keys):
        return _sibling_exchange_comm([(gb[k],) + dims[k[0]] for k in keys])

    def half_adds(keys, sib):
        for k, s in zip(keys, sib):
            pre[k] = _half_add(jc_arr, gf[k], s, *dims[k[0]])

    def chip_exchange(keys):
        return _chip_exchange_comm([pre[k][0] for k in keys])

    def owner_sums(keys):
        for n, l in keys:
            mine[n] = _owner_sum(jc_arr, pre[n, l][1], got[n, l], l, mine.get(n))

    def run(fn, comms, *args):
        if not comms:
            return fn(*args), []
        res, arrived = fn(*args, comm=_compose(*comms))
        outs, pos = [], 0
        for cm in comms:
            outs.append(arrived[pos:pos + len(cm.out_shape)])
            pos += len(cm.out_shape)
        return res, outs

    assert DEPTH == 2
    upper = [(n, 1) for n, _, _, _ in BIG]
    early = [(n, 0) for n in mixers + ("w_up", "w_down")]
    for l in reversed(range(DEPTH)):
        W = {n: full[(n, l)] for n in big_w}
        sv = saved[l]
        last = l == 0
        mlp_args = (dx, sv["x1"], sv["a"], g_mlp[l:l + 1], W["w_up"], W["w_down"].reshape(N_CHIP, D, D))
        (dx1, da, u_t, h2_t, dyb, parts["g_mlp"][l]), arrived = run(_mlp_bwd, [chip_exchange(upper)] if last else [], *mlp_args)
        if last:
            got.update(zip(upper, arrived[0]))
        gf["w_up", l], gb["w_up", l] = _dw(h2_t, da)
        gf["w_down", l], gb["w_down", l] = _dw(u_t, dyb.reshape(-1, _dw_chunk(T), D))
        datt, dcv, dya, dyc, dgate, parts["b_gates"][l] = _mix_bwd(
            dx1, sv["ya"], sv["yc"], sv["pgate"], b_gates[l:l + 1], W["w_attn_out"], W["w_conv_out"], W["w_o"])
        gf["w_o", l], gb["w_o", l] = _dw(sv["mg_t"], dx1)
        gf["w_attn_out", l], gb["w_attn_out", l] = _dw(sv["att_t"], dya)
        gf["w_conv_out", l], gb["w_conv_out", l] = _dw(sv["cv_t"], dyc)
        conv_args = (dcv, sv["pconv"], conv_w_full[l], conv_b[l:l + 1], S)
        (dconv, parts["conv"][l]), arrived = run(_conv_bwd, [sibling_exchange(early)] if last else [], *conv_args)
        if last:
            half_adds(early, arrived[0])
        attn_args = (sv["pqkv"], datt, sv["lse"], sinks[l], attn_bias, S)
        (dq, dkv, parts["sinks"][l]), arrived = run(_attn_bwd, [chip_exchange(early)] if last else [], *attn_args)
        if last:
            got.update(zip(early, arrived[0]))
            owner_sums(upper + early)
        pieces = [(dq, D), (dkv, QKV_W - D), (dconv, CONV_W), (dgate, GATE_W)]
        gf["w_in", l], gb["w_in", l] = _dw_pieces(sv["ht"], pieces)
        in_args = (pieces, W["w_in"], xs[l], dx1, g_mix[l:l + 1])
        if not last:
            (dx, parts["g_mix"][l]), arrived = run(_inproj_bwd, [sibling_exchange(upper)], *in_args)
            half_adds(upper, arrived[0])
        else:
            tail = [("w_in", 0)]
            half_adds(tail, _run_comm(sibling_exchange(tail), "grad_sibling_exchange"))
            done = _sibling_assemble_comm(list(mine.values()), [(1,) if n == "w_in" else (0, 1) for n in mine])
            (dx, parts["g_mix"][l]), arrived = run(_inproj_bwd, [chip_exchange(tail), done], *in_args)
            got.update(zip(tail, arrived[0]))
            mine = dict(zip(mine, arrived[1]))
            owner_sums(tail)
            mine["w_in"] = _run_comm(_sibling_assemble_comm([mine["w_in"]], [(0,)]), "grad_sibling_assemble")[0]
    grads = mine

    res = {}
    for n in big_w:
        res[n] = tuple(_adamw(big_w[n], grads[n], big_m[n], big_v[n]))

    small_p = dict(g_mix=g_mix, b_gates=b_gates, sinks=sinks, conv_w=conv_w, conv_b=conv_b, g_mlp=g_mlp, g_final=g_final.reshape(1, D))
    small_m = dict(g_mix=m_g_mix, b_gates=m_b_gates, sinks=m_sinks, conv_w=m_conv_w, conv_b=m_conv_b, g_mlp=m_g_mlp,
                   g_final=m_g_final.reshape(1, D))
    small_v = dict(g_mix=v_g_mix, b_gates=v_b_gates, sinks=v_sinks, conv_w=v_conv_w, conv_b=v_conv_b, g_mlp=v_g_mlp,
                   g_final=v_g_final.reshape(1, D))
    loss, small = _small_step(parts, small_p, small_m, small_v)
    for n, vals in small.items():
        res[n] = tuple(v.reshape(D) for v in vals) if n == "g_final" else tuple(vals)

    weights = ["g_mix", "w_in", "b_gates", "sinks", "w_attn_out", "conv_w", "conv_b", "w_conv_out", "w_o", "g_mlp", "w_up",
               "w_down", "g_final"]
    out = [loss.reshape(()), dx.reshape(B, S, D)]
    for k in range(4):
        out += [res[n][k] for n in weights]
    return tuple(out)
```

```python
import functools

import numpy as np
import jax
import jax.numpy as jnp
from jax import lax
from jax.experimental import pallas as pl
from jax.experimental.pallas import tpu as pltpu

F32 = jnp.float32
BF16 = jnp.bfloat16

D_MODEL = 1024
HEAD_DIM = 64
N_Q_HEADS = 16
N_KV_HEADS = 4
GQA_GROUP = 4
WINDOW = 128
D_FF = 4096
DEPTH = 2
RMS_EPS = 1e-6
NEG_INF = -1e30
ATTN_SCALE = HEAD_DIM ** -0.5
QKV_W = 1536
CONV_W = 3072
GATE_W = 2048
IN_COLS = QKV_W + CONV_W + GATE_W
COL_TILE = 512
N_CHIP = 4
ADAM_LR = 0.001
ADAM_B1 = 0.9
ADAM_B2 = 0.999
ADAM_EPS = 1e-08
ADAM_WD = 0.01
ADAM_STEP = 10
V7X_VMEM_BYTES = 64 * 2 ** 20
VMEM_LIMIT = V7X_VMEM_BYTES - 8 * 2 ** 20
MESH = pl.DeviceIdType.MESH
ANY = pl.BlockSpec(memory_space=pl.ANY)
SMALL_ROWS = 24

_SLOPES = [float(v) for v in np.power(np.float32(2.0), -8.0 * np.arange(1, N_Q_HEADS + 1, dtype=np.float32) / N_Q_HEADS)]


def _params(*sem):
    return pltpu.CompilerParams(dimension_semantics=sem, vmem_limit_bytes=VMEM_LIMIT)


class _Hosted:
    def __init__(self, inputs, out_shape, aliases, scratch, start, finish):
        self.inputs, self.out_shape, self.aliases, self.scratch = list(inputs), list(out_shape), dict(aliases), list(scratch)
        self.start, self.finish = start, finish


def _hosted_call(comm, kern, *, out_shape, grid, in_specs, out_specs, args, name, sem, scratch_shapes=()):
    single = not isinstance(out_shape, (tuple, list))
    outs = [out_shape] if single else list(out_shape)
    ospecs = [out_specs] if single else list(out_specs)
    if comm is None:
        res = pl.pallas_call(kern, out_shape=outs, grid=grid, in_specs=list(in_specs), out_specs=ospecs,
                             scratch_shapes=list(scratch_shapes), name=name, compiler_params=_params(*sem))(*args)
        return res[0] if single else res
    n_in, n_out, n_scr = len(args), len(outs), len(scratch_shapes)
    ci, co, cs = len(comm.inputs), len(comm.out_shape), len(comm.scratch)

    def body(*refs):
        cuts = np.cumsum([0, n_in, ci, n_out, co, n_scr, cs])
        a, b, c, d, e, f = [refs[lo:hi] for lo, hi in zip(cuts[:-1], cuts[1:])]
        ids = [pl.program_id(k) for k in range(len(grid))]
        first = functools.reduce(jnp.logical_and, [i == 0 for i in ids])
        last = functools.reduce(jnp.logical_and, [i == n - 1 for i, n in zip(ids, grid)])
        pl.when(first)(lambda: comm.start(b, d, f))
        kern(*a, *c, *e)
        pl.when(last)(lambda: comm.finish(b, d, f))

    res = pl.pallas_call(
        body, out_shape=outs + comm.out_shape, grid=grid, in_specs=list(in_specs) + [ANY] * ci, out_specs=ospecs + [ANY] * co,
        scratch_shapes=list(scratch_shapes) + comm.scratch,
        input_output_aliases={n_in + i: n_out + o for i, o in comm.aliases.items()},
        name=name + "_carrier", compiler_params=_params(*(["arbitrary"] * len(grid))))(*args, *comm.inputs)
    main = res[:n_out]
    return (main[0] if single else main), res[n_out:]


def _nt(a, b):
    return lax.dot_general(a, b, (((1,), (1,)), ((), ())), preferred_element_type=F32)


def _tn(a, b):
    return lax.dot_general(a, b, (((0,), (0,)), ((), ())), preferred_element_type=F32)


def _nn(a, b):
    return jnp.dot(a, b, preferred_element_type=F32)


def _rms_stats(xf):
    r = lax.rsqrt(jnp.mean(xf * xf, axis=-1, keepdims=True) + RMS_EPS)
    return r, xf * r


def _rms_bwd(dh, xh, r, g):
    dxh = dh * g
    dx = r * (dxh - xh * jnp.mean(dxh * xh, axis=-1, keepdims=True))
    dg = jnp.sum(dh * xh, axis=0, keepdims=True)
    return dx, dg


def _dw_chunk(T):
    return min(2048, T)


def _row_halves(tm):
    return (slice(0, tm // 2), slice(tm // 2, tm))


def _resident(shape):
    return pl.BlockSpec(shape, lambda *_: (0,) * len(shape), pipeline_mode=pl.Buffered(1))


def _norm_proj(x, g, w, comm=None):
    T, D = x.shape
    tm = min(512, T)
    tk = _dw_chunk(T)
    per = tk // tm
    widths = (QKV_W, CONV_W, GATE_W)

    def kern(x_ref, g_ref, w_ref, ht_ref, *o_refs):
        _, xh = _rms_stats(x_ref[...])
        h = (xh * g_ref[...]).astype(BF16)
        ht_ref[...] = h.T
        off = 0
        for o_ref, wd in zip(o_refs, widths):
            o_ref[...] = _nn(h, w_ref[:, off:off + wd]).astype(BF16)
            off += wd

    row = lambda wd: pl.BlockSpec((tm, wd), lambda i: (i, 0))
    return _hosted_call(
        comm, kern,
        out_shape=[jax.ShapeDtypeStruct((T // tk, D, tk), BF16)] + [jax.ShapeDtypeStruct((T, wd), BF16) for wd in widths],
        grid=(T // tm,), in_specs=[row(D), pl.BlockSpec((1, D), lambda i: (0, 0)), _resident((D, IN_COLS))],
        out_specs=[pl.BlockSpec((None, D, tm), lambda i: (i // per, 0, i % per))] + [row(wd) for wd in widths],
        name="norm_proj", sem=("parallel",), args=(x, g, w))


GW = GQA_GROUP * WINDOW
BAND = 2 * WINDOW
KV_W = N_KV_HEADS * HEAD_DIM


def _attn_bias_table():
    jj = np.arange(BAND)[:, None]
    col = np.arange(GW)[None, :]
    dist = WINDOW + (col % WINDOW) - jj
    valid = (dist >= 0) & (dist < WINDOW)
    slopes = np.asarray(_SLOPES, np.float32).reshape(N_KV_HEADS, GQA_GROUP)
    tab = np.empty((2, N_KV_HEADS, BAND, GW), np.float32)
    for hk in range(N_KV_HEADS):
        bias = -slopes[hk][col // WINDOW] * dist.astype(np.float32)
        tab[0, hk] = np.where(valid, bias, np.float32(NEG_INF))
        tab[1, hk] = np.where(valid & (jj >= WINDOW), bias, np.float32(NEG_INF))
    return jnp.asarray(tab)


def _stack_heads(ref, hk):
    return jnp.concatenate(
        [ref[:, HEAD_DIM * (GQA_GROUP * hk + g): HEAD_DIM * (GQA_GROUP * hk + g + 1)] for g in range(GQA_GROUP)], axis=0)


def _kv_band(cur_ref, prev_ref, hk):
    k0 = N_Q_HEADS * HEAD_DIM
    sl = slice(HEAD_DIM * hk, HEAD_DIM * (hk + 1))
    ksl, vsl = slice(k0 + sl.start, k0 + sl.stop), slice(k0 + KV_W + sl.start, k0 + KV_W + sl.stop)
    k_band = jnp.concatenate([prev_ref[:, sl], cur_ref[:, ksl]], axis=0)
    v_band = jnp.concatenate([prev_ref[:, KV_W + sl.start:KV_W + sl.stop], cur_ref[:, vsl]], axis=0)
    return k_band, v_band


def _lane_row(vals):
    return jnp.concatenate([jnp.full((1, WINDOW), v, F32) for v in vals], axis=1)


def _attn_fwd(pqkv, sinks, bias, seq, comm=None):
    T = pqkv.shape[0]
    nblk = seq // WINDOW

    def kern(sink_ref, cur_ref, prev_ref, bias_ref, o_ref, lse_ref, ot_ref):
        i = pl.program_id(0)
        first_i = ((i % nblk) == 0).astype(jnp.int32)
        bands = [_kv_band(cur_ref, prev_ref, hk) for hk in range(N_KV_HEADS)]
        sts = [_nt(bands[hk][0], _stack_heads(cur_ref, hk) * ATTN_SCALE) + bias_ref[first_i, hk] for hk in range(N_KV_HEADS)]
        ps, scales = [], []
        for hk in range(N_KV_HEADS):
            heads = [GQA_GROUP * hk + g for g in range(GQA_GROUP)]
            sink = _lane_row([sink_ref[h] for h in heads])
            m = jnp.maximum(jnp.max(sts[hk], axis=0, keepdims=True), sink)
            p = jnp.exp(sts[hk] - m)
            den = jnp.sum(p, axis=0, keepdims=True) + jnp.exp(sink - m)
            lse = m + jnp.log(den)
            for g, h in enumerate(heads):
                lse_ref[h:h + 1, :] = lse[:, WINDOW * g:WINDOW * (g + 1)]
            ps.append(p.astype(BF16))
            scales.append(1.0 / den)
        for hk in range(N_KV_HEADS):
            ot = _tn(bands[hk][1], ps[hk]) * scales[hk]
            for g in range(GQA_GROUP):
                h = GQA_GROUP * hk + g
                ot_ref[HEAD_DIM * h:HEAD_DIM * (h + 1), :] = ot[:, WINDOW * g:WINDOW * (g + 1)].astype(BF16)
        o_ref[...] = ot_ref[...].T

    return _hosted_call(
        comm, kern,
        out_shape=(jax.ShapeDtypeStruct((T, D_MODEL), BF16), jax.ShapeDtypeStruct((N_Q_HEADS, T), F32)),
        grid=(T // WINDOW,),
        in_specs=[pl.BlockSpec(memory_space=pltpu.SMEM),
                  pl.BlockSpec((WINDOW, QKV_W), lambda i: (i, 0)),
                  pl.BlockSpec((WINDOW, 2 * KV_W), lambda i: (jnp.maximum(i - 1, 0), 2)),
                  _resident(bias.shape)],
        out_specs=(pl.BlockSpec((WINDOW, D_MODEL), lambda i: (i, 0)), pl.BlockSpec((N_Q_HEADS, WINDOW), lambda i: (0, i))),
        scratch_shapes=[pltpu.VMEM((D_MODEL, WINDOW), BF16)],
        name="attn_fwd", sem=("parallel",), args=(sinks, pqkv, pqkv, bias))


def _pick_row(a, row):
    rid = lax.broadcasted_iota(jnp.int32, a.shape, 0)
    return jnp.sum(jnp.where(rid == row, a, 0.0), axis=0, keepdims=True)


def _conv_taps(yc, halo_yc, first_i):
    keep = (1 - first_i).astype(F32)
    p1 = _pick_row(halo_yc, 15) * keep
    p2 = _pick_row(halo_yc, 14) * keep
    rowid = lax.broadcasted_iota(jnp.int32, yc.shape, 0)
    s1 = jnp.where(rowid == 0, p1, pltpu.roll(yc, 1, 0))
    s2 = jnp.where(rowid == 0, p2, jnp.where(rowid == 1, p1, pltpu.roll(yc, 2, 0)))
    return s1, s2


def _conv_fwd(pconv, conv_w, conv_b, seq):
    T = pconv.shape[0]
    tm = min(512, seq)
    per_seq = seq // tm
    D = D_MODEL

    def kern(cur_ref, halo_ref, w_ref, b_ref, o_ref, ot_ref):
        i = pl.program_id(0)
        first_i = ((i % per_seq) == 0).astype(jnp.int32)
        cb = cur_ref[:, 0:D].astype(F32)
        yc = cur_ref[:, D:2 * D].astype(F32) * cur_ref[:, 2 * D:3 * D].astype(F32)
        halo_yc = halo_ref[:, D:2 * D].astype(F32) * halo_ref[:, 2 * D:3 * D].astype(F32)
        s1, s2 = _conv_taps(yc, halo_yc, first_i)
        z = w_ref[0:1, :] * s2 + w_ref[1:2, :] * s1 + w_ref[2:3, :] * yc
        cv = (cb * (z + b_ref[...])).astype(BF16)
        o_ref[...] = cv
        ot_ref[...] = cv.T

    return pl.pallas_call(
        kern, out_shape=(jax.ShapeDtypeStruct((T, D), BF16), jax.ShapeDtypeStruct((D, T), BF16)), grid=(T // tm,),
        in_specs=[pl.BlockSpec((tm, CONV_W), lambda i: (i, 0)),
                  pl.BlockSpec((16, CONV_W), lambda i: (jnp.maximum(i * (tm // 16) - 1, 0), 0)),
                  pl.BlockSpec((3, D), lambda i: (0, 0)), pl.BlockSpec((1, D), lambda i: (0, 0))],
        out_specs=(pl.BlockSpec((tm, D), lambda i: (i, 0)), pl.BlockSpec((D, tm), lambda i: (0, i))),
        name="conv_fwd", compiler_params=_params("parallel"))(pconv, pconv, conv_w, conv_b)


def _mix_fwd(x, att, cv, pgate, b_gates, wao, wco, wo):
    T, D = x.shape
    tm = min(512, T)

    halves = _row_halves(tm)

    def kern(x_ref, att_ref, cv_ref, pg_ref, bg_ref, wao_ref, wco_ref, wo_ref, x1_ref, ya_ref, yc_ref, mgt_ref, att_t_ref):
        att_t_ref[...] = att_ref[...].T
        yas = [_nn(att_ref[rows, :], wao_ref[...]) for rows in halves]
        ycs = [_nn(cv_ref[rows, :], wco_ref[...]) for rows in halves]
        mgs = []
        for rows, ya, yc in zip(halves, yas, ycs):
            sa = jax.nn.sigmoid(pg_ref[rows, 0:D].astype(F32) + bg_ref[:, 0:D])
            sc = jax.nn.sigmoid(pg_ref[rows, D:2 * D].astype(F32) + bg_ref[:, D:2 * D])
            mg = (sa * ya + sc * yc).astype(BF16)
            ya_ref[rows, :] = ya.astype(BF16)
            yc_ref[rows, :] = yc.astype(BF16)
            mgt_ref[:, rows] = mg.T
            mgs.append(mg)
        for rows, mg in zip(halves, mgs):
            x1_ref[rows, :] = x_ref[rows, :] + _nn(mg, wo_ref[...])

    row = lambda w: pl.BlockSpec((tm, w), lambda i: (i, 0))
    full = lambda a, b: pl.BlockSpec((a, b), lambda i: (0, 0))
    bf = jax.ShapeDtypeStruct((T, D), BF16)
    tsp = pl.BlockSpec((D, tm), lambda i: (0, i))
    bft = jax.ShapeDtypeStruct((D, T), BF16)
    return pl.pallas_call(
        kern, out_shape=(jax.ShapeDtypeStruct((T, D), F32), bf, bf, bft, bft), grid=(T // tm,),
        in_specs=[row(D), row(D), row(D), row(GATE_W), full(1, GATE_W)] + [_resident((D, D))] * 3,
        out_specs=(row(D), row(D), row(D), tsp, tsp),
        name="mix_fwd", compiler_params=_params("parallel"))(x, att, cv, pgate, b_gates, wao, wco, wo)


def _mlp_fwd(x1, g, wup, wdn, comm=None):
    T, D = x1.shape
    tm = min(1024, T)
    nj = D_FF // D

    def kern(x_ref, g_ref, wup_ref, wdn_ref, x2_ref, a_ref, h_scr, acc_scr):
        j = pl.program_id(1)

        @pl.when(j == 0)
        def _():
            xf = x_ref[...]
            _, xh = _rms_stats(xf)
            h_scr[...] = (xh * g_ref[...]).astype(BF16)
            acc_scr[...] = xf

        a = _nn(h_scr[...], wup_ref[j])
        a_ref[...] = a.astype(BF16)
        u = jnp.square(jnp.maximum(a, 0.0)).astype(BF16)
        acc_scr[...] += _nn(u, wdn_ref[j])

        @pl.when(j == nj - 1)
        def _():
            x2_ref[...] = acc_scr[...]

    return _hosted_call(
        comm, kern, out_shape=(jax.ShapeDtypeStruct((T, D), F32), jax.ShapeDtypeStruct((T, D_FF), BF16)), grid=(T // tm, nj),
        in_specs=[pl.BlockSpec((tm, D), lambda i, j: (i, 0)), pl.BlockSpec((1, D), lambda i, j: (0, 0)),
                  _resident((nj, D, D)), _resident((nj, D, D))],
        out_specs=(pl.BlockSpec((tm, D), lambda i, j: (i, 0)), pl.BlockSpec((tm, D), lambda i, j: (i, j))),
        scratch_shapes=[pltpu.VMEM((tm, D), BF16), pltpu.VMEM((tm, D), F32)],
        name="mlp_fwd", sem=("parallel", "arbitrary"), args=(x1, g, wup, wdn))


def _loss_bwd(x, g, tgt):
    T, D = x.shape
    tm = min(512, T)

    def kern(x_ref, g_ref, t_ref, st_ref, dx_ref):
        i = pl.program_id(0)

        @pl.when(i == 0)
        def _():
            st_ref[...] = jnp.zeros_like(st_ref)

        gg = g_ref[...]
        r, xh = _rms_stats(x_ref[...])
        e = xh * gg - t_ref[...]
        part = 0.5 * jnp.sum(jnp.mean(e * e, axis=-1, keepdims=True), axis=0, keepdims=True)
        dx, dg = _rms_bwd(e * (1.0 / D), xh, r, gg)
        dx_ref[...] = dx
        st_ref[0:1, :] += dg
        st_ref[1:2, 0:1] += part

    return pl.pallas_call(
        kern, out_shape=(jax.ShapeDtypeStruct((8, D), F32), jax.ShapeDtypeStruct((T, D), F32)), grid=(T // tm,),
        in_specs=[pl.BlockSpec((tm, D), lambda i: (i, 0)), pl.BlockSpec((1, D), lambda i: (0, 0)),
                  pl.BlockSpec((tm, D), lambda i: (i, 0))],
        out_specs=(pl.BlockSpec((8, D), lambda i: (0, 0)), pl.BlockSpec((tm, D), lambda i: (i, 0))),
        name="loss_bwd", compiler_params=_params("arbitrary"))(x, g, tgt)


def _mlp_bwd(dx2, x1, a, g, wup, wdn, comm=None):
    T, D = x1.shape
    tm = min(512, T)
    nj = D_FF // D
    tk = _dw_chunk(T)
    per = tk // tm

    def kern(dx2_ref, x1_ref, a_ref, g_ref, wup_ref, wdn_ref, dx1_ref, da_ref, ut_ref, h2t_ref, dyb_ref, dg_ref, acc_scr):
        i, j = pl.program_id(0), pl.program_id(1)

        @pl.when((i == 0) & (j == 0))
        def _():
            dg_ref[...] = jnp.zeros_like(dg_ref)

        @pl.when(j == 0)
        def _():
            dyb_ref[...] = dx2_ref[...].astype(BF16)
            acc_scr[...] = jnp.zeros_like(acc_scr)

        du = _nt(dyb_ref[...], wdn_ref[j])
        relu = jnp.maximum(a_ref[...].astype(F32), 0.0)
        da = (du * (2.0 * relu)).astype(BF16)
        da_ref[...] = da
        ut_ref[...] = jnp.square(relu).astype(BF16).T
        acc_scr[...] += _nt(da, wup_ref[j])

        @pl.when(j == nj - 1)
        def _():
            gg = g_ref[...]
            r, xh = _rms_stats(x1_ref[...])
            h2t_ref[...] = (xh * gg).astype(BF16).T
            dx, dg = _rms_bwd(acc_scr[...], xh, r, gg)
            dx1_ref[...] = dx2_ref[...] + dx
            dg_ref[...] += dg

    return _hosted_call(
        comm, kern,
        out_shape=(jax.ShapeDtypeStruct((T, D), F32), jax.ShapeDtypeStruct((T, D_FF), BF16),
                   jax.ShapeDtypeStruct((D_FF, T), BF16), jax.ShapeDtypeStruct((T // tk, D, tk), BF16),
                   jax.ShapeDtypeStruct((T, D), BF16), jax.ShapeDtypeStruct((1, D), F32)),
        grid=(T // tm, nj),
        in_specs=[pl.BlockSpec((tm, D), lambda i, j: (i, 0)), pl.BlockSpec((tm, D), lambda i, j: (i, 0)),
                  pl.BlockSpec((tm, D), lambda i, j: (i, j)), pl.BlockSpec((1, D), lambda i, j: (0, 0)),
                  _resident((nj, D, D)), _resident((nj, D, D))],
        out_specs=(pl.BlockSpec((tm, D), lambda i, j: (i, 0)), pl.BlockSpec((tm, D), lambda i, j: (i, j)),
                   pl.BlockSpec((D, tm), lambda i, j: (j, i)), pl.BlockSpec((None, D, tm), lambda i, j: (i // per, 0, i % per)),
                   pl.BlockSpec((tm, D), lambda i, j: (i, 0)), pl.BlockSpec((1, D), lambda i, j: (0, 0))),
        scratch_shapes=[pltpu.VMEM((tm, D), F32)],
        name="mlp_bwd", sem=("arbitrary", "arbitrary"), args=(dx2, x1, a, g, wup, wdn))


def _mix_bwd(dx1, ya, yc, pgate, b_gates, wao, wco, wo):
    T, D = dx1.shape
    tm = min(512, T)
    halves = _row_halves(tm)

    def kern(dx_ref, ya_ref, yc_ref, pg_ref, bg_ref, wao_ref, wco_ref, wo_ref,
             datt_ref, dcv_ref, dya_ref, dyc_ref, dgt_ref, dbg_ref):
        @pl.when(pl.program_id(0) == 0)
        def _():
            dbg_ref[...] = jnp.zeros_like(dbg_ref)

        dms = [_nt(dx_ref[rows, :].astype(BF16), wo_ref[...]) for rows in halves]
        dys = []
        for rows, dm in zip(halves, dms):
            sa = jax.nn.sigmoid(pg_ref[rows, 0:D].astype(F32) + bg_ref[:, 0:D])
            sc = jax.nn.sigmoid(pg_ref[rows, D:2 * D].astype(F32) + bg_ref[:, D:2 * D])
            dya = (dm * sa).astype(BF16)
            dyc = (dm * sc).astype(BF16)
            dga = dm * ya_ref[rows, :].astype(F32) * (sa * (1.0 - sa))
            dgc = dm * yc_ref[rows, :].astype(F32) * (sc * (1.0 - sc))
            dya_ref[rows, :] = dya
            dyc_ref[rows, :] = dyc
            dgt_ref[rows, 0:D] = dga.astype(BF16)
            dgt_ref[rows, D:2 * D] = dgc.astype(BF16)
            dbg_ref[:, 0:D] += jnp.sum(dga, axis=0, keepdims=True)
            dbg_ref[:, D:2 * D] += jnp.sum(dgc, axis=0, keepdims=True)
            dys.append((dya, dyc))
        for rows, (dya, dyc) in zip(halves, dys):
            datt_ref[rows, :] = _nt(dya, wao_ref[...]).astype(BF16)
            dcv_ref[rows, :] = _nt(dyc, wco_ref[...]).astype(BF16)

    row = lambda w: pl.BlockSpec((tm, w), lambda i: (i, 0))
    full = lambda a, b: pl.BlockSpec((a, b), lambda i: (0, 0))
    bf = jax.ShapeDtypeStruct((T, D), BF16)
    return pl.pallas_call(
        kern,
        out_shape=(bf, bf, bf, bf, jax.ShapeDtypeStruct((T, GATE_W), BF16), jax.ShapeDtypeStruct((1, GATE_W), F32)),
        grid=(T // tm,),
        in_specs=[row(D), row(D), row(D), row(GATE_W), full(1, GATE_W)] + [_resident((D, D))] * 3,
        out_specs=(row(D), row(D), row(D), row(D), row(GATE_W), full(1, GATE_W)),
        name="mix_bwd", compiler_params=_params("arbitrary"))(dx1, ya, yc, pgate, b_gates, wao, wco, wo)


def _conv_bwd(dcv, pconv, conv_w, conv_b, seq, comm=None):
    T = pconv.shape[0]
    tm = min(512, seq)
    per_seq = seq // tm
    D = D_MODEL
    nb16 = T // 16

    def kern(dcv_ref, dcvn_ref, cur_ref, prev_ref, next_ref, w_ref, b_ref, o_ref, dwb_ref):
        i = pl.program_id(0)

        @pl.when(i == 0)
        def _():
            dwb_ref[...] = jnp.zeros_like(dwb_ref)

        first_i = ((i % per_seq) == 0).astype(jnp.int32)
        keep_next = 1.0 - (((i + 1) % per_seq) == 0).astype(F32)
        cb = cur_ref[:, 0:D].astype(F32)
        cc = cur_ref[:, D:2 * D].astype(F32)
        cu = cur_ref[:, 2 * D:3 * D].astype(F32)
        yc = cc * cu
        halo_yc = prev_ref[:, D:2 * D].astype(F32) * prev_ref[:, 2 * D:3 * D].astype(F32)
        s1, s2 = _conv_taps(yc, halo_yc, first_i)
        w0, w1, w2 = w_ref[0:1, :], w_ref[1:2, :], w_ref[2:3, :]
        z = w0 * s2 + w1 * s1 + w2 * yc
        dcv = dcv_ref[...].astype(F32)
        dz = dcv * cb
        dzn = dcvn_ref[...].astype(F32) * next_ref[:, 0:D].astype(F32) * keep_next
        n1, n2 = _pick_row(dzn, 0), _pick_row(dzn, 1)
        rowid = lax.broadcasted_iota(jnp.int32, dz.shape, 0)
        u1 = jnp.where(rowid == tm - 1, n1, pltpu.roll(dz, tm - 1, 0))
        u2 = jnp.where(rowid == tm - 1, n2, jnp.where(rowid == tm - 2, n1, pltpu.roll(dz, tm - 2, 0)))
        dyc = w2 * dz + w1 * u1 + w0 * u2
        o_ref[:, 0:D] = (dcv * (z + b_ref[...])).astype(BF16)
        o_ref[:, D:2 * D] = (dyc * cu).astype(BF16)
        o_ref[:, 2 * D:3 * D] = (dyc * cc).astype(BF16)
        dwb_ref[0:1, :] += jnp.sum(dz * s2, axis=0, keepdims=True)
        dwb_ref[1:2, :] += jnp.sum(dz * s1, axis=0, keepdims=True)
        dwb_ref[2:3, :] += jnp.sum(dz * yc, axis=0, keepdims=True)
        dwb_ref[3:4, :] += jnp.sum(dz, axis=0, keepdims=True)

    prev_map = lambda i: (jnp.maximum(i * (tm // 16) - 1, 0), 0)
    next_map = lambda i: (jnp.minimum((i + 1) * (tm // 16), nb16 - 1), 0)
    return _hosted_call(
        comm, kern, out_shape=(jax.ShapeDtypeStruct((T, CONV_W), BF16), jax.ShapeDtypeStruct((8, D), F32)), grid=(T // tm,),
        in_specs=[pl.BlockSpec((tm, D), lambda i: (i, 0)), pl.BlockSpec((16, D), next_map),
                  pl.BlockSpec((tm, CONV_W), lambda i: (i, 0)), pl.BlockSpec((16, CONV_W), prev_map),
                  pl.BlockSpec((16, CONV_W), next_map),
                  pl.BlockSpec((3, D), lambda i: (0, 0)), pl.BlockSpec((1, D), lambda i: (0, 0))],
        out_specs=(pl.BlockSpec((tm, CONV_W), lambda i: (i, 0)), pl.BlockSpec((8, D), lambda i: (0, 0))),
        name="conv_bwd", sem=("arbitrary",), args=(dcv, dcv, pconv, pconv, pconv, conv_w, conv_b))


def _attn_bwd(pqkv, datt, lse, sinks, bias, seq, comm=None):
    T = pqkv.shape[0]
    nblk = seq // WINDOW
    nseq = T // seq
    KVW = KV_W

    def kern(sink_ref, cur_ref, prev_ref, do_ref, lse_ref, bias_ref, dq_ref, dkv_ref, ds_ref, kc_scr, vc_scr, dqt_scr):
        b, st = pl.program_id(0), pl.program_id(1)

        @pl.when((b == 0) & (st == 0))
        def _():
            ds_ref[...] = jnp.zeros_like(ds_ref)

        @pl.when(st == 0)
        def _():
            kc_scr[...] = jnp.zeros_like(kc_scr)
            vc_scr[...] = jnp.zeros_like(vc_scr)

        @pl.when(st < nblk)
        def _():
            first_i = (st == 0).astype(jnp.int32)
            groups = range(N_KV_HEADS)
            bands = [_kv_band(cur_ref, prev_ref, hk) for hk in groups]
            qs = [_stack_heads(cur_ref, hk) for hk in groups]
            dos = [_stack_heads(do_ref, hk) for hk in groups]
            sts = [_nt(bands[hk][0], qs[hk] * ATTN_SCALE) + bias_ref[first_i, hk] for hk in groups]
            dps = [_nt(bands[hk][1], dos[hk]) for hk in groups]
            pbs, dsss = [], []
            for hk in groups:
                heads = [GQA_GROUP * hk + g for g in range(GQA_GROUP)]
                sink = _lane_row([sink_ref[h] for h in heads])
                lse_g = jnp.concatenate([lse_ref[h:h + 1, :] for h in heads], axis=1)
                p = jnp.exp(sts[hk] - lse_g)
                d_row = jnp.sum(p * dps[hk], axis=0, keepdims=True)
                dsss.append((p * (dps[hk] - d_row) * ATTN_SCALE).astype(BF16))
                pbs.append(p.astype(BF16))
                psd = jnp.exp(sink - lse_g) * d_row
                for g, h in enumerate(heads):
                    ds_ref[0:1, h:h + 1] -= jnp.sum(psd[:, WINDOW * g:WINDOW * (g + 1)], axis=1, keepdims=True)
            for hk in groups:
                dqt = _tn(bands[hk][0], dsss[hk])
                dk_b = _nn(dsss[hk], qs[hk])
                dv_b = _nn(pbs[hk], dos[hk])
                for g in range(GQA_GROUP):
                    h = GQA_GROUP * hk + g
                    dqt_scr[HEAD_DIM * h:HEAD_DIM * (h + 1), :] = dqt[:, WINDOW * g:WINDOW * (g + 1)].astype(BF16)
                ksl = slice(HEAD_DIM * hk, HEAD_DIM * (hk + 1))
                vsl = slice(KVW + HEAD_DIM * hk, KVW + HEAD_DIM * (hk + 1))
                dkv_ref[:, ksl] = (kc_scr[:, ksl] + dk_b[0:WINDOW]).astype(BF16)
                dkv_ref[:, vsl] = (vc_scr[:, ksl] + dv_b[0:WINDOW]).astype(BF16)
                kc_scr[:, ksl] = dk_b[WINDOW:2 * WINDOW]
                vc_scr[:, ksl] = dv_b[WINDOW:2 * WINDOW]
            dq_ref[...] = dqt_scr[...].T

        @pl.when(st == nblk)
        def _():
            dkv_ref[:, 0:KVW] = kc_scr[...].astype(BF16)
            dkv_ref[:, KVW:2 * KVW] = vc_scr[...].astype(BF16)

    cur_map = lambda b, s: (b * nblk + jnp.minimum(s, nblk - 1), 0)
    prev_row = lambda b, s: b * nblk + jnp.clip(s - 1, 0, nblk - 1)
    return _hosted_call(
        comm, kern,
        out_shape=(jax.ShapeDtypeStruct((T, D_MODEL), BF16), jax.ShapeDtypeStruct((T, 2 * KVW), BF16),
                   jax.ShapeDtypeStruct((8, 128), F32)),
        grid=(nseq, nblk + 1),
        in_specs=[pl.BlockSpec(memory_space=pltpu.SMEM),
                  pl.BlockSpec((WINDOW, QKV_W), cur_map),
                  pl.BlockSpec((WINDOW, 2 * KVW), lambda b, s: (prev_row(b, s), 2)),
                  pl.BlockSpec((WINDOW, D_MODEL), cur_map),
                  pl.BlockSpec((N_Q_HEADS, WINDOW), lambda b, s: (0, b * nblk + jnp.minimum(s, nblk - 1))),
                  _resident(bias.shape)],
        out_specs=(pl.BlockSpec((WINDOW, D_MODEL), cur_map),
                   pl.BlockSpec((WINDOW, 2 * KVW), lambda b, s: (prev_row(b, s), 0)),
                   pl.BlockSpec((8, 128), lambda b, s: (0, 0))),
        scratch_shapes=[pltpu.VMEM((WINDOW, KVW), F32), pltpu.VMEM((WINDOW, KVW), F32), pltpu.VMEM((D_MODEL, WINDOW), BF16)],
        name="attn_bwd", sem=("arbitrary", "arbitrary"), args=(sinks, pqkv, pqkv, datt, lse, bias))


def _piece_tiles(pieces):
    out, start = [], 0
    for arr, width in pieces:
        out.append((arr, start, width // COL_TILE))
        start += width // COL_TILE
    return out, start


def _inproj_bwd(pieces, w_in, x, dx_in, g, comm=None):
    T, D = x.shape
    tm = min(512, T)

    def kern(*refs):
        p_refs = refs[:len(pieces)]
        w_ref, x_ref, dxin_ref, g_ref, dx_ref, dg_ref = refs[len(pieces):]

        @pl.when(pl.program_id(0) == 0)
        def _():
            dg_ref[...] = jnp.zeros_like(dg_ref)

        dh, off = None, 0
        for p_ref, (_, width) in zip(p_refs, pieces):
            part = _nt(p_ref[...], w_ref[:, off:off + width])
            dh = part if dh is None else dh + part
            off += width
        gg = g_ref[...]
        r, xh = _rms_stats(x_ref[...])
        dx, dg = _rms_bwd(dh, xh, r, gg)
        dx_ref[...] = dxin_ref[...] + dx
        dg_ref[...] += dg

    row = lambda wd: pl.BlockSpec((tm, wd), lambda i: (i, 0))
    return _hosted_call(
        comm, kern, out_shape=(jax.ShapeDtypeStruct((T, D), F32), jax.ShapeDtypeStruct((1, D), F32)), grid=(T // tm,),
        in_specs=[row(wd) for _, wd in pieces] + [_resident((D, IN_COLS)), row(D), row(D), pl.BlockSpec((1, D), lambda i: (0, 0))],
        out_specs=(row(D), pl.BlockSpec((1, D), lambda i: (0, 0))),
        name="inproj_bwd", sem=("arbitrary",), args=(*[a for a, _ in pieces], w_in, x, dx_in, g))


def _dw_pieces(lhs_t, pieces):
    nt, K, tk = lhs_t.shape
    tiles, nj = _piece_tiles(pieces)

    def kern(*refs):
        lhs_ref = refs[0]
        p_refs = refs[1:1 + len(tiles)]
        o_ref, ob_ref = refs[1 + len(tiles):]
        j, t = pl.program_id(0), pl.program_id(1)

        @pl.when(t == 0)
        def _():
            o_ref[...] = jnp.zeros_like(o_ref)

        for p_ref, (_, start, n) in zip(p_refs, tiles):
            @pl.when((j >= start) & (j < start + n))
            def _(p_ref=p_ref):
                o_ref[...] += _nn(lhs_ref[t], p_ref[...])

        @pl.when(t == nt - 1)
        def _():
            ob_ref[...] = o_ref[...].astype(BF16)

    def p_map(start, n):
        return lambda j, t: (jnp.where((j >= start) & (j < start + n), t, 0), jnp.clip(j - start, 0, n - 1))

    N = nj * COL_TILE
    return pl.pallas_call(
        kern, out_shape=(jax.ShapeDtypeStruct((K, N), F32), jax.ShapeDtypeStruct((K, N), BF16)), grid=(nj, nt),
        in_specs=[_resident((nt, K, tk))] + [pl.BlockSpec((tk, COL_TILE), p_map(s, n)) for _, s, n in tiles],
        out_specs=(pl.BlockSpec((K, COL_TILE), lambda j, t: (0, j)), pl.BlockSpec((K, COL_TILE), lambda j, t: (0, j))),
        name="dw_pieces", compiler_params=_params("arbitrary", "arbitrary"))(lhs_t, *[a for a, _, _ in tiles])


def _dw(lhs_t, rhs):
    lhs_res, rhs_res = lhs_t.ndim == 3, rhs.ndim == 3
    W = D_MODEL
    if lhs_res:
        nt, K, tk = lhs_t.shape
    else:
        K, tk = lhs_t.shape[0], _dw_chunk(lhs_t.shape[1])
        nt = lhs_t.shape[1] // tk
    N = rhs.shape[-1]

    def kern(lhs_ref, rhs_ref, o_ref, ob_ref):
        t = pl.program_id(2)

        @pl.when(t == 0)
        def _():
            o_ref[...] = jnp.zeros_like(o_ref)

        a = lhs_ref[t] if lhs_res else lhs_ref[...]
        b = rhs_ref[t] if rhs_res else rhs_ref[...]
        o_ref[...] += _nn(a, b.astype(BF16))

        @pl.when(t == nt - 1)
        def _():
            ob_ref[...] = o_ref[...].astype(BF16)

    omap = lambda i, j, t: (i, j)
    lspec = _resident((nt, W, tk)) if lhs_res else pl.BlockSpec((W, tk), lambda i, j, t: (i, t))
    rspec = _resident((nt, tk, W)) if rhs_res else pl.BlockSpec((tk, W), lambda i, j, t: (t, j))
    return pl.pallas_call(
        kern, out_shape=(jax.ShapeDtypeStruct((K, N), F32), jax.ShapeDtypeStruct((K, N), BF16)), grid=(K // W, N // W, nt),
        in_specs=[lspec, rspec], out_specs=(pl.BlockSpec((W, W), omap), pl.BlockSpec((W, W), omap)),
        name="dw", compiler_params=_params("arbitrary", "arbitrary", "arbitrary"))(lhs_t, rhs)


BIG = (("w_in", D_MODEL, IN_COLS, "col"), ("w_attn_out", D_MODEL, D_MODEL, "row"), ("w_conv_out", D_MODEL, D_MODEL, "row"),
       ("w_o", D_MODEL, D_MODEL, "row"), ("w_up", D_MODEL, D_FF, "col"), ("w_down", D_FF, D_MODEL, "row"))


def _shard_dims(rows, cols, kind):
    return (rows, cols // N_CHIP) if kind in ("col", "chip") else (rows // N_CHIP, cols)


def _window(ref, rows, cols, kind, chip, half):
    sr, sc = _shard_dims(rows, cols, kind)
    hr = sr // 2
    if kind == "col":
        return ref.at[pl.ds(half * hr, hr), pl.ds(chip * sc, sc)]
    if kind == "chip":
        return ref.at[chip, pl.ds(half * hr, hr), :]
    return ref.at[pl.ds(chip * sr + half * hr, hr), :]


def _mesh_pos():
    x, y, c = lax.axis_index("x"), lax.axis_index("y"), lax.axis_index("c")
    return x, y, c, 2 * x + y


_REL_BITS = (2, 1, 3)


def _rel_dev(x, y, c, r):
    return ((1 - x, y, c), (x, 1 - y, c), (1 - x, 1 - y, c))[r]


def _for_my_chip(j, fn):
    for js in range(N_CHIP):
        pl.when(j == js)(functools.partial(fn, js))


def _cast_into_full(j_arr, shard, l, rows, cols, kind):
    sr, sc = _shard_dims(rows, cols, kind)
    tr = min(256, sr)

    def kern(j_ref, s_ref, o_ref):
        o_ref[...] = s_ref[...].astype(BF16)

    shape, block = (rows, cols), (tr, sc)
    if kind == "col":
        omap = lambda i, j_ref: (i, j_ref[0])
    elif kind == "chip":
        shape, block = (N_CHIP, rows, sc), (None, tr, sc)
        omap = lambda i, j_ref: (j_ref[0], i, 0)
    else:
        omap = lambda i, j_ref: (j_ref[0] * (sr // tr) + i, 0)
    gs = pltpu.PrefetchScalarGridSpec(
        num_scalar_prefetch=1, grid=(sr // tr,),
        in_specs=[pl.BlockSpec((None, tr, sc), lambda i, j_ref: (l, i, 0))], out_specs=pl.BlockSpec(block, omap))
    return pl.pallas_call(kern, out_shape=jax.ShapeDtypeStruct(shape, BF16), grid_spec=gs, name="cast_into_full",
                          compiler_params=_params("arbitrary"))(j_arr, shard)


def _gather_comm(fulls, cw=None):
    n_big = len(fulls)
    n_piece = n_big + (0 if cw is None else 1)

    def pieces(in_refs, o_refs, js, c):
        def piece(p, chip, half):
            if p == n_big:
                return o_refs[p].at[half, chip]
            _, rows, cols, kind = fulls[p]
            return _window(o_refs[p], rows, cols, kind, chip, half)

        def mine(p):
            return in_refs[p].at[c] if p == n_big else piece(p, js, c)

        return piece, mine

    def local_copies(in_refs, piece, js, loc_sem):
        if cw is None:
            return []
        return [pltpu.make_async_copy(in_refs[n_big].at[half], piece(n_big, js, half), loc_sem.at[half]) for half in range(2)]

    def ici_copy(piece, mine, js, x, y, c, r, p, send_sem, recv_sem):
        return pltpu.make_async_remote_copy(mine(p), piece(p, js, c), send_sem.at[r * n_piece + p], recv_sem.at[r * n_piece + p],
                                            _rel_dev(x, y, c, r), MESH)

    def start(in_refs, o_refs, sems):
        send_sem, recv_sem, _, _, loc_sem = sems
        x, y, c, j = _mesh_pos()

        def run(js):
            piece, mine = pieces(in_refs, o_refs, js, c)
            for cp in local_copies(in_refs, piece, js, loc_sem):
                cp.start()
            for r in range(3):
                for p in range(n_piece):
                    ici_copy(piece, mine, js, x, y, c, r, p, send_sem, recv_sem).start()

        _for_my_chip(j, run)

    def finish(in_refs, o_refs, sems):
        send_sem, recv_sem, fsend_sem, frecv_sem, loc_sem = sems
        x, y, c, j = _mesh_pos()

        def run(js):
            piece, mine = pieces(in_refs, o_refs, js, c)
            fwds = []
            for r in range(3):
                ks = js ^ _REL_BITS[r]
                for p in range(n_piece):
                    got = piece(p, ks, c)
                    pltpu.make_async_remote_copy(got, got, send_sem.at[r * n_piece + p], recv_sem.at[r * n_piece + p],
                                                 _rel_dev(x, y, c, r), MESH).wait_recv()
                    cp = pltpu.make_async_remote_copy(got, got, fsend_sem.at[r * n_piece + p], frecv_sem.at[r * n_piece + p],
                                                      (x, y, 1 - c), MESH)
                    cp.start()
                    fwds.append(cp)
            for r in range(3):
                ks = js ^ _REL_BITS[r]
                for p in range(n_piece):
                    got = piece(p, ks, 1 - c)
                    pltpu.make_async_remote_copy(got, got, fsend_sem.at[r * n_piece + p], frecv_sem.at[r * n_piece + p],
                                                 (x, y, 1 - c), MESH).wait_recv()
            for r in range(3):
                for p in range(n_piece):
                    ici_copy(piece, mine, js, x, y, c, r, p, send_sem, recv_sem).wait_send()
            for cp in fwds:
                cp.wait_send()
            for cp in local_copies(in_refs, piece, js, loc_sem):
                cp.wait()

        _for_my_chip(j, run)

    out_shape = [jax.ShapeDtypeStruct(a.shape, BF16) for a, _, _, _ in fulls]
    ins = [a for a, _, _, _ in fulls]
    if cw is not None:
        out_shape.append(jax.ShapeDtypeStruct((DEPTH, N_CHIP, 3, D_MODEL // N_CHIP), F32))
        ins.append(cw)
    scratch = [pltpu.SemaphoreType.DMA((3 * n_piece,))] * 4 + [pltpu.SemaphoreType.DMA((2,))]
    return _Hosted(ins, out_shape, {p: p for p in range(n_big)}, scratch, start, finish)


def _compose(*comms):
    comms = [cm for cm in comms if cm is not None]
    if len(comms) <= 1:
        return comms[0] if comms else None
    ins, outs, aliases, scratch, cuts = [], [], {}, [], []
    for cm in comms:
        cuts.append((len(ins), len(outs), len(scratch)))
        aliases.update({len(ins) + i: len(outs) + o for i, o in cm.aliases.items()})
        ins, outs, scratch = ins + cm.inputs, outs + cm.out_shape, scratch + cm.scratch

    def parts(a, b, s):
        for cm, (i0, o0, s0) in zip(comms, cuts):
            yield cm, a[i0:i0 + len(cm.inputs)], b[o0:o0 + len(cm.out_shape)], s[s0:s0 + len(cm.scratch)]

    def start(a, b, s):
        for cm, pa, pb, ps in parts(a, b, s):
            cm.start(pa, pb, ps)

    def finish(a, b, s):
        for cm, pa, pb, ps in parts(a, b, s):
            cm.finish(pa, pb, ps)

    return _Hosted(ins, outs, aliases, scratch, start, finish)


def _run_comm(comm, name):
    n_in, n_out = len(comm.inputs), len(comm.out_shape)

    def body(*refs):
        comm.start(refs[:n_in], refs[n_in:n_in + n_out], refs[n_in + n_out:])
        comm.finish(refs[:n_in], refs[n_in:n_in + n_out], refs[n_in + n_out:])

    return pl.pallas_call(body, out_shape=comm.out_shape, in_specs=[ANY] * n_in, out_specs=[ANY] * n_out,
                          input_output_aliases=comm.aliases, scratch_shapes=comm.scratch, name=name)(*comm.inputs)


def _sibling_exchange_comm(gb):
    n = len(gb)

    def copies(g_refs, o_refs, sems, c):
        send_sem, recv_sem = sems
        x, y, _, _ = _mesh_pos()
        return [pltpu.make_async_remote_copy(_window(g_refs[t], rows, cols, kind, chip, 1 - c),
                                             _window(o_refs[t], rows, cols, kind, chip, 1 - c),
                                             send_sem.at[N_CHIP * t + chip], recv_sem.at[N_CHIP * t + chip], (x, y, 1 - c), MESH)
                for t, (_, rows, cols, kind) in enumerate(gb) for chip in range(N_CHIP)]

    def start(g_refs, o_refs, sems):
        for cp in copies(g_refs, o_refs, sems, lax.axis_index("c")):
            cp.start()

    def finish(g_refs, o_refs, sems):
        c = lax.axis_index("c")
        for cp in copies(g_refs, o_refs, sems, 1 - c):
            cp.wait_recv()
        for cp in copies(g_refs, o_refs, sems, c):
            cp.wait_send()

    return _Hosted([a for a, _, _, _ in gb], [jax.ShapeDtypeStruct(a.shape, BF16) for a, _, _, _ in gb], {},
                   [pltpu.SemaphoreType.DMA((N_CHIP * n,))] * 2, start, finish)


def _half_add(jc_arr, g, sib, rows, cols, kind):
    sr, sc = _shard_dims(rows, cols, kind)
    hr = sr // 2

    def kern(jc_ref, g_ref, s_ref, ob_ref, of_ref):
        v = g_ref[...] + s_ref[...].astype(F32)
        ob_ref[...] = v.astype(BF16)

        @pl.when(pl.program_id(0) == jc_ref[0])
        def _():
            of_ref[...] = v

    if kind == "col":
        imap = lambda j, jc_ref: (jc_ref[1], j)
    else:
        imap = lambda j, jc_ref: (2 * j + jc_ref[1], 0)
    gs = pltpu.PrefetchScalarGridSpec(
        num_scalar_prefetch=1, grid=(N_CHIP,),
        in_specs=[pl.BlockSpec((hr, sc), imap), pl.BlockSpec((hr, sc), imap)],
        out_specs=[pl.BlockSpec((None, hr, sc), lambda j, jc_ref: (j, 0, 0)), pl.BlockSpec((hr, sc), lambda j, jc_ref: (0, 0))])
    return pl.pallas_call(
        kern, out_shape=(jax.ShapeDtypeStruct((N_CHIP, hr, sc), BF16), jax.ShapeDtypeStruct((hr, sc), F32)),
        grid_spec=gs, name="grad_half_add", compiler_params=_params("arbitrary"))(jc_arr, g, sib)


def _chip_exchange_comm(sbs):
    n = len(sbs)

    def copies(s_refs, o_refs, sems):
        send_sem, recv_sem = sems
        x, y, c, j = _mesh_pos()
        return [pltpu.make_async_remote_copy(s_refs[t].at[j ^ _REL_BITS[r]], o_refs[t].at[r], send_sem.at[n * r + t],
                                             recv_sem.at[n * r + t], _rel_dev(x, y, c, r), MESH)
                for r in range(3) for t in range(n)]

    def start(s_refs, o_refs, sems):
        for cp in copies(s_refs, o_refs, sems):
            cp.start()

    def finish(s_refs, o_refs, sems):
        for cp in copies(s_refs, o_refs, sems):
            cp.wait()

    return _Hosted(sbs, [jax.ShapeDtypeStruct((3,) + a.shape[1:], BF16) for a in sbs], {},
                   [pltpu.SemaphoreType.DMA((3 * n,))] * 2, start, finish)


def _owner_sum(jc_arr, sf, rb, l, into=None):
    hr, sc = sf.shape

    def kern(jc_ref, s_ref, r0_ref, r1_ref, r2_ref, *rest):
        o_ref = rest[-1]
        o_ref[...] = ((s_ref[...] + r0_ref[...].astype(F32)) + r1_ref[...].astype(F32)) + r2_ref[...].astype(F32)

    in_specs = [pl.BlockSpec((hr, sc), lambda i, jc_ref: (0, 0))]
    in_specs += [pl.BlockSpec((None, hr, sc), lambda i, jc_ref, r=r: (r, 0, 0)) for r in range(3)]
    args = [jc_arr, sf, rb, rb, rb]
    aliases = {}
    if into is not None:
        in_specs.append(ANY)
        args.append(into)
        aliases = {len(args) - 1: 0}
    gs = pltpu.PrefetchScalarGridSpec(
        num_scalar_prefetch=1, grid=(1,), in_specs=in_specs,
        out_specs=pl.BlockSpec((None, hr, sc), lambda i, jc_ref: (l, jc_ref[1], 0)))
    return pl.pallas_call(kern, out_shape=jax.ShapeDtypeStruct((DEPTH, 2 * hr, sc), F32), grid_spec=gs,
                          input_output_aliases=aliases, name="grad_owner_sum", compiler_params=_params("arbitrary"))(*args)


def _sibling_assemble_comm(grads, layers):
    n = len(grads)
    todo = [(q, l) for q in range(n) for l in layers[q]]

    def copies(o_refs, sems, half):
        send_sem, recv_sem = sems
        x, y, c, _ = _mesh_pos()
        out = []
        for k, (q, l) in enumerate(todo):
            hr = grads[q].shape[1] // 2
            w = o_refs[q].at[l, pl.ds(half * hr, hr), :]
            out.append(pltpu.make_async_remote_copy(w, w, send_sem.at[k], recv_sem.at[k], (x, y, 1 - c), MESH))
        return out

    def start(_, o_refs, sems):
        for cp in copies(o_refs, sems, lax.axis_index("c")):
            cp.start()

    def finish(_, o_refs, sems):
        c = lax.axis_index("c")
        for cp in copies(o_refs, sems, 1 - c):
            cp.wait_recv()
        for cp in copies(o_refs, sems, c):
            cp.wait_send()

    return _Hosted(grads, [jax.ShapeDtypeStruct(g.shape, F32) for g in grads], {q: q for q in range(n)},
                   [pltpu.SemaphoreType.DMA((len(todo),))] * 2, start, finish)


def _adamw_math(w, g, m, v):
    m = ADAM_B1 * m + (1.0 - ADAM_B1) * g
    v = ADAM_B2 * v + (1.0 - ADAM_B2) * jnp.square(g)
    m_hat = m / (1.0 - ADAM_B1 ** ADAM_STEP)
    v_hat = v / (1.0 - ADAM_B2 ** ADAM_STEP)
    delta = -ADAM_LR * (m_hat / (jnp.sqrt(v_hat) + ADAM_EPS) + ADAM_WD * w)
    return delta, m, v


def _adamw(w, g, m, v):
    shape = w.shape
    C = shape[-1]
    R = int(np.prod(shape[:-1]))
    tr = min(256, R)
    args = [a.reshape(R, C) for a in (w, g, m, v)]

    def kern(w_ref, g_ref, m_ref, v_ref, go_ref, d_ref, nm_ref, nv_ref):
        g_val = g_ref[...]
        d, nm, nv = _adamw_math(w_ref[...], g_val, m_ref[...], v_ref[...])
        go_ref[...] = g_val
        d_ref[...] = d
        nm_ref[...] = nm
        nv_ref[...] = nv

    spec = pl.BlockSpec((tr, C), lambda i: (i, 0))
    outs = pl.pallas_call(
        kern, out_shape=[jax.ShapeDtypeStruct((R, C), F32)] * 4, grid=(R // tr,), in_specs=[spec] * 4, out_specs=[spec] * 4,
        name="adamw", compiler_params=_params("parallel"))(*args)
    return [o.reshape(shape) for o in outs]


_ROW_G_MIX, _ROW_B_GATES, _ROW_SINKS, _ROW_CONV, _ROW_G_MLP, _ROW_G_FINAL, _ROW_LOSS = 0, 2, 6, 8, 16, 18, 19


def _small_step(parts, params, moms, vels):
    names = ["g_mix", "b_gates", "sinks", "conv_w", "conv_b", "g_mlp", "g_final"]
    D = D_MODEL
    QW = D // N_CHIP
    n_dev = 8

    def body(*refs):
        it = iter(refs)
        dgmix = [next(it) for _ in range(DEPTH)]
        dbg = [next(it) for _ in range(DEPTH)]
        dsk = [next(it) for _ in range(DEPTH)]
        dwb = [next(it) for _ in range(DEPTH)]
        dgmlp = [next(it) for _ in range(DEPTH)]
        lst = next(it)
        p_refs = {n: next(it) for n in names}
        m_refs = {n: next(it) for n in names}
        v_refs = {n: next(it) for n in names}
        loss_ref = next(it)
        outs = {n: [next(it) for _ in range(4)] for n in names}
        pack_ref, all_ref, send_sem, recv_sem = next(it), next(it), next(it), next(it)

        x, y, c, j = _mesh_pos()
        me = 4 * x + 2 * y + c
        pack_ref[...] = jnp.zeros_like(pack_ref)
        for l in range(DEPTH):
            pack_ref[_ROW_G_MIX + l:_ROW_G_MIX + l + 1, :] = dgmix[l][...]
            pack_ref[_ROW_B_GATES + 2 * l:_ROW_B_GATES + 2 * l + 1, :] = dbg[l][:, 0:D]
            pack_ref[_ROW_B_GATES + 2 * l + 1:_ROW_B_GATES + 2 * l + 2, :] = dbg[l][:, D:2 * D]
            pack_ref[_ROW_SINKS + l:_ROW_SINKS + l + 1, 0:128] = dsk[l][0:1, :]
            pack_ref[_ROW_CONV + 4 * l:_ROW_CONV + 4 * l + 4, :] = dwb[l][0:4, :]
            pack_ref[_ROW_G_MLP + l:_ROW_G_MLP + l + 1, :] = dgmlp[l][...]
        pack_ref[_ROW_G_FINAL:_ROW_G_FINAL + 1, :] = lst[0:1, :]
        pack_ref[_ROW_LOSS:_ROW_LOSS + 1, :] = lst[1:2, :]

        all_ref[me] = pack_ref[...]
        cps = []
        for k in range(1, n_dev):
            dx_, dy_, dc_ = (k >> 2) & 1, (k >> 1) & 1, k & 1
            peer = (x ^ dx_, y ^ dy_, c ^ dc_)
            cp = pltpu.make_async_remote_copy(pack_ref, all_ref.at[me], send_sem.at[k - 1], recv_sem.at[k - 1], peer, MESH)
            cp.start()
            cps.append(cp)
        for cp in cps:
            cp.wait()

        tot = all_ref[0]
        for d in range(1, n_dev):
            tot = tot + all_ref[d]
        pack_ref[...] = tot

        loss_ref[...] = pack_ref[_ROW_LOSS:_ROW_LOSS + 1, 0:1]

        def finish(name, idx, g):
            w, m, v = p_refs[name][idx], m_refs[name][idx], v_refs[name][idx]
            d, nm, nv = _adamw_math(w, g, m, v)
            for ref, val in zip(outs[name], (g, d, nm, nv)):
                ref[idx] = val

        for l in range(DEPTH):
            finish("g_mix", (slice(l, l + 1), slice(None)), pack_ref[_ROW_G_MIX + l:_ROW_G_MIX + l + 1, :])
            finish("g_mlp", (slice(l, l + 1), slice(None)), pack_ref[_ROW_G_MLP + l:_ROW_G_MLP + l + 1, :])
            finish("conv_b", (slice(l, l + 1), slice(None)), pack_ref[_ROW_CONV + 4 * l + 3:_ROW_CONV + 4 * l + 4, :])
            finish("sinks", (slice(l, l + 1), slice(None)), pack_ref[_ROW_SINKS + l:_ROW_SINKS + l + 1, 0:N_Q_HEADS])
            for hf in range(2):
                finish("b_gates", (slice(l, l + 1), slice(hf * D, (hf + 1) * D)),
                       pack_ref[_ROW_B_GATES + 2 * l + hf:_ROW_B_GATES + 2 * l + hf + 1, :])
        finish("g_final", (slice(0, 1), slice(None)), pack_ref[_ROW_G_FINAL:_ROW_G_FINAL + 1, :])

        def conv_w_chip(js):
            for l in range(DEPTH):
                for k in range(3):
                    row = _ROW_CONV + 4 * l + k
                    finish("conv_w", (l, slice(k, k + 1), slice(None)), pack_ref[row:row + 1, js * QW:(js + 1) * QW])

        _for_my_chip(j, conv_w_chip)

    vm = pl.BlockSpec(memory_space=pltpu.VMEM)
    ins = (parts["g_mix"] + parts["b_gates"] + parts["sinks"] + parts["conv"] + parts["g_mlp"] + [parts["loss"]]
           + [params[n] for n in names] + [moms[n] for n in names] + [vels[n] for n in names])
    out_shape = [jax.ShapeDtypeStruct((1, 1), F32)]
    for n in names:
        out_shape += [jax.ShapeDtypeStruct(params[n].shape, F32)] * 4
    res = pl.pallas_call(
        body, out_shape=out_shape, in_specs=[vm] * len(ins), out_specs=[vm] * len(out_shape),
        scratch_shapes=[pltpu.VMEM((SMALL_ROWS, D), F32), pltpu.VMEM((n_dev, SMALL_ROWS, D), F32),
                        pltpu.SemaphoreType.DMA((n_dev - 1,)), pltpu.SemaphoreType.DMA((n_dev - 1,))],
        name="small_allreduce_adamw")(*ins)
    loss = res[0]
    out = {n: res[1 + 4 * i:5 + 4 * i] for i, n in enumerate(names)}
    return loss, out


def kernel(x, g_mix, w_in, b_gates, sinks, w_attn_out, conv_w, conv_b, w_conv_out, w_o, g_mlp, w_up, w_down, g_final, loss_target, m_g_mix, m_w_in, m_b_gates, m_sinks, m_w_attn_out, m_conv_w, m_conv_b, m_w_conv_out, m_w_o, m_g_mlp, m_w_up, m_w_down, m_g_final, v_g_mix, v_w_in, v_b_gates, v_sinks, v_w_attn_out, v_conv_w, v_conv_b, v_w_conv_out, v_w_o, v_g_mlp, v_w_up, v_w_down, v_g_final):
    B, S, D = x.shape
    T = B * S
    big_w = dict(w_in=w_in, w_attn_out=w_attn_out, w_conv_out=w_conv_out, w_o=w_o, w_up=w_up, w_down=w_down)
    big_m = dict(w_in=m_w_in, w_attn_out=m_w_attn_out, w_conv_out=m_w_conv_out, w_o=m_w_o, w_up=m_w_up, w_down=m_w_down)
    big_v = dict(w_in=v_w_in, w_attn_out=v_w_attn_out, w_conv_out=v_w_conv_out, w_o=v_w_o, w_up=v_w_up, w_down=v_w_down)

    c_arr = lax.axis_index("c").astype(jnp.int32).reshape(1)
    j_arr = (2 * lax.axis_index("x") + lax.axis_index("y")).astype(jnp.int32).reshape(1)
    jc_arr = jnp.concatenate([j_arr, c_arr])
    order = [(n, l) for n, _, _, _ in BIG for l in range(DEPTH)]
    dims = {n: (r, c_, k) for n, r, c_, k in BIG}

    wdims = dict(dims, w_up=(D, D_FF, "chip"))
    full = {(n, l): _cast_into_full(j_arr, big_w[n], l, *wdims[n]) for n, l in order}
    mixers = ("w_attn_out", "w_conv_out", "w_o")

    def gather_of(keys):
        return _gather_comm([(full[k],) + wdims[k[0]] for k in keys])

    carried = {("proj", 0): [(n, 0) for n in mixers], ("attn", 0): [("w_up", 0), ("w_down", 0)],
               ("mlp", 0): [("w_in", 1)] + [(n, 1) for n in mixers], ("attn", 1): [("w_up", 1), ("w_down", 1)]}

    def carry(fn, where, *args):
        keys = carried.get(where)
        if keys is None:
            return fn(*args)
        res, got = fn(*args, comm=gather_of(keys))
        full.update(zip(keys, got))
        return res

    first = _run_comm(_gather_comm([(full["w_in", 0],) + dims["w_in"]], conv_w), "gather_weights")
    full["w_in", 0] = first[0]
    conv_w_full = jnp.transpose(first[1], (0, 2, 1, 3)).reshape(DEPTH, 3, D)
    attn_bias = _attn_bias_table()

    xs = [x.reshape(T, D)]
    saved = []
    for l in range(DEPTH):
        ht, pqkv, pconv, pgate = carry(_norm_proj, ("proj", l), xs[-1], g_mix[l:l + 1], full["w_in", l])
        att, lse = carry(_attn_fwd, ("attn", l), pqkv, sinks[l], attn_bias, S)
        cv, cv_t = _conv_fwd(pconv, conv_w_full[l], conv_b[l:l + 1], S)
        x1, ya, yc, mg_t, att_t = _mix_fwd(xs[-1], att, cv, pgate, b_gates[l:l + 1], full["w_attn_out", l],
                                           full["w_conv_out", l], full["w_o", l])
        x2, a = carry(_mlp_fwd, ("mlp", l), x1, g_mlp[l:l + 1], full["w_up", l], full["w_down", l].reshape(N_CHIP, D, D))
        saved.append(dict(ht=ht, pqkv=pqkv, pconv=pconv, pgate=pgate, att=att, att_t=att_t, lse=lse, cv_t=cv_t, x1=x1, ya=ya,
                          yc=yc, mg_t=mg_t, a=a))
        xs.append(x2)

    loss_stats, dx = _loss_bwd(xs[-1], g_final.reshape(1, D), loss_target.reshape(T, D))

    parts = dict(g_mix=[None] * DEPTH, b_gates=[None] * DEPTH, sinks=[None] * DEPTH, conv=[None] * DEPTH,
                 g_mlp=[None] * DEPTH, loss=loss_stats)
    gf, gb, pre, got, mine = {}, {}, {}, {}, {}

    def sibling_exchange(keys):
        return _sibling_exchange_comm([(gb[k],) + dims[k[0]] for k in keys])

    def half_adds(keys, sib):
        for k, s in zip(keys, sib):
            pre[k] = _half_add(jc_arr, gf[k], s, *dims[k[0]])

    def chip_exchange(keys):
        return _chip_exchange_comm([pre[k][0] for k in keys])

    def owner_sums(keys):
        for n, l in keys:
            mine[n] = _owner_sum(jc_arr, pre[n, l][1], got[n, l], l, mine.get(n))

    def run(fn, comms, *args):
        if not comms:
            return fn(*args), []
        res, arrived = fn(*args, comm=_compose(*comms))
        outs, pos = [], 0
        for cm in comms:
            outs.append(arrived[pos:pos + len(cm.out_shape)])
            pos += len(cm.out_shape)
        return res, outs

    assert DEPTH == 2
    upper = [(n, 1) for n, _, _, _ in BIG]
    early = [(n, 0) for n in mixers + ("w_up", "w_down")]
    for l in reversed(range(DEPTH)):
        W = {n: full[(n, l)] for n in big_w}
        sv = saved[l]
        last = l == 0
        mlp_args = (dx, sv["x1"], sv["a"], g_mlp[l:l + 1], W["w_up"], W["w_down"].reshape(N_CHIP, D, D))
        (dx1, da, u_t, h2_t, dyb, parts["g_mlp"][l]), arrived = run(_mlp_bwd, [chip_exchange(upper)] if last else [], *mlp_args)
        if last:
            got.update(zip(upper, arrived[0]))
        gf["w_up", l], gb["w_up", l] = _dw(h2_t, da)
        gf["w_down", l], gb["w_down", l] = _dw(u_t, dyb.reshape(-1, _dw_chunk(T), D))
        datt, dcv, dya, dyc, dgate, parts["b_gates"][l] = _mix_bwd(
            dx1, sv["ya"], sv["yc"], sv["pgate"], b_gates[l:l + 1], W["w_attn_out"], W["w_conv_out"], W["w_o"])
        gf["w_o", l], gb["w_o", l] = _dw(sv["mg_t"], dx1)
        gf["w_attn_out", l], gb["w_attn_out", l] = _dw(sv["att_t"], dya)
        gf["w_conv_out", l], gb["w_conv_out", l] = _dw(sv["cv_t"], dyc)
        conv_args = (dcv, sv["pconv"], conv_w_full[l], conv_b[l:l + 1], S)
        (dconv, parts["conv"][l]), arrived = run(_conv_bwd, [sibling_exchange(early)] if last else [], *conv_args)
        if last:
            half_adds(early, arrived[0])
        attn_args = (sv["pqkv"], datt, sv["lse"], sinks[l], attn_bias, S)
        (dq, dkv, parts["sinks"][l]), arrived = run(_attn_bwd, [chip_exchange(early)] if last else [], *attn_args)
        if last:
            got.update(zip(early, arrived[0]))
            owner_sums(upper + early)
        pieces = [(dq, D), (dkv, QKV_W - D), (dconv, CONV_W), (dgate, GATE_W)]
        gf["w_in", l], gb["w_in", l] = _dw_pieces(sv["ht"], pieces)
        in_args = (pieces, W["w_in"], xs[l], dx1, g_mix[l:l + 1])
        if not last:
            (dx, parts["g_mix"][l]), arrived = run(_inproj_bwd, [sibling_exchange(upper)], *in_args)
            half_adds(upper, arrived[0])
        else:
            tail = [("w_in", 0)]
            half_adds(tail, _run_comm(sibling_exchange(tail), "grad_sibling_exchange"))
            done = _sibling_assemble_comm(list(mine.values()), [(1,) if n == "w_in" else (0, 1) for n in mine])
            (dx, parts["g_mix"][l]), arrived = run(_inproj_bwd, [chip_exchange(tail), done], *in_args)
            got.update(zip(tail, arrived[0]))
            mine = dict(zip(mine, arrived[1]))
            owner_sums(tail)
            mine["w_in"] = _run_comm(_sibling_assemble_comm([mine["w_in"]], [(0,)]), "grad_sibling_assemble")[0]
    grads = mine

    res = {}
    for n in big_w:
        res[n] = tuple(_adamw(big_w[n], grads[n], big_m[n], big_v[n]))

    small_p = dict(g_mix=g_mix, b_gates=b_gates, sinks=sinks, conv_w=conv_w, conv_b=conv_b, g_mlp=g_mlp, g_final=g_final.reshape(1, D))
    small_m = dict(g_mix=m_g_mix, b_gates=m_b_gates, sinks=m_sinks, conv_w=m_conv_w, conv_b=m_conv_b, g_mlp=m_g_mlp,
                   g_final=m_g_final.reshape(1, D))
    small_v = dict(g_mix=v_g_mix, b_gates=v_b_gates, sinks=v_sinks, conv_w=v_conv_w, conv_b=v_conv_b, g_mlp=v_g_mlp,
                   g_final=v_g_final.reshape(1, D))
    loss, small = _small_step(parts, small_p, small_m, small_v)
    for n, vals in small.items():
        res[n] = tuple(v.reshape(D) for v in vals) if n == "g_final" else tuple(vals)

    weights = ["g_mix", "w_in", "b_gates", "sinks", "w_attn_out", "conv_w", "conv_b", "w_conv_out", "w_o", "g_mlp", "w_up",
               "w_down", "g_final"]
    out = [loss.reshape(()), dx.reshape(B, S, D)]
    for k in range(4):
        out += [res[n][k] for n in weights]
    return tuple(out)
```

```python
import functools

import numpy as np
import jax
import jax.numpy as jnp
from jax import lax
from jax.experimental import pallas as pl
from jax.experimental.pallas import tpu as pltpu

F32 = jnp.float32
BF16 = jnp.bfloat16

D_MODEL = 1024
HEAD_DIM = 64
N_Q_HEADS = 16
N_KV_HEADS = 4
GQA_GROUP = 4
WINDOW = 128
D_FF = 4096
DEPTH = 2
RMS_EPS = 1e-6
NEG_INF = -1e30
ATTN_SCALE = HEAD_DIM ** -0.5
QKV_W = 1536
CONV_W = 3072
GATE_W = 2048
IN_COLS = QKV_W + CONV_W + GATE_W
COL_TILE = 512
N_CHIP = 4
ADAM_LR = 0.001
ADAM_B1 = 0.9
ADAM_B2 = 0.999
ADAM_EPS = 1e-08
ADAM_WD = 0.01
ADAM_STEP = 10
V7X_VMEM_BYTES = 64 * 2 ** 20
VMEM_LIMIT = V7X_VMEM_BYTES - 8 * 2 ** 20
MESH = pl.DeviceIdType.MESH
ANY = pl.BlockSpec(memory_space=pl.ANY)
SMALL_ROWS = 24

_SLOPES = [float(v) for v in np.power(np.float32(2.0), -8.0 * np.arange(1, N_Q_HEADS + 1, dtype=np.float32) / N_Q_HEADS)]


def _params(*sem):
    return pltpu.CompilerParams(dimension_semantics=sem, vmem_limit_bytes=VMEM_LIMIT)


class _Hosted:
    def __init__(self, inputs, out_shape, aliases, scratch, start, finish):
        self.inputs, self.out_shape, self.aliases, self.scratch = list(inputs), list(out_shape), dict(aliases), list(scratch)
        self.start, self.finish = start, finish


def _hosted_call(comm, kern, *, out_shape, grid, in_specs, out_specs, args, name, sem, scratch_shapes=()):
    single = not isinstance(out_shape, (tuple, list))
    outs = [out_shape] if single else list(out_shape)
    ospecs = [out_specs] if single else list(out_specs)
    if comm is None:
        res = pl.pallas_call(kern, out_shape=outs, grid=grid, in_specs=list(in_specs), out_specs=ospecs,
                             scratch_shapes=list(scratch_shapes), name=name, compiler_params=_params(*sem))(*args)
        return res[0] if single else res
    n_in, n_out, n_scr = len(args), len(outs), len(scratch_shapes)
    ci, co, cs = len(comm.inputs), len(comm.out_shape), len(comm.scratch)

    def body(*refs):
        cuts = np.cumsum([0, n_in, ci, n_out, co, n_scr, cs])
        a, b, c, d, e, f = [refs[lo:hi] for lo, hi in zip(cuts[:-1], cuts[1:])]
        ids = [pl.program_id(k) for k in range(len(grid))]
        first = functools.reduce(jnp.logical_and, [i == 0 for i in ids])
        last = functools.reduce(jnp.logical_and, [i == n - 1 for i, n in zip(ids, grid)])
        pl.when(first)(lambda: comm.start(b, d, f))
        kern(*a, *c, *e)
        pl.when(last)(lambda: comm.finish(b, d, f))

    res = pl.pallas_call(
        body, out_shape=outs + comm.out_shape, grid=grid, in_specs=list(in_specs) + [ANY] * ci, out_specs=ospecs + [ANY] * co,
        scratch_shapes=list(scratch_shapes) + comm.scratch,
        input_output_aliases={n_in + i: n_out + o for i, o in comm.aliases.items()},
        name=name + "_carrier", compiler_params=_params(*(["arbitrary"] * len(grid))))(*args, *comm.inputs)
    main = res[:n_out]
    return (main[0] if single else main), res[n_out:]


def _nt(a, b):
    return lax.dot_general(a, b, (((1,), (1,)), ((), ())), preferred_element_type=F32)


def _tn(a, b):
    return lax.dot_general(a, b, (((0,), (0,)), ((), ())), preferred_element_type=F32)


def _nn(a, b):
    return jnp.dot(a, b, preferred_element_type=F32)


def _rms_stats(xf):
    r = lax.rsqrt(jnp.mean(xf * xf, axis=-1, keepdims=True) + RMS_EPS)
    return r, xf * r


def _rms_bwd(dh, xh, r, g):
    dxh = dh * g
    dx = r * (dxh - xh * jnp.mean(dxh * xh, axis=-1, keepdims=True))
    dg = jnp.sum(dh * xh, axis=0, keepdims=True)
    return dx, dg


def _dw_chunk(T):
    return min(2048, T)


def _row_halves(tm):
    return (slice(0, tm // 2), slice(tm // 2, tm))


def _resident(shape):
    return pl.BlockSpec(shape, lambda *_: (0,) * len(shape), pipeline_mode=pl.Buffered(1))


def _norm_proj(x, g, w, comm=None):
    T, D = x.shape
    tm = min(512, T)
    tk = _dw_chunk(T)
    per = tk // tm
    widths = (QKV_W, CONV_W, GATE_W)

    def kern(x_ref, g_ref, w_ref, ht_ref, *o_refs):
        _, xh = _rms_stats(x_ref[...])
        h = (xh * g_ref[...]).astype(BF16)
        ht_ref[...] = h.T
        off = 0
        for o_ref, wd in zip(o_refs, widths):
            o_ref[...] = _nn(h, w_ref[:, off:off + wd]).astype(BF16)
            off += wd

    row = lambda wd: pl.BlockSpec((tm, wd), lambda i: (i, 0))
    return _hosted_call(
        comm, kern,
        out_shape=[jax.ShapeDtypeStruct((T // tk, D, tk), BF16)] + [jax.ShapeDtypeStruct((T, wd), BF16) for wd in widths],
        grid=(T // tm,), in_specs=[row(D), pl.BlockSpec((1, D), lambda i: (0, 0)), _resident((D, IN_COLS))],
        out_specs=[pl.BlockSpec((None, D, tm), lambda i: (i // per, 0, i % per))] + [row(wd) for wd in widths],
        name="norm_proj", sem=("parallel",), args=(x, g, w))


GW = GQA_GROUP * WINDOW
BAND = 2 * WINDOW
KV_W = N_KV_HEADS * HEAD_DIM


def _attn_bias_table():
    jj = np.arange(BAND)[:, None]
    col = np.arange(GW)[None, :]
    dist = WINDOW + (col % WINDOW) - jj
    valid = (dist >= 0) & (dist < WINDOW)
    slopes = np.asarray(_SLOPES, np.float32).reshape(N_KV_HEADS, GQA_GROUP)
    tab = np.empty((2, N_KV_HEADS, BAND, GW), np.float32)
    for hk in range(N_KV_HEADS):
        bias = -slopes[hk][col // WINDOW] * dist.astype(np.float32)
        tab[0, hk] = np.where(valid, bias, np.float32(NEG_INF))
        tab[1, hk] = np.where(valid & (jj >= WINDOW), bias, np.float32(NEG_INF))
    return jnp.asarray(tab)


def _stack_heads(ref, hk):
    return jnp.concatenate(
        [ref[:, HEAD_DIM * (GQA_GROUP * hk + g): HEAD_DIM * (GQA_GROUP * hk + g + 1)] for g in range(GQA_GROUP)], axis=0)


def _kv_band(cur_ref, prev_ref, hk):
    k0 = N_Q_HEADS * HEAD_DIM
    sl = slice(HEAD_DIM * hk, HEAD_DIM * (hk + 1))
    ksl, vsl = slice(k0 + sl.start, k0 + sl.stop), slice(k0 + KV_W + sl.start, k0 + KV_W + sl.stop)
    k_band = jnp.concatenate([prev_ref[:, sl], cur_ref[:, ksl]], axis=0)
    v_band = jnp.concatenate([prev_ref[:, KV_W + sl.start:KV_W + sl.stop], cur_ref[:, vsl]], axis=0)
    return k_band, v_band


def _lane_row(vals):
    return jnp.concatenate([jnp.full((1, WINDOW), v, F32) for v in vals], axis=1)


def _attn_fwd(pqkv, sinks, bias, seq, comm=None):
    T = pqkv.shape[0]
    nblk = seq // WINDOW

    def kern(sink_ref, cur_ref, prev_ref, bias_ref, o_ref, ot_ref, lse_ref):
        i = pl.program_id(0)
        first_i = ((i % nblk) == 0).astype(jnp.int32)
        bands = [_kv_band(cur_ref, prev_ref, hk) for hk in range(N_KV_HEADS)]
        sts = [_nt(bands[hk][0], _stack_heads(cur_ref, hk) * ATTN_SCALE) + bias_ref[first_i, hk] for hk in range(N_KV_HEADS)]
        ps, scales = [], []
        for hk in range(N_KV_HEADS):
            heads = [GQA_GROUP * hk + g for g in range(GQA_GROUP)]
            sink = _lane_row([sink_ref[h] for h in heads])
            m = jnp.maximum(jnp.max(sts[hk], axis=0, keepdims=True), sink)
            p = jnp.exp(sts[hk] - m)
            den = jnp.sum(p, axis=0, keepdims=True) + jnp.exp(sink - m)
            lse = m + jnp.log(den)
            for g, h in enumerate(heads):
                lse_ref[h:h + 1, :] = lse[:, WINDOW * g:WINDOW * (g + 1)]
            ps.append(p.astype(BF16))
            scales.append(1.0 / den)
        for hk in range(N_KV_HEADS):
            ot = _tn(bands[hk][1], ps[hk]) * scales[hk]
            for g in range(GQA_GROUP):
                h = GQA_GROUP * hk + g
                ot_ref[HEAD_DIM * h:HEAD_DIM * (h + 1), :] = ot[:, WINDOW * g:WINDOW * (g + 1)].astype(BF16)
        o_ref[...] = ot_ref[...].T

    return _hosted_call(
        comm, kern,
        out_shape=(jax.ShapeDtypeStruct((T, D_MODEL), BF16), jax.ShapeDtypeStruct((D_MODEL, T), BF16),
                   jax.ShapeDtypeStruct((N_Q_HEADS, T), F32)),
        grid=(T // WINDOW,),
        in_specs=[pl.BlockSpec(memory_space=pltpu.SMEM),
                  pl.BlockSpec((WINDOW, QKV_W), lambda i: (i, 0)),
                  pl.BlockSpec((WINDOW, 2 * KV_W), lambda i: (jnp.maximum(i - 1, 0), 2)),
                  _resident(bias.shape)],
        out_specs=(pl.BlockSpec((WINDOW, D_MODEL), lambda i: (i, 0)), pl.BlockSpec((D_MODEL, WINDOW), lambda i: (0, i)),
                   pl.BlockSpec((N_Q_HEADS, WINDOW), lambda i: (0, i))),
        name="attn_fwd", sem=("parallel",), args=(sinks, pqkv, pqkv, bias))


def _pick_row(a, row):
    rid = lax.broadcasted_iota(jnp.int32, a.shape, 0)
    return jnp.sum(jnp.where(rid == row, a, 0.0), axis=0, keepdims=True)


def _conv_taps(yc, halo_yc, first_i):
    keep = (1 - first_i).astype(F32)
    p1 = _pick_row(halo_yc, 15) * keep
    p2 = _pick_row(halo_yc, 14) * keep
    rowid = lax.broadcasted_iota(jnp.int32, yc.shape, 0)
    s1 = jnp.where(rowid == 0, p1, pltpu.roll(yc, 1, 0))
    s2 = jnp.where(rowid == 0, p2, jnp.where(rowid == 1, p1, pltpu.roll(yc, 2, 0)))
    return s1, s2


def _conv_fwd(pconv, conv_w, conv_b, seq):
    T = pconv.shape[0]
    tm = min(512, seq)
    per_seq = seq // tm
    D = D_MODEL

    def kern(cur_ref, halo_ref, w_ref, b_ref, o_ref, ot_ref):
        i = pl.program_id(0)
        first_i = ((i % per_seq) == 0).astype(jnp.int32)
        cb = cur_ref[:, 0:D].astype(F32)
        yc = cur_ref[:, D:2 * D].astype(F32) * cur_ref[:, 2 * D:3 * D].astype(F32)
        halo_yc = halo_ref[:, D:2 * D].astype(F32) * halo_ref[:, 2 * D:3 * D].astype(F32)
        s1, s2 = _conv_taps(yc, halo_yc, first_i)
        z = w_ref[0:1, :] * s2 + w_ref[1:2, :] * s1 + w_ref[2:3, :] * yc
        cv = (cb * (z + b_ref[...])).astype(BF16)
        o_ref[...] = cv
        ot_ref[...] = cv.T

    return pl.pallas_call(
        kern, out_shape=(jax.ShapeDtypeStruct((T, D), BF16), jax.ShapeDtypeStruct((D, T), BF16)), grid=(T // tm,),
        in_specs=[pl.BlockSpec((tm, CONV_W), lambda i: (i, 0)),
                  pl.BlockSpec((16, CONV_W), lambda i: (jnp.maximum(i * (tm // 16) - 1, 0), 0)),
                  pl.BlockSpec((3, D), lambda i: (0, 0)), pl.BlockSpec((1, D), lambda i: (0, 0))],
        out_specs=(pl.BlockSpec((tm, D), lambda i: (i, 0)), pl.BlockSpec((D, tm), lambda i: (0, i))),
        name="conv_fwd", compiler_params=_params("parallel"))(pconv, pconv, conv_w, conv_b)


def _mix_fwd(x, att, cv, pgate, b_gates, wao, wco, wo):
    T, D = x.shape
    tm = min(512, T)

    halves = _row_halves(tm)

    def kern(x_ref, att_ref, cv_ref, pg_ref, bg_ref, wao_ref, wco_ref, wo_ref, x1_ref, ya_ref, yc_ref, mgt_ref):
        yas = [_nn(att_ref[rows, :], wao_ref[...]) for rows in halves]
        ycs = [_nn(cv_ref[rows, :], wco_ref[...]) for rows in halves]
        mgs = []
        for rows, ya, yc in zip(halves, yas, ycs):
            sa = jax.nn.sigmoid(pg_ref[rows, 0:D].astype(F32) + bg_ref[:, 0:D])
            sc = jax.nn.sigmoid(pg_ref[rows, D:2 * D].astype(F32) + bg_ref[:, D:2 * D])
            mg = (sa * ya + sc * yc).astype(BF16)
            ya_ref[rows, :] = ya.astype(BF16)
            yc_ref[rows, :] = yc.astype(BF16)
            mgt_ref[:, rows] = mg.T
            mgs.append(mg)
        for rows, mg in zip(halves, mgs):
            x1_ref[rows, :] = x_ref[rows, :] + _nn(mg, wo_ref[...])

    row = lambda w: pl.BlockSpec((tm, w), lambda i: (i, 0))
    full = lambda a, b: pl.BlockSpec((a, b), lambda i: (0, 0))
    bf = jax.ShapeDtypeStruct((T, D), BF16)
    return pl.pallas_call(
        kern, out_shape=(jax.ShapeDtypeStruct((T, D), F32), bf, bf, jax.ShapeDtypeStruct((D, T), BF16)), grid=(T // tm,),
        in_specs=[row(D), row(D), row(D), row(GATE_W), full(1, GATE_W)] + [_resident((D, D))] * 3,
        out_specs=(row(D), row(D), row(D), pl.BlockSpec((D, tm), lambda i: (0, i))),
        name="mix_fwd", compiler_params=_params("parallel"))(x, att, cv, pgate, b_gates, wao, wco, wo)


def _mlp_fwd(x1, g, wup, wdn, comm=None):
    T, D = x1.shape
    tm = min(1024, T)
    nj = D_FF // D

    def kern(x_ref, g_ref, wup_ref, wdn_ref, x2_ref, a_ref, h_scr, acc_scr):
        j = pl.program_id(1)

        @pl.when(j == 0)
        def _():
            xf = x_ref[...]
            _, xh = _rms_stats(xf)
            h_scr[...] = (xh * g_ref[...]).astype(BF16)
            acc_scr[...] = xf

        a = _nn(h_scr[...], wup_ref[j])
        a_ref[...] = a.astype(BF16)
        u = jnp.square(jnp.maximum(a, 0.0)).astype(BF16)
        acc_scr[...] += _nn(u, wdn_ref[j])

        @pl.when(j == nj - 1)
        def _():
            x2_ref[...] = acc_scr[...]

    return _hosted_call(
        comm, kern, out_shape=(jax.ShapeDtypeStruct((T, D), F32), jax.ShapeDtypeStruct((T, D_FF), BF16)), grid=(T // tm, nj),
        in_specs=[pl.BlockSpec((tm, D), lambda i, j: (i, 0)), pl.BlockSpec((1, D), lambda i, j: (0, 0)),
                  _resident((nj, D, D)), _resident((nj, D, D))],
        out_specs=(pl.BlockSpec((tm, D), lambda i, j: (i, 0)), pl.BlockSpec((tm, D), lambda i, j: (i, j))),
        scratch_shapes=[pltpu.VMEM((tm, D), BF16), pltpu.VMEM((tm, D), F32)],
        name="mlp_fwd", sem=("parallel", "arbitrary"), args=(x1, g, wup, wdn))


def _loss_bwd(x, g, tgt):
    T, D = x.shape
    tm = min(512, T)

    def kern(x_ref, g_ref, t_ref, st_ref, dx_ref):
        i = pl.program_id(0)

        @pl.when(i == 0)
        def _():
            st_ref[...] = jnp.zeros_like(st_ref)

        gg = g_ref[...]
        r, xh = _rms_stats(x_ref[...])
        e = xh * gg - t_ref[...]
        part = 0.5 * jnp.sum(jnp.mean(e * e, axis=-1, keepdims=True), axis=0, keepdims=True)
        dx, dg = _rms_bwd(e * (1.0 / D), xh, r, gg)
        dx_ref[...] = dx
        st_ref[0:1, :] += dg
        st_ref[1:2, 0:1] += part

    return pl.pallas_call(
        kern, out_shape=(jax.ShapeDtypeStruct((8, D), F32), jax.ShapeDtypeStruct((T, D), F32)), grid=(T // tm,),
        in_specs=[pl.BlockSpec((tm, D), lambda i: (i, 0)), pl.BlockSpec((1, D), lambda i: (0, 0)),
                  pl.BlockSpec((tm, D), lambda i: (i, 0))],
        out_specs=(pl.BlockSpec((8, D), lambda i: (0, 0)), pl.BlockSpec((tm, D), lambda i: (i, 0))),
        name="loss_bwd", compiler_params=_params("arbitrary"))(x, g, tgt)


def _mlp_bwd(dx2, x1, a, g, wup, wdn, comm=None):
    T, D = x1.shape
    tm = min(512, T)
    nj = D_FF // D
    tk = _dw_chunk(T)
    per = tk // tm

    def kern(dx2_ref, x1_ref, a_ref, g_ref, wup_ref, wdn_ref, dx1_ref, da_ref, ut_ref, h2t_ref, dyb_ref, dg_ref, acc_scr):
        i, j = pl.program_id(0), pl.program_id(1)

        @pl.when((i == 0) & (j == 0))
        def _():
            dg_ref[...] = jnp.zeros_like(dg_ref)

        @pl.when(j == 0)
        def _():
            dyb_ref[...] = dx2_ref[...].astype(BF16)
            acc_scr[...] = jnp.zeros_like(acc_scr)

        du = _nt(dyb_ref[...], wdn_ref[j])
        relu = jnp.maximum(a_ref[...].astype(F32), 0.0)
        da = (du * (2.0 * relu)).astype(BF16)
        da_ref[...] = da
        ut_ref[...] = jnp.square(relu).astype(BF16).T
        acc_scr[...] += _nt(da, wup_ref[j])

        @pl.when(j == nj - 1)
        def _():
            gg = g_ref[...]
            r, xh = _rms_stats(x1_ref[...])
            h2t_ref[...] = (xh * gg).astype(BF16).T
            dx, dg = _rms_bwd(acc_scr[...], xh, r, gg)
            dx1_ref[...] = dx2_ref[...] + dx
            dg_ref[...] += dg

    return _hosted_call(
        comm, kern,
        out_shape=(jax.ShapeDtypeStruct((T, D), F32), jax.ShapeDtypeStruct((T, D_FF), BF16),
                   jax.ShapeDtypeStruct((D_FF, T), BF16), jax.ShapeDtypeStruct((T // tk, D, tk), BF16),
                   jax.ShapeDtypeStruct((T, D), BF16), jax.ShapeDtypeStruct((1, D), F32)),
        grid=(T // tm, nj),
        in_specs=[pl.BlockSpec((tm, D), lambda i, j: (i, 0)), pl.BlockSpec((tm, D), lambda i, j: (i, 0)),
                  pl.BlockSpec((tm, D), lambda i, j: (i, j)), pl.BlockSpec((1, D), lambda i, j: (0, 0)),
                  _resident((nj, D, D)), _resident((nj, D, D))],
        out_specs=(pl.BlockSpec((tm, D), lambda i, j: (i, 0)), pl.BlockSpec((tm, D), lambda i, j: (i, j)),
                   pl.BlockSpec((D, tm), lambda i, j: (j, i)), pl.BlockSpec((None, D, tm), lambda i, j: (i // per, 0, i % per)),
                   pl.BlockSpec((tm, D), lambda i, j: (i, 0)), pl.BlockSpec((1, D), lambda i, j: (0, 0))),
        scratch_shapes=[pltpu.VMEM((tm, D), F32)],
        name="mlp_bwd", sem=("arbitrary", "arbitrary"), args=(dx2, x1, a, g, wup, wdn))


def _mix_bwd(dx1, ya, yc, pgate, b_gates, wao, wco, wo):
    T, D = dx1.shape
    tm = min(512, T)
    halves = _row_halves(tm)

    def kern(dx_ref, ya_ref, yc_ref, pg_ref, bg_ref, wao_ref, wco_ref, wo_ref,
             datt_ref, dcv_ref, dya_ref, dyc_ref, dgt_ref, dbg_ref):
        @pl.when(pl.program_id(0) == 0)
        def _():
            dbg_ref[...] = jnp.zeros_like(dbg_ref)

        dms = [_nt(dx_ref[rows, :].astype(BF16), wo_ref[...]) for rows in halves]
        dys = []
        for rows, dm in zip(halves, dms):
            sa = jax.nn.sigmoid(pg_ref[rows, 0:D].astype(F32) + bg_ref[:, 0:D])
            sc = jax.nn.sigmoid(pg_ref[rows, D:2 * D].astype(F32) + bg_ref[:, D:2 * D])
            dya = (dm * sa).astype(BF16)
            dyc = (dm * sc).astype(BF16)
            dga = dm * ya_ref[rows, :].astype(F32) * (sa * (1.0 - sa))
            dgc = dm * yc_ref[rows, :].astype(F32) * (sc * (1.0 - sc))
            dya_ref[rows, :] = dya
            dyc_ref[rows, :] = dyc
            dgt_ref[rows, 0:D] = dga.astype(BF16)
            dgt_ref[rows, D:2 * D] = dgc.astype(BF16)
            dbg_ref[:, 0:D] += jnp.sum(dga, axis=0, keepdims=True)
            dbg_ref[:, D:2 * D] += jnp.sum(dgc, axis=0, keepdims=True)
            dys.append((dya, dyc))
        for rows, (dya, dyc) in zip(halves, dys):
            datt_ref[rows, :] = _nt(dya, wao_ref[...]).astype(BF16)
            dcv_ref[rows, :] = _nt(dyc, wco_ref[...]).astype(BF16)

    row = lambda w: pl.BlockSpec((tm, w), lambda i: (i, 0))
    full = lambda a, b: pl.BlockSpec((a, b), lambda i: (0, 0))
    bf = jax.ShapeDtypeStruct((T, D), BF16)
    return pl.pallas_call(
        kern,
        out_shape=(bf, bf, bf, bf, jax.ShapeDtypeStruct((T, GATE_W), BF16), jax.ShapeDtypeStruct((1, GATE_W), F32)),
        grid=(T // tm,),
        in_specs=[row(D), row(D), row(D), row(GATE_W), full(1, GATE_W)] + [_resident((D, D))] * 3,
        out_specs=(row(D), row(D), row(D), row(D), row(GATE_W), full(1, GATE_W)),
        name="mix_bwd", compiler_params=_params("arbitrary"))(dx1, ya, yc, pgate, b_gates, wao, wco, wo)


def _conv_bwd(dcv, pconv, conv_w, conv_b, seq, comm=None):
    T = pconv.shape[0]
    tm = min(512, seq)
    per_seq = seq // tm
    D = D_MODEL
    nb16 = T // 16

    def kern(dcv_ref, dcvn_ref, cur_ref, prev_ref, next_ref, w_ref, b_ref, o_ref, dwb_ref):
        i = pl.program_id(0)

        @pl.when(i == 0)
        def _():
            dwb_ref[...] = jnp.zeros_like(dwb_ref)

        first_i = ((i % per_seq) == 0).astype(jnp.int32)
        keep_next = 1.0 - (((i + 1) % per_seq) == 0).astype(F32)
        cb = cur_ref[:, 0:D].astype(F32)
        cc = cur_ref[:, D:2 * D].astype(F32)
        cu = cur_ref[:, 2 * D:3 * D].astype(F32)
        yc = cc * cu
        halo_yc = prev_ref[:, D:2 * D].astype(F32) * prev_ref[:, 2 * D:3 * D].astype(F32)
        s1, s2 = _conv_taps(yc, halo_yc, first_i)
        w0, w1, w2 = w_ref[0:1, :], w_ref[1:2, :], w_ref[2:3, :]
        z = w0 * s2 + w1 * s1 + w2 * yc
        dcv = dcv_ref[...].astype(F32)
        dz = dcv * cb
        dzn = dcvn_ref[...].astype(F32) * next_ref[:, 0:D].astype(F32) * keep_next
        n1, n2 = _pick_row(dzn, 0), _pick_row(dzn, 1)
        rowid = lax.broadcasted_iota(jnp.int32, dz.shape, 0)
        u1 = jnp.where(rowid == tm - 1, n1, pltpu.roll(dz, tm - 1, 0))
        u2 = jnp.where(rowid == tm - 1, n2, jnp.where(rowid == tm - 2, n1, pltpu.roll(dz, tm - 2, 0)))
        dyc = w2 * dz + w1 * u1 + w0 * u2
        o_ref[:, 0:D] = (dcv * (z + b_ref[...])).astype(BF16)
        o_ref[:, D:2 * D] = (dyc * cu).astype(BF16)
        o_ref[:, 2 * D:3 * D] = (dyc * cc).astype(BF16)
        dwb_ref[0:1, :] += jnp.sum(dz * s2, axis=0, keepdims=True)
        dwb_ref[1:2, :] += jnp.sum(dz * s1, axis=0, keepdims=True)
        dwb_ref[2:3, :] += jnp.sum(dz * yc, axis=0, keepdims=True)
        dwb_ref[3:4, :] += jnp.sum(dz, axis=0, keepdims=True)

    prev_map = lambda i: (jnp.maximum(i * (tm // 16) - 1, 0), 0)
    next_map = lambda i: (jnp.minimum((i + 1) * (tm // 16), nb16 - 1), 0)
    return _hosted_call(
        comm, kern, out_shape=(jax.ShapeDtypeStruct((T, CONV_W), BF16), jax.ShapeDtypeStruct((8, D), F32)), grid=(T // tm,),
        in_specs=[pl.BlockSpec((tm, D), lambda i: (i, 0)), pl.BlockSpec((16, D), next_map),
                  pl.BlockSpec((tm, CONV_W), lambda i: (i, 0)), pl.BlockSpec((16, CONV_W), prev_map),
                  pl.BlockSpec((16, CONV_W), next_map),
                  pl.BlockSpec((3, D), lambda i: (0, 0)), pl.BlockSpec((1, D), lambda i: (0, 0))],
        out_specs=(pl.BlockSpec((tm, CONV_W), lambda i: (i, 0)), pl.BlockSpec((8, D), lambda i: (0, 0))),
        name="conv_bwd", sem=("arbitrary",), args=(dcv, dcv, pconv, pconv, pconv, conv_w, conv_b))


def _attn_bwd(pqkv, datt, lse, sinks, bias, seq, comm=None):
    T = pqkv.shape[0]
    nblk = seq // WINDOW
    nseq = T // seq
    KVW = KV_W

    def kern(sink_ref, cur_ref, prev_ref, do_ref, lse_ref, bias_ref, dq_ref, dkv_ref, ds_ref, kc_scr, vc_scr, dqt_scr):
        b, st = pl.program_id(0), pl.program_id(1)

        @pl.when((b == 0) & (st == 0))
        def _():
            ds_ref[...] = jnp.zeros_like(ds_ref)

        @pl.when(st == 0)
        def _():
            kc_scr[...] = jnp.zeros_like(kc_scr)
            vc_scr[...] = jnp.zeros_like(vc_scr)

        @pl.when(st < nblk)
        def _():
            first_i = (st == 0).astype(jnp.int32)
            groups = range(N_KV_HEADS)
            bands = [_kv_band(cur_ref, prev_ref, hk) for hk in groups]
            qs = [_stack_heads(cur_ref, hk) for hk in groups]
            dos = [_stack_heads(do_ref, hk) for hk in groups]
            sts = [_nt(bands[hk][0], qs[hk] * ATTN_SCALE) + bias_ref[first_i, hk] for hk in groups]
            dps = [_nt(bands[hk][1], dos[hk]) for hk in groups]
            pbs, dsss = [], []
            for hk in groups:
                heads = [GQA_GROUP * hk + g for g in range(GQA_GROUP)]
                sink = _lane_row([sink_ref[h] for h in heads])
                lse_g = jnp.concatenate([lse_ref[h:h + 1, :] for h in heads], axis=1)
                p = jnp.exp(sts[hk] - lse_g)
                d_row = jnp.sum(p * dps[hk], axis=0, keepdims=True)
                dsss.append((p * (dps[hk] - d_row) * ATTN_SCALE).astype(BF16))
                pbs.append(p.astype(BF16))
                psd = jnp.exp(sink - lse_g) * d_row
                for g, h in enumerate(heads):
                    ds_ref[0:1, h:h + 1] -= jnp.sum(psd[:, WINDOW * g:WINDOW * (g + 1)], axis=1, keepdims=True)
            for hk in groups:
                dqt = _tn(bands[hk][0], dsss[hk])
                dk_b = _nn(dsss[hk], qs[hk])
                dv_b = _nn(pbs[hk], dos[hk])
                for g in range(GQA_GROUP):
                    h = GQA_GROUP * hk + g
                    dqt_scr[HEAD_DIM * h:HEAD_DIM * (h + 1), :] = dqt[:, WINDOW * g:WINDOW * (g + 1)].astype(BF16)
                ksl = slice(HEAD_DIM * hk, HEAD_DIM * (hk + 1))
                vsl = slice(KVW + HEAD_DIM * hk, KVW + HEAD_DIM * (hk + 1))
                dkv_ref[:, ksl] = (kc_scr[:, ksl] + dk_b[0:WINDOW]).astype(BF16)
                dkv_ref[:, vsl] = (vc_scr[:, ksl] + dv_b[0:WINDOW]).astype(BF16)
                kc_scr[:, ksl] = dk_b[WINDOW:2 * WINDOW]
                vc_scr[:, ksl] = dv_b[WINDOW:2 * WINDOW]
            dq_ref[...] = dqt_scr[...].T

        @pl.when(st == nblk)
        def _():
            dkv_ref[:, 0:KVW] = kc_scr[...].astype(BF16)
            dkv_ref[:, KVW:2 * KVW] = vc_scr[...].astype(BF16)

    cur_map = lambda b, s: (b * nblk + jnp.minimum(s, nblk - 1), 0)
    prev_row = lambda b, s: b * nblk + jnp.clip(s - 1, 0, nblk - 1)
    return _hosted_call(
        comm, kern,
        out_shape=(jax.ShapeDtypeStruct((T, D_MODEL), BF16), jax.ShapeDtypeStruct((T, 2 * KVW), BF16),
                   jax.ShapeDtypeStruct((8, 128), F32)),
        grid=(nseq, nblk + 1),
        in_specs=[pl.BlockSpec(memory_space=pltpu.SMEM),
                  pl.BlockSpec((WINDOW, QKV_W), cur_map),
                  pl.BlockSpec((WINDOW, 2 * KVW), lambda b, s: (prev_row(b, s), 2)),
                  pl.BlockSpec((WINDOW, D_MODEL), cur_map),
                  pl.BlockSpec((N_Q_HEADS, WINDOW), lambda b, s: (0, b * nblk + jnp.minimum(s, nblk - 1))),
                  _resident(bias.shape)],
        out_specs=(pl.BlockSpec((WINDOW, D_MODEL), cur_map),
                   pl.BlockSpec((WINDOW, 2 * KVW), lambda b, s: (prev_row(b, s), 0)),
                   pl.BlockSpec((8, 128), lambda b, s: (0, 0))),
        scratch_shapes=[pltpu.VMEM((WINDOW, KVW), F32), pltpu.VMEM((WINDOW, KVW), F32), pltpu.VMEM((D_MODEL, WINDOW), BF16)],
        name="attn_bwd", sem=("arbitrary", "arbitrary"), args=(sinks, pqkv, pqkv, datt, lse, bias))


def _piece_tiles(pieces):
    out, start = [], 0
    for arr, width in pieces:
        out.append((arr, start, width // COL_TILE))
        start += width // COL_TILE
    return out, start


def _inproj_bwd(pieces, w_in, x, dx_in, g, comm=None):
    T, D = x.shape
    tm = min(512, T)

    def kern(*refs):
        p_refs = refs[:len(pieces)]
        w_ref, x_ref, dxin_ref, g_ref, dx_ref, dg_ref = refs[len(pieces):]

        @pl.when(pl.program_id(0) == 0)
        def _():
            dg_ref[...] = jnp.zeros_like(dg_ref)

        dh, off = None, 0
        for p_ref, (_, width) in zip(p_refs, pieces):
            part = _nt(p_ref[...], w_ref[:, off:off + width])
            dh = part if dh is None else dh + part
            off += width
        gg = g_ref[...]
        r, xh = _rms_stats(x_ref[...])
        dx, dg = _rms_bwd(dh, xh, r, gg)
        dx_ref[...] = dxin_ref[...] + dx
        dg_ref[...] += dg

    row = lambda wd: pl.BlockSpec((tm, wd), lambda i: (i, 0))
    return _hosted_call(
        comm, kern, out_shape=(jax.ShapeDtypeStruct((T, D), F32), jax.ShapeDtypeStruct((1, D), F32)), grid=(T // tm,),
        in_specs=[row(wd) for _, wd in pieces] + [_resident((D, IN_COLS)), row(D), row(D), pl.BlockSpec((1, D), lambda i: (0, 0))],
        out_specs=(row(D), pl.BlockSpec((1, D), lambda i: (0, 0))),
        name="inproj_bwd", sem=("arbitrary",), args=(*[a for a, _ in pieces], w_in, x, dx_in, g))


def _dw_pieces(lhs_t, pieces):
    nt, K, tk = lhs_t.shape
    tiles, nj = _piece_tiles(pieces)

    def kern(*refs):
        lhs_ref = refs[0]
        p_refs = refs[1:1 + len(tiles)]
        o_ref, ob_ref = refs[1 + len(tiles):]
        j, t = pl.program_id(0), pl.program_id(1)

        @pl.when(t == 0)
        def _():
            o_ref[...] = jnp.zeros_like(o_ref)

        for p_ref, (_, start, n) in zip(p_refs, tiles):
            @pl.when((j >= start) & (j < start + n))
            def _(p_ref=p_ref):
                o_ref[...] += _nn(lhs_ref[t], p_ref[...])

        @pl.when(t == nt - 1)
        def _():
            ob_ref[...] = o_ref[...].astype(BF16)

    def p_map(start, n):
        return lambda j, t: (jnp.where((j >= start) & (j < start + n), t, 0), jnp.clip(j - start, 0, n - 1))

    N = nj * COL_TILE
    return pl.pallas_call(
        kern, out_shape=(jax.ShapeDtypeStruct((K, N), F32), jax.ShapeDtypeStruct((K, N), BF16)), grid=(nj, nt),
        in_specs=[_resident((nt, K, tk))] + [pl.BlockSpec((tk, COL_TILE), p_map(s, n)) for _, s, n in tiles],
        out_specs=(pl.BlockSpec((K, COL_TILE), lambda j, t: (0, j)), pl.BlockSpec((K, COL_TILE), lambda j, t: (0, j))),
        name="dw_pieces", compiler_params=_params("arbitrary", "arbitrary"))(lhs_t, *[a for a, _, _ in tiles])


def _dw(lhs_t, rhs):
    lhs_res, rhs_res = lhs_t.ndim == 3, rhs.ndim == 3
    W = D_MODEL
    if lhs_res:
        nt, K, tk = lhs_t.shape
    else:
        K, tk = lhs_t.shape[0], _dw_chunk(lhs_t.shape[1])
        nt = lhs_t.shape[1] // tk
    N = rhs.shape[-1]

    def kern(lhs_ref, rhs_ref, o_ref, ob_ref):
        t = pl.program_id(2)

        @pl.when(t == 0)
        def _():
            o_ref[...] = jnp.zeros_like(o_ref)

        a = lhs_ref[t] if lhs_res else lhs_ref[...]
        b = rhs_ref[t] if rhs_res else rhs_ref[...]
        o_ref[...] += _nn(a, b.astype(BF16))

        @pl.when(t == nt - 1)
        def _():
            ob_ref[...] = o_ref[...].astype(BF16)

    omap = lambda i, j, t: (i, j)
    lspec = _resident((nt, W, tk)) if lhs_res else pl.BlockSpec((W, tk), lambda i, j, t: (i, t))
    rspec = _resident((nt, tk, W)) if rhs_res else pl.BlockSpec((tk, W), lambda i, j, t: (t, j))
    return pl.pallas_call(
        kern, out_shape=(jax.ShapeDtypeStruct((K, N), F32), jax.ShapeDtypeStruct((K, N), BF16)), grid=(K // W, N // W, nt),
        in_specs=[lspec, rspec], out_specs=(pl.BlockSpec((W, W), omap), pl.BlockSpec((W, W), omap)),
        name="dw", compiler_params=_params("arbitrary", "arbitrary", "arbitrary"))(lhs_t, rhs)


BIG = (("w_in", D_MODEL, IN_COLS, "col"), ("w_attn_out", D_MODEL, D_MODEL, "row"), ("w_conv_out", D_MODEL, D_MODEL, "row"),
       ("w_o", D_MODEL, D_MODEL, "row"), ("w_up", D_MODEL, D_FF, "col"), ("w_down", D_FF, D_MODEL, "row"))


def _shard_dims(rows, cols, kind):
    return (rows, cols // N_CHIP) if kind in ("col", "chip") else (rows // N_CHIP, cols)


def _window(ref, rows, cols, kind, chip, half):
    sr, sc = _shard_dims(rows, cols, kind)
    hr = sr // 2
    if kind == "col":
        return ref.at[pl.ds(half * hr, hr), pl.ds(chip * sc, sc)]
    if kind == "chip":
        return ref.at[chip, pl.ds(half * hr, hr), :]
    return ref.at[pl.ds(chip * sr + half * hr, hr), :]


def _mesh_pos():
    x, y, c = lax.axis_index("x"), lax.axis_index("y"), lax.axis_index("c")
    return x, y, c, 2 * x + y


_REL_BITS = (2, 1, 3)


def _rel_dev(x, y, c, r):
    return ((1 - x, y, c), (x, 1 - y, c), (1 - x, 1 - y, c))[r]


def _for_my_chip(j, fn):
    for js in range(N_CHIP):
        pl.when(j == js)(functools.partial(fn, js))


def _cast_into_full(j_arr, shard, l, rows, cols, kind):
    sr, sc = _shard_dims(rows, cols, kind)
    tr = min(256, sr)

    def kern(j_ref, s_ref, o_ref):
        o_ref[...] = s_ref[...].astype(BF16)

    shape, block = (rows, cols), (tr, sc)
    if kind == "col":
        omap = lambda i, j_ref: (i, j_ref[0])
    elif kind == "chip":
        shape, block = (N_CHIP, rows, sc), (None, tr, sc)
        omap = lambda i, j_ref: (j_ref[0], i, 0)
    else:
        omap = lambda i, j_ref: (j_ref[0] * (sr // tr) + i, 0)
    gs = pltpu.PrefetchScalarGridSpec(
        num_scalar_prefetch=1, grid=(sr // tr,),
        in_specs=[pl.BlockSpec((None, tr, sc), lambda i, j_ref: (l, i, 0))], out_specs=pl.BlockSpec(block, omap))
    return pl.pallas_call(kern, out_shape=jax.ShapeDtypeStruct(shape, BF16), grid_spec=gs, name="cast_into_full",
                          compiler_params=_params("arbitrary"))(j_arr, shard)


def _gather_comm(fulls, cw=None):
    n_big = len(fulls)
    n_piece = n_big + (0 if cw is None else 1)

    def pieces(in_refs, o_refs, js, c):
        def piece(p, chip, half):
            if p == n_big:
                return o_refs[p].at[half, chip]
            _, rows, cols, kind = fulls[p]
            return _window(o_refs[p], rows, cols, kind, chip, half)

        def mine(p):
            return in_refs[p].at[c] if p == n_big else piece(p, js, c)

        return piece, mine

    def local_copies(in_refs, piece, js, loc_sem):
        if cw is None:
            return []
        return [pltpu.make_async_copy(in_refs[n_big].at[half], piece(n_big, js, half), loc_sem.at[half]) for half in range(2)]

    def ici_copy(piece, mine, js, x, y, c, r, p, send_sem, recv_sem):
        return pltpu.make_async_remote_copy(mine(p), piece(p, js, c), send_sem.at[r * n_piece + p], recv_sem.at[r * n_piece + p],
                                            _rel_dev(x, y, c, r), MESH)

    def start(in_refs, o_refs, sems):
        send_sem, recv_sem, _, _, loc_sem = sems
        x, y, c, j = _mesh_pos()

        def run(js):
            piece, mine = pieces(in_refs, o_refs, js, c)
            for cp in local_copies(in_refs, piece, js, loc_sem):
                cp.start()
            for r in range(3):
                for p in range(n_piece):
                    ici_copy(piece, mine, js, x, y, c, r, p, send_sem, recv_sem).start()

        _for_my_chip(j, run)

    def finish(in_refs, o_refs, sems):
        send_sem, recv_sem, fsend_sem, frecv_sem, loc_sem = sems
        x, y, c, j = _mesh_pos()

        def run(js):
            piece, mine = pieces(in_refs, o_refs, js, c)
            fwds = []
            for r in range(3):
                ks = js ^ _REL_BITS[r]
                for p in range(n_piece):
                    got = piece(p, ks, c)
                    pltpu.make_async_remote_copy(got, got, send_sem.at[r * n_piece + p], recv_sem.at[r * n_piece + p],
                                                 _rel_dev(x, y, c, r), MESH).wait_recv()
                    cp = pltpu.make_async_remote_copy(got, got, fsend_sem.at[r * n_piece + p], frecv_sem.at[r * n_piece + p],
                                                      (x, y, 1 - c), MESH)
                    cp.start()
                    fwds.append(cp)
            for r in range(3):
                ks = js ^ _REL_BITS[r]
                for p in range(n_piece):
                    got = piece(p, ks, 1 - c)
                    pltpu.make_async_remote_copy(got, got, fsend_sem.at[r * n_piece + p], frecv_sem.at[r * n_piece + p],
                                                 (x, y, 1 - c), MESH).wait_recv()
            for r in range(3):
                for p in range(n_piece):
                    ici_copy(piece, mine, js, x, y, c, r, p, send_sem, recv_sem).wait_send()
            for cp in fwds:
                cp.wait_send()
            for cp in local_copies(in_refs, piece, js, loc_sem):
                cp.wait()

        _for_my_chip(j, run)

    out_shape = [jax.ShapeDtypeStruct(a.shape, BF16) for a, _, _, _ in fulls]
    ins = [a for a, _, _, _ in fulls]
    if cw is not None:
        out_shape.append(jax.ShapeDtypeStruct((DEPTH, N_CHIP, 3, D_MODEL // N_CHIP), F32))
        ins.append(cw)
    scratch = [pltpu.SemaphoreType.DMA((3 * n_piece,))] * 4 + [pltpu.SemaphoreType.DMA((2,))]
    return _Hosted(ins, out_shape, {p: p for p in range(n_big)}, scratch, start, finish)


def _compose(*comms):
    comms = [cm for cm in comms if cm is not None]
    if len(comms) <= 1:
        return comms[0] if comms else None
    ins, outs, aliases, scratch, cuts = [], [], {}, [], []
    for cm in comms:
        cuts.append((len(ins), len(outs), len(scratch)))
        aliases.update({len(ins) + i: len(outs) + o for i, o in cm.aliases.items()})
        ins, outs, scratch = ins + cm.inputs, outs + cm.out_shape, scratch + cm.scratch

    def parts(a, b, s):
        for cm, (i0, o0, s0) in zip(comms, cuts):
            yield cm, a[i0:i0 + len(cm.inputs)], b[o0:o0 + len(cm.out_shape)], s[s0:s0 + len(cm.scratch)]

    def start(a, b, s):
        for cm, pa, pb, ps in parts(a, b, s):
            cm.start(pa, pb, ps)

    def finish(a, b, s):
        for cm, pa, pb, ps in parts(a, b, s):
            cm.finish(pa, pb, ps)

    return _Hosted(ins, outs, aliases, scratch, start, finish)


def _run_comm(comm, name):
    n_in, n_out = len(comm.inputs), len(comm.out_shape)

    def body(*refs):
        comm.start(refs[:n_in], refs[n_in:n_in + n_out], refs[n_in + n_out:])
        comm.finish(refs[:n_in], refs[n_in:n_in + n_out], refs[n_in + n_out:])

    return pl.pallas_call(body, out_shape=comm.out_shape, in_specs=[ANY] * n_in, out_specs=[ANY] * n_out,
                          input_output_aliases=comm.aliases, scratch_shapes=comm.scratch, name=name)(*comm.inputs)


def _sibling_exchange_comm(gb):
    n = len(gb)

    def copies(g_refs, o_refs, sems, c):
        send_sem, recv_sem = sems
        x, y, _, _ = _mesh_pos()
        return [pltpu.make_async_remote_copy(_window(g_refs[t], rows, cols, kind, chip, 1 - c),
                                             _window(o_refs[t], rows, cols, kind, chip, 1 - c),
                                             send_sem.at[N_CHIP * t + chip], recv_sem.at[N_CHIP * t + chip], (x, y, 1 - c), MESH)
                for t, (_, rows, cols, kind) in enumerate(gb) for chip in range(N_CHIP)]

    def start(g_refs, o_refs, sems):
        for cp in copies(g_refs, o_refs, sems, lax.axis_index("c")):
            cp.start()

    def finish(g_refs, o_refs, sems):
        c = lax.axis_index("c")
        for cp in copies(g_refs, o_refs, sems, 1 - c):
            cp.wait_recv()
        for cp in copies(g_refs, o_refs, sems, c):
            cp.wait_send()

    return _Hosted([a for a, _, _, _ in gb], [jax.ShapeDtypeStruct(a.shape, BF16) for a, _, _, _ in gb], {},
                   [pltpu.SemaphoreType.DMA((N_CHIP * n,))] * 2, start, finish)


def _half_add(jc_arr, g, sib, rows, cols, kind):
    sr, sc = _shard_dims(rows, cols, kind)
    hr = sr // 2

    def kern(jc_ref, g_ref, s_ref, ob_ref, of_ref):
        v = g_ref[...] + s_ref[...].astype(F32)
        ob_ref[...] = v.astype(BF16)

        @pl.when(pl.program_id(0) == jc_ref[0])
        def _():
            of_ref[...] = v

    if kind == "col":
        imap = lambda j, jc_ref: (jc_ref[1], j)
    else:
        imap = lambda j, jc_ref: (2 * j + jc_ref[1], 0)
    gs = pltpu.PrefetchScalarGridSpec(
        num_scalar_prefetch=1, grid=(N_CHIP,),
        in_specs=[pl.BlockSpec((hr, sc), imap), pl.BlockSpec((hr, sc), imap)],
        out_specs=[pl.BlockSpec((None, hr, sc), lambda j, jc_ref: (j, 0, 0)), pl.BlockSpec((hr, sc), lambda j, jc_ref: (0, 0))])
    return pl.pallas_call(
        kern, out_shape=(jax.ShapeDtypeStruct((N_CHIP, hr, sc), BF16), jax.ShapeDtypeStruct((hr, sc), F32)),
        grid_spec=gs, name="grad_half_add", compiler_params=_params("arbitrary"))(jc_arr, g, sib)


def _chip_exchange_comm(sbs):
    n = len(sbs)

    def copies(s_refs, o_refs, sems):
        send_sem, recv_sem = sems
        x, y, c, j = _mesh_pos()
        return [pltpu.make_async_remote_copy(s_refs[t].at[j ^ _REL_BITS[r]], o_refs[t].at[r], send_sem.at[n * r + t],
                                             recv_sem.at[n * r + t], _rel_dev(x, y, c, r), MESH)
                for r in range(3) for t in range(n)]

    def start(s_refs, o_refs, sems):
        for cp in copies(s_refs, o_refs, sems):
            cp.start()

    def finish(s_refs, o_refs, sems):
        for cp in copies(s_refs, o_refs, sems):
            cp.wait()

    return _Hosted(sbs, [jax.ShapeDtypeStruct((3,) + a.shape[1:], BF16) for a in sbs], {},
                   [pltpu.SemaphoreType.DMA((3 * n,))] * 2, start, finish)


def _owner_sum(jc_arr, sf, rb, l, into=None):
    hr, sc = sf.shape

    def kern(jc_ref, s_ref, r0_ref, r1_ref, r2_ref, *rest):
        o_ref = rest[-1]
        o_ref[...] = ((s_ref[...] + r0_ref[...].astype(F32)) + r1_ref[...].astype(F32)) + r2_ref[...].astype(F32)

    in_specs = [pl.BlockSpec((hr, sc), lambda i, jc_ref: (0, 0))]
    in_specs += [pl.BlockSpec((None, hr, sc), lambda i, jc_ref, r=r: (r, 0, 0)) for r in range(3)]
    args = [jc_arr, sf, rb, rb, rb]
    aliases = {}
    if into is not None:
        in_specs.append(ANY)
        args.append(into)
        aliases = {len(args) - 1: 0}
    gs = pltpu.PrefetchScalarGridSpec(
        num_scalar_prefetch=1, grid=(1,), in_specs=in_specs,
        out_specs=pl.BlockSpec((None, hr, sc), lambda i, jc_ref: (l, jc_ref[1], 0)))
    return pl.pallas_call(kern, out_shape=jax.ShapeDtypeStruct((DEPTH, 2 * hr, sc), F32), grid_spec=gs,
                          input_output_aliases=aliases, name="grad_owner_sum", compiler_params=_params("arbitrary"))(*args)


def _sibling_assemble_comm(grads, layers):
    n = len(grads)
    todo = [(q, l) for q in range(n) for l in layers[q]]

    def copies(o_refs, sems, half):
        send_sem, recv_sem = sems
        x, y, c, _ = _mesh_pos()
        out = []
        for k, (q, l) in enumerate(todo):
            hr = grads[q].shape[1] // 2
            w = o_refs[q].at[l, pl.ds(half * hr, hr), :]
            out.append(pltpu.make_async_remote_copy(w, w, send_sem.at[k], recv_sem.at[k], (x, y, 1 - c), MESH))
        return out

    def start(_, o_refs, sems):
        for cp in copies(o_refs, sems, lax.axis_index("c")):
            cp.start()

    def finish(_, o_refs, sems):
        c = lax.axis_index("c")
        for cp in copies(o_refs, sems, 1 - c):
            cp.wait_recv()
        for cp in copies(o_refs, sems, c):
            cp.wait_send()

    return _Hosted(grads, [jax.ShapeDtypeStruct(g.shape, F32) for g in grads], {q: q for q in range(n)},
                   [pltpu.SemaphoreType.DMA((len(todo),))] * 2, start, finish)


def _adamw_math(w, g, m, v):
    m = ADAM_B1 * m + (1.0 - ADAM_B1) * g
    v = ADAM_B2 * v + (1.0 - ADAM_B2) * jnp.square(g)
    m_hat = m / (1.0 - ADAM_B1 ** ADAM_STEP)
    v_hat = v / (1.0 - ADAM_B2 ** ADAM_STEP)
    delta = -ADAM_LR * (m_hat / (jnp.sqrt(v_hat) + ADAM_EPS) + ADAM_WD * w)
    return delta, m, v


def _adamw(w, g, m, v):
    shape = w.shape
    C = shape[-1]
    R = int(np.prod(shape[:-1]))
    tr = min(256, R)
    args = [a.reshape(R, C) for a in (w, g, m, v)]

    def kern(w_ref, g_ref, m_ref, v_ref, go_ref, d_ref, nm_ref, nv_ref):
        g_val = g_ref[...]
        d, nm, nv = _adamw_math(w_ref[...], g_val, m_ref[...], v_ref[...])
        go_ref[...] = g_val
        d_ref[...] = d
        nm_ref[...] = nm
        nv_ref[...] = nv

    spec = pl.BlockSpec((tr, C), lambda i: (i, 0))
    outs = pl.pallas_call(
        kern, out_shape=[jax.ShapeDtypeStruct((R, C), F32)] * 4, grid=(R // tr,), in_specs=[spec] * 4, out_specs=[spec] * 4,
        name="adamw", compiler_params=_params("parallel"))(*args)
    return [o.reshape(shape) for o in outs]


_ROW_G_MIX, _ROW_B_GATES, _ROW_SINKS, _ROW_CONV, _ROW_G_MLP, _ROW_G_FINAL, _ROW_LOSS = 0, 2, 6, 8, 16, 18, 19


def _small_step(parts, params, moms, vels):
    names = ["g_mix", "b_gates", "sinks", "conv_w", "conv_b", "g_mlp", "g_final"]
    D = D_MODEL
    QW = D // N_CHIP
    n_dev = 8

    def body(*refs):
        it = iter(refs)
        dgmix = [next(it) for _ in range(DEPTH)]
        dbg = [next(it) for _ in range(DEPTH)]
        dsk = [next(it) for _ in range(DEPTH)]
        dwb = [next(it) for _ in range(DEPTH)]
        dgmlp = [next(it) for _ in range(DEPTH)]
        lst = next(it)
        p_refs = {n: next(it) for n in names}
        m_refs = {n: next(it) for n in names}
        v_refs = {n: next(it) for n in names}
        loss_ref = next(it)
        outs = {n: [next(it) for _ in range(4)] for n in names}
        pack_ref, all_ref, send_sem, recv_sem = next(it), next(it), next(it), next(it)

        x, y, c, j = _mesh_pos()
        me = 4 * x + 2 * y + c
        pack_ref[...] = jnp.zeros_like(pack_ref)
        for l in range(DEPTH):
            pack_ref[_ROW_G_MIX + l:_ROW_G_MIX + l + 1, :] = dgmix[l][...]
            pack_ref[_ROW_B_GATES + 2 * l:_ROW_B_GATES + 2 * l + 1, :] = dbg[l][:, 0:D]
            pack_ref[_ROW_B_GATES + 2 * l + 1:_ROW_B_GATES + 2 * l + 2, :] = dbg[l][:, D:2 * D]
            pack_ref[_ROW_SINKS + l:_ROW_SINKS + l + 1, 0:128] = dsk[l][0:1, :]
            pack_ref[_ROW_CONV + 4 * l:_ROW_CONV + 4 * l + 4, :] = dwb[l][0:4, :]
            pack_ref[_ROW_G_MLP + l:_ROW_G_MLP + l + 1, :] = dgmlp[l][...]
        pack_ref[_ROW_G_FINAL:_ROW_G_FINAL + 1, :] = lst[0:1, :]
        pack_ref[_ROW_LOSS:_ROW_LOSS + 1, :] = lst[1:2, :]

        all_ref[me] = pack_ref[...]
        cps = []
        for k in range(1, n_dev):
            dx_, dy_, dc_ = (k >> 2) & 1, (k >> 1) & 1, k & 1
            peer = (x ^ dx_, y ^ dy_, c ^ dc_)
            cp = pltpu.make_async_remote_copy(pack_ref, all_ref.at[me], send_sem.at[k - 1], recv_sem.at[k - 1], peer, MESH)
            cp.start()
            cps.append(cp)
        for cp in cps:
            cp.wait()

        tot = all_ref[0]
        for d in range(1, n_dev):
            tot = tot + all_ref[d]
        pack_ref[...] = tot

        loss_ref[...] = pack_ref[_ROW_LOSS:_ROW_LOSS + 1, 0:1]

        def finish(name, idx, g):
            w, m, v = p_refs[name][idx], m_refs[name][idx], v_refs[name][idx]
            d, nm, nv = _adamw_math(w, g, m, v)
            for ref, val in zip(outs[name], (g, d, nm, nv)):
                ref[idx] = val

        for l in range(DEPTH):
            finish("g_mix", (slice(l, l + 1), slice(None)), pack_ref[_ROW_G_MIX + l:_ROW_G_MIX + l + 1, :])
            finish("g_mlp", (slice(l, l + 1), slice(None)), pack_ref[_ROW_G_MLP + l:_ROW_G_MLP + l + 1, :])
            finish("conv_b", (slice(l, l + 1), slice(None)), pack_ref[_ROW_CONV + 4 * l + 3:_ROW_CONV + 4 * l + 4, :])
            finish("sinks", (slice(l, l + 1), slice(None)), pack_ref[_ROW_SINKS + l:_ROW_SINKS + l + 1, 0:N_Q_HEADS])
            for hf in range(2):
                finish("b_gates", (slice(l, l + 1), slice(hf * D, (hf + 1) * D)),
                       pack_ref[_ROW_B_GATES + 2 * l + hf:_ROW_B_GATES + 2 * l + hf + 1, :])
        finish("g_final", (slice(0, 1), slice(None)), pack_ref[_ROW_G_FINAL:_ROW_G_FINAL + 1, :])

        def conv_w_chip(js):
            for l in range(DEPTH):
                for k in range(3):
                    row = _ROW_CONV + 4 * l + k
                    finish("conv_w", (l, slice(k, k + 1), slice(None)), pack_ref[row:row + 1, js * QW:(js + 1) * QW])

        _for_my_chip(j, conv_w_chip)

    vm = pl.BlockSpec(memory_space=pltpu.VMEM)
    ins = (parts["g_mix"] + parts["b_gates"] + parts["sinks"] + parts["conv"] + parts["g_mlp"] + [parts["loss"]]
           + [params[n] for n in names] + [moms[n] for n in names] + [vels[n] for n in names])
    out_shape = [jax.ShapeDtypeStruct((1, 1), F32)]
    for n in names:
        out_shape += [jax.ShapeDtypeStruct(params[n].shape, F32)] * 4
    res = pl.pallas_call(
        body, out_shape=out_shape, in_specs=[vm] * len(ins), out_specs=[vm] * len(out_shape),
        scratch_shapes=[pltpu.VMEM((SMALL_ROWS, D), F32), pltpu.VMEM((n_dev, SMALL_ROWS, D), F32),
                        pltpu.SemaphoreType.DMA((n_dev - 1,)), pltpu.SemaphoreType.DMA((n_dev - 1,))],
        name="small_allreduce_adamw")(*ins)
    loss = res[0]
    out = {n: res[1 + 4 * i:5 + 4 * i] for i, n in enumerate(names)}
    return loss, out


def kernel(x, g_mix, w_in, b_gates, sinks, w_attn_out, conv_w, conv_b, w_conv_out, w_o, g_mlp, w_up, w_down, g_final, loss_target, m_g_mix, m_w_in, m_b_gates, m_sinks, m_w_attn_out, m_conv_w, m_conv_b, m_w_conv_out, m_w_o, m_g_mlp, m_w_up, m_w_down, m_g_final, v_g_mix, v_w_in, v_b_gates, v_sinks, v_w_attn_out, v_conv_w, v_conv_b, v_w_conv_out, v_w_o, v_g_mlp, v_w_up, v_w_down, v_g_final):
    B, S, D = x.shape
    T = B * S
    big_w = dict(w_in=w_in, w_attn_out=w_attn_out, w_conv_out=w_conv_out, w_o=w_o, w_up=w_up, w_down=w_down)
    big_m = dict(w_in=m_w_in, w_attn_out=m_w_attn_out, w_conv_out=m_w_conv_out, w_o=m_w_o, w_up=m_w_up, w_down=m_w_down)
    big_v = dict(w_in=v_w_in, w_attn_out=v_w_attn_out, w_conv_out=v_w_conv_out, w_o=v_w_o, w_up=v_w_up, w_down=v_w_down)

    c_arr = lax.axis_index("c").astype(jnp.int32).reshape(1)
    j_arr = (2 * lax.axis_index("x") + lax.axis_index("y")).astype(jnp.int32).reshape(1)
    jc_arr = jnp.concatenate([j_arr, c_arr])
    order = [(n, l) for n, _, _, _ in BIG for l in range(DEPTH)]
    dims = {n: (r, c_, k) for n, r, c_, k in BIG}

    wdims = dict(dims, w_up=(D, D_FF, "chip"))
    full = {(n, l): _cast_into_full(j_arr, big_w[n], l, *wdims[n]) for n, l in order}
    mixers = ("w_attn_out", "w_conv_out", "w_o")

    def gather_of(keys):
        return _gather_comm([(full[k],) + wdims[k[0]] for k in keys])

    carried = {("proj", 0): [(n, 0) for n in mixers] + [("w_up", 0)], ("attn", 0): [("w_down", 0)],
               ("mlp", 0): [("w_in", 1)] + [(n, 1) for n in mixers] + [("w_up", 1)], ("attn", 1): [("w_down", 1)]}

    def carry(fn, where, *args):
        keys = carried.get(where)
        if keys is None:
            return fn(*args)
        res, got = fn(*args, comm=gather_of(keys))
        full.update(zip(keys, got))
        return res

    first = _run_comm(_gather_comm([(full["w_in", 0],) + dims["w_in"]], conv_w), "gather_weights")
    full["w_in", 0] = first[0]
    conv_w_full = jnp.transpose(first[1], (0, 2, 1, 3)).reshape(DEPTH, 3, D)
    attn_bias = _attn_bias_table()

    xs = [x.reshape(T, D)]
    saved = []
    for l in range(DEPTH):
        ht, pqkv, pconv, pgate = carry(_norm_proj, ("proj", l), xs[-1], g_mix[l:l + 1], full["w_in", l])
        att, att_t, lse = carry(_attn_fwd, ("attn", l), pqkv, sinks[l], attn_bias, S)
        cv, cv_t = _conv_fwd(pconv, conv_w_full[l], conv_b[l:l + 1], S)
        x1, ya, yc, mg_t = _mix_fwd(xs[-1], att, cv, pgate, b_gates[l:l + 1], full["w_attn_out", l], full["w_conv_out", l],
                                    full["w_o", l])
        x2, a = carry(_mlp_fwd, ("mlp", l), x1, g_mlp[l:l + 1], full["w_up", l], full["w_down", l].reshape(N_CHIP, D, D))
        saved.append(dict(ht=ht, pqkv=pqkv, pconv=pconv, pgate=pgate, att=att, att_t=att_t, lse=lse, cv_t=cv_t, x1=x1, ya=ya,
                          yc=yc, mg_t=mg_t, a=a))
        xs.append(x2)

    loss_stats, dx = _loss_bwd(xs[-1], g_final.reshape(1, D), loss_target.reshape(T, D))

    parts = dict(g_mix=[None] * DEPTH, b_gates=[None] * DEPTH, sinks=[None] * DEPTH, conv=[None] * DEPTH,
                 g_mlp=[None] * DEPTH, loss=loss_stats)
    gf, gb, pre, got, mine = {}, {}, {}, {}, {}

    def sibling_exchange(keys):
        return _sibling_exchange_comm([(gb[k],) + dims[k[0]] for k in keys])

    def half_adds(keys, sib):
        for k, s in zip(keys, sib):
            pre[k] = _half_add(jc_arr, gf[k], s, *dims[k[0]])

    def chip_exchange(keys):
        return _chip_exchange_comm([pre[k][0] for k in keys])

    def owner_sums(keys):
        for n, l in keys:
            mine[n] = _owner_sum(jc_arr, pre[n, l][1], got[n, l], l, mine.get(n))

    def run(fn, comms, *args):
        if not comms:
            return fn(*args), []
        res, arrived = fn(*args, comm=_compose(*comms))
        outs, pos = [], 0
        for cm in comms:
            outs.append(arrived[pos:pos + len(cm.out_shape)])
            pos += len(cm.out_shape)
        return res, outs

    assert DEPTH == 2
    upper = [(n, 1) for n, _, _, _ in BIG]
    early = [(n, 0) for n in mixers + ("w_up", "w_down")]
    for l in reversed(range(DEPTH)):
        W = {n: full[(n, l)] for n in big_w}
        sv = saved[l]
        last = l == 0
        mlp_args = (dx, sv["x1"], sv["a"], g_mlp[l:l + 1], W["w_up"], W["w_down"].reshape(N_CHIP, D, D))
        (dx1, da, u_t, h2_t, dyb, parts["g_mlp"][l]), arrived = run(_mlp_bwd, [chip_exchange(upper)] if last else [], *mlp_args)
        if last:
            got.update(zip(upper, arrived[0]))
        gf["w_up", l], gb["w_up", l] = _dw(h2_t, da)
        gf["w_down", l], gb["w_down", l] = _dw(u_t, dyb.reshape(-1, _dw_chunk(T), D))
        datt, dcv, dya, dyc, dgate, parts["b_gates"][l] = _mix_bwd(
            dx1, sv["ya"], sv["yc"], sv["pgate"], b_gates[l:l + 1], W["w_attn_out"], W["w_conv_out"], W["w_o"])
        gf["w_o", l], gb["w_o", l] = _dw(sv["mg_t"], dx1)
        gf["w_attn_out", l], gb["w_attn_out", l] = _dw(sv["att_t"], dya)
        gf["w_conv_out", l], gb["w_conv_out", l] = _dw(sv["cv_t"], dyc)
        conv_args = (dcv, sv["pconv"], conv_w_full[l], conv_b[l:l + 1], S)
        (dconv, parts["conv"][l]), arrived = run(_conv_bwd, [sibling_exchange(early)] if last else [], *conv_args)
        if last:
            half_adds(early, arrived[0])
        attn_args = (sv["pqkv"], datt, sv["lse"], sinks[l], attn_bias, S)
        (dq, dkv, parts["sinks"][l]), arrived = run(_attn_bwd, [chip_exchange(early)] if last else [], *attn_args)
        if last:
            got.update(zip(early, arrived[0]))
            owner_sums(upper + early)
        pieces = [(dq, D), (dkv, QKV_W - D), (dconv, CONV_W), (dgate, GATE_W)]
        gf["w_in", l], gb["w_in", l] = _dw_pieces(sv["ht"], pieces)
        in_args = (pieces, W["w_in"], xs[l], dx1, g_mix[l:l + 1])
        if not last:
            (dx, parts["g_mix"][l]), arrived = run(_inproj_bwd, [sibling_exchange(upper)], *in_args)
            half_adds(upper, arrived[0])
        else:
            tail = [("w_in", 0)]
            half_adds(tail, _run_comm(sibling_exchange(tail), "grad_sibling_exchange"))
            done = _sibling_assemble_comm(list(mine.values()), [(1,) if n == "w_in" else (0, 1) for n in mine])
            (dx, parts["g_mix"][l]), arrived = run(_inproj_bwd, [chip_exchange(tail), done], *in_args)
            got.update(zip(tail, arrived[0]))
            mine = dict(zip(mine, arrived[1]))
            owner_sums(tail)
            mine["w_in"] = _run_comm(_sibling_assemble_comm([mine["w_in"]], [(0,)]), "grad_sibling_assemble")[0]
    grads = mine

    res = {}
    for n in big_w:
        res[n] = tuple(_adamw(big_w[n], grads[n], big_m[n], big_v[n]))

    small_p = dict(g_mix=g_mix, b_gates=b_gates, sinks=sinks, conv_w=conv_w, conv_b=conv_b, g_mlp=g_mlp, g_final=g_final.reshape(1, D))
    small_m = dict(g_mix=m_g_mix, b_gates=m_b_gates, sinks=m_sinks, conv_w=m_conv_w, conv_b=m_conv_b, g_mlp=m_g_mlp,
                   g_final=m_g_final.reshape(1, D))
    small_v = dict(g_mix=v_g_mix, b_gates=v_b_gates, sinks=v_sinks, conv_w=v_conv_w, conv_b=v_conv_b, g_mlp=v_g_mlp,
                   g_final=v_g_final.reshape(1, D))
    loss, small = _small_step(parts, small_p, small_m, small_v)
    for n, vals in small.items():
        res[n] = tuple(v.reshape(D) for v in vals) if n == "g_final" else tuple(vals)

    weights = ["g_mix", "w_in", "b_gates", "sinks", "w_attn_out", "conv_w", "conv_b", "w_conv_out", "w_o", "g_mlp", "w_up",
               "w_down", "g_final"]
    out = [loss.reshape(()), dx.reshape(B, S, D)]
    for k in range(4):
        out += [res[n][k] for n in weights]
    return tuple(out)
```

```python
import functools

import numpy as np
import jax
import jax.numpy as jnp
from jax import lax
from jax.experimental import pallas as pl
from jax.experimental.pallas import tpu as pltpu

F32 = jnp.float32
BF16 = jnp.bfloat16

D_MODEL = 1024
HEAD_DIM = 64
N_Q_HEADS = 16
N_KV_HEADS = 4
GQA_GROUP = 4
WINDOW = 128
D_FF = 4096
DEPTH = 2
RMS_EPS = 1e-6
NEG_INF = -1e30
ATTN_SCALE = HEAD_DIM ** -0.5
QKV_W = 1536
CONV_W = 3072
GATE_W = 2048
IN_COLS = QKV_W + CONV_W + GATE_W
COL_TILE = 512
N_CHIP = 4
ADAM_LR = 0.001
ADAM_B1 = 0.9
ADAM_B2 = 0.999
ADAM_EPS = 1e-08
ADAM_WD = 0.01
ADAM_STEP = 10
V7X_VMEM_BYTES = 64 * 2 ** 20
VMEM_LIMIT = V7X_VMEM_BYTES - 8 * 2 ** 20
MESH = pl.DeviceIdType.MESH
ANY = pl.BlockSpec(memory_space=pl.ANY)
SMALL_ROWS = 24

_SLOPES = [float(v) for v in np.power(np.float32(2.0), -8.0 * np.arange(1, N_Q_HEADS + 1, dtype=np.float32) / N_Q_HEADS)]


def _params(*sem):
    return pltpu.CompilerParams(dimension_semantics=sem, vmem_limit_bytes=VMEM_LIMIT)


class _Hosted:
    def __init__(self, inputs, out_shape, aliases, scratch, start, finish):
        self.inputs, self.out_shape, self.aliases, self.scratch = list(inputs), list(out_shape), dict(aliases), list(scratch)
        self.start, self.finish = start, finish


def _hosted_call(comm, kern, *, out_shape, grid, in_specs, out_specs, args, name, sem, scratch_shapes=()):
    single = not isinstance(out_shape, (tuple, list))
    outs = [out_shape] if single else list(out_shape)
    ospecs = [out_specs] if single else list(out_specs)
    if comm is None:
        res = pl.pallas_call(kern, out_shape=outs, grid=grid, in_specs=list(in_specs), out_specs=ospecs,
                             scratch_shapes=list(scratch_shapes), name=name, compiler_params=_params(*sem))(*args)
        return res[0] if single else res
    n_in, n_out, n_scr = len(args), len(outs), len(scratch_shapes)
    ci, co, cs = len(comm.inputs), len(comm.out_shape), len(comm.scratch)

    def body(*refs):
        cuts = np.cumsum([0, n_in, ci, n_out, co, n_scr, cs])
        a, b, c, d, e, f = [refs[lo:hi] for lo, hi in zip(cuts[:-1], cuts[1:])]
        ids = [pl.program_id(k) for k in range(len(grid))]
        first = functools.reduce(jnp.logical_and, [i == 0 for i in ids])
        last = functools.reduce(jnp.logical_and, [i == n - 1 for i, n in zip(ids, grid)])
        pl.when(first)(lambda: comm.start(b, d, f))
        kern(*a, *c, *e)
        pl.when(last)(lambda: comm.finish(b, d, f))

    res = pl.pallas_call(
        body, out_shape=outs + comm.out_shape, grid=grid, in_specs=list(in_specs) + [ANY] * ci, out_specs=ospecs + [ANY] * co,
        scratch_shapes=list(scratch_shapes) + comm.scratch,
        input_output_aliases={n_in + i: n_out + o for i, o in comm.aliases.items()},
        name=name + "_carrier", compiler_params=_params(*(["arbitrary"] * len(grid))))(*args, *comm.inputs)
    main = res[:n_out]
    return (main[0] if single else main), res[n_out:]


def _nt(a, b):
    return lax.dot_general(a, b, (((1,), (1,)), ((), ())), preferred_element_type=F32)


def _tn(a, b):
    return lax.dot_general(a, b, (((0,), (0,)), ((), ())), preferred_element_type=F32)


def _nn(a, b):
    return jnp.dot(a, b, preferred_element_type=F32)


def _rms_stats(xf):
    r = lax.rsqrt(jnp.mean(xf * xf, axis=-1, keepdims=True) + RMS_EPS)
    return r, xf * r


def _rms_bwd(dh, xh, r, g):
    dxh = dh * g
    dx = r * (dxh - xh * jnp.mean(dxh * xh, axis=-1, keepdims=True))
    dg = jnp.sum(dh * xh, axis=0, keepdims=True)
    return dx, dg


def _dw_chunk(T):
    return min(2048, T)


def _row_halves(tm):
    return (slice(0, tm // 2), slice(tm // 2, tm))


def _resident(shape):
    return pl.BlockSpec(shape, lambda *_: (0,) * len(shape), pipeline_mode=pl.Buffered(1))


def _norm_proj(x, g, w, comm=None):
    T, D = x.shape
    tm = min(512, T)
    tk = _dw_chunk(T)
    per = tk // tm
    widths = (QKV_W, CONV_W, GATE_W)

    def kern(x_ref, g_ref, w_ref, ht_ref, *o_refs):
        _, xh = _rms_stats(x_ref[...])
        h = (xh * g_ref[...]).astype(BF16)
        ht_ref[...] = h.T
        off = 0
        for o_ref, wd in zip(o_refs, widths):
            o_ref[...] = _nn(h, w_ref[:, off:off + wd]).astype(BF16)
            off += wd

    row = lambda wd: pl.BlockSpec((tm, wd), lambda i: (i, 0))
    return _hosted_call(
        comm, kern,
        out_shape=[jax.ShapeDtypeStruct((T // tk, D, tk), BF16)] + [jax.ShapeDtypeStruct((T, wd), BF16) for wd in widths],
        grid=(T // tm,), in_specs=[row(D), pl.BlockSpec((1, D), lambda i: (0, 0)), _resident((D, IN_COLS))],
        out_specs=[pl.BlockSpec((None, D, tm), lambda i: (i // per, 0, i % per))] + [row(wd) for wd in widths],
        name="norm_proj", sem=("parallel",), args=(x, g, w))


GW = GQA_GROUP * WINDOW
BAND = 2 * WINDOW
KV_W = N_KV_HEADS * HEAD_DIM


def _attn_bias_table():
    jj = np.arange(BAND)[:, None]
    col = np.arange(GW)[None, :]
    dist = WINDOW + (col % WINDOW) - jj
    valid = (dist >= 0) & (dist < WINDOW)
    slopes = np.asarray(_SLOPES, np.float32).reshape(N_KV_HEADS, GQA_GROUP)
    tab = np.empty((2, N_KV_HEADS, BAND, GW), np.float32)
    for hk in range(N_KV_HEADS):
        bias = -slopes[hk][col // WINDOW] * dist.astype(np.float32)
        tab[0, hk] = np.where(valid, bias, np.float32(NEG_INF))
        tab[1, hk] = np.where(valid & (jj >= WINDOW), bias, np.float32(NEG_INF))
    return jnp.asarray(tab)


def _stack_heads(ref, hk):
    return jnp.concatenate(
        [ref[:, HEAD_DIM * (GQA_GROUP * hk + g): HEAD_DIM * (GQA_GROUP * hk + g + 1)] for g in range(GQA_GROUP)], axis=0)


def _kv_band(cur_ref, prev_ref, hk):
    k0 = N_Q_HEADS * HEAD_DIM
    sl = slice(HEAD_DIM * hk, HEAD_DIM * (hk + 1))
    ksl, vsl = slice(k0 + sl.start, k0 + sl.stop), slice(k0 + KV_W + sl.start, k0 + KV_W + sl.stop)
    k_band = jnp.concatenate([prev_ref[:, sl], cur_ref[:, ksl]], axis=0)
    v_band = jnp.concatenate([prev_ref[:, KV_W + sl.start:KV_W + sl.stop], cur_ref[:, vsl]], axis=0)
    return k_band, v_band


def _lane_row(vals):
    return jnp.concatenate([jnp.full((1, WINDOW), v, F32) for v in vals], axis=1)


def _attn_fwd(pqkv, sinks, bias, seq, comm=None):
    T = pqkv.shape[0]
    nblk = seq // WINDOW

    def kern(sink_ref, cur_ref, prev_ref, bias_ref, o_ref, ot_ref, lse_ref):
        i = pl.program_id(0)
        first_i = ((i % nblk) == 0).astype(jnp.int32)
        bands = [_kv_band(cur_ref, prev_ref, hk) for hk in range(N_KV_HEADS)]
        sts = [_nt(bands[hk][0], _stack_heads(cur_ref, hk) * ATTN_SCALE) + bias_ref[first_i, hk] for hk in range(N_KV_HEADS)]
        ps, scales = [], []
        for hk in range(N_KV_HEADS):
            heads = [GQA_GROUP * hk + g for g in range(GQA_GROUP)]
            sink = _lane_row([sink_ref[h] for h in heads])
            m = jnp.maximum(jnp.max(sts[hk], axis=0, keepdims=True), sink)
            p = jnp.exp(sts[hk] - m)
            den = jnp.sum(p, axis=0, keepdims=True) + jnp.exp(sink - m)
            lse = m + jnp.log(den)
            for g, h in enumerate(heads):
                lse_ref[h:h + 1, :] = lse[:, WINDOW * g:WINDOW * (g + 1)]
            ps.append(p.astype(BF16))
            scales.append(1.0 / den)
        for hk in range(N_KV_HEADS):
            ot = _tn(bands[hk][1], ps[hk]) * scales[hk]
            for g in range(GQA_GROUP):
                h = GQA_GROUP * hk + g
                ot_ref[HEAD_DIM * h:HEAD_DIM * (h + 1), :] = ot[:, WINDOW * g:WINDOW * (g + 1)].astype(BF16)
        o_ref[...] = ot_ref[...].T

    return _hosted_call(
        comm, kern,
        out_shape=(jax.ShapeDtypeStruct((T, D_MODEL), BF16), jax.ShapeDtypeStruct((D_MODEL, T), BF16),
                   jax.ShapeDtypeStruct((N_Q_HEADS, T), F32)),
        grid=(T // WINDOW,),
        in_specs=[pl.BlockSpec(memory_space=pltpu.SMEM),
                  pl.BlockSpec((WINDOW, QKV_W), lambda i: (i, 0)),
                  pl.BlockSpec((WINDOW, 2 * KV_W), lambda i: (jnp.maximum(i - 1, 0), 2)),
                  _resident(bias.shape)],
        out_specs=(pl.BlockSpec((WINDOW, D_MODEL), lambda i: (i, 0)), pl.BlockSpec((D_MODEL, WINDOW), lambda i: (0, i)),
                   pl.BlockSpec((N_Q_HEADS, WINDOW), lambda i: (0, i))),
        name="attn_fwd", sem=("parallel",), args=(sinks, pqkv, pqkv, bias))


def _pick_row(a, row):
    rid = lax.broadcasted_iota(jnp.int32, a.shape, 0)
    return jnp.sum(jnp.where(rid == row, a, 0.0), axis=0, keepdims=True)


def _conv_taps(yc, halo_yc, first_i):
    keep = (1 - first_i).astype(F32)
    p1 = _pick_row(halo_yc, 15) * keep
    p2 = _pick_row(halo_yc, 14) * keep
    rowid = lax.broadcasted_iota(jnp.int32, yc.shape, 0)
    s1 = jnp.where(rowid == 0, p1, pltpu.roll(yc, 1, 0))
    s2 = jnp.where(rowid == 0, p2, jnp.where(rowid == 1, p1, pltpu.roll(yc, 2, 0)))
    return s1, s2


def _mix_fwd(x, att, pconv, pgate, conv_w, conv_b, b_gates, wao, wco, wo, seq):
    T, D = x.shape
    tm = min(512, seq)
    per_seq = seq // tm

    def kern(x_ref, att_ref, pc_ref, halo_ref, pg_ref, cw_ref, cb_ref, bg_ref, wao_ref, wco_ref, wo_ref,
             x1_ref, ya_ref, yc_ref, mgt_ref, cvt_ref):
        first_i = ((pl.program_id(0) % per_seq) == 0).astype(jnp.int32)
        ya = _nn(att_ref[...], wao_ref[...])
        u = pc_ref[:, D:2 * D].astype(F32) * pc_ref[:, 2 * D:3 * D].astype(F32)
        halo_u = halo_ref[:, D:2 * D].astype(F32) * halo_ref[:, 2 * D:3 * D].astype(F32)
        s1, s2 = _conv_taps(u, halo_u, first_i)
        z = cw_ref[0:1, :] * s2 + cw_ref[1:2, :] * s1 + cw_ref[2:3, :] * u
        cv = (pc_ref[:, 0:D].astype(F32) * (z + cb_ref[...])).astype(BF16)
        cvt_ref[...] = cv.T
        yc = _nn(cv, wco_ref[...])
        sa = jax.nn.sigmoid(pg_ref[:, 0:D].astype(F32) + bg_ref[:, 0:D])
        sc = jax.nn.sigmoid(pg_ref[:, D:2 * D].astype(F32) + bg_ref[:, D:2 * D])
        mg = (sa * ya + sc * yc).astype(BF16)
        ya_ref[...] = ya.astype(BF16)
        yc_ref[...] = yc.astype(BF16)
        mgt_ref[...] = mg.T
        x1_ref[...] = x_ref[...] + _nn(mg, wo_ref[...])

    row = lambda w: pl.BlockSpec((tm, w), lambda i: (i, 0))
    full = lambda a, b: pl.BlockSpec((a, b), lambda i: (0, 0))
    tsp = pl.BlockSpec((D, tm), lambda i: (0, i))
    bf, bft = jax.ShapeDtypeStruct((T, D), BF16), jax.ShapeDtypeStruct((D, T), BF16)
    return pl.pallas_call(
        kern, out_shape=(jax.ShapeDtypeStruct((T, D), F32), bf, bf, bft, bft), grid=(T // tm,),
        in_specs=[row(D), row(D), row(CONV_W), pl.BlockSpec((16, CONV_W), lambda i: (jnp.maximum(i * (tm // 16) - 1, 0), 0)),
                  row(GATE_W), full(3, D), full(1, D), full(1, GATE_W)] + [_resident((D, D))] * 3,
        out_specs=(row(D), row(D), row(D), tsp, tsp),
        name="mix_fwd", compiler_params=_params("parallel"))(x, att, pconv, pconv, pgate, conv_w, conv_b, b_gates, wao, wco, wo)


def _mlp_fwd(x1, g, wup, wdn, comm=None):
    T, D = x1.shape
    tm = min(1024, T)
    nj = D_FF // D

    def kern(x_ref, g_ref, wup_ref, wdn_ref, x2_ref, a_ref, h_scr, acc_scr):
        j = pl.program_id(1)

        @pl.when(j == 0)
        def _():
            xf = x_ref[...]
            _, xh = _rms_stats(xf)
            h_scr[...] = (xh * g_ref[...]).astype(BF16)
            acc_scr[...] = xf

        a = _nn(h_scr[...], wup_ref[j])
        a_ref[...] = a.astype(BF16)
        u = jnp.square(jnp.maximum(a, 0.0)).astype(BF16)
        acc_scr[...] += _nn(u, wdn_ref[j])

        @pl.when(j == nj - 1)
        def _():
            x2_ref[...] = acc_scr[...]

    return _hosted_call(
        comm, kern, out_shape=(jax.ShapeDtypeStruct((T, D), F32), jax.ShapeDtypeStruct((T, D_FF), BF16)), grid=(T // tm, nj),
        in_specs=[pl.BlockSpec((tm, D), lambda i, j: (i, 0)), pl.BlockSpec((1, D), lambda i, j: (0, 0)),
                  _resident((nj, D, D)), _resident((nj, D, D))],
        out_specs=(pl.BlockSpec((tm, D), lambda i, j: (i, 0)), pl.BlockSpec((tm, D), lambda i, j: (i, j))),
        scratch_shapes=[pltpu.VMEM((tm, D), BF16), pltpu.VMEM((tm, D), F32)],
        name="mlp_fwd", sem=("parallel", "arbitrary"), args=(x1, g, wup, wdn))


def _loss_bwd(x, g, tgt):
    T, D = x.shape
    tm = min(512, T)

    def kern(x_ref, g_ref, t_ref, st_ref, dx_ref):
        i = pl.program_id(0)

        @pl.when(i == 0)
        def _():
            st_ref[...] = jnp.zeros_like(st_ref)

        gg = g_ref[...]
        r, xh = _rms_stats(x_ref[...])
        e = xh * gg - t_ref[...]
        part = 0.5 * jnp.sum(jnp.mean(e * e, axis=-1, keepdims=True), axis=0, keepdims=True)
        dx, dg = _rms_bwd(e * (1.0 / D), xh, r, gg)
        dx_ref[...] = dx
        st_ref[0:1, :] += dg
        st_ref[1:2, 0:1] += part

    return pl.pallas_call(
        kern, out_shape=(jax.ShapeDtypeStruct((8, D), F32), jax.ShapeDtypeStruct((T, D), F32)), grid=(T // tm,),
        in_specs=[pl.BlockSpec((tm, D), lambda i: (i, 0)), pl.BlockSpec((1, D), lambda i: (0, 0)),
                  pl.BlockSpec((tm, D), lambda i: (i, 0))],
        out_specs=(pl.BlockSpec((8, D), lambda i: (0, 0)), pl.BlockSpec((tm, D), lambda i: (i, 0))),
        name="loss_bwd", compiler_params=_params("arbitrary"))(x, g, tgt)


def _mlp_bwd(dx2, x1, a, g, wup, wdn, comm=None):
    T, D = x1.shape
    tm = min(512, T)
    nj = D_FF // D
    tk = _dw_chunk(T)
    per = tk // tm

    def kern(dx2_ref, x1_ref, a_ref, g_ref, wup_ref, wdn_ref, dx1_ref, da_ref, ut_ref, h2t_ref, dyb_ref, dg_ref, acc_scr):
        i, j = pl.program_id(0), pl.program_id(1)

        @pl.when((i == 0) & (j == 0))
        def _():
            dg_ref[...] = jnp.zeros_like(dg_ref)

        @pl.when(j == 0)
        def _():
            dyb_ref[...] = dx2_ref[...].astype(BF16)
            acc_scr[...] = jnp.zeros_like(acc_scr)

        du = _nt(dyb_ref[...], wdn_ref[j])
        relu = jnp.maximum(a_ref[...].astype(F32), 0.0)
        da = (du * (2.0 * relu)).astype(BF16)
        da_ref[...] = da
        ut_ref[...] = jnp.square(relu).astype(BF16).T
        acc_scr[...] += _nt(da, wup_ref[j])

        @pl.when(j == nj - 1)
        def _():
            gg = g_ref[...]
            r, xh = _rms_stats(x1_ref[...])
            h2t_ref[...] = (xh * gg).astype(BF16).T
            dx, dg = _rms_bwd(acc_scr[...], xh, r, gg)
            dx1_ref[...] = dx2_ref[...] + dx
            dg_ref[...] += dg

    return _hosted_call(
        comm, kern,
        out_shape=(jax.ShapeDtypeStruct((T, D), F32), jax.ShapeDtypeStruct((T, D_FF), BF16),
                   jax.ShapeDtypeStruct((D_FF, T), BF16), jax.ShapeDtypeStruct((T // tk, D, tk), BF16),
                   jax.ShapeDtypeStruct((T, D), BF16), jax.ShapeDtypeStruct((1, D), F32)),
        grid=(T // tm, nj),
        in_specs=[pl.BlockSpec((tm, D), lambda i, j: (i, 0)), pl.BlockSpec((tm, D), lambda i, j: (i, 0)),
                  pl.BlockSpec((tm, D), lambda i, j: (i, j)), pl.BlockSpec((1, D), lambda i, j: (0, 0)),
                  _resident((nj, D, D)), _resident((nj, D, D))],
        out_specs=(pl.BlockSpec((tm, D), lambda i, j: (i, 0)), pl.BlockSpec((tm, D), lambda i, j: (i, j)),
                   pl.BlockSpec((D, tm), lambda i, j: (j, i)), pl.BlockSpec((None, D, tm), lambda i, j: (i // per, 0, i % per)),
                   pl.BlockSpec((tm, D), lambda i, j: (i, 0)), pl.BlockSpec((1, D), lambda i, j: (0, 0))),
        scratch_shapes=[pltpu.VMEM((tm, D), F32)],
        name="mlp_bwd", sem=("arbitrary", "arbitrary"), args=(dx2, x1, a, g, wup, wdn))


def _mix_bwd(dx1, ya, yc, pgate, b_gates, wao, wco, wo):
    T, D = dx1.shape
    tm = min(512, T)
    halves = _row_halves(tm)

    def kern(dx_ref, ya_ref, yc_ref, pg_ref, bg_ref, wao_ref, wco_ref, wo_ref,
             datt_ref, dcv_ref, dya_ref, dyc_ref, dgt_ref, dbg_ref):
        @pl.when(pl.program_id(0) == 0)
        def _():
            dbg_ref[...] = jnp.zeros_like(dbg_ref)

        dms = [_nt(dx_ref[rows, :].astype(BF16), wo_ref[...]) for rows in halves]
        dys = []
        for rows, dm in zip(halves, dms):
            sa = jax.nn.sigmoid(pg_ref[rows, 0:D].astype(F32) + bg_ref[:, 0:D])
            sc = jax.nn.sigmoid(pg_ref[rows, D:2 * D].astype(F32) + bg_ref[:, D:2 * D])
            dya = (dm * sa).astype(BF16)
            dyc = (dm * sc).astype(BF16)
            dga = dm * ya_ref[rows, :].astype(F32) * (sa * (1.0 - sa))
            dgc = dm * yc_ref[rows, :].astype(F32) * (sc * (1.0 - sc))
            dya_ref[rows, :] = dya
            dyc_ref[rows, :] = dyc
            dgt_ref[rows, 0:D] = dga.astype(BF16)
            dgt_ref[rows, D:2 * D] = dgc.astype(BF16)
            dbg_ref[:, 0:D] += jnp.sum(dga, axis=0, keepdims=True)
            dbg_ref[:, D:2 * D] += jnp.sum(dgc, axis=0, keepdims=True)
            dys.append((dya, dyc))
        for rows, (dya, dyc) in zip(halves, dys):
            datt_ref[rows, :] = _nt(dya, wao_ref[...]).astype(BF16)
            dcv_ref[rows, :] = _nt(dyc, wco_ref[...]).astype(BF16)

    row = lambda w: pl.BlockSpec((tm, w), lambda i: (i, 0))
    full = lambda a, b: pl.BlockSpec((a, b), lambda i: (0, 0))
    bf = jax.ShapeDtypeStruct((T, D), BF16)
    return pl.pallas_call(
        kern,
        out_shape=(bf, bf, bf, bf, jax.ShapeDtypeStruct((T, GATE_W), BF16), jax.ShapeDtypeStruct((1, GATE_W), F32)),
        grid=(T // tm,),
        in_specs=[row(D), row(D), row(D), row(GATE_W), full(1, GATE_W)] + [_resident((D, D))] * 3,
        out_specs=(row(D), row(D), row(D), row(D), row(GATE_W), full(1, GATE_W)),
        name="mix_bwd", compiler_params=_params("arbitrary"))(dx1, ya, yc, pgate, b_gates, wao, wco, wo)


def _conv_bwd(dcv, pconv, conv_w, conv_b, seq, comm=None):
    T = pconv.shape[0]
    tm = min(512, seq)
    per_seq = seq // tm
    D = D_MODEL
    nb16 = T // 16

    def kern(dcv_ref, dcvn_ref, cur_ref, prev_ref, next_ref, w_ref, b_ref, o_ref, dwb_ref):
        i = pl.program_id(0)

        @pl.when(i == 0)
        def _():
            dwb_ref[...] = jnp.zeros_like(dwb_ref)

        first_i = ((i % per_seq) == 0).astype(jnp.int32)
        keep_next = 1.0 - (((i + 1) % per_seq) == 0).astype(F32)
        cb = cur_ref[:, 0:D].astype(F32)
        cc = cur_ref[:, D:2 * D].astype(F32)
        cu = cur_ref[:, 2 * D:3 * D].astype(F32)
        yc = cc * cu
        halo_yc = prev_ref[:, D:2 * D].astype(F32) * prev_ref[:, 2 * D:3 * D].astype(F32)
        s1, s2 = _conv_taps(yc, halo_yc, first_i)
        w0, w1, w2 = w_ref[0:1, :], w_ref[1:2, :], w_ref[2:3, :]
        z = w0 * s2 + w1 * s1 + w2 * yc
        dcv = dcv_ref[...].astype(F32)
        dz = dcv * cb
        dzn = dcvn_ref[...].astype(F32) * next_ref[:, 0:D].astype(F32) * keep_next
        n1, n2 = _pick_row(dzn, 0), _pick_row(dzn, 1)
        rowid = lax.broadcasted_iota(jnp.int32, dz.shape, 0)
        u1 = jnp.where(rowid == tm - 1, n1, pltpu.roll(dz, tm - 1, 0))
        u2 = jnp.where(rowid == tm - 1, n2, jnp.where(rowid == tm - 2, n1, pltpu.roll(dz, tm - 2, 0)))
        dyc = w2 * dz + w1 * u1 + w0 * u2
        o_ref[:, 0:D] = (dcv * (z + b_ref[...])).astype(BF16)
        o_ref[:, D:2 * D] = (dyc * cu).astype(BF16)
        o_ref[:, 2 * D:3 * D] = (dyc * cc).astype(BF16)
        dwb_ref[0:1, :] += jnp.sum(dz * s2, axis=0, keepdims=True)
        dwb_ref[1:2, :] += jnp.sum(dz * s1, axis=0, keepdims=True)
        dwb_ref[2:3, :] += jnp.sum(dz * yc, axis=0, keepdims=True)
        dwb_ref[3:4, :] += jnp.sum(dz, axis=0, keepdims=True)

    prev_map = lambda i: (jnp.maximum(i * (tm // 16) - 1, 0), 0)
    next_map = lambda i: (jnp.minimum((i + 1) * (tm // 16), nb16 - 1), 0)
    return _hosted_call(
        comm, kern, out_shape=(jax.ShapeDtypeStruct((T, CONV_W), BF16), jax.ShapeDtypeStruct((8, D), F32)), grid=(T // tm,),
        in_specs=[pl.BlockSpec((tm, D), lambda i: (i, 0)), pl.BlockSpec((16, D), next_map),
                  pl.BlockSpec((tm, CONV_W), lambda i: (i, 0)), pl.BlockSpec((16, CONV_W), prev_map),
                  pl.BlockSpec((16, CONV_W), next_map),
                  pl.BlockSpec((3, D), lambda i: (0, 0)), pl.BlockSpec((1, D), lambda i: (0, 0))],
        out_specs=(pl.BlockSpec((tm, CONV_W), lambda i: (i, 0)), pl.BlockSpec((8, D), lambda i: (0, 0))),
        name="conv_bwd", sem=("arbitrary",), args=(dcv, dcv, pconv, pconv, pconv, conv_w, conv_b))


def _attn_bwd(pqkv, datt, lse, sinks, bias, seq, comm=None):
    T = pqkv.shape[0]
    nblk = seq // WINDOW
    nseq = T // seq
    KVW = KV_W

    def kern(sink_ref, cur_ref, prev_ref, do_ref, lse_ref, bias_ref, dq_ref, dkv_ref, ds_ref, kc_scr, vc_scr, dqt_scr):
        b, st = pl.program_id(0), pl.program_id(1)

        @pl.when((b == 0) & (st == 0))
        def _():
            ds_ref[...] = jnp.zeros_like(ds_ref)

        @pl.when(st == 0)
        def _():
            kc_scr[...] = jnp.zeros_like(kc_scr)
            vc_scr[...] = jnp.zeros_like(vc_scr)

        @pl.when(st < nblk)
        def _():
            first_i = (st == 0).astype(jnp.int32)
            groups = range(N_KV_HEADS)
            bands = [_kv_band(cur_ref, prev_ref, hk) for hk in groups]
            qs = [_stack_heads(cur_ref, hk) for hk in groups]
            dos = [_stack_heads(do_ref, hk) for hk in groups]
            sts = [_nt(bands[hk][0], qs[hk] * ATTN_SCALE) + bias_ref[first_i, hk] for hk in groups]
            dps = [_nt(bands[hk][1], dos[hk]) for hk in groups]
            pbs, dsss = [], []
            for hk in groups:
                heads = [GQA_GROUP * hk + g for g in range(GQA_GROUP)]
                sink = _lane_row([sink_ref[h] for h in heads])
                lse_g = jnp.concatenate([lse_ref[h:h + 1, :] for h in heads], axis=1)
                p = jnp.exp(sts[hk] - lse_g)
                d_row = jnp.sum(p * dps[hk], axis=0, keepdims=True)
                dsss.append((p * (dps[hk] - d_row) * ATTN_SCALE).astype(BF16))
                pbs.append(p.astype(BF16))
                psd = jnp.exp(sink - lse_g) * d_row
                for g, h in enumerate(heads):
                    ds_ref[0:1, h:h + 1] -= jnp.sum(psd[:, WINDOW * g:WINDOW * (g + 1)], axis=1, keepdims=True)
            for hk in groups:
                dqt = _tn(bands[hk][0], dsss[hk])
                dk_b = _nn(dsss[hk], qs[hk])
                dv_b = _nn(pbs[hk], dos[hk])
                for g in range(GQA_GROUP):
                    h = GQA_GROUP * hk + g
                    dqt_scr[HEAD_DIM * h:HEAD_DIM * (h + 1), :] = dqt[:, WINDOW * g:WINDOW * (g + 1)].astype(BF16)
                ksl = slice(HEAD_DIM * hk, HEAD_DIM * (hk + 1))
                vsl = slice(KVW + HEAD_DIM * hk, KVW + HEAD_DIM * (hk + 1))
                dkv_ref[:, ksl] = (kc_scr[:, ksl] + dk_b[0:WINDOW]).astype(BF16)
                dkv_ref[:, vsl] = (vc_scr[:, ksl] + dv_b[0:WINDOW]).astype(BF16)
                kc_scr[:, ksl] = dk_b[WINDOW:2 * WINDOW]
                vc_scr[:, ksl] = dv_b[WINDOW:2 * WINDOW]
            dq_ref[...] = dqt_scr[...].T

        @pl.when(st == nblk)
        def _():
            dkv_ref[:, 0:KVW] = kc_scr[...].astype(BF16)
            dkv_ref[:, KVW:2 * KVW] = vc_scr[...].astype(BF16)

    cur_map = lambda b, s: (b * nblk + jnp.minimum(s, nblk - 1), 0)
    prev_row = lambda b, s: b * nblk + jnp.clip(s - 1, 0, nblk - 1)
    return _hosted_call(
        comm, kern,
        out_shape=(jax.ShapeDtypeStruct((T, D_MODEL), BF16), jax.ShapeDtypeStruct((T, 2 * KVW), BF16),
                   jax.ShapeDtypeStruct((8, 128), F32)),
        grid=(nseq, nblk + 1),
        in_specs=[pl.BlockSpec(memory_space=pltpu.SMEM),
                  pl.BlockSpec((WINDOW, QKV_W), cur_map),
                  pl.BlockSpec((WINDOW, 2 * KVW), lambda b, s: (prev_row(b, s), 2)),
                  pl.BlockSpec((WINDOW, D_MODEL), cur_map),
                  pl.BlockSpec((N_Q_HEADS, WINDOW), lambda b, s: (0, b * nblk + jnp.minimum(s, nblk - 1))),
                  _resident(bias.shape)],
        out_specs=(pl.BlockSpec((WINDOW, D_MODEL), cur_map),
                   pl.BlockSpec((WINDOW, 2 * KVW), lambda b, s: (prev_row(b, s), 0)),
                   pl.BlockSpec((8, 128), lambda b, s: (0, 0))),
        scratch_shapes=[pltpu.VMEM((WINDOW, KVW), F32), pltpu.VMEM((WINDOW, KVW), F32), pltpu.VMEM((D_MODEL, WINDOW), BF16)],
        name="attn_bwd", sem=("arbitrary", "arbitrary"), args=(sinks, pqkv, pqkv, datt, lse, bias))


def _piece_tiles(pieces):
    out, start = [], 0
    for arr, width in pieces:
        out.append((arr, start, width // COL_TILE))
        start += width // COL_TILE
    return out, start


def _inproj_bwd(pieces, w_in, x, dx_in, g, comm=None):
    T, D = x.shape
    tm = min(512, T)

    def kern(*refs):
        p_refs = refs[:len(pieces)]
        w_ref, x_ref, dxin_ref, g_ref, dx_ref, dg_ref = refs[len(pieces):]

        @pl.when(pl.program_id(0) == 0)
        def _():
            dg_ref[...] = jnp.zeros_like(dg_ref)

        dh, off = None, 0
        for p_ref, (_, width) in zip(p_refs, pieces):
            part = _nt(p_ref[...], w_ref[:, off:off + width])
            dh = part if dh is None else dh + part
            off += width
        gg = g_ref[...]
        r, xh = _rms_stats(x_ref[...])
        dx, dg = _rms_bwd(dh, xh, r, gg)
        dx_ref[...] = dxin_ref[...] + dx
        dg_ref[...] += dg

    row = lambda wd: pl.BlockSpec((tm, wd), lambda i: (i, 0))
    return _hosted_call(
        comm, kern, out_shape=(jax.ShapeDtypeStruct((T, D), F32), jax.ShapeDtypeStruct((1, D), F32)), grid=(T // tm,),
        in_specs=[row(wd) for _, wd in pieces] + [_resident((D, IN_COLS)), row(D), row(D), pl.BlockSpec((1, D), lambda i: (0, 0))],
        out_specs=(row(D), pl.BlockSpec((1, D), lambda i: (0, 0))),
        name="inproj_bwd", sem=("arbitrary",), args=(*[a for a, _ in pieces], w_in, x, dx_in, g))


def _dw_pieces(lhs_t, pieces):
    nt, K, tk = lhs_t.shape
    tiles, nj = _piece_tiles(pieces)

    def kern(*refs):
        lhs_ref = refs[0]
        p_refs = refs[1:1 + len(tiles)]
        o_ref, ob_ref = refs[1 + len(tiles):]
        j, t = pl.program_id(0), pl.program_id(1)

        @pl.when(t == 0)
        def _():
            o_ref[...] = jnp.zeros_like(o_ref)

        for p_ref, (_, start, n) in zip(p_refs, tiles):
            @pl.when((j >= start) & (j < start + n))
            def _(p_ref=p_ref):
                o_ref[...] += _nn(lhs_ref[t], p_ref[...])

        @pl.when(t == nt - 1)
        def _():
            ob_ref[...] = o_ref[...].astype(BF16)

    def p_map(start, n):
        return lambda j, t: (jnp.where((j >= start) & (j < start + n), t, 0), jnp.clip(j - start, 0, n - 1))

    N = nj * COL_TILE
    return pl.pallas_call(
        kern, out_shape=(jax.ShapeDtypeStruct((K, N), F32), jax.ShapeDtypeStruct((K, N), BF16)), grid=(nj, nt),
        in_specs=[_resident((nt, K, tk))] + [pl.BlockSpec((tk, COL_TILE), p_map(s, n)) for _, s, n in tiles],
        out_specs=(pl.BlockSpec((K, COL_TILE), lambda j, t: (0, j)), pl.BlockSpec((K, COL_TILE), lambda j, t: (0, j))),
        name="dw_pieces", compiler_params=_params("arbitrary", "arbitrary"))(lhs_t, *[a for a, _, _ in tiles])


def _dw(lhs_t, rhs):
    lhs_res, rhs_res = lhs_t.ndim == 3, rhs.ndim == 3
    W = D_MODEL
    if lhs_res:
        nt, K, tk = lhs_t.shape
    else:
        K, tk = lhs_t.shape[0], _dw_chunk(lhs_t.shape[1])
        nt = lhs_t.shape[1] // tk
    N = rhs.shape[-1]

    def kern(lhs_ref, rhs_ref, o_ref, ob_ref):
        t = pl.program_id(2)

        @pl.when(t == 0)
        def _():
            o_ref[...] = jnp.zeros_like(o_ref)

        a = lhs_ref[t] if lhs_res else lhs_ref[...]
        b = rhs_ref[t] if rhs_res else rhs_ref[...]
        o_ref[...] += _nn(a, b.astype(BF16))

        @pl.when(t == nt - 1)
        def _():
            ob_ref[...] = o_ref[...].astype(BF16)

    omap = lambda i, j, t: (i, j)
    lspec = _resident((nt, W, tk)) if lhs_res else pl.BlockSpec((W, tk), lambda i, j, t: (i, t))
    rspec = _resident((nt, tk, W)) if rhs_res else pl.BlockSpec((tk, W), lambda i, j, t: (t, j))
    return pl.pallas_call(
        kern, out_shape=(jax.ShapeDtypeStruct((K, N), F32), jax.ShapeDtypeStruct((K, N), BF16)), grid=(K // W, N // W, nt),
        in_specs=[lspec, rspec], out_specs=(pl.BlockSpec((W, W), omap), pl.BlockSpec((W, W), omap)),
        name="dw", compiler_params=_params("arbitrary", "arbitrary", "arbitrary"))(lhs_t, rhs)


BIG = (("w_in", D_MODEL, IN_COLS, "col"), ("w_attn_out", D_MODEL, D_MODEL, "row"), ("w_conv_out", D_MODEL, D_MODEL, "row"),
       ("w_o", D_MODEL, D_MODEL, "row"), ("w_up", D_MODEL, D_FF, "col"), ("w_down", D_FF, D_MODEL, "row"))


def _shard_dims(rows, cols, kind):
    return (rows, cols // N_CHIP) if kind in ("col", "chip") else (rows // N_CHIP, cols)


def _window(ref, rows, cols, kind, chip, half):
    sr, sc = _shard_dims(rows, cols, kind)
    hr = sr // 2
    if kind == "col":
        return ref.at[pl.ds(half * hr, hr), pl.ds(chip * sc, sc)]
    if kind == "chip":
        return ref.at[chip, pl.ds(half * hr, hr), :]
    return ref.at[pl.ds(chip * sr + half * hr, hr), :]


def _mesh_pos():
    x, y, c = lax.axis_index("x"), lax.axis_index("y"), lax.axis_index("c")
    return x, y, c, 2 * x + y


_REL_BITS = (2, 1, 3)


def _rel_dev(x, y, c, r):
    return ((1 - x, y, c), (x, 1 - y, c), (1 - x, 1 - y, c))[r]


def _for_my_chip(j, fn):
    for js in range(N_CHIP):
        pl.when(j == js)(functools.partial(fn, js))


def _cast_into_full(j_arr, shard, l, rows, cols, kind):
    sr, sc = _shard_dims(rows, cols, kind)
    tr = min(256, sr)

    def kern(j_ref, s_ref, o_ref):
        o_ref[...] = s_ref[...].astype(BF16)

    shape, block = (rows, cols), (tr, sc)
    if kind == "col":
        omap = lambda i, j_ref: (i, j_ref[0])
    elif kind == "chip":
        shape, block = (N_CHIP, rows, sc), (None, tr, sc)
        omap = lambda i, j_ref: (j_ref[0], i, 0)
    else:
        omap = lambda i, j_ref: (j_ref[0] * (sr // tr) + i, 0)
    gs = pltpu.PrefetchScalarGridSpec(
        num_scalar_prefetch=1, grid=(sr // tr,),
        in_specs=[pl.BlockSpec((None, tr, sc), lambda i, j_ref: (l, i, 0))], out_specs=pl.BlockSpec(block, omap))
    return pl.pallas_call(kern, out_shape=jax.ShapeDtypeStruct(shape, BF16), grid_spec=gs, name="cast_into_full",
                          compiler_params=_params("arbitrary"))(j_arr, shard)


def _gather_comm(fulls, cw=None):
    n_big = len(fulls)
    n_piece = n_big + (0 if cw is None else 1)

    def pieces(in_refs, o_refs, js, c):
        def piece(p, chip, half):
            if p == n_big:
                return o_refs[p].at[half, chip]
            _, rows, cols, kind = fulls[p]
            return _window(o_refs[p], rows, cols, kind, chip, half)

        def mine(p):
            return in_refs[p].at[c] if p == n_big else piece(p, js, c)

        return piece, mine

    def local_copies(in_refs, piece, js, loc_sem):
        if cw is None:
            return []
        return [pltpu.make_async_copy(in_refs[n_big].at[half], piece(n_big, js, half), loc_sem.at[half]) for half in range(2)]

    def ici_copy(piece, mine, js, x, y, c, r, p, send_sem, recv_sem):
        return pltpu.make_async_remote_copy(mine(p), piece(p, js, c), send_sem.at[r * n_piece + p], recv_sem.at[r * n_piece + p],
                                            _rel_dev(x, y, c, r), MESH)

    def start(in_refs, o_refs, sems):
        send_sem, recv_sem, _, _, loc_sem = sems
        x, y, c, j = _mesh_pos()

        def run(js):
            piece, mine = pieces(in_refs, o_refs, js, c)
            for cp in local_copies(in_refs, piece, js, loc_sem):
                cp.start()
            for r in range(3):
                for p in range(n_piece):
                    ici_copy(piece, mine, js, x, y, c, r, p, send_sem, recv_sem).start()

        _for_my_chip(j, run)

    def finish(in_refs, o_refs, sems):
        send_sem, recv_sem, fsend_sem, frecv_sem, loc_sem = sems
        x, y, c, j = _mesh_pos()

        def run(js):
            piece, mine = pieces(in_refs, o_refs, js, c)
            fwds = []
            for r in range(3):
                ks = js ^ _REL_BITS[r]
                for p in range(n_piece):
                    got = piece(p, ks, c)
                    pltpu.make_async_remote_copy(got, got, send_sem.at[r * n_piece + p], recv_sem.at[r * n_piece + p],
                                                 _rel_dev(x, y, c, r), MESH).wait_recv()
                    cp = pltpu.make_async_remote_copy(got, got, fsend_sem.at[r * n_piece + p], frecv_sem.at[r * n_piece + p],
                                                      (x, y, 1 - c), MESH)
                    cp.start()
                    fwds.append(cp)
            for r in range(3):
                ks = js ^ _REL_BITS[r]
                for p in range(n_piece):
                    got = piece(p, ks, 1 - c)
                    pltpu.make_async_remote_copy(got, got, fsend_sem.at[r * n_piece + p], frecv_sem.at[r * n_piece + p],
                                                 (x, y, 1 - c), MESH).wait_recv()
            for r in range(3):
                for p in range(n_piece):
                    ici_copy(piece, mine, js, x, y, c, r, p, send_sem, recv_sem).wait_send()
            for cp in fwds:
                cp.wait_send()
            for cp in local_copies(in_refs, piece, js, loc_sem):
                cp.wait()

        _for_my_chip(j, run)

    out_shape = [jax.ShapeDtypeStruct(a.shape, BF16) for a, _, _, _ in fulls]
    ins = [a for a, _, _, _ in fulls]
    if cw is not None:
        out_shape.append(jax.ShapeDtypeStruct((DEPTH, N_CHIP, 3, D_MODEL // N_CHIP), F32))
        ins.append(cw)
    scratch = [pltpu.SemaphoreType.DMA((3 * n_piece,))] * 4 + [pltpu.SemaphoreType.DMA((2,))]
    return _Hosted(ins, out_shape, {p: p for p in range(n_big)}, scratch, start, finish)


def _compose(*comms):
    comms = [cm for cm in comms if cm is not None]
    if len(comms) <= 1:
        return comms[0] if comms else None
    ins, outs, aliases, scratch, cuts = [], [], {}, [], []
    for cm in comms:
        cuts.append((len(ins), len(outs), len(scratch)))
        aliases.update({len(ins) + i: len(outs) + o for i, o in cm.aliases.items()})
        ins, outs, scratch = ins + cm.inputs, outs + cm.out_shape, scratch + cm.scratch

    def parts(a, b, s):
        for cm, (i0, o0, s0) in zip(comms, cuts):
            yield cm, a[i0:i0 + len(cm.inputs)], b[o0:o0 + len(cm.out_shape)], s[s0:s0 + len(cm.scratch)]

    def start(a, b, s):
        for cm, pa, pb, ps in parts(a, b, s):
            cm.start(pa, pb, ps)

    def finish(a, b, s):
        for cm, pa, pb, ps in parts(a, b, s):
            cm.finish(pa, pb, ps)

    return _Hosted(ins, outs, aliases, scratch, start, finish)


def _run_comm(comm, name):
    n_in, n_out = len(comm.inputs), len(comm.out_shape)

    def body(*refs):
        comm.start(refs[:n_in], refs[n_in:n_in + n_out], refs[n_in + n_out:])
        comm.finish(refs[:n_in], refs[n_in:n_in + n_out], refs[n_in + n_out:])

    return pl.pallas_call(body, out_shape=comm.out_shape, in_specs=[ANY] * n_in, out_specs=[ANY] * n_out,
                          input_output_aliases=comm.aliases, scratch_shapes=comm.scratch, name=name)(*comm.inputs)


def _sibling_exchange_comm(gb):
    n = len(gb)

    def copies(g_refs, o_refs, sems, c):
        send_sem, recv_sem = sems
        x, y, _, _ = _mesh_pos()
        return [pltpu.make_async_remote_copy(_window(g_refs[t], rows, cols, kind, chip, 1 - c),
                                             _window(o_refs[t], rows, cols, kind, chip, 1 - c),
                                             send_sem.at[N_CHIP * t + chip], recv_sem.at[N_CHIP * t + chip], (x, y, 1 - c), MESH)
                for t, (_, rows, cols, kind) in enumerate(gb) for chip in range(N_CHIP)]

    def start(g_refs, o_refs, sems):
        for cp in copies(g_refs, o_refs, sems, lax.axis_index("c")):
            cp.start()

    def finish(g_refs, o_refs, sems):
        c = lax.axis_index("c")
        for cp in copies(g_refs, o_refs, sems, 1 - c):
            cp.wait_recv()
        for cp in copies(g_refs, o_refs, sems, c):
            cp.wait_send()

    return _Hosted([a for a, _, _, _ in gb], [jax.ShapeDtypeStruct(a.shape, BF16) for a, _, _, _ in gb], {},
                   [pltpu.SemaphoreType.DMA((N_CHIP * n,))] * 2, start, finish)


def _half_add(jc_arr, g, sib, rows, cols, kind):
    sr, sc = _shard_dims(rows, cols, kind)
    hr = sr // 2

    def kern(jc_ref, g_ref, s_ref, ob_ref, of_ref):
        v = g_ref[...] + s_ref[...].astype(F32)
        ob_ref[...] = v.astype(BF16)

        @pl.when(pl.program_id(0) == jc_ref[0])
        def _():
            of_ref[...] = v

    if kind == "col":
        imap = lambda j, jc_ref: (jc_ref[1], j)
    else:
        imap = lambda j, jc_ref: (2 * j + jc_ref[1], 0)
    gs = pltpu.PrefetchScalarGridSpec(
        num_scalar_prefetch=1, grid=(N_CHIP,),
        in_specs=[pl.BlockSpec((hr, sc), imap), pl.BlockSpec((hr, sc), imap)],
        out_specs=[pl.BlockSpec((None, hr, sc), lambda j, jc_ref: (j, 0, 0)), pl.BlockSpec((hr, sc), lambda j, jc_ref: (0, 0))])
    return pl.pallas_call(
        kern, out_shape=(jax.ShapeDtypeStruct((N_CHIP, hr, sc), BF16), jax.ShapeDtypeStruct((hr, sc), F32)),
        grid_spec=gs, name="grad_half_add", compiler_params=_params("arbitrary"))(jc_arr, g, sib)


def _chip_exchange_comm(sbs):
    n = len(sbs)

    def copies(s_refs, o_refs, sems):
        send_sem, recv_sem = sems
        x, y, c, j = _mesh_pos()
        return [pltpu.make_async_remote_copy(s_refs[t].at[j ^ _REL_BITS[r]], o_refs[t].at[r], send_sem.at[n * r + t],
                                             recv_sem.at[n * r + t], _rel_dev(x, y, c, r), MESH)
                for r in range(3) for t in range(n)]

    def start(s_refs, o_refs, sems):
        for cp in copies(s_refs, o_refs, sems):
            cp.start()

    def finish(s_refs, o_refs, sems):
        for cp in copies(s_refs, o_refs, sems):
            cp.wait()

    return _Hosted(sbs, [jax.ShapeDtypeStruct((3,) + a.shape[1:], BF16) for a in sbs], {},
                   [pltpu.SemaphoreType.DMA((3 * n,))] * 2, start, finish)


def _owner_sum(jc_arr, sf, rb, l, into=None):
    hr, sc = sf.shape

    def kern(jc_ref, s_ref, r0_ref, r1_ref, r2_ref, *rest):
        o_ref = rest[-1]
        o_ref[...] = ((s_ref[...] + r0_ref[...].astype(F32)) + r1_ref[...].astype(F32)) + r2_ref[...].astype(F32)

    in_specs = [pl.BlockSpec((hr, sc), lambda i, jc_ref: (0, 0))]
    in_specs += [pl.BlockSpec((None, hr, sc), lambda i, jc_ref, r=r: (r, 0, 0)) for r in range(3)]
    args = [jc_arr, sf, rb, rb, rb]
    aliases = {}
    if into is not None:
        in_specs.append(ANY)
        args.append(into)
        aliases = {len(args) - 1: 0}
    gs = pltpu.PrefetchScalarGridSpec(
        num_scalar_prefetch=1, grid=(1,), in_specs=in_specs,
        out_specs=pl.BlockSpec((None, hr, sc), lambda i, jc_ref: (l, jc_ref[1], 0)))
    return pl.pallas_call(kern, out_shape=jax.ShapeDtypeStruct((DEPTH, 2 * hr, sc), F32), grid_spec=gs,
                          input_output_aliases=aliases, name="grad_owner_sum", compiler_params=_params("arbitrary"))(*args)


def _sibling_assemble_comm(grads, layers):
    n = len(grads)
    todo = [(q, l) for q in range(n) for l in layers[q]]

    def copies(o_refs, sems, half):
        send_sem, recv_sem = sems
        x, y, c, _ = _mesh_pos()
        out = []
        for k, (q, l) in enumerate(todo):
            hr = grads[q].shape[1] // 2
            w = o_refs[q].at[l, pl.ds(half * hr, hr), :]
            out.append(pltpu.make_async_remote_copy(w, w, send_sem.at[k], recv_sem.at[k], (x, y, 1 - c), MESH))
        return out

    def start(_, o_refs, sems):
        for cp in copies(o_refs, sems, lax.axis_index("c")):
            cp.start()

    def finish(_, o_refs, sems):
        c = lax.axis_index("c")
        for cp in copies(o_refs, sems, 1 - c):
            cp.wait_recv()
        for cp in copies(o_refs, sems, c):
            cp.wait_send()

    return _Hosted(grads, [jax.ShapeDtypeStruct(g.shape, F32) for g in grads], {q: q for q in range(n)},
                   [pltpu.SemaphoreType.DMA((len(todo),))] * 2, start, finish)


def _adamw_math(w, g, m, v):
    m = ADAM_B1 * m + (1.0 - ADAM_B1) * g
    v = ADAM_B2 * v + (1.0 - ADAM_B2) * jnp.square(g)
    m_hat = m / (1.0 - ADAM_B1 ** ADAM_STEP)
    v_hat = v / (1.0 - ADAM_B2 ** ADAM_STEP)
    delta = -ADAM_LR * (m_hat / (jnp.sqrt(v_hat) + ADAM_EPS) + ADAM_WD * w)
    return delta, m, v


def _adamw(w, g, m, v):
    shape = w.shape
    C = shape[-1]
    R = int(np.prod(shape[:-1]))
    tr = min(256, R)
    args = [a.reshape(R, C) for a in (w, g, m, v)]

    def kern(w_ref, g_ref, m_ref, v_ref, go_ref, d_ref, nm_ref, nv_ref):
        g_val = g_ref[...]
        d, nm, nv = _adamw_math(w_ref[...], g_val, m_ref[...], v_ref[...])
        go_ref[...] = g_val
        d_ref[...] = d
        nm_ref[...] = nm
        nv_ref[...] = nv

    spec = pl.BlockSpec((tr, C), lambda i: (i, 0))
    outs = pl.pallas_call(
        kern, out_shape=[jax.ShapeDtypeStruct((R, C), F32)] * 4, grid=(R // tr,), in_specs=[spec] * 4, out_specs=[spec] * 4,
        name="adamw", compiler_params=_params("parallel"))(*args)
    return [o.reshape(shape) for o in outs]


_ROW_G_MIX, _ROW_B_GATES, _ROW_SINKS, _ROW_CONV, _ROW_G_MLP, _ROW_G_FINAL, _ROW_LOSS = 0, 2, 6, 8, 16, 18, 19


def _small_step(parts, params, moms, vels):
    names = ["g_mix", "b_gates", "sinks", "conv_w", "conv_b", "g_mlp", "g_final"]
    D = D_MODEL
    QW = D // N_CHIP
    n_dev = 8

    def body(*refs):
        it = iter(refs)
        dgmix = [next(it) for _ in range(DEPTH)]
        dbg = [next(it) for _ in range(DEPTH)]
        dsk = [next(it) for _ in range(DEPTH)]
        dwb = [next(it) for _ in range(DEPTH)]
        dgmlp = [next(it) for _ in range(DEPTH)]
        lst = next(it)
        p_refs = {n: next(it) for n in names}
        m_refs = {n: next(it) for n in names}
        v_refs = {n: next(it) for n in names}
        loss_ref = next(it)
        outs = {n: [next(it) for _ in range(4)] for n in names}
        pack_ref, all_ref, send_sem, recv_sem = next(it), next(it), next(it), next(it)

        x, y, c, j = _mesh_pos()
        me = 4 * x + 2 * y + c
        pack_ref[...] = jnp.zeros_like(pack_ref)
        for l in range(DEPTH):
            pack_ref[_ROW_G_MIX + l:_ROW_G_MIX + l + 1, :] = dgmix[l][...]
            pack_ref[_ROW_B_GATES + 2 * l:_ROW_B_GATES + 2 * l + 1, :] = dbg[l][:, 0:D]
            pack_ref[_ROW_B_GATES + 2 * l + 1:_ROW_B_GATES + 2 * l + 2, :] = dbg[l][:, D:2 * D]
            pack_ref[_ROW_SINKS + l:_ROW_SINKS + l + 1, 0:128] = dsk[l][0:1, :]
            pack_ref[_ROW_CONV + 4 * l:_ROW_CONV + 4 * l + 4, :] = dwb[l][0:4, :]
            pack_ref[_ROW_G_MLP + l:_ROW_G_MLP + l + 1, :] = dgmlp[l][...]
        pack_ref[_ROW_G_FINAL:_ROW_G_FINAL + 1, :] = lst[0:1, :]
        pack_ref[_ROW_LOSS:_ROW_LOSS + 1, :] = lst[1:2, :]

        all_ref[me] = pack_ref[...]
        cps = []
        for k in range(1, n_dev):
            dx_, dy_, dc_ = (k >> 2) & 1, (k >> 1) & 1, k & 1
            peer = (x ^ dx_, y ^ dy_, c ^ dc_)
            cp = pltpu.make_async_remote_copy(pack_ref, all_ref.at[me], send_sem.at[k - 1], recv_sem.at[k - 1], peer, MESH)
            cp.start()
            cps.append(cp)
        for cp in cps:
            cp.wait()

        tot = all_ref[0]
        for d in range(1, n_dev):
            tot = tot + all_ref[d]
        pack_ref[...] = tot

        loss_ref[...] = pack_ref[_ROW_LOSS:_ROW_LOSS + 1, 0:1]

        def finish(name, idx, g):
            w, m, v = p_refs[name][idx], m_refs[name][idx], v_refs[name][idx]
            d, nm, nv = _adamw_math(w, g, m, v)
            for ref, val in zip(outs[name], (g, d, nm, nv)):
                ref[idx] = val

        for l in range(DEPTH):
            finish("g_mix", (slice(l, l + 1), slice(None)), pack_ref[_ROW_G_MIX + l:_ROW_G_MIX + l + 1, :])
            finish("g_mlp", (slice(l, l + 1), slice(None)), pack_ref[_ROW_G_MLP + l:_ROW_G_MLP + l + 1, :])
            finish("conv_b", (slice(l, l + 1), slice(None)), pack_ref[_ROW_CONV + 4 * l + 3:_ROW_CONV + 4 * l + 4, :])
            finish("sinks", (slice(l, l + 1), slice(None)), pack_ref[_ROW_SINKS + l:_ROW_SINKS + l + 1, 0:N_Q_HEADS])
            for hf in range(2):
                finish("b_gates", (slice(l, l + 1), slice(hf * D, (hf + 1) * D)),
                       pack_ref[_ROW_B_GATES + 2 * l + hf:_ROW_B_GATES + 2 * l + hf + 1, :])
        finish("g_final", (slice(0, 1), slice(None)), pack_ref[_ROW_G_FINAL:_ROW_G_FINAL + 1, :])

        def conv_w_chip(js):
            for l in range(DEPTH):
                for k in range(3):
                    row = _ROW_CONV + 4 * l + k
                    finish("conv_w", (l, slice(k, k + 1), slice(None)), pack_ref[row:row + 1, js * QW:(js + 1) * QW])

        _for_my_chip(j, conv_w_chip)

    vm = pl.BlockSpec(memory_space=pltpu.VMEM)
    ins = (parts["g_mix"] + parts["b_gates"] + parts["sinks"] + parts["conv"] + parts["g_mlp"] + [parts["loss"]]
           + [params[n] for n in names] + [moms[n] for n in names] + [vels[n] for n in names])
    out_shape = [jax.ShapeDtypeStruct((1, 1), F32)]
    for n in names:
        out_shape += [jax.ShapeDtypeStruct(params[n].shape, F32)] * 4
    res = pl.pallas_call(
        body, out_shape=out_shape, in_specs=[vm] * len(ins), out_specs=[vm] * len(out_shape),
        scratch_shapes=[pltpu.VMEM((SMALL_ROWS, D), F32), pltpu.VMEM((n_dev, SMALL_ROWS, D), F32),
                        pltpu.SemaphoreType.DMA((n_dev - 1,)), pltpu.SemaphoreType.DMA((n_dev - 1,))],
        name="small_allreduce_adamw")(*ins)
    loss = res[0]
    out = {n: res[1 + 4 * i:5 + 4 * i] for i, n in enumerate(names)}
    return loss, out


def kernel(x, g_mix, w_in, b_gates, sinks, w_attn_out, conv_w, conv_b, w_conv_out, w_o, g_mlp, w_up, w_down, g_final, loss_target, m_g_mix, m_w_in, m_b_gates, m_sinks, m_w_attn_out, m_conv_w, m_conv_b, m_w_conv_out, m_w_o, m_g_mlp, m_w_up, m_w_down, m_g_final, v_g_mix, v_w_in, v_b_gates, v_sinks, v_w_attn_out, v_conv_w, v_conv_b, v_w_conv_out, v_w_o, v_g_mlp, v_w_up, v_w_down, v_g_final):
    B, S, D = x.shape
    T = B * S
    big_w = dict(w_in=w_in, w_attn_out=w_attn_out, w_conv_out=w_conv_out, w_o=w_o, w_up=w_up, w_down=w_down)
    big_m = dict(w_in=m_w_in, w_attn_out=m_w_attn_out, w_conv_out=m_w_conv_out, w_o=m_w_o, w_up=m_w_up, w_down=m_w_down)
    big_v = dict(w_in=v_w_in, w_attn_out=v_w_attn_out, w_conv_out=v_w_conv_out, w_o=v_w_o, w_up=v_w_up, w_down=v_w_down)

    c_arr = lax.axis_index("c").astype(jnp.int32).reshape(1)
    j_arr = (2 * lax.axis_index("x") + lax.axis_index("y")).astype(jnp.int32).reshape(1)
    jc_arr = jnp.concatenate([j_arr, c_arr])
    order = [(n, l) for n, _, _, _ in BIG for l in range(DEPTH)]
    dims = {n: (r, c_, k) for n, r, c_, k in BIG}

    wdims = dict(dims, w_up=(D, D_FF, "chip"))
    full = {(n, l): _cast_into_full(j_arr, big_w[n], l, *wdims[n]) for n, l in order}
    mixers = ("w_attn_out", "w_conv_out", "w_o")

    def gather_of(keys):
        return _gather_comm([(full[k],) + wdims[k[0]] for k in keys])

    carried = {("proj", 0): [(n, 0) for n in mixers] + [("w_up", 0)], ("attn", 0): [("w_down", 0)],
               ("mlp", 0): [("w_in", 1)] + [(n, 1) for n in mixers] + [("w_up", 1)], ("attn", 1): [("w_down", 1)]}

    def carry(fn, where, *args):
        keys = carried.get(where)
        if keys is None:
            return fn(*args)
        res, got = fn(*args, comm=gather_of(keys))
        full.update(zip(keys, got))
        return res

    first = _run_comm(_gather_comm([(full["w_in", 0],) + dims["w_in"]], conv_w), "gather_weights")
    full["w_in", 0] = first[0]
    conv_w_full = jnp.transpose(first[1], (0, 2, 1, 3)).reshape(DEPTH, 3, D)
    attn_bias = _attn_bias_table()

    xs = [x.reshape(T, D)]
    saved = []
    for l in range(DEPTH):
        ht, pqkv, pconv, pgate = carry(_norm_proj, ("proj", l), xs[-1], g_mix[l:l + 1], full["w_in", l])
        att, att_t, lse = carry(_attn_fwd, ("attn", l), pqkv, sinks[l], attn_bias, S)
        x1, ya, yc, mg_t, cv_t = _mix_fwd(xs[-1], att, pconv, pgate, conv_w_full[l], conv_b[l:l + 1], b_gates[l:l + 1],
                                          full["w_attn_out", l], full["w_conv_out", l], full["w_o", l], S)
        x2, a = carry(_mlp_fwd, ("mlp", l), x1, g_mlp[l:l + 1], full["w_up", l], full["w_down", l].reshape(N_CHIP, D, D))
        saved.append(dict(ht=ht, pqkv=pqkv, pconv=pconv, pgate=pgate, att=att, att_t=att_t, lse=lse, cv_t=cv_t, x1=x1, ya=ya,
                          yc=yc, mg_t=mg_t, a=a))
        xs.append(x2)

    loss_stats, dx = _loss_bwd(xs[-1], g_final.reshape(1, D), loss_target.reshape(T, D))

    parts = dict(g_mix=[None] * DEPTH, b_gates=[None] * DEPTH, sinks=[None] * DEPTH, conv=[None] * DEPTH,
                 g_mlp=[None] * DEPTH, loss=loss_stats)
    gf, gb, pre, got, mine = {}, {}, {}, {}, {}

    def sibling_exchange(keys):
        return _sibling_exchange_comm([(gb[k],) + dims[k[0]] for k in keys])

    def half_adds(keys, sib):
        for k, s in zip(keys, sib):
            pre[k] = _half_add(jc_arr, gf[k], s, *dims[k[0]])

    def chip_exchange(keys):
        return _chip_exchange_comm([pre[k][0] for k in keys])

    def owner_sums(keys):
        for n, l in keys:
            mine[n] = _owner_sum(jc_arr, pre[n, l][1], got[n, l], l, mine.get(n))

    def run(fn, comms, *args):
        if not comms:
            return fn(*args), []
        res, arrived = fn(*args, comm=_compose(*comms))
        outs, pos = [], 0
        for cm in comms:
            outs.append(arrived[pos:pos + len(cm.out_shape)])
            pos += len(cm.out_shape)
        return res, outs

    assert DEPTH == 2
    upper = [(n, 1) for n, _, _, _ in BIG]
    early = [(n, 0) for n in mixers + ("w_up", "w_down")]
    for l in reversed(range(DEPTH)):
        W = {n: full[(n, l)] for n in big_w}
        sv = saved[l]
        last = l == 0
        mlp_args = (dx, sv["x1"], sv["a"], g_mlp[l:l + 1], W["w_up"], W["w_down"].reshape(N_CHIP, D, D))
        (dx1, da, u_t, h2_t, dyb, parts["g_mlp"][l]), arrived = run(_mlp_bwd, [chip_exchange(upper)] if last else [], *mlp_args)
        if last:
            got.update(zip(upper, arrived[0]))
        gf["w_up", l], gb["w_up", l] = _dw(h2_t, da)
        gf["w_down", l], gb["w_down", l] = _dw(u_t, dyb.reshape(-1, _dw_chunk(T), D))
        datt, dcv, dya, dyc, dgate, parts["b_gates"][l] = _mix_bwd(
            dx1, sv["ya"], sv["yc"], sv["pgate"], b_gates[l:l + 1], W["w_attn_out"], W["w_conv_out"], W["w_o"])
        gf["w_o", l], gb["w_o", l] = _dw(sv["mg_t"], dx1)
        gf["w_attn_out", l], gb["w_attn_out", l] = _dw(sv["att_t"], dya)
        gf["w_conv_out", l], gb["w_conv_out", l] = _dw(sv["cv_t"], dyc)
        conv_args = (dcv, sv["pconv"], conv_w_full[l], conv_b[l:l + 1], S)
        (dconv, parts["conv"][l]), arrived = run(_conv_bwd, [sibling_exchange(early)] if last else [], *conv_args)
        if last:
            half_adds(early, arrived[0])
        attn_args = (sv["pqkv"], datt, sv["lse"], sinks[l], attn_bias, S)
        (dq, dkv, parts["sinks"][l]), arrived = run(_attn_bwd, [chip_exchange(early)] if last else [], *attn_args)
        if last:
            got.update(zip(early, arrived[0]))
            owner_sums(upper + early)
        pieces = [(dq, D), (dkv, QKV_W - D), (dconv, CONV_W), (dgate, GATE_W)]
        gf["w_in", l], gb["w_in", l] = _dw_pieces(sv["ht"], pieces)
        in_args = (pieces, W["w_in"], xs[l], dx1, g_mix[l:l + 1])
        if not last:
            (dx, parts["g_mix"][l]), arrived = run(_inproj_bwd, [sibling_exchange(upper)], *in_args)
            half_adds(upper, arrived[0])
        else:
            tail = [("w_in", 0)]
            half_adds(tail, _run_comm(sibling_exchange(tail), "grad_sibling_exchange"))
            done = _sibling_assemble_comm(list(mine.values()), [(1,) if n == "w_in" else (0, 1) for n in mine])
            (dx, parts["g_mix"][l]), arrived = run(_inproj_bwd, [chip_exchange(tail), done], *in_args)
            got.update(zip(tail, arrived[0]))
            mine = dict(zip(mine, arrived[1]))
            owner_sums(tail)
            mine["w_in"] = _run_comm(_sibling_assemble_comm([mine["w_in"]], [(0,)]), "grad_sibling_assemble")[0]
    grads = mine

    res = {}
    for n in big_w:
        res[n] = tuple(_adamw(big_w[n], grads[n], big_m[n], big_v[n]))

    small_p = dict(g_mix=g_mix, b_gates=b_gates, sinks=sinks, conv_w=conv_w, conv_b=conv_b, g_mlp=g_mlp, g_final=g_final.reshape(1, D))
    small_m = dict(g_mix=m_g_mix, b_gates=m_b_gates, sinks=m_sinks, conv_w=m_conv_w, conv_b=m_conv_b, g_mlp=m_g_mlp,
                   g_final=m_g_final.reshape(1, D))
    small_v = dict(g_mix=v_g_mix, b_gates=v_b_gates, sinks=v_sinks, conv_w=v_conv_w, conv_b=v_conv_b, g_mlp=v_g_mlp,
                   g_final=v_g_final.reshape(1, D))
    loss, small = _small_step(parts, small_p, small_m, small_v)
    for n, vals in small.items():
        res[n] = tuple(v.reshape(D) for v in vals) if n == "g_final" else tuple(vals)

    weights = ["g_mix", "w_in", "b_gates", "sinks", "w_attn_out", "conv_w", "conv_b", "w_conv_out", "w_o", "g_mlp", "w_up",
               "w_down", "g_final"]
    out = [loss.reshape(()), dx.reshape(B, S, D)]
    for k in range(4):
        out += [res[n][k] for n in weights]
    return tuple(out)
```

```python
import functools

import numpy as np
import jax
import jax.numpy as jnp
from jax import lax
from jax.experimental import pallas as pl
from jax.experimental.pallas import tpu as pltpu

F32 = jnp.float32
BF16 = jnp.bfloat16

D_MODEL = 1024
HEAD_DIM = 64
N_Q_HEADS = 16
N_KV_HEADS = 4
GQA_GROUP = 4
WINDOW = 128
D_FF = 4096
DEPTH = 2
RMS_EPS = 1e-6
NEG_INF = -1e30
ATTN_SCALE = HEAD_DIM ** -0.5
QKV_W = 1536
CONV_W = 3072
GATE_W = 2048
IN_COLS = QKV_W + CONV_W + GATE_W
COL_TILE = 512
N_CHIP = 4
ADAM_LR = 0.001
ADAM_B1 = 0.9
ADAM_B2 = 0.999
ADAM_EPS = 1e-08
ADAM_WD = 0.01
ADAM_STEP = 10
V7X_VMEM_BYTES = 64 * 2 ** 20
VMEM_LIMIT = V7X_VMEM_BYTES - 8 * 2 ** 20
MESH = pl.DeviceIdType.MESH
ANY = pl.BlockSpec(memory_space=pl.ANY)
SMALL_ROWS = 24

_SLOPES = [float(v) for v in np.power(np.float32(2.0), -8.0 * np.arange(1, N_Q_HEADS + 1, dtype=np.float32) / N_Q_HEADS)]


def _params(*sem):
    return pltpu.CompilerParams(dimension_semantics=sem, vmem_limit_bytes=VMEM_LIMIT)


class _Hosted:
    def __init__(self, inputs, out_shape, aliases, scratch, start, finish):
        self.inputs, self.out_shape, self.aliases, self.scratch = list(inputs), list(out_shape), dict(aliases), list(scratch)
        self.start, self.finish = start, finish


def _hosted_call(comm, kern, *, out_shape, grid, in_specs, out_specs, args, name, sem, scratch_shapes=()):
    single = not isinstance(out_shape, (tuple, list))
    outs = [out_shape] if single else list(out_shape)
    ospecs = [out_specs] if single else list(out_specs)
    if comm is None:
        res = pl.pallas_call(kern, out_shape=outs, grid=grid, in_specs=list(in_specs), out_specs=ospecs,
                             scratch_shapes=list(scratch_shapes), name=name, compiler_params=_params(*sem))(*args)
        return res[0] if single else res
    n_in, n_out, n_scr = len(args), len(outs), len(scratch_shapes)
    ci, co, cs = len(comm.inputs), len(comm.out_shape), len(comm.scratch)

    def body(*refs):
        cuts = np.cumsum([0, n_in, ci, n_out, co, n_scr, cs])
        a, b, c, d, e, f = [refs[lo:hi] for lo, hi in zip(cuts[:-1], cuts[1:])]
        ids = [pl.program_id(k) for k in range(len(grid))]
        first = functools.reduce(jnp.logical_and, [i == 0 for i in ids])
        last = functools.reduce(jnp.logical_and, [i == n - 1 for i, n in zip(ids, grid)])
        pl.when(first)(lambda: comm.start(b, d, f))
        kern(*a, *c, *e)
        pl.when(last)(lambda: comm.finish(b, d, f))

    res = pl.pallas_call(
        body, out_shape=outs + comm.out_shape, grid=grid, in_specs=list(in_specs) + [ANY] * ci, out_specs=ospecs + [ANY] * co,
        scratch_shapes=list(scratch_shapes) + comm.scratch,
        input_output_aliases={n_in + i: n_out + o for i, o in comm.aliases.items()},
        name=name + "_carrier", compiler_params=_params(*(["arbitrary"] * len(grid))))(*args, *comm.inputs)
    main = res[:n_out]
    return (main[0] if single else main), res[n_out:]


def _nt(a, b):
    return lax.dot_general(a, b, (((1,), (1,)), ((), ())), preferred_element_type=F32)


def _tn(a, b):
    return lax.dot_general(a, b, (((0,), (0,)), ((), ())), preferred_element_type=F32)


def _nn(a, b):
    return jnp.dot(a, b, preferred_element_type=F32)


def _rms_stats(xf):
    r = lax.rsqrt(jnp.mean(xf * xf, axis=-1, keepdims=True) + RMS_EPS)
    return r, xf * r


def _rms_bwd(dh, xh, r, g):
    dxh = dh * g
    dx = r * (dxh - xh * jnp.mean(dxh * xh, axis=-1, keepdims=True))
    dg = jnp.sum(dh * xh, axis=0, keepdims=True)
    return dx, dg


def _dw_chunk(T):
    return min(2048, T)


def _resident(shape):
    return pl.BlockSpec(shape, lambda *_: (0,) * len(shape), pipeline_mode=pl.Buffered(1))


def _norm_proj(x, g, w, comm=None):
    T, D = x.shape
    tm = min(512, T)
    tk = _dw_chunk(T)
    per = tk // tm
    widths = (QKV_W, CONV_W, GATE_W)

    def kern(x_ref, g_ref, w_ref, ht_ref, *o_refs):
        _, xh = _rms_stats(x_ref[...])
        h = (xh * g_ref[...]).astype(BF16)
        ht_ref[...] = h.T
        off = 0
        for o_ref, wd in zip(o_refs, widths):
            o_ref[...] = _nn(h, w_ref[:, off:off + wd]).astype(BF16)
            off += wd

    row = lambda wd: pl.BlockSpec((tm, wd), lambda i: (i, 0))
    return _hosted_call(
        comm, kern,
        out_shape=[jax.ShapeDtypeStruct((T // tk, D, tk), BF16)] + [jax.ShapeDtypeStruct((T, wd), BF16) for wd in widths],
        grid=(T // tm,), in_specs=[row(D), pl.BlockSpec((1, D), lambda i: (0, 0)), _resident((D, IN_COLS))],
        out_specs=[pl.BlockSpec((None, D, tm), lambda i: (i // per, 0, i % per))] + [row(wd) for wd in widths],
        name="norm_proj", sem=("parallel",), args=(x, g, w))


GW = GQA_GROUP * WINDOW
BAND = 2 * WINDOW
KV_W = N_KV_HEADS * HEAD_DIM


def _attn_bias_table():
    jj = np.arange(BAND)[:, None]
    col = np.arange(GW)[None, :]
    dist = WINDOW + (col % WINDOW) - jj
    valid = (dist >= 0) & (dist < WINDOW)
    slopes = np.asarray(_SLOPES, np.float32).reshape(N_KV_HEADS, GQA_GROUP)
    tab = np.empty((2, N_KV_HEADS, BAND, GW), np.float32)
    for hk in range(N_KV_HEADS):
        bias = -slopes[hk][col // WINDOW] * dist.astype(np.float32)
        tab[0, hk] = np.where(valid, bias, np.float32(NEG_INF))
        tab[1, hk] = np.where(valid & (jj >= WINDOW), bias, np.float32(NEG_INF))
    return jnp.asarray(tab)


def _stack_heads(ref, hk):
    return jnp.concatenate(
        [ref[:, HEAD_DIM * (GQA_GROUP * hk + g): HEAD_DIM * (GQA_GROUP * hk + g + 1)] for g in range(GQA_GROUP)], axis=0)


def _kv_band(cur_ref, prev_ref, hk):
    k0 = N_Q_HEADS * HEAD_DIM
    sl = slice(HEAD_DIM * hk, HEAD_DIM * (hk + 1))
    ksl, vsl = slice(k0 + sl.start, k0 + sl.stop), slice(k0 + KV_W + sl.start, k0 + KV_W + sl.stop)
    k_band = jnp.concatenate([prev_ref[:, sl], cur_ref[:, ksl]], axis=0)
    v_band = jnp.concatenate([prev_ref[:, KV_W + sl.start:KV_W + sl.stop], cur_ref[:, vsl]], axis=0)
    return k_band, v_band


def _lane_row(vals):
    return jnp.concatenate([jnp.full((1, WINDOW), v, F32) for v in vals], axis=1)


def _attn_fwd(pqkv, sinks, bias, seq, comm=None):
    T = pqkv.shape[0]
    nblk = seq // WINDOW

    def kern(sink_ref, cur_ref, prev_ref, bias_ref, o_ref, ot_ref, lse_ref):
        i = pl.program_id(0)
        first_i = ((i % nblk) == 0).astype(jnp.int32)
        bands = [_kv_band(cur_ref, prev_ref, hk) for hk in range(N_KV_HEADS)]
        sts = [_nt(bands[hk][0], _stack_heads(cur_ref, hk) * ATTN_SCALE) + bias_ref[first_i, hk] for hk in range(N_KV_HEADS)]
        ps, scales = [], []
        for hk in range(N_KV_HEADS):
            heads = [GQA_GROUP * hk + g for g in range(GQA_GROUP)]
            sink = _lane_row([sink_ref[h] for h in heads])
            m = jnp.maximum(jnp.max(sts[hk], axis=0, keepdims=True), sink)
            p = jnp.exp(sts[hk] - m)
            den = jnp.sum(p, axis=0, keepdims=True) + jnp.exp(sink - m)
            lse = m + jnp.log(den)
            for g, h in enumerate(heads):
                lse_ref[h:h + 1, :] = lse[:, WINDOW * g:WINDOW * (g + 1)]
            ps.append(p.astype(BF16))
            scales.append(1.0 / den)
        for hk in range(N_KV_HEADS):
            ot = _tn(bands[hk][1], ps[hk]) * scales[hk]
            for g in range(GQA_GROUP):
                h = GQA_GROUP * hk + g
                ot_ref[HEAD_DIM * h:HEAD_DIM * (h + 1), :] = ot[:, WINDOW * g:WINDOW * (g + 1)].astype(BF16)
        o_ref[...] = ot_ref[...].T

    return _hosted_call(
        comm, kern,
        out_shape=(jax.ShapeDtypeStruct((T, D_MODEL), BF16), jax.ShapeDtypeStruct((D_MODEL, T), BF16),
                   jax.ShapeDtypeStruct((N_Q_HEADS, T), F32)),
        grid=(T // WINDOW,),
        in_specs=[pl.BlockSpec(memory_space=pltpu.SMEM),
                  pl.BlockSpec((WINDOW, QKV_W), lambda i: (i, 0)),
                  pl.BlockSpec((WINDOW, 2 * KV_W), lambda i: (jnp.maximum(i - 1, 0), 2)),
                  _resident(bias.shape)],
        out_specs=(pl.BlockSpec((WINDOW, D_MODEL), lambda i: (i, 0)), pl.BlockSpec((D_MODEL, WINDOW), lambda i: (0, i)),
                   pl.BlockSpec((N_Q_HEADS, WINDOW), lambda i: (0, i))),
        name="attn_fwd", sem=("parallel",), args=(sinks, pqkv, pqkv, bias))


def _pick_row(a, row):
    rid = lax.broadcasted_iota(jnp.int32, a.shape, 0)
    return jnp.sum(jnp.where(rid == row, a, 0.0), axis=0, keepdims=True)


def _conv_taps(yc, halo_yc, first_i):
    keep = (1 - first_i).astype(F32)
    p1 = _pick_row(halo_yc, 15) * keep
    p2 = _pick_row(halo_yc, 14) * keep
    rowid = lax.broadcasted_iota(jnp.int32, yc.shape, 0)
    s1 = jnp.where(rowid == 0, p1, pltpu.roll(yc, 1, 0))
    s2 = jnp.where(rowid == 0, p2, jnp.where(rowid == 1, p1, pltpu.roll(yc, 2, 0)))
    return s1, s2


def _mix_fwd(x, att, pconv, pgate, conv_w, conv_b, b_gates, wao, wco, wo, seq):
    T, D = x.shape
    tm = min(512, seq)
    per_seq = seq // tm

    def kern(x_ref, att_ref, pc_ref, halo_ref, pg_ref, cw_ref, cb_ref, bg_ref, wao_ref, wco_ref, wo_ref,
             x1_ref, ya_ref, yc_ref, mgt_ref, cvt_ref):
        first_i = ((pl.program_id(0) % per_seq) == 0).astype(jnp.int32)
        ya = _nn(att_ref[...], wao_ref[...])
        u = pc_ref[:, D:2 * D].astype(F32) * pc_ref[:, 2 * D:3 * D].astype(F32)
        halo_u = halo_ref[:, D:2 * D].astype(F32) * halo_ref[:, 2 * D:3 * D].astype(F32)
        s1, s2 = _conv_taps(u, halo_u, first_i)
        z = cw_ref[0:1, :] * s2 + cw_ref[1:2, :] * s1 + cw_ref[2:3, :] * u
        cv = (pc_ref[:, 0:D].astype(F32) * (z + cb_ref[...])).astype(BF16)
        cvt_ref[...] = cv.T
        yc = _nn(cv, wco_ref[...])
        sa = jax.nn.sigmoid(pg_ref[:, 0:D].astype(F32) + bg_ref[:, 0:D])
        sc = jax.nn.sigmoid(pg_ref[:, D:2 * D].astype(F32) + bg_ref[:, D:2 * D])
        mg = (sa * ya + sc * yc).astype(BF16)
        ya_ref[...] = ya.astype(BF16)
        yc_ref[...] = yc.astype(BF16)
        mgt_ref[...] = mg.T
        x1_ref[...] = x_ref[...] + _nn(mg, wo_ref[...])

    row = lambda w: pl.BlockSpec((tm, w), lambda i: (i, 0))
    full = lambda a, b: pl.BlockSpec((a, b), lambda i: (0, 0))
    tsp = pl.BlockSpec((D, tm), lambda i: (0, i))
    bf, bft = jax.ShapeDtypeStruct((T, D), BF16), jax.ShapeDtypeStruct((D, T), BF16)
    return pl.pallas_call(
        kern, out_shape=(jax.ShapeDtypeStruct((T, D), F32), bf, bf, bft, bft), grid=(T // tm,),
        in_specs=[row(D), row(D), row(CONV_W), pl.BlockSpec((16, CONV_W), lambda i: (jnp.maximum(i * (tm // 16) - 1, 0), 0)),
                  row(GATE_W), full(3, D), full(1, D), full(1, GATE_W)] + [_resident((D, D))] * 3,
        out_specs=(row(D), row(D), row(D), tsp, tsp),
        name="mix_fwd", compiler_params=_params("parallel"))(x, att, pconv, pconv, pgate, conv_w, conv_b, b_gates, wao, wco, wo)


def _mlp_fwd(x1, g, wup, wdn, comm=None):
    T, D = x1.shape
    tm = min(1024, T)
    nj = D_FF // D

    def kern(x_ref, g_ref, wup_ref, wdn_ref, x2_ref, a_ref, h_scr, acc_scr):
        j = pl.program_id(1)

        @pl.when(j == 0)
        def _():
            xf = x_ref[...]
            _, xh = _rms_stats(xf)
            h_scr[...] = (xh * g_ref[...]).astype(BF16)
            acc_scr[...] = xf

        a = _nn(h_scr[...], wup_ref[j])
        a_ref[...] = a.astype(BF16)
        u = jnp.square(jnp.maximum(a, 0.0)).astype(BF16)
        acc_scr[...] += _nn(u, wdn_ref[j])

        @pl.when(j == nj - 1)
        def _():
            x2_ref[...] = acc_scr[...]

    return _hosted_call(
        comm, kern, out_shape=(jax.ShapeDtypeStruct((T, D), F32), jax.ShapeDtypeStruct((T, D_FF), BF16)), grid=(T // tm, nj),
        in_specs=[pl.BlockSpec((tm, D), lambda i, j: (i, 0)), pl.BlockSpec((1, D), lambda i, j: (0, 0)),
                  _resident((nj, D, D)), _resident((nj, D, D))],
        out_specs=(pl.BlockSpec((tm, D), lambda i, j: (i, 0)), pl.BlockSpec((tm, D), lambda i, j: (i, j))),
        scratch_shapes=[pltpu.VMEM((tm, D), BF16), pltpu.VMEM((tm, D), F32)],
        name="mlp_fwd", sem=("parallel", "arbitrary"), args=(x1, g, wup, wdn))


def _loss_bwd(x, g, tgt):
    T, D = x.shape
    tm = min(512, T)

    def kern(x_ref, g_ref, t_ref, st_ref, dx_ref):
        i = pl.program_id(0)

        @pl.when(i == 0)
        def _():
            st_ref[...] = jnp.zeros_like(st_ref)

        gg = g_ref[...]
        r, xh = _rms_stats(x_ref[...])
        e = xh * gg - t_ref[...]
        part = 0.5 * jnp.sum(jnp.mean(e * e, axis=-1, keepdims=True), axis=0, keepdims=True)
        dx, dg = _rms_bwd(e * (1.0 / D), xh, r, gg)
        dx_ref[...] = dx
        st_ref[0:1, :] += dg
        st_ref[1:2, 0:1] += part

    return pl.pallas_call(
        kern, out_shape=(jax.ShapeDtypeStruct((8, D), F32), jax.ShapeDtypeStruct((T, D), F32)), grid=(T // tm,),
        in_specs=[pl.BlockSpec((tm, D), lambda i: (i, 0)), pl.BlockSpec((1, D), lambda i: (0, 0)),
                  pl.BlockSpec((tm, D), lambda i: (i, 0))],
        out_specs=(pl.BlockSpec((8, D), lambda i: (0, 0)), pl.BlockSpec((tm, D), lambda i: (i, 0))),
        name="loss_bwd", compiler_params=_params("arbitrary"))(x, g, tgt)


def _mlp_bwd(dx2, x1, a, g, wup, wdn, comm=None):
    T, D = x1.shape
    tm = min(512, T)
    nj = D_FF // D
    tk = _dw_chunk(T)
    per = tk // tm

    def kern(dx2_ref, x1_ref, a_ref, g_ref, wup_ref, wdn_ref, dx1_ref, da_ref, ut_ref, h2t_ref, dyb_ref, dg_ref, acc_scr):
        i, j = pl.program_id(0), pl.program_id(1)

        @pl.when((i == 0) & (j == 0))
        def _():
            dg_ref[...] = jnp.zeros_like(dg_ref)

        @pl.when(j == 0)
        def _():
            dyb_ref[...] = dx2_ref[...].astype(BF16)
            acc_scr[...] = jnp.zeros_like(acc_scr)

        du = _nt(dyb_ref[...], wdn_ref[j])
        relu = jnp.maximum(a_ref[...].astype(F32), 0.0)
        da = (du * (2.0 * relu)).astype(BF16)
        da_ref[...] = da
        ut_ref[...] = jnp.square(relu).astype(BF16).T
        acc_scr[...] += _nt(da, wup_ref[j])

        @pl.when(j == nj - 1)
        def _():
            gg = g_ref[...]
            r, xh = _rms_stats(x1_ref[...])
            h2t_ref[...] = (xh * gg).astype(BF16).T
            dx, dg = _rms_bwd(acc_scr[...], xh, r, gg)
            dx1_ref[...] = dx2_ref[...] + dx
            dg_ref[...] += dg

    return _hosted_call(
        comm, kern,
        out_shape=(jax.ShapeDtypeStruct((T, D), F32), jax.ShapeDtypeStruct((T, D_FF), BF16),
                   jax.ShapeDtypeStruct((D_FF, T), BF16), jax.ShapeDtypeStruct((T // tk, D, tk), BF16),
                   jax.ShapeDtypeStruct((T, D), BF16), jax.ShapeDtypeStruct((1, D), F32)),
        grid=(T // tm, nj),
        in_specs=[pl.BlockSpec((tm, D), lambda i, j: (i, 0)), pl.BlockSpec((tm, D), lambda i, j: (i, 0)),
                  pl.BlockSpec((tm, D), lambda i, j: (i, j)), pl.BlockSpec((1, D), lambda i, j: (0, 0)),
                  _resident((nj, D, D)), _resident((nj, D, D))],
        out_specs=(pl.BlockSpec((tm, D), lambda i, j: (i, 0)), pl.BlockSpec((tm, D), lambda i, j: (i, j)),
                   pl.BlockSpec((D, tm), lambda i, j: (j, i)), pl.BlockSpec((None, D, tm), lambda i, j: (i // per, 0, i % per)),
                   pl.BlockSpec((tm, D), lambda i, j: (i, 0)), pl.BlockSpec((1, D), lambda i, j: (0, 0))),
        scratch_shapes=[pltpu.VMEM((tm, D), F32)],
        name="mlp_bwd", sem=("arbitrary", "arbitrary"), args=(dx2, x1, a, g, wup, wdn))


def _mix_bwd(dx1, ya, yc, pgate, pconv, conv_w, conv_b, b_gates, wao, wco, wo, seq, comm=None):
    T, D = dx1.shape
    tm = min(256, seq)
    per_seq = seq // tm
    n = T // tm

    def kern(dx_ref, ya_ref, yc_ref, pg_ref, pc_ref, halo_ref, cw_ref, cb_ref, bg_ref, wao_ref, wco_ref, wo_ref,
             datt_ref, dya_ref, dyc_ref, dgt_ref, dcn_ref, dbg_ref, dwb_ref, next_scr):
        i = pl.program_id(0)
        r = n - 1 - i

        @pl.when(i == 0)
        def _():
            dbg_ref[...] = jnp.zeros_like(dbg_ref)
            dwb_ref[...] = jnp.zeros_like(dwb_ref)
            next_scr[...] = jnp.zeros_like(next_scr)

        dm = _nt(dx_ref[...].astype(BF16), wo_ref[...])
        sa = jax.nn.sigmoid(pg_ref[:, 0:D].astype(F32) + bg_ref[:, 0:D])
        sc = jax.nn.sigmoid(pg_ref[:, D:2 * D].astype(F32) + bg_ref[:, D:2 * D])
        dya = (dm * sa).astype(BF16)
        dyc = (dm * sc).astype(BF16)
        dga = dm * ya_ref[...].astype(F32) * (sa * (1.0 - sa))
        dgc = dm * yc_ref[...].astype(F32) * (sc * (1.0 - sc))
        dya_ref[...] = dya
        dyc_ref[...] = dyc
        dgt_ref[:, 0:D] = dga.astype(BF16)
        dgt_ref[:, D:2 * D] = dgc.astype(BF16)
        dbg_ref[:, 0:D] += jnp.sum(dga, axis=0, keepdims=True)
        dbg_ref[:, D:2 * D] += jnp.sum(dgc, axis=0, keepdims=True)
        datt_ref[...] = _nt(dya, wao_ref[...]).astype(BF16)
        dcv = _nt(dyc, wco_ref[...])

        first_i = ((r % per_seq) == 0).astype(jnp.int32)
        keep_next = 1.0 - (((r + 1) % per_seq) == 0).astype(F32)
        cb = pc_ref[:, 0:D].astype(F32)
        cc = pc_ref[:, D:2 * D].astype(F32)
        cu = pc_ref[:, 2 * D:3 * D].astype(F32)
        u = cc * cu
        halo_u = halo_ref[:, D:2 * D].astype(F32) * halo_ref[:, 2 * D:3 * D].astype(F32)
        s1, s2 = _conv_taps(u, halo_u, first_i)
        w0, w1, w2 = cw_ref[0:1, :], cw_ref[1:2, :], cw_ref[2:3, :]
        z = w0 * s2 + w1 * s1 + w2 * u
        dz = dcv * cb
        n1 = next_scr[0:1, :] * keep_next
        n2 = next_scr[1:2, :] * keep_next
        next_scr[...] = dz[0:8, :]
        rowid = lax.broadcasted_iota(jnp.int32, dz.shape, 0)
        u1 = jnp.where(rowid == tm - 1, n1, pltpu.roll(dz, tm - 1, 0))
        u2 = jnp.where(rowid == tm - 1, n2, jnp.where(rowid == tm - 2, n1, pltpu.roll(dz, tm - 2, 0)))
        du = w2 * dz + w1 * u1 + w0 * u2
        dcn_ref[:, 0:D] = (dcv * (z + cb_ref[...])).astype(BF16)
        dcn_ref[:, D:2 * D] = (du * cu).astype(BF16)
        dcn_ref[:, 2 * D:3 * D] = (du * cc).astype(BF16)
        dwb_ref[0:1, :] += jnp.sum(dz * s2, axis=0, keepdims=True)
        dwb_ref[1:2, :] += jnp.sum(dz * s1, axis=0, keepdims=True)
        dwb_ref[2:3, :] += jnp.sum(dz * u, axis=0, keepdims=True)
        dwb_ref[3:4, :] += jnp.sum(dz, axis=0, keepdims=True)

    row = lambda w: pl.BlockSpec((tm, w), lambda i: (n - 1 - i, 0))
    full = lambda a, b: pl.BlockSpec((a, b), lambda i: (0, 0))
    halo = pl.BlockSpec((16, CONV_W), lambda i: (jnp.maximum((n - 1 - i) * (tm // 16) - 1, 0), 0))
    bf = jax.ShapeDtypeStruct((T, D), BF16)
    return _hosted_call(
        comm, kern,
        out_shape=(bf, bf, bf, jax.ShapeDtypeStruct((T, GATE_W), BF16), jax.ShapeDtypeStruct((T, CONV_W), BF16),
                   jax.ShapeDtypeStruct((1, GATE_W), F32), jax.ShapeDtypeStruct((8, D), F32)),
        grid=(n,),
        in_specs=[row(D), row(D), row(D), row(GATE_W), row(CONV_W), halo, full(3, D), full(1, D), full(1, GATE_W)]
        + [_resident((D, D))] * 3,
        out_specs=(row(D), row(D), row(D), row(GATE_W), row(CONV_W), full(1, GATE_W), full(8, D)),
        scratch_shapes=[pltpu.VMEM((8, D), F32)],
        name="mix_bwd", sem=("arbitrary",), args=(dx1, ya, yc, pgate, pconv, pconv, conv_w, conv_b, b_gates, wao, wco, wo))


def _attn_bwd(pqkv, datt, lse, sinks, bias, seq, comm=None):
    T = pqkv.shape[0]
    nblk = seq // WINDOW
    nseq = T // seq
    KVW = KV_W

    def kern(sink_ref, cur_ref, prev_ref, do_ref, lse_ref, bias_ref, dq_ref, dkv_ref, ds_ref, kc_scr, vc_scr, dqt_scr):
        b, st = pl.program_id(0), pl.program_id(1)

        @pl.when((b == 0) & (st == 0))
        def _():
            ds_ref[...] = jnp.zeros_like(ds_ref)

        @pl.when(st == 0)
        def _():
            kc_scr[...] = jnp.zeros_like(kc_scr)
            vc_scr[...] = jnp.zeros_like(vc_scr)

        @pl.when(st < nblk)
        def _():
            first_i = (st == 0).astype(jnp.int32)
            groups = range(N_KV_HEADS)
            bands = [_kv_band(cur_ref, prev_ref, hk) for hk in groups]
            qs = [_stack_heads(cur_ref, hk) for hk in groups]
            dos = [_stack_heads(do_ref, hk) for hk in groups]
            sts = [_nt(bands[hk][0], qs[hk] * ATTN_SCALE) + bias_ref[first_i, hk] for hk in groups]
            dps = [_nt(bands[hk][1], dos[hk]) for hk in groups]
            pbs, dsss = [], []
            for hk in groups:
                heads = [GQA_GROUP * hk + g for g in range(GQA_GROUP)]
                sink = _lane_row([sink_ref[h] for h in heads])
                lse_g = jnp.concatenate([lse_ref[h:h + 1, :] for h in heads], axis=1)
                p = jnp.exp(sts[hk] - lse_g)
                d_row = jnp.sum(p * dps[hk], axis=0, keepdims=True)
                dsss.append((p * (dps[hk] - d_row) * ATTN_SCALE).astype(BF16))
                pbs.append(p.astype(BF16))
                psd = jnp.exp(sink - lse_g) * d_row
                for g, h in enumerate(heads):
                    ds_ref[0:1, h:h + 1] -= jnp.sum(psd[:, WINDOW * g:WINDOW * (g + 1)], axis=1, keepdims=True)
            for hk in groups:
                dqt = _tn(bands[hk][0], dsss[hk])
                dk_b = _nn(dsss[hk], qs[hk])
                dv_b = _nn(pbs[hk], dos[hk])
                for g in range(GQA_GROUP):
                    h = GQA_GROUP * hk + g
                    dqt_scr[HEAD_DIM * h:HEAD_DIM * (h + 1), :] = dqt[:, WINDOW * g:WINDOW * (g + 1)].astype(BF16)
                ksl = slice(HEAD_DIM * hk, HEAD_DIM * (hk + 1))
                vsl = slice(KVW + HEAD_DIM * hk, KVW + HEAD_DIM * (hk + 1))
                dkv_ref[:, ksl] = (kc_scr[:, ksl] + dk_b[0:WINDOW]).astype(BF16)
                dkv_ref[:, vsl] = (vc_scr[:, ksl] + dv_b[0:WINDOW]).astype(BF16)
                kc_scr[:, ksl] = dk_b[WINDOW:2 * WINDOW]
                vc_scr[:, ksl] = dv_b[WINDOW:2 * WINDOW]
            dq_ref[...] = dqt_scr[...].T

        @pl.when(st == nblk)
        def _():
            dkv_ref[:, 0:KVW] = kc_scr[...].astype(BF16)
            dkv_ref[:, KVW:2 * KVW] = vc_scr[...].astype(BF16)

    cur_map = lambda b, s: (b * nblk + jnp.minimum(s, nblk - 1), 0)
    prev_row = lambda b, s: b * nblk + jnp.clip(s - 1, 0, nblk - 1)
    return _hosted_call(
        comm, kern,
        out_shape=(jax.ShapeDtypeStruct((T, D_MODEL), BF16), jax.ShapeDtypeStruct((T, 2 * KVW), BF16),
                   jax.ShapeDtypeStruct((8, 128), F32)),
        grid=(nseq, nblk + 1),
        in_specs=[pl.BlockSpec(memory_space=pltpu.SMEM),
                  pl.BlockSpec((WINDOW, QKV_W), cur_map),
                  pl.BlockSpec((WINDOW, 2 * KVW), lambda b, s: (prev_row(b, s), 2)),
                  pl.BlockSpec((WINDOW, D_MODEL), cur_map),
                  pl.BlockSpec((N_Q_HEADS, WINDOW), lambda b, s: (0, b * nblk + jnp.minimum(s, nblk - 1))),
                  _resident(bias.shape)],
        out_specs=(pl.BlockSpec((WINDOW, D_MODEL), cur_map),
                   pl.BlockSpec((WINDOW, 2 * KVW), lambda b, s: (prev_row(b, s), 0)),
                   pl.BlockSpec((8, 128), lambda b, s: (0, 0))),
        scratch_shapes=[pltpu.VMEM((WINDOW, KVW), F32), pltpu.VMEM((WINDOW, KVW), F32), pltpu.VMEM((D_MODEL, WINDOW), BF16)],
        name="attn_bwd", sem=("arbitrary", "arbitrary"), args=(sinks, pqkv, pqkv, datt, lse, bias))


def _piece_tiles(pieces):
    out, start = [], 0
    for arr, width in pieces:
        out.append((arr, start, width // COL_TILE))
        start += width // COL_TILE
    return out, start


def _inproj_bwd(pieces, w_in, x, dx_in, g, comm=None):
    T, D = x.shape
    tm = min(512, T)

    def kern(*refs):
        p_refs = refs[:len(pieces)]
        w_ref, x_ref, dxin_ref, g_ref, dx_ref, dg_ref = refs[len(pieces):]

        @pl.when(pl.program_id(0) == 0)
        def _():
            dg_ref[...] = jnp.zeros_like(dg_ref)

        dh, off = None, 0
        for p_ref, (_, width) in zip(p_refs, pieces):
            part = _nt(p_ref[...], w_ref[:, off:off + width])
            dh = part if dh is None else dh + part
            off += width
        gg = g_ref[...]
        r, xh = _rms_stats(x_ref[...])
        dx, dg = _rms_bwd(dh, xh, r, gg)
        dx_ref[...] = dxin_ref[...] + dx
        dg_ref[...] += dg

    row = lambda wd: pl.BlockSpec((tm, wd), lambda i: (i, 0))
    return _hosted_call(
        comm, kern, out_shape=(jax.ShapeDtypeStruct((T, D), F32), jax.ShapeDtypeStruct((1, D), F32)), grid=(T // tm,),
        in_specs=[row(wd) for _, wd in pieces] + [_resident((D, IN_COLS)), row(D), row(D), pl.BlockSpec((1, D), lambda i: (0, 0))],
        out_specs=(row(D), pl.BlockSpec((1, D), lambda i: (0, 0))),
        name="inproj_bwd", sem=("arbitrary",), args=(*[a for a, _ in pieces], w_in, x, dx_in, g))


def _dw_pieces(lhs_t, pieces):
    nt, K, tk = lhs_t.shape
    tiles, nj = _piece_tiles(pieces)

    def kern(*refs):
        lhs_ref = refs[0]
        p_refs = refs[1:1 + len(tiles)]
        o_ref, ob_ref = refs[1 + len(tiles):]
        j, t = pl.program_id(0), pl.program_id(1)

        @pl.when(t == 0)
        def _():
            o_ref[...] = jnp.zeros_like(o_ref)

        for p_ref, (_, start, n) in zip(p_refs, tiles):
            @pl.when((j >= start) & (j < start + n))
            def _(p_ref=p_ref):
                o_ref[...] += _nn(lhs_ref[t], p_ref[...])

        @pl.when(t == nt - 1)
        def _():
            ob_ref[...] = o_ref[...].astype(BF16)

    def p_map(start, n):
        return lambda j, t: (jnp.where((j >= start) & (j < start + n), t, 0), jnp.clip(j - start, 0, n - 1))

    N = nj * COL_TILE
    return pl.pallas_call(
        kern, out_shape=(jax.ShapeDtypeStruct((K, N), F32), jax.ShapeDtypeStruct((K, N), BF16)), grid=(nj, nt),
        in_specs=[_resident((nt, K, tk))] + [pl.BlockSpec((tk, COL_TILE), p_map(s, n)) for _, s, n in tiles],
        out_specs=(pl.BlockSpec((K, COL_TILE), lambda j, t: (0, j)), pl.BlockSpec((K, COL_TILE), lambda j, t: (0, j))),
        name="dw_pieces", compiler_params=_params("arbitrary", "arbitrary"))(lhs_t, *[a for a, _, _ in tiles])


def _dw(lhs_t, rhs):
    lhs_res, rhs_res = lhs_t.ndim == 3, rhs.ndim == 3
    W = D_MODEL
    if lhs_res:
        nt, K, tk = lhs_t.shape
    else:
        K, tk = lhs_t.shape[0], _dw_chunk(lhs_t.shape[1])
        nt = lhs_t.shape[1] // tk
    N = rhs.shape[-1]

    def kern(lhs_ref, rhs_ref, o_ref, ob_ref):
        t = pl.program_id(2)

        @pl.when(t == 0)
        def _():
            o_ref[...] = jnp.zeros_like(o_ref)

        a = lhs_ref[t] if lhs_res else lhs_ref[...]
        b = rhs_ref[t] if rhs_res else rhs_ref[...]
        o_ref[...] += _nn(a, b.astype(BF16))

        @pl.when(t == nt - 1)
        def _():
            ob_ref[...] = o_ref[...].astype(BF16)

    omap = lambda i, j, t: (i, j)
    lspec = _resident((nt, W, tk)) if lhs_res else pl.BlockSpec((W, tk), lambda i, j, t: (i, t))
    rspec = _resident((nt, tk, W)) if rhs_res else pl.BlockSpec((tk, W), lambda i, j, t: (t, j))
    return pl.pallas_call(
        kern, out_shape=(jax.ShapeDtypeStruct((K, N), F32), jax.ShapeDtypeStruct((K, N), BF16)), grid=(K // W, N // W, nt),
        in_specs=[lspec, rspec], out_specs=(pl.BlockSpec((W, W), omap), pl.BlockSpec((W, W), omap)),
        name="dw", compiler_params=_params("arbitrary", "arbitrary", "arbitrary"))(lhs_t, rhs)


BIG = (("w_in", D_MODEL, IN_COLS, "col"), ("w_attn_out", D_MODEL, D_MODEL, "row"), ("w_conv_out", D_MODEL, D_MODEL, "row"),
       ("w_o", D_MODEL, D_MODEL, "row"), ("w_up", D_MODEL, D_FF, "col"), ("w_down", D_FF, D_MODEL, "row"))


def _shard_dims(rows, cols, kind):
    return (rows, cols // N_CHIP) if kind in ("col", "chip") else (rows // N_CHIP, cols)


def _window(ref, rows, cols, kind, chip, half):
    sr, sc = _shard_dims(rows, cols, kind)
    hr = sr // 2
    if kind == "col":
        return ref.at[pl.ds(half * hr, hr), pl.ds(chip * sc, sc)]
    if kind == "chip":
        return ref.at[chip, pl.ds(half * hr, hr), :]
    return ref.at[pl.ds(chip * sr + half * hr, hr), :]


def _mesh_pos():
    x, y, c = lax.axis_index("x"), lax.axis_index("y"), lax.axis_index("c")
    return x, y, c, 2 * x + y


_REL_BITS = (2, 1, 3)


def _rel_dev(x, y, c, r):
    return ((1 - x, y, c), (x, 1 - y, c), (1 - x, 1 - y, c))[r]


def _for_my_chip(j, fn):
    for js in range(N_CHIP):
        pl.when(j == js)(functools.partial(fn, js))


def _cast_into_full(j_arr, shard, l, rows, cols, kind):
    sr, sc = _shard_dims(rows, cols, kind)
    tr = min(256, sr)

    def kern(j_ref, s_ref, o_ref):
        o_ref[...] = s_ref[...].astype(BF16)

    shape, block = (rows, cols), (tr, sc)
    if kind == "col":
        omap = lambda i, j_ref: (i, j_ref[0])
    elif kind == "chip":
        shape, block = (N_CHIP, rows, sc), (None, tr, sc)
        omap = lambda i, j_ref: (j_ref[0], i, 0)
    else:
        omap = lambda i, j_ref: (j_ref[0] * (sr // tr) + i, 0)
    gs = pltpu.PrefetchScalarGridSpec(
        num_scalar_prefetch=1, grid=(sr // tr,),
        in_specs=[pl.BlockSpec((None, tr, sc), lambda i, j_ref: (l, i, 0))], out_specs=pl.BlockSpec(block, omap))
    return pl.pallas_call(kern, out_shape=jax.ShapeDtypeStruct(shape, BF16), grid_spec=gs, name="cast_into_full",
                          compiler_params=_params("arbitrary"))(j_arr, shard)


def _gather_comm(fulls, cw=None):
    n_big = len(fulls)
    n_piece = n_big + (0 if cw is None else 1)

    def pieces(in_refs, o_refs, js, c):
        def piece(p, chip, half):
            if p == n_big:
                return o_refs[p].at[half, chip]
            _, rows, cols, kind = fulls[p]
            return _window(o_refs[p], rows, cols, kind, chip, half)

        def mine(p):
            return in_refs[p].at[c] if p == n_big else piece(p, js, c)

        return piece, mine

    def local_copies(in_refs, piece, js, loc_sem):
        if cw is None:
            return []
        return [pltpu.make_async_copy(in_refs[n_big].at[half], piece(n_big, js, half), loc_sem.at[half]) for half in range(2)]

    def ici_copy(piece, mine, js, x, y, c, r, p, send_sem, recv_sem):
        return pltpu.make_async_remote_copy(mine(p), piece(p, js, c), send_sem.at[r * n_piece + p], recv_sem.at[r * n_piece + p],
                                            _rel_dev(x, y, c, r), MESH)

    def start(in_refs, o_refs, sems):
        send_sem, recv_sem, _, _, loc_sem = sems
        x, y, c, j = _mesh_pos()

        def run(js):
            piece, mine = pieces(in_refs, o_refs, js, c)
            for cp in local_copies(in_refs, piece, js, loc_sem):
                cp.start()
            for r in range(3):
                for p in range(n_piece):
                    ici_copy(piece, mine, js, x, y, c, r, p, send_sem, recv_sem).start()

        _for_my_chip(j, run)

    def finish(in_refs, o_refs, sems):
        send_sem, recv_sem, fsend_sem, frecv_sem, loc_sem = sems
        x, y, c, j = _mesh_pos()

        def run(js):
            piece, mine = pieces(in_refs, o_refs, js, c)
            fwds = []
            for r in range(3):
                ks = js ^ _REL_BITS[r]
                for p in range(n_piece):
                    got = piece(p, ks, c)
                    pltpu.make_async_remote_copy(got, got, send_sem.at[r * n_piece + p], recv_sem.at[r * n_piece + p],
                                                 _rel_dev(x, y, c, r), MESH).wait_recv()
                    cp = pltpu.make_async_remote_copy(got, got, fsend_sem.at[r * n_piece + p], frecv_sem.at[r * n_piece + p],
                                                      (x, y, 1 - c), MESH)
                    cp.start()
                    fwds.append(cp)
            for r in range(3):
                ks = js ^ _REL_BITS[r]
                for p in range(n_piece):
                    got = piece(p, ks, 1 - c)
                    pltpu.make_async_remote_copy(got, got, fsend_sem.at[r * n_piece + p], frecv_sem.at[r * n_piece + p],
                                                 (x, y, 1 - c), MESH).wait_recv()
            for r in range(3):
                for p in range(n_piece):
                    ici_copy(piece, mine, js, x, y, c, r, p, send_sem, recv_sem).wait_send()
            for cp in fwds:
                cp.wait_send()
            for cp in local_copies(in_refs, piece, js, loc_sem):
                cp.wait()

        _for_my_chip(j, run)

    out_shape = [jax.ShapeDtypeStruct(a.shape, BF16) for a, _, _, _ in fulls]
    ins = [a for a, _, _, _ in fulls]
    if cw is not None:
        out_shape.append(jax.ShapeDtypeStruct((DEPTH, N_CHIP, 3, D_MODEL // N_CHIP), F32))
        ins.append(cw)
    scratch = [pltpu.SemaphoreType.DMA((3 * n_piece,))] * 4 + [pltpu.SemaphoreType.DMA((2,))]
    return _Hosted(ins, out_shape, {p: p for p in range(n_big)}, scratch, start, finish)


def _compose(*comms):
    comms = [cm for cm in comms if cm is not None]
    if len(comms) <= 1:
        return comms[0] if comms else None
    ins, outs, aliases, scratch, cuts = [], [], {}, [], []
    for cm in comms:
        cuts.append((len(ins), len(outs), len(scratch)))
        aliases.update({len(ins) + i: len(outs) + o for i, o in cm.aliases.items()})
        ins, outs, scratch = ins + cm.inputs, outs + cm.out_shape, scratch + cm.scratch

    def parts(a, b, s):
        for cm, (i0, o0, s0) in zip(comms, cuts):
            yield cm, a[i0:i0 + len(cm.inputs)], b[o0:o0 + len(cm.out_shape)], s[s0:s0 + len(cm.scratch)]

    def start(a, b, s):
        for cm, pa, pb, ps in parts(a, b, s):
            cm.start(pa, pb, ps)

    def finish(a, b, s):
        for cm, pa, pb, ps in parts(a, b, s):
            cm.finish(pa, pb, ps)

    return _Hosted(ins, outs, aliases, scratch, start, finish)


def _run_comm(comm, name):
    n_in, n_out = len(comm.inputs), len(comm.out_shape)

    def body(*refs):
        comm.start(refs[:n_in], refs[n_in:n_in + n_out], refs[n_in + n_out:])
        comm.finish(refs[:n_in], refs[n_in:n_in + n_out], refs[n_in + n_out:])

    return pl.pallas_call(body, out_shape=comm.out_shape, in_specs=[ANY] * n_in, out_specs=[ANY] * n_out,
                          input_output_aliases=comm.aliases, scratch_shapes=comm.scratch, name=name)(*comm.inputs)


def _sibling_exchange_comm(gb):
    n = len(gb)

    def copies(g_refs, o_refs, sems, c):
        send_sem, recv_sem = sems
        x, y, _, _ = _mesh_pos()
        return [pltpu.make_async_remote_copy(_window(g_refs[t], rows, cols, kind, chip, 1 - c),
                                             _window(o_refs[t], rows, cols, kind, chip, 1 - c),
                                             send_sem.at[N_CHIP * t + chip], recv_sem.at[N_CHIP * t + chip], (x, y, 1 - c), MESH)
                for t, (_, rows, cols, kind) in enumerate(gb) for chip in range(N_CHIP)]

    def start(g_refs, o_refs, sems):
        for cp in copies(g_refs, o_refs, sems, lax.axis_index("c")):
            cp.start()

    def finish(g_refs, o_refs, sems):
        c = lax.axis_index("c")
        for cp in copies(g_refs, o_refs, sems, 1 - c):
            cp.wait_recv()
        for cp in copies(g_refs, o_refs, sems, c):
            cp.wait_send()

    return _Hosted([a for a, _, _, _ in gb], [jax.ShapeDtypeStruct(a.shape, BF16) for a, _, _, _ in gb], {},
                   [pltpu.SemaphoreType.DMA((N_CHIP * n,))] * 2, start, finish)


def _half_add(jc_arr, g, sib, rows, cols, kind):
    sr, sc = _shard_dims(rows, cols, kind)
    hr = sr // 2

    def kern(jc_ref, g_ref, s_ref, ob_ref, of_ref):
        v = g_ref[...] + s_ref[...].astype(F32)
        ob_ref[...] = v.astype(BF16)

        @pl.when(pl.program_id(0) == jc_ref[0])
        def _():
            of_ref[...] = v

    if kind == "col":
        imap = lambda j, jc_ref: (jc_ref[1], j)
    else:
        imap = lambda j, jc_ref: (2 * j + jc_ref[1], 0)
    gs = pltpu.PrefetchScalarGridSpec(
        num_scalar_prefetch=1, grid=(N_CHIP,),
        in_specs=[pl.BlockSpec((hr, sc), imap), pl.BlockSpec((hr, sc), imap)],
        out_specs=[pl.BlockSpec((None, hr, sc), lambda j, jc_ref: (j, 0, 0)), pl.BlockSpec((hr, sc), lambda j, jc_ref: (0, 0))])
    return pl.pallas_call(
        kern, out_shape=(jax.ShapeDtypeStruct((N_CHIP, hr, sc), BF16), jax.ShapeDtypeStruct((hr, sc), F32)),
        grid_spec=gs, name="grad_half_add", compiler_params=_params("arbitrary"))(jc_arr, g, sib)


def _chip_exchange_comm(sbs):
    n = len(sbs)

    def copies(s_refs, o_refs, sems):
        send_sem, recv_sem = sems
        x, y, c, j = _mesh_pos()
        return [pltpu.make_async_remote_copy(s_refs[t].at[j ^ _REL_BITS[r]], o_refs[t].at[r], send_sem.at[n * r + t],
                                             recv_sem.at[n * r + t], _rel_dev(x, y, c, r), MESH)
                for r in range(3) for t in range(n)]

    def start(s_refs, o_refs, sems):
        for cp in copies(s_refs, o_refs, sems):
            cp.start()

    def finish(s_refs, o_refs, sems):
        for cp in copies(s_refs, o_refs, sems):
            cp.wait()

    return _Hosted(sbs, [jax.ShapeDtypeStruct((3,) + a.shape[1:], BF16) for a in sbs], {},
                   [pltpu.SemaphoreType.DMA((3 * n,))] * 2, start, finish)


def _owner_sum(jc_arr, sf, rb, l, into=None):
    hr, sc = sf.shape

    def kern(jc_ref, s_ref, r0_ref, r1_ref, r2_ref, *rest):
        o_ref = rest[-1]
        o_ref[...] = ((s_ref[...] + r0_ref[...].astype(F32)) + r1_ref[...].astype(F32)) + r2_ref[...].astype(F32)

    in_specs = [pl.BlockSpec((hr, sc), lambda i, jc_ref: (0, 0))]
    in_specs += [pl.BlockSpec((None, hr, sc), lambda i, jc_ref, r=r: (r, 0, 0)) for r in range(3)]
    args = [jc_arr, sf, rb, rb, rb]
    aliases = {}
    if into is not None:
        in_specs.append(ANY)
        args.append(into)
        aliases = {len(args) - 1: 0}
    gs = pltpu.PrefetchScalarGridSpec(
        num_scalar_prefetch=1, grid=(1,), in_specs=in_specs,
        out_specs=pl.BlockSpec((None, hr, sc), lambda i, jc_ref: (l, jc_ref[1], 0)))
    return pl.pallas_call(kern, out_shape=jax.ShapeDtypeStruct((DEPTH, 2 * hr, sc), F32), grid_spec=gs,
                          input_output_aliases=aliases, name="grad_owner_sum", compiler_params=_params("arbitrary"))(*args)


def _sibling_assemble_comm(grads, layers):
    n = len(grads)
    todo = [(q, l) for q in range(n) for l in layers[q]]

    def copies(o_refs, sems, half):
        send_sem, recv_sem = sems
        x, y, c, _ = _mesh_pos()
        out = []
        for k, (q, l) in enumerate(todo):
            hr = grads[q].shape[1] // 2
            w = o_refs[q].at[l, pl.ds(half * hr, hr), :]
            out.append(pltpu.make_async_remote_copy(w, w, send_sem.at[k], recv_sem.at[k], (x, y, 1 - c), MESH))
        return out

    def start(_, o_refs, sems):
        for cp in copies(o_refs, sems, lax.axis_index("c")):
            cp.start()

    def finish(_, o_refs, sems):
        c = lax.axis_index("c")
        for cp in copies(o_refs, sems, 1 - c):
            cp.wait_recv()
        for cp in copies(o_refs, sems, c):
            cp.wait_send()

    return _Hosted(grads, [jax.ShapeDtypeStruct(g.shape, F32) for g in grads], {q: q for q in range(n)},
                   [pltpu.SemaphoreType.DMA((len(todo),))] * 2, start, finish)


def _adamw_math(w, g, m, v):
    m = ADAM_B1 * m + (1.0 - ADAM_B1) * g
    v = ADAM_B2 * v + (1.0 - ADAM_B2) * jnp.square(g)
    m_hat = m / (1.0 - ADAM_B1 ** ADAM_STEP)
    v_hat = v / (1.0 - ADAM_B2 ** ADAM_STEP)
    delta = -ADAM_LR * (m_hat / (jnp.sqrt(v_hat) + ADAM_EPS) + ADAM_WD * w)
    return delta, m, v


def _adamw(w, g, m, v):
    shape = w.shape
    C = shape[-1]
    R = int(np.prod(shape[:-1]))
    tr = min(256, R)
    args = [a.reshape(R, C) for a in (w, g, m, v)]

    def kern(w_ref, g_ref, m_ref, v_ref, go_ref, d_ref, nm_ref, nv_ref):
        g_val = g_ref[...]
        d, nm, nv = _adamw_math(w_ref[...], g_val, m_ref[...], v_ref[...])
        go_ref[...] = g_val
        d_ref[...] = d
        nm_ref[...] = nm
        nv_ref[...] = nv

    spec = pl.BlockSpec((tr, C), lambda i: (i, 0))
    outs = pl.pallas_call(
        kern, out_shape=[jax.ShapeDtypeStruct((R, C), F32)] * 4, grid=(R // tr,), in_specs=[spec] * 4, out_specs=[spec] * 4,
        name="adamw", compiler_params=_params("parallel"))(*args)
    return [o.reshape(shape) for o in outs]


_ROW_G_MIX, _ROW_B_GATES, _ROW_SINKS, _ROW_CONV, _ROW_G_MLP, _ROW_G_FINAL, _ROW_LOSS = 0, 2, 6, 8, 16, 18, 19


def _small_step(parts, params, moms, vels):
    names = ["g_mix", "b_gates", "sinks", "conv_w", "conv_b", "g_mlp", "g_final"]
    D = D_MODEL
    QW = D // N_CHIP
    n_dev = 8

    def body(*refs):
        it = iter(refs)
        dgmix = [next(it) for _ in range(DEPTH)]
        dbg = [next(it) for _ in range(DEPTH)]
        dsk = [next(it) for _ in range(DEPTH)]
        dwb = [next(it) for _ in range(DEPTH)]
        dgmlp = [next(it) for _ in range(DEPTH)]
        lst = next(it)
        p_refs = {n: next(it) for n in names}
        m_refs = {n: next(it) for n in names}
        v_refs = {n: next(it) for n in names}
        loss_ref = next(it)
        outs = {n: [next(it) for _ in range(4)] for n in names}
        pack_ref, all_ref, send_sem, recv_sem = next(it), next(it), next(it), next(it)

        x, y, c, j = _mesh_pos()
        me = 4 * x + 2 * y + c
        pack_ref[...] = jnp.zeros_like(pack_ref)
        for l in range(DEPTH):
            pack_ref[_ROW_G_MIX + l:_ROW_G_MIX + l + 1, :] = dgmix[l][...]
            pack_ref[_ROW_B_GATES + 2 * l:_ROW_B_GATES + 2 * l + 1, :] = dbg[l][:, 0:D]
            pack_ref[_ROW_B_GATES + 2 * l + 1:_ROW_B_GATES + 2 * l + 2, :] = dbg[l][:, D:2 * D]
            pack_ref[_ROW_SINKS + l:_ROW_SINKS + l + 1, 0:128] = dsk[l][0:1, :]
            pack_ref[_ROW_CONV + 4 * l:_ROW_CONV + 4 * l + 4, :] = dwb[l][0:4, :]
            pack_ref[_ROW_G_MLP + l:_ROW_G_MLP + l + 1, :] = dgmlp[l][...]
        pack_ref[_ROW_G_FINAL:_ROW_G_FINAL + 1, :] = lst[0:1, :]
        pack_ref[_ROW_LOSS:_ROW_LOSS + 1, :] = lst[1:2, :]

        all_ref[me] = pack_ref[...]
        cps = []
        for k in range(1, n_dev):
            dx_, dy_, dc_ = (k >> 2) & 1, (k >> 1) & 1, k & 1
            peer = (x ^ dx_, y ^ dy_, c ^ dc_)
            cp = pltpu.make_async_remote_copy(pack_ref, all_ref.at[me], send_sem.at[k - 1], recv_sem.at[k - 1], peer, MESH)
            cp.start()
            cps.append(cp)
        for cp in cps:
            cp.wait()

        tot = all_ref[0]
        for d in range(1, n_dev):
            tot = tot + all_ref[d]
        pack_ref[...] = tot

        loss_ref[...] = pack_ref[_ROW_LOSS:_ROW_LOSS + 1, 0:1]

        def finish(name, idx, g):
            w, m, v = p_refs[name][idx], m_refs[name][idx], v_refs[name][idx]
            d, nm, nv = _adamw_math(w, g, m, v)
            for ref, val in zip(outs[name], (g, d, nm, nv)):
                ref[idx] = val

        for l in range(DEPTH):
            finish("g_mix", (slice(l, l + 1), slice(None)), pack_ref[_ROW_G_MIX + l:_ROW_G_MIX + l + 1, :])
            finish("g_mlp", (slice(l, l + 1), slice(None)), pack_ref[_ROW_G_MLP + l:_ROW_G_MLP + l + 1, :])
            finish("conv_b", (slice(l, l + 1), slice(None)), pack_ref[_ROW_CONV + 4 * l + 3:_ROW_CONV + 4 * l + 4, :])
            finish("sinks", (slice(l, l + 1), slice(None)), pack_ref[_ROW_SINKS + l:_ROW_SINKS + l + 1, 0:N_Q_HEADS])
            for hf in range(2):
                finish("b_gates", (slice(l, l + 1), slice(hf * D, (hf + 1) * D)),
                       pack_ref[_ROW_B_GATES + 2 * l + hf:_ROW_B_GATES + 2 * l + hf + 1, :])
        finish("g_final", (slice(0, 1), slice(None)), pack_ref[_ROW_G_FINAL:_ROW_G_FINAL + 1, :])

        def conv_w_chip(js):
            for l in range(DEPTH):
                for k in range(3):
                    row = _ROW_CONV + 4 * l + k
                    finish("conv_w", (l, slice(k, k + 1), slice(None)), pack_ref[row:row + 1, js * QW:(js + 1) * QW])

        _for_my_chip(j, conv_w_chip)

    vm = pl.BlockSpec(memory_space=pltpu.VMEM)
    ins = (parts["g_mix"] + parts["b_gates"] + parts["sinks"] + parts["conv"] + parts["g_mlp"] + [parts["loss"]]
           + [params[n] for n in names] + [moms[n] for n in names] + [vels[n] for n in names])
    out_shape = [jax.ShapeDtypeStruct((1, 1), F32)]
    for n in names:
        out_shape += [jax.ShapeDtypeStruct(params[n].shape, F32)] * 4
    res = pl.pallas_call(
        body, out_shape=out_shape, in_specs=[vm] * len(ins), out_specs=[vm] * len(out_shape),
        scratch_shapes=[pltpu.VMEM((SMALL_ROWS, D), F32), pltpu.VMEM((n_dev, SMALL_ROWS, D), F32),
                        pltpu.SemaphoreType.DMA((n_dev - 1,)), pltpu.SemaphoreType.DMA((n_dev - 1,))],
        name="small_allreduce_adamw")(*ins)
    loss = res[0]
    out = {n: res[1 + 4 * i:5 + 4 * i] for i, n in enumerate(names)}
    return loss, out


def kernel(x, g_mix, w_in, b_gates, sinks, w_attn_out, conv_w, conv_b, w_conv_out, w_o, g_mlp, w_up, w_down, g_final, loss_target, m_g_mix, m_w_in, m_b_gates, m_sinks, m_w_attn_out, m_conv_w, m_conv_b, m_w_conv_out, m_w_o, m_g_mlp, m_w_up, m_w_down, m_g_final, v_g_mix, v_w_in, v_b_gates, v_sinks, v_w_attn_out, v_conv_w, v_conv_b, v_w_conv_out, v_w_o, v_g_mlp, v_w_up, v_w_down, v_g_final):
    B, S, D = x.shape
    T = B * S
    big_w = dict(w_in=w_in, w_attn_out=w_attn_out, w_conv_out=w_conv_out, w_o=w_o, w_up=w_up, w_down=w_down)
    big_m = dict(w_in=m_w_in, w_attn_out=m_w_attn_out, w_conv_out=m_w_conv_out, w_o=m_w_o, w_up=m_w_up, w_down=m_w_down)
    big_v = dict(w_in=v_w_in, w_attn_out=v_w_attn_out, w_conv_out=v_w_conv_out, w_o=v_w_o, w_up=v_w_up, w_down=v_w_down)

    c_arr = lax.axis_index("c").astype(jnp.int32).reshape(1)
    j_arr = (2 * lax.axis_index("x") + lax.axis_index("y")).astype(jnp.int32).reshape(1)
    jc_arr = jnp.concatenate([j_arr, c_arr])
    order = [(n, l) for n, _, _, _ in BIG for l in range(DEPTH)]
    dims = {n: (r, c_, k) for n, r, c_, k in BIG}

    wdims = dict(dims, w_up=(D, D_FF, "chip"))
    full = {(n, l): _cast_into_full(j_arr, big_w[n], l, *wdims[n]) for n, l in order}
    mixers = ("w_attn_out", "w_conv_out", "w_o")

    def gather_of(keys):
        return _gather_comm([(full[k],) + wdims[k[0]] for k in keys])

    carried = {("proj", 0): [(n, 0) for n in mixers] + [("w_up", 0)], ("attn", 0): [("w_down", 0)],
               ("mlp", 0): [("w_in", 1)] + [(n, 1) for n in mixers] + [("w_up", 1)], ("attn", 1): [("w_down", 1)]}

    def carry(fn, where, *args):
        keys = carried.get(where)
        if keys is None:
            return fn(*args)
        res, got = fn(*args, comm=gather_of(keys))
        full.update(zip(keys, got))
        return res

    first = _run_comm(_gather_comm([(full["w_in", 0],) + dims["w_in"]], conv_w), "gather_weights")
    full["w_in", 0] = first[0]
    conv_w_full = jnp.transpose(first[1], (0, 2, 1, 3)).reshape(DEPTH, 3, D)
    attn_bias = _attn_bias_table()

    xs = [x.reshape(T, D)]
    saved = []
    for l in range(DEPTH):
        ht, pqkv, pconv, pgate = carry(_norm_proj, ("proj", l), xs[-1], g_mix[l:l + 1], full["w_in", l])
        att, att_t, lse = carry(_attn_fwd, ("attn", l), pqkv, sinks[l], attn_bias, S)
        x1, ya, yc, mg_t, cv_t = _mix_fwd(xs[-1], att, pconv, pgate, conv_w_full[l], conv_b[l:l + 1], b_gates[l:l + 1],
                                          full["w_attn_out", l], full["w_conv_out", l], full["w_o", l], S)
        x2, a = carry(_mlp_fwd, ("mlp", l), x1, g_mlp[l:l + 1], full["w_up", l], full["w_down", l].reshape(N_CHIP, D, D))
        saved.append(dict(ht=ht, pqkv=pqkv, pconv=pconv, pgate=pgate, att=att, att_t=att_t, lse=lse, cv_t=cv_t, x1=x1, ya=ya,
                          yc=yc, mg_t=mg_t, a=a))
        xs.append(x2)

    loss_stats, dx = _loss_bwd(xs[-1], g_final.reshape(1, D), loss_target.reshape(T, D))

    parts = dict(g_mix=[None] * DEPTH, b_gates=[None] * DEPTH, sinks=[None] * DEPTH, conv=[None] * DEPTH,
                 g_mlp=[None] * DEPTH, loss=loss_stats)
    gf, gb, pre, got, mine = {}, {}, {}, {}, {}

    def sibling_exchange(keys):
        return _sibling_exchange_comm([(gb[k],) + dims[k[0]] for k in keys])

    def half_adds(keys, sib):
        for k, s in zip(keys, sib):
            pre[k] = _half_add(jc_arr, gf[k], s, *dims[k[0]])

    def chip_exchange(keys):
        return _chip_exchange_comm([pre[k][0] for k in keys])

    def owner_sums(keys):
        for n, l in keys:
            mine[n] = _owner_sum(jc_arr, pre[n, l][1], got[n, l], l, mine.get(n))

    def run(fn, comms, *args):
        if not comms:
            return fn(*args), []
        res, arrived = fn(*args, comm=_compose(*comms))
        outs, pos = [], 0
        for cm in comms:
            outs.append(arrived[pos:pos + len(cm.out_shape)])
            pos += len(cm.out_shape)
        return res, outs

    assert DEPTH == 2
    upper = [(n, 1) for n, _, _, _ in BIG]
    early_mlp, early_mix = [("w_up", 0), ("w_down", 0)], [(n, 0) for n in mixers]
    early = early_mix + early_mlp
    for l in reversed(range(DEPTH)):
        W = {n: full[(n, l)] for n in big_w}
        sv = saved[l]
        last = l == 0
        mlp_args = (dx, sv["x1"], sv["a"], g_mlp[l:l + 1], W["w_up"], W["w_down"].reshape(N_CHIP, D, D))
        (dx1, da, u_t, h2_t, dyb, parts["g_mlp"][l]), arrived = run(_mlp_bwd, [chip_exchange(upper)] if last else [], *mlp_args)
        if last:
            got.update(zip(upper, arrived[0]))
        gf["w_up", l], gb["w_up", l] = _dw(h2_t, da)
        gf["w_down", l], gb["w_down", l] = _dw(u_t, dyb.reshape(-1, _dw_chunk(T), D))
        mix_args = (dx1, sv["ya"], sv["yc"], sv["pgate"], sv["pconv"], conv_w_full[l], conv_b[l:l + 1], b_gates[l:l + 1],
                    W["w_attn_out"], W["w_conv_out"], W["w_o"], S)
        (datt, dya, dyc, dgate, dconv, parts["b_gates"][l], parts["conv"][l]), arrived = run(
            _mix_bwd, [sibling_exchange(early_mlp)] if last else [], *mix_args)
        if last:
            half_adds(early_mlp, arrived[0])
        gf["w_o", l], gb["w_o", l] = _dw(sv["mg_t"], dx1)
        gf["w_attn_out", l], gb["w_attn_out", l] = _dw(sv["att_t"], dya)
        gf["w_conv_out", l], gb["w_conv_out", l] = _dw(sv["cv_t"], dyc)
        if last:
            half_adds(early_mix, _run_comm(sibling_exchange(early_mix), "grad_sibling_exchange_mixers"))
        attn_args = (sv["pqkv"], datt, sv["lse"], sinks[l], attn_bias, S)
        (dq, dkv, parts["sinks"][l]), arrived = run(_attn_bwd, [chip_exchange(early)] if last else [], *attn_args)
        if last:
            got.update(zip(early, arrived[0]))
            owner_sums(upper + early)
        pieces = [(dq, D), (dkv, QKV_W - D), (dconv, CONV_W), (dgate, GATE_W)]
        gf["w_in", l], gb["w_in", l] = _dw_pieces(sv["ht"], pieces)
        in_args = (pieces, W["w_in"], xs[l], dx1, g_mix[l:l + 1])
        if not last:
            (dx, parts["g_mix"][l]), arrived = run(_inproj_bwd, [sibling_exchange(upper)], *in_args)
            half_adds(upper, arrived[0])
        else:
            tail = [("w_in", 0)]
            half_adds(tail, _run_comm(sibling_exchange(tail), "grad_sibling_exchange"))
            done = _sibling_assemble_comm(list(mine.values()), [(1,) if n == "w_in" else (0, 1) for n in mine])
            (dx, parts["g_mix"][l]), arrived = run(_inproj_bwd, [chip_exchange(tail), done], *in_args)
            got.update(zip(tail, arrived[0]))
            mine = dict(zip(mine, arrived[1]))
            owner_sums(tail)
            mine["w_in"] = _run_comm(_sibling_assemble_comm([mine["w_in"]], [(0,)]), "grad_sibling_assemble")[0]
    grads = mine

    res = {}
    for n in big_w:
        res[n] = tuple(_adamw(big_w[n], grads[n], big_m[n], big_v[n]))

    small_p = dict(g_mix=g_mix, b_gates=b_gates, sinks=sinks, conv_w=conv_w, conv_b=conv_b, g_mlp=g_mlp, g_final=g_final.reshape(1, D))
    small_m = dict(g_mix=m_g_mix, b_gates=m_b_gates, sinks=m_sinks, conv_w=m_conv_w, conv_b=m_conv_b, g_mlp=m_g_mlp,
                   g_final=m_g_final.reshape(1, D))
    small_v = dict(g_mix=v_g_mix, b_gates=v_b_gates, sinks=v_sinks, conv_w=v_conv_w, conv_b=v_conv_b, g_mlp=v_g_mlp,
                   g_final=v_g_final.reshape(1, D))
    loss, small = _small_step(parts, small_p, small_m, small_v)
    for n, vals in small.items():
        res[n] = tuple(v.reshape(D) for v in vals) if n == "g_final" else tuple(vals)

    weights = ["g_mix", "w_in", "b_gates", "sinks", "w_attn_out", "conv_w", "conv_b", "w_conv_out", "w_o", "g_mlp", "w_up",
               "w_down", "g_final"]
    out = [loss.reshape(()), dx.reshape(B, S, D)]
    for k in range(4):
        out += [res[n][k] for n in weights]
    return tuple(out)
```

```python
import functools

import numpy as np
import jax
import jax.numpy as jnp
from jax import lax
from jax.experimental import pallas as pl
from jax.experimental.pallas import tpu as pltpu

F32 = jnp.float32
BF16 = jnp.bfloat16

D_MODEL = 1024
HEAD_DIM = 64
N_Q_HEADS = 16
N_KV_HEADS = 4
GQA_GROUP = 4
WINDOW = 128
D_FF = 4096
DEPTH = 2
RMS_EPS = 1e-6
NEG_INF = -1e30
ATTN_SCALE = HEAD_DIM ** -0.5
QKV_W = 1536
CONV_W = 3072
GATE_W = 2048
IN_COLS = QKV_W + CONV_W + GATE_W
COL_TILE = 512
N_CHIP = 4
ADAM_LR = 0.001
ADAM_B1 = 0.9
ADAM_B2 = 0.999
ADAM_EPS = 1e-08
ADAM_WD = 0.01
ADAM_STEP = 10
V7X_VMEM_BYTES = 64 * 2 ** 20
VMEM_LIMIT = V7X_VMEM_BYTES - 8 * 2 ** 20
MESH = pl.DeviceIdType.MESH
ANY = pl.BlockSpec(memory_space=pl.ANY)
SMALL_ROWS = 24

_SLOPES = [float(v) for v in np.power(np.float32(2.0), -8.0 * np.arange(1, N_Q_HEADS + 1, dtype=np.float32) / N_Q_HEADS)]


def _params(*sem):
    return pltpu.CompilerParams(dimension_semantics=sem, vmem_limit_bytes=VMEM_LIMIT)


class _Hosted:
    def __init__(self, inputs, out_shape, aliases, scratch, start, finish):
        self.inputs, self.out_shape, self.aliases, self.scratch = list(inputs), list(out_shape), dict(aliases), list(scratch)
        self.start, self.finish = start, finish


def _hosted_call(comm, kern, *, out_shape, grid, in_specs, out_specs, args, name, sem, scratch_shapes=(), aliases=None):
    single = not isinstance(out_shape, (tuple, list))
    outs = [out_shape] if single else list(out_shape)
    ospecs = [out_specs] if single else list(out_specs)
    aliases = dict(aliases or {})
    if comm is None:
        res = pl.pallas_call(kern, out_shape=outs, grid=grid, in_specs=list(in_specs), out_specs=ospecs,
                             scratch_shapes=list(scratch_shapes), input_output_aliases=aliases, name=name,
                             compiler_params=_params(*sem))(*args)
        return res[0] if single else res
    n_in, n_out, n_scr = len(args), len(outs), len(scratch_shapes)
    ci, co, cs = len(comm.inputs), len(comm.out_shape), len(comm.scratch)

    def body(*refs):
        cuts = np.cumsum([0, n_in, ci, n_out, co, n_scr, cs])
        a, b, c, d, e, f = [refs[lo:hi] for lo, hi in zip(cuts[:-1], cuts[1:])]
        ids = [pl.program_id(k) for k in range(len(grid))]
        first = functools.reduce(jnp.logical_and, [i == 0 for i in ids])
        last = functools.reduce(jnp.logical_and, [i == n - 1 for i, n in zip(ids, grid)])
        pl.when(first)(lambda: comm.start(b, d, f))
        kern(*a, *c, *e)
        pl.when(last)(lambda: comm.finish(b, d, f))

    res = pl.pallas_call(
        body, out_shape=outs + comm.out_shape, grid=grid, in_specs=list(in_specs) + [ANY] * ci, out_specs=ospecs + [ANY] * co,
        scratch_shapes=list(scratch_shapes) + comm.scratch,
        input_output_aliases=aliases | {n_in + i: n_out + o for i, o in comm.aliases.items()},
        name=name + "_carrier", compiler_params=_params(*(["arbitrary"] * len(grid))))(*args, *comm.inputs)
    main = res[:n_out]
    return (main[0] if single else main), res[n_out:]


def _nt(a, b):
    return lax.dot_general(a, b, (((1,), (1,)), ((), ())), preferred_element_type=F32)


def _tn(a, b):
    return lax.dot_general(a, b, (((0,), (0,)), ((), ())), preferred_element_type=F32)


def _nn(a, b):
    return jnp.dot(a, b, preferred_element_type=F32)


def _rms_stats(xf):
    r = lax.rsqrt(jnp.mean(xf * xf, axis=-1, keepdims=True) + RMS_EPS)
    return r, xf * r


def _rms_bwd(dh, xh, r, g):
    dxh = dh * g
    dx = r * (dxh - xh * jnp.mean(dxh * xh, axis=-1, keepdims=True))
    dg = jnp.sum(dh * xh, axis=0, keepdims=True)
    return dx, dg


def _dw_chunk(T):
    return min(2048, T)


def _resident(shape):
    return pl.BlockSpec(shape, lambda *_: (0,) * len(shape), pipeline_mode=pl.Buffered(1))


def _norm_proj(x, g, w, comm=None):
    T, D = x.shape
    tm = min(512, T)
    tk = _dw_chunk(T)
    per = tk // tm
    widths = (QKV_W, CONV_W, GATE_W)

    def kern(x_ref, g_ref, w_ref, ht_ref, *o_refs):
        _, xh = _rms_stats(x_ref[...])
        h = (xh * g_ref[...]).astype(BF16)
        ht_ref[...] = h.T
        off = 0
        for o_ref, wd in zip(o_refs, widths):
            o_ref[...] = _nn(h, w_ref[:, off:off + wd]).astype(BF16)
            off += wd

    row = lambda wd: pl.BlockSpec((tm, wd), lambda i: (i, 0))
    return _hosted_call(
        comm, kern,
        out_shape=[jax.ShapeDtypeStruct((T // tk, D, tk), BF16)] + [jax.ShapeDtypeStruct((T, wd), BF16) for wd in widths],
        grid=(T // tm,), in_specs=[row(D), pl.BlockSpec((1, D), lambda i: (0, 0)), _resident((D, IN_COLS))],
        out_specs=[pl.BlockSpec((None, D, tm), lambda i: (i // per, 0, i % per))] + [row(wd) for wd in widths],
        name="norm_proj", sem=("parallel",), args=(x, g, w))


GW = GQA_GROUP * WINDOW
BAND = 2 * WINDOW
KV_W = N_KV_HEADS * HEAD_DIM


def _attn_bias_table():
    jj = np.arange(BAND)[:, None]
    col = np.arange(GW)[None, :]
    dist = WINDOW + (col % WINDOW) - jj
    valid = (dist >= 0) & (dist < WINDOW)
    slopes = np.asarray(_SLOPES, np.float32).reshape(N_KV_HEADS, GQA_GROUP)
    tab = np.empty((2, N_KV_HEADS, BAND, GW), np.float32)
    for hk in range(N_KV_HEADS):
        bias = -slopes[hk][col // WINDOW] * dist.astype(np.float32)
        tab[0, hk] = np.where(valid, bias, np.float32(NEG_INF))
        tab[1, hk] = np.where(valid & (jj >= WINDOW), bias, np.float32(NEG_INF))
    return jnp.asarray(tab)


def _stack_heads(ref, hk):
    return jnp.concatenate(
        [ref[:, HEAD_DIM * (GQA_GROUP * hk + g): HEAD_DIM * (GQA_GROUP * hk + g + 1)] for g in range(GQA_GROUP)], axis=0)


def _kv_band(cur_ref, prev_ref, hk):
    k0 = N_Q_HEADS * HEAD_DIM
    sl = slice(HEAD_DIM * hk, HEAD_DIM * (hk + 1))
    ksl, vsl = slice(k0 + sl.start, k0 + sl.stop), slice(k0 + KV_W + sl.start, k0 + KV_W + sl.stop)
    k_band = jnp.concatenate([prev_ref[:, sl], cur_ref[:, ksl]], axis=0)
    v_band = jnp.concatenate([prev_ref[:, KV_W + sl.start:KV_W + sl.stop], cur_ref[:, vsl]], axis=0)
    return k_band, v_band


def _lane_row(vals):
    return jnp.concatenate([jnp.full((1, WINDOW), v, F32) for v in vals], axis=1)


def _attn_fwd(pqkv, sinks, bias, seq, comm=None):
    T = pqkv.shape[0]
    nblk = seq // WINDOW

    def kern(sink_ref, cur_ref, prev_ref, bias_ref, o_ref, ot_ref, lse_ref):
        i = pl.program_id(0)
        first_i = ((i % nblk) == 0).astype(jnp.int32)
        bands = [_kv_band(cur_ref, prev_ref, hk) for hk in range(N_KV_HEADS)]
        sts = [_nt(bands[hk][0], _stack_heads(cur_ref, hk) * ATTN_SCALE) + bias_ref[first_i, hk] for hk in range(N_KV_HEADS)]
        ps, scales = [], []
        for hk in range(N_KV_HEADS):
            heads = [GQA_GROUP * hk + g for g in range(GQA_GROUP)]
            sink = _lane_row([sink_ref[h] for h in heads])
            m = jnp.maximum(jnp.max(sts[hk], axis=0, keepdims=True), sink)
            p = jnp.exp(sts[hk] - m)
            den = jnp.sum(p, axis=0, keepdims=True) + jnp.exp(sink - m)
            lse = m + jnp.log(den)
            for g, h in enumerate(heads):
                lse_ref[h:h + 1, :] = lse[:, WINDOW * g:WINDOW * (g + 1)]
            ps.append(p.astype(BF16))
            scales.append(1.0 / den)
        for hk in range(N_KV_HEADS):
            ot = _tn(bands[hk][1], ps[hk]) * scales[hk]
            for g in range(GQA_GROUP):
                h = GQA_GROUP * hk + g
                ot_ref[HEAD_DIM * h:HEAD_DIM * (h + 1), :] = ot[:, WINDOW * g:WINDOW * (g + 1)].astype(BF16)
        o_ref[...] = ot_ref[...].T

    return _hosted_call(
        comm, kern,
        out_shape=(jax.ShapeDtypeStruct((T, D_MODEL), BF16), jax.ShapeDtypeStruct((D_MODEL, T), BF16),
                   jax.ShapeDtypeStruct((N_Q_HEADS, T), F32)),
        grid=(T // WINDOW,),
        in_specs=[pl.BlockSpec(memory_space=pltpu.SMEM),
                  pl.BlockSpec((WINDOW, QKV_W), lambda i: (i, 0)),
                  pl.BlockSpec((WINDOW, 2 * KV_W), lambda i: (jnp.maximum(i - 1, 0), 2)),
                  _resident(bias.shape)],
        out_specs=(pl.BlockSpec((WINDOW, D_MODEL), lambda i: (i, 0)), pl.BlockSpec((D_MODEL, WINDOW), lambda i: (0, i)),
                   pl.BlockSpec((N_Q_HEADS, WINDOW), lambda i: (0, i))),
        name="attn_fwd", sem=("parallel",), args=(sinks, pqkv, pqkv, bias))


def _pick_row(a, row):
    rid = lax.broadcasted_iota(jnp.int32, a.shape, 0)
    return jnp.sum(jnp.where(rid == row, a, 0.0), axis=0, keepdims=True)


def _conv_taps(yc, halo_yc, first_i):
    keep = (1 - first_i).astype(F32)
    p1 = _pick_row(halo_yc, 15) * keep
    p2 = _pick_row(halo_yc, 14) * keep
    rowid = lax.broadcasted_iota(jnp.int32, yc.shape, 0)
    s1 = jnp.where(rowid == 0, p1, pltpu.roll(yc, 1, 0))
    s2 = jnp.where(rowid == 0, p2, jnp.where(rowid == 1, p1, pltpu.roll(yc, 2, 0)))
    return s1, s2


def _mix_fwd(x, att, pconv, pgate, conv_w, conv_b, b_gates, wao, wco, wo, seq):
    T, D = x.shape
    tm = min(512, seq)
    per_seq = seq // tm

    def kern(x_ref, att_ref, pc_ref, halo_ref, pg_ref, cw_ref, cb_ref, bg_ref, wao_ref, wco_ref, wo_ref,
             x1_ref, ya_ref, yc_ref, mgt_ref, cvt_ref):
        first_i = ((pl.program_id(0) % per_seq) == 0).astype(jnp.int32)
        ya = _nn(att_ref[...], wao_ref[...])
        u = pc_ref[:, D:2 * D].astype(F32) * pc_ref[:, 2 * D:3 * D].astype(F32)
        halo_u = halo_ref[:, D:2 * D].astype(F32) * halo_ref[:, 2 * D:3 * D].astype(F32)
        s1, s2 = _conv_taps(u, halo_u, first_i)
        z = cw_ref[0:1, :] * s2 + cw_ref[1:2, :] * s1 + cw_ref[2:3, :] * u
        cv = (pc_ref[:, 0:D].astype(F32) * (z + cb_ref[...])).astype(BF16)
        cvt_ref[...] = cv.T
        yc = _nn(cv, wco_ref[...])
        sa = jax.nn.sigmoid(pg_ref[:, 0:D].astype(F32) + bg_ref[:, 0:D])
        sc = jax.nn.sigmoid(pg_ref[:, D:2 * D].astype(F32) + bg_ref[:, D:2 * D])
        mg = (sa * ya + sc * yc).astype(BF16)
        ya_ref[...] = ya.astype(BF16)
        yc_ref[...] = yc.astype(BF16)
        mgt_ref[...] = mg.T
        x1_ref[...] = x_ref[...] + _nn(mg, wo_ref[...])

    row = lambda w: pl.BlockSpec((tm, w), lambda i: (i, 0))
    full = lambda a, b: pl.BlockSpec((a, b), lambda i: (0, 0))
    tsp = pl.BlockSpec((D, tm), lambda i: (0, i))
    bf, bft = jax.ShapeDtypeStruct((T, D), BF16), jax.ShapeDtypeStruct((D, T), BF16)
    return pl.pallas_call(
        kern, out_shape=(jax.ShapeDtypeStruct((T, D), F32), bf, bf, bft, bft), grid=(T // tm,),
        in_specs=[row(D), row(D), row(CONV_W), pl.BlockSpec((16, CONV_W), lambda i: (jnp.maximum(i * (tm // 16) - 1, 0), 0)),
                  row(GATE_W), full(3, D), full(1, D), full(1, GATE_W)] + [_resident((D, D))] * 3,
        out_specs=(row(D), row(D), row(D), tsp, tsp),
        name="mix_fwd", compiler_params=_params("parallel"))(x, att, pconv, pconv, pgate, conv_w, conv_b, b_gates, wao, wco, wo)


def _mlp_fwd(x1, g, wup, wdn, comm=None):
    T, D = x1.shape
    tm = min(1024, T)
    nj = D_FF // D

    def kern(x_ref, g_ref, wup_ref, wdn_ref, x2_ref, a_ref, h_scr, acc_scr):
        j = pl.program_id(1)

        @pl.when(j == 0)
        def _():
            xf = x_ref[...]
            _, xh = _rms_stats(xf)
            h_scr[...] = (xh * g_ref[...]).astype(BF16)
            acc_scr[...] = xf

        a = _nn(h_scr[...], wup_ref[j])
        a_ref[...] = a.astype(BF16)
        u = jnp.square(jnp.maximum(a, 0.0)).astype(BF16)
        acc_scr[...] += _nn(u, wdn_ref[j])

        @pl.when(j == nj - 1)
        def _():
            x2_ref[...] = acc_scr[...]

    return _hosted_call(
        comm, kern, out_shape=(jax.ShapeDtypeStruct((T, D), F32), jax.ShapeDtypeStruct((T, D_FF), BF16)), grid=(T // tm, nj),
        in_specs=[pl.BlockSpec((tm, D), lambda i, j: (i, 0)), pl.BlockSpec((1, D), lambda i, j: (0, 0)),
                  _resident((nj, D, D)), _resident((nj, D, D))],
        out_specs=(pl.BlockSpec((tm, D), lambda i, j: (i, 0)), pl.BlockSpec((tm, D), lambda i, j: (i, j))),
        scratch_shapes=[pltpu.VMEM((tm, D), BF16), pltpu.VMEM((tm, D), F32)],
        name="mlp_fwd", sem=("parallel", "arbitrary"), args=(x1, g, wup, wdn))


def _loss_bwd(x, g, tgt):
    T, D = x.shape
    tm = min(512, T)

    def kern(x_ref, g_ref, t_ref, st_ref, dx_ref):
        i = pl.program_id(0)

        @pl.when(i == 0)
        def _():
            st_ref[...] = jnp.zeros_like(st_ref)

        gg = g_ref[...]
        r, xh = _rms_stats(x_ref[...])
        e = xh * gg - t_ref[...]
        part = 0.5 * jnp.sum(jnp.mean(e * e, axis=-1, keepdims=True), axis=0, keepdims=True)
        dx, dg = _rms_bwd(e * (1.0 / D), xh, r, gg)
        dx_ref[...] = dx
        st_ref[0:1, :] += dg
        st_ref[1:2, 0:1] += part

    return pl.pallas_call(
        kern, out_shape=(jax.ShapeDtypeStruct((8, D), F32), jax.ShapeDtypeStruct((T, D), F32)), grid=(T // tm,),
        in_specs=[pl.BlockSpec((tm, D), lambda i: (i, 0)), pl.BlockSpec((1, D), lambda i: (0, 0)),
                  pl.BlockSpec((tm, D), lambda i: (i, 0))],
        out_specs=(pl.BlockSpec((8, D), lambda i: (0, 0)), pl.BlockSpec((tm, D), lambda i: (i, 0))),
        name="loss_bwd", compiler_params=_params("arbitrary"))(x, g, tgt)


def _mlp_bwd(dx2, x1, a, g, wup, wdn, comm=None):
    T, D = x1.shape
    tm = min(512, T)
    nj = D_FF // D
    tk = _dw_chunk(T)
    per = tk // tm

    def kern(dx2_ref, x1_ref, a_ref, g_ref, wup_ref, wdn_ref, dx1_ref, da_ref, ut_ref, h2t_ref, dyb_ref, dg_ref, acc_scr):
        i, j = pl.program_id(0), pl.program_id(1)

        @pl.when((i == 0) & (j == 0))
        def _():
            dg_ref[...] = jnp.zeros_like(dg_ref)

        @pl.when(j == 0)
        def _():
            dyb_ref[...] = dx2_ref[...].astype(BF16)
            acc_scr[...] = jnp.zeros_like(acc_scr)

        du = _nt(dyb_ref[...], wdn_ref[j])
        relu = jnp.maximum(a_ref[...].astype(F32), 0.0)
        da = (du * (2.0 * relu)).astype(BF16)
        da_ref[...] = da
        ut_ref[...] = jnp.square(relu).astype(BF16).T
        acc_scr[...] += _nt(da, wup_ref[j])

        @pl.when(j == nj - 1)
        def _():
            gg = g_ref[...]
            r, xh = _rms_stats(x1_ref[...])
            h2t_ref[...] = (xh * gg).astype(BF16).T
            dx, dg = _rms_bwd(acc_scr[...], xh, r, gg)
            dx1_ref[...] = dx2_ref[...] + dx
            dg_ref[...] += dg

    return _hosted_call(
        comm, kern,
        out_shape=(jax.ShapeDtypeStruct((T, D), F32), jax.ShapeDtypeStruct((T, D_FF), BF16),
                   jax.ShapeDtypeStruct((D_FF, T), BF16), jax.ShapeDtypeStruct((T // tk, D, tk), BF16),
                   jax.ShapeDtypeStruct((T, D), BF16), jax.ShapeDtypeStruct((1, D), F32)),
        grid=(T // tm, nj),
        in_specs=[pl.BlockSpec((tm, D), lambda i, j: (i, 0)), pl.BlockSpec((tm, D), lambda i, j: (i, 0)),
                  pl.BlockSpec((tm, D), lambda i, j: (i, j)), pl.BlockSpec((1, D), lambda i, j: (0, 0)),
                  _resident((nj, D, D)), _resident((nj, D, D))],
        out_specs=(pl.BlockSpec((tm, D), lambda i, j: (i, 0)), pl.BlockSpec((tm, D), lambda i, j: (i, j)),
                   pl.BlockSpec((D, tm), lambda i, j: (j, i)), pl.BlockSpec((None, D, tm), lambda i, j: (i // per, 0, i % per)),
                   pl.BlockSpec((tm, D), lambda i, j: (i, 0)), pl.BlockSpec((1, D), lambda i, j: (0, 0))),
        scratch_shapes=[pltpu.VMEM((tm, D), F32)],
        name="mlp_bwd", sem=("arbitrary", "arbitrary"), args=(dx2, x1, a, g, wup, wdn))


def _mix_bwd(dx1, ya, yc, pgate, pconv, conv_w, conv_b, b_gates, wao, wco, wo, seq, comm=None):
    T, D = dx1.shape
    tm = min(256, seq)
    per_seq = seq // tm
    n = T // tm

    def kern(dx_ref, ya_ref, yc_ref, pg_ref, pc_ref, halo_ref, cw_ref, cb_ref, bg_ref, wao_ref, wco_ref, wo_ref,
             datt_ref, dya_ref, dyc_ref, dgt_ref, dcn_ref, dbg_ref, dwb_ref, next_scr):
        i = pl.program_id(0)
        r = n - 1 - i

        @pl.when(i == 0)
        def _():
            dbg_ref[...] = jnp.zeros_like(dbg_ref)
            dwb_ref[...] = jnp.zeros_like(dwb_ref)
            next_scr[...] = jnp.zeros_like(next_scr)

        dm = _nt(dx_ref[...].astype(BF16), wo_ref[...])
        sa = jax.nn.sigmoid(pg_ref[:, 0:D].astype(F32) + bg_ref[:, 0:D])
        sc = jax.nn.sigmoid(pg_ref[:, D:2 * D].astype(F32) + bg_ref[:, D:2 * D])
        dya = (dm * sa).astype(BF16)
        dyc = (dm * sc).astype(BF16)
        dga = dm * ya_ref[...].astype(F32) * (sa * (1.0 - sa))
        dgc = dm * yc_ref[...].astype(F32) * (sc * (1.0 - sc))
        dya_ref[...] = dya
        dyc_ref[...] = dyc
        dgt_ref[:, 0:D] = dga.astype(BF16)
        dgt_ref[:, D:2 * D] = dgc.astype(BF16)
        dbg_ref[:, 0:D] += jnp.sum(dga, axis=0, keepdims=True)
        dbg_ref[:, D:2 * D] += jnp.sum(dgc, axis=0, keepdims=True)
        datt_ref[...] = _nt(dya, wao_ref[...]).astype(BF16)
        dcv = _nt(dyc, wco_ref[...])

        first_i = ((r % per_seq) == 0).astype(jnp.int32)
        keep_next = 1.0 - (((r + 1) % per_seq) == 0).astype(F32)
        cb = pc_ref[:, 0:D].astype(F32)
        cc = pc_ref[:, D:2 * D].astype(F32)
        cu = pc_ref[:, 2 * D:3 * D].astype(F32)
        u = cc * cu
        halo_u = halo_ref[:, D:2 * D].astype(F32) * halo_ref[:, 2 * D:3 * D].astype(F32)
        s1, s2 = _conv_taps(u, halo_u, first_i)
        w0, w1, w2 = cw_ref[0:1, :], cw_ref[1:2, :], cw_ref[2:3, :]
        z = w0 * s2 + w1 * s1 + w2 * u
        dz = dcv * cb
        n1 = next_scr[0:1, :] * keep_next
        n2 = next_scr[1:2, :] * keep_next
        next_scr[...] = dz[0:8, :]
        rowid = lax.broadcasted_iota(jnp.int32, dz.shape, 0)
        u1 = jnp.where(rowid == tm - 1, n1, pltpu.roll(dz, tm - 1, 0))
        u2 = jnp.where(rowid == tm - 1, n2, jnp.where(rowid == tm - 2, n1, pltpu.roll(dz, tm - 2, 0)))
        du = w2 * dz + w1 * u1 + w0 * u2
        dcn_ref[:, 0:D] = (dcv * (z + cb_ref[...])).astype(BF16)
        dcn_ref[:, D:2 * D] = (du * cu).astype(BF16)
        dcn_ref[:, 2 * D:3 * D] = (du * cc).astype(BF16)
        dwb_ref[0:1, :] += jnp.sum(dz * s2, axis=0, keepdims=True)
        dwb_ref[1:2, :] += jnp.sum(dz * s1, axis=0, keepdims=True)
        dwb_ref[2:3, :] += jnp.sum(dz * u, axis=0, keepdims=True)
        dwb_ref[3:4, :] += jnp.sum(dz, axis=0, keepdims=True)

    row = lambda w: pl.BlockSpec((tm, w), lambda i: (n - 1 - i, 0))
    full = lambda a, b: pl.BlockSpec((a, b), lambda i: (0, 0))
    halo = pl.BlockSpec((16, CONV_W), lambda i: (jnp.maximum((n - 1 - i) * (tm // 16) - 1, 0), 0))
    bf = jax.ShapeDtypeStruct((T, D), BF16)
    return _hosted_call(
        comm, kern,
        out_shape=(bf, bf, bf, jax.ShapeDtypeStruct((T, GATE_W), BF16), jax.ShapeDtypeStruct((T, CONV_W), BF16),
                   jax.ShapeDtypeStruct((1, GATE_W), F32), jax.ShapeDtypeStruct((8, D), F32)),
        grid=(n,),
        in_specs=[row(D), row(D), row(D), row(GATE_W), row(CONV_W), halo, full(3, D), full(1, D), full(1, GATE_W)]
        + [_resident((D, D))] * 3,
        out_specs=(row(D), row(D), row(D), row(GATE_W), row(CONV_W), full(1, GATE_W), full(8, D)),
        scratch_shapes=[pltpu.VMEM((8, D), F32)],
        name="mix_bwd", sem=("arbitrary",), args=(dx1, ya, yc, pgate, pconv, pconv, conv_w, conv_b, b_gates, wao, wco, wo))


def _attn_bwd(pqkv, datt, lse, sinks, bias, seq, comm=None):
    T = pqkv.shape[0]
    nblk = seq // WINDOW
    nseq = T // seq
    KVW = KV_W

    def kern(sink_ref, cur_ref, prev_ref, do_ref, lse_ref, bias_ref, dq_ref, dkv_ref, ds_ref, kc_scr, vc_scr, dqt_scr):
        b, st = pl.program_id(0), pl.program_id(1)

        @pl.when((b == 0) & (st == 0))
        def _():
            ds_ref[...] = jnp.zeros_like(ds_ref)

        @pl.when(st == 0)
        def _():
            kc_scr[...] = jnp.zeros_like(kc_scr)
            vc_scr[...] = jnp.zeros_like(vc_scr)

        @pl.when(st < nblk)
        def _():
            first_i = (st == 0).astype(jnp.int32)
            groups = range(N_KV_HEADS)
            bands = [_kv_band(cur_ref, prev_ref, hk) for hk in groups]
            qs = [_stack_heads(cur_ref, hk) for hk in groups]
            dos = [_stack_heads(do_ref, hk) for hk in groups]
            sts = [_nt(bands[hk][0], qs[hk] * ATTN_SCALE) + bias_ref[first_i, hk] for hk in groups]
            dps = [_nt(bands[hk][1], dos[hk]) for hk in groups]
            pbs, dsss = [], []
            for hk in groups:
                heads = [GQA_GROUP * hk + g for g in range(GQA_GROUP)]
                sink = _lane_row([sink_ref[h] for h in heads])
                lse_g = jnp.concatenate([lse_ref[h:h + 1, :] for h in heads], axis=1)
                p = jnp.exp(sts[hk] - lse_g)
                d_row = jnp.sum(p * dps[hk], axis=0, keepdims=True)
                dsss.append((p * (dps[hk] - d_row) * ATTN_SCALE).astype(BF16))
                pbs.append(p.astype(BF16))
                psd = jnp.exp(sink - lse_g) * d_row
                for g, h in enumerate(heads):
                    ds_ref[0:1, h:h + 1] -= jnp.sum(psd[:, WINDOW * g:WINDOW * (g + 1)], axis=1, keepdims=True)
            for hk in groups:
                dqt = _tn(bands[hk][0], dsss[hk])
                dk_b = _nn(dsss[hk], qs[hk])
                dv_b = _nn(pbs[hk], dos[hk])
                for g in range(GQA_GROUP):
                    h = GQA_GROUP * hk + g
                    dqt_scr[HEAD_DIM * h:HEAD_DIM * (h + 1), :] = dqt[:, WINDOW * g:WINDOW * (g + 1)].astype(BF16)
                ksl = slice(HEAD_DIM * hk, HEAD_DIM * (hk + 1))
                vsl = slice(KVW + HEAD_DIM * hk, KVW + HEAD_DIM * (hk + 1))
                dkv_ref[:, ksl] = (kc_scr[:, ksl] + dk_b[0:WINDOW]).astype(BF16)
                dkv_ref[:, vsl] = (vc_scr[:, ksl] + dv_b[0:WINDOW]).astype(BF16)
                kc_scr[:, ksl] = dk_b[WINDOW:2 * WINDOW]
                vc_scr[:, ksl] = dv_b[WINDOW:2 * WINDOW]
            dq_ref[...] = dqt_scr[...].T

        @pl.when(st == nblk)
        def _():
            dkv_ref[:, 0:KVW] = kc_scr[...].astype(BF16)
            dkv_ref[:, KVW:2 * KVW] = vc_scr[...].astype(BF16)

    cur_map = lambda b, s: (b * nblk + jnp.minimum(s, nblk - 1), 0)
    prev_row = lambda b, s: b * nblk + jnp.clip(s - 1, 0, nblk - 1)
    return _hosted_call(
        comm, kern,
        out_shape=(jax.ShapeDtypeStruct((T, D_MODEL), BF16), jax.ShapeDtypeStruct((T, 2 * KVW), BF16),
                   jax.ShapeDtypeStruct((8, 128), F32)),
        grid=(nseq, nblk + 1),
        in_specs=[pl.BlockSpec(memory_space=pltpu.SMEM),
                  pl.BlockSpec((WINDOW, QKV_W), cur_map),
                  pl.BlockSpec((WINDOW, 2 * KVW), lambda b, s: (prev_row(b, s), 2)),
                  pl.BlockSpec((WINDOW, D_MODEL), cur_map),
                  pl.BlockSpec((N_Q_HEADS, WINDOW), lambda b, s: (0, b * nblk + jnp.minimum(s, nblk - 1))),
                  _resident(bias.shape)],
        out_specs=(pl.BlockSpec((WINDOW, D_MODEL), cur_map),
                   pl.BlockSpec((WINDOW, 2 * KVW), lambda b, s: (prev_row(b, s), 0)),
                   pl.BlockSpec((8, 128), lambda b, s: (0, 0))),
        scratch_shapes=[pltpu.VMEM((WINDOW, KVW), F32), pltpu.VMEM((WINDOW, KVW), F32), pltpu.VMEM((D_MODEL, WINDOW), BF16)],
        name="attn_bwd", sem=("arbitrary", "arbitrary"), args=(sinks, pqkv, pqkv, datt, lse, bias))


def _piece_tiles(pieces):
    out, start = [], 0
    for arr, width in pieces:
        out.append((arr, start, width // COL_TILE))
        start += width // COL_TILE
    return out, start


def _inproj_tile(T):
    return min(512, T)


def _inproj_bwd(pieces, w_in, x, dx_in, g, comm=None, tiles=None, into=None):
    T, D = x.shape
    tm = _inproj_tile(T)
    first, count = tiles or (0, T // tm)

    def kern(*refs):
        p_refs = refs[:len(pieces)]
        w_ref, x_ref, dxin_ref, g_ref = refs[len(pieces):len(pieces) + 4]
        dx_ref, dg_ref = refs[-2:]

        @pl.when(pl.program_id(0) == 0)
        def _():
            dg_ref[...] = refs[-4][...] if into else jnp.zeros_like(dg_ref)

        dh, off = None, 0
        for p_ref, (_, width) in zip(p_refs, pieces):
            part = _nt(p_ref[...], w_ref[:, off:off + width])
            dh = part if dh is None else dh + part
            off += width
        gg = g_ref[...]
        r, xh = _rms_stats(x_ref[...])
        dx, dg = _rms_bwd(dh, xh, r, gg)
        dx_ref[...] = dxin_ref[...] + dx
        dg_ref[...] += dg

    row = lambda wd: pl.BlockSpec((tm, wd), lambda i: (first + i, 0))
    vec = pl.BlockSpec((1, D), lambda i: (0, 0))
    n_in = len(pieces) + 4
    return _hosted_call(
        comm, kern, out_shape=(jax.ShapeDtypeStruct((T, D), F32), jax.ShapeDtypeStruct((1, D), F32)), grid=(count,),
        in_specs=[row(wd) for _, wd in pieces] + [_resident((D, IN_COLS)), row(D), row(D), vec] + ([vec, ANY] if into else []),
        out_specs=(row(D), vec), aliases={n_in + 1: 0} if into else None,
        name="inproj_bwd", sem=("arbitrary",), args=(*[a for a, _ in pieces], w_in, x, dx_in, g, *(into[::-1] if into else ())))


def _dw_pieces(lhs_t, pieces):
    nt, K, tk = lhs_t.shape
    tiles, nj = _piece_tiles(pieces)

    def kern(*refs):
        lhs_ref = refs[0]
        p_refs = refs[1:1 + len(tiles)]
        o_ref, ob_ref = refs[1 + len(tiles):]
        j, t = pl.program_id(0), pl.program_id(1)

        @pl.when(t == 0)
        def _():
            o_ref[...] = jnp.zeros_like(o_ref)

        for p_ref, (_, start, n) in zip(p_refs, tiles):
            @pl.when((j >= start) & (j < start + n))
            def _(p_ref=p_ref):
                o_ref[...] += _nn(lhs_ref[t], p_ref[...])

        @pl.when(t == nt - 1)
        def _():
            ob_ref[...] = o_ref[...].astype(BF16)

    def p_map(start, n):
        return lambda j, t: (jnp.where((j >= start) & (j < start + n), t, 0), jnp.clip(j - start, 0, n - 1))

    N = nj * COL_TILE
    return pl.pallas_call(
        kern, out_shape=(jax.ShapeDtypeStruct((K, N), F32), jax.ShapeDtypeStruct((K, N), BF16)), grid=(nj, nt),
        in_specs=[_resident((nt, K, tk))] + [pl.BlockSpec((tk, COL_TILE), p_map(s, n)) for _, s, n in tiles],
        out_specs=(pl.BlockSpec((K, COL_TILE), lambda j, t: (0, j)), pl.BlockSpec((K, COL_TILE), lambda j, t: (0, j))),
        name="dw_pieces", compiler_params=_params("arbitrary", "arbitrary"))(lhs_t, *[a for a, _, _ in tiles])


def _dw(lhs_t, rhs):
    lhs_res, rhs_res = lhs_t.ndim == 3, rhs.ndim == 3
    W = D_MODEL
    if lhs_res:
        nt, K, tk = lhs_t.shape
    else:
        K, tk = lhs_t.shape[0], _dw_chunk(lhs_t.shape[1])
        nt = lhs_t.shape[1] // tk
    N = rhs.shape[-1]

    def kern(lhs_ref, rhs_ref, o_ref, ob_ref):
        t = pl.program_id(2)

        @pl.when(t == 0)
        def _():
            o_ref[...] = jnp.zeros_like(o_ref)

        a = lhs_ref[t] if lhs_res else lhs_ref[...]
        b = rhs_ref[t] if rhs_res else rhs_ref[...]
        o_ref[...] += _nn(a, b.astype(BF16))

        @pl.when(t == nt - 1)
        def _():
            ob_ref[...] = o_ref[...].astype(BF16)

    omap = lambda i, j, t: (i, j)
    lspec = _resident((nt, W, tk)) if lhs_res else pl.BlockSpec((W, tk), lambda i, j, t: (i, t))
    rspec = _resident((nt, tk, W)) if rhs_res else pl.BlockSpec((tk, W), lambda i, j, t: (t, j))
    return pl.pallas_call(
        kern, out_shape=(jax.ShapeDtypeStruct((K, N), F32), jax.ShapeDtypeStruct((K, N), BF16)), grid=(K // W, N // W, nt),
        in_specs=[lspec, rspec], out_specs=(pl.BlockSpec((W, W), omap), pl.BlockSpec((W, W), omap)),
        name="dw", compiler_params=_params("arbitrary", "arbitrary", "arbitrary"))(lhs_t, rhs)


BIG = (("w_in", D_MODEL, IN_COLS, "col"), ("w_attn_out", D_MODEL, D_MODEL, "row"), ("w_conv_out", D_MODEL, D_MODEL, "row"),
       ("w_o", D_MODEL, D_MODEL, "row"), ("w_up", D_MODEL, D_FF, "col"), ("w_down", D_FF, D_MODEL, "row"))


def _shard_dims(rows, cols, kind):
    return (rows, cols // N_CHIP) if kind in ("col", "chip") else (rows // N_CHIP, cols)


def _window(ref, rows, cols, kind, chip, half):
    sr, sc = _shard_dims(rows, cols, kind)
    hr = sr // 2
    if kind == "col":
        return ref.at[pl.ds(half * hr, hr), pl.ds(chip * sc, sc)]
    if kind == "chip":
        return ref.at[chip, pl.ds(half * hr, hr), :]
    return ref.at[pl.ds(chip * sr + half * hr, hr), :]


def _mesh_pos():
    x, y, c = lax.axis_index("x"), lax.axis_index("y"), lax.axis_index("c")
    return x, y, c, 2 * x + y


_REL_BITS = (2, 1, 3)


def _rel_dev(x, y, c, r):
    return ((1 - x, y, c), (x, 1 - y, c), (1 - x, 1 - y, c))[r]


def _for_my_chip(j, fn):
    for js in range(N_CHIP):
        pl.when(j == js)(functools.partial(fn, js))


def _cast_into_full(j_arr, shard, l, rows, cols, kind):
    sr, sc = _shard_dims(rows, cols, kind)
    tr = min(256, sr)

    def kern(j_ref, s_ref, o_ref):
        o_ref[...] = s_ref[...].astype(BF16)

    shape, block = (rows, cols), (tr, sc)
    if kind == "col":
        omap = lambda i, j_ref: (i, j_ref[0])
    elif kind == "chip":
        shape, block = (N_CHIP, rows, sc), (None, tr, sc)
        omap = lambda i, j_ref: (j_ref[0], i, 0)
    else:
        omap = lambda i, j_ref: (j_ref[0] * (sr // tr) + i, 0)
    gs = pltpu.PrefetchScalarGridSpec(
        num_scalar_prefetch=1, grid=(sr // tr,),
        in_specs=[pl.BlockSpec((None, tr, sc), lambda i, j_ref: (l, i, 0))], out_specs=pl.BlockSpec(block, omap))
    return pl.pallas_call(kern, out_shape=jax.ShapeDtypeStruct(shape, BF16), grid_spec=gs, name="cast_into_full",
                          compiler_params=_params("arbitrary"))(j_arr, shard)


def _gather_comm(fulls, cw=None):
    n_big = len(fulls)
    n_piece = n_big + (0 if cw is None else 1)

    def pieces(in_refs, o_refs, js, c):
        def piece(p, chip, half):
            if p == n_big:
                return o_refs[p].at[half, chip]
            _, rows, cols, kind = fulls[p]
            return _window(o_refs[p], rows, cols, kind, chip, half)

        def mine(p):
            return in_refs[p].at[c] if p == n_big else piece(p, js, c)

        return piece, mine

    def local_copies(in_refs, piece, js, loc_sem):
        if cw is None:
            return []
        return [pltpu.make_async_copy(in_refs[n_big].at[half], piece(n_big, js, half), loc_sem.at[half]) for half in range(2)]

    def ici_copy(piece, mine, js, x, y, c, r, p, send_sem, recv_sem):
        return pltpu.make_async_remote_copy(mine(p), piece(p, js, c), send_sem.at[r * n_piece + p], recv_sem.at[r * n_piece + p],
                                            _rel_dev(x, y, c, r), MESH)

    def start(in_refs, o_refs, sems):
        send_sem, recv_sem, _, _, loc_sem = sems
        x, y, c, j = _mesh_pos()

        def run(js):
            piece, mine = pieces(in_refs, o_refs, js, c)
            for cp in local_copies(in_refs, piece, js, loc_sem):
                cp.start()
            for r in range(3):
                for p in range(n_piece):
                    ici_copy(piece, mine, js, x, y, c, r, p, send_sem, recv_sem).start()

        _for_my_chip(j, run)

    def finish(in_refs, o_refs, sems):
        send_sem, recv_sem, fsend_sem, frecv_sem, loc_sem = sems
        x, y, c, j = _mesh_pos()

        def run(js):
            piece, mine = pieces(in_refs, o_refs, js, c)
            fwds = []
            for r in range(3):
                ks = js ^ _REL_BITS[r]
                for p in range(n_piece):
                    got = piece(p, ks, c)
                    pltpu.make_async_remote_copy(got, got, send_sem.at[r * n_piece + p], recv_sem.at[r * n_piece + p],
                                                 _rel_dev(x, y, c, r), MESH).wait_recv()
                    cp = pltpu.make_async_remote_copy(got, got, fsend_sem.at[r * n_piece + p], frecv_sem.at[r * n_piece + p],
                                                      (x, y, 1 - c), MESH)
                    cp.start()
                    fwds.append(cp)
            for r in range(3):
                ks = js ^ _REL_BITS[r]
                for p in range(n_piece):
                    got = piece(p, ks, 1 - c)
                    pltpu.make_async_remote_copy(got, got, fsend_sem.at[r * n_piece + p], frecv_sem.at[r * n_piece + p],
                                                 (x, y, 1 - c), MESH).wait_recv()
            for r in range(3):
                for p in range(n_piece):
                    ici_copy(piece, mine, js, x, y, c, r, p, send_sem, recv_sem).wait_send()
            for cp in fwds:
                cp.wait_send()
            for cp in local_copies(in_refs, piece, js, loc_sem):
                cp.wait()

        _for_my_chip(j, run)

    out_shape = [jax.ShapeDtypeStruct(a.shape, BF16) for a, _, _, _ in fulls]
    ins = [a for a, _, _, _ in fulls]
    if cw is not None:
        out_shape.append(jax.ShapeDtypeStruct((DEPTH, N_CHIP, 3, D_MODEL // N_CHIP), F32))
        ins.append(cw)
    scratch = [pltpu.SemaphoreType.DMA((3 * n_piece,))] * 4 + [pltpu.SemaphoreType.DMA((2,))]
    return _Hosted(ins, out_shape, {p: p for p in range(n_big)}, scratch, start, finish)


def _compose(*comms):
    comms = [cm for cm in comms if cm is not None]
    if len(comms) <= 1:
        return comms[0] if comms else None
    ins, outs, aliases, scratch, cuts = [], [], {}, [], []
    for cm in comms:
        cuts.append((len(ins), len(outs), len(scratch)))
        aliases.update({len(ins) + i: len(outs) + o for i, o in cm.aliases.items()})
        ins, outs, scratch = ins + cm.inputs, outs + cm.out_shape, scratch + cm.scratch

    def parts(a, b, s):
        for cm, (i0, o0, s0) in zip(comms, cuts):
            yield cm, a[i0:i0 + len(cm.inputs)], b[o0:o0 + len(cm.out_shape)], s[s0:s0 + len(cm.scratch)]

    def start(a, b, s):
        for cm, pa, pb, ps in parts(a, b, s):
            cm.start(pa, pb, ps)

    def finish(a, b, s):
        for cm, pa, pb, ps in parts(a, b, s):
            cm.finish(pa, pb, ps)

    return _Hosted(ins, outs, aliases, scratch, start, finish)


def _run_comm(comm, name):
    n_in, n_out = len(comm.inputs), len(comm.out_shape)

    def body(*refs):
        comm.start(refs[:n_in], refs[n_in:n_in + n_out], refs[n_in + n_out:])
        comm.finish(refs[:n_in], refs[n_in:n_in + n_out], refs[n_in + n_out:])

    return pl.pallas_call(body, out_shape=comm.out_shape, in_specs=[ANY] * n_in, out_specs=[ANY] * n_out,
                          input_output_aliases=comm.aliases, scratch_shapes=comm.scratch, name=name)(*comm.inputs)


def _sibling_exchange_comm(gb):
    n = len(gb)

    def copies(g_refs, o_refs, sems, c):
        send_sem, recv_sem = sems
        x, y, _, _ = _mesh_pos()
        return [pltpu.make_async_remote_copy(_window(g_refs[t], rows, cols, kind, chip, 1 - c),
                                             _window(o_refs[t], rows, cols, kind, chip, 1 - c),
                                             send_sem.at[N_CHIP * t + chip], recv_sem.at[N_CHIP * t + chip], (x, y, 1 - c), MESH)
                for t, (_, rows, cols, kind) in enumerate(gb) for chip in range(N_CHIP)]

    def start(g_refs, o_refs, sems):
        for cp in copies(g_refs, o_refs, sems, lax.axis_index("c")):
            cp.start()

    def finish(g_refs, o_refs, sems):
        c = lax.axis_index("c")
        for cp in copies(g_refs, o_refs, sems, 1 - c):
            cp.wait_recv()
        for cp in copies(g_refs, o_refs, sems, c):
            cp.wait_send()

    return _Hosted([a for a, _, _, _ in gb], [jax.ShapeDtypeStruct(a.shape, BF16) for a, _, _, _ in gb], {},
                   [pltpu.SemaphoreType.DMA((N_CHIP * n,))] * 2, start, finish)


def _half_add(jc_arr, g, sib, rows, cols, kind):
    sr, sc = _shard_dims(rows, cols, kind)
    hr = sr // 2

    def kern(jc_ref, g_ref, s_ref, ob_ref, of_ref):
        v = g_ref[...] + s_ref[...].astype(F32)
        ob_ref[...] = v.astype(BF16)

        @pl.when(pl.program_id(0) == jc_ref[0])
        def _():
            of_ref[...] = v

    if kind == "col":
        imap = lambda j, jc_ref: (jc_ref[1], j)
    else:
        imap = lambda j, jc_ref: (2 * j + jc_ref[1], 0)
    gs = pltpu.PrefetchScalarGridSpec(
        num_scalar_prefetch=1, grid=(N_CHIP,),
        in_specs=[pl.BlockSpec((hr, sc), imap), pl.BlockSpec((hr, sc), imap)],
        out_specs=[pl.BlockSpec((None, hr, sc), lambda j, jc_ref: (j, 0, 0)), pl.BlockSpec((hr, sc), lambda j, jc_ref: (0, 0))])
    return pl.pallas_call(
        kern, out_shape=(jax.ShapeDtypeStruct((N_CHIP, hr, sc), BF16), jax.ShapeDtypeStruct((hr, sc), F32)),
        grid_spec=gs, name="grad_half_add", compiler_params=_params("arbitrary"))(jc_arr, g, sib)


def _chip_exchange_comm(sbs):
    n = len(sbs)

    def copies(s_refs, o_refs, sems):
        send_sem, recv_sem = sems
        x, y, c, j = _mesh_pos()
        return [pltpu.make_async_remote_copy(s_refs[t].at[j ^ _REL_BITS[r]], o_refs[t].at[r], send_sem.at[n * r + t],
                                             recv_sem.at[n * r + t], _rel_dev(x, y, c, r), MESH)
                for r in range(3) for t in range(n)]

    def start(s_refs, o_refs, sems):
        for cp in copies(s_refs, o_refs, sems):
            cp.start()

    def finish(s_refs, o_refs, sems):
        for cp in copies(s_refs, o_refs, sems):
            cp.wait()

    return _Hosted(sbs, [jax.ShapeDtypeStruct((3,) + a.shape[1:], BF16) for a in sbs], {},
                   [pltpu.SemaphoreType.DMA((3 * n,))] * 2, start, finish)


def _owner_sum(jc_arr, sf, rb, l, into=None):
    hr, sc = sf.shape

    def kern(jc_ref, s_ref, r0_ref, r1_ref, r2_ref, *rest):
        o_ref = rest[-1]
        o_ref[...] = ((s_ref[...] + r0_ref[...].astype(F32)) + r1_ref[...].astype(F32)) + r2_ref[...].astype(F32)

    in_specs = [pl.BlockSpec((hr, sc), lambda i, jc_ref: (0, 0))]
    in_specs += [pl.BlockSpec((None, hr, sc), lambda i, jc_ref, r=r: (r, 0, 0)) for r in range(3)]
    args = [jc_arr, sf, rb, rb, rb]
    aliases = {}
    if into is not None:
        in_specs.append(ANY)
        args.append(into)
        aliases = {len(args) - 1: 0}
    gs = pltpu.PrefetchScalarGridSpec(
        num_scalar_prefetch=1, grid=(1,), in_specs=in_specs,
        out_specs=pl.BlockSpec((None, hr, sc), lambda i, jc_ref: (l, jc_ref[1], 0)))
    return pl.pallas_call(kern, out_shape=jax.ShapeDtypeStruct((DEPTH, 2 * hr, sc), F32), grid_spec=gs,
                          input_output_aliases=aliases, name="grad_owner_sum", compiler_params=_params("arbitrary"))(*args)


def _sibling_assemble_comm(grads, layers):
    n = len(grads)
    todo = [(q, l) for q in range(n) for l in layers[q]]

    def copies(o_refs, sems, half):
        send_sem, recv_sem = sems
        x, y, c, _ = _mesh_pos()
        out = []
        for k, (q, l) in enumerate(todo):
            hr = grads[q].shape[1] // 2
            w = o_refs[q].at[l, pl.ds(half * hr, hr), :]
            out.append(pltpu.make_async_remote_copy(w, w, send_sem.at[k], recv_sem.at[k], (x, y, 1 - c), MESH))
        return out

    def start(_, o_refs, sems):
        for cp in copies(o_refs, sems, lax.axis_index("c")):
            cp.start()

    def finish(_, o_refs, sems):
        c = lax.axis_index("c")
        for cp in copies(o_refs, sems, 1 - c):
            cp.wait_recv()
        for cp in copies(o_refs, sems, c):
            cp.wait_send()

    return _Hosted(grads, [jax.ShapeDtypeStruct(g.shape, F32) for g in grads], {q: q for q in range(n)},
                   [pltpu.SemaphoreType.DMA((len(todo),))] * 2, start, finish)


def _adamw_math(w, g, m, v):
    m = ADAM_B1 * m + (1.0 - ADAM_B1) * g
    v = ADAM_B2 * v + (1.0 - ADAM_B2) * jnp.square(g)
    m_hat = m / (1.0 - ADAM_B1 ** ADAM_STEP)
    v_hat = v / (1.0 - ADAM_B2 ** ADAM_STEP)
    delta = -ADAM_LR * (m_hat / (jnp.sqrt(v_hat) + ADAM_EPS) + ADAM_WD * w)
    return delta, m, v


def _adamw(w, g, m, v):
    shape = w.shape
    C = shape[-1]
    R = int(np.prod(shape[:-1]))
    tr = min(256, R)
    args = [a.reshape(R, C) for a in (w, g, m, v)]

    def kern(w_ref, g_ref, m_ref, v_ref, go_ref, d_ref, nm_ref, nv_ref):
        g_val = g_ref[...]
        d, nm, nv = _adamw_math(w_ref[...], g_val, m_ref[...], v_ref[...])
        go_ref[...] = g_val
        d_ref[...] = d
        nm_ref[...] = nm
        nv_ref[...] = nv

    spec = pl.BlockSpec((tr, C), lambda i: (i, 0))
    outs = pl.pallas_call(
        kern, out_shape=[jax.ShapeDtypeStruct((R, C), F32)] * 4, grid=(R // tr,), in_specs=[spec] * 4, out_specs=[spec] * 4,
        name="adamw", compiler_params=_params("parallel"))(*args)
    return [o.reshape(shape) for o in outs]


_ROW_G_MIX, _ROW_B_GATES, _ROW_SINKS, _ROW_CONV, _ROW_G_MLP, _ROW_G_FINAL, _ROW_LOSS = 0, 2, 6, 8, 16, 18, 19


def _small_step(parts, params, moms, vels):
    names = ["g_mix", "b_gates", "sinks", "conv_w", "conv_b", "g_mlp", "g_final"]
    D = D_MODEL
    QW = D // N_CHIP
    n_dev = 8

    def body(*refs):
        it = iter(refs)
        dgmix = [next(it) for _ in range(DEPTH)]
        dbg = [next(it) for _ in range(DEPTH)]
        dsk = [next(it) for _ in range(DEPTH)]
        dwb = [next(it) for _ in range(DEPTH)]
        dgmlp = [next(it) for _ in range(DEPTH)]
        lst = next(it)
        p_refs = {n: next(it) for n in names}
        m_refs = {n: next(it) for n in names}
        v_refs = {n: next(it) for n in names}
        loss_ref = next(it)
        outs = {n: [next(it) for _ in range(4)] for n in names}
        pack_ref, all_ref, send_sem, recv_sem = next(it), next(it), next(it), next(it)

        x, y, c, j = _mesh_pos()
        me = 4 * x + 2 * y + c
        pack_ref[...] = jnp.zeros_like(pack_ref)
        for l in range(DEPTH):
            pack_ref[_ROW_G_MIX + l:_ROW_G_MIX + l + 1, :] = dgmix[l][...]
            pack_ref[_ROW_B_GATES + 2 * l:_ROW_B_GATES + 2 * l + 1, :] = dbg[l][:, 0:D]
            pack_ref[_ROW_B_GATES + 2 * l + 1:_ROW_B_GATES + 2 * l + 2, :] = dbg[l][:, D:2 * D]
            pack_ref[_ROW_SINKS + l:_ROW_SINKS + l + 1, 0:128] = dsk[l][0:1, :]
            pack_ref[_ROW_CONV + 4 * l:_ROW_CONV + 4 * l + 4, :] = dwb[l][0:4, :]
            pack_ref[_ROW_G_MLP + l:_ROW_G_MLP + l + 1, :] = dgmlp[l][...]
        pack_ref[_ROW_G_FINAL:_ROW_G_FINAL + 1, :] = lst[0:1, :]
        pack_ref[_ROW_LOSS:_ROW_LOSS + 1, :] = lst[1:2, :]

        all_ref[me] = pack_ref[...]
        cps = []
        for k in range(1, n_dev):
            dx_, dy_, dc_ = (k >> 2) & 1, (k >> 1) & 1, k & 1
            peer = (x ^ dx_, y ^ dy_, c ^ dc_)
            cp = pltpu.make_async_remote_copy(pack_ref, all_ref.at[me], send_sem.at[k - 1], recv_sem.at[k - 1], peer, MESH)
            cp.start()
            cps.append(cp)
        for cp in cps:
            cp.wait()

        tot = all_ref[0]
        for d in range(1, n_dev):
            tot = tot + all_ref[d]
        pack_ref[...] = tot

        loss_ref[...] = pack_ref[_ROW_LOSS:_ROW_LOSS + 1, 0:1]

        def finish(name, idx, g):
            w, m, v = p_refs[name][idx], m_refs[name][idx], v_refs[name][idx]
            d, nm, nv = _adamw_math(w, g, m, v)
            for ref, val in zip(outs[name], (g, d, nm, nv)):
                ref[idx] = val

        for l in range(DEPTH):
            finish("g_mix", (slice(l, l + 1), slice(None)), pack_ref[_ROW_G_MIX + l:_ROW_G_MIX + l + 1, :])
            finish("g_mlp", (slice(l, l + 1), slice(None)), pack_ref[_ROW_G_MLP + l:_ROW_G_MLP + l + 1, :])
            finish("conv_b", (slice(l, l + 1), slice(None)), pack_ref[_ROW_CONV + 4 * l + 3:_ROW_CONV + 4 * l + 4, :])
            finish("sinks", (slice(l, l + 1), slice(None)), pack_ref[_ROW_SINKS + l:_ROW_SINKS + l + 1, 0:N_Q_HEADS])
            for hf in range(2):
                finish("b_gates", (slice(l, l + 1), slice(hf * D, (hf + 1) * D)),
                       pack_ref[_ROW_B_GATES + 2 * l + hf:_ROW_B_GATES + 2 * l + hf + 1, :])
        finish("g_final", (slice(0, 1), slice(None)), pack_ref[_ROW_G_FINAL:_ROW_G_FINAL + 1, :])

        def conv_w_chip(js):
            for l in range(DEPTH):
                for k in range(3):
                    row = _ROW_CONV + 4 * l + k
                    finish("conv_w", (l, slice(k, k + 1), slice(None)), pack_ref[row:row + 1, js * QW:(js + 1) * QW])

        _for_my_chip(j, conv_w_chip)

    vm = pl.BlockSpec(memory_space=pltpu.VMEM)
    ins = (parts["g_mix"] + parts["b_gates"] + parts["sinks"] + parts["conv"] + parts["g_mlp"] + [parts["loss"]]
           + [params[n] for n in names] + [moms[n] for n in names] + [vels[n] for n in names])
    out_shape = [jax.ShapeDtypeStruct((1, 1), F32)]
    for n in names:
        out_shape += [jax.ShapeDtypeStruct(params[n].shape, F32)] * 4
    res = pl.pallas_call(
        body, out_shape=out_shape, in_specs=[vm] * len(ins), out_specs=[vm] * len(out_shape),
        scratch_shapes=[pltpu.VMEM((SMALL_ROWS, D), F32), pltpu.VMEM((n_dev, SMALL_ROWS, D), F32),
                        pltpu.SemaphoreType.DMA((n_dev - 1,)), pltpu.SemaphoreType.DMA((n_dev - 1,))],
        name="small_allreduce_adamw")(*ins)
    loss = res[0]
    out = {n: res[1 + 4 * i:5 + 4 * i] for i, n in enumerate(names)}
    return loss, out


def kernel(x, g_mix, w_in, b_gates, sinks, w_attn_out, conv_w, conv_b, w_conv_out, w_o, g_mlp, w_up, w_down, g_final, loss_target, m_g_mix, m_w_in, m_b_gates, m_sinks, m_w_attn_out, m_conv_w, m_conv_b, m_w_conv_out, m_w_o, m_g_mlp, m_w_up, m_w_down, m_g_final, v_g_mix, v_w_in, v_b_gates, v_sinks, v_w_attn_out, v_conv_w, v_conv_b, v_w_conv_out, v_w_o, v_g_mlp, v_w_up, v_w_down, v_g_final):
    B, S, D = x.shape
    T = B * S
    big_w = dict(w_in=w_in, w_attn_out=w_attn_out, w_conv_out=w_conv_out, w_o=w_o, w_up=w_up, w_down=w_down)
    big_m = dict(w_in=m_w_in, w_attn_out=m_w_attn_out, w_conv_out=m_w_conv_out, w_o=m_w_o, w_up=m_w_up, w_down=m_w_down)
    big_v = dict(w_in=v_w_in, w_attn_out=v_w_attn_out, w_conv_out=v_w_conv_out, w_o=v_w_o, w_up=v_w_up, w_down=v_w_down)

    c_arr = lax.axis_index("c").astype(jnp.int32).reshape(1)
    j_arr = (2 * lax.axis_index("x") + lax.axis_index("y")).astype(jnp.int32).reshape(1)
    jc_arr = jnp.concatenate([j_arr, c_arr])
    order = [(n, l) for n, _, _, _ in BIG for l in range(DEPTH)]
    dims = {n: (r, c_, k) for n, r, c_, k in BIG}

    wdims = dict(dims, w_up=(D, D_FF, "chip"))
    full = {(n, l): _cast_into_full(j_arr, big_w[n], l, *wdims[n]) for n, l in order}
    mixers = ("w_attn_out", "w_conv_out", "w_o")

    def gather_of(keys):
        return _gather_comm([(full[k],) + wdims[k[0]] for k in keys])

    carried = {("proj", 0): [(n, 0) for n in mixers] + [("w_up", 0)], ("attn", 0): [("w_down", 0)],
               ("mlp", 0): [("w_in", 1)] + [(n, 1) for n in mixers] + [("w_up", 1)], ("attn", 1): [("w_down", 1)]}

    def carry(fn, where, *args):
        keys = carried.get(where)
        if keys is None:
            return fn(*args)
        res, got = fn(*args, comm=gather_of(keys))
        full.update(zip(keys, got))
        return res

    first = _run_comm(_gather_comm([(full["w_in", 0],) + dims["w_in"]], conv_w), "gather_weights")
    full["w_in", 0] = first[0]
    conv_w_full = jnp.transpose(first[1], (0, 2, 1, 3)).reshape(DEPTH, 3, D)
    attn_bias = _attn_bias_table()

    xs = [x.reshape(T, D)]
    saved = []
    for l in range(DEPTH):
        ht, pqkv, pconv, pgate = carry(_norm_proj, ("proj", l), xs[-1], g_mix[l:l + 1], full["w_in", l])
        att, att_t, lse = carry(_attn_fwd, ("attn", l), pqkv, sinks[l], attn_bias, S)
        x1, ya, yc, mg_t, cv_t = _mix_fwd(xs[-1], att, pconv, pgate, conv_w_full[l], conv_b[l:l + 1], b_gates[l:l + 1],
                                          full["w_attn_out", l], full["w_conv_out", l], full["w_o", l], S)
        x2, a = carry(_mlp_fwd, ("mlp", l), x1, g_mlp[l:l + 1], full["w_up", l], full["w_down", l].reshape(N_CHIP, D, D))
        saved.append(dict(ht=ht, pqkv=pqkv, pconv=pconv, pgate=pgate, att=att, att_t=att_t, lse=lse, cv_t=cv_t, x1=x1, ya=ya,
                          yc=yc, mg_t=mg_t, a=a))
        xs.append(x2)

    loss_stats, dx = _loss_bwd(xs[-1], g_final.reshape(1, D), loss_target.reshape(T, D))

    parts = dict(g_mix=[None] * DEPTH, b_gates=[None] * DEPTH, sinks=[None] * DEPTH, conv=[None] * DEPTH,
                 g_mlp=[None] * DEPTH, loss=loss_stats)
    gf, gb, pre, got, mine = {}, {}, {}, {}, {}

    def sibling_exchange(keys):
        return _sibling_exchange_comm([(gb[k],) + dims[k[0]] for k in keys])

    def half_adds(keys, sib):
        for k, s in zip(keys, sib):
            pre[k] = _half_add(jc_arr, gf[k], s, *dims[k[0]])

    def chip_exchange(keys):
        return _chip_exchange_comm([pre[k][0] for k in keys])

    def owner_sums(keys):
        for n, l in keys:
            mine[n] = _owner_sum(jc_arr, pre[n, l][1], got[n, l], l, mine.get(n))

    def run(fn, comms, *args, **kw):
        if not comms:
            return fn(*args, **kw), []
        res, arrived = fn(*args, comm=_compose(*comms), **kw)
        outs, pos = [], 0
        for cm in comms:
            outs.append(arrived[pos:pos + len(cm.out_shape)])
            pos += len(cm.out_shape)
        return res, outs

    assert DEPTH == 2
    upper = [(n, 1) for n, _, _, _ in BIG]
    early_mlp, early_mix = [("w_up", 0), ("w_down", 0)], [(n, 0) for n in mixers]
    early = early_mix + early_mlp
    for l in reversed(range(DEPTH)):
        W = {n: full[(n, l)] for n in big_w}
        sv = saved[l]
        last = l == 0
        mlp_args = (dx, sv["x1"], sv["a"], g_mlp[l:l + 1], W["w_up"], W["w_down"].reshape(N_CHIP, D, D))
        (dx1, da, u_t, h2_t, dyb, parts["g_mlp"][l]), arrived = run(_mlp_bwd, [chip_exchange(upper)] if last else [], *mlp_args)
        if last:
            got.update(zip(upper, arrived[0]))
        gf["w_up", l], gb["w_up", l] = _dw(h2_t, da)
        gf["w_down", l], gb["w_down", l] = _dw(u_t, dyb.reshape(-1, _dw_chunk(T), D))
        mix_args = (dx1, sv["ya"], sv["yc"], sv["pgate"], sv["pconv"], conv_w_full[l], conv_b[l:l + 1], b_gates[l:l + 1],
                    W["w_attn_out"], W["w_conv_out"], W["w_o"], S)
        (datt, dya, dyc, dgate, dconv, parts["b_gates"][l], parts["conv"][l]), arrived = run(
            _mix_bwd, [sibling_exchange(early_mlp)] if last else [], *mix_args)
        if last:
            half_adds(early_mlp, arrived[0])
        gf["w_o", l], gb["w_o", l] = _dw(sv["mg_t"], dx1)
        gf["w_attn_out", l], gb["w_attn_out", l] = _dw(sv["att_t"], dya)
        gf["w_conv_out", l], gb["w_conv_out", l] = _dw(sv["cv_t"], dyc)
        if last:
            half_adds(early_mix, _run_comm(sibling_exchange(early_mix), "grad_sibling_exchange_mixers"))
        attn_args = (sv["pqkv"], datt, sv["lse"], sinks[l], attn_bias, S)
        (dq, dkv, parts["sinks"][l]), arrived = run(_attn_bwd, [chip_exchange(early)] if last else [], *attn_args)
        if last:
            got.update(zip(early, arrived[0]))
            owner_sums(upper + early)
        pieces = [(dq, D), (dkv, QKV_W - D), (dconv, CONV_W), (dgate, GATE_W)]
        gf["w_in", l], gb["w_in", l] = _dw_pieces(sv["ht"], pieces)
        in_args = (pieces, W["w_in"], xs[l], dx1, g_mix[l:l + 1])
        if not last:
            (dx, parts["g_mix"][l]), arrived = run(_inproj_bwd, [sibling_exchange(upper)], *in_args)
            half_adds(upper, arrived[0])
        else:
            tail = [("w_in", 0)]
            half_adds(tail, _run_comm(sibling_exchange(tail), "grad_sibling_exchange"))
            done = _sibling_assemble_comm(list(mine.values()), [(1,) if n == "w_in" else (0, 1) for n in mine])
            n_tiles = T // _inproj_tile(T)
            assert n_tiles >= 2
            n_carry = max(1, n_tiles * 3 // 4)
            part, arrived = run(_inproj_bwd, [chip_exchange(tail), done], *in_args, tiles=(0, n_carry))
            dx, parts["g_mix"][l] = _inproj_bwd(*in_args, tiles=(n_carry, n_tiles - n_carry), into=part)
            got.update(zip(tail, arrived[0]))
            mine = dict(zip(mine, arrived[1]))
            owner_sums(tail)
            mine["w_in"] = _run_comm(_sibling_assemble_comm([mine["w_in"]], [(0,)]), "grad_sibling_assemble")[0]
    grads = mine

    res = {}
    for n in big_w:
        res[n] = tuple(_adamw(big_w[n], grads[n], big_m[n], big_v[n]))

    small_p = dict(g_mix=g_mix, b_gates=b_gates, sinks=sinks, conv_w=conv_w, conv_b=conv_b, g_mlp=g_mlp, g_final=g_final.reshape(1, D))
    small_m = dict(g_mix=m_g_mix, b_gates=m_b_gates, sinks=m_sinks, conv_w=m_conv_w, conv_b=m_conv_b, g_mlp=m_g_mlp,
                   g_final=m_g_final.reshape(1, D))
    small_v = dict(g_mix=v_g_mix, b_gates=v_b_gates, sinks=v_sinks, conv_w=v_conv_w, conv_b=v_conv_b, g_mlp=v_g_mlp,
                   g_final=v_g_final.reshape(1, D))
    loss, small = _small_step(parts, small_p, small_m, small_v)
    for n, vals in small.items():
        res[n] = tuple(v.reshape(D) for v in vals) if n == "g_final" else tuple(vals)

    weights = ["g_mix", "w_in", "b_gates", "sinks", "w_attn_out", "conv_w", "conv_b", "w_conv_out", "w_o", "g_mlp", "w_up",
               "w_down", "g_final"]
    out = [loss.reshape(()), dx.reshape(B, S, D)]
    for k in range(4):
        out += [res[n][k] for n in weights]
    return tuple(out)
```

```python
import functools

import numpy as np
import jax
import jax.numpy as jnp
from jax import lax
from jax.experimental import pallas as pl
from jax.experimental.pallas import tpu as pltpu

F32 = jnp.float32
BF16 = jnp.bfloat16

D_MODEL = 1024
HEAD_DIM = 64
N_Q_HEADS = 16
N_KV_HEADS = 4
GQA_GROUP = 4
WINDOW = 128
D_FF = 4096
DEPTH = 2
RMS_EPS = 1e-6
NEG_INF = -1e30
ATTN_SCALE = HEAD_DIM ** -0.5
QKV_W = 1536
CONV_W = 3072
GATE_W = 2048
IN_COLS = QKV_W + CONV_W + GATE_W
COL_TILE = 512
N_CHIP = 4
ADAM_LR = 0.001
ADAM_B1 = 0.9
ADAM_B2 = 0.999
ADAM_EPS = 1e-08
ADAM_WD = 0.01
ADAM_STEP = 10
V7X_VMEM_BYTES = 64 * 2 ** 20
VMEM_LIMIT = V7X_VMEM_BYTES - 8 * 2 ** 20
MESH = pl.DeviceIdType.MESH
ANY = pl.BlockSpec(memory_space=pl.ANY)
SMALL_ROWS = 24

_SLOPES = [float(v) for v in np.power(np.float32(2.0), -8.0 * np.arange(1, N_Q_HEADS + 1, dtype=np.float32) / N_Q_HEADS)]


def _params(*sem):
    return pltpu.CompilerParams(dimension_semantics=sem, vmem_limit_bytes=VMEM_LIMIT)


class _Hosted:
    def __init__(self, inputs, out_shape, aliases, scratch, start, finish):
        self.inputs, self.out_shape, self.aliases, self.scratch = list(inputs), list(out_shape), dict(aliases), list(scratch)
        self.start, self.finish = start, finish


def _hosted_call(comm, kern, *, out_shape, grid, in_specs, out_specs, args, name, sem, scratch_shapes=()):
    single = not isinstance(out_shape, (tuple, list))
    outs = [out_shape] if single else list(out_shape)
    ospecs = [out_specs] if single else list(out_specs)
    if comm is None:
        res = pl.pallas_call(kern, out_shape=outs, grid=grid, in_specs=list(in_specs), out_specs=ospecs,
                             scratch_shapes=list(scratch_shapes), name=name, compiler_params=_params(*sem))(*args)
        return res[0] if single else res
    n_in, n_out, n_scr = len(args), len(outs), len(scratch_shapes)
    ci, co, cs = len(comm.inputs), len(comm.out_shape), len(comm.scratch)

    def body(*refs):
        cuts = np.cumsum([0, n_in, ci, n_out, co, n_scr, cs])
        a, b, c, d, e, f = [refs[lo:hi] for lo, hi in zip(cuts[:-1], cuts[1:])]
        ids = [pl.program_id(k) for k in range(len(grid))]
        first = functools.reduce(jnp.logical_and, [i == 0 for i in ids])
        last = functools.reduce(jnp.logical_and, [i == n - 1 for i, n in zip(ids, grid)])
        pl.when(first)(lambda: comm.start(b, d, f))
        kern(*a, *c, *e)
        pl.when(last)(lambda: comm.finish(b, d, f))

    res = pl.pallas_call(
        body, out_shape=outs + comm.out_shape, grid=grid, in_specs=list(in_specs) + [ANY] * ci, out_specs=ospecs + [ANY] * co,
        scratch_shapes=list(scratch_shapes) + comm.scratch,
        input_output_aliases={n_in + i: n_out + o for i, o in comm.aliases.items()},
        name=name + "_carrier", compiler_params=_params(*(["arbitrary"] * len(grid))))(*args, *comm.inputs)
    main = res[:n_out]
    return (main[0] if single else main), res[n_out:]


def _nt(a, b):
    return lax.dot_general(a, b, (((1,), (1,)), ((), ())), preferred_element_type=F32)


def _tn(a, b):
    return lax.dot_general(a, b, (((0,), (0,)), ((), ())), preferred_element_type=F32)


def _nn(a, b):
    return jnp.dot(a, b, preferred_element_type=F32)


def _rms_stats(xf):
    r = lax.rsqrt(jnp.mean(xf * xf, axis=-1, keepdims=True) + RMS_EPS)
    return r, xf * r


def _rms_bwd(dh, xh, r, g):
    dxh = dh * g
    dx = r * (dxh - xh * jnp.mean(dxh * xh, axis=-1, keepdims=True))
    dg = jnp.sum(dh * xh, axis=0, keepdims=True)
    return dx, dg


def _dw_chunk(T):
    return min(2048, T)


def _resident(shape):
    return pl.BlockSpec(shape, lambda *_: (0,) * len(shape), pipeline_mode=pl.Buffered(1))


def _norm_proj(x, g, w, comm=None):
    T, D = x.shape
    tm = min(512, T)
    tk = _dw_chunk(T)
    per = tk // tm
    widths = (QKV_W, CONV_W, GATE_W)

    def kern(x_ref, g_ref, w_ref, ht_ref, *o_refs):
        _, xh = _rms_stats(x_ref[...])
        h = (xh * g_ref[...]).astype(BF16)
        ht_ref[...] = h.T
        off = 0
        for o_ref, wd in zip(o_refs, widths):
            o_ref[...] = _nn(h, w_ref[:, off:off + wd]).astype(BF16)
            off += wd

    row = lambda wd: pl.BlockSpec((tm, wd), lambda i: (i, 0))
    return _hosted_call(
        comm, kern,
        out_shape=[jax.ShapeDtypeStruct((T // tk, D, tk), BF16)] + [jax.ShapeDtypeStruct((T, wd), BF16) for wd in widths],
        grid=(T // tm,), in_specs=[row(D), pl.BlockSpec((1, D), lambda i: (0, 0)), _resident((D, IN_COLS))],
        out_specs=[pl.BlockSpec((None, D, tm), lambda i: (i // per, 0, i % per))] + [row(wd) for wd in widths],
        name="norm_proj", sem=("parallel",), args=(x, g, w))


GW = GQA_GROUP * WINDOW
BAND = 2 * WINDOW
KV_W = N_KV_HEADS * HEAD_DIM


def _attn_bias_table():
    jj = np.arange(BAND)[:, None]
    col = np.arange(GW)[None, :]
    dist = WINDOW + (col % WINDOW) - jj
    valid = (dist >= 0) & (dist < WINDOW)
    slopes = np.asarray(_SLOPES, np.float32).reshape(N_KV_HEADS, GQA_GROUP)
    tab = np.empty((2, N_KV_HEADS, BAND, GW), np.float32)
    for hk in range(N_KV_HEADS):
        bias = -slopes[hk][col // WINDOW] * dist.astype(np.float32)
        tab[0, hk] = np.where(valid, bias, np.float32(NEG_INF))
        tab[1, hk] = np.where(valid & (jj >= WINDOW), bias, np.float32(NEG_INF))
    return jnp.asarray(tab)


def _stack_heads(ref, hk):
    return jnp.concatenate(
        [ref[:, HEAD_DIM * (GQA_GROUP * hk + g): HEAD_DIM * (GQA_GROUP * hk + g + 1)] for g in range(GQA_GROUP)], axis=0)


def _kv_band(cur_ref, prev_ref, hk):
    k0 = N_Q_HEADS * HEAD_DIM
    sl = slice(HEAD_DIM * hk, HEAD_DIM * (hk + 1))
    ksl, vsl = slice(k0 + sl.start, k0 + sl.stop), slice(k0 + KV_W + sl.start, k0 + KV_W + sl.stop)
    k_band = jnp.concatenate([prev_ref[:, sl], cur_ref[:, ksl]], axis=0)
    v_band = jnp.concatenate([prev_ref[:, KV_W + sl.start:KV_W + sl.stop], cur_ref[:, vsl]], axis=0)
    return k_band, v_band


def _lane_row(vals):
    return jnp.concatenate([jnp.full((1, WINDOW), v, F32) for v in vals], axis=1)


def _attn_fwd(pqkv, sinks, bias, seq, comm=None):
    T = pqkv.shape[0]
    nblk = seq // WINDOW

    def kern(sink_ref, cur_ref, prev_ref, bias_ref, o_ref, ot_ref, lse_ref):
        i = pl.program_id(0)
        first_i = ((i % nblk) == 0).astype(jnp.int32)
        bands = [_kv_band(cur_ref, prev_ref, hk) for hk in range(N_KV_HEADS)]
        sts = [_nt(bands[hk][0], _stack_heads(cur_ref, hk) * ATTN_SCALE) + bias_ref[first_i, hk] for hk in range(N_KV_HEADS)]
        ps, scales = [], []
        for hk in range(N_KV_HEADS):
            heads = [GQA_GROUP * hk + g for g in range(GQA_GROUP)]
            sink = _lane_row([sink_ref[h] for h in heads])
            m = jnp.maximum(jnp.max(sts[hk], axis=0, keepdims=True), sink)
            p = jnp.exp(sts[hk] - m)
            den = jnp.sum(p, axis=0, keepdims=True) + jnp.exp(sink - m)
            lse = m + jnp.log(den)
            for g, h in enumerate(heads):
                lse_ref[h:h + 1, :] = lse[:, WINDOW * g:WINDOW * (g + 1)]
            ps.append(p.astype(BF16))
            scales.append(1.0 / den)
        for hk in range(N_KV_HEADS):
            ot = _tn(bands[hk][1], ps[hk]) * scales[hk]
            for g in range(GQA_GROUP):
                h = GQA_GROUP * hk + g
                ot_ref[HEAD_DIM * h:HEAD_DIM * (h + 1), :] = ot[:, WINDOW * g:WINDOW * (g + 1)].astype(BF16)
        o_ref[...] = ot_ref[...].T

    return _hosted_call(
        comm, kern,
        out_shape=(jax.ShapeDtypeStruct((T, D_MODEL), BF16), jax.ShapeDtypeStruct((D_MODEL, T), BF16),
                   jax.ShapeDtypeStruct((N_Q_HEADS, T), F32)),
        grid=(T // WINDOW,),
        in_specs=[pl.BlockSpec(memory_space=pltpu.SMEM),
                  pl.BlockSpec((WINDOW, QKV_W), lambda i: (i, 0)),
                  pl.BlockSpec((WINDOW, 2 * KV_W), lambda i: (jnp.maximum(i - 1, 0), 2)),
                  _resident(bias.shape)],
        out_specs=(pl.BlockSpec((WINDOW, D_MODEL), lambda i: (i, 0)), pl.BlockSpec((D_MODEL, WINDOW), lambda i: (0, i)),
                   pl.BlockSpec((N_Q_HEADS, WINDOW), lambda i: (0, i))),
        name="attn_fwd", sem=("parallel",), args=(sinks, pqkv, pqkv, bias))


def _pick_row(a, row):
    rid = lax.broadcasted_iota(jnp.int32, a.shape, 0)
    return jnp.sum(jnp.where(rid == row, a, 0.0), axis=0, keepdims=True)


def _conv_taps(yc, halo_yc, first_i):
    keep = (1 - first_i).astype(F32)
    p1 = _pick_row(halo_yc, 15) * keep
    p2 = _pick_row(halo_yc, 14) * keep
    rowid = lax.broadcasted_iota(jnp.int32, yc.shape, 0)
    s1 = jnp.where(rowid == 0, p1, pltpu.roll(yc, 1, 0))
    s2 = jnp.where(rowid == 0, p2, jnp.where(rowid == 1, p1, pltpu.roll(yc, 2, 0)))
    return s1, s2


def _mix_fwd(x, att, pconv, pgate, conv_w, conv_b, b_gates, wao, wco, wo, seq):
    T, D = x.shape
    tm = min(512, seq)
    per_seq = seq // tm

    def kern(x_ref, att_ref, pc_ref, halo_ref, pg_ref, cw_ref, cb_ref, bg_ref, wao_ref, wco_ref, wo_ref,
             x1_ref, ya_ref, yc_ref, mgt_ref, cvt_ref):
        first_i = ((pl.program_id(0) % per_seq) == 0).astype(jnp.int32)
        ya = _nn(att_ref[...], wao_ref[...])
        u = pc_ref[:, D:2 * D].astype(F32) * pc_ref[:, 2 * D:3 * D].astype(F32)
        halo_u = halo_ref[:, D:2 * D].astype(F32) * halo_ref[:, 2 * D:3 * D].astype(F32)
        s1, s2 = _conv_taps(u, halo_u, first_i)
        z = cw_ref[0:1, :] * s2 + cw_ref[1:2, :] * s1 + cw_ref[2:3, :] * u
        cv = (pc_ref[:, 0:D].astype(F32) * (z + cb_ref[...])).astype(BF16)
        cvt_ref[...] = cv.T
        yc = _nn(cv, wco_ref[...])
        sa = jax.nn.sigmoid(pg_ref[:, 0:D].astype(F32) + bg_ref[:, 0:D])
        sc = jax.nn.sigmoid(pg_ref[:, D:2 * D].astype(F32) + bg_ref[:, D:2 * D])
        mg = (sa * ya + sc * yc).astype(BF16)
        ya_ref[...] = ya.astype(BF16)
        yc_ref[...] = yc.astype(BF16)
        mgt_ref[...] = mg.T
        x1_ref[...] = x_ref[...] + _nn(mg, wo_ref[...])

    row = lambda w: pl.BlockSpec((tm, w), lambda i: (i, 0))
    full = lambda a, b: pl.BlockSpec((a, b), lambda i: (0, 0))
    tsp = pl.BlockSpec((D, tm), lambda i: (0, i))
    bf, bft = jax.ShapeDtypeStruct((T, D), BF16), jax.ShapeDtypeStruct((D, T), BF16)
    return pl.pallas_call(
        kern, out_shape=(jax.ShapeDtypeStruct((T, D), F32), bf, bf, bft, bft), grid=(T // tm,),
        in_specs=[row(D), row(D), row(CONV_W), pl.BlockSpec((16, CONV_W), lambda i: (jnp.maximum(i * (tm // 16) - 1, 0), 0)),
                  row(GATE_W), full(3, D), full(1, D), full(1, GATE_W)] + [_resident((D, D))] * 3,
        out_specs=(row(D), row(D), row(D), tsp, tsp),
        name="mix_fwd", compiler_params=_params("parallel"))(x, att, pconv, pconv, pgate, conv_w, conv_b, b_gates, wao, wco, wo)


def _mlp_fwd(x1, g, wup, wdn, comm=None):
    T, D = x1.shape
    tm = min(1024, T)
    nj = D_FF // D

    def kern(x_ref, g_ref, wup_ref, wdn_ref, x2_ref, a_ref, h_scr, acc_scr):
        j = pl.program_id(1)

        @pl.when(j == 0)
        def _():
            xf = x_ref[...]
            _, xh = _rms_stats(xf)
            h_scr[...] = (xh * g_ref[...]).astype(BF16)
            acc_scr[...] = xf

        a = _nn(h_scr[...], wup_ref[j])
        a_ref[...] = a.astype(BF16)
        u = jnp.square(jnp.maximum(a, 0.0)).astype(BF16)
        acc_scr[...] += _nn(u, wdn_ref[j])

        @pl.when(j == nj - 1)
        def _():
            x2_ref[...] = acc_scr[...]

    return _hosted_call(
        comm, kern, out_shape=(jax.ShapeDtypeStruct((T, D), F32), jax.ShapeDtypeStruct((T, D_FF), BF16)), grid=(T // tm, nj),
        in_specs=[pl.BlockSpec((tm, D), lambda i, j: (i, 0)), pl.BlockSpec((1, D), lambda i, j: (0, 0)),
                  _resident((nj, D, D)), _resident((nj, D, D))],
        out_specs=(pl.BlockSpec((tm, D), lambda i, j: (i, 0)), pl.BlockSpec((tm, D), lambda i, j: (i, j))),
        scratch_shapes=[pltpu.VMEM((tm, D), BF16), pltpu.VMEM((tm, D), F32)],
        name="mlp_fwd", sem=("parallel", "arbitrary"), args=(x1, g, wup, wdn))


def _loss_bwd(x, g, tgt):
    T, D = x.shape
    tm = min(512, T)

    def kern(x_ref, g_ref, t_ref, st_ref, dx_ref):
        i = pl.program_id(0)

        @pl.when(i == 0)
        def _():
            st_ref[...] = jnp.zeros_like(st_ref)

        gg = g_ref[...]
        r, xh = _rms_stats(x_ref[...])
        e = xh * gg - t_ref[...]
        part = 0.5 * jnp.sum(jnp.mean(e * e, axis=-1, keepdims=True), axis=0, keepdims=True)
        dx, dg = _rms_bwd(e * (1.0 / D), xh, r, gg)
        dx_ref[...] = dx
        st_ref[0:1, :] += dg
        st_ref[1:2, 0:1] += part

    return pl.pallas_call(
        kern, out_shape=(jax.ShapeDtypeStruct((8, D), F32), jax.ShapeDtypeStruct((T, D), F32)), grid=(T // tm,),
        in_specs=[pl.BlockSpec((tm, D), lambda i: (i, 0)), pl.BlockSpec((1, D), lambda i: (0, 0)),
                  pl.BlockSpec((tm, D), lambda i: (i, 0))],
        out_specs=(pl.BlockSpec((8, D), lambda i: (0, 0)), pl.BlockSpec((tm, D), lambda i: (i, 0))),
        name="loss_bwd", compiler_params=_params("arbitrary"))(x, g, tgt)


def _mlp_bwd(dx2, x1, a, g, wup, wdn, comm=None):
    T, D = x1.shape
    tm = min(512, T)
    nj = D_FF // D
    tk = _dw_chunk(T)
    per = tk // tm

    def kern(dx2_ref, x1_ref, a_ref, g_ref, wup_ref, wdn_ref, dx1_ref, da_ref, ut_ref, h2t_ref, dyb_ref, dg_ref, acc_scr):
        i, j = pl.program_id(0), pl.program_id(1)

        @pl.when((i == 0) & (j == 0))
        def _():
            dg_ref[...] = jnp.zeros_like(dg_ref)

        @pl.when(j == 0)
        def _():
            dyb_ref[...] = dx2_ref[...].astype(BF16)
            acc_scr[...] = jnp.zeros_like(acc_scr)

        du = _nt(dyb_ref[...], wdn_ref[j])
        relu = jnp.maximum(a_ref[...].astype(F32), 0.0)
        da = (du * (2.0 * relu)).astype(BF16)
        da_ref[...] = da
        ut_ref[...] = jnp.square(relu).astype(BF16).T
        acc_scr[...] += _nt(da, wup_ref[j])

        @pl.when(j == nj - 1)
        def _():
            gg = g_ref[...]
            r, xh = _rms_stats(x1_ref[...])
            h2t_ref[...] = (xh * gg).astype(BF16).T
            dx, dg = _rms_bwd(acc_scr[...], xh, r, gg)
            dx1_ref[...] = dx2_ref[...] + dx
            dg_ref[...] += dg

    return _hosted_call(
        comm, kern,
        out_shape=(jax.ShapeDtypeStruct((T, D), F32), jax.ShapeDtypeStruct((T, D_FF), BF16),
                   jax.ShapeDtypeStruct((D_FF, T), BF16), jax.ShapeDtypeStruct((T // tk, D, tk), BF16),
                   jax.ShapeDtypeStruct((T, D), BF16), jax.ShapeDtypeStruct((1, D), F32)),
        grid=(T // tm, nj),
        in_specs=[pl.BlockSpec((tm, D), lambda i, j: (i, 0)), pl.BlockSpec((tm, D), lambda i, j: (i, 0)),
                  pl.BlockSpec((tm, D), lambda i, j: (i, j)), pl.BlockSpec((1, D), lambda i, j: (0, 0)),
                  _resident((nj, D, D)), _resident((nj, D, D))],
        out_specs=(pl.BlockSpec((tm, D), lambda i, j: (i, 0)), pl.BlockSpec((tm, D), lambda i, j: (i, j)),
                   pl.BlockSpec((D, tm), lambda i, j: (j, i)), pl.BlockSpec((None, D, tm), lambda i, j: (i // per, 0, i % per)),
                   pl.BlockSpec((tm, D), lambda i, j: (i, 0)), pl.BlockSpec((1, D), lambda i, j: (0, 0))),
        scratch_shapes=[pltpu.VMEM((tm, D), F32)],
        name="mlp_bwd", sem=("arbitrary", "arbitrary"), args=(dx2, x1, a, g, wup, wdn))


def _mix_bwd(dx1, ya, yc, pgate, pconv, conv_w, conv_b, b_gates, wao, wco, wo, seq, comm=None):
    T, D = dx1.shape
    tm = min(512, seq)
    per_seq = seq // tm
    n = T // tm

    def kern(dx_ref, ya_ref, yc_ref, pg_ref, pc_ref, halo_ref, cw_ref, cb_ref, bg_ref, wao_ref, wco_ref, wo_ref,
             datt_ref, dya_ref, dyc_ref, dgt_ref, dcn_ref, dbg_ref, dwb_ref, next_scr):
        i = pl.program_id(0)
        r = n - 1 - i

        @pl.when(i == 0)
        def _():
            dbg_ref[...] = jnp.zeros_like(dbg_ref)
            dwb_ref[...] = jnp.zeros_like(dwb_ref)
            next_scr[...] = jnp.zeros_like(next_scr)

        dm = _nt(dx_ref[...].astype(BF16), wo_ref[...])
        sa = jax.nn.sigmoid(pg_ref[:, 0:D].astype(F32) + bg_ref[:, 0:D])
        sc = jax.nn.sigmoid(pg_ref[:, D:2 * D].astype(F32) + bg_ref[:, D:2 * D])
        dya = (dm * sa).astype(BF16)
        dyc = (dm * sc).astype(BF16)
        dga = dm * ya_ref[...].astype(F32) * (sa * (1.0 - sa))
        dgc = dm * yc_ref[...].astype(F32) * (sc * (1.0 - sc))
        dya_ref[...] = dya
        dyc_ref[...] = dyc
        dgt_ref[:, 0:D] = dga.astype(BF16)
        dgt_ref[:, D:2 * D] = dgc.astype(BF16)
        dbg_ref[:, 0:D] += jnp.sum(dga, axis=0, keepdims=True)
        dbg_ref[:, D:2 * D] += jnp.sum(dgc, axis=0, keepdims=True)
        datt_ref[...] = _nt(dya, wao_ref[...]).astype(BF16)
        dcv = _nt(dyc, wco_ref[...])

        first_i = ((r % per_seq) == 0).astype(jnp.int32)
        keep_next = 1.0 - (((r + 1) % per_seq) == 0).astype(F32)
        cb = pc_ref[:, 0:D].astype(F32)
        cc = pc_ref[:, D:2 * D].astype(F32)
        cu = pc_ref[:, 2 * D:3 * D].astype(F32)
        u = cc * cu
        halo_u = halo_ref[:, D:2 * D].astype(F32) * halo_ref[:, 2 * D:3 * D].astype(F32)
        s1, s2 = _conv_taps(u, halo_u, first_i)
        w0, w1, w2 = cw_ref[0:1, :], cw_ref[1:2, :], cw_ref[2:3, :]
        z = w0 * s2 + w1 * s1 + w2 * u
        dz = dcv * cb
        n1 = next_scr[0:1, :] * keep_next
        n2 = next_scr[1:2, :] * keep_next
        next_scr[...] = dz[0:8, :]
        rowid = lax.broadcasted_iota(jnp.int32, dz.shape, 0)
        u1 = jnp.where(rowid == tm - 1, n1, pltpu.roll(dz, tm - 1, 0))
        u2 = jnp.where(rowid == tm - 1, n2, jnp.where(rowid == tm - 2, n1, pltpu.roll(dz, tm - 2, 0)))
        du = w2 * dz + w1 * u1 + w0 * u2
        dcn_ref[:, 0:D] = (dcv * (z + cb_ref[...])).astype(BF16)
        dcn_ref[:, D:2 * D] = (du * cu).astype(BF16)
        dcn_ref[:, 2 * D:3 * D] = (du * cc).astype(BF16)
        dwb_ref[0:1, :] += jnp.sum(dz * s2, axis=0, keepdims=True)
        dwb_ref[1:2, :] += jnp.sum(dz * s1, axis=0, keepdims=True)
        dwb_ref[2:3, :] += jnp.sum(dz * u, axis=0, keepdims=True)
        dwb_ref[3:4, :] += jnp.sum(dz, axis=0, keepdims=True)

    row = lambda w: pl.BlockSpec((tm, w), lambda i: (n - 1 - i, 0))
    full = lambda a, b: pl.BlockSpec((a, b), lambda i: (0, 0))
    halo = pl.BlockSpec((16, CONV_W), lambda i: (jnp.maximum((n - 1 - i) * (tm // 16) - 1, 0), 0))
    bf = jax.ShapeDtypeStruct((T, D), BF16)
    return _hosted_call(
        comm, kern,
        out_shape=(bf, bf, bf, jax.ShapeDtypeStruct((T, GATE_W), BF16), jax.ShapeDtypeStruct((T, CONV_W), BF16),
                   jax.ShapeDtypeStruct((1, GATE_W), F32), jax.ShapeDtypeStruct((8, D), F32)),
        grid=(n,),
        in_specs=[row(D), row(D), row(D), row(GATE_W), row(CONV_W), halo, full(3, D), full(1, D), full(1, GATE_W)]
        + [_resident((D, D))] * 3,
        out_specs=(row(D), row(D), row(D), row(GATE_W), row(CONV_W), full(1, GATE_W), full(8, D)),
        scratch_shapes=[pltpu.VMEM((8, D), F32)],
        name="mix_bwd", sem=("arbitrary",), args=(dx1, ya, yc, pgate, pconv, pconv, conv_w, conv_b, b_gates, wao, wco, wo))


def _attn_bwd(pqkv, datt, lse, sinks, bias, seq, comm=None):
    T = pqkv.shape[0]
    nblk = seq // WINDOW
    nseq = T // seq
    KVW = KV_W

    def kern(sink_ref, cur_ref, prev_ref, do_ref, lse_ref, bias_ref, dq_ref, dkv_ref, ds_ref, kc_scr, vc_scr, dqt_scr):
        b, st = pl.program_id(0), pl.program_id(1)

        @pl.when((b == 0) & (st == 0))
        def _():
            ds_ref[...] = jnp.zeros_like(ds_ref)

        @pl.when(st == 0)
        def _():
            kc_scr[...] = jnp.zeros_like(kc_scr)
            vc_scr[...] = jnp.zeros_like(vc_scr)

        @pl.when(st < nblk)
        def _():
            first_i = (st == 0).astype(jnp.int32)
            groups = range(N_KV_HEADS)
            bands = [_kv_band(cur_ref, prev_ref, hk) for hk in groups]
            qs = [_stack_heads(cur_ref, hk) for hk in groups]
            dos = [_stack_heads(do_ref, hk) for hk in groups]
            sts = [_nt(bands[hk][0], qs[hk] * ATTN_SCALE) + bias_ref[first_i, hk] for hk in groups]
            dps = [_nt(bands[hk][1], dos[hk]) for hk in groups]
            pbs, dsss = [], []
            for hk in groups:
                heads = [GQA_GROUP * hk + g for g in range(GQA_GROUP)]
                sink = _lane_row([sink_ref[h] for h in heads])
                lse_g = jnp.concatenate([lse_ref[h:h + 1, :] for h in heads], axis=1)
                p = jnp.exp(sts[hk] - lse_g)
                d_row = jnp.sum(p * dps[hk], axis=0, keepdims=True)
                dsss.append((p * (dps[hk] - d_row) * ATTN_SCALE).astype(BF16))
                pbs.append(p.astype(BF16))
                psd = jnp.exp(sink - lse_g) * d_row
                for g, h in enumerate(heads):
                    ds_ref[0:1, h:h + 1] -= jnp.sum(psd[:, WINDOW * g:WINDOW * (g + 1)], axis=1, keepdims=True)
            for hk in groups:
                dqt = _tn(bands[hk][0], dsss[hk])
                dk_b = _nn(dsss[hk], qs[hk])
                dv_b = _nn(pbs[hk], dos[hk])
                for g in range(GQA_GROUP):
                    h = GQA_GROUP * hk + g
                    dqt_scr[HEAD_DIM * h:HEAD_DIM * (h + 1), :] = dqt[:, WINDOW * g:WINDOW * (g + 1)].astype(BF16)
                ksl = slice(HEAD_DIM * hk, HEAD_DIM * (hk + 1))
                vsl = slice(KVW + HEAD_DIM * hk, KVW + HEAD_DIM * (hk + 1))
                dkv_ref[:, ksl] = (kc_scr[:, ksl] + dk_b[0:WINDOW]).astype(BF16)
                dkv_ref[:, vsl] = (vc_scr[:, ksl] + dv_b[0:WINDOW]).astype(BF16)
                kc_scr[:, ksl] = dk_b[WINDOW:2 * WINDOW]
                vc_scr[:, ksl] = dv_b[WINDOW:2 * WINDOW]
            dq_ref[...] = dqt_scr[...].T

        @pl.when(st == nblk)
        def _():
            dkv_ref[:, 0:KVW] = kc_scr[...].astype(BF16)
            dkv_ref[:, KVW:2 * KVW] = vc_scr[...].astype(BF16)

    cur_map = lambda b, s: (b * nblk + jnp.minimum(s, nblk - 1), 0)
    prev_row = lambda b, s: b * nblk + jnp.clip(s - 1, 0, nblk - 1)
    return _hosted_call(
        comm, kern,
        out_shape=(jax.ShapeDtypeStruct((T, D_MODEL), BF16), jax.ShapeDtypeStruct((T, 2 * KVW), BF16),
                   jax.ShapeDtypeStruct((8, 128), F32)),
        grid=(nseq, nblk + 1),
        in_specs=[pl.BlockSpec(memory_space=pltpu.SMEM),
                  pl.BlockSpec((WINDOW, QKV_W), cur_map),
                  pl.BlockSpec((WINDOW, 2 * KVW), lambda b, s: (prev_row(b, s), 2)),
                  pl.BlockSpec((WINDOW, D_MODEL), cur_map),
                  pl.BlockSpec((N_Q_HEADS, WINDOW), lambda b, s: (0, b * nblk + jnp.minimum(s, nblk - 1))),
                  _resident(bias.shape)],
        out_specs=(pl.BlockSpec((WINDOW, D_MODEL), cur_map),
                   pl.BlockSpec((WINDOW, 2 * KVW), lambda b, s: (prev_row(b, s), 0)),
                   pl.BlockSpec((8, 128), lambda b, s: (0, 0))),
        scratch_shapes=[pltpu.VMEM((WINDOW, KVW), F32), pltpu.VMEM((WINDOW, KVW), F32), pltpu.VMEM((D_MODEL, WINDOW), BF16)],
        name="attn_bwd", sem=("arbitrary", "arbitrary"), args=(sinks, pqkv, pqkv, datt, lse, bias))


def _piece_tiles(pieces):
    out, start = [], 0
    for arr, width in pieces:
        out.append((arr, start, width // COL_TILE))
        start += width // COL_TILE
    return out, start


def _inproj_bwd(pieces, w_in, x, dx_in, g, comm=None):
    T, D = x.shape
    tm = min(512, T)

    def kern(*refs):
        p_refs = refs[:len(pieces)]
        w_ref, x_ref, dxin_ref, g_ref, dx_ref, dg_ref = refs[len(pieces):]

        @pl.when(pl.program_id(0) == 0)
        def _():
            dg_ref[...] = jnp.zeros_like(dg_ref)

        dh, off = None, 0
        for p_ref, (_, width) in zip(p_refs, pieces):
            part = _nt(p_ref[...], w_ref[:, off:off + width])
            dh = part if dh is None else dh + part
            off += width
        gg = g_ref[...]
        r, xh = _rms_stats(x_ref[...])
        dx, dg = _rms_bwd(dh, xh, r, gg)
        dx_ref[...] = dxin_ref[...] + dx
        dg_ref[...] += dg

    row = lambda wd: pl.BlockSpec((tm, wd), lambda i: (i, 0))
    return _hosted_call(
        comm, kern, out_shape=(jax.ShapeDtypeStruct((T, D), F32), jax.ShapeDtypeStruct((1, D), F32)), grid=(T // tm,),
        in_specs=[row(wd) for _, wd in pieces] + [_resident((D, IN_COLS)), row(D), row(D), pl.BlockSpec((1, D), lambda i: (0, 0))],
        out_specs=(row(D), pl.BlockSpec((1, D), lambda i: (0, 0))),
        name="inproj_bwd", sem=("arbitrary",), args=(*[a for a, _ in pieces], w_in, x, dx_in, g))


def _dw_pieces(lhs_t, pieces):
    nt, K, tk = lhs_t.shape
    tiles, nj = _piece_tiles(pieces)

    def kern(*refs):
        lhs_ref = refs[0]
        p_refs = refs[1:1 + len(tiles)]
        o_ref, ob_ref = refs[1 + len(tiles):]
        j, t = pl.program_id(0), pl.program_id(1)

        @pl.when(t == 0)
        def _():
            o_ref[...] = jnp.zeros_like(o_ref)

        for p_ref, (_, start, n) in zip(p_refs, tiles):
            @pl.when((j >= start) & (j < start + n))
            def _(p_ref=p_ref):
                o_ref[...] += _nn(lhs_ref[t], p_ref[...])

        @pl.when(t == nt - 1)
        def _():
            ob_ref[...] = o_ref[...].astype(BF16)

    def p_map(start, n):
        return lambda j, t: (jnp.where((j >= start) & (j < start + n), t, 0), jnp.clip(j - start, 0, n - 1))

    N = nj * COL_TILE
    return pl.pallas_call(
        kern, out_shape=(jax.ShapeDtypeStruct((K, N), F32), jax.ShapeDtypeStruct((K, N), BF16)), grid=(nj, nt),
        in_specs=[_resident((nt, K, tk))] + [pl.BlockSpec((tk, COL_TILE), p_map(s, n)) for _, s, n in tiles],
        out_specs=(pl.BlockSpec((K, COL_TILE), lambda j, t: (0, j)), pl.BlockSpec((K, COL_TILE), lambda j, t: (0, j))),
        name="dw_pieces", compiler_params=_params("arbitrary", "arbitrary"))(lhs_t, *[a for a, _, _ in tiles])


def _dw(lhs_t, rhs):
    lhs_res, rhs_res = lhs_t.ndim == 3, rhs.ndim == 3
    W = D_MODEL
    if lhs_res:
        nt, K, tk = lhs_t.shape
    else:
        K, tk = lhs_t.shape[0], _dw_chunk(lhs_t.shape[1])
        nt = lhs_t.shape[1] // tk
    N = rhs.shape[-1]

    def kern(lhs_ref, rhs_ref, o_ref, ob_ref):
        t = pl.program_id(2)

        @pl.when(t == 0)
        def _():
            o_ref[...] = jnp.zeros_like(o_ref)

        a = lhs_ref[t] if lhs_res else lhs_ref[...]
        b = rhs_ref[t] if rhs_res else rhs_ref[...]
        o_ref[...] += _nn(a, b.astype(BF16))

        @pl.when(t == nt - 1)
        def _():
            ob_ref[...] = o_ref[...].astype(BF16)

    omap = lambda i, j, t: (i, j)
    lspec = _resident((nt, W, tk)) if lhs_res else pl.BlockSpec((W, tk), lambda i, j, t: (i, t))
    rspec = _resident((nt, tk, W)) if rhs_res else pl.BlockSpec((tk, W), lambda i, j, t: (t, j))
    return pl.pallas_call(
        kern, out_shape=(jax.ShapeDtypeStruct((K, N), F32), jax.ShapeDtypeStruct((K, N), BF16)), grid=(K // W, N // W, nt),
        in_specs=[lspec, rspec], out_specs=(pl.BlockSpec((W, W), omap), pl.BlockSpec((W, W), omap)),
        name="dw", compiler_params=_params("arbitrary", "arbitrary", "arbitrary"))(lhs_t, rhs)


BIG = (("w_in", D_MODEL, IN_COLS, "col"), ("w_attn_out", D_MODEL, D_MODEL, "row"), ("w_conv_out", D_MODEL, D_MODEL, "row"),
       ("w_o", D_MODEL, D_MODEL, "row"), ("w_up", D_MODEL, D_FF, "col"), ("w_down", D_FF, D_MODEL, "row"))


def _shard_dims(rows, cols, kind):
    return (rows, cols // N_CHIP) if kind in ("col", "chip") else (rows // N_CHIP, cols)


def _window(ref, rows, cols, kind, chip, half):
    sr, sc = _shard_dims(rows, cols, kind)
    hr = sr // 2
    if kind == "col":
        return ref.at[pl.ds(half * hr, hr), pl.ds(chip * sc, sc)]
    if kind == "chip":
        return ref.at[chip, pl.ds(half * hr, hr), :]
    return ref.at[pl.ds(chip * sr + half * hr, hr), :]


def _mesh_pos():
    x, y, c = lax.axis_index("x"), lax.axis_index("y"), lax.axis_index("c")
    return x, y, c, 2 * x + y


_REL_BITS = (2, 1, 3)


def _rel_dev(x, y, c, r):
    return ((1 - x, y, c), (x, 1 - y, c), (1 - x, 1 - y, c))[r]


def _for_my_chip(j, fn):
    for js in range(N_CHIP):
        pl.when(j == js)(functools.partial(fn, js))


def _cast_into_full(j_arr, shard, l, rows, cols, kind):
    sr, sc = _shard_dims(rows, cols, kind)
    tr = min(256, sr)

    def kern(j_ref, s_ref, o_ref):
        o_ref[...] = s_ref[...].astype(BF16)

    shape, block = (rows, cols), (tr, sc)
    if kind == "col":
        omap = lambda i, j_ref: (i, j_ref[0])
    elif kind == "chip":
        shape, block = (N_CHIP, rows, sc), (None, tr, sc)
        omap = lambda i, j_ref: (j_ref[0], i, 0)
    else:
        omap = lambda i, j_ref: (j_ref[0] * (sr // tr) + i, 0)
    gs = pltpu.PrefetchScalarGridSpec(
        num_scalar_prefetch=1, grid=(sr // tr,),
        in_specs=[pl.BlockSpec((None, tr, sc), lambda i, j_ref: (l, i, 0))], out_specs=pl.BlockSpec(block, omap))
    return pl.pallas_call(kern, out_shape=jax.ShapeDtypeStruct(shape, BF16), grid_spec=gs, name="cast_into_full",
                          compiler_params=_params("arbitrary"))(j_arr, shard)


def _gather_comm(fulls, cw=None):
    n_big = len(fulls)
    n_piece = n_big + (0 if cw is None else 1)

    def pieces(in_refs, o_refs, js, c):
        def piece(p, chip, half):
            if p == n_big:
                return o_refs[p].at[half, chip]
            _, rows, cols, kind = fulls[p]
            return _window(o_refs[p], rows, cols, kind, chip, half)

        def mine(p):
            return in_refs[p].at[c] if p == n_big else piece(p, js, c)

        return piece, mine

    def local_copies(in_refs, piece, js, loc_sem):
        if cw is None:
            return []
        return [pltpu.make_async_copy(in_refs[n_big].at[half], piece(n_big, js, half), loc_sem.at[half]) for half in range(2)]

    def ici_copy(piece, mine, js, x, y, c, r, p, send_sem, recv_sem):
        return pltpu.make_async_remote_copy(mine(p), piece(p, js, c), send_sem.at[r * n_piece + p], recv_sem.at[r * n_piece + p],
                                            _rel_dev(x, y, c, r), MESH)

    def start(in_refs, o_refs, sems):
        send_sem, recv_sem, _, _, loc_sem = sems
        x, y, c, j = _mesh_pos()

        def run(js):
            piece, mine = pieces(in_refs, o_refs, js, c)
            for cp in local_copies(in_refs, piece, js, loc_sem):
                cp.start()
            for r in range(3):
                for p in range(n_piece):
                    ici_copy(piece, mine, js, x, y, c, r, p, send_sem, recv_sem).start()

        _for_my_chip(j, run)

    def finish(in_refs, o_refs, sems):
        send_sem, recv_sem, fsend_sem, frecv_sem, loc_sem = sems
        x, y, c, j = _mesh_pos()

        def run(js):
            piece, mine = pieces(in_refs, o_refs, js, c)
            fwds = []
            for r in range(3):
                ks = js ^ _REL_BITS[r]
                for p in range(n_piece):
                    got = piece(p, ks, c)
                    pltpu.make_async_remote_copy(got, got, send_sem.at[r * n_piece + p], recv_sem.at[r * n_piece + p],
                                                 _rel_dev(x, y, c, r), MESH).wait_recv()
                    cp = pltpu.make_async_remote_copy(got, got, fsend_sem.at[r * n_piece + p], frecv_sem.at[r * n_piece + p],
                                                      (x, y, 1 - c), MESH)
                    cp.start()
                    fwds.append(cp)
            for r in range(3):
                ks = js ^ _REL_BITS[r]
                for p in range(n_piece):
                    got = piece(p, ks, 1 - c)
                    pltpu.make_async_remote_copy(got, got, fsend_sem.at[r * n_piece + p], frecv_sem.at[r * n_piece + p],
                                                 (x, y, 1 - c), MESH).wait_recv()
            for r in range(3):
                for p in range(n_piece):
                    ici_copy(piece, mine, js, x, y, c, r, p, send_sem, recv_sem).wait_send()
            for cp in fwds:
                cp.wait_send()
            for cp in local_copies(in_refs, piece, js, loc_sem):
                cp.wait()

        _for_my_chip(j, run)

    out_shape = [jax.ShapeDtypeStruct(a.shape, BF16) for a, _, _, _ in fulls]
    ins = [a for a, _, _, _ in fulls]
    if cw is not None:
        out_shape.append(jax.ShapeDtypeStruct((DEPTH, N_CHIP, 3, D_MODEL // N_CHIP), F32))
        ins.append(cw)
    scratch = [pltpu.SemaphoreType.DMA((3 * n_piece,))] * 4 + [pltpu.SemaphoreType.DMA((2,))]
    return _Hosted(ins, out_shape, {p: p for p in range(n_big)}, scratch, start, finish)


def _compose(*comms):
    comms = [cm for cm in comms if cm is not None]
    if len(comms) <= 1:
        return comms[0] if comms else None
    ins, outs, aliases, scratch, cuts = [], [], {}, [], []
    for cm in comms:
        cuts.append((len(ins), len(outs), len(scratch)))
        aliases.update({len(ins) + i: len(outs) + o for i, o in cm.aliases.items()})
        ins, outs, scratch = ins + cm.inputs, outs + cm.out_shape, scratch + cm.scratch

    def parts(a, b, s):
        for cm, (i0, o0, s0) in zip(comms, cuts):
            yield cm, a[i0:i0 + len(cm.inputs)], b[o0:o0 + len(cm.out_shape)], s[s0:s0 + len(cm.scratch)]

    def start(a, b, s):
        for cm, pa, pb, ps in parts(a, b, s):
            cm.start(pa, pb, ps)

    def finish(a, b, s):
        for cm, pa, pb, ps in parts(a, b, s):
            cm.finish(pa, pb, ps)

    return _Hosted(ins, outs, aliases, scratch, start, finish)


def _run_comm(comm, name):
    n_in, n_out = len(comm.inputs), len(comm.out_shape)

    def body(*refs):
        comm.start(refs[:n_in], refs[n_in:n_in + n_out], refs[n_in + n_out:])
        comm.finish(refs[:n_in], refs[n_in:n_in + n_out], refs[n_in + n_out:])

    return pl.pallas_call(body, out_shape=comm.out_shape, in_specs=[ANY] * n_in, out_specs=[ANY] * n_out,
                          input_output_aliases=comm.aliases, scratch_shapes=comm.scratch, name=name)(*comm.inputs)


def _sibling_exchange_comm(gb):
    n = len(gb)

    def copies(g_refs, o_refs, sems, c):
        send_sem, recv_sem = sems
        x, y, _, _ = _mesh_pos()
        return [pltpu.make_async_remote_copy(_window(g_refs[t], rows, cols, kind, chip, 1 - c),
                                             _window(o_refs[t], rows, cols, kind, chip, 1 - c),
                                             send_sem.at[N_CHIP * t + chip], recv_sem.at[N_CHIP * t + chip], (x, y, 1 - c), MESH)
                for t, (_, rows, cols, kind) in enumerate(gb) for chip in range(N_CHIP)]

    def start(g_refs, o_refs, sems):
        for cp in copies(g_refs, o_refs, sems, lax.axis_index("c")):
            cp.start()

    def finish(g_refs, o_refs, sems):
        c = lax.axis_index("c")
        for cp in copies(g_refs, o_refs, sems, 1 - c):
            cp.wait_recv()
        for cp in copies(g_refs, o_refs, sems, c):
            cp.wait_send()

    return _Hosted([a for a, _, _, _ in gb], [jax.ShapeDtypeStruct(a.shape, BF16) for a, _, _, _ in gb], {},
                   [pltpu.SemaphoreType.DMA((N_CHIP * n,))] * 2, start, finish)


def _half_add(jc_arr, g, sib, rows, cols, kind):
    sr, sc = _shard_dims(rows, cols, kind)
    hr = sr // 2

    def kern(jc_ref, g_ref, s_ref, ob_ref, of_ref):
        v = g_ref[...] + s_ref[...].astype(F32)
        ob_ref[...] = v.astype(BF16)

        @pl.when(pl.program_id(0) == jc_ref[0])
        def _():
            of_ref[...] = v

    if kind == "col":
        imap = lambda j, jc_ref: (jc_ref[1], j)
    else:
        imap = lambda j, jc_ref: (2 * j + jc_ref[1], 0)
    gs = pltpu.PrefetchScalarGridSpec(
        num_scalar_prefetch=1, grid=(N_CHIP,),
        in_specs=[pl.BlockSpec((hr, sc), imap), pl.BlockSpec((hr, sc), imap)],
        out_specs=[pl.BlockSpec((None, hr, sc), lambda j, jc_ref: (j, 0, 0)), pl.BlockSpec((hr, sc), lambda j, jc_ref: (0, 0))])
    return pl.pallas_call(
        kern, out_shape=(jax.ShapeDtypeStruct((N_CHIP, hr, sc), BF16), jax.ShapeDtypeStruct((hr, sc), F32)),
        grid_spec=gs, name="grad_half_add", compiler_params=_params("arbitrary"))(jc_arr, g, sib)


def _chip_exchange_comm(sbs):
    n = len(sbs)

    def copies(s_refs, o_refs, sems):
        send_sem, recv_sem = sems
        x, y, c, j = _mesh_pos()
        return [pltpu.make_async_remote_copy(s_refs[t].at[j ^ _REL_BITS[r]], o_refs[t].at[r], send_sem.at[n * r + t],
                                             recv_sem.at[n * r + t], _rel_dev(x, y, c, r), MESH)
                for r in range(3) for t in range(n)]

    def start(s_refs, o_refs, sems):
        for cp in copies(s_refs, o_refs, sems):
            cp.start()

    def finish(s_refs, o_refs, sems):
        for cp in copies(s_refs, o_refs, sems):
            cp.wait()

    return _Hosted(sbs, [jax.ShapeDtypeStruct((3,) + a.shape[1:], BF16) for a in sbs], {},
                   [pltpu.SemaphoreType.DMA((3 * n,))] * 2, start, finish)


def _owner_sum(jc_arr, sf, rb, l, into=None):
    hr, sc = sf.shape

    def kern(jc_ref, s_ref, r0_ref, r1_ref, r2_ref, *rest):
        o_ref = rest[-1]
        o_ref[...] = ((s_ref[...] + r0_ref[...].astype(F32)) + r1_ref[...].astype(F32)) + r2_ref[...].astype(F32)

    in_specs = [pl.BlockSpec((hr, sc), lambda i, jc_ref: (0, 0))]
    in_specs += [pl.BlockSpec((None, hr, sc), lambda i, jc_ref, r=r: (r, 0, 0)) for r in range(3)]
    args = [jc_arr, sf, rb, rb, rb]
    aliases = {}
    if into is not None:
        in_specs.append(ANY)
        args.append(into)
        aliases = {len(args) - 1: 0}
    gs = pltpu.PrefetchScalarGridSpec(
        num_scalar_prefetch=1, grid=(1,), in_specs=in_specs,
        out_specs=pl.BlockSpec((None, hr, sc), lambda i, jc_ref: (l, jc_ref[1], 0)))
    return pl.pallas_call(kern, out_shape=jax.ShapeDtypeStruct((DEPTH, 2 * hr, sc), F32), grid_spec=gs,
                          input_output_aliases=aliases, name="grad_owner_sum", compiler_params=_params("arbitrary"))(*args)


def _sibling_assemble_comm(grads, layers):
    n = len(grads)
    todo = [(q, l) for q in range(n) for l in layers[q]]

    def copies(o_refs, sems, half):
        send_sem, recv_sem = sems
        x, y, c, _ = _mesh_pos()
        out = []
        for k, (q, l) in enumerate(todo):
            hr = grads[q].shape[1] // 2
            w = o_refs[q].at[l, pl.ds(half * hr, hr), :]
            out.append(pltpu.make_async_remote_copy(w, w, send_sem.at[k], recv_sem.at[k], (x, y, 1 - c), MESH))
        return out

    def start(_, o_refs, sems):
        for cp in copies(o_refs, sems, lax.axis_index("c")):
            cp.start()

    def finish(_, o_refs, sems):
        c = lax.axis_index("c")
        for cp in copies(o_refs, sems, 1 - c):
            cp.wait_recv()
        for cp in copies(o_refs, sems, c):
            cp.wait_send()

    return _Hosted(grads, [jax.ShapeDtypeStruct(g.shape, F32) for g in grads], {q: q for q in range(n)},
                   [pltpu.SemaphoreType.DMA((len(todo),))] * 2, start, finish)


def _adamw_math(w, g, m, v):
    m = ADAM_B1 * m + (1.0 - ADAM_B1) * g
    v = ADAM_B2 * v + (1.0 - ADAM_B2) * jnp.square(g)
    m_hat = m / (1.0 - ADAM_B1 ** ADAM_STEP)
    v_hat = v / (1.0 - ADAM_B2 ** ADAM_STEP)
    delta = -ADAM_LR * (m_hat / (jnp.sqrt(v_hat) + ADAM_EPS) + ADAM_WD * w)
    return delta, m, v


def _adamw(w, g, m, v):
    shape = w.shape
    C = shape[-1]
    R = int(np.prod(shape[:-1]))
    tr = min(256, R)
    args = [a.reshape(R, C) for a in (w, g, m, v)]

    def kern(w_ref, g_ref, m_ref, v_ref, go_ref, d_ref, nm_ref, nv_ref):
        g_val = g_ref[...]
        d, nm, nv = _adamw_math(w_ref[...], g_val, m_ref[...], v_ref[...])
        go_ref[...] = g_val
        d_ref[...] = d
        nm_ref[...] = nm
        nv_ref[...] = nv

    spec = pl.BlockSpec((tr, C), lambda i: (i, 0))
    outs = pl.pallas_call(
        kern, out_shape=[jax.ShapeDtypeStruct((R, C), F32)] * 4, grid=(R // tr,), in_specs=[spec] * 4, out_specs=[spec] * 4,
        name="adamw", compiler_params=_params("parallel"))(*args)
    return [o.reshape(shape) for o in outs]


_ROW_G_MIX, _ROW_B_GATES, _ROW_SINKS, _ROW_CONV, _ROW_G_MLP, _ROW_G_FINAL, _ROW_LOSS = 0, 2, 6, 8, 16, 18, 19


def _small_step(parts, params, moms, vels):
    names = ["g_mix", "b_gates", "sinks", "conv_w", "conv_b", "g_mlp", "g_final"]
    D = D_MODEL
    QW = D // N_CHIP
    n_dev = 8

    def body(*refs):
        it = iter(refs)
        dgmix = [next(it) for _ in range(DEPTH)]
        dbg = [next(it) for _ in range(DEPTH)]
        dsk = [next(it) for _ in range(DEPTH)]
        dwb = [next(it) for _ in range(DEPTH)]
        dgmlp = [next(it) for _ in range(DEPTH)]
        lst = next(it)
        p_refs = {n: next(it) for n in names}
        m_refs = {n: next(it) for n in names}
        v_refs = {n: next(it) for n in names}
        loss_ref = next(it)
        outs = {n: [next(it) for _ in range(4)] for n in names}
        pack_ref, all_ref, send_sem, recv_sem = next(it), next(it), next(it), next(it)

        x, y, c, j = _mesh_pos()
        me = 4 * x + 2 * y + c
        pack_ref[...] = jnp.zeros_like(pack_ref)
        for l in range(DEPTH):
            pack_ref[_ROW_G_MIX + l:_ROW_G_MIX + l + 1, :] = dgmix[l][...]
            pack_ref[_ROW_B_GATES + 2 * l:_ROW_B_GATES + 2 * l + 1, :] = dbg[l][:, 0:D]
            pack_ref[_ROW_B_GATES + 2 * l + 1:_ROW_B_GATES + 2 * l + 2, :] = dbg[l][:, D:2 * D]
            pack_ref[_ROW_SINKS + l:_ROW_SINKS + l + 1, 0:128] = dsk[l][0:1, :]
            pack_ref[_ROW_CONV + 4 * l:_ROW_CONV + 4 * l + 4, :] = dwb[l][0:4, :]
            pack_ref[_ROW_G_MLP + l:_ROW_G_MLP + l + 1, :] = dgmlp[l][...]
        pack_ref[_ROW_G_FINAL:_ROW_G_FINAL + 1, :] = lst[0:1, :]
        pack_ref[_ROW_LOSS:_ROW_LOSS + 1, :] = lst[1:2, :]

        all_ref[me] = pack_ref[...]
        cps = []
        for k in range(1, n_dev):
            dx_, dy_, dc_ = (k >> 2) & 1, (k >> 1) & 1, k & 1
            peer = (x ^ dx_, y ^ dy_, c ^ dc_)
            cp = pltpu.make_async_remote_copy(pack_ref, all_ref.at[me], send_sem.at[k - 1], recv_sem.at[k - 1], peer, MESH)
            cp.start()
            cps.append(cp)
        for cp in cps:
            cp.wait()

        tot = all_ref[0]
        for d in range(1, n_dev):
            tot = tot + all_ref[d]
        pack_ref[...] = tot

        loss_ref[...] = pack_ref[_ROW_LOSS:_ROW_LOSS + 1, 0:1]

        def finish(name, idx, g):
            w, m, v = p_refs[name][idx], m_refs[name][idx], v_refs[name][idx]
            d, nm, nv = _adamw_math(w, g, m, v)
            for ref, val in zip(outs[name], (g, d, nm, nv)):
                ref[idx] = val

        for l in range(DEPTH):
            finish("g_mix", (slice(l, l + 1), slice(None)), pack_ref[_ROW_G_MIX + l:_ROW_G_MIX + l + 1, :])
            finish("g_mlp", (slice(l, l + 1), slice(None)), pack_ref[_ROW_G_MLP + l:_ROW_G_MLP + l + 1, :])
            finish("conv_b", (slice(l, l + 1), slice(None)), pack_ref[_ROW_CONV + 4 * l + 3:_ROW_CONV + 4 * l + 4, :])
            finish("sinks", (slice(l, l + 1), slice(None)), pack_ref[_ROW_SINKS + l:_ROW_SINKS + l + 1, 0:N_Q_HEADS])
            for hf in range(2):
                finish("b_gates", (slice(l, l + 1), slice(hf * D, (hf + 1) * D)),
                       pack_ref[_ROW_B_GATES + 2 * l + hf:_ROW_B_GATES + 2 * l + hf + 1, :])
        finish("g_final", (slice(0, 1), slice(None)), pack_ref[_ROW_G_FINAL:_ROW_G_FINAL + 1, :])

        def conv_w_chip(js):
            for l in range(DEPTH):
                for k in range(3):
                    row = _ROW_CONV + 4 * l + k
                    finish("conv_w", (l, slice(k, k + 1), slice(None)), pack_ref[row:row + 1, js * QW:(js + 1) * QW])

        _for_my_chip(j, conv_w_chip)

    vm = pl.BlockSpec(memory_space=pltpu.VMEM)
    ins = (parts["g_mix"] + parts["b_gates"] + parts["sinks"] + parts["conv"] + parts["g_mlp"] + [parts["loss"]]
           + [params[n] for n in names] + [moms[n] for n in names] + [vels[n] for n in names])
    out_shape = [jax.ShapeDtypeStruct((1, 1), F32)]
    for n in names:
        out_shape += [jax.ShapeDtypeStruct(params[n].shape, F32)] * 4
    res = pl.pallas_call(
        body, out_shape=out_shape, in_specs=[vm] * len(ins), out_specs=[vm] * len(out_shape),
        scratch_shapes=[pltpu.VMEM((SMALL_ROWS, D), F32), pltpu.VMEM((n_dev, SMALL_ROWS, D), F32),
                        pltpu.SemaphoreType.DMA((n_dev - 1,)), pltpu.SemaphoreType.DMA((n_dev - 1,))],
        name="small_allreduce_adamw")(*ins)
    loss = res[0]
    out = {n: res[1 + 4 * i:5 + 4 * i] for i, n in enumerate(names)}
    return loss, out


def kernel(x, g_mix, w_in, b_gates, sinks, w_attn_out, conv_w, conv_b, w_conv_out, w_o, g_mlp, w_up, w_down, g_final, loss_target, m_g_mix, m_w_in, m_b_gates, m_sinks, m_w_attn_out, m_conv_w, m_conv_b, m_w_conv_out, m_w_o, m_g_mlp, m_w_up, m_w_down, m_g_final, v_g_mix, v_w_in, v_b_gates, v_sinks, v_w_attn_out, v_conv_w, v_conv_b, v_w_conv_out, v_w_o, v_g_mlp, v_w_up, v_w_down, v_g_final):
    B, S, D = x.shape
    T = B * S
    big_w = dict(w_in=w_in, w_attn_out=w_attn_out, w_conv_out=w_conv_out, w_o=w_o, w_up=w_up, w_down=w_down)
    big_m = dict(w_in=m_w_in, w_attn_out=m_w_attn_out, w_conv_out=m_w_conv_out, w_o=m_w_o, w_up=m_w_up, w_down=m_w_down)
    big_v = dict(w_in=v_w_in, w_attn_out=v_w_attn_out, w_conv_out=v_w_conv_out, w_o=v_w_o, w_up=v_w_up, w_down=v_w_down)

    c_arr = lax.axis_index("c").astype(jnp.int32).reshape(1)
    j_arr = (2 * lax.axis_index("x") + lax.axis_index("y")).astype(jnp.int32).reshape(1)
    jc_arr = jnp.concatenate([j_arr, c_arr])
    order = [(n, l) for n, _, _, _ in BIG for l in range(DEPTH)]
    dims = {n: (r, c_, k) for n, r, c_, k in BIG}

    wdims = dict(dims, w_up=(D, D_FF, "chip"))
    full = {(n, l): _cast_into_full(j_arr, big_w[n], l, *wdims[n]) for n, l in order}
    mixers = ("w_attn_out", "w_conv_out", "w_o")

    def gather_of(keys):
        return _gather_comm([(full[k],) + wdims[k[0]] for k in keys])

    carried = {("proj", 0): [(n, 0) for n in mixers] + [("w_up", 0)], ("attn", 0): [("w_down", 0)],
               ("mlp", 0): [("w_in", 1)] + [(n, 1) for n in mixers] + [("w_up", 1)], ("attn", 1): [("w_down", 1)]}

    def carry(fn, where, *args):
        keys = carried.get(where)
        if keys is None:
            return fn(*args)
        res, got = fn(*args, comm=gather_of(keys))
        full.update(zip(keys, got))
        return res

    first = _run_comm(_gather_comm([(full["w_in", 0],) + dims["w_in"]], conv_w), "gather_weights")
    full["w_in", 0] = first[0]
    conv_w_full = jnp.transpose(first[1], (0, 2, 1, 3)).reshape(DEPTH, 3, D)
    attn_bias = _attn_bias_table()

    xs = [x.reshape(T, D)]
    saved = []
    for l in range(DEPTH):
        ht, pqkv, pconv, pgate = carry(_norm_proj, ("proj", l), xs[-1], g_mix[l:l + 1], full["w_in", l])
        att, att_t, lse = carry(_attn_fwd, ("attn", l), pqkv, sinks[l], attn_bias, S)
        x1, ya, yc, mg_t, cv_t = _mix_fwd(xs[-1], att, pconv, pgate, conv_w_full[l], conv_b[l:l + 1], b_gates[l:l + 1],
                                          full["w_attn_out", l], full["w_conv_out", l], full["w_o", l], S)
        x2, a = carry(_mlp_fwd, ("mlp", l), x1, g_mlp[l:l + 1], full["w_up", l], full["w_down", l].reshape(N_CHIP, D, D))
        saved.append(dict(ht=ht, pqkv=pqkv, pconv=pconv, pgate=pgate, att=att, att_t=att_t, lse=lse, cv_t=cv_t, x1=x1, ya=ya,
                          yc=yc, mg_t=mg_t, a=a))
        xs.append(x2)

    loss_stats, dx = _loss_bwd(xs[-1], g_final.reshape(1, D), loss_target.reshape(T, D))

    parts = dict(g_mix=[None] * DEPTH, b_gates=[None] * DEPTH, sinks=[None] * DEPTH, conv=[None] * DEPTH,
                 g_mlp=[None] * DEPTH, loss=loss_stats)
    gf, gb, pre, got, mine = {}, {}, {}, {}, {}

    def sibling_exchange(keys):
        return _sibling_exchange_comm([(gb[k],) + dims[k[0]] for k in keys])

    def half_adds(keys, sib):
        for k, s in zip(keys, sib):
            pre[k] = _half_add(jc_arr, gf[k], s, *dims[k[0]])

    def chip_exchange(keys):
        return _chip_exchange_comm([pre[k][0] for k in keys])

    def owner_sums(keys):
        for n, l in keys:
            mine[n] = _owner_sum(jc_arr, pre[n, l][1], got[n, l], l, mine.get(n))

    def run(fn, comms, *args):
        if not comms:
            return fn(*args), []
        res, arrived = fn(*args, comm=_compose(*comms))
        outs, pos = [], 0
        for cm in comms:
            outs.append(arrived[pos:pos + len(cm.out_shape)])
            pos += len(cm.out_shape)
        return res, outs

    assert DEPTH == 2
    upper = [(n, 1) for n, _, _, _ in BIG]
    early_mlp, early_mix = [("w_up", 0), ("w_down", 0)], [(n, 0) for n in mixers]
    early = early_mix + early_mlp
    for l in reversed(range(DEPTH)):
        W = {n: full[(n, l)] for n in big_w}
        sv = saved[l]
        last = l == 0
        mlp_args = (dx, sv["x1"], sv["a"], g_mlp[l:l + 1], W["w_up"], W["w_down"].reshape(N_CHIP, D, D))
        (dx1, da, u_t, h2_t, dyb, parts["g_mlp"][l]), arrived = run(_mlp_bwd, [chip_exchange(upper)] if last else [], *mlp_args)
        if last:
            got.update(zip(upper, arrived[0]))
        gf["w_up", l], gb["w_up", l] = _dw(h2_t, da)
        gf["w_down", l], gb["w_down", l] = _dw(u_t, dyb.reshape(-1, _dw_chunk(T), D))
        mix_args = (dx1, sv["ya"], sv["yc"], sv["pgate"], sv["pconv"], conv_w_full[l], conv_b[l:l + 1], b_gates[l:l + 1],
                    W["w_attn_out"], W["w_conv_out"], W["w_o"], S)
        (datt, dya, dyc, dgate, dconv, parts["b_gates"][l], parts["conv"][l]), arrived = run(
            _mix_bwd, [sibling_exchange(early_mlp)] if last else [], *mix_args)
        if last:
            half_adds(early_mlp, arrived[0])
        gf["w_o", l], gb["w_o", l] = _dw(sv["mg_t"], dx1)
        gf["w_attn_out", l], gb["w_attn_out", l] = _dw(sv["att_t"], dya)
        gf["w_conv_out", l], gb["w_conv_out", l] = _dw(sv["cv_t"], dyc)
        if last:
            half_adds(early_mix, _run_comm(sibling_exchange(early_mix), "grad_sibling_exchange_mixers"))
        attn_args = (sv["pqkv"], datt, sv["lse"], sinks[l], attn_bias, S)
        (dq, dkv, parts["sinks"][l]), arrived = run(_attn_bwd, [chip_exchange(early)] if last else [], *attn_args)
        if last:
            got.update(zip(early, arrived[0]))
            owner_sums(upper + early)
        pieces = [(dq, D), (dkv, QKV_W - D), (dconv, CONV_W), (dgate, GATE_W)]
        gf["w_in", l], gb["w_in", l] = _dw_pieces(sv["ht"], pieces)
        in_args = (pieces, W["w_in"], xs[l], dx1, g_mix[l:l + 1])
        if not last:
            (dx, parts["g_mix"][l]), arrived = run(_inproj_bwd, [sibling_exchange(upper)], *in_args)
            half_adds(upper, arrived[0])
        else:
            tail = [("w_in", 0)]
            half_adds(tail, _run_comm(sibling_exchange(tail), "grad_sibling_exchange"))
            done = _sibling_assemble_comm(list(mine.values()), [(1,) if n == "w_in" else (0, 1) for n in mine])
            (dx, parts["g_mix"][l]), arrived = run(_inproj_bwd, [chip_exchange(tail), done], *in_args)
            got.update(zip(tail, arrived[0]))
            mine = dict(zip(mine, arrived[1]))
            owner_sums(tail)
            mine["w_in"] = _run_comm(_sibling_assemble_comm([mine["w_in"]], [(0,)]), "grad_sibling_assemble")[0]
    grads = mine

    res = {}
    for n in big_w:
        res[n] = tuple(_adamw(big_w[n], grads[n], big_m[n], big_v[n]))

    small_p = dict(g_mix=g_mix, b_gates=b_gates, sinks=sinks, conv_w=conv_w, conv_b=conv_b, g_mlp=g_mlp, g_final=g_final.reshape(1, D))
    small_m = dict(g_mix=m_g_mix, b_gates=m_b_gates, sinks=m_sinks, conv_w=m_conv_w, conv_b=m_conv_b, g_mlp=m_g_mlp,
                   g_final=m_g_final.reshape(1, D))
    small_v = dict(g_mix=v_g_mix, b_gates=v_b_gates, sinks=v_sinks, conv_w=v_conv_w, conv_b=v_conv_b, g_mlp=v_g_mlp,
                   g_final=v_g_final.reshape(1, D))
    loss, small = _small_step(parts, small_p, small_m, small_v)
    for n, vals in small.items():
        res[n] = tuple(v.reshape(D) for v in vals) if n == "g_final" else tuple(vals)

    weights = ["g_mix", "w_in", "b_gates", "sinks", "w_attn_out", "conv_w", "conv_b", "w_conv_out", "w_o", "g_mlp", "w_up",
               "w_down", "g_final"]
    out = [loss.reshape(()), dx.reshape(B, S, D)]
    for k in range(4):
        out += [res[n][k] for n in weights]
    return tuple(out)
```

```python
import functools

import numpy as np
import jax
import jax.numpy as jnp
from jax import lax
from jax.experimental import pallas as pl
from jax.experimental.pallas import tpu as pltpu

F32 = jnp.float32
BF16 = jnp.bfloat16

D_MODEL = 1024
HEAD_DIM = 64
N_Q_HEADS = 16
N_KV_HEADS = 4
GQA_GROUP = 4
WINDOW = 128
D_FF = 4096
DEPTH = 2
RMS_EPS = 1e-6
NEG_INF = -1e30
ATTN_SCALE = HEAD_DIM ** -0.5
QKV_W = 1536
CONV_W = 3072
GATE_W = 2048
IN_COLS = QKV_W + CONV_W + GATE_W
COL_TILE = 512
N_CHIP = 4
ADAM_LR = 0.001
ADAM_B1 = 0.9
ADAM_B2 = 0.999
ADAM_EPS = 1e-08
ADAM_WD = 0.01
ADAM_STEP = 10
V7X_VMEM_BYTES = 64 * 2 ** 20
VMEM_LIMIT = V7X_VMEM_BYTES - 8 * 2 ** 20
MESH = pl.DeviceIdType.MESH
ANY = pl.BlockSpec(memory_space=pl.ANY)
SMALL_ROWS = 24

_SLOPES = [float(v) for v in np.power(np.float32(2.0), -8.0 * np.arange(1, N_Q_HEADS + 1, dtype=np.float32) / N_Q_HEADS)]


def _params(*sem):
    return pltpu.CompilerParams(dimension_semantics=sem, vmem_limit_bytes=VMEM_LIMIT)


class _Hosted:
    def __init__(self, inputs, out_shape, aliases, scratch, start, finish):
        self.inputs, self.out_shape, self.aliases, self.scratch = list(inputs), list(out_shape), dict(aliases), list(scratch)
        self.start, self.finish = start, finish


def _hosted_call(comm, kern, *, out_shape, grid, in_specs, out_specs, args, name, sem, scratch_shapes=()):
    single = not isinstance(out_shape, (tuple, list))
    outs = [out_shape] if single else list(out_shape)
    ospecs = [out_specs] if single else list(out_specs)
    if comm is None:
        res = pl.pallas_call(kern, out_shape=outs, grid=grid, in_specs=list(in_specs), out_specs=ospecs,
                             scratch_shapes=list(scratch_shapes), name=name, compiler_params=_params(*sem))(*args)
        return res[0] if single else res
    n_in, n_out, n_scr = len(args), len(outs), len(scratch_shapes)
    ci, co, cs = len(comm.inputs), len(comm.out_shape), len(comm.scratch)

    def body(*refs):
        cuts = np.cumsum([0, n_in, ci, n_out, co, n_scr, cs])
        a, b, c, d, e, f = [refs[lo:hi] for lo, hi in zip(cuts[:-1], cuts[1:])]
        ids = [pl.program_id(k) for k in range(len(grid))]
        first = functools.reduce(jnp.logical_and, [i == 0 for i in ids])
        last = functools.reduce(jnp.logical_and, [i == n - 1 for i, n in zip(ids, grid)])
        pl.when(first)(lambda: comm.start(b, d, f))
        kern(*a, *c, *e)
        pl.when(last)(lambda: comm.finish(b, d, f))

    res = pl.pallas_call(
        body, out_shape=outs + comm.out_shape, grid=grid, in_specs=list(in_specs) + [ANY] * ci, out_specs=ospecs + [ANY] * co,
        scratch_shapes=list(scratch_shapes) + comm.scratch,
        input_output_aliases={n_in + i: n_out + o for i, o in comm.aliases.items()},
        name=name + "_carrier", compiler_params=_params(*(["arbitrary"] * len(grid))))(*args, *comm.inputs)
    main = res[:n_out]
    return (main[0] if single else main), res[n_out:]


def _nt(a, b):
    return lax.dot_general(a, b, (((1,), (1,)), ((), ())), preferred_element_type=F32)


def _tn(a, b):
    return lax.dot_general(a, b, (((0,), (0,)), ((), ())), preferred_element_type=F32)


def _nn(a, b):
    return jnp.dot(a, b, preferred_element_type=F32)


def _rms_stats(xf):
    r = lax.rsqrt(jnp.mean(xf * xf, axis=-1, keepdims=True) + RMS_EPS)
    return r, xf * r


def _rms_bwd(dh, xh, r, g):
    dxh = dh * g
    dx = r * (dxh - xh * jnp.mean(dxh * xh, axis=-1, keepdims=True))
    dg = jnp.sum(dh * xh, axis=0, keepdims=True)
    return dx, dg


def _dw_chunk(T):
    return min(2048, T)


def _resident(shape):
    return pl.BlockSpec(shape, lambda *_: (0,) * len(shape), pipeline_mode=pl.Buffered(1))


def _norm_proj(x, g, w, comm=None):
    T, D = x.shape
    tm = min(512, T)
    tk = _dw_chunk(T)
    per = tk // tm
    widths = (QKV_W, CONV_W, GATE_W)

    def kern(x_ref, g_ref, w_ref, ht_ref, *o_refs):
        _, xh = _rms_stats(x_ref[...])
        h = (xh * g_ref[...]).astype(BF16)
        ht_ref[...] = h.T
        off = 0
        for o_ref, wd in zip(o_refs, widths):
            o_ref[...] = _nn(h, w_ref[:, off:off + wd]).astype(BF16)
            off += wd

    row = lambda wd: pl.BlockSpec((tm, wd), lambda i: (i, 0))
    return _hosted_call(
        comm, kern,
        out_shape=[jax.ShapeDtypeStruct((T // tk, D, tk), BF16)] + [jax.ShapeDtypeStruct((T, wd), BF16) for wd in widths],
        grid=(T // tm,), in_specs=[row(D), pl.BlockSpec((1, D), lambda i: (0, 0)), _resident((D, IN_COLS))],
        out_specs=[pl.BlockSpec((None, D, tm), lambda i: (i // per, 0, i % per))] + [row(wd) for wd in widths],
        name="norm_proj", sem=("parallel",), args=(x, g, w))


GW = GQA_GROUP * WINDOW
BAND = 2 * WINDOW
KV_W = N_KV_HEADS * HEAD_DIM


def _attn_bias_table():
    jj = np.arange(BAND)[:, None]
    col = np.arange(GW)[None, :]
    dist = WINDOW + (col % WINDOW) - jj
    valid = (dist >= 0) & (dist < WINDOW)
    slopes = np.asarray(_SLOPES, np.float32).reshape(N_KV_HEADS, GQA_GROUP)
    tab = np.empty((2, N_KV_HEADS, BAND, GW), np.float32)
    for hk in range(N_KV_HEADS):
        bias = -slopes[hk][col // WINDOW] * dist.astype(np.float32)
        tab[0, hk] = np.where(valid, bias, np.float32(NEG_INF))
        tab[1, hk] = np.where(valid & (jj >= WINDOW), bias, np.float32(NEG_INF))
    return jnp.asarray(tab)


def _stack_heads(ref, hk):
    return jnp.concatenate(
        [ref[:, HEAD_DIM * (GQA_GROUP * hk + g): HEAD_DIM * (GQA_GROUP * hk + g + 1)] for g in range(GQA_GROUP)], axis=0)


def _kv_band(cur_ref, prev_ref, hk):
    k0 = N_Q_HEADS * HEAD_DIM
    sl = slice(HEAD_DIM * hk, HEAD_DIM * (hk + 1))
    ksl, vsl = slice(k0 + sl.start, k0 + sl.stop), slice(k0 + KV_W + sl.start, k0 + KV_W + sl.stop)
    k_band = jnp.concatenate([prev_ref[:, sl], cur_ref[:, ksl]], axis=0)
    v_band = jnp.concatenate([prev_ref[:, KV_W + sl.start:KV_W + sl.stop], cur_ref[:, vsl]], axis=0)
    return k_band, v_band


def _lane_row(vals):
    return jnp.concatenate([jnp.full((1, WINDOW), v, F32) for v in vals], axis=1)


def _attn_fwd(pqkv, sinks, bias, seq, comm=None):
    T = pqkv.shape[0]
    nblk = seq // WINDOW

    def kern(sink_ref, cur_ref, prev_ref, bias_ref, o_ref, ot_ref, lse_ref):
        i = pl.program_id(0)
        first_i = ((i % nblk) == 0).astype(jnp.int32)
        bands = [_kv_band(cur_ref, prev_ref, hk) for hk in range(N_KV_HEADS)]
        sts = [_nt(bands[hk][0], _stack_heads(cur_ref, hk) * ATTN_SCALE) + bias_ref[first_i, hk] for hk in range(N_KV_HEADS)]
        ps, scales = [], []
        for hk in range(N_KV_HEADS):
            heads = [GQA_GROUP * hk + g for g in range(GQA_GROUP)]
            sink = _lane_row([sink_ref[h] for h in heads])
            m = jnp.maximum(jnp.max(sts[hk], axis=0, keepdims=True), sink)
            p = jnp.exp(sts[hk] - m)
            den = jnp.sum(p, axis=0, keepdims=True) + jnp.exp(sink - m)
            lse = m + jnp.log(den)
            for g, h in enumerate(heads):
                lse_ref[h:h + 1, :] = lse[:, WINDOW * g:WINDOW * (g + 1)]
            ps.append(p.astype(BF16))
            scales.append(1.0 / den)
        for hk in range(N_KV_HEADS):
            ot = _tn(bands[hk][1], ps[hk]) * scales[hk]
            for g in range(GQA_GROUP):
                h = GQA_GROUP * hk + g
                ot_ref[HEAD_DIM * h:HEAD_DIM * (h + 1), :] = ot[:, WINDOW * g:WINDOW * (g + 1)].astype(BF16)
        o_ref[...] = ot_ref[...].T

    return _hosted_call(
        comm, kern,
        out_shape=(jax.ShapeDtypeStruct((T, D_MODEL), BF16), jax.ShapeDtypeStruct((D_MODEL, T), BF16),
                   jax.ShapeDtypeStruct((N_Q_HEADS, T), F32)),
        grid=(T // WINDOW,),
        in_specs=[pl.BlockSpec(memory_space=pltpu.SMEM),
                  pl.BlockSpec((WINDOW, QKV_W), lambda i: (i, 0)),
                  pl.BlockSpec((WINDOW, 2 * KV_W), lambda i: (jnp.maximum(i - 1, 0), 2)),
                  _resident(bias.shape)],
        out_specs=(pl.BlockSpec((WINDOW, D_MODEL), lambda i: (i, 0)), pl.BlockSpec((D_MODEL, WINDOW), lambda i: (0, i)),
                   pl.BlockSpec((N_Q_HEADS, WINDOW), lambda i: (0, i))),
        name="attn_fwd", sem=("parallel",), args=(sinks, pqkv, pqkv, bias))


def _pick_row(a, row):
    rid = lax.broadcasted_iota(jnp.int32, a.shape, 0)
    return jnp.sum(jnp.where(rid == row, a, 0.0), axis=0, keepdims=True)


def _conv_taps(yc, halo_yc, first_i):
    keep = (1 - first_i).astype(F32)
    p1 = _pick_row(halo_yc, 15) * keep
    p2 = _pick_row(halo_yc, 14) * keep
    rowid = lax.broadcasted_iota(jnp.int32, yc.shape, 0)
    s1 = jnp.where(rowid == 0, p1, pltpu.roll(yc, 1, 0))
    s2 = jnp.where(rowid == 0, p2, jnp.where(rowid == 1, p1, pltpu.roll(yc, 2, 0)))
    return s1, s2


def _mix_fwd(x, att, pconv, pgate, conv_w, conv_b, b_gates, wao, wco, wo, seq):
    T, D = x.shape
    tm = min(512, seq)
    per_seq = seq // tm

    def kern(x_ref, att_ref, pc_ref, halo_ref, pg_ref, cw_ref, cb_ref, bg_ref, wao_ref, wco_ref, wo_ref,
             x1_ref, ya_ref, yc_ref, mgt_ref, cvt_ref):
        first_i = ((pl.program_id(0) % per_seq) == 0).astype(jnp.int32)
        ya = _nn(att_ref[...], wao_ref[...])
        u = pc_ref[:, D:2 * D].astype(F32) * pc_ref[:, 2 * D:3 * D].astype(F32)
        halo_u = halo_ref[:, D:2 * D].astype(F32) * halo_ref[:, 2 * D:3 * D].astype(F32)
        s1, s2 = _conv_taps(u, halo_u, first_i)
        z = cw_ref[0:1, :] * s2 + cw_ref[1:2, :] * s1 + cw_ref[2:3, :] * u
        cv = (pc_ref[:, 0:D].astype(F32) * (z + cb_ref[...])).astype(BF16)
        cvt_ref[...] = cv.T
        yc = _nn(cv, wco_ref[...])
        sa = jax.nn.sigmoid(pg_ref[:, 0:D].astype(F32) + bg_ref[:, 0:D])
        sc = jax.nn.sigmoid(pg_ref[:, D:2 * D].astype(F32) + bg_ref[:, D:2 * D])
        mg = (sa * ya + sc * yc).astype(BF16)
        ya_ref[...] = ya.astype(BF16)
        yc_ref[...] = yc.astype(BF16)
        mgt_ref[...] = mg.T
        x1_ref[...] = x_ref[...] + _nn(mg, wo_ref[...])

    row = lambda w: pl.BlockSpec((tm, w), lambda i: (i, 0))
    full = lambda a, b: pl.BlockSpec((a, b), lambda i: (0, 0))
    tsp = pl.BlockSpec((D, tm), lambda i: (0, i))
    bf, bft = jax.ShapeDtypeStruct((T, D), BF16), jax.ShapeDtypeStruct((D, T), BF16)
    return pl.pallas_call(
        kern, out_shape=(jax.ShapeDtypeStruct((T, D), F32), bf, bf, bft, bft), grid=(T // tm,),
        in_specs=[row(D), row(D), row(CONV_W), pl.BlockSpec((16, CONV_W), lambda i: (jnp.maximum(i * (tm // 16) - 1, 0), 0)),
                  row(GATE_W), full(3, D), full(1, D), full(1, GATE_W)] + [_resident((D, D))] * 3,
        out_specs=(row(D), row(D), row(D), tsp, tsp),
        name="mix_fwd", compiler_params=_params("parallel"))(x, att, pconv, pconv, pgate, conv_w, conv_b, b_gates, wao, wco, wo)


def _mlp_fwd(x1, g, wup, wdn, comm=None):
    T, D = x1.shape
    tm = min(1024, T)
    nj = D_FF // D

    def kern(x_ref, g_ref, wup_ref, wdn_ref, x2_ref, a_ref, h_scr, acc_scr):
        j = pl.program_id(1)

        @pl.when(j == 0)
        def _():
            xf = x_ref[...]
            _, xh = _rms_stats(xf)
            h_scr[...] = (xh * g_ref[...]).astype(BF16)
            acc_scr[...] = xf

        a = _nn(h_scr[...], wup_ref[j])
        a_ref[...] = a.astype(BF16)
        u = jnp.square(jnp.maximum(a, 0.0)).astype(BF16)
        acc_scr[...] += _nn(u, wdn_ref[j])

        @pl.when(j == nj - 1)
        def _():
            x2_ref[...] = acc_scr[...]

    return _hosted_call(
        comm, kern, out_shape=(jax.ShapeDtypeStruct((T, D), F32), jax.ShapeDtypeStruct((T, D_FF), BF16)), grid=(T // tm, nj),
        in_specs=[pl.BlockSpec((tm, D), lambda i, j: (i, 0)), pl.BlockSpec((1, D), lambda i, j: (0, 0)),
                  _resident((nj, D, D)), _resident((nj, D, D))],
        out_specs=(pl.BlockSpec((tm, D), lambda i, j: (i, 0)), pl.BlockSpec((tm, D), lambda i, j: (i, j))),
        scratch_shapes=[pltpu.VMEM((tm, D), BF16), pltpu.VMEM((tm, D), F32)],
        name="mlp_fwd", sem=("parallel", "arbitrary"), args=(x1, g, wup, wdn))


def _loss_bwd(x, g, tgt):
    T, D = x.shape
    tm = min(512, T)

    def kern(x_ref, g_ref, t_ref, st_ref, dx_ref):
        i = pl.program_id(0)

        @pl.when(i == 0)
        def _():
            st_ref[...] = jnp.zeros_like(st_ref)

        gg = g_ref[...]
        r, xh = _rms_stats(x_ref[...])
        e = xh * gg - t_ref[...]
        part = 0.5 * jnp.sum(jnp.mean(e * e, axis=-1, keepdims=True), axis=0, keepdims=True)
        dx, dg = _rms_bwd(e * (1.0 / D), xh, r, gg)
        dx_ref[...] = dx
        st_ref[0:1, :] += dg
        st_ref[1:2, 0:1] += part

    return pl.pallas_call(
        kern, out_shape=(jax.ShapeDtypeStruct((8, D), F32), jax.ShapeDtypeStruct((T, D), F32)), grid=(T // tm,),
        in_specs=[pl.BlockSpec((tm, D), lambda i: (i, 0)), pl.BlockSpec((1, D), lambda i: (0, 0)),
                  pl.BlockSpec((tm, D), lambda i: (i, 0))],
        out_specs=(pl.BlockSpec((8, D), lambda i: (0, 0)), pl.BlockSpec((tm, D), lambda i: (i, 0))),
        name="loss_bwd", compiler_params=_params("arbitrary"))(x, g, tgt)


def _mlp_bwd(dx2, x1, a, g, wup, wdn, comm=None):
    T, D = x1.shape
    tm = min(512, T)
    nj = D_FF // D
    tk = _dw_chunk(T)
    per = tk // tm

    def kern(dx2_ref, x1_ref, a_ref, g_ref, wup_ref, wdn_ref, dx1_ref, da_ref, ut_ref, h2t_ref, dyb_ref, dg_ref, acc_scr):
        i, j = pl.program_id(0), pl.program_id(1)

        @pl.when((i == 0) & (j == 0))
        def _():
            dg_ref[...] = jnp.zeros_like(dg_ref)

        @pl.when(j == 0)
        def _():
            dyb_ref[...] = dx2_ref[...].astype(BF16)
            acc_scr[...] = jnp.zeros_like(acc_scr)

        du = _nt(dyb_ref[...], wdn_ref[j])
        relu = jnp.maximum(a_ref[...].astype(F32), 0.0)
        da = (du * (2.0 * relu)).astype(BF16)
        da_ref[...] = da
        ut_ref[...] = jnp.square(relu).astype(BF16).T
        acc_scr[...] += _nt(da, wup_ref[j])

        @pl.when(j == nj - 1)
        def _():
            gg = g_ref[...]
            r, xh = _rms_stats(x1_ref[...])
            h2t_ref[...] = (xh * gg).astype(BF16).T
            dx, dg = _rms_bwd(acc_scr[...], xh, r, gg)
            dx1_ref[...] = dx2_ref[...] + dx
            dg_ref[...] += dg

    return _hosted_call(
        comm, kern,
        out_shape=(jax.ShapeDtypeStruct((T, D), F32), jax.ShapeDtypeStruct((T, D_FF), BF16),
                   jax.ShapeDtypeStruct((D_FF, T), BF16), jax.ShapeDtypeStruct((T // tk, D, tk), BF16),
                   jax.ShapeDtypeStruct((T, D), BF16), jax.ShapeDtypeStruct((1, D), F32)),
        grid=(T // tm, nj),
        in_specs=[pl.BlockSpec((tm, D), lambda i, j: (i, 0)), pl.BlockSpec((tm, D), lambda i, j: (i, 0)),
                  pl.BlockSpec((tm, D), lambda i, j: (i, j)), pl.BlockSpec((1, D), lambda i, j: (0, 0)),
                  _resident((nj, D, D)), _resident((nj, D, D))],
        out_specs=(pl.BlockSpec((tm, D), lambda i, j: (i, 0)), pl.BlockSpec((tm, D), lambda i, j: (i, j)),
                   pl.BlockSpec((D, tm), lambda i, j: (j, i)), pl.BlockSpec((None, D, tm), lambda i, j: (i // per, 0, i % per)),
                   pl.BlockSpec((tm, D), lambda i, j: (i, 0)), pl.BlockSpec((1, D), lambda i, j: (0, 0))),
        scratch_shapes=[pltpu.VMEM((tm, D), F32)],
        name="mlp_bwd", sem=("arbitrary", "arbitrary"), args=(dx2, x1, a, g, wup, wdn))


def _mix_bwd(dx1, ya, yc, pgate, pconv, conv_w, conv_b, b_gates, wao, wco, wo, seq, comm=None):
    T, D = dx1.shape
    tm = min(512, seq)
    per_seq = seq // tm
    n = T // tm

    def kern(dx_ref, ya_ref, yc_ref, pg_ref, pc_ref, halo_ref, cw_ref, cb_ref, bg_ref, wao_ref, wco_ref, wo_ref,
             datt_ref, dya_ref, dyc_ref, dgt_ref, dcn_ref, dbg_ref, dwb_ref, next_scr):
        i = pl.program_id(0)
        r = n - 1 - i

        @pl.when(i == 0)
        def _():
            dbg_ref[...] = jnp.zeros_like(dbg_ref)
            dwb_ref[...] = jnp.zeros_like(dwb_ref)
            next_scr[...] = jnp.zeros_like(next_scr)

        dm = _nt(dx_ref[...].astype(BF16), wo_ref[...])
        sa = jax.nn.sigmoid(pg_ref[:, 0:D].astype(F32) + bg_ref[:, 0:D])
        sc = jax.nn.sigmoid(pg_ref[:, D:2 * D].astype(F32) + bg_ref[:, D:2 * D])
        dya = (dm * sa).astype(BF16)
        dyc = (dm * sc).astype(BF16)
        dga = dm * ya_ref[...].astype(F32) * (sa * (1.0 - sa))
        dgc = dm * yc_ref[...].astype(F32) * (sc * (1.0 - sc))
        dya_ref[...] = dya
        dyc_ref[...] = dyc
        dgt_ref[:, 0:D] = dga.astype(BF16)
        dgt_ref[:, D:2 * D] = dgc.astype(BF16)
        dbg_ref[:, 0:D] += jnp.sum(dga, axis=0, keepdims=True)
        dbg_ref[:, D:2 * D] += jnp.sum(dgc, axis=0, keepdims=True)
        datt_ref[...] = _nt(dya, wao_ref[...]).astype(BF16)
        dcv = _nt(dyc, wco_ref[...])

        first_i = ((r % per_seq) == 0).astype(jnp.int32)
        keep_next = 1.0 - (((r + 1) % per_seq) == 0).astype(F32)
        cb = pc_ref[:, 0:D].astype(F32)
        cc = pc_ref[:, D:2 * D].astype(F32)
        cu = pc_ref[:, 2 * D:3 * D].astype(F32)
        u = cc * cu
        halo_u = halo_ref[:, D:2 * D].astype(F32) * halo_ref[:, 2 * D:3 * D].astype(F32)
        s1, s2 = _conv_taps(u, halo_u, first_i)
        w0, w1, w2 = cw_ref[0:1, :], cw_ref[1:2, :], cw_ref[2:3, :]
        z = w0 * s2 + w1 * s1 + w2 * u
        dz = dcv * cb
        n1 = next_scr[0:1, :] * keep_next
        n2 = next_scr[1:2, :] * keep_next
        next_scr[...] = dz[0:8, :]
        rowid = lax.broadcasted_iota(jnp.int32, dz.shape, 0)
        u1 = jnp.where(rowid == tm - 1, n1, pltpu.roll(dz, tm - 1, 0))
        u2 = jnp.where(rowid == tm - 1, n2, jnp.where(rowid == tm - 2, n1, pltpu.roll(dz, tm - 2, 0)))
        du = w2 * dz + w1 * u1 + w0 * u2
        dcn_ref[:, 0:D] = (dcv * (z + cb_ref[...])).astype(BF16)
        dcn_ref[:, D:2 * D] = (du * cu).astype(BF16)
        dcn_ref[:, 2 * D:3 * D] = (du * cc).astype(BF16)
        dwb_ref[0:1, :] += jnp.sum(dz * s2, axis=0, keepdims=True)
        dwb_ref[1:2, :] += jnp.sum(dz * s1, axis=0, keepdims=True)
        dwb_ref[2:3, :] += jnp.sum(dz * u, axis=0, keepdims=True)
        dwb_ref[3:4, :] += jnp.sum(dz, axis=0, keepdims=True)

    row = lambda w: pl.BlockSpec((tm, w), lambda i: (n - 1 - i, 0))
    full = lambda a, b: pl.BlockSpec((a, b), lambda i: (0, 0))
    halo = pl.BlockSpec((16, CONV_W), lambda i: (jnp.maximum((n - 1 - i) * (tm // 16) - 1, 0), 0))
    bf = jax.ShapeDtypeStruct((T, D), BF16)
    return _hosted_call(
        comm, kern,
        out_shape=(bf, bf, bf, jax.ShapeDtypeStruct((T, GATE_W), BF16), jax.ShapeDtypeStruct((T, CONV_W), BF16),
                   jax.ShapeDtypeStruct((1, GATE_W), F32), jax.ShapeDtypeStruct((8, D), F32)),
        grid=(n,),
        in_specs=[row(D), row(D), row(D), row(GATE_W), row(CONV_W), halo, full(3, D), full(1, D), full(1, GATE_W)]
        + [_resident((D, D))] * 3,
        out_specs=(row(D), row(D), row(D), row(GATE_W), row(CONV_W), full(1, GATE_W), full(8, D)),
        scratch_shapes=[pltpu.VMEM((8, D), F32)],
        name="mix_bwd", sem=("arbitrary",), args=(dx1, ya, yc, pgate, pconv, pconv, conv_w, conv_b, b_gates, wao, wco, wo))


def _attn_bwd(pqkv, datt, lse, sinks, bias, seq, comm=None):
    T = pqkv.shape[0]
    nblk = seq // WINDOW
    nseq = T // seq
    KVW = KV_W

    def kern(sink_ref, cur_ref, prev_ref, do_ref, lse_ref, bias_ref, dq_ref, dkv_ref, ds_ref, kc_scr, vc_scr, dqt_scr):
        b, st = pl.program_id(0), pl.program_id(1)

        @pl.when((b == 0) & (st == 0))
        def _():
            ds_ref[...] = jnp.zeros_like(ds_ref)

        @pl.when(st == 0)
        def _():
            kc_scr[...] = jnp.zeros_like(kc_scr)
            vc_scr[...] = jnp.zeros_like(vc_scr)

        @pl.when(st < nblk)
        def _():
            first_i = (st == 0).astype(jnp.int32)
            groups = range(N_KV_HEADS)
            bands = [_kv_band(cur_ref, prev_ref, hk) for hk in groups]
            qs = [_stack_heads(cur_ref, hk) for hk in groups]
            dos = [_stack_heads(do_ref, hk) for hk in groups]
            sts = [_nt(bands[hk][0], qs[hk] * ATTN_SCALE) + bias_ref[first_i, hk] for hk in groups]
            dps = [_nt(bands[hk][1], dos[hk]) for hk in groups]
            pbs, dsss = [], []
            for hk in groups:
                heads = [GQA_GROUP * hk + g for g in range(GQA_GROUP)]
                sink = _lane_row([sink_ref[h] for h in heads])
                lse_g = jnp.concatenate([lse_ref[h:h + 1, :] for h in heads], axis=1)
                p = jnp.exp(sts[hk] - lse_g)
                d_row = jnp.sum(p * dps[hk], axis=0, keepdims=True)
                dsss.append((p * (dps[hk] - d_row) * ATTN_SCALE).astype(BF16))
                pbs.append(p.astype(BF16))
                psd = jnp.exp(sink - lse_g) * d_row
                for g, h in enumerate(heads):
                    ds_ref[0:1, h:h + 1] -= jnp.sum(psd[:, WINDOW * g:WINDOW * (g + 1)], axis=1, keepdims=True)
            for hk in groups:
                dqt = _tn(bands[hk][0], dsss[hk])
                dk_b = _nn(dsss[hk], qs[hk])
                dv_b = _nn(pbs[hk], dos[hk])
                for g in range(GQA_GROUP):
                    h = GQA_GROUP * hk + g
                    dqt_scr[HEAD_DIM * h:HEAD_DIM * (h + 1), :] = dqt[:, WINDOW * g:WINDOW * (g + 1)].astype(BF16)
                ksl = slice(HEAD_DIM * hk, HEAD_DIM * (hk + 1))
                vsl = slice(KVW + HEAD_DIM * hk, KVW + HEAD_DIM * (hk + 1))
                dkv_ref[:, ksl] = (kc_scr[:, ksl] + dk_b[0:WINDOW]).astype(BF16)
                dkv_ref[:, vsl] = (vc_scr[:, ksl] + dv_b[0:WINDOW]).astype(BF16)
                kc_scr[:, ksl] = dk_b[WINDOW:2 * WINDOW]
                vc_scr[:, ksl] = dv_b[WINDOW:2 * WINDOW]
            dq_ref[...] = dqt_scr[...].T

        @pl.when(st == nblk)
        def _():
            dkv_ref[:, 0:KVW] = kc_scr[...].astype(BF16)
            dkv_ref[:, KVW:2 * KVW] = vc_scr[...].astype(BF16)

    cur_map = lambda b, s: (b * nblk + jnp.minimum(s, nblk - 1), 0)
    prev_row = lambda b, s: b * nblk + jnp.clip(s - 1, 0, nblk - 1)
    return _hosted_call(
        comm, kern,
        out_shape=(jax.ShapeDtypeStruct((T, D_MODEL), BF16), jax.ShapeDtypeStruct((T, 2 * KVW), BF16),
                   jax.ShapeDtypeStruct((8, 128), F32)),
        grid=(nseq, nblk + 1),
        in_specs=[pl.BlockSpec(memory_space=pltpu.SMEM),
                  pl.BlockSpec((WINDOW, QKV_W), cur_map),
                  pl.BlockSpec((WINDOW, 2 * KVW), lambda b, s: (prev_row(b, s), 2)),
                  pl.BlockSpec((WINDOW, D_MODEL), cur_map),
                  pl.BlockSpec((N_Q_HEADS, WINDOW), lambda b, s: (0, b * nblk + jnp.minimum(s, nblk - 1))),
                  _resident(bias.shape)],
        out_specs=(pl.BlockSpec((WINDOW, D_MODEL), cur_map),
                   pl.BlockSpec((WINDOW, 2 * KVW), lambda b, s: (prev_row(b, s), 0)),
                   pl.BlockSpec((8, 128), lambda b, s: (0, 0))),
        scratch_shapes=[pltpu.VMEM((WINDOW, KVW), F32), pltpu.VMEM((WINDOW, KVW), F32), pltpu.VMEM((D_MODEL, WINDOW), BF16)],
        name="attn_bwd", sem=("arbitrary", "arbitrary"), args=(sinks, pqkv, pqkv, datt, lse, bias))


def _piece_tiles(pieces):
    out, start = [], 0
    for arr, width in pieces:
        out.append((arr, start, width // COL_TILE))
        start += width // COL_TILE
    return out, start


def _inproj_bwd(pieces, w_in, x, dx_in, g, comm=None):
    T, D = x.shape
    tm = min(512, T)

    def kern(*refs):
        p_refs = refs[:len(pieces)]
        w_ref, x_ref, dxin_ref, g_ref, dx_ref, dg_ref = refs[len(pieces):]

        @pl.when(pl.program_id(0) == 0)
        def _():
            dg_ref[...] = jnp.zeros_like(dg_ref)

        dh, off = None, 0
        for p_ref, (_, width) in zip(p_refs, pieces):
            part = _nt(p_ref[...], w_ref[:, off:off + width])
            dh = part if dh is None else dh + part
            off += width
        gg = g_ref[...]
        r, xh = _rms_stats(x_ref[...])
        dx, dg = _rms_bwd(dh, xh, r, gg)
        dx_ref[...] = dxin_ref[...] + dx
        dg_ref[...] += dg

    row = lambda wd: pl.BlockSpec((tm, wd), lambda i: (i, 0))
    return _hosted_call(
        comm, kern, out_shape=(jax.ShapeDtypeStruct((T, D), F32), jax.ShapeDtypeStruct((1, D), F32)), grid=(T // tm,),
        in_specs=[row(wd) for _, wd in pieces] + [_resident((D, IN_COLS)), row(D), row(D), pl.BlockSpec((1, D), lambda i: (0, 0))],
        out_specs=(row(D), pl.BlockSpec((1, D), lambda i: (0, 0))),
        name="inproj_bwd", sem=("arbitrary",), args=(*[a for a, _ in pieces], w_in, x, dx_in, g))


def _dw_pieces(lhs_t, pieces):
    nt, K, tk = lhs_t.shape
    tiles, nj = _piece_tiles(pieces)

    def kern(*refs):
        lhs_ref = refs[0]
        p_refs = refs[1:1 + len(tiles)]
        o_ref, ob_ref = refs[1 + len(tiles):]
        j, t = pl.program_id(0), pl.program_id(1)

        @pl.when(t == 0)
        def _():
            o_ref[...] = jnp.zeros_like(o_ref)

        for p_ref, (_, start, n) in zip(p_refs, tiles):
            @pl.when((j >= start) & (j < start + n))
            def _(p_ref=p_ref):
                o_ref[...] += _nn(lhs_ref[t], p_ref[...])

        @pl.when(t == nt - 1)
        def _():
            ob_ref[...] = o_ref[...].astype(BF16)

    def p_map(start, n):
        return lambda j, t: (jnp.where((j >= start) & (j < start + n), t, 0), jnp.clip(j - start, 0, n - 1))

    N = nj * COL_TILE
    return pl.pallas_call(
        kern, out_shape=(jax.ShapeDtypeStruct((K, N), F32), jax.ShapeDtypeStruct((K, N), BF16)), grid=(nj, nt),
        in_specs=[_resident((nt, K, tk))] + [pl.BlockSpec((tk, COL_TILE), p_map(s, n)) for _, s, n in tiles],
        out_specs=(pl.BlockSpec((K, COL_TILE), lambda j, t: (0, j)), pl.BlockSpec((K, COL_TILE), lambda j, t: (0, j))),
        name="dw_pieces", compiler_params=_params("arbitrary", "arbitrary"))(lhs_t, *[a for a, _, _ in tiles])


def _dw(lhs_t, rhs):
    lhs_res, rhs_res = lhs_t.ndim == 3, rhs.ndim == 3
    W = D_MODEL
    if lhs_res:
        nt, K, tk = lhs_t.shape
    else:
        K, tk = lhs_t.shape[0], _dw_chunk(lhs_t.shape[1])
        nt = lhs_t.shape[1] // tk
    N = rhs.shape[-1]

    def kern(lhs_ref, rhs_ref, o_ref, ob_ref):
        t = pl.program_id(2)

        @pl.when(t == 0)
        def _():
            o_ref[...] = jnp.zeros_like(o_ref)

        a = lhs_ref[t] if lhs_res else lhs_ref[...]
        b = rhs_ref[t] if rhs_res else rhs_ref[...]
        o_ref[...] += _nn(a, b.astype(BF16))

        @pl.when(t == nt - 1)
        def _():
            ob_ref[...] = o_ref[...].astype(BF16)

    omap = lambda i, j, t: (i, j)
    lspec = _resident((nt, W, tk)) if lhs_res else pl.BlockSpec((W, tk), lambda i, j, t: (i, t))
    rspec = _resident((nt, tk, W)) if rhs_res else pl.BlockSpec((tk, W), lambda i, j, t: (t, j))
    return pl.pallas_call(
        kern, out_shape=(jax.ShapeDtypeStruct((K, N), F32), jax.ShapeDtypeStruct((K, N), BF16)), grid=(K // W, N // W, nt),
        in_specs=[lspec, rspec], out_specs=(pl.BlockSpec((W, W), omap), pl.BlockSpec((W, W), omap)),
        name="dw", compiler_params=_params("arbitrary", "arbitrary", "arbitrary"))(lhs_t, rhs)


BIG = (("w_in", D_MODEL, IN_COLS, "col"), ("w_attn_out", D_MODEL, D_MODEL, "row"), ("w_conv_out", D_MODEL, D_MODEL, "row"),
       ("w_o", D_MODEL, D_MODEL, "row"), ("w_up", D_MODEL, D_FF, "col"), ("w_down", D_FF, D_MODEL, "row"))


def _shard_dims(rows, cols, kind):
    return (rows, cols // N_CHIP) if kind in ("col", "chip") else (rows // N_CHIP, cols)


def _window(ref, rows, cols, kind, chip, half):
    sr, sc = _shard_dims(rows, cols, kind)
    hr = sr // 2
    if kind == "col":
        return ref.at[pl.ds(half * hr, hr), pl.ds(chip * sc, sc)]
    if kind == "chip":
        return ref.at[chip, pl.ds(half * hr, hr), :]
    return ref.at[pl.ds(chip * sr + half * hr, hr), :]


def _mesh_pos():
    x, y, c = lax.axis_index("x"), lax.axis_index("y"), lax.axis_index("c")
    return x, y, c, 2 * x + y


_REL_BITS = (2, 1, 3)


def _rel_dev(x, y, c, r):
    return ((1 - x, y, c), (x, 1 - y, c), (1 - x, 1 - y, c))[r]


def _for_my_chip(j, fn):
    for js in range(N_CHIP):
        pl.when(j == js)(functools.partial(fn, js))


def _cast_into_full(j_arr, shard, l, rows, cols, kind):
    sr, sc = _shard_dims(rows, cols, kind)
    tr = min(256, sr)

    def kern(j_ref, s_ref, o_ref):
        o_ref[...] = s_ref[...].astype(BF16)

    shape, block = (rows, cols), (tr, sc)
    if kind == "col":
        omap = lambda i, j_ref: (i, j_ref[0])
    elif kind == "chip":
        shape, block = (N_CHIP, rows, sc), (None, tr, sc)
        omap = lambda i, j_ref: (j_ref[0], i, 0)
    else:
        omap = lambda i, j_ref: (j_ref[0] * (sr // tr) + i, 0)
    gs = pltpu.PrefetchScalarGridSpec(
        num_scalar_prefetch=1, grid=(sr // tr,),
        in_specs=[pl.BlockSpec((None, tr, sc), lambda i, j_ref: (l, i, 0))], out_specs=pl.BlockSpec(block, omap))
    return pl.pallas_call(kern, out_shape=jax.ShapeDtypeStruct(shape, BF16), grid_spec=gs, name="cast_into_full",
                          compiler_params=_params("arbitrary"))(j_arr, shard)


def _gather_comm(fulls, cw=None):
    n_big = len(fulls)
    n_piece = n_big + (0 if cw is None else 1)

    def pieces(in_refs, o_refs, js, c):
        def piece(p, chip, half):
            if p == n_big:
                return o_refs[p].at[half, chip]
            _, rows, cols, kind = fulls[p]
            return _window(o_refs[p], rows, cols, kind, chip, half)

        def mine(p):
            return in_refs[p].at[c] if p == n_big else piece(p, js, c)

        return piece, mine

    def local_copies(in_refs, piece, js, loc_sem):
        if cw is None:
            return []
        return [pltpu.make_async_copy(in_refs[n_big].at[half], piece(n_big, js, half), loc_sem.at[half]) for half in range(2)]

    def ici_copy(piece, mine, js, x, y, c, r, p, send_sem, recv_sem):
        return pltpu.make_async_remote_copy(mine(p), piece(p, js, c), send_sem.at[r * n_piece + p], recv_sem.at[r * n_piece + p],
                                            _rel_dev(x, y, c, r), MESH)

    def start(in_refs, o_refs, sems):
        send_sem, recv_sem, _, _, loc_sem = sems
        x, y, c, j = _mesh_pos()

        def run(js):
            piece, mine = pieces(in_refs, o_refs, js, c)
            for cp in local_copies(in_refs, piece, js, loc_sem):
                cp.start()
            for r in range(3):
                for p in range(n_piece):
                    ici_copy(piece, mine, js, x, y, c, r, p, send_sem, recv_sem).start()

        _for_my_chip(j, run)

    def finish(in_refs, o_refs, sems):
        send_sem, recv_sem, fsend_sem, frecv_sem, loc_sem = sems
        x, y, c, j = _mesh_pos()

        def run(js):
            piece, mine = pieces(in_refs, o_refs, js, c)
            fwds = []
            for r in range(3):
                ks = js ^ _REL_BITS[r]
                for p in range(n_piece):
                    got = piece(p, ks, c)
                    pltpu.make_async_remote_copy(got, got, send_sem.at[r * n_piece + p], recv_sem.at[r * n_piece + p],
                                                 _rel_dev(x, y, c, r), MESH).wait_recv()
                    cp = pltpu.make_async_remote_copy(got, got, fsend_sem.at[r * n_piece + p], frecv_sem.at[r * n_piece + p],
                                                      (x, y, 1 - c), MESH)
                    cp.start()
                    fwds.append(cp)
            for r in range(3):
                ks = js ^ _REL_BITS[r]
                for p in range(n_piece):
                    got = piece(p, ks, 1 - c)
                    pltpu.make_async_remote_copy(got, got, fsend_sem.at[r * n_piece + p], frecv_sem.at[r * n_piece + p],
                                                 (x, y, 1 - c), MESH).wait_recv()
            for r in range(3):
                for p in range(n_piece):
                    ici_copy(piece, mine, js, x, y, c, r, p, send_sem, recv_sem).wait_send()
            for cp in fwds:
                cp.wait_send()
            for cp in local_copies(in_refs, piece, js, loc_sem):
                cp.wait()

        _for_my_chip(j, run)

    out_shape = [jax.ShapeDtypeStruct(a.shape, BF16) for a, _, _, _ in fulls]
    ins = [a for a, _, _, _ in fulls]
    if cw is not None:
        out_shape.append(jax.ShapeDtypeStruct((DEPTH, N_CHIP, 3, D_MODEL // N_CHIP), F32))
        ins.append(cw)
    scratch = [pltpu.SemaphoreType.DMA((3 * n_piece,))] * 4 + [pltpu.SemaphoreType.DMA((2,))]
    return _Hosted(ins, out_shape, {p: p for p in range(n_big)}, scratch, start, finish)


def _compose(*comms):
    comms = [cm for cm in comms if cm is not None]
    if len(comms) <= 1:
        return comms[0] if comms else None
    ins, outs, aliases, scratch, cuts = [], [], {}, [], []
    for cm in comms:
        cuts.append((len(ins), len(outs), len(scratch)))
        aliases.update({len(ins) + i: len(outs) + o for i, o in cm.aliases.items()})
        ins, outs, scratch = ins + cm.inputs, outs + cm.out_shape, scratch + cm.scratch

    def parts(a, b, s):
        for cm, (i0, o0, s0) in zip(comms, cuts):
            yield cm, a[i0:i0 + len(cm.inputs)], b[o0:o0 + len(cm.out_shape)], s[s0:s0 + len(cm.scratch)]

    def start(a, b, s):
        for cm, pa, pb, ps in parts(a, b, s):
            cm.start(pa, pb, ps)

    def finish(a, b, s):
        for cm, pa, pb, ps in parts(a, b, s):
            cm.finish(pa, pb, ps)

    return _Hosted(ins, outs, aliases, scratch, start, finish)


def _run_comm(comm, name):
    n_in, n_out = len(comm.inputs), len(comm.out_shape)

    def body(*refs):
        comm.start(refs[:n_in], refs[n_in:n_in + n_out], refs[n_in + n_out:])
        comm.finish(refs[:n_in], refs[n_in:n_in + n_out], refs[n_in + n_out:])

    return pl.pallas_call(body, out_shape=comm.out_shape, in_specs=[ANY] * n_in, out_specs=[ANY] * n_out,
                          input_output_aliases=comm.aliases, scratch_shapes=comm.scratch, name=name)(*comm.inputs)


def _sibling_exchange_comm(gb):
    n = len(gb)

    def copies(g_refs, o_refs, sems, c):
        send_sem, recv_sem = sems
        x, y, _, _ = _mesh_pos()
        return [pltpu.make_async_remote_copy(_window(g_refs[t], rows, cols, kind, chip, 1 - c),
                                             _window(o_refs[t], rows, cols, kind, chip, 1 - c),
                                             send_sem.at[N_CHIP * t + chip], recv_sem.at[N_CHIP * t + chip], (x, y, 1 - c), MESH)
                for t, (_, rows, cols, kind) in enumerate(gb) for chip in range(N_CHIP)]

    def start(g_refs, o_refs, sems):
        for cp in copies(g_refs, o_refs, sems, lax.axis_index("c")):
            cp.start()

    def finish(g_refs, o_refs, sems):
        c = lax.axis_index("c")
        for cp in copies(g_refs, o_refs, sems, 1 - c):
            cp.wait_recv()
        for cp in copies(g_refs, o_refs, sems, c):
            cp.wait_send()

    return _Hosted([a for a, _, _, _ in gb], [jax.ShapeDtypeStruct(a.shape, BF16) for a, _, _, _ in gb], {},
                   [pltpu.SemaphoreType.DMA((N_CHIP * n,))] * 2, start, finish)


def _half_add(jc_arr, g, sib, rows, cols, kind):
    sr, sc = _shard_dims(rows, cols, kind)
    hr = sr // 2

    def kern(jc_ref, g_ref, s_ref, ob_ref, of_ref):
        v = g_ref[...] + s_ref[...].astype(F32)
        ob_ref[...] = v.astype(BF16)

        @pl.when(pl.program_id(0) == jc_ref[0])
        def _():
            of_ref[...] = v

    if kind == "col":
        imap = lambda j, jc_ref: (jc_ref[1], j)
    else:
        imap = lambda j, jc_ref: (2 * j + jc_ref[1], 0)
    gs = pltpu.PrefetchScalarGridSpec(
        num_scalar_prefetch=1, grid=(N_CHIP,),
        in_specs=[pl.BlockSpec((hr, sc), imap), pl.BlockSpec((hr, sc), imap)],
        out_specs=[pl.BlockSpec((None, hr, sc), lambda j, jc_ref: (j, 0, 0)), pl.BlockSpec((hr, sc), lambda j, jc_ref: (0, 0))])
    return pl.pallas_call(
        kern, out_shape=(jax.ShapeDtypeStruct((N_CHIP, hr, sc), BF16), jax.ShapeDtypeStruct((hr, sc), F32)),
        grid_spec=gs, name="grad_half_add", compiler_params=_params("arbitrary"))(jc_arr, g, sib)


def _chip_exchange_comm(sbs):
    n = len(sbs)

    def copies(s_refs, o_refs, sems):
        send_sem, recv_sem = sems
        x, y, c, j = _mesh_pos()
        return [pltpu.make_async_remote_copy(s_refs[t].at[j ^ _REL_BITS[r]], o_refs[t].at[r], send_sem.at[n * r + t],
                                             recv_sem.at[n * r + t], _rel_dev(x, y, c, r), MESH)
                for r in range(3) for t in range(n)]

    def start(s_refs, o_refs, sems):
        for cp in copies(s_refs, o_refs, sems):
            cp.start()

    def finish(s_refs, o_refs, sems):
        for cp in copies(s_refs, o_refs, sems):
            cp.wait()

    return _Hosted(sbs, [jax.ShapeDtypeStruct((3,) + a.shape[1:], BF16) for a in sbs], {},
                   [pltpu.SemaphoreType.DMA((3 * n,))] * 2, start, finish)


def _owner_sum(jc_arr, sf, rb, l, into=None):
    hr, sc = sf.shape

    def kern(jc_ref, s_ref, r0_ref, r1_ref, r2_ref, *rest):
        o_ref = rest[-1]
        o_ref[...] = ((s_ref[...] + r0_ref[...].astype(F32)) + r1_ref[...].astype(F32)) + r2_ref[...].astype(F32)

    in_specs = [pl.BlockSpec((hr, sc), lambda i, jc_ref: (0, 0))]
    in_specs += [pl.BlockSpec((None, hr, sc), lambda i, jc_ref, r=r: (r, 0, 0)) for r in range(3)]
    args = [jc_arr, sf, rb, rb, rb]
    aliases = {}
    if into is not None:
        in_specs.append(ANY)
        args.append(into)
        aliases = {len(args) - 1: 0}
    gs = pltpu.PrefetchScalarGridSpec(
        num_scalar_prefetch=1, grid=(1,), in_specs=in_specs,
        out_specs=pl.BlockSpec((None, hr, sc), lambda i, jc_ref: (l, jc_ref[1], 0)))
    return pl.pallas_call(kern, out_shape=jax.ShapeDtypeStruct((DEPTH, 2 * hr, sc), F32), grid_spec=gs,
                          input_output_aliases=aliases, name="grad_owner_sum", compiler_params=_params("arbitrary"))(*args)


def _sibling_assemble_comm(grads, layers):
    n = len(grads)
    todo = [(q, l) for q in range(n) for l in layers[q]]

    def copies(o_refs, sems, half):
        send_sem, recv_sem = sems
        x, y, c, _ = _mesh_pos()
        out = []
        for k, (q, l) in enumerate(todo):
            hr = grads[q].shape[1] // 2
            w = o_refs[q].at[l, pl.ds(half * hr, hr), :]
            out.append(pltpu.make_async_remote_copy(w, w, send_sem.at[k], recv_sem.at[k], (x, y, 1 - c), MESH))
        return out

    def start(_, o_refs, sems):
        for cp in copies(o_refs, sems, lax.axis_index("c")):
            cp.start()

    def finish(_, o_refs, sems):
        c = lax.axis_index("c")
        for cp in copies(o_refs, sems, 1 - c):
            cp.wait_recv()
        for cp in copies(o_refs, sems, c):
            cp.wait_send()

    return _Hosted(grads, [jax.ShapeDtypeStruct(g.shape, F32) for g in grads], {q: q for q in range(n)},
                   [pltpu.SemaphoreType.DMA((len(todo),))] * 2, start, finish)


def _adamw_math(w, g, m, v):
    m = ADAM_B1 * m + (1.0 - ADAM_B1) * g
    v = ADAM_B2 * v + (1.0 - ADAM_B2) * jnp.square(g)
    m_hat = m / (1.0 - ADAM_B1 ** ADAM_STEP)
    v_hat = v / (1.0 - ADAM_B2 ** ADAM_STEP)
    delta = -ADAM_LR * (m_hat / (jnp.sqrt(v_hat) + ADAM_EPS) + ADAM_WD * w)
    return delta, m, v


def _adamw(w, g, m, v):
    shape = w.shape
    C = shape[-1]
    R = int(np.prod(shape[:-1]))
    tr = min(256, R)
    args = [a.reshape(R, C) for a in (w, g, m, v)]

    def kern(w_ref, g_ref, m_ref, v_ref, go_ref, d_ref, nm_ref, nv_ref):
        g_val = g_ref[...]
        d, nm, nv = _adamw_math(w_ref[...], g_val, m_ref[...], v_ref[...])
        go_ref[...] = g_val
        d_ref[...] = d
        nm_ref[...] = nm
        nv_ref[...] = nv

    spec = pl.BlockSpec((tr, C), lambda i: (i, 0))
    outs = pl.pallas_call(
        kern, out_shape=[jax.ShapeDtypeStruct((R, C), F32)] * 4, grid=(R // tr,), in_specs=[spec] * 4, out_specs=[spec] * 4,
        name="adamw", compiler_params=_params("parallel"))(*args)
    return [o.reshape(shape) for o in outs]


_ROW_G_MIX, _ROW_B_GATES, _ROW_SINKS, _ROW_CONV, _ROW_G_MLP, _ROW_G_FINAL, _ROW_LOSS = 0, 2, 6, 8, 16, 18, 19


def _small_step(parts, params, moms, vels):
    names = ["g_mix", "b_gates", "sinks", "conv_w", "conv_b", "g_mlp", "g_final"]
    D = D_MODEL
    QW = D // N_CHIP
    n_dev = 8

    def body(*refs):
        it = iter(refs)
        dgmix = [next(it) for _ in range(DEPTH)]
        dbg = [next(it) for _ in range(DEPTH)]
        dsk = [next(it) for _ in range(DEPTH)]
        dwb = [next(it) for _ in range(DEPTH)]
        dgmlp = [next(it) for _ in range(DEPTH)]
        lst = next(it)
        p_refs = {n: next(it) for n in names}
        m_refs = {n: next(it) for n in names}
        v_refs = {n: next(it) for n in names}
        loss_ref = next(it)
        outs = {n: [next(it) for _ in range(4)] for n in names}
        pack_ref, all_ref, send_sem, recv_sem = next(it), next(it), next(it), next(it)

        x, y, c, j = _mesh_pos()
        me = 4 * x + 2 * y + c
        pack_ref[...] = jnp.zeros_like(pack_ref)
        for l in range(DEPTH):
            pack_ref[_ROW_G_MIX + l:_ROW_G_MIX + l + 1, :] = dgmix[l][...]
            pack_ref[_ROW_B_GATES + 2 * l:_ROW_B_GATES + 2 * l + 1, :] = dbg[l][:, 0:D]
            pack_ref[_ROW_B_GATES + 2 * l + 1:_ROW_B_GATES + 2 * l + 2, :] = dbg[l][:, D:2 * D]
            pack_ref[_ROW_SINKS + l:_ROW_SINKS + l + 1, 0:128] = dsk[l][0:1, :]
            pack_ref[_ROW_CONV + 4 * l:_ROW_CONV + 4 * l + 4, :] = dwb[l][0:4, :]
            pack_ref[_ROW_G_MLP + l:_ROW_G_MLP + l + 1, :] = dgmlp[l][...]
        pack_ref[_ROW_G_FINAL:_ROW_G_FINAL + 1, :] = lst[0:1, :]
        pack_ref[_ROW_LOSS:_ROW_LOSS + 1, :] = lst[1:2, :]

        all_ref[me] = pack_ref[...]
        cps = []
        for k in range(1, n_dev):
            dx_, dy_, dc_ = (k >> 2) & 1, (k >> 1) & 1, k & 1
            peer = (x ^ dx_, y ^ dy_, c ^ dc_)
            cp = pltpu.make_async_remote_copy(pack_ref, all_ref.at[me], send_sem.at[k - 1], recv_sem.at[k - 1], peer, MESH)
            cp.start()
            cps.append(cp)
        for cp in cps:
            cp.wait()

        tot = all_ref[0]
        for d in range(1, n_dev):
            tot = tot + all_ref[d]
        pack_ref[...] = tot

        loss_ref[...] = pack_ref[_ROW_LOSS:_ROW_LOSS + 1, 0:1]

        def finish(name, idx, g):
            w, m, v = p_refs[name][idx], m_refs[name][idx], v_refs[name][idx]
            d, nm, nv = _adamw_math(w, g, m, v)
            for ref, val in zip(outs[name], (g, d, nm, nv)):
                ref[idx] = val

        for l in range(DEPTH):
            finish("g_mix", (slice(l, l + 1), slice(None)), pack_ref[_ROW_G_MIX + l:_ROW_G_MIX + l + 1, :])
            finish("g_mlp", (slice(l, l + 1), slice(None)), pack_ref[_ROW_G_MLP + l:_ROW_G_MLP + l + 1, :])
            finish("conv_b", (slice(l, l + 1), slice(None)), pack_ref[_ROW_CONV + 4 * l + 3:_ROW_CONV + 4 * l + 4, :])
            finish("sinks", (slice(l, l + 1), slice(None)), pack_ref[_ROW_SINKS + l:_ROW_SINKS + l + 1, 0:N_Q_HEADS])
            for hf in range(2):
                finish("b_gates", (slice(l, l + 1), slice(hf * D, (hf + 1) * D)),
                       pack_ref[_ROW_B_GATES + 2 * l + hf:_ROW_B_GATES + 2 * l + hf + 1, :])
        finish("g_final", (slice(0, 1), slice(None)), pack_ref[_ROW_G_FINAL:_ROW_G_FINAL + 1, :])

        def conv_w_chip(js):
            for l in range(DEPTH):
                for k in range(3):
                    row = _ROW_CONV + 4 * l + k
                    finish("conv_w", (l, slice(k, k + 1), slice(None)), pack_ref[row:row + 1, js * QW:(js + 1) * QW])

        _for_my_chip(j, conv_w_chip)

    vm = pl.BlockSpec(memory_space=pltpu.VMEM)
    ins = (parts["g_mix"] + parts["b_gates"] + parts["sinks"] + parts["conv"] + parts["g_mlp"] + [parts["loss"]]
           + [params[n] for n in names] + [moms[n] for n in names] + [vels[n] for n in names])
    out_shape = [jax.ShapeDtypeStruct((1, 1), F32)]
    for n in names:
        out_shape += [jax.ShapeDtypeStruct(params[n].shape, F32)] * 4
    res = pl.pallas_call(
        body, out_shape=out_shape, in_specs=[vm] * len(ins), out_specs=[vm] * len(out_shape),
        scratch_shapes=[pltpu.VMEM((SMALL_ROWS, D), F32), pltpu.VMEM((n_dev, SMALL_ROWS, D), F32),
                        pltpu.SemaphoreType.DMA((n_dev - 1,)), pltpu.SemaphoreType.DMA((n_dev - 1,))],
        name="small_allreduce_adamw")(*ins)
    loss = res[0]
    out = {n: res[1 + 4 * i:5 + 4 * i] for i, n in enumerate(names)}
    return loss, out


def kernel(x, g_mix, w_in, b_gates, sinks, w_attn_out, conv_w, conv_b, w_conv_out, w_o, g_mlp, w_up, w_down, g_final, loss_target, m_g_mix, m_w_in, m_b_gates, m_sinks, m_w_attn_out, m_conv_w, m_conv_b, m_w_conv_out, m_w_o, m_g_mlp, m_w_up, m_w_down, m_g_final, v_g_mix, v_w_in, v_b_gates, v_sinks, v_w_attn_out, v_conv_w, v_conv_b, v_w_conv_out, v_w_o, v_g_mlp, v_w_up, v_w_down, v_g_final):
    B, S, D = x.shape
    T = B * S
    big_w = dict(w_in=w_in, w_attn_out=w_attn_out, w_conv_out=w_conv_out, w_o=w_o, w_up=w_up, w_down=w_down)
    big_m = dict(w_in=m_w_in, w_attn_out=m_w_attn_out, w_conv_out=m_w_conv_out, w_o=m_w_o, w_up=m_w_up, w_down=m_w_down)
    big_v = dict(w_in=v_w_in, w_attn_out=v_w_attn_out, w_conv_out=v_w_conv_out, w_o=v_w_o, w_up=v_w_up, w_down=v_w_down)

    c_arr = lax.axis_index("c").astype(jnp.int32).reshape(1)
    j_arr = (2 * lax.axis_index("x") + lax.axis_index("y")).astype(jnp.int32).reshape(1)
    jc_arr = jnp.concatenate([j_arr, c_arr])
    order = [(n, l) for n, _, _, _ in BIG for l in range(DEPTH)]
    dims = {n: (r, c_, k) for n, r, c_, k in BIG}

    wdims = dict(dims, w_up=(D, D_FF, "chip"))
    full = {(n, l): _cast_into_full(j_arr, big_w[n], l, *wdims[n]) for n, l in order}
    mixers = ("w_attn_out", "w_conv_out", "w_o")

    def gather_of(keys):
        return _gather_comm([(full[k],) + wdims[k[0]] for k in keys])

    carried = {("proj", 0): [(n, 0) for n in mixers] + [("w_up", 0)], ("attn", 0): [("w_down", 0)],
               ("mlp", 0): [("w_in", 1)] + [(n, 1) for n in mixers], ("proj", 1): [("w_up", 1)],
               ("attn", 1): [("w_down", 1)]}

    def carry(fn, where, *args):
        keys = carried.get(where)
        if keys is None:
            return fn(*args)
        res, got = fn(*args, comm=gather_of(keys))
        full.update(zip(keys, got))
        return res

    first = _run_comm(_gather_comm([(full["w_in", 0],) + dims["w_in"]], conv_w), "gather_weights")
    full["w_in", 0] = first[0]
    conv_w_full = jnp.transpose(first[1], (0, 2, 1, 3)).reshape(DEPTH, 3, D)
    attn_bias = _attn_bias_table()

    xs = [x.reshape(T, D)]
    saved = []
    for l in range(DEPTH):
        ht, pqkv, pconv, pgate = carry(_norm_proj, ("proj", l), xs[-1], g_mix[l:l + 1], full["w_in", l])
        att, att_t, lse = carry(_attn_fwd, ("attn", l), pqkv, sinks[l], attn_bias, S)
        x1, ya, yc, mg_t, cv_t = _mix_fwd(xs[-1], att, pconv, pgate, conv_w_full[l], conv_b[l:l + 1], b_gates[l:l + 1],
                                          full["w_attn_out", l], full["w_conv_out", l], full["w_o", l], S)
        x2, a = carry(_mlp_fwd, ("mlp", l), x1, g_mlp[l:l + 1], full["w_up", l], full["w_down", l].reshape(N_CHIP, D, D))
        saved.append(dict(ht=ht, pqkv=pqkv, pconv=pconv, pgate=pgate, att=att, att_t=att_t, lse=lse, cv_t=cv_t, x1=x1, ya=ya,
                          yc=yc, mg_t=mg_t, a=a))
        xs.append(x2)

    loss_stats, dx = _loss_bwd(xs[-1], g_final.reshape(1, D), loss_target.reshape(T, D))

    parts = dict(g_mix=[None] * DEPTH, b_gates=[None] * DEPTH, sinks=[None] * DEPTH, conv=[None] * DEPTH,
                 g_mlp=[None] * DEPTH, loss=loss_stats)
    gf, gb, pre, got, mine = {}, {}, {}, {}, {}

    def sibling_exchange(keys):
        return _sibling_exchange_comm([(gb[k],) + dims[k[0]] for k in keys])

    def half_adds(keys, sib):
        for k, s in zip(keys, sib):
            pre[k] = _half_add(jc_arr, gf[k], s, *dims[k[0]])

    def chip_exchange(keys):
        return _chip_exchange_comm([pre[k][0] for k in keys])

    def owner_sums(keys):
        for n, l in keys:
            mine[n] = _owner_sum(jc_arr, pre[n, l][1], got[n, l], l, mine.get(n))

    def run(fn, comms, *args):
        if not comms:
            return fn(*args), []
        res, arrived = fn(*args, comm=_compose(*comms))
        outs, pos = [], 0
        for cm in comms:
            outs.append(arrived[pos:pos + len(cm.out_shape)])
            pos += len(cm.out_shape)
        return res, outs

    assert DEPTH == 2
    upper = [(n, 1) for n, _, _, _ in BIG]
    early_mlp, early_mix = [("w_up", 0), ("w_down", 0)], [(n, 0) for n in mixers]
    early = early_mix + early_mlp
    for l in reversed(range(DEPTH)):
        W = {n: full[(n, l)] for n in big_w}
        sv = saved[l]
        last = l == 0
        mlp_args = (dx, sv["x1"], sv["a"], g_mlp[l:l + 1], W["w_up"], W["w_down"].reshape(N_CHIP, D, D))
        (dx1, da, u_t, h2_t, dyb, parts["g_mlp"][l]), arrived = run(_mlp_bwd, [chip_exchange(upper)] if last else [], *mlp_args)
        if last:
            got.update(zip(upper, arrived[0]))
        gf["w_up", l], gb["w_up", l] = _dw(h2_t, da)
        gf["w_down", l], gb["w_down", l] = _dw(u_t, dyb.reshape(-1, _dw_chunk(T), D))
        mix_args = (dx1, sv["ya"], sv["yc"], sv["pgate"], sv["pconv"], conv_w_full[l], conv_b[l:l + 1], b_gates[l:l + 1],
                    W["w_attn_out"], W["w_conv_out"], W["w_o"], S)
        (datt, dya, dyc, dgate, dconv, parts["b_gates"][l], parts["conv"][l]), arrived = run(
            _mix_bwd, [sibling_exchange(early_mlp)] if last else [], *mix_args)
        if last:
            half_adds(early_mlp, arrived[0])
        gf["w_o", l], gb["w_o", l] = _dw(sv["mg_t"], dx1)
        gf["w_attn_out", l], gb["w_attn_out", l] = _dw(sv["att_t"], dya)
        gf["w_conv_out", l], gb["w_conv_out", l] = _dw(sv["cv_t"], dyc)
        if last:
            half_adds(early_mix, _run_comm(sibling_exchange(early_mix), "grad_sibling_exchange_mixers"))
        attn_args = (sv["pqkv"], datt, sv["lse"], sinks[l], attn_bias, S)
        (dq, dkv, parts["sinks"][l]), arrived = run(_attn_bwd, [chip_exchange(early)] if last else [], *attn_args)
        if last:
            got.update(zip(early, arrived[0]))
            owner_sums(upper + early)
        pieces = [(dq, D), (dkv, QKV_W - D), (dconv, CONV_W), (dgate, GATE_W)]
        gf["w_in", l], gb["w_in", l] = _dw_pieces(sv["ht"], pieces)
        in_args = (pieces, W["w_in"], xs[l], dx1, g_mix[l:l + 1])
        if not last:
            (dx, parts["g_mix"][l]), arrived = run(_inproj_bwd, [sibling_exchange(upper)], *in_args)
            half_adds(upper, arrived[0])
        else:
            tail = [("w_in", 0)]
            half_adds(tail, _run_comm(sibling_exchange(tail), "grad_sibling_exchange"))
            done = _sibling_assemble_comm(list(mine.values()), [(1,) if n == "w_in" else (0, 1) for n in mine])
            (dx, parts["g_mix"][l]), arrived = run(_inproj_bwd, [chip_exchange(tail), done], *in_args)
            got.update(zip(tail, arrived[0]))
            mine = dict(zip(mine, arrived[1]))
            owner_sums(tail)
            mine["w_in"] = _run_comm(_sibling_assemble_comm([mine["w_in"]], [(0,)]), "grad_sibling_assemble")[0]
    grads = mine

    res = {}
    for n in big_w:
        res[n] = tuple(_adamw(big_w[n], grads[n], big_m[n], big_v[n]))

    small_p = dict(g_mix=g_mix, b_gates=b_gates, sinks=sinks, conv_w=conv_w, conv_b=conv_b, g_mlp=g_mlp, g_final=g_final.reshape(1, D))
    small_m = dict(g_mix=m_g_mix, b_gates=m_b_gates, sinks=m_sinks, conv_w=m_conv_w, conv_b=m_conv_b, g_mlp=m_g_mlp,
                   g_final=m_g_final.reshape(1, D))
    small_v = dict(g_mix=v_g_mix, b_gates=v_b_gates, sinks=v_sinks, conv_w=v_conv_w, conv_b=v_conv_b, g_mlp=v_g_mlp,
                   g_final=v_g_final.reshape(1, D))
    loss, small = _small_step(parts, small_p, small_m, small_v)
    for n, vals in small.items():
        res[n] = tuple(v.reshape(D) for v in vals) if n == "g_final" else tuple(vals)

    weights = ["g_mix", "w_in", "b_gates", "sinks", "w_attn_out", "conv_w", "conv_b", "w_conv_out", "w_o", "g_mlp", "w_up",
               "w_down", "g_final"]
    out = [loss.reshape(()), dx.reshape(B, S, D)]
    for k in range(4):
        out += [res[n][k] for n in weights]
    return tuple(out)
```

```python
import functools

import numpy as np
import jax
import jax.numpy as jnp
from jax import lax
from jax.experimental import pallas as pl
from jax.experimental.pallas import tpu as pltpu

F32 = jnp.float32
BF16 = jnp.bfloat16

D_MODEL = 1024
HEAD_DIM = 64
N_Q_HEADS = 16
N_KV_HEADS = 4
GQA_GROUP = 4
WINDOW = 128
D_FF = 4096
DEPTH = 2
RMS_EPS = 1e-6
NEG_INF = -1e30
ATTN_SCALE = HEAD_DIM ** -0.5
QKV_W = 1536
CONV_W = 3072
GATE_W = 2048
IN_COLS = QKV_W + CONV_W + GATE_W
COL_TILE = 512
N_CHIP = 4
ADAM_LR = 0.001
ADAM_B1 = 0.9
ADAM_B2 = 0.999
ADAM_EPS = 1e-08
ADAM_WD = 0.01
ADAM_STEP = 10
V7X_VMEM_BYTES = 64 * 2 ** 20
VMEM_LIMIT = V7X_VMEM_BYTES - 8 * 2 ** 20
MESH = pl.DeviceIdType.MESH
ANY = pl.BlockSpec(memory_space=pl.ANY)
SMALL_ROWS = 24

_SLOPES = [float(v) for v in np.power(np.float32(2.0), -8.0 * np.arange(1, N_Q_HEADS + 1, dtype=np.float32) / N_Q_HEADS)]


def _params(*sem):
    return pltpu.CompilerParams(dimension_semantics=sem, vmem_limit_bytes=VMEM_LIMIT)


class _Hosted:
    def __init__(self, inputs, out_shape, aliases, scratch, start, finish):
        self.inputs, self.out_shape, self.aliases, self.scratch = list(inputs), list(out_shape), dict(aliases), list(scratch)
        self.start, self.finish = start, finish


def _hosted_call(comm, kern, *, out_shape, grid, in_specs, out_specs, args, name, sem, scratch_shapes=()):
    single = not isinstance(out_shape, (tuple, list))
    outs = [out_shape] if single else list(out_shape)
    ospecs = [out_specs] if single else list(out_specs)
    if comm is None:
        res = pl.pallas_call(kern, out_shape=outs, grid=grid, in_specs=list(in_specs), out_specs=ospecs,
                             scratch_shapes=list(scratch_shapes), name=name, compiler_params=_params(*sem))(*args)
        return res[0] if single else res
    n_in, n_out, n_scr = len(args), len(outs), len(scratch_shapes)
    ci, co, cs = len(comm.inputs), len(comm.out_shape), len(comm.scratch)

    def body(*refs):
        cuts = np.cumsum([0, n_in, ci, n_out, co, n_scr, cs])
        a, b, c, d, e, f = [refs[lo:hi] for lo, hi in zip(cuts[:-1], cuts[1:])]
        ids = [pl.program_id(k) for k in range(len(grid))]
        first = functools.reduce(jnp.logical_and, [i == 0 for i in ids])
        last = functools.reduce(jnp.logical_and, [i == n - 1 for i, n in zip(ids, grid)])
        pl.when(first)(lambda: comm.start(b, d, f))
        kern(*a, *c, *e)
        pl.when(last)(lambda: comm.finish(b, d, f))

    res = pl.pallas_call(
        body, out_shape=outs + comm.out_shape, grid=grid, in_specs=list(in_specs) + [ANY] * ci, out_specs=ospecs + [ANY] * co,
        scratch_shapes=list(scratch_shapes) + comm.scratch,
        input_output_aliases={n_in + i: n_out + o for i, o in comm.aliases.items()},
        name=name + "_carrier", compiler_params=_params(*(["arbitrary"] * len(grid))))(*args, *comm.inputs)
    main = res[:n_out]
    return (main[0] if single else main), res[n_out:]


def _nt(a, b):
    return lax.dot_general(a, b, (((1,), (1,)), ((), ())), preferred_element_type=F32)


def _tn(a, b):
    return lax.dot_general(a, b, (((0,), (0,)), ((), ())), preferred_element_type=F32)


def _nn(a, b):
    return jnp.dot(a, b, preferred_element_type=F32)


def _rms_stats(xf):
    r = lax.rsqrt(jnp.mean(xf * xf, axis=-1, keepdims=True) + RMS_EPS)
    return r, xf * r


def _rms_bwd(dh, xh, r, g):
    dxh = dh * g
    dx = r * (dxh - xh * jnp.mean(dxh * xh, axis=-1, keepdims=True))
    dg = jnp.sum(dh * xh, axis=0, keepdims=True)
    return dx, dg


def _dw_chunk(T):
    return min(2048, T)


def _resident(shape):
    return pl.BlockSpec(shape, lambda *_: (0,) * len(shape), pipeline_mode=pl.Buffered(1))


def _norm_proj(x, g, w, comm=None):
    T, D = x.shape
    tm = min(512, T)
    tk = _dw_chunk(T)
    per = tk // tm
    widths = (QKV_W, CONV_W, GATE_W)

    def kern(x_ref, g_ref, w_ref, ht_ref, *o_refs):
        _, xh = _rms_stats(x_ref[...])
        h = (xh * g_ref[...]).astype(BF16)
        ht_ref[...] = h.T
        off = 0
        for o_ref, wd in zip(o_refs, widths):
            o_ref[...] = _nn(h, w_ref[:, off:off + wd]).astype(BF16)
            off += wd

    row = lambda wd: pl.BlockSpec((tm, wd), lambda i: (i, 0))
    return _hosted_call(
        comm, kern,
        out_shape=[jax.ShapeDtypeStruct((T // tk, D, tk), BF16)] + [jax.ShapeDtypeStruct((T, wd), BF16) for wd in widths],
        grid=(T // tm,), in_specs=[row(D), pl.BlockSpec((1, D), lambda i: (0, 0)), _resident((D, IN_COLS))],
        out_specs=[pl.BlockSpec((None, D, tm), lambda i: (i // per, 0, i % per))] + [row(wd) for wd in widths],
        name="norm_proj", sem=("parallel",), args=(x, g, w))


GW = GQA_GROUP * WINDOW
BAND = 2 * WINDOW
KV_W = N_KV_HEADS * HEAD_DIM


def _attn_bias_table():
    jj = np.arange(BAND)[:, None]
    col = np.arange(GW)[None, :]
    dist = WINDOW + (col % WINDOW) - jj
    valid = (dist >= 0) & (dist < WINDOW)
    slopes = np.asarray(_SLOPES, np.float32).reshape(N_KV_HEADS, GQA_GROUP)
    tab = np.empty((2, N_KV_HEADS, BAND, GW), np.float32)
    for hk in range(N_KV_HEADS):
        bias = -slopes[hk][col // WINDOW] * dist.astype(np.float32)
        tab[0, hk] = np.where(valid, bias, np.float32(NEG_INF))
        tab[1, hk] = np.where(valid & (jj >= WINDOW), bias, np.float32(NEG_INF))
    return jnp.asarray(tab)


def _stack_heads(ref, hk):
    return jnp.concatenate(
        [ref[:, HEAD_DIM * (GQA_GROUP * hk + g): HEAD_DIM * (GQA_GROUP * hk + g + 1)] for g in range(GQA_GROUP)], axis=0)


def _kv_band(cur_ref, prev_ref, hk):
    k0 = N_Q_HEADS * HEAD_DIM
    sl = slice(HEAD_DIM * hk, HEAD_DIM * (hk + 1))
    ksl, vsl = slice(k0 + sl.start, k0 + sl.stop), slice(k0 + KV_W + sl.start, k0 + KV_W + sl.stop)
    k_band = jnp.concatenate([prev_ref[:, sl], cur_ref[:, ksl]], axis=0)
    v_band = jnp.concatenate([prev_ref[:, KV_W + sl.start:KV_W + sl.stop], cur_ref[:, vsl]], axis=0)
    return k_band, v_band


def _lane_row(vals):
    return jnp.concatenate([jnp.full((1, WINDOW), v, F32) for v in vals], axis=1)


def _attn_fwd(pqkv, sinks, bias, seq, comm=None):
    T = pqkv.shape[0]
    nblk = seq // WINDOW

    def kern(sink_ref, cur_ref, prev_ref, bias_ref, o_ref, ot_ref, lse_ref):
        i = pl.program_id(0)
        first_i = ((i % nblk) == 0).astype(jnp.int32)
        bands = [_kv_band(cur_ref, prev_ref, hk) for hk in range(N_KV_HEADS)]
        sts = [_nt(bands[hk][0], _stack_heads(cur_ref, hk) * ATTN_SCALE) + bias_ref[first_i, hk] for hk in range(N_KV_HEADS)]
        ps, scales = [], []
        for hk in range(N_KV_HEADS):
            heads = [GQA_GROUP * hk + g for g in range(GQA_GROUP)]
            sink = _lane_row([sink_ref[h] for h in heads])
            m = jnp.maximum(jnp.max(sts[hk], axis=0, keepdims=True), sink)
            p = jnp.exp(sts[hk] - m)
            den = jnp.sum(p, axis=0, keepdims=True) + jnp.exp(sink - m)
            lse = m + jnp.log(den)
            for g, h in enumerate(heads):
                lse_ref[h:h + 1, :] = lse[:, WINDOW * g:WINDOW * (g + 1)]
            ps.append(p.astype(BF16))
            scales.append(1.0 / den)
        for hk in range(N_KV_HEADS):
            ot = _tn(bands[hk][1], ps[hk]) * scales[hk]
            for g in range(GQA_GROUP):
                h = GQA_GROUP * hk + g
                ot_ref[HEAD_DIM * h:HEAD_DIM * (h + 1), :] = ot[:, WINDOW * g:WINDOW * (g + 1)].astype(BF16)
        o_ref[...] = ot_ref[...].T

    return _hosted_call(
        comm, kern,
        out_shape=(jax.ShapeDtypeStruct((T, D_MODEL), BF16), jax.ShapeDtypeStruct((D_MODEL, T), BF16),
                   jax.ShapeDtypeStruct((N_Q_HEADS, T), F32)),
        grid=(T // WINDOW,),
        in_specs=[pl.BlockSpec(memory_space=pltpu.SMEM),
                  pl.BlockSpec((WINDOW, QKV_W), lambda i: (i, 0)),
                  pl.BlockSpec((WINDOW, 2 * KV_W), lambda i: (jnp.maximum(i - 1, 0), 2)),
                  _resident(bias.shape)],
        out_specs=(pl.BlockSpec((WINDOW, D_MODEL), lambda i: (i, 0)), pl.BlockSpec((D_MODEL, WINDOW), lambda i: (0, i)),
                   pl.BlockSpec((N_Q_HEADS, WINDOW), lambda i: (0, i))),
        name="attn_fwd", sem=("parallel",), args=(sinks, pqkv, pqkv, bias))


def _pick_row(a, row):
    rid = lax.broadcasted_iota(jnp.int32, a.shape, 0)
    return jnp.sum(jnp.where(rid == row, a, 0.0), axis=0, keepdims=True)


def _conv_taps(yc, halo_yc, first_i):
    keep = (1 - first_i).astype(F32)
    p1 = _pick_row(halo_yc, 15) * keep
    p2 = _pick_row(halo_yc, 14) * keep
    rowid = lax.broadcasted_iota(jnp.int32, yc.shape, 0)
    s1 = jnp.where(rowid == 0, p1, pltpu.roll(yc, 1, 0))
    s2 = jnp.where(rowid == 0, p2, jnp.where(rowid == 1, p1, pltpu.roll(yc, 2, 0)))
    return s1, s2


def _mix_fwd(x, att, pconv, pgate, conv_w, conv_b, b_gates, wao, wco, wo, seq):
    T, D = x.shape
    tm = min(512, seq)
    per_seq = seq // tm

    def kern(x_ref, att_ref, pc_ref, halo_ref, pg_ref, cw_ref, cb_ref, bg_ref, wao_ref, wco_ref, wo_ref,
             x1_ref, ya_ref, yc_ref, mgt_ref, cvt_ref):
        first_i = ((pl.program_id(0) % per_seq) == 0).astype(jnp.int32)
        ya = _nn(att_ref[...], wao_ref[...])
        u = pc_ref[:, D:2 * D].astype(F32) * pc_ref[:, 2 * D:3 * D].astype(F32)
        halo_u = halo_ref[:, D:2 * D].astype(F32) * halo_ref[:, 2 * D:3 * D].astype(F32)
        s1, s2 = _conv_taps(u, halo_u, first_i)
        z = cw_ref[0:1, :] * s2 + cw_ref[1:2, :] * s1 + cw_ref[2:3, :] * u
        cv = (pc_ref[:, 0:D].astype(F32) * (z + cb_ref[...])).astype(BF16)
        cvt_ref[...] = cv.T
        yc = _nn(cv, wco_ref[...])
        sa = jax.nn.sigmoid(pg_ref[:, 0:D].astype(F32) + bg_ref[:, 0:D])
        sc = jax.nn.sigmoid(pg_ref[:, D:2 * D].astype(F32) + bg_ref[:, D:2 * D])
        mg = (sa * ya + sc * yc).astype(BF16)
        ya_ref[...] = ya.astype(BF16)
        yc_ref[...] = yc.astype(BF16)
        mgt_ref[...] = mg.T
        x1_ref[...] = x_ref[...] + _nn(mg, wo_ref[...])

    row = lambda w: pl.BlockSpec((tm, w), lambda i: (i, 0))
    full = lambda a, b: pl.BlockSpec((a, b), lambda i: (0, 0))
    tsp = pl.BlockSpec((D, tm), lambda i: (0, i))
    bf, bft = jax.ShapeDtypeStruct((T, D), BF16), jax.ShapeDtypeStruct((D, T), BF16)
    return pl.pallas_call(
        kern, out_shape=(jax.ShapeDtypeStruct((T, D), F32), bf, bf, bft, bft), grid=(T // tm,),
        in_specs=[row(D), row(D), row(CONV_W), pl.BlockSpec((16, CONV_W), lambda i: (jnp.maximum(i * (tm // 16) - 1, 0), 0)),
                  row(GATE_W), full(3, D), full(1, D), full(1, GATE_W)] + [_resident((D, D))] * 3,
        out_specs=(row(D), row(D), row(D), tsp, tsp),
        name="mix_fwd", compiler_params=_params("parallel"))(x, att, pconv, pconv, pgate, conv_w, conv_b, b_gates, wao, wco, wo)


def _mlp_fwd(x1, g, wup, wdn, comm=None):
    T, D = x1.shape
    tm = min(1024, T)
    nj = D_FF // D

    def kern(x_ref, g_ref, wup_ref, wdn_ref, x2_ref, a_ref, h_scr, acc_scr):
        j = pl.program_id(1)

        @pl.when(j == 0)
        def _():
            xf = x_ref[...]
            _, xh = _rms_stats(xf)
            h_scr[...] = (xh * g_ref[...]).astype(BF16)
            acc_scr[...] = xf

        a = _nn(h_scr[...], wup_ref[j])
        a_ref[...] = a.astype(BF16)
        u = jnp.square(jnp.maximum(a, 0.0)).astype(BF16)
        acc_scr[...] += _nn(u, wdn_ref[j])

        @pl.when(j == nj - 1)
        def _():
            x2_ref[...] = acc_scr[...]

    return _hosted_call(
        comm, kern, out_shape=(jax.ShapeDtypeStruct((T, D), F32), jax.ShapeDtypeStruct((T, D_FF), BF16)), grid=(T // tm, nj),
        in_specs=[pl.BlockSpec((tm, D), lambda i, j: (i, 0)), pl.BlockSpec((1, D), lambda i, j: (0, 0)),
                  _resident((nj, D, D)), _resident((nj, D, D))],
        out_specs=(pl.BlockSpec((tm, D), lambda i, j: (i, 0)), pl.BlockSpec((tm, D), lambda i, j: (i, j))),
        scratch_shapes=[pltpu.VMEM((tm, D), BF16), pltpu.VMEM((tm, D), F32)],
        name="mlp_fwd", sem=("parallel", "arbitrary"), args=(x1, g, wup, wdn))


def _loss_bwd(x, g, tgt):
    T, D = x.shape
    tm = min(512, T)

    def kern(x_ref, g_ref, t_ref, st_ref, dx_ref):
        i = pl.program_id(0)

        @pl.when(i == 0)
        def _():
            st_ref[...] = jnp.zeros_like(st_ref)

        gg = g_ref[...]
        r, xh = _rms_stats(x_ref[...])
        e = xh * gg - t_ref[...]
        part = 0.5 * jnp.sum(jnp.mean(e * e, axis=-1, keepdims=True), axis=0, keepdims=True)
        dx, dg = _rms_bwd(e * (1.0 / D), xh, r, gg)
        dx_ref[...] = dx
        st_ref[0:1, :] += dg
        st_ref[1:2, 0:1] += part

    return pl.pallas_call(
        kern, out_shape=(jax.ShapeDtypeStruct((8, D), F32), jax.ShapeDtypeStruct((T, D), F32)), grid=(T // tm,),
        in_specs=[pl.BlockSpec((tm, D), lambda i: (i, 0)), pl.BlockSpec((1, D), lambda i: (0, 0)),
                  pl.BlockSpec((tm, D), lambda i: (i, 0))],
        out_specs=(pl.BlockSpec((8, D), lambda i: (0, 0)), pl.BlockSpec((tm, D), lambda i: (i, 0))),
        name="loss_bwd", compiler_params=_params("arbitrary"))(x, g, tgt)


def _mlp_bwd(dx2, x1, a, g, wup, wdn, comm=None):
    T, D = x1.shape
    tm = min(512, T)
    nj = D_FF // D
    tk = _dw_chunk(T)
    per = tk // tm

    def kern(dx2_ref, x1_ref, a_ref, g_ref, wup_ref, wdn_ref, dx1_ref, da_ref, ut_ref, h2t_ref, dyb_ref, dg_ref, acc_scr):
        i, j = pl.program_id(0), pl.program_id(1)

        @pl.when((i == 0) & (j == 0))
        def _():
            dg_ref[...] = jnp.zeros_like(dg_ref)

        @pl.when(j == 0)
        def _():
            dyb_ref[...] = dx2_ref[...].astype(BF16)
            acc_scr[...] = jnp.zeros_like(acc_scr)

        du = _nt(dyb_ref[...], wdn_ref[j])
        relu = jnp.maximum(a_ref[...].astype(F32), 0.0)
        da = (du * (2.0 * relu)).astype(BF16)
        da_ref[...] = da
        ut_ref[...] = jnp.square(relu).astype(BF16).T
        acc_scr[...] += _nt(da, wup_ref[j])

        @pl.when(j == nj - 1)
        def _():
            gg = g_ref[...]
            r, xh = _rms_stats(x1_ref[...])
            h2t_ref[...] = (xh * gg).astype(BF16).T
            dx, dg = _rms_bwd(acc_scr[...], xh, r, gg)
            dx1_ref[...] = dx2_ref[...] + dx
            dg_ref[...] += dg

    return _hosted_call(
        comm, kern,
        out_shape=(jax.ShapeDtypeStruct((T, D), F32), jax.ShapeDtypeStruct((T, D_FF), BF16),
                   jax.ShapeDtypeStruct((D_FF, T), BF16), jax.ShapeDtypeStruct((T // tk, D, tk), BF16),
                   jax.ShapeDtypeStruct((T, D), BF16), jax.ShapeDtypeStruct((1, D), F32)),
        grid=(T // tm, nj),
        in_specs=[pl.BlockSpec((tm, D), lambda i, j: (i, 0)), pl.BlockSpec((tm, D), lambda i, j: (i, 0)),
                  pl.BlockSpec((tm, D), lambda i, j: (i, j)), pl.BlockSpec((1, D), lambda i, j: (0, 0)),
                  _resident((nj, D, D)), _resident((nj, D, D))],
        out_specs=(pl.BlockSpec((tm, D), lambda i, j: (i, 0)), pl.BlockSpec((tm, D), lambda i, j: (i, j)),
                   pl.BlockSpec((D, tm), lambda i, j: (j, i)), pl.BlockSpec((None, D, tm), lambda i, j: (i // per, 0, i % per)),
                   pl.BlockSpec((tm, D), lambda i, j: (i, 0)), pl.BlockSpec((1, D), lambda i, j: (0, 0))),
        scratch_shapes=[pltpu.VMEM((tm, D), F32)],
        name="mlp_bwd", sem=("arbitrary", "arbitrary"), args=(dx2, x1, a, g, wup, wdn))


def _mix_bwd(dx1, ya, yc, pgate, pconv, conv_w, conv_b, b_gates, wao, wco, wo, seq, comm=None):
    T, D = dx1.shape
    tm = min(512, seq)
    per_seq = seq // tm
    n = T // tm
    GATE_ROWS = 16

    def kern(dx_ref, ya_ref, yc_ref, pg_ref, pc_ref, halo_ref, cw_ref, cb_ref, bg_ref, wao_ref, wco_ref, wo_ref,
             datt_ref, dya_ref, dyc_ref, dgt_ref, dcn_ref, dbg_ref, dwb_ref, next_scr, dm_scr):
        i = pl.program_id(0)
        r = n - 1 - i

        @pl.when(i == 0)
        def _():
            dbg_ref[...] = jnp.zeros_like(dbg_ref)
            dwb_ref[...] = jnp.zeros_like(dwb_ref)
            next_scr[...] = jnp.zeros_like(next_scr)

        dm_scr[...] = _nt(dx_ref[...].astype(BF16), wo_ref[...])
        sum_a = jnp.zeros((1, D), F32)
        sum_c = jnp.zeros((1, D), F32)
        for c in range(tm // GATE_ROWS):
            rows = slice(c * GATE_ROWS, (c + 1) * GATE_ROWS)
            dm = dm_scr[rows, :]
            sa = jax.nn.sigmoid(pg_ref[rows, 0:D].astype(F32) + bg_ref[:, 0:D])
            sc = jax.nn.sigmoid(pg_ref[rows, D:2 * D].astype(F32) + bg_ref[:, D:2 * D])
            dya_ref[rows, :] = (dm * sa).astype(BF16)
            dyc_ref[rows, :] = (dm * sc).astype(BF16)
            dga = dm * ya_ref[rows, :].astype(F32) * (sa * (1.0 - sa))
            dgc = dm * yc_ref[rows, :].astype(F32) * (sc * (1.0 - sc))
            dgt_ref[rows, 0:D] = dga.astype(BF16)
            dgt_ref[rows, D:2 * D] = dgc.astype(BF16)
            sum_a = sum_a + jnp.sum(dga, axis=0, keepdims=True)
            sum_c = sum_c + jnp.sum(dgc, axis=0, keepdims=True)
        dbg_ref[:, 0:D] += sum_a
        dbg_ref[:, D:2 * D] += sum_c
        datt_ref[...] = _nt(dya_ref[...], wao_ref[...]).astype(BF16)
        dcv = _nt(dyc_ref[...], wco_ref[...])

        first_i = ((r % per_seq) == 0).astype(jnp.int32)
        keep_next = 1.0 - (((r + 1) % per_seq) == 0).astype(F32)
        cb = pc_ref[:, 0:D].astype(F32)
        cc = pc_ref[:, D:2 * D].astype(F32)
        cu = pc_ref[:, 2 * D:3 * D].astype(F32)
        u = cc * cu
        halo_u = halo_ref[:, D:2 * D].astype(F32) * halo_ref[:, 2 * D:3 * D].astype(F32)
        s1, s2 = _conv_taps(u, halo_u, first_i)
        w0, w1, w2 = cw_ref[0:1, :], cw_ref[1:2, :], cw_ref[2:3, :]
        z = w0 * s2 + w1 * s1 + w2 * u
        dz = dcv * cb
        n1 = next_scr[0:1, :] * keep_next
        n2 = next_scr[1:2, :] * keep_next
        next_scr[...] = dz[0:8, :]
        rowid = lax.broadcasted_iota(jnp.int32, dz.shape, 0)
        u1 = jnp.where(rowid == tm - 1, n1, pltpu.roll(dz, tm - 1, 0))
        u2 = jnp.where(rowid == tm - 1, n2, jnp.where(rowid == tm - 2, n1, pltpu.roll(dz, tm - 2, 0)))
        du = w2 * dz + w1 * u1 + w0 * u2
        dcn_ref[:, 0:D] = (dcv * (z + cb_ref[...])).astype(BF16)
        dcn_ref[:, D:2 * D] = (du * cu).astype(BF16)
        dcn_ref[:, 2 * D:3 * D] = (du * cc).astype(BF16)
        dwb_ref[0:1, :] += jnp.sum(dz * s2, axis=0, keepdims=True)
        dwb_ref[1:2, :] += jnp.sum(dz * s1, axis=0, keepdims=True)
        dwb_ref[2:3, :] += jnp.sum(dz * u, axis=0, keepdims=True)
        dwb_ref[3:4, :] += jnp.sum(dz, axis=0, keepdims=True)

    row = lambda w: pl.BlockSpec((tm, w), lambda i: (n - 1 - i, 0))
    full = lambda a, b: pl.BlockSpec((a, b), lambda i: (0, 0))
    halo = pl.BlockSpec((16, CONV_W), lambda i: (jnp.maximum((n - 1 - i) * (tm // 16) - 1, 0), 0))
    bf = jax.ShapeDtypeStruct((T, D), BF16)
    return _hosted_call(
        comm, kern,
        out_shape=(bf, bf, bf, jax.ShapeDtypeStruct((T, GATE_W), BF16), jax.ShapeDtypeStruct((T, CONV_W), BF16),
                   jax.ShapeDtypeStruct((1, GATE_W), F32), jax.ShapeDtypeStruct((8, D), F32)),
        grid=(n,),
        in_specs=[row(D), row(D), row(D), row(GATE_W), row(CONV_W), halo, full(3, D), full(1, D), full(1, GATE_W)]
        + [_resident((D, D))] * 3,
        out_specs=(row(D), row(D), row(D), row(GATE_W), row(CONV_W), full(1, GATE_W), full(8, D)),
        scratch_shapes=[pltpu.VMEM((8, D), F32), pltpu.VMEM((tm, D), F32)],
        name="mix_bwd", sem=("arbitrary",), args=(dx1, ya, yc, pgate, pconv, pconv, conv_w, conv_b, b_gates, wao, wco, wo))


def _attn_bwd(pqkv, datt, lse, sinks, bias, seq, comm=None):
    T = pqkv.shape[0]
    nblk = seq // WINDOW
    nseq = T // seq
    KVW = KV_W

    def kern(sink_ref, cur_ref, prev_ref, do_ref, lse_ref, bias_ref, dq_ref, dkv_ref, ds_ref, kc_scr, vc_scr, dqt_scr):
        b, st = pl.program_id(0), pl.program_id(1)

        @pl.when((b == 0) & (st == 0))
        def _():
            ds_ref[...] = jnp.zeros_like(ds_ref)

        @pl.when(st == 0)
        def _():
            kc_scr[...] = jnp.zeros_like(kc_scr)
            vc_scr[...] = jnp.zeros_like(vc_scr)

        @pl.when(st < nblk)
        def _():
            first_i = (st == 0).astype(jnp.int32)
            groups = range(N_KV_HEADS)
            bands = [_kv_band(cur_ref, prev_ref, hk) for hk in groups]
            qs = [_stack_heads(cur_ref, hk) for hk in groups]
            dos = [_stack_heads(do_ref, hk) for hk in groups]
            sts = [_nt(bands[hk][0], qs[hk] * ATTN_SCALE) + bias_ref[first_i, hk] for hk in groups]
            dps = [_nt(bands[hk][1], dos[hk]) for hk in groups]
            pbs, dsss = [], []
            for hk in groups:
                heads = [GQA_GROUP * hk + g for g in range(GQA_GROUP)]
                sink = _lane_row([sink_ref[h] for h in heads])
                lse_g = jnp.concatenate([lse_ref[h:h + 1, :] for h in heads], axis=1)
                p = jnp.exp(sts[hk] - lse_g)
                d_row = jnp.sum(p * dps[hk], axis=0, keepdims=True)
                dsss.append((p * (dps[hk] - d_row) * ATTN_SCALE).astype(BF16))
                pbs.append(p.astype(BF16))
                psd = jnp.exp(sink - lse_g) * d_row
                for g, h in enumerate(heads):
                    ds_ref[0:1, h:h + 1] -= jnp.sum(psd[:, WINDOW * g:WINDOW * (g + 1)], axis=1, keepdims=True)
            for hk in groups:
                dqt = _tn(bands[hk][0], dsss[hk])
                dk_b = _nn(dsss[hk], qs[hk])
                dv_b = _nn(pbs[hk], dos[hk])
                for g in range(GQA_GROUP):
                    h = GQA_GROUP * hk + g
                    dqt_scr[HEAD_DIM * h:HEAD_DIM * (h + 1), :] = dqt[:, WINDOW * g:WINDOW * (g + 1)].astype(BF16)
                ksl = slice(HEAD_DIM * hk, HEAD_DIM * (hk + 1))
                vsl = slice(KVW + HEAD_DIM * hk, KVW + HEAD_DIM * (hk + 1))
                dkv_ref[:, ksl] = (kc_scr[:, ksl] + dk_b[0:WINDOW]).astype(BF16)
                dkv_ref[:, vsl] = (vc_scr[:, ksl] + dv_b[0:WINDOW]).astype(BF16)
                kc_scr[:, ksl] = dk_b[WINDOW:2 * WINDOW]
                vc_scr[:, ksl] = dv_b[WINDOW:2 * WINDOW]
            dq_ref[...] = dqt_scr[...].T

        @pl.when(st == nblk)
        def _():
            dkv_ref[:, 0:KVW] = kc_scr[...].astype(BF16)
            dkv_ref[:, KVW:2 * KVW] = vc_scr[...].astype(BF16)

    cur_map = lambda b, s: (b * nblk + jnp.minimum(s, nblk - 1), 0)
    prev_row = lambda b, s: b * nblk + jnp.clip(s - 1, 0, nblk - 1)
    return _hosted_call(
        comm, kern,
        out_shape=(jax.ShapeDtypeStruct((T, D_MODEL), BF16), jax.ShapeDtypeStruct((T, 2 * KVW), BF16),
                   jax.ShapeDtypeStruct((8, 128), F32)),
        grid=(nseq, nblk + 1),
        in_specs=[pl.BlockSpec(memory_space=pltpu.SMEM),
                  pl.BlockSpec((WINDOW, QKV_W), cur_map),
                  pl.BlockSpec((WINDOW, 2 * KVW), lambda b, s: (prev_row(b, s), 2)),
                  pl.BlockSpec((WINDOW, D_MODEL), cur_map),
                  pl.BlockSpec((N_Q_HEADS, WINDOW), lambda b, s: (0, b * nblk + jnp.minimum(s, nblk - 1))),
                  _resident(bias.shape)],
        out_specs=(pl.BlockSpec((WINDOW, D_MODEL), cur_map),
                   pl.BlockSpec((WINDOW, 2 * KVW), lambda b, s: (prev_row(b, s), 0)),
                   pl.BlockSpec((8, 128), lambda b, s: (0, 0))),
        scratch_shapes=[pltpu.VMEM((WINDOW, KVW), F32), pltpu.VMEM((WINDOW, KVW), F32), pltpu.VMEM((D_MODEL, WINDOW), BF16)],
        name="attn_bwd", sem=("arbitrary", "arbitrary"), args=(sinks, pqkv, pqkv, datt, lse, bias))


def _piece_tiles(pieces):
    out, start = [], 0
    for arr, width in pieces:
        out.append((arr, start, width // COL_TILE))
        start += width // COL_TILE
    return out, start


def _inproj_bwd(pieces, w_in, x, dx_in, g, comm=None):
    T, D = x.shape
    tm = min(512, T)

    def kern(*refs):
        p_refs = refs[:len(pieces)]
        w_ref, x_ref, dxin_ref, g_ref, dx_ref, dg_ref = refs[len(pieces):]

        @pl.when(pl.program_id(0) == 0)
        def _():
            dg_ref[...] = jnp.zeros_like(dg_ref)

        dh, off = None, 0
        for p_ref, (_, width) in zip(p_refs, pieces):
            part = _nt(p_ref[...], w_ref[:, off:off + width])
            dh = part if dh is None else dh + part
            off += width
        gg = g_ref[...]
        r, xh = _rms_stats(x_ref[...])
        dx, dg = _rms_bwd(dh, xh, r, gg)
        dx_ref[...] = dxin_ref[...] + dx
        dg_ref[...] += dg

    row = lambda wd: pl.BlockSpec((tm, wd), lambda i: (i, 0))
    return _hosted_call(
        comm, kern, out_shape=(jax.ShapeDtypeStruct((T, D), F32), jax.ShapeDtypeStruct((1, D), F32)), grid=(T // tm,),
        in_specs=[row(wd) for _, wd in pieces] + [_resident((D, IN_COLS)), row(D), row(D), pl.BlockSpec((1, D), lambda i: (0, 0))],
        out_specs=(row(D), pl.BlockSpec((1, D), lambda i: (0, 0))),
        name="inproj_bwd", sem=("arbitrary",), args=(*[a for a, _ in pieces], w_in, x, dx_in, g))


def _dw_pieces(lhs_t, pieces):
    nt, K, tk = lhs_t.shape
    tiles, nj = _piece_tiles(pieces)

    def kern(*refs):
        lhs_ref = refs[0]
        p_refs = refs[1:1 + len(tiles)]
        o_ref, ob_ref = refs[1 + len(tiles):]
        j, t = pl.program_id(0), pl.program_id(1)

        @pl.when(t == 0)
        def _():
            o_ref[...] = jnp.zeros_like(o_ref)

        for p_ref, (_, start, n) in zip(p_refs, tiles):
            @pl.when((j >= start) & (j < start + n))
            def _(p_ref=p_ref):
                o_ref[...] += _nn(lhs_ref[t], p_ref[...])

        @pl.when(t == nt - 1)
        def _():
            ob_ref[...] = o_ref[...].astype(BF16)

    def p_map(start, n):
        return lambda j, t: (jnp.where((j >= start) & (j < start + n), t, 0), jnp.clip(j - start, 0, n - 1))

    N = nj * COL_TILE
    return pl.pallas_call(
        kern, out_shape=(jax.ShapeDtypeStruct((K, N), F32), jax.ShapeDtypeStruct((K, N), BF16)), grid=(nj, nt),
        in_specs=[_resident((nt, K, tk))] + [pl.BlockSpec((tk, COL_TILE), p_map(s, n)) for _, s, n in tiles],
        out_specs=(pl.BlockSpec((K, COL_TILE), lambda j, t: (0, j)), pl.BlockSpec((K, COL_TILE), lambda j, t: (0, j))),
        name="dw_pieces", compiler_params=_params("arbitrary", "arbitrary"))(lhs_t, *[a for a, _, _ in tiles])


def _dw(lhs_t, rhs):
    lhs_res, rhs_res = lhs_t.ndim == 3, rhs.ndim == 3
    W = D_MODEL
    if lhs_res:
        nt, K, tk = lhs_t.shape
    else:
        K, tk = lhs_t.shape[0], _dw_chunk(lhs_t.shape[1])
        nt = lhs_t.shape[1] // tk
    N = rhs.shape[-1]

    def kern(lhs_ref, rhs_ref, o_ref, ob_ref):
        t = pl.program_id(2)

        @pl.when(t == 0)
        def _():
            o_ref[...] = jnp.zeros_like(o_ref)

        a = lhs_ref[t] if lhs_res else lhs_ref[...]
        b = rhs_ref[t] if rhs_res else rhs_ref[...]
        o_ref[...] += _nn(a, b.astype(BF16))

        @pl.when(t == nt - 1)
        def _():
            ob_ref[...] = o_ref[...].astype(BF16)

    omap = lambda i, j, t: (i, j)
    lspec = _resident((nt, W, tk)) if lhs_res else pl.BlockSpec((W, tk), lambda i, j, t: (i, t))
    rspec = _resident((nt, tk, W)) if rhs_res else pl.BlockSpec((tk, W), lambda i, j, t: (t, j))
    return pl.pallas_call(
        kern, out_shape=(jax.ShapeDtypeStruct((K, N), F32), jax.ShapeDtypeStruct((K, N), BF16)), grid=(K // W, N // W, nt),
        in_specs=[lspec, rspec], out_specs=(pl.BlockSpec((W, W), omap), pl.BlockSpec((W, W), omap)),
        name="dw", compiler_params=_params("arbitrary", "arbitrary", "arbitrary"))(lhs_t, rhs)


BIG = (("w_in", D_MODEL, IN_COLS, "col"), ("w_attn_out", D_MODEL, D_MODEL, "row"), ("w_conv_out", D_MODEL, D_MODEL, "row"),
       ("w_o", D_MODEL, D_MODEL, "row"), ("w_up", D_MODEL, D_FF, "col"), ("w_down", D_FF, D_MODEL, "row"))


def _shard_dims(rows, cols, kind):
    return (rows, cols // N_CHIP) if kind in ("col", "chip") else (rows // N_CHIP, cols)


def _window(ref, rows, cols, kind, chip, half):
    sr, sc = _shard_dims(rows, cols, kind)
    hr = sr // 2
    if kind == "col":
        return ref.at[pl.ds(half * hr, hr), pl.ds(chip * sc, sc)]
    if kind == "chip":
        return ref.at[chip, pl.ds(half * hr, hr), :]
    return ref.at[pl.ds(chip * sr + half * hr, hr), :]


def _mesh_pos():
    x, y, c = lax.axis_index("x"), lax.axis_index("y"), lax.axis_index("c")
    return x, y, c, 2 * x + y


_REL_BITS = (2, 1, 3)


def _rel_dev(x, y, c, r):
    return ((1 - x, y, c), (x, 1 - y, c), (1 - x, 1 - y, c))[r]


def _for_my_chip(j, fn):
    for js in range(N_CHIP):
        pl.when(j == js)(functools.partial(fn, js))


def _cast_into_full(j_arr, shard, l, rows, cols, kind):
    sr, sc = _shard_dims(rows, cols, kind)
    tr = min(256, sr)

    def kern(j_ref, s_ref, o_ref):
        o_ref[...] = s_ref[...].astype(BF16)

    shape, block = (rows, cols), (tr, sc)
    if kind == "col":
        omap = lambda i, j_ref: (i, j_ref[0])
    elif kind == "chip":
        shape, block = (N_CHIP, rows, sc), (None, tr, sc)
        omap = lambda i, j_ref: (j_ref[0], i, 0)
    else:
        omap = lambda i, j_ref: (j_ref[0] * (sr // tr) + i, 0)
    gs = pltpu.PrefetchScalarGridSpec(
        num_scalar_prefetch=1, grid=(sr // tr,),
        in_specs=[pl.BlockSpec((None, tr, sc), lambda i, j_ref: (l, i, 0))], out_specs=pl.BlockSpec(block, omap))
    return pl.pallas_call(kern, out_shape=jax.ShapeDtypeStruct(shape, BF16), grid_spec=gs, name="cast_into_full",
                          compiler_params=_params("arbitrary"))(j_arr, shard)


def _gather_comm(fulls, cw=None):
    n_big = len(fulls)
    n_piece = n_big + (0 if cw is None else 1)

    def pieces(in_refs, o_refs, js, c):
        def piece(p, chip, half):
            if p == n_big:
                return o_refs[p].at[half, chip]
            _, rows, cols, kind = fulls[p]
            return _window(o_refs[p], rows, cols, kind, chip, half)

        def mine(p):
            return in_refs[p].at[c] if p == n_big else piece(p, js, c)

        return piece, mine

    def local_copies(in_refs, piece, js, loc_sem):
        if cw is None:
            return []
        return [pltpu.make_async_copy(in_refs[n_big].at[half], piece(n_big, js, half), loc_sem.at[half]) for half in range(2)]

    def ici_copy(piece, mine, js, x, y, c, r, p, send_sem, recv_sem):
        return pltpu.make_async_remote_copy(mine(p), piece(p, js, c), send_sem.at[r * n_piece + p], recv_sem.at[r * n_piece + p],
                                            _rel_dev(x, y, c, r), MESH)

    def start(in_refs, o_refs, sems):
        send_sem, recv_sem, _, _, loc_sem = sems
        x, y, c, j = _mesh_pos()

        def run(js):
            piece, mine = pieces(in_refs, o_refs, js, c)
            for cp in local_copies(in_refs, piece, js, loc_sem):
                cp.start()
            for r in range(3):
                for p in range(n_piece):
                    ici_copy(piece, mine, js, x, y, c, r, p, send_sem, recv_sem).start()

        _for_my_chip(j, run)

    def finish(in_refs, o_refs, sems):
        send_sem, recv_sem, fsend_sem, frecv_sem, loc_sem = sems
        x, y, c, j = _mesh_pos()

        def run(js):
            piece, mine = pieces(in_refs, o_refs, js, c)
            fwds = []
            for r in range(3):
                ks = js ^ _REL_BITS[r]
                for p in range(n_piece):
                    got = piece(p, ks, c)
                    pltpu.make_async_remote_copy(got, got, send_sem.at[r * n_piece + p], recv_sem.at[r * n_piece + p],
                                                 _rel_dev(x, y, c, r), MESH).wait_recv()
                    cp = pltpu.make_async_remote_copy(got, got, fsend_sem.at[r * n_piece + p], frecv_sem.at[r * n_piece + p],
                                                      (x, y, 1 - c), MESH)
                    cp.start()
                    fwds.append(cp)
            for r in range(3):
                ks = js ^ _REL_BITS[r]
                for p in range(n_piece):
                    got = piece(p, ks, 1 - c)
                    pltpu.make_async_remote_copy(got, got, fsend_sem.at[r * n_piece + p], frecv_sem.at[r * n_piece + p],
                                                 (x, y, 1 - c), MESH).wait_recv()
            for r in range(3):
                for p in range(n_piece):
                    ici_copy(piece, mine, js, x, y, c, r, p, send_sem, recv_sem).wait_send()
            for cp in fwds:
                cp.wait_send()
            for cp in local_copies(in_refs, piece, js, loc_sem):
                cp.wait()

        _for_my_chip(j, run)

    out_shape = [jax.ShapeDtypeStruct(a.shape, BF16) for a, _, _, _ in fulls]
    ins = [a for a, _, _, _ in fulls]
    if cw is not None:
        out_shape.append(jax.ShapeDtypeStruct((DEPTH, N_CHIP, 3, D_MODEL // N_CHIP), F32))
        ins.append(cw)
    scratch = [pltpu.SemaphoreType.DMA((3 * n_piece,))] * 4 + [pltpu.SemaphoreType.DMA((2,))]
    return _Hosted(ins, out_shape, {p: p for p in range(n_big)}, scratch, start, finish)


def _compose(*comms):
    comms = [cm for cm in comms if cm is not None]
    if len(comms) <= 1:
        return comms[0] if comms else None
    ins, outs, aliases, scratch, cuts = [], [], {}, [], []
    for cm in comms:
        cuts.append((len(ins), len(outs), len(scratch)))
        aliases.update({len(ins) + i: len(outs) + o for i, o in cm.aliases.items()})
        ins, outs, scratch = ins + cm.inputs, outs + cm.out_shape, scratch + cm.scratch

    def parts(a, b, s):
        for cm, (i0, o0, s0) in zip(comms, cuts):
            yield cm, a[i0:i0 + len(cm.inputs)], b[o0:o0 + len(cm.out_shape)], s[s0:s0 + len(cm.scratch)]

    def start(a, b, s):
        for cm, pa, pb, ps in parts(a, b, s):
            cm.start(pa, pb, ps)

    def finish(a, b, s):
        for cm, pa, pb, ps in parts(a, b, s):
            cm.finish(pa, pb, ps)

    return _Hosted(ins, outs, aliases, scratch, start, finish)


def _run_comm(comm, name):
    n_in, n_out = len(comm.inputs), len(comm.out_shape)

    def body(*refs):
        comm.start(refs[:n_in], refs[n_in:n_in + n_out], refs[n_in + n_out:])
        comm.finish(refs[:n_in], refs[n_in:n_in + n_out], refs[n_in + n_out:])

    return pl.pallas_call(body, out_shape=comm.out_shape, in_specs=[ANY] * n_in, out_specs=[ANY] * n_out,
                          input_output_aliases=comm.aliases, scratch_shapes=comm.scratch, name=name)(*comm.inputs)


def _sibling_exchange_comm(gb):
    n = len(gb)

    def copies(g_refs, o_refs, sems, c):
        send_sem, recv_sem = sems
        x, y, _, _ = _mesh_pos()
        return [pltpu.make_async_remote_copy(_window(g_refs[t], rows, cols, kind, chip, 1 - c),
                                             _window(o_refs[t], rows, cols, kind, chip, 1 - c),
                                             send_sem.at[N_CHIP * t + chip], recv_sem.at[N_CHIP * t + chip], (x, y, 1 - c), MESH)
                for t, (_, rows, cols, kind) in enumerate(gb) for chip in range(N_CHIP)]

    def start(g_refs, o_refs, sems):
        for cp in copies(g_refs, o_refs, sems, lax.axis_index("c")):
            cp.start()

    def finish(g_refs, o_refs, sems):
        c = lax.axis_index("c")
        for cp in copies(g_refs, o_refs, sems, 1 - c):
            cp.wait_recv()
        for cp in copies(g_refs, o_refs, sems, c):
            cp.wait_send()

    return _Hosted([a for a, _, _, _ in gb], [jax.ShapeDtypeStruct(a.shape, BF16) for a, _, _, _ in gb], {},
                   [pltpu.SemaphoreType.DMA((N_CHIP * n,))] * 2, start, finish)


def _half_add(jc_arr, g, sib, rows, cols, kind):
    sr, sc = _shard_dims(rows, cols, kind)
    hr = sr // 2

    def kern(jc_ref, g_ref, s_ref, ob_ref, of_ref):
        v = g_ref[...] + s_ref[...].astype(F32)
        ob_ref[...] = v.astype(BF16)

        @pl.when(pl.program_id(0) == jc_ref[0])
        def _():
            of_ref[...] = v

    if kind == "col":
        imap = lambda j, jc_ref: (jc_ref[1], j)
    else:
        imap = lambda j, jc_ref: (2 * j + jc_ref[1], 0)
    gs = pltpu.PrefetchScalarGridSpec(
        num_scalar_prefetch=1, grid=(N_CHIP,),
        in_specs=[pl.BlockSpec((hr, sc), imap), pl.BlockSpec((hr, sc), imap)],
        out_specs=[pl.BlockSpec((None, hr, sc), lambda j, jc_ref: (j, 0, 0)), pl.BlockSpec((hr, sc), lambda j, jc_ref: (0, 0))])
    return pl.pallas_call(
        kern, out_shape=(jax.ShapeDtypeStruct((N_CHIP, hr, sc), BF16), jax.ShapeDtypeStruct((hr, sc), F32)),
        grid_spec=gs, name="grad_half_add", compiler_params=_params("arbitrary"))(jc_arr, g, sib)


def _chip_exchange_comm(sbs):
    n = len(sbs)

    def copies(s_refs, o_refs, sems):
        send_sem, recv_sem = sems
        x, y, c, j = _mesh_pos()
        return [pltpu.make_async_remote_copy(s_refs[t].at[j ^ _REL_BITS[r]], o_refs[t].at[r], send_sem.at[n * r + t],
                                             recv_sem.at[n * r + t], _rel_dev(x, y, c, r), MESH)
                for r in range(3) for t in range(n)]

    def start(s_refs, o_refs, sems):
        for cp in copies(s_refs, o_refs, sems):
            cp.start()

    def finish(s_refs, o_refs, sems):
        for cp in copies(s_refs, o_refs, sems):
            cp.wait()

    return _Hosted(sbs, [jax.ShapeDtypeStruct((3,) + a.shape[1:], BF16) for a in sbs], {},
                   [pltpu.SemaphoreType.DMA((3 * n,))] * 2, start, finish)


def _owner_sum(jc_arr, sf, rb, l, into=None):
    hr, sc = sf.shape

    def kern(jc_ref, s_ref, r0_ref, r1_ref, r2_ref, *rest):
        o_ref = rest[-1]
        o_ref[...] = ((s_ref[...] + r0_ref[...].astype(F32)) + r1_ref[...].astype(F32)) + r2_ref[...].astype(F32)

    in_specs = [pl.BlockSpec((hr, sc), lambda i, jc_ref: (0, 0))]
    in_specs += [pl.BlockSpec((None, hr, sc), lambda i, jc_ref, r=r: (r, 0, 0)) for r in range(3)]
    args = [jc_arr, sf, rb, rb, rb]
    aliases = {}
    if into is not None:
        in_specs.append(ANY)
        args.append(into)
        aliases = {len(args) - 1: 0}
    gs = pltpu.PrefetchScalarGridSpec(
        num_scalar_prefetch=1, grid=(1,), in_specs=in_specs,
        out_specs=pl.BlockSpec((None, hr, sc), lambda i, jc_ref: (l, jc_ref[1], 0)))
    return pl.pallas_call(kern, out_shape=jax.ShapeDtypeStruct((DEPTH, 2 * hr, sc), F32), grid_spec=gs,
                          input_output_aliases=aliases, name="grad_owner_sum", compiler_params=_params("arbitrary"))(*args)


def _sibling_assemble_comm(grads, layers):
    n = len(grads)
    todo = [(q, l) for q in range(n) for l in layers[q]]

    def copies(o_refs, sems, half):
        send_sem, recv_sem = sems
        x, y, c, _ = _mesh_pos()
        out = []
        for k, (q, l) in enumerate(todo):
            hr = grads[q].shape[1] // 2
            w = o_refs[q].at[l, pl.ds(half * hr, hr), :]
            out.append(pltpu.make_async_remote_copy(w, w, send_sem.at[k], recv_sem.at[k], (x, y, 1 - c), MESH))
        return out

    def start(_, o_refs, sems):
        for cp in copies(o_refs, sems, lax.axis_index("c")):
            cp.start()

    def finish(_, o_refs, sems):
        c = lax.axis_index("c")
        for cp in copies(o_refs, sems, 1 - c):
            cp.wait_recv()
        for cp in copies(o_refs, sems, c):
            cp.wait_send()

    return _Hosted(grads, [jax.ShapeDtypeStruct(g.shape, F32) for g in grads], {q: q for q in range(n)},
                   [pltpu.SemaphoreType.DMA((len(todo),))] * 2, start, finish)


def _adamw_math(w, g, m, v):
    m = ADAM_B1 * m + (1.0 - ADAM_B1) * g
    v = ADAM_B2 * v + (1.0 - ADAM_B2) * jnp.square(g)
    m_hat = m / (1.0 - ADAM_B1 ** ADAM_STEP)
    v_hat = v / (1.0 - ADAM_B2 ** ADAM_STEP)
    delta = -ADAM_LR * (m_hat / (jnp.sqrt(v_hat) + ADAM_EPS) + ADAM_WD * w)
    return delta, m, v


def _adamw(w, g, m, v):
    shape = w.shape
    C = shape[-1]
    R = int(np.prod(shape[:-1]))
    tr = min(256, R)
    args = [a.reshape(R, C) for a in (w, g, m, v)]

    def kern(w_ref, g_ref, m_ref, v_ref, go_ref, d_ref, nm_ref, nv_ref):
        g_val = g_ref[...]
        d, nm, nv = _adamw_math(w_ref[...], g_val, m_ref[...], v_ref[...])
        go_ref[...] = g_val
        d_ref[...] = d
        nm_ref[...] = nm
        nv_ref[...] = nv

    spec = pl.BlockSpec((tr, C), lambda i: (i, 0))
    outs = pl.pallas_call(
        kern, out_shape=[jax.ShapeDtypeStruct((R, C), F32)] * 4, grid=(R // tr,), in_specs=[spec] * 4, out_specs=[spec] * 4,
        name="adamw", compiler_params=_params("parallel"))(*args)
    return [o.reshape(shape) for o in outs]


_ROW_G_MIX, _ROW_B_GATES, _ROW_SINKS, _ROW_CONV, _ROW_G_MLP, _ROW_G_FINAL, _ROW_LOSS = 0, 2, 6, 8, 16, 18, 19


def _small_step(parts, params, moms, vels):
    names = ["g_mix", "b_gates", "sinks", "conv_w", "conv_b", "g_mlp", "g_final"]
    D = D_MODEL
    QW = D // N_CHIP
    n_dev = 8

    def body(*refs):
        it = iter(refs)
        dgmix = [next(it) for _ in range(DEPTH)]
        dbg = [next(it) for _ in range(DEPTH)]
        dsk = [next(it) for _ in range(DEPTH)]
        dwb = [next(it) for _ in range(DEPTH)]
        dgmlp = [next(it) for _ in range(DEPTH)]
        lst = next(it)
        p_refs = {n: next(it) for n in names}
        m_refs = {n: next(it) for n in names}
        v_refs = {n: next(it) for n in names}
        loss_ref = next(it)
        outs = {n: [next(it) for _ in range(4)] for n in names}
        pack_ref, all_ref, send_sem, recv_sem = next(it), next(it), next(it), next(it)

        x, y, c, j = _mesh_pos()
        me = 4 * x + 2 * y + c
        pack_ref[...] = jnp.zeros_like(pack_ref)
        for l in range(DEPTH):
            pack_ref[_ROW_G_MIX + l:_ROW_G_MIX + l + 1, :] = dgmix[l][...]
            pack_ref[_ROW_B_GATES + 2 * l:_ROW_B_GATES + 2 * l + 1, :] = dbg[l][:, 0:D]
            pack_ref[_ROW_B_GATES + 2 * l + 1:_ROW_B_GATES + 2 * l + 2, :] = dbg[l][:, D:2 * D]
            pack_ref[_ROW_SINKS + l:_ROW_SINKS + l + 1, 0:128] = dsk[l][0:1, :]
            pack_ref[_ROW_CONV + 4 * l:_ROW_CONV + 4 * l + 4, :] = dwb[l][0:4, :]
            pack_ref[_ROW_G_MLP + l:_ROW_G_MLP + l + 1, :] = dgmlp[l][...]
        pack_ref[_ROW_G_FINAL:_ROW_G_FINAL + 1, :] = lst[0:1, :]
        pack_ref[_ROW_LOSS:_ROW_LOSS + 1, :] = lst[1:2, :]

        all_ref[me] = pack_ref[...]
        cps = []
        for k in range(1, n_dev):
            dx_, dy_, dc_ = (k >> 2) & 1, (k >> 1) & 1, k & 1
            peer = (x ^ dx_, y ^ dy_, c ^ dc_)
            cp = pltpu.make_async_remote_copy(pack_ref, all_ref.at[me], send_sem.at[k - 1], recv_sem.at[k - 1], peer, MESH)
            cp.start()
            cps.append(cp)
        for cp in cps:
            cp.wait()

        tot = all_ref[0]
        for d in range(1, n_dev):
            tot = tot + all_ref[d]
        pack_ref[...] = tot

        loss_ref[...] = pack_ref[_ROW_LOSS:_ROW_LOSS + 1, 0:1]

        def finish(name, idx, g):
            w, m, v = p_refs[name][idx], m_refs[name][idx], v_refs[name][idx]
            d, nm, nv = _adamw_math(w, g, m, v)
            for ref, val in zip(outs[name], (g, d, nm, nv)):
                ref[idx] = val

        for l in range(DEPTH):
            finish("g_mix", (slice(l, l + 1), slice(None)), pack_ref[_ROW_G_MIX + l:_ROW_G_MIX + l + 1, :])
            finish("g_mlp", (slice(l, l + 1), slice(None)), pack_ref[_ROW_G_MLP + l:_ROW_G_MLP + l + 1, :])
            finish("conv_b", (slice(l, l + 1), slice(None)), pack_ref[_ROW_CONV + 4 * l + 3:_ROW_CONV + 4 * l + 4, :])
            finish("sinks", (slice(l, l + 1), slice(None)), pack_ref[_ROW_SINKS + l:_ROW_SINKS + l + 1, 0:N_Q_HEADS])
            for hf in range(2):
                finish("b_gates", (slice(l, l + 1), slice(hf * D, (hf + 1) * D)),
                       pack_ref[_ROW_B_GATES + 2 * l + hf:_ROW_B_GATES + 2 * l + hf + 1, :])
        finish("g_final", (slice(0, 1), slice(None)), pack_ref[_ROW_G_FINAL:_ROW_G_FINAL + 1, :])

        def conv_w_chip(js):
            for l in range(DEPTH):
                for k in range(3):
                    row = _ROW_CONV + 4 * l + k
                    finish("conv_w", (l, slice(k, k + 1), slice(None)), pack_ref[row:row + 1, js * QW:(js + 1) * QW])

        _for_my_chip(j, conv_w_chip)

    vm = pl.BlockSpec(memory_space=pltpu.VMEM)
    ins = (parts["g_mix"] + parts["b_gates"] + parts["sinks"] + parts["conv"] + parts["g_mlp"] + [parts["loss"]]
           + [params[n] for n in names] + [moms[n] for n in names] + [vels[n] for n in names])
    out_shape = [jax.ShapeDtypeStruct((1, 1), F32)]
    for n in names:
        out_shape += [jax.ShapeDtypeStruct(params[n].shape, F32)] * 4
    res = pl.pallas_call(
        body, out_shape=out_shape, in_specs=[vm] * len(ins), out_specs=[vm] * len(out_shape),
        scratch_shapes=[pltpu.VMEM((SMALL_ROWS, D), F32), pltpu.VMEM((n_dev, SMALL_ROWS, D), F32),
                        pltpu.SemaphoreType.DMA((n_dev - 1,)), pltpu.SemaphoreType.DMA((n_dev - 1,))],
        name="small_allreduce_adamw")(*ins)
    loss = res[0]
    out = {n: res[1 + 4 * i:5 + 4 * i] for i, n in enumerate(names)}
    return loss, out


def kernel(x, g_mix, w_in, b_gates, sinks, w_attn_out, conv_w, conv_b, w_conv_out, w_o, g_mlp, w_up, w_down, g_final, loss_target, m_g_mix, m_w_in, m_b_gates, m_sinks, m_w_attn_out, m_conv_w, m_conv_b, m_w_conv_out, m_w_o, m_g_mlp, m_w_up, m_w_down, m_g_final, v_g_mix, v_w_in, v_b_gates, v_sinks, v_w_attn_out, v_conv_w, v_conv_b, v_w_conv_out, v_w_o, v_g_mlp, v_w_up, v_w_down, v_g_final):
    B, S, D = x.shape
    T = B * S
    big_w = dict(w_in=w_in, w_attn_out=w_attn_out, w_conv_out=w_conv_out, w_o=w_o, w_up=w_up, w_down=w_down)
    big_m = dict(w_in=m_w_in, w_attn_out=m_w_attn_out, w_conv_out=m_w_conv_out, w_o=m_w_o, w_up=m_w_up, w_down=m_w_down)
    big_v = dict(w_in=v_w_in, w_attn_out=v_w_attn_out, w_conv_out=v_w_conv_out, w_o=v_w_o, w_up=v_w_up, w_down=v_w_down)

    c_arr = lax.axis_index("c").astype(jnp.int32).reshape(1)
    j_arr = (2 * lax.axis_index("x") + lax.axis_index("y")).astype(jnp.int32).reshape(1)
    jc_arr = jnp.concatenate([j_arr, c_arr])
    order = [(n, l) for n, _, _, _ in BIG for l in range(DEPTH)]
    dims = {n: (r, c_, k) for n, r, c_, k in BIG}

    wdims = dict(dims, w_up=(D, D_FF, "chip"))
    full = {(n, l): _cast_into_full(j_arr, big_w[n], l, *wdims[n]) for n, l in order}
    mixers = ("w_attn_out", "w_conv_out", "w_o")

    def gather_of(keys):
        return _gather_comm([(full[k],) + wdims[k[0]] for k in keys])

    carried = {("proj", 0): [(n, 0) for n in mixers] + [("w_up", 0)], ("attn", 0): [("w_down", 0)],
               ("mlp", 0): [("w_in", 1)] + [(n, 1) for n in mixers] + [("w_up", 1)], ("attn", 1): [("w_down", 1)]}

    def carry(fn, where, *args):
        keys = carried.get(where)
        if keys is None:
            return fn(*args)
        res, got = fn(*args, comm=gather_of(keys))
        full.update(zip(keys, got))
        return res

    first = _run_comm(_gather_comm([(full["w_in", 0],) + dims["w_in"]], conv_w), "gather_weights")
    full["w_in", 0] = first[0]
    conv_w_full = jnp.transpose(first[1], (0, 2, 1, 3)).reshape(DEPTH, 3, D)
    attn_bias = _attn_bias_table()

    xs = [x.reshape(T, D)]
    saved = []
    for l in range(DEPTH):
        ht, pqkv, pconv, pgate = carry(_norm_proj, ("proj", l), xs[-1], g_mix[l:l + 1], full["w_in", l])
        att, att_t, lse = carry(_attn_fwd, ("attn", l), pqkv, sinks[l], attn_bias, S)
        x1, ya, yc, mg_t, cv_t = _mix_fwd(xs[-1], att, pconv, pgate, conv_w_full[l], conv_b[l:l + 1], b_gates[l:l + 1],
                                          full["w_attn_out", l], full["w_conv_out", l], full["w_o", l], S)
        x2, a = carry(_mlp_fwd, ("mlp", l), x1, g_mlp[l:l + 1], full["w_up", l], full["w_down", l].reshape(N_CHIP, D, D))
        saved.append(dict(ht=ht, pqkv=pqkv, pconv=pconv, pgate=pgate, att=att, att_t=att_t, lse=lse, cv_t=cv_t, x1=x1, ya=ya,
                          yc=yc, mg_t=mg_t, a=a))
        xs.append(x2)

    loss_stats, dx = _loss_bwd(xs[-1], g_final.reshape(1, D), loss_target.reshape(T, D))

    parts = dict(g_mix=[None] * DEPTH, b_gates=[None] * DEPTH, sinks=[None] * DEPTH, conv=[None] * DEPTH,
                 g_mlp=[None] * DEPTH, loss=loss_stats)
    gf, gb, pre, got, mine = {}, {}, {}, {}, {}

    def sibling_exchange(keys):
        return _sibling_exchange_comm([(gb[k],) + dims[k[0]] for k in keys])

    def half_adds(keys, sib):
        for k, s in zip(keys, sib):
            pre[k] = _half_add(jc_arr, gf[k], s, *dims[k[0]])

    def chip_exchange(keys):
        return _chip_exchange_comm([pre[k][0] for k in keys])

    def owner_sums(keys):
        for n, l in keys:
            mine[n] = _owner_sum(jc_arr, pre[n, l][1], got[n, l], l, mine.get(n))

    def run(fn, comms, *args):
        if not comms:
            return fn(*args), []
        res, arrived = fn(*args, comm=_compose(*comms))
        outs, pos = [], 0
        for cm in comms:
            outs.append(arrived[pos:pos + len(cm.out_shape)])
            pos += len(cm.out_shape)
        return res, outs

    assert DEPTH == 2
    upper = [(n, 1) for n, _, _, _ in BIG]
    early_mlp, early_mix = [("w_up", 0), ("w_down", 0)], [(n, 0) for n in mixers]
    early = early_mix + early_mlp
    for l in reversed(range(DEPTH)):
        W = {n: full[(n, l)] for n in big_w}
        sv = saved[l]
        last = l == 0
        mlp_args = (dx, sv["x1"], sv["a"], g_mlp[l:l + 1], W["w_up"], W["w_down"].reshape(N_CHIP, D, D))
        (dx1, da, u_t, h2_t, dyb, parts["g_mlp"][l]), arrived = run(_mlp_bwd, [chip_exchange(upper)] if last else [], *mlp_args)
        if last:
            got.update(zip(upper, arrived[0]))
        gf["w_up", l], gb["w_up", l] = _dw(h2_t, da)
        gf["w_down", l], gb["w_down", l] = _dw(u_t, dyb.reshape(-1, _dw_chunk(T), D))
        mix_args = (dx1, sv["ya"], sv["yc"], sv["pgate"], sv["pconv"], conv_w_full[l], conv_b[l:l + 1], b_gates[l:l + 1],
                    W["w_attn_out"], W["w_conv_out"], W["w_o"], S)
        (datt, dya, dyc, dgate, dconv, parts["b_gates"][l], parts["conv"][l]), arrived = run(
            _mix_bwd, [sibling_exchange(early_mlp)] if last else [], *mix_args)
        if last:
            half_adds(early_mlp, arrived[0])
        gf["w_o", l], gb["w_o", l] = _dw(sv["mg_t"], dx1)
        gf["w_attn_out", l], gb["w_attn_out", l] = _dw(sv["att_t"], dya)
        gf["w_conv_out", l], gb["w_conv_out", l] = _dw(sv["cv_t"], dyc)
        if last:
            half_adds(early_mix, _run_comm(sibling_exchange(early_mix), "grad_sibling_exchange_mixers"))
        attn_args = (sv["pqkv"], datt, sv["lse"], sinks[l], attn_bias, S)
        (dq, dkv, parts["sinks"][l]), arrived = run(_attn_bwd, [chip_exchange(early)] if last else [], *attn_args)
        if last:
            got.update(zip(early, arrived[0]))
            owner_sums(upper + early)
        pieces = [(dq, D), (dkv, QKV_W - D), (dconv, CONV_W), (dgate, GATE_W)]
        gf["w_in", l], gb["w_in", l] = _dw_pieces(sv["ht"], pieces)
        in_args = (pieces, W["w_in"], xs[l], dx1, g_mix[l:l + 1])
        if not last:
            (dx, parts["g_mix"][l]), arrived = run(_inproj_bwd, [sibling_exchange(upper)], *in_args)
            half_adds(upper, arrived[0])
        else:
            tail = [("w_in", 0)]
            half_adds(tail, _run_comm(sibling_exchange(tail), "grad_sibling_exchange"))
            done = _sibling_assemble_comm(list(mine.values()), [(1,) if n == "w_in" else (0, 1) for n in mine])
            (dx, parts["g_mix"][l]), arrived = run(_inproj_bwd, [chip_exchange(tail), done], *in_args)
            got.update(zip(tail, arrived[0]))
            mine = dict(zip(mine, arrived[1]))
            owner_sums(tail)
            mine["w_in"] = _run_comm(_sibling_assemble_comm([mine["w_in"]], [(0,)]), "grad_sibling_assemble")[0]
    grads = mine

    res = {}
    for n in big_w:
        res[n] = tuple(_adamw(big_w[n], grads[n], big_m[n], big_v[n]))

    small_p = dict(g_mix=g_mix, b_gates=b_gates, sinks=sinks, conv_w=conv_w, conv_b=conv_b, g_mlp=g_mlp, g_final=g_final.reshape(1, D))
    small_m = dict(g_mix=m_g_mix, b_gates=m_b_gates, sinks=m_sinks, conv_w=m_conv_w, conv_b=m_conv_b, g_mlp=m_g_mlp,
                   g_final=m_g_final.reshape(1, D))
    small_v = dict(g_mix=v_g_mix, b_gates=v_b_gates, sinks=v_sinks, conv_w=v_conv_w, conv_b=v_conv_b, g_mlp=v_g_mlp,
                   g_final=v_g_final.reshape(1, D))
    loss, small = _small_step(parts, small_p, small_m, small_v)
    for n, vals in small.items():
        res[n] = tuple(v.reshape(D) for v in vals) if n == "g_final" else tuple(vals)

    weights = ["g_mix", "w_in", "b_gates", "sinks", "w_attn_out", "conv_w", "conv_b", "w_conv_out", "w_o", "g_mlp", "w_up",
               "w_down", "g_final"]
    out = [loss.reshape(()), dx.reshape(B, S, D)]
    for k in range(4):
        out += [res[n][k] for n in weights]
    return tuple(out)
```
